```python
import jax, jax.numpy as jnp
from jax import lax
import numpy as np

D_MODEL = 1024
BATCH = 8
SEQ = 8192
DEPTH = 1

SSD_EXPAND = 2
SSD_D_INNER = SSD_EXPAND * D_MODEL
SSD_HEAD_DIM = 64
SSD_N_HEADS = SSD_D_INNER // SSD_HEAD_DIM
SSD_N_GROUPS = 4
SSD_HEADS_PER_GROUP = SSD_N_HEADS // SSD_N_GROUPS
SSD_D_STATE = 128
SSD_CONV_WIDTH = 4
SSD_CHUNK = 128
SSD_CONV_DIM = SSD_D_INNER + 2 * SSD_N_GROUPS * SSD_D_STATE

ATTN_HEAD_DIM = 64
ATTN_N_HEADS = D_MODEL // ATTN_HEAD_DIM
ATTN_N_KV_HEADS = 4
ATTN_REP = ATTN_N_HEADS // ATTN_N_KV_HEADS
ATTN_WINDOW = 128
ATTN_WIDTH = ATTN_N_HEADS * ATTN_HEAD_DIM
KV_WIDTH = ATTN_N_KV_HEADS * ATTN_HEAD_DIM
ROPE_THETA = 10000.0

FFN_D_FF = 2816
FFN_CONV_WIDTH = 3

NORM_EPS = 1e-6

IN_SIZES = (SSD_D_INNER, SSD_CONV_DIM, SSD_N_HEADS, ATTN_WIDTH, KV_WIDTH, KV_WIDTH, D_MODEL, D_MODEL)
IN_PROJ_DIM = sum(IN_SIZES)

kernel_name = "hybrid_ssd_swa_sink_gated_convffn"


def _split(t, sizes):
    idx = np.cumsum(np.array(sizes))[:-1].tolist()
    return jnp.split(t, idx, axis=-1)


def rms_norm(x, w):
    xf = x.astype(jnp.float32)
    y = xf * lax.rsqrt(jnp.mean(xf * xf, axis=-1, keepdims=True) + NORM_EPS)
    return (y * w.astype(jnp.float32)).astype(x.dtype)


def gated_group_rms_norm(y, z, w):
    b, s, d = y.shape
    g = (y.astype(jnp.float32) * jax.nn.silu(z.astype(jnp.float32))).reshape(b, s, SSD_N_GROUPS, d // SSD_N_GROUPS)
    g = g * lax.rsqrt(jnp.mean(g * g, axis=-1, keepdims=True) + NORM_EPS)
    return (g.reshape(b, s, d) * w.astype(jnp.float32)).astype(y.dtype)


def causal_dwconv(x, w, bias):
    k = w.shape[0]
    y = lax.conv_general_dilated(x, w[:, None, :].astype(x.dtype), window_strides=(1,), padding=((k - 1, 0),),
                                 dimension_numbers=('NWC', 'WIO', 'NWC'), feature_group_count=x.shape[-1])
    return y + bias.astype(x.dtype)


def rope_tables(positions):
    half = ATTN_HEAD_DIM // 2
    inv_freq = ROPE_THETA ** (-jnp.arange(half, dtype=jnp.float32) * 2.0 / ATTN_HEAD_DIM)
    ang = positions.astype(jnp.float32)[..., None] * inv_freq
    return jnp.cos(ang), jnp.sin(ang)


def apply_rope(t, cos, sin):
    b, s = t.shape[:2]
    half = t.shape[-1] // 2
    shp = (b, s) + (1,) * (t.ndim - 3) + (half,)
    c, sn = cos.reshape(shp), sin.reshape(shp)
    tf = t.astype(jnp.float32)
    t1, t2 = tf[..., :half], tf[..., half:]
    return jnp.concatenate([t1 * c - t2 * sn, t2 * c + t1 * sn], axis=-1).astype(t.dtype)


def ssd_chunked(xh, dt, a, bm, cm):
    b, s, g, j, p = xh.shape
    n = bm.shape[-1]
    c = s // SSD_CHUNK
    xf = xh.astype(jnp.float32)
    X = (xf * dt[..., None]).reshape(b, c, SSD_CHUNK, g, j, p)
    adt = (dt * a).reshape(b, c, SSD_CHUNK, g, j).transpose(0, 3, 4, 1, 2)
    a_cs = jnp.cumsum(adt, axis=-1)
    Bc = bm.astype(jnp.float32).reshape(b, c, SSD_CHUNK, g, n)
    Cc = cm.astype(jnp.float32).reshape(b, c, SSD_CHUNK, g, n)
    causal = jnp.tril(jnp.ones((SSD_CHUNK, SSD_CHUNK), dtype=bool))
    seg = a_cs[..., :, None] - a_cs[..., None, :]
    lmat = jnp.exp(jnp.where(causal, seg, -jnp.inf))
    cb = jnp.einsum('bclgn,bcsgn->bcgls', Cc, Bc)
    y_diag = jnp.einsum('bcgls,bgjcls,bcsgjp->bclgjp', cb, lmat, X)
    decay_states = jnp.exp(a_cs[..., -1:] - a_cs)
    states = jnp.einsum('bclgn,bgjcl,bclgjp->bcgjpn', Bc, decay_states, X)
    chunk_decay = jnp.exp(a_cs[..., -1])

    def step(h, inp):
        st, dc = inp
        return h * dc[..., None, None] + st, h

    h0 = jnp.zeros((b, g, j, p, n), jnp.float32)
    _, prev = lax.scan(step, h0, (jnp.moveaxis(states, 1, 0), jnp.moveaxis(chunk_decay, -1, 0)))
    prev = jnp.moveaxis(prev, 0, 1)
    y_off = jnp.einsum('bclgn,bcgjpn,bgjcl->bclgjp', Cc, prev, jnp.exp(a_cs))
    return (y_diag + y_off).reshape(b, s, g, j, p)


def sliding_window_sink_attention(q, k, v, sinks):
    b, s, g, r, d = q.shape
    w = ATTN_WINDOW
    nb = s // w
    qb = q.reshape(b, nb, w, g, r, d) * (d ** -0.5)
    kb = k.reshape(b, nb, w, g, d)
    vb = v.reshape(b, nb, w, g, d)
    kk = jnp.concatenate([jnp.concatenate([jnp.zeros_like(kb[:, :1]), kb[:, :-1]], axis=1), kb], axis=2)
    vv = jnp.concatenate([jnp.concatenate([jnp.zeros_like(vb[:, :1]), vb[:, :-1]], axis=1), vb], axis=2)
    scores = jnp.einsum('bnqgrd,bnkgd->bgrnqk', qb, kk, preferred_element_type=jnp.float32)
    qpos = jnp.arange(w)[:, None] + w
    kpos = jnp.arange(2 * w)[None, :]
    diff = qpos - kpos
    band = (diff >= 0) & (diff < w)
    valid = band[None] & ((jnp.arange(nb)[:, None, None] > 0) | (kpos[None] >= w))
    scores = jnp.where(valid, scores, -jnp.inf)
    sink = sinks.astype(jnp.float32).reshape(g, r)[None, :, :, None, None, None]
    m = jnp.maximum(jnp.max(scores, axis=-1, keepdims=True), sink)
    pexp = jnp.exp(scores - m)
    probs = pexp / (jnp.sum(pexp, axis=-1, keepdims=True) + jnp.exp(sink - m))
    out = jnp.einsum('bgrnqk,bnkgd->bnqgrd', probs.astype(v.dtype), vv)
    return out.reshape(b, s, g * r * d)


def hybrid_layer(x, cos, sin, norm_mix_pre_w, w_in, ssd_conv_w, ssd_conv_b, ssd_dt_bias, ssd_a_log, ssd_d,
                 ssd_norm_w, ssd_w_out, attn_sinks, attn_w_out, w_mix_out, norm_mix_post_w,
                 norm_ffn_pre_w, ffn_w_up, ffn_conv_w, ffn_conv_b, ffn_w_down, norm_ffn_post_w):
    b, s, _ = x.shape
    u = rms_norm(x, norm_mix_pre_w)
    proj = u @ w_in
    z, xbc, dt_raw, q, k, v, gate_ssd, gate_attn = _split(proj, IN_SIZES)

    xbc = jax.nn.silu(causal_dwconv(xbc, ssd_conv_w, ssd_conv_b))
    xs, bm, cm = _split(xbc, (SSD_D_INNER, SSD_N_GROUPS * SSD_D_STATE, SSD_N_GROUPS * SSD_D_STATE))
    xh = xs.reshape(b, s, SSD_N_GROUPS, SSD_HEADS_PER_GROUP, SSD_HEAD_DIM)
    bm = bm.reshape(b, s, SSD_N_GROUPS, SSD_D_STATE)
    cm = cm.reshape(b, s, SSD_N_GROUPS, SSD_D_STATE)
    dt = jax.nn.softplus(dt_raw.astype(jnp.float32) + ssd_dt_bias.astype(jnp.float32)).reshape(b, s, SSD_N_GROUPS, SSD_HEADS_PER_GROUP)
    a = -jnp.exp(ssd_a_log.astype(jnp.float32)).reshape(SSD_N_GROUPS, SSD_HEADS_PER_GROUP)
    y = ssd_chunked(xh, dt, a, bm, cm)
    y = y + ssd_d.astype(jnp.float32).reshape(SSD_N_GROUPS, SSD_HEADS_PER_GROUP)[..., None] * xh.astype(jnp.float32)
    y = gated_group_rms_norm(y.reshape(b, s, SSD_D_INNER).astype(x.dtype), z, ssd_norm_w)
    y_ssd = y @ ssd_w_out

    q = apply_rope(q.reshape(b, s, ATTN_N_KV_HEADS, ATTN_REP, ATTN_HEAD_DIM), cos, sin)
    k = apply_rope(k.reshape(b, s, ATTN_N_KV_HEADS, ATTN_HEAD_DIM), cos, sin)
    v = v.reshape(b, s, ATTN_N_KV_HEADS, ATTN_HEAD_DIM)
    y_attn = sliding_window_sink_attention(q, k, v, attn_sinks) @ attn_w_out

    merged = jax.nn.sigmoid(gate_ssd) * y_ssd + jax.nn.sigmoid(gate_attn) * y_attn
    x = x + rms_norm(merged @ w_mix_out, norm_mix_post_w)

    h = rms_norm(x, norm_ffn_pre_w)
    up = causal_dwconv(h @ ffn_w_up, ffn_conv_w, ffn_conv_b)
    gate, val = _split(up, (FFN_D_FF, FFN_D_FF))
    ff = (jax.nn.gelu(gate, approximate=True) * val) @ ffn_w_down
    return x + rms_norm(ff, norm_ffn_post_w)


def _fwd_setup_inputs(seed: int = 0) -> dict:
    key = jax.random.key(seed)
    ks = jax.random.split(key, 24)
    L = DEPTH
    f32 = jnp.float32

    def nrm(k, shape, scale):
        return jax.random.normal(k, shape, f32) * scale

    def gain(k, d):
        return 1.0 + 0.02 * jax.random.normal(k, (L, d), f32)

    x = jax.random.normal(ks[0], (BATCH, SEQ, D_MODEL), f32)
    start = jax.random.randint(ks[1], (BATCH,), 0, 4096, dtype=jnp.int32)
    positions = (start[:, None] + jnp.arange(SEQ, dtype=jnp.int32)[None, :]).astype(jnp.int32)
    u = jax.random.uniform(ks[2], (L, SSD_N_HEADS), f32)
    dt0 = jnp.exp(u * (np.log(0.1) - np.log(0.001)) + np.log(0.001)).astype(f32)
    ssd_dt_bias = dt0 + jnp.log(-jnp.expm1(-dt0))
    ssd_a_log = jnp.log(jax.random.uniform(ks[3], (L, SSD_N_HEADS), f32, minval=1.0, maxval=16.0))
    return {
        "x": x,
        "positions": positions,
        "norm_mix_pre_w": gain(ks[4], D_MODEL),
        "w_in": nrm(ks[5], (L, D_MODEL, IN_PROJ_DIM), D_MODEL ** -0.5),
        "ssd_conv_w": nrm(ks[6], (L, SSD_CONV_WIDTH, SSD_CONV_DIM), SSD_CONV_WIDTH ** -0.5),
        "ssd_conv_b": nrm(ks[7], (L, SSD_CONV_DIM), 0.02),
        "ssd_dt_bias": ssd_dt_bias,
        "ssd_a_log": ssd_a_log,
        "ssd_d": 1.0 + 0.1 * jax.random.normal(ks[8], (L, SSD_N_HEADS), f32),
        "ssd_norm_w": gain(ks[9], SSD_D_INNER),
        "ssd_w_out": nrm(ks[10], (L, SSD_D_INNER, D_MODEL), SSD_D_INNER ** -0.5),
        "attn_sinks": nrm(ks[11], (L, ATTN_N_HEADS), 1.0),
        "attn_w_out": nrm(ks[12], (L, ATTN_WIDTH, D_MODEL), ATTN_WIDTH ** -0.5),
        "w_mix_out": nrm(ks[13], (L, D_MODEL, D_MODEL), D_MODEL ** -0.5),
        "norm_mix_post_w": gain(ks[14], D_MODEL),
        "norm_ffn_pre_w": gain(ks[15], D_MODEL),
        "ffn_w_up": nrm(ks[16], (L, D_MODEL, 2 * FFN_D_FF), D_MODEL ** -0.5),
        "ffn_conv_w": nrm(ks[17], (L, FFN_CONV_WIDTH, 2 * FFN_D_FF), FFN_CONV_WIDTH ** -0.5),
        "ffn_conv_b": nrm(ks[18], (L, 2 * FFN_D_FF), 0.02),
        "ffn_w_down": nrm(ks[19], (L, FFN_D_FF, D_MODEL), FFN_D_FF ** -0.5),
        "norm_ffn_post_w": gain(ks[20], D_MODEL),
    }


def _fwd_reference(x, positions, norm_mix_pre_w, w_in, ssd_conv_w, ssd_conv_b, ssd_dt_bias, ssd_a_log, ssd_d,
              ssd_norm_w, ssd_w_out, attn_sinks, attn_w_out, w_mix_out, norm_mix_post_w,
              norm_ffn_pre_w, ffn_w_up, ffn_conv_w, ffn_conv_b, ffn_w_down, norm_ffn_post_w):
    cos, sin = rope_tables(positions)
    for i in range(DEPTH):
        x = hybrid_layer(x, cos, sin, norm_mix_pre_w[i], w_in[i], ssd_conv_w[i], ssd_conv_b[i], ssd_dt_bias[i],
                         ssd_a_log[i], ssd_d[i], ssd_norm_w[i], ssd_w_out[i], attn_sinks[i], attn_w_out[i],
                         w_mix_out[i], norm_mix_post_w[i], norm_ffn_pre_w[i], ffn_w_up[i], ffn_conv_w[i],
                         ffn_conv_b[i], ffn_w_down[i], norm_ffn_post_w[i])
    return x


import jax as _jax
import jax.numpy as _jnp

TWIN_FORMAT = 'train_step'
FWD_PARAMS = ['x', 'positions', 'norm_mix_pre_w', 'w_in', 'ssd_conv_w', 'ssd_conv_b', 'ssd_dt_bias', 'ssd_a_log', 'ssd_d', 'ssd_norm_w', 'ssd_w_out', 'attn_sinks', 'attn_w_out', 'w_mix_out', 'norm_mix_post_w', 'norm_ffn_pre_w', 'ffn_w_up', 'ffn_conv_w', 'ffn_conv_b', 'ffn_w_down', 'norm_ffn_post_w']
TWIN_WEIGHTS = ['norm_mix_pre_w', 'w_in', 'ssd_conv_w', 'ssd_conv_b', 'ssd_dt_bias', 'ssd_a_log', 'ssd_d', 'ssd_norm_w', 'ssd_w_out', 'attn_sinks', 'attn_w_out', 'w_mix_out', 'norm_mix_post_w', 'norm_ffn_pre_w', 'ffn_w_up', 'ffn_conv_w', 'ffn_conv_b', 'ffn_w_down', 'norm_ffn_post_w']
TWIN_DIFF_INPUT = 'x'
TWIN_INPUTS = ['x', 'positions', 'norm_mix_pre_w', 'w_in', 'ssd_conv_w', 'ssd_conv_b', 'ssd_dt_bias', 'ssd_a_log', 'ssd_d', 'ssd_norm_w', 'ssd_w_out', 'attn_sinks', 'attn_w_out', 'w_mix_out', 'norm_mix_post_w', 'norm_ffn_pre_w', 'ffn_w_up', 'ffn_conv_w', 'ffn_conv_b', 'ffn_w_down', 'norm_ffn_post_w', 'loss_target', 'm_norm_mix_pre_w', 'm_w_in', 'm_ssd_conv_w', 'm_ssd_conv_b', 'm_ssd_dt_bias', 'm_ssd_a_log', 'm_ssd_d', 'm_ssd_norm_w', 'm_ssd_w_out', 'm_attn_sinks', 'm_attn_w_out', 'm_w_mix_out', 'm_norm_mix_post_w', 'm_norm_ffn_pre_w', 'm_ffn_w_up', 'm_ffn_conv_w', 'm_ffn_conv_b', 'm_ffn_w_down', 'm_norm_ffn_post_w', 'v_norm_mix_pre_w', 'v_w_in', 'v_ssd_conv_w', 'v_ssd_conv_b', 'v_ssd_dt_bias', 'v_ssd_a_log', 'v_ssd_d', 'v_ssd_norm_w', 'v_ssd_w_out', 'v_attn_sinks', 'v_attn_w_out', 'v_w_mix_out', 'v_norm_mix_post_w', 'v_norm_ffn_pre_w', 'v_ffn_w_up', 'v_ffn_conv_w', 'v_ffn_conv_b', 'v_ffn_w_down', 'v_norm_ffn_post_w']
TWIN_OUTPUTS = ['loss', 'grad_x', 'grad_norm_mix_pre_w', 'grad_w_in', 'grad_ssd_conv_w', 'grad_ssd_conv_b', 'grad_ssd_dt_bias', 'grad_ssd_a_log', 'grad_ssd_d', 'grad_ssd_norm_w', 'grad_ssd_w_out', 'grad_attn_sinks', 'grad_attn_w_out', 'grad_w_mix_out', 'grad_norm_mix_post_w', 'grad_norm_ffn_pre_w', 'grad_ffn_w_up', 'grad_ffn_conv_w', 'grad_ffn_conv_b', 'grad_ffn_w_down', 'grad_norm_ffn_post_w', 'delta_norm_mix_pre_w', 'delta_w_in', 'delta_ssd_conv_w', 'delta_ssd_conv_b', 'delta_ssd_dt_bias', 'delta_ssd_a_log', 'delta_ssd_d', 'delta_ssd_norm_w', 'delta_ssd_w_out', 'delta_attn_sinks', 'delta_attn_w_out', 'delta_w_mix_out', 'delta_norm_mix_post_w', 'delta_norm_ffn_pre_w', 'delta_ffn_w_up', 'delta_ffn_conv_w', 'delta_ffn_conv_b', 'delta_ffn_w_down', 'delta_norm_ffn_post_w', 'new_m_norm_mix_pre_w', 'new_m_w_in', 'new_m_ssd_conv_w', 'new_m_ssd_conv_b', 'new_m_ssd_dt_bias', 'new_m_ssd_a_log', 'new_m_ssd_d', 'new_m_ssd_norm_w', 'new_m_ssd_w_out', 'new_m_attn_sinks', 'new_m_attn_w_out', 'new_m_w_mix_out', 'new_m_norm_mix_post_w', 'new_m_norm_ffn_pre_w', 'new_m_ffn_w_up', 'new_m_ffn_conv_w', 'new_m_ffn_conv_b', 'new_m_ffn_w_down', 'new_m_norm_ffn_post_w', 'new_v_norm_mix_pre_w', 'new_v_w_in', 'new_v_ssd_conv_w', 'new_v_ssd_conv_b', 'new_v_ssd_dt_bias', 'new_v_ssd_a_log', 'new_v_ssd_d', 'new_v_ssd_norm_w', 'new_v_ssd_w_out', 'new_v_attn_sinks', 'new_v_attn_w_out', 'new_v_w_mix_out', 'new_v_norm_mix_post_w', 'new_v_norm_ffn_pre_w', 'new_v_ffn_w_up', 'new_v_ffn_conv_w', 'new_v_ffn_conv_b', 'new_v_ffn_w_down', 'new_v_norm_ffn_post_w']
TWIN_LEAF_KINDS = {'loss': 'loss', 'grad_x': 'grad_x', 'grad_norm_mix_pre_w': 'grad_w', 'grad_w_in': 'grad_w', 'grad_ssd_conv_w': 'grad_w', 'grad_ssd_conv_b': 'grad_w', 'grad_ssd_dt_bias': 'grad_w', 'grad_ssd_a_log': 'grad_w', 'grad_ssd_d': 'grad_w', 'grad_ssd_norm_w': 'grad_w', 'grad_ssd_w_out': 'grad_w', 'grad_attn_sinks': 'grad_w', 'grad_attn_w_out': 'grad_w', 'grad_w_mix_out': 'grad_w', 'grad_norm_mix_post_w': 'grad_w', 'grad_norm_ffn_pre_w': 'grad_w', 'grad_ffn_w_up': 'grad_w', 'grad_ffn_conv_w': 'grad_w', 'grad_ffn_conv_b': 'grad_w', 'grad_ffn_w_down': 'grad_w', 'grad_norm_ffn_post_w': 'grad_w', 'delta_norm_mix_pre_w': 'delta_w', 'delta_w_in': 'delta_w', 'delta_ssd_conv_w': 'delta_w', 'delta_ssd_conv_b': 'delta_w', 'delta_ssd_dt_bias': 'delta_w', 'delta_ssd_a_log': 'delta_w', 'delta_ssd_d': 'delta_w', 'delta_ssd_norm_w': 'delta_w', 'delta_ssd_w_out': 'delta_w', 'delta_attn_sinks': 'delta_w', 'delta_attn_w_out': 'delta_w', 'delta_w_mix_out': 'delta_w', 'delta_norm_mix_post_w': 'delta_w', 'delta_norm_ffn_pre_w': 'delta_w', 'delta_ffn_w_up': 'delta_w', 'delta_ffn_conv_w': 'delta_w', 'delta_ffn_conv_b': 'delta_w', 'delta_ffn_w_down': 'delta_w', 'delta_norm_ffn_post_w': 'delta_w', 'new_m_norm_mix_pre_w': 'new_m', 'new_m_w_in': 'new_m', 'new_m_ssd_conv_w': 'new_m', 'new_m_ssd_conv_b': 'new_m', 'new_m_ssd_dt_bias': 'new_m', 'new_m_ssd_a_log': 'new_m', 'new_m_ssd_d': 'new_m', 'new_m_ssd_norm_w': 'new_m', 'new_m_ssd_w_out': 'new_m', 'new_m_attn_sinks': 'new_m', 'new_m_attn_w_out': 'new_m', 'new_m_w_mix_out': 'new_m', 'new_m_norm_mix_post_w': 'new_m', 'new_m_norm_ffn_pre_w': 'new_m', 'new_m_ffn_w_up': 'new_m', 'new_m_ffn_conv_w': 'new_m', 'new_m_ffn_conv_b': 'new_m', 'new_m_ffn_w_down': 'new_m', 'new_m_norm_ffn_post_w': 'new_m', 'new_v_norm_mix_pre_w': 'new_v', 'new_v_w_in': 'new_v', 'new_v_ssd_conv_w': 'new_v', 'new_v_ssd_conv_b': 'new_v', 'new_v_ssd_dt_bias': 'new_v', 'new_v_ssd_a_log': 'new_v', 'new_v_ssd_d': 'new_v', 'new_v_ssd_norm_w': 'new_v', 'new_v_ssd_w_out': 'new_v', 'new_v_attn_sinks': 'new_v', 'new_v_attn_w_out': 'new_v', 'new_v_w_mix_out': 'new_v', 'new_v_norm_mix_post_w': 'new_v', 'new_v_norm_ffn_pre_w': 'new_v', 'new_v_ffn_w_up': 'new_v', 'new_v_ffn_conv_w': 'new_v', 'new_v_ffn_conv_b': 'new_v', 'new_v_ffn_w_down': 'new_v', 'new_v_norm_ffn_post_w': 'new_v'}


def _forward(args):
    return _fwd_reference(*[args[k] for k in FWD_PARAMS])


def _output_shape():
    def fwd():
        inp = _fwd_setup_inputs(0)
        return _fwd_reference(*[inp[k] for k in FWD_PARAMS])
    out = _jax.eval_shape(fwd)
    return out.shape, out.dtype

N_MICROBATCH = 1
ADAM_LR = 0.001
ADAM_B1 = 0.9
ADAM_B2 = 0.999
ADAM_EPS = 1e-08
ADAM_WD = 0.01
ADAM_STEP = 10
PER_EXAMPLE_BATCH_AXIS = {'x': 0, 'positions': 0, 'loss_target': 0}
SHARED_INPUTS = []
_WEIGHT_DTYPES = {'norm_mix_pre_w': _jnp.float32, 'w_in': _jnp.float32, 'ssd_conv_w': _jnp.float32, 'ssd_conv_b': _jnp.float32, 'ssd_dt_bias': _jnp.float32, 'ssd_a_log': _jnp.float32, 'ssd_d': _jnp.float32, 'ssd_norm_w': _jnp.float32, 'ssd_w_out': _jnp.float32, 'attn_sinks': _jnp.float32, 'attn_w_out': _jnp.float32, 'w_mix_out': _jnp.float32, 'norm_mix_post_w': _jnp.float32, 'norm_ffn_pre_w': _jnp.float32, 'ffn_w_up': _jnp.float32, 'ffn_conv_w': _jnp.float32, 'ffn_conv_b': _jnp.float32, 'ffn_w_down': _jnp.float32, 'norm_ffn_post_w': _jnp.float32}
MOMENT_SCALE = {'norm_mix_pre_w': 1.006238e+00, 'w_in': 3.661485e-01, 'ssd_conv_w': 5.196729e-01, 'ssd_conv_b': 1.479051e+00, 'ssd_dt_bias': 1.350082e+00, 'ssd_a_log': 2.447753e+00, 'ssd_d': 3.409083e+00, 'ssd_norm_w': 9.101638e-01, 'ssd_w_out': 1.163144e+00, 'attn_sinks': 9.735176e-02, 'attn_w_out': 1.266927e-01, 'w_mix_out': 1.198830e+00, 'norm_mix_post_w': 6.383275e+01, 'norm_ffn_pre_w': 1.001237e+00, 'ffn_w_up': 4.157937e-01, 'ffn_conv_w': 4.833992e-01, 'ffn_conv_b': 2.451683e+00, 'ffn_w_down': 9.005534e-01, 'norm_ffn_post_w': 6.387665e+01}


def _to_microbatches(a, axis):
    t = _jnp.moveaxis(a, axis, 0)
    t = t.reshape((N_MICROBATCH, t.shape[0] // N_MICROBATCH) + t.shape[1:])
    return _jnp.moveaxis(t, 1, axis + 1)


def setup_inputs(seed: int = 0) -> dict:
    inp = _fwd_setup_inputs(seed)
    key = _jax.random.fold_in(_jax.random.key(seed), 7919)
    shape, _ = _output_shape()
    out = dict(inp)
    out["loss_target"] = _jax.random.normal(_jax.random.fold_in(key, 0), shape, _jnp.float32)
    for i, name in enumerate(TWIN_WEIGHTS):
        w = inp[name].astype(_jnp.float32)
        if MOMENT_SCALE is None:
            s = _jnp.sqrt(_jnp.mean(_jnp.square(w)) + 1e-30)
        else:
            s = MOMENT_SCALE[name]
        km, kv = _jax.random.split(_jax.random.fold_in(key, i + 1))
        out[name] = w
        out["m_" + name] = s * _jax.random.normal(km, w.shape, _jnp.float32)
        out["v_" + name] = (s * s) * _jax.random.uniform(kv, w.shape, _jnp.float32, 0.5, 1.5)
    if N_MICROBATCH > 1:
        for name, axis in PER_EXAMPLE_BATCH_AXIS.items():
            out[name] = _to_microbatches(out[name], axis)
    return {'x': out['x'], 'positions': out['positions'], 'norm_mix_pre_w': out['norm_mix_pre_w'], 'w_in': out['w_in'], 'ssd_conv_w': out['ssd_conv_w'], 'ssd_conv_b': out['ssd_conv_b'], 'ssd_dt_bias': out['ssd_dt_bias'], 'ssd_a_log': out['ssd_a_log'], 'ssd_d': out['ssd_d'], 'ssd_norm_w': out['ssd_norm_w'], 'ssd_w_out': out['ssd_w_out'], 'attn_sinks': out['attn_sinks'], 'attn_w_out': out['attn_w_out'], 'w_mix_out': out['w_mix_out'], 'norm_mix_post_w': out['norm_mix_post_w'], 'norm_ffn_pre_w': out['norm_ffn_pre_w'], 'ffn_w_up': out['ffn_w_up'], 'ffn_conv_w': out['ffn_conv_w'], 'ffn_conv_b': out['ffn_conv_b'], 'ffn_w_down': out['ffn_w_down'], 'norm_ffn_post_w': out['norm_ffn_post_w'], 'loss_target': out['loss_target'], 'm_norm_mix_pre_w': out['m_norm_mix_pre_w'], 'm_w_in': out['m_w_in'], 'm_ssd_conv_w': out['m_ssd_conv_w'], 'm_ssd_conv_b': out['m_ssd_conv_b'], 'm_ssd_dt_bias': out['m_ssd_dt_bias'], 'm_ssd_a_log': out['m_ssd_a_log'], 'm_ssd_d': out['m_ssd_d'], 'm_ssd_norm_w': out['m_ssd_norm_w'], 'm_ssd_w_out': out['m_ssd_w_out'], 'm_attn_sinks': out['m_attn_sinks'], 'm_attn_w_out': out['m_attn_w_out'], 'm_w_mix_out': out['m_w_mix_out'], 'm_norm_mix_post_w': out['m_norm_mix_post_w'], 'm_norm_ffn_pre_w': out['m_norm_ffn_pre_w'], 'm_ffn_w_up': out['m_ffn_w_up'], 'm_ffn_conv_w': out['m_ffn_conv_w'], 'm_ffn_conv_b': out['m_ffn_conv_b'], 'm_ffn_w_down': out['m_ffn_w_down'], 'm_norm_ffn_post_w': out['m_norm_ffn_post_w'], 'v_norm_mix_pre_w': out['v_norm_mix_pre_w'], 'v_w_in': out['v_w_in'], 'v_ssd_conv_w': out['v_ssd_conv_w'], 'v_ssd_conv_b': out['v_ssd_conv_b'], 'v_ssd_dt_bias': out['v_ssd_dt_bias'], 'v_ssd_a_log': out['v_ssd_a_log'], 'v_ssd_d': out['v_ssd_d'], 'v_ssd_norm_w': out['v_ssd_norm_w'], 'v_ssd_w_out': out['v_ssd_w_out'], 'v_attn_sinks': out['v_attn_sinks'], 'v_attn_w_out': out['v_attn_w_out'], 'v_w_mix_out': out['v_w_mix_out'], 'v_norm_mix_post_w': out['v_norm_mix_post_w'], 'v_norm_ffn_pre_w': out['v_norm_ffn_pre_w'], 'v_ffn_w_up': out['v_ffn_w_up'], 'v_ffn_conv_w': out['v_ffn_conv_w'], 'v_ffn_conv_b': out['v_ffn_conv_b'], 'v_ffn_w_down': out['v_ffn_w_down'], 'v_norm_ffn_post_w': out['v_norm_ffn_post_w']}


def _loss(weights, diff, rest, loss_target):
    with _jax.named_scope("forward"):
        args = {**rest, TWIN_DIFF_INPUT: diff, **{k: w.astype(_WEIGHT_DTYPES[k]) for k, w in weights.items()}}
        y = _forward(args)
    with _jax.named_scope("loss_head"):
        err = _jnp.square(y.astype(_jnp.float32) - loss_target)
        return 0.5 * _jnp.sum(_jnp.mean(err, axis=-1)) if err.ndim else 0.5 * err


def _adamw(w, g, m, v):
    m = ADAM_B1 * m + (1.0 - ADAM_B1) * g
    v = ADAM_B2 * v + (1.0 - ADAM_B2) * _jnp.square(g)
    m_hat = m / (1.0 - ADAM_B1 ** ADAM_STEP)
    v_hat = v / (1.0 - ADAM_B2 ** ADAM_STEP)
    delta = -ADAM_LR * (m_hat / (_jnp.sqrt(v_hat) + ADAM_EPS) + ADAM_WD * w)
    return delta, m, v


def reference(x, positions, norm_mix_pre_w, w_in, ssd_conv_w, ssd_conv_b, ssd_dt_bias, ssd_a_log, ssd_d, ssd_norm_w, ssd_w_out, attn_sinks, attn_w_out, w_mix_out, norm_mix_post_w, norm_ffn_pre_w, ffn_w_up, ffn_conv_w, ffn_conv_b, ffn_w_down, norm_ffn_post_w, loss_target, m_norm_mix_pre_w, m_w_in, m_ssd_conv_w, m_ssd_conv_b, m_ssd_dt_bias, m_ssd_a_log, m_ssd_d, m_ssd_norm_w, m_ssd_w_out, m_attn_sinks, m_attn_w_out, m_w_mix_out, m_norm_mix_post_w, m_norm_ffn_pre_w, m_ffn_w_up, m_ffn_conv_w, m_ffn_conv_b, m_ffn_w_down, m_norm_ffn_post_w, v_norm_mix_pre_w, v_w_in, v_ssd_conv_w, v_ssd_conv_b, v_ssd_dt_bias, v_ssd_a_log, v_ssd_d, v_ssd_norm_w, v_ssd_w_out, v_attn_sinks, v_attn_w_out, v_w_mix_out, v_norm_mix_post_w, v_norm_ffn_pre_w, v_ffn_w_up, v_ffn_conv_w, v_ffn_conv_b, v_ffn_w_down, v_norm_ffn_post_w):
    given = dict(x=x, positions=positions, norm_mix_pre_w=norm_mix_pre_w, w_in=w_in, ssd_conv_w=ssd_conv_w, ssd_conv_b=ssd_conv_b, ssd_dt_bias=ssd_dt_bias, ssd_a_log=ssd_a_log, ssd_d=ssd_d, ssd_norm_w=ssd_norm_w, ssd_w_out=ssd_w_out, attn_sinks=attn_sinks, attn_w_out=attn_w_out, w_mix_out=w_mix_out, norm_mix_post_w=norm_mix_post_w, norm_ffn_pre_w=norm_ffn_pre_w, ffn_w_up=ffn_w_up, ffn_conv_w=ffn_conv_w, ffn_conv_b=ffn_conv_b, ffn_w_down=ffn_w_down, norm_ffn_post_w=norm_ffn_post_w, loss_target=loss_target, m_norm_mix_pre_w=m_norm_mix_pre_w, m_w_in=m_w_in, m_ssd_conv_w=m_ssd_conv_w, m_ssd_conv_b=m_ssd_conv_b, m_ssd_dt_bias=m_ssd_dt_bias, m_ssd_a_log=m_ssd_a_log, m_ssd_d=m_ssd_d, m_ssd_norm_w=m_ssd_norm_w, m_ssd_w_out=m_ssd_w_out, m_attn_sinks=m_attn_sinks, m_attn_w_out=m_attn_w_out, m_w_mix_out=m_w_mix_out, m_norm_mix_post_w=m_norm_mix_post_w, m_norm_ffn_pre_w=m_norm_ffn_pre_w, m_ffn_w_up=m_ffn_w_up, m_ffn_conv_w=m_ffn_conv_w, m_ffn_conv_b=m_ffn_conv_b, m_ffn_w_down=m_ffn_w_down, m_norm_ffn_post_w=m_norm_ffn_post_w, v_norm_mix_pre_w=v_norm_mix_pre_w, v_w_in=v_w_in, v_ssd_conv_w=v_ssd_conv_w, v_ssd_conv_b=v_ssd_conv_b, v_ssd_dt_bias=v_ssd_dt_bias, v_ssd_a_log=v_ssd_a_log, v_ssd_d=v_ssd_d, v_ssd_norm_w=v_ssd_norm_w, v_ssd_w_out=v_ssd_w_out, v_attn_sinks=v_attn_sinks, v_attn_w_out=v_attn_w_out, v_w_mix_out=v_w_mix_out, v_norm_mix_post_w=v_norm_mix_post_w, v_norm_ffn_pre_w=v_norm_ffn_pre_w, v_ffn_w_up=v_ffn_w_up, v_ffn_conv_w=v_ffn_conv_w, v_ffn_conv_b=v_ffn_conv_b, v_ffn_w_down=v_ffn_w_down, v_norm_ffn_post_w=v_norm_ffn_post_w)
    weights = {n: given[n] for n in TWIN_WEIGHTS}
    shared = {n: given[n] for n in SHARED_INPUTS}
    per_example = {n: given[n] for n in ['x', 'positions']}
    grad_fn = _jax.value_and_grad(_loss, argnums=(0, 1))

    def one_microbatch(ex, loss_target):
        ex = dict(ex)
        diff = ex.pop(TWIN_DIFF_INPUT)
        return grad_fn(weights, diff, {**shared, **ex}, loss_target)

    if N_MICROBATCH == 1:
        loss, (grad_w, grad_x) = one_microbatch(per_example, given["loss_target"])
    else:
        def body(carry, xs):
            loss_sum, grad_sum = carry
            l_k, (gw_k, gx_k) = one_microbatch(xs[0], xs[1])
            with _jax.named_scope("update"):
                return (loss_sum + l_k, _jax.tree.map(_jnp.add, grad_sum, gw_k)), gx_k

        init = (_jnp.zeros((), _jnp.float32), _jax.tree.map(_jnp.zeros_like, weights))
        (loss, grad_w), grad_x = _jax.lax.scan(body, init, (per_example, given["loss_target"]))
    with _jax.named_scope("update"):
        delta_w, new_m, new_v = {}, {}, {}
        for n in TWIN_WEIGHTS:
            delta_w[n], new_m[n], new_v[n] = _adamw(weights[n], grad_w[n], given["m_" + n], given["v_" + n])
    return (loss, grad_x, *[grad_w[n] for n in TWIN_WEIGHTS], *[delta_w[n] for n in TWIN_WEIGHTS],
            *[new_m[n] for n in TWIN_WEIGHTS], *[new_v[n] for n in TWIN_WEIGHTS])
```

```python
import functools

import jax
import jax.numpy as jnp
from jax import lax
from jax.experimental import pallas as pl
from jax.experimental.pallas import tpu as pltpu

F32 = jnp.float32
BF16 = jnp.bfloat16
SDS = jax.ShapeDtypeStruct
HIGHEST = lax.Precision.HIGHEST

D_MODEL = 1024
SSD_D_INNER = 2048
SSD_N_HEADS = 32
SSD_HEAD_DIM = 64
SSD_N_GROUPS = 4
SSD_HEADS_PER_GROUP = 8
SSD_D_STATE = 128
SSD_CONV_DIM = 3072
CHUNK = 128
ATTN_N_HEADS = 16
KV_WIDTH = 256
FFN_D_FF = 2816
IN_PROJ_DIM = 8736
ROPE_THETA = 10000.0
NORM_EPS = 1e-6
ADAM_LR, ADAM_B1, ADAM_B2, ADAM_EPS, ADAM_WD, ADAM_STEP = 0.001, 0.9, 0.999, 1e-08, 0.01, 10

PROJ_W = 9216
OFF_Z, OFF_XBC, OFF_Q, OFF_K, OFF_V, OFF_GS, OFF_GA, OFF_DT = 0, 2048, 5120, 6144, 6400, 6656, 7680, 8704
VMEM_LIMIT_MB = 48
NEG = -1e30

WEIGHTS = ('norm_mix_pre_w', 'w_in', 'ssd_conv_w', 'ssd_conv_b', 'ssd_dt_bias', 'ssd_a_log', 'ssd_d', 'ssd_norm_w',
           'ssd_w_out', 'attn_sinks', 'attn_w_out', 'w_mix_out', 'norm_mix_post_w', 'norm_ffn_pre_w', 'ffn_w_up',
           'ffn_conv_w', 'ffn_conv_b', 'ffn_w_down', 'norm_ffn_post_w')
SHARDED = (('w_in', 1, (1024, 2184)), ('ssd_conv_w', 1, (4, 768)), ('ssd_w_out', 0, (512, 1024)),
           ('attn_w_out', 0, (256, 1024)), ('w_mix_out', 0, (256, 1024)), ('ffn_w_up', 1, (1024, 1408)),
           ('ffn_conv_w', 1, (3, 1408)), ('ffn_w_down', 0, (704, 1024)))
MATMUL_WEIGHTS = ('w_in', 'ssd_w_out', 'attn_w_out', 'w_mix_out', 'ffn_w_up', 'ffn_w_down')
REPLICATED = tuple(n for n in WEIGHTS if n not in {s[0] for s in SHARDED})
N_CHIPS = 4
COMM_LANES = 1024


def _cp(vmem_mb=VMEM_LIMIT_MB, **kw):
    return pltpu.CompilerParams(vmem_limit_bytes=vmem_mb << 20, **kw)


def _iota(shape, axis):
    return lax.broadcasted_iota(jnp.int32, shape, axis)


def _sigmoid(v):
    return 1.0 / (1.0 + jnp.exp(-v))


def _mm_call(name, a, b, *, tm, tn, tk, epilogue, outs, extra_in=(), trans_a=False):
    if trans_a:
        kdim, m = a.shape
    else:
        m, kdim = a.shape
    n = b.shape[1]
    assert b.shape[0] == kdim and m % tm == 0 and n % tn == 0 and kdim % tk == 0, (name, a.shape, b.shape, tm, tn, tk)
    gi, gj, gk = m // tm, n // tn, kdim // tk
    n_in, n_out = len(extra_in), len(outs)

    def body(a_ref, b_ref, *rest):
        ins, out_refs, scratch = rest[:n_in], rest[n_in:n_in + n_out], rest[n_in + n_out:]
        i, j, k = pl.program_id(0), pl.program_id(1), pl.program_id(2)
        av = a_ref[...].astype(BF16)
        bv = b_ref[...].astype(BF16)
        if trans_a:
            part = lax.dot_general(av, bv, (((0,), (0,)), ((), ())), preferred_element_type=F32)
        else:
            part = jnp.dot(av, bv, preferred_element_type=F32)
        if gk == 1:
            epilogue(part, i, j, ins, out_refs)
        else:
            acc = scratch[0]

            @pl.when(k == 0)
            def _():
                acc[...] = part

            @pl.when(k > 0)
            def _():
                acc[...] += part

            @pl.when(k == gk - 1)
            def _():
                epilogue(acc[...], i, j, ins, out_refs)

    a_spec = pl.BlockSpec((tk, tm), lambda i, j, k: (k, i)) if trans_a else pl.BlockSpec((tm, tk), lambda i, j, k: (i, k))
    in_specs = [a_spec, pl.BlockSpec((tk, tn), lambda i, j, k: (k, j))]
    in_specs += [pl.BlockSpec(bs, im) for _, bs, im in extra_in]
    res = pl.pallas_call(
        body, name=name, grid=(gi, gj, gk), in_specs=in_specs,
        out_specs=[pl.BlockSpec(bs, im) for _, _, bs, im in outs],
        out_shape=[SDS(s, d) for s, d, _, _ in outs],
        scratch_shapes=[pltpu.VMEM((tm, tn), F32)] if gk > 1 else [],
        compiler_params=_cp(dimension_semantics=("arbitrary", "arbitrary", "arbitrary")),
    )(a, b, *[e[0] for e in extra_in])
    return res


def _mm_plain(name, a, b, *, tm, tn, tk, out_dtype=F32, trans_a=False):
    m = a.shape[1] if trans_a else a.shape[0]

    def epilogue(acc, i, j, ins, outs):
        outs[0][...] = acc.astype(out_dtype)

    return _mm_call(name, a, b, tm=tm, tn=tn, tk=tk, epilogue=epilogue, trans_a=trans_a,
                    outs=[((m, b.shape[1]), out_dtype, (tm, tn), lambda i, j, k: (i, j))])[0]


def _accumulate(ref, first, value):
    @pl.when(first)
    def _():
        ref[...] = value

    @pl.when(jnp.logical_not(first))
    def _():
        ref[...] += value


def _rms_bwd(xv, w, dy):
    r = lax.rsqrt(jnp.mean(xv * xv, axis=-1, keepdims=True) + NORM_EPS)
    xn = xv * r
    dxh = dy * w
    dx = r * (dxh - xn * jnp.mean(dxh * xn, axis=-1, keepdims=True))
    return dx, jnp.sum(dy * xn, axis=0, keepdims=True)


def _norm_mm(name, x, wn, w, *, tm, tn):
    t, dm = x.shape
    n = w.shape[1]
    tm = min(tm, t)

    def body(x_ref, wn_ref, w_ref, o_ref, u_ref):
        @pl.when(pl.program_id(1) == 0)
        def _():
            xv = x_ref[...]
            r = lax.rsqrt(jnp.mean(xv * xv, axis=-1, keepdims=True) + NORM_EPS)
            u_ref[...] = (xv * r * wn_ref[...]).astype(BF16)

        o_ref[...] = jnp.dot(u_ref[...], w_ref[...], preferred_element_type=F32)

    return pl.pallas_call(
        body, name=name, grid=(t // tm, n // tn),
        in_specs=[pl.BlockSpec((tm, dm), lambda i, j: (i, 0)), pl.BlockSpec((1, dm), lambda i, j: (0, 0)),
                  pl.BlockSpec((dm, tn), lambda i, j: (0, j))],
        out_specs=[pl.BlockSpec((tm, tn), lambda i, j: (i, j)), pl.BlockSpec((tm, dm), lambda i, j: (i, 0))],
        out_shape=[SDS((t, n), F32), SDS((t, dm), BF16)],
        compiler_params=_cp(dimension_semantics=("arbitrary", "arbitrary")),
    )(x, wn, w)


def _shift_down(tile, halo, s):
    if s == 0:
        return tile
    r = pltpu.roll(tile, s, axis=0)
    h = pltpu.roll(halo, s, axis=0)
    head = jnp.where(_iota(h.shape, 0) < s, h, r[0:8])
    return jnp.concatenate([head, r[8:]], axis=0)


def _shift_up(tile, halo, s):
    if s == 0:
        return tile
    n = tile.shape[0]
    r = pltpu.roll(tile, n - s, axis=0)
    h = pltpu.roll(halo, 8 - s, axis=0)
    tail = jnp.where(_iota(h.shape, 0) >= 8 - s, h, r[n - 8:])
    return jnp.concatenate([r[:n - 8], tail], axis=0)


def _conv_apply(tile, halo, wv, bv, kw):
    acc = bv + wv[kw - 1:kw, :] * tile
    for k in range(kw - 1):
        acc = acc + wv[k:k + 1, :] * _shift_down(tile, halo, kw - 1 - k)
    return acc


def _prev_halo_spec(tm, tc, col0):
    return pl.BlockSpec((8, tc), lambda i, j: (jnp.maximum(i * (tm // 8) - 1, 0), col0 + j))


def _silu_parts(pre):
    sg = _sigmoid(pre)
    return pre * sg, sg * (1.0 + pre * (1.0 - sg))


def _conv_silu_fwd(proj, w, b, *, tm, tc=512):
    t = proj.shape[0]
    c = w.shape[1]
    tm = min(tm, t)
    col0 = OFF_XBC // tc

    def body(x_ref, h_ref, w_ref, b_ref, o_ref):
        halo = jnp.where(pl.program_id(0) > 0, h_ref[...], 0.0)
        o_ref[...] = _silu_parts(_conv_apply(x_ref[...], halo, w_ref[...], b_ref[...], 4))[0]

    return pl.pallas_call(
        body, name="ssd_conv_fwd", grid=(t // tm, c // tc),
        in_specs=[pl.BlockSpec((tm, tc), lambda i, j: (i, col0 + j)), _prev_halo_spec(tm, tc, col0),
                  pl.BlockSpec((4, tc), lambda i, j: (0, j)), pl.BlockSpec((1, tc), lambda i, j: (0, j))],
        out_specs=pl.BlockSpec((tm, tc), lambda i, j: (i, j)),
        out_shape=SDS((t, c), F32),
        compiler_params=_cp(dimension_semantics=("arbitrary", "arbitrary")),
    )(proj, proj, w, b)


def _conv_silu_bwd1(d_out, proj, w, b, *, tm, tc=512):
    t = proj.shape[0]
    c = w.shape[1]
    tm = min(tm, t)
    col0 = OFF_XBC // tc

    def body(g_ref, x_ref, h_ref, w_ref, b_ref, o_ref, db_ref):
        i = pl.program_id(1)
        halo = jnp.where(i > 0, h_ref[...], 0.0)
        d_pre = g_ref[...] * _silu_parts(_conv_apply(x_ref[...], halo, w_ref[...], b_ref[...], 4))[1]
        o_ref[...] = d_pre
        _accumulate(db_ref, i == 0, jnp.sum(d_pre, axis=0, keepdims=True))

    return pl.pallas_call(
        body, name="ssd_conv_bwd1", grid=(c // tc, t // tm),
        in_specs=[pl.BlockSpec((tm, tc), lambda j, i: (i, j)), pl.BlockSpec((tm, tc), lambda j, i: (i, col0 + j)),
                  pl.BlockSpec((8, tc), lambda j, i: (jnp.maximum(i * (tm // 8) - 1, 0), col0 + j)),
                  pl.BlockSpec((4, tc), lambda j, i: (0, j)), pl.BlockSpec((1, tc), lambda j, i: (0, j))],
        out_specs=[pl.BlockSpec((tm, tc), lambda j, i: (i, j)), pl.BlockSpec((1, tc), lambda j, i: (0, j))],
        out_shape=[SDS((t, c), F32), SDS((1, c), F32)],
        compiler_params=_cp(dimension_semantics=("arbitrary", "arbitrary")),
    )(d_out, proj, proj, w, b)


def _conv_bwd2(name, d_pre, src, src_col0, w, *, tm, tc):
    t, c = d_pre.shape
    kw = w.shape[0]
    tm = min(tm, t)
    ni = t // tm
    col0 = src_col0 // tc

    def body(g_ref, gn_ref, x_ref, xh_ref, w_ref, o_ref, dw_ref):
        i = pl.program_id(1)
        g = g_ref[...]
        g_next = jnp.where(i < ni - 1, gn_ref[...], 0.0)
        x_prev = jnp.where(i > 0, xh_ref[...], 0.0)
        xv = x_ref[...]
        wv = w_ref[...]
        d_in = wv[kw - 1:kw, :] * g
        for k in range(kw - 1):
            d_in = d_in + wv[k:k + 1, :] * _shift_up(g, g_next, kw - 1 - k)
        o_ref[...] = d_in.astype(o_ref.dtype)
        rows = [jnp.sum(g * _shift_down(xv, x_prev, kw - 1 - k), axis=0, keepdims=True) for k in range(kw)]

        @pl.when(i == 0)
        def _():
            for k in range(kw):
                dw_ref[k:k + 1, :] = rows[k]

        @pl.when(i > 0)
        def _():
            for k in range(kw):
                dw_ref[k:k + 1, :] += rows[k]

    return pl.pallas_call(
        body, name=name, grid=(c // tc, ni),
        in_specs=[pl.BlockSpec((tm, tc), lambda j, i: (i, j)),
                  pl.BlockSpec((8, tc), lambda j, i: (jnp.minimum((i + 1) * (tm // 8), t // 8 - 1), j)),
                  pl.BlockSpec((tm, tc), lambda j, i: (i, col0 + j)),
                  pl.BlockSpec((8, tc), lambda j, i: (jnp.maximum(i * (tm // 8) - 1, 0), col0 + j)),
                  pl.BlockSpec((kw, tc), lambda j, i: (0, j))],
        out_specs=[pl.BlockSpec((tm, tc), lambda j, i: (i, j)), pl.BlockSpec((kw, tc), lambda j, i: (0, j))],
        out_shape=[SDS((t, c), BF16), SDS((kw, c), F32)],
        compiler_params=_cp(dimension_semantics=("arbitrary", "arbitrary")),
    )(d_pre, d_pre, src, src, w)


GELU_C = 0.7978845608028654


def _gelu_parts(v):
    inner = GELU_C * (v + 0.044715 * v * v * v)
    th = jnp.tanh(inner)
    val = 0.5 * v * (1.0 + th)
    grad = 0.5 * (1.0 + th) + 0.5 * v * (1.0 - th * th) * GELU_C * (1.0 + 3.0 * 0.044715 * v * v)
    return val, grad


def _ffn_act_specs(tm, tc, nj, order):
    def im(f):
        return (lambda i, j: f(i, j)) if order == "ij" else (lambda j, i: f(i, j))
    halo = lambda i: jnp.maximum(i * (tm // 8) - 1, 0)
    return [pl.BlockSpec((tm, tc), im(lambda i, j: (i, j))), pl.BlockSpec((8, tc), im(lambda i, j: (halo(i), j))),
            pl.BlockSpec((tm, tc), im(lambda i, j: (i, nj + j))), pl.BlockSpec((8, tc), im(lambda i, j: (halo(i), nj + j))),
            pl.BlockSpec((3, tc), im(lambda i, j: (0, j))), pl.BlockSpec((3, tc), im(lambda i, j: (0, nj + j))),
            pl.BlockSpec((1, tc), im(lambda i, j: (0, j))), pl.BlockSpec((1, tc), im(lambda i, j: (0, nj + j)))]


def _ffn_act_fwd(up_raw, w, b, *, tm, tc=1408):
    t = up_raw.shape[0]
    tm = min(tm, t)
    nj = FFN_D_FF // tc

    def body(g_ref, gh_ref, v_ref, vh_ref, wg_ref, wv_ref, bg_ref, bv_ref, o_ref):
        first = pl.program_id(0) > 0
        gate = _conv_apply(g_ref[...], jnp.where(first, gh_ref[...], 0.0), wg_ref[...], bg_ref[...], 3)
        val = _conv_apply(v_ref[...], jnp.where(first, vh_ref[...], 0.0), wv_ref[...], bv_ref[...], 3)
        o_ref[...] = (_gelu_parts(gate)[0] * val).astype(BF16)

    return pl.pallas_call(
        body, name="ffn_act_fwd", grid=(t // tm, nj), in_specs=_ffn_act_specs(tm, tc, nj, "ij"),
        out_specs=pl.BlockSpec((tm, tc), lambda i, j: (i, j)), out_shape=SDS((t, FFN_D_FF), BF16),
        compiler_params=_cp(dimension_semantics=("arbitrary", "arbitrary")),
    )(up_raw, up_raw, up_raw, up_raw, w, w, b, b)


def _ffn_act_bwd(up_raw, d_act, w, b, *, tm, tc=1408):
    t = up_raw.shape[0]
    tm = min(tm, t)
    nj = FFN_D_FF // tc

    def body(g_ref, gh_ref, v_ref, vh_ref, wg_ref, wv_ref, bg_ref, bv_ref, da_ref, dg_ref, dv_ref, dbg_ref, dbv_ref):
        i = pl.program_id(1)
        gate = _conv_apply(g_ref[...], jnp.where(i > 0, gh_ref[...], 0.0), wg_ref[...], bg_ref[...], 3)
        val = _conv_apply(v_ref[...], jnp.where(i > 0, vh_ref[...], 0.0), wv_ref[...], bv_ref[...], 3)
        ge, dge = _gelu_parts(gate)
        da = da_ref[...]
        d_gate = da * val * dge
        d_val = da * ge
        dg_ref[...] = d_gate
        dv_ref[...] = d_val
        _accumulate(dbg_ref, i == 0, jnp.sum(d_gate, axis=0, keepdims=True))
        _accumulate(dbv_ref, i == 0, jnp.sum(d_val, axis=0, keepdims=True))

    tile = pl.BlockSpec((tm, tc), lambda j, i: (i, j))
    row = pl.BlockSpec((1, tc), lambda j, i: (0, j))
    return pl.pallas_call(
        body, name="ffn_act_bwd", grid=(nj, t // tm), in_specs=_ffn_act_specs(tm, tc, nj, "ji") + [tile],
        out_specs=[tile, tile, row, row],
        out_shape=[SDS((t, FFN_D_FF), F32), SDS((t, FFN_D_FF), F32), SDS((1, FFN_D_FF), F32), SDS((1, FFN_D_FF), F32)],
        compiler_params=_cp(dimension_semantics=("arbitrary", "arbitrary")),
    )(up_raw, up_raw, up_raw, up_raw, w, w, b, b, d_act)


def _softplus(v):
    e = jnp.exp(-jnp.abs(v))
    small = e * (1.0 - 0.5 * e)
    return jnp.maximum(v, 0.0) + jnp.where(e < 1e-4, small, jnp.log(1.0 + e))


def _dt_fwd(proj, bias_pad, *, tm):
    t = proj.shape[0]
    tm = min(tm, t)

    def body(x_ref, b_ref, o_ref):
        o_ref[...] = _softplus(x_ref[...] + b_ref[...])

    return pl.pallas_call(
        body, name="dt_fwd", grid=(t // tm,),
        in_specs=[pl.BlockSpec((tm, 128), lambda i: (i, OFF_DT // 128)), pl.BlockSpec((1, 128), lambda i: (0, 0))],
        out_specs=pl.BlockSpec((tm, 128), lambda i: (i, 0)), out_shape=SDS((t, 128), F32),
        compiler_params=_cp(dimension_semantics=("arbitrary",)),
    )(proj, bias_pad)


def _dt_bwd(d_dt, proj, bias_pad, *, tm):
    t = proj.shape[0]
    tm = min(tm, t)

    def body(g_ref, x_ref, b_ref, o_ref, db_ref):
        d_raw = g_ref[...] * _sigmoid(x_ref[...] + b_ref[...])
        o_ref[:, 0:128] = d_raw.astype(BF16)
        o_ref[:, 128:512] = jnp.zeros((tm, 384), BF16)
        _accumulate(db_ref, pl.program_id(0) == 0, jnp.sum(d_raw, axis=0, keepdims=True))

    return pl.pallas_call(
        body, name="dt_bwd", grid=(t // tm,),
        in_specs=[pl.BlockSpec((tm, 128), lambda i: (i, 0)), pl.BlockSpec((tm, 128), lambda i: (i, OFF_DT // 128)),
                  pl.BlockSpec((1, 128), lambda i: (0, 0))],
        out_specs=[pl.BlockSpec((tm, 512), lambda i: (i, 0)), pl.BlockSpec((1, 128), lambda i: (0, 0))],
        out_shape=[SDS((t, 512), BF16), SDS((1, 128), F32)],
        compiler_params=_cp(dimension_semantics=("arbitrary",)),
    )(d_dt, proj, bias_pad)


def _ssd_decay(dt_ref, dtT_ref, al_ref, alT_ref):
    tril = (_iota((CHUNK, CHUNK), 0) >= _iota((CHUNK, CHUNK), 1)).astype(F32)
    dt = dt_ref[0]
    a_row = -jnp.exp(al_ref[0])
    adt = dt * a_row
    adt_t = dtT_ref[0] * (-jnp.exp(alT_ref[0]))
    acs = jnp.dot(tril, adt, precision=HIGHEST, preferred_element_type=F32)
    acs_t = lax.dot_general(adt_t, tril, (((1,), (1,)), ((), ())), precision=HIGHEST, preferred_element_type=F32)
    return dt, a_row, acs, acs_t


def _ssd_specs(nc, rev):
    ci = (lambda c: nc - 1 - c) if rev else (lambda c: c)
    return [pl.BlockSpec((CHUNK, 512), lambda c, g: (ci(c), g)),
            pl.BlockSpec((CHUNK, 128), lambda c, g: (ci(c), 16 + g)),
            pl.BlockSpec((CHUNK, 128), lambda c, g: (ci(c), 20 + g)),
            pl.BlockSpec((1, CHUNK, 128), lambda c, g: (g, ci(c), 0)),
            pl.BlockSpec((1, 8, CHUNK), lambda c, g: (g, 0, ci(c))),
            pl.BlockSpec((1, 1, 128), lambda c, g: (g, 0, 0)),
            pl.BlockSpec((1, 8, 1), lambda c, g: (g, 0, 0)),
            pl.BlockSpec((1, 512), lambda c, g: (0, g))]


def _ssd_fwd(xc, dtg, dtg_t, alog, alog_t, d_exp):
    t = xc.shape[0]
    nc = t // CHUNK

    def body(x_ref, b_ref, c_ref, dt_ref, dtT_ref, al_ref, alT_ref, d_ref, y_ref, hs_ref, h_scr):
        c, g = pl.program_id(0), pl.program_id(1)

        @pl.when(c == 0)
        def _():
            h_scr[g] = jnp.zeros((8, SSD_HEAD_DIM, SSD_D_STATE), F32)

        dt, _, acs, acs_t = _ssd_decay(dt_ref, dtT_ref, al_ref, alT_ref)
        causal = _iota((CHUNK, CHUNK), 0) >= _iota((CHUNK, CHUNK), 1)
        bb = b_ref[...].astype(BF16)
        cb16 = c_ref[...].astype(BF16)
        cb = lax.dot_general(cb16, bb, (((1,), (1,)), ((), ())), preferred_element_type=F32)
        a_end = acs[CHUNK - 1:CHUNK, :]
        for j in range(8):
            sl = slice(j * 64, (j + 1) * 64)
            xj = x_ref[:, sl]
            a_col = acs[:, j:j + 1]
            xdt = xj * dt[:, j:j + 1]
            lmat = jnp.where(causal, jnp.exp(jnp.minimum(a_col - acs_t[j:j + 1, :], 0.0)), 0.0)
            y_diag = jnp.dot((cb * lmat).astype(BF16), xdt.astype(BF16), preferred_element_type=F32)
            hj = h_scr[g, j]
            y_off = lax.dot_general(cb16, hj.astype(BF16), (((1,), (1,)), ((), ())), preferred_element_type=F32)
            y_ref[:, sl] = y_diag + jnp.exp(a_col) * y_off + d_ref[:, sl] * xj
            decay = jnp.exp(a_end[:, j:j + 1] - a_col)
            st = lax.dot_general((xdt * decay).astype(BF16), bb, (((0,), (0,)), ((), ())), preferred_element_type=F32)
            hs_ref[0, j] = hj
            h_scr[g, j] = hj * jnp.exp(a_end[:, j:j + 1]) + st

    return pl.pallas_call(
        body, name="ssd_fwd", grid=(nc, SSD_N_GROUPS), in_specs=_ssd_specs(nc, False),
        out_specs=[pl.BlockSpec((CHUNK, 512), lambda c, g: (c, g)),
                   pl.BlockSpec((1, 8, SSD_HEAD_DIM, SSD_D_STATE), lambda c, g: (c, g, 0, 0))],
        out_shape=[SDS((t, SSD_D_INNER), F32), SDS((nc, SSD_N_HEADS, SSD_HEAD_DIM, SSD_D_STATE), F32)],
        scratch_shapes=[pltpu.VMEM((SSD_N_GROUPS, 8, SSD_HEAD_DIM, SSD_D_STATE), F32)],
        compiler_params=_cp(dimension_semantics=("arbitrary", "arbitrary")),
    )(xc, xc, xc, dtg, dtg_t, alog, alog_t, d_exp)


def _ssd_bwd(xc, dtg, dtg_t, alog, alog_t, d_exp, d_y, hs):
    t = xc.shape[0]
    nc = t // CHUNK

    def body(x_ref, b_ref, c_ref, dt_ref, dtT_ref, al_ref, alT_ref, d_ref, dy_ref, hs_ref,
             dx_ref, db_ref, dc_ref, ddt_ref, dal_ref, dd_ref, g_scr):
        c, g = pl.program_id(0), pl.program_id(1)

        @pl.when(c == 0)
        def _():
            g_scr[g] = jnp.zeros((8, SSD_HEAD_DIM, SSD_D_STATE), F32)

        dt, a_row, acs, acs_t = _ssd_decay(dt_ref, dtT_ref, al_ref, alT_ref)
        row, col = _iota((CHUNK, CHUNK), 0), _iota((CHUNK, CHUNK), 1)
        causal = row >= col
        triu = (row <= col).astype(F32)
        bb = b_ref[...].astype(BF16)
        cb16 = c_ref[...].astype(BF16)
        cb = lax.dot_general(cb16, bb, (((1,), (1,)), ((), ())), preferred_element_type=F32)
        a_end = acs[CHUNK - 1:CHUNK, :]
        lane1 = _iota((1, 128), 1)
        zero = jnp.zeros((CHUNK, CHUNK), F32)
        d_cb, d_c, d_b = zero, zero, zero
        da_col, da_row, dxx = zero, zero, zero
        dd_row = jnp.zeros((1, 128), F32)
        for j in range(8):
            sl = slice(j * 64, (j + 1) * 64)
            xj, dyj = x_ref[:, sl], dy_ref[:, sl]
            dtc = dt[:, j:j + 1]
            a_col = acs[:, j:j + 1]
            xdt = xj * dtc
            xdt16 = xdt.astype(BF16)
            lmat = jnp.where(causal, jnp.exp(jnp.minimum(a_col - acs_t[j:j + 1, :], 0.0)), 0.0)
            mmat = cb * lmat
            e_a = jnp.exp(a_col)
            e_end = jnp.exp(a_end[:, j:j + 1])
            decay = jnp.exp(a_end[:, j:j + 1] - a_col)
            hj = hs_ref[0, j]
            h16 = hj.astype(BF16)
            gj = g_scr[g, j]
            g16 = gj.astype(BF16)
            dy16 = dyj.astype(BF16)
            dye16 = (dyj * e_a).astype(BF16)
            d_m = jnp.where(causal, lax.dot_general(dy16, xdt16, (((1,), (1,)), ((), ())), preferred_element_type=F32), 0.0)
            d_x = lax.dot_general(mmat.astype(BF16), dy16, (((0,), (0,)), ((), ())), preferred_element_type=F32)
            d_seg = d_m * mmat
            d_cb = d_cb + d_m * lmat
            ch = lax.dot_general(cb16, h16, (((1,), (1,)), ((), ())), preferred_element_type=F32)
            d_a = jnp.sum(dyj * e_a * ch, axis=1, keepdims=True)
            d_c = d_c + jnp.dot(dye16, h16, preferred_element_type=F32)
            d_h = gj * e_end + lax.dot_general(dye16, cb16, (((0,), (0,)), ((), ())), preferred_element_type=F32)
            bg = lax.dot_general(bb, g16, (((1,), (1,)), ((), ())), preferred_element_type=F32)
            d_x = d_x + decay * bg
            d_decay = jnp.sum(xdt * bg, axis=1, keepdims=True) * decay
            d_b = d_b + jnp.dot((xdt * decay).astype(BF16), g16, preferred_element_type=F32)
            d_end = e_end * jnp.sum(gj * hj) + jnp.sum(d_decay)
            d_a = d_a - d_decay + jnp.sum(d_seg, axis=1, keepdims=True)
            d_a = d_a + jnp.where(_iota((CHUNK, 1), 0) == CHUNK - 1, d_end, 0.0)
            da_col = jnp.where(col == j, d_a, da_col)
            da_row = jnp.where(row == j, -jnp.sum(d_seg, axis=0, keepdims=True), da_row)
            dxx = jnp.where(col == j, jnp.sum(d_x * xj, axis=1, keepdims=True), dxx)
            dx_ref[:, sl] = d_x * dtc + d_ref[:, sl] * dyj
            dd_row = jnp.where(lane1 == j, jnp.sum(dyj * xj), dd_row)
            g_scr[g, j] = d_h
        d_cb16 = d_cb.astype(BF16)
        dc_ref[...] = d_c + jnp.dot(d_cb16, bb, preferred_element_type=F32)
        db_ref[...] = d_b + lax.dot_general(d_cb16, cb16, (((0,), (0,)), ((), ())), preferred_element_type=F32)
        d_adt = (jnp.dot(triu, da_col, precision=HIGHEST, preferred_element_type=F32)
                 + lax.dot_general(triu, da_row, (((1,), (1,)), ((), ())), precision=HIGHEST, preferred_element_type=F32))
        ddt_ref[0] = d_adt * a_row + dxx
        d_alog = jnp.sum(d_adt * dt, axis=0, keepdims=True) * a_row
        first = c == 0

        @pl.when(first)
        def _():
            dal_ref[g] = d_alog
            dd_ref[g] = dd_row

        @pl.when(jnp.logical_not(first))
        def _():
            dal_ref[g] += d_alog
            dd_ref[g] += dd_row

    rc = lambda c: nc - 1 - c
    whole = pl.BlockSpec((SSD_N_GROUPS, 1, 128), lambda c, g: (0, 0, 0))
    return pl.pallas_call(
        body, name="ssd_bwd", grid=(nc, SSD_N_GROUPS),
        in_specs=_ssd_specs(nc, True) + [pl.BlockSpec((CHUNK, 512), lambda c, g: (rc(c), g)),
                                        pl.BlockSpec((1, 8, SSD_HEAD_DIM, SSD_D_STATE), lambda c, g: (rc(c), g, 0, 0))],
        out_specs=[pl.BlockSpec((CHUNK, 512), lambda c, g: (rc(c), g)), pl.BlockSpec((CHUNK, 128), lambda c, g: (rc(c), g)),
                   pl.BlockSpec((CHUNK, 128), lambda c, g: (rc(c), g)), pl.BlockSpec((1, CHUNK, 128), lambda c, g: (g, rc(c), 0)),
                   whole, whole],
        out_shape=[SDS((t, SSD_D_INNER), F32), SDS((t, 512), F32), SDS((t, 512), F32), SDS((SSD_N_GROUPS, t, 128), F32),
                   SDS((SSD_N_GROUPS, 1, 128), F32), SDS((SSD_N_GROUPS, 1, 128), F32)],
        scratch_shapes=[pltpu.VMEM((SSD_N_GROUPS, 8, SSD_HEAD_DIM, SSD_D_STATE), F32)],
        compiler_params=_cp(dimension_semantics=("arbitrary", "arbitrary")),
    )(xc, xc, xc, dtg, dtg_t, alog, alog_t, d_exp, d_y, hs)


def _gated_norm_fwd(y, proj, w, *, tm):
    t = y.shape[0]
    tm = min(tm, t)

    def body(y_ref, z_ref, w_ref, o_ref):
        gv = y_ref[...] * _silu_parts(z_ref[...])[0]
        r = lax.rsqrt(jnp.mean(gv * gv, axis=-1, keepdims=True) + NORM_EPS)
        o_ref[...] = (gv * r * w_ref[...]).astype(BF16)

    tile = pl.BlockSpec((tm, 512), lambda i, g: (i, g))
    return pl.pallas_call(
        body, name="gated_norm_fwd", grid=(t // tm, SSD_N_GROUPS),
        in_specs=[tile, tile, pl.BlockSpec((1, 512), lambda i, g: (0, g))], out_specs=tile,
        out_shape=SDS((t, SSD_D_INNER), BF16),
        compiler_params=_cp(dimension_semantics=("arbitrary", "arbitrary")),
    )(y, proj, w)


def _rope(ch, cos_t, sin_t):
    first = (_iota(ch.shape, 1) & 32) == 0
    partner = jnp.where(first, pltpu.roll(ch, 96, axis=1), pltpu.roll(ch, 32, axis=1))
    return ch * cos_t + partner * sin_t


def _rope_qkv(proj, cos_t, sin_t, *, tm):
    t = proj.shape[0]
    tm = min(tm, t)

    def body(q_ref, k_ref, v_ref, c_ref, s_ref, qr_ref, kp_ref, vp_ref):
        cv, sv = c_ref[...], s_ref[...]
        lo = _iota((tm, 128), 1) < 64
        for m in range(8):
            sl = slice(m * 128, (m + 1) * 128)
            qr_ref[:, sl] = (_rope(q_ref[:, sl], cv, sv) * 0.125).astype(BF16)
        for m2 in range(2):
            sl = slice(m2 * 128, (m2 + 1) * 128)
            for src, dst in ((_rope(k_ref[:, sl], cv, sv), kp_ref), (v_ref[:, sl], vp_ref)):
                sw = pltpu.roll(src, 64, axis=1)
                base = 4 * m2 * 128
                dst[:, base:base + 128] = jnp.where(lo, src, 0.0).astype(BF16)
                dst[:, base + 128:base + 256] = jnp.where(lo, 0.0, sw).astype(BF16)
                dst[:, base + 256:base + 384] = jnp.where(lo, sw, 0.0).astype(BF16)
                dst[:, base + 384:base + 512] = jnp.where(lo, 0.0, src).astype(BF16)

    return pl.pallas_call(
        body, name="rope_qkv", grid=(t // tm,),
        in_specs=[pl.BlockSpec((tm, 1024), lambda i: (i, OFF_Q // 1024)), pl.BlockSpec((tm, 256), lambda i: (i, OFF_K // 256)),
                  pl.BlockSpec((tm, 256), lambda i: (i, OFF_V // 256)), pl.BlockSpec((tm, 128), lambda i: (i, 0)),
                  pl.BlockSpec((tm, 128), lambda i: (i, 0))],
        out_specs=[pl.BlockSpec((tm, 1024), lambda i: (i, 0))] * 3,
        out_shape=[SDS((t, 1024), BF16)] * 3,
        compiler_params=_cp(dimension_semantics=("arbitrary",)),
    )(proj, proj, proj, cos_t, sin_t)


def _attn_valid(n):
    qi, kj = _iota((CHUNK, 2 * CHUNK), 0), _iota((CHUNK, 2 * CHUNK), 1)
    return (kj > qi) & (kj <= qi + CHUNK) & ((n > 0) | (kj >= CHUNK))


def _attn_fwd(qr, kp, vp, sinks):
    t = qr.shape[0]
    nb = t // CHUNK

    def body(q_ref, kc_ref, kprev_ref, vc_ref, vprev_ref, sk_ref, o_ref, lse_ref):
        n = pl.program_id(0)
        valid = _attn_valid(n)
        lane = _iota((CHUNK, 128), 1)
        lse_all = jnp.zeros((CHUNK, 128), F32)
        for m in range(8):
            g = m // 2
            qch = q_ref[:, m * 128:(m + 1) * 128]
            o_pair = jnp.zeros((CHUNK, 128), F32)
            for e in range(2):
                h = 2 * m + e
                sl = slice((2 * g + e) * 128, (2 * g + e + 1) * 128)
                kk = jnp.concatenate([kprev_ref[:, sl], kc_ref[:, sl]], axis=0)
                vv = jnp.concatenate([vprev_ref[:, sl], vc_ref[:, sl]], axis=0)
                s = lax.dot_general(qch, kk, (((1,), (1,)), ((), ())), preferred_element_type=F32)
                s = jnp.where(valid, s, NEG)
                sink = sk_ref[0:1, h:h + 1]
                mx = jnp.maximum(jnp.max(s, axis=1, keepdims=True), sink)
                p = jnp.exp(s - mx)
                den = jnp.sum(p, axis=1, keepdims=True) + jnp.exp(sink - mx)
                o_pair = o_pair + jnp.dot((p / den).astype(BF16), vv, preferred_element_type=F32)
                lse_all = jnp.where(lane == h, mx + jnp.log(den), lse_all)
            o_ref[:, m * 128:(m + 1) * 128] = o_pair.astype(BF16)
        lse_ref[...] = lse_all

    cur = pl.BlockSpec((CHUNK, 1024), lambda n: (n, 0))
    prev = pl.BlockSpec((CHUNK, 1024), lambda n: (jnp.maximum(n - 1, 0), 0))
    return pl.pallas_call(
        body, name="attn_fwd", grid=(nb,),
        in_specs=[cur, cur, prev, cur, prev, pl.BlockSpec((1, 128), lambda n: (0, 0))],
        out_specs=[cur, pl.BlockSpec((CHUNK, 128), lambda n: (n, 0))],
        out_shape=[SDS((t, 1024), BF16), SDS((t, 128), F32)],
        compiler_params=_cp(dimension_semantics=("arbitrary",)),
    )(qr, kp, kp, vp, vp, sinks)


def _attn_bwd(qr, kp, vp, d_o, o, lse, sinks, cos_t, sin_t):
    t = qr.shape[0]
    nb = t // CHUNK

    def body(q_ref, kc_ref, kprev_ref, vc_ref, vprev_ref, do_ref, o_ref, lse_ref, sk_ref, c_ref, s_ref, cp_ref, sp_ref,
             dq_ref, dk_ref, dv_ref, dsk_ref, acc_k, acc_v):
        n = pl.program_id(0)
        lane = _iota((CHUNK, 128), 1)
        lo = lane < 64
        lane1 = _iota((1, 128), 1)

        @pl.when(n == 0)
        def _():
            acc_k[...] = jnp.zeros_like(acc_k)
            acc_v[...] = jnp.zeros_like(acc_v)
            dsk_ref[...] = jnp.zeros((1, 128), F32)

        @pl.when(n > 0)
        def _():
            for r in range(8):
                acc_k[r, 0:CHUNK] = acc_k[r, CHUNK:2 * CHUNK]
                acc_v[r, 0:CHUNK] = acc_v[r, CHUNK:2 * CHUNK]
                acc_k[r, CHUNK:2 * CHUNK] = jnp.zeros((CHUNK, 128), F32)
                acc_v[r, CHUNK:2 * CHUNK] = jnp.zeros((CHUNK, 128), F32)

        @pl.when(n < nb)
        def _():
            valid = _attn_valid(n)
            lse_all = lse_ref[...]
            dsk = jnp.zeros((1, 128), F32)
            for m in range(8):
                g = m // 2
                csl = slice(m * 128, (m + 1) * 128)
                qch = q_ref[:, csl]
                doch = do_ref[:, csl]
                prod = doch.astype(F32) * o_ref[:, csl].astype(F32)
                dq_pair = jnp.zeros((CHUNK, 128), F32)
                for e in range(2):
                    h = 2 * m + e
                    sl = slice((2 * g + e) * 128, (2 * g + e + 1) * 128)
                    kk = jnp.concatenate([kprev_ref[:, sl], kc_ref[:, sl]], axis=0)
                    vv = jnp.concatenate([vprev_ref[:, sl], vc_ref[:, sl]], axis=0)
                    lse_h = lse_all[:, h:h + 1]
                    s = lax.dot_general(qch, kk, (((1,), (1,)), ((), ())), preferred_element_type=F32)
                    p = jnp.exp(jnp.where(valid, s, NEG) - lse_h)
                    delta = jnp.sum(jnp.where(lo if e == 0 else jnp.logical_not(lo), prod, 0.0), axis=1, keepdims=True)
                    d_p = lax.dot_general(doch, vv, (((1,), (1,)), ((), ())), preferred_element_type=F32)
                    d_s16 = (p * (d_p - delta)).astype(BF16)
                    dq_pair = dq_pair + jnp.dot(d_s16, kk, preferred_element_type=F32)
                    acc_k[2 * g + e] += lax.dot_general(d_s16, qch, (((0,), (0,)), ((), ())), preferred_element_type=F32)
                    acc_v[2 * g + e] += lax.dot_general(p.astype(BF16), doch, (((0,), (0,)), ((), ())), preferred_element_type=F32)
                    p_sink = jnp.exp(sk_ref[0:1, h:h + 1] - lse_h)
                    dsk = jnp.where(lane1 == h, -jnp.sum(p_sink * delta), dsk)
                dq_ref[:, csl] = (_rope(dq_pair, c_ref[...], -s_ref[...]) * 0.125).astype(BF16)
            dsk_ref[...] += dsk

        @pl.when(n > 0)
        def _():
            for m2 in range(2):
                halves = []
                for g in (2 * m2, 2 * m2 + 1):
                    for acc in (acc_k, acc_v):
                        comb = jnp.where(lo, acc[2 * g, 0:CHUNK], acc[2 * g + 1, 0:CHUNK])
                        halves.append(comb + pltpu.roll(comb, 64, axis=1))
                d_kr = jnp.where(lo, halves[0], halves[2])
                d_v = jnp.where(lo, halves[1], halves[3])
                sl = slice(m2 * 128, (m2 + 1) * 128)
                dk_ref[:, sl] = _rope(d_kr, cp_ref[...], -sp_ref[...]).astype(BF16)
                dv_ref[:, sl] = d_v.astype(BF16)

    qn = lambda n: jnp.minimum(n, nb - 1)
    pn = lambda n: jnp.maximum(jnp.minimum(n, nb) - 1, 0)
    cur = pl.BlockSpec((CHUNK, 1024), lambda n: (qn(n), 0))
    prev = pl.BlockSpec((CHUNK, 1024), lambda n: (pn(n), 0))
    cur128 = pl.BlockSpec((CHUNK, 128), lambda n: (qn(n), 0))
    prev128 = pl.BlockSpec((CHUNK, 128), lambda n: (pn(n), 0))
    one = pl.BlockSpec((1, 128), lambda n: (0, 0))
    return pl.pallas_call(
        body, name="attn_bwd", grid=(nb + 1,),
        in_specs=[cur, cur, prev, cur, prev, cur, cur, cur128, one, cur128, cur128, prev128, prev128],
        out_specs=[cur, pl.BlockSpec((CHUNK, 256), lambda n: (pn(n), 0)), pl.BlockSpec((CHUNK, 256), lambda n: (pn(n), 0)), one],
        out_shape=[SDS((t, 1024), BF16), SDS((t, KV_WIDTH), BF16), SDS((t, KV_WIDTH), BF16), SDS((1, 128), F32)],
        scratch_shapes=[pltpu.VMEM((8, 2 * CHUNK, 128), F32), pltpu.VMEM((8, 2 * CHUNK, 128), F32)],
        compiler_params=_cp(dimension_semantics=("arbitrary",)),
    )(qr, kp, kp, vp, vp, d_o, o, lse, sinks, cos_t, sin_t, cos_t, sin_t)


def _adamw(name, w, g, m, v, *, tr):
    rows, cols = w.shape
    tr = min(tr, rows)
    assert rows % tr == 0

    def body(w_ref, g_ref, m_ref, v_ref, d_ref, nm_ref, nv_ref):
        gv = g_ref[...]
        nm = ADAM_B1 * m_ref[...] + (1.0 - ADAM_B1) * gv
        nv = ADAM_B2 * v_ref[...] + (1.0 - ADAM_B2) * (gv * gv)
        m_hat = nm / (1.0 - ADAM_B1 ** ADAM_STEP)
        v_hat = nv / (1.0 - ADAM_B2 ** ADAM_STEP)
        d_ref[...] = -ADAM_LR * (m_hat / (jnp.sqrt(v_hat) + ADAM_EPS) + ADAM_WD * w_ref[...])
        nm_ref[...] = nm
        nv_ref[...] = nv

    tile = pl.BlockSpec((tr, cols), lambda i: (i, 0))
    return pl.pallas_call(
        body, name=name, grid=(rows // tr,), in_specs=[tile] * 4, out_specs=[tile] * 3,
        out_shape=[SDS((rows, cols), F32)] * 3, compiler_params=_cp(dimension_semantics=("arbitrary",)),
    )(w, g, m, v)


def _local_step(x, cos_t, sin_t, tgt, wb, ps):
    t = x.shape[0]
    tm = min(512, t)
    tmw = min(1024, t)
    ij = lambda i, j, k: (i, j)
    i0 = lambda i, j, k: (i, 0)
    c0 = lambda i, j, k: (0, 0)
    cj = lambda i, j, k: (0, j)

    proj, u = _norm_mm("in_proj", x, ps['norm_mix_pre_w'], wb['cat'], tm=tmw, tn=512)
    xc = _conv_silu_fwd(proj, ps['ssd_conv_w'], ps['ssd_conv_b'], tm=tm)
    bias_pad = jnp.pad(ps['ssd_dt_bias'], ((0, 0), (0, 96)))
    dt = _dt_fwd(proj, bias_pad, tm=tmw)
    dt32 = dt[:, :SSD_N_HEADS].reshape(t, SSD_N_GROUPS, 8)
    dtg = jnp.pad(dt32.transpose(1, 0, 2), ((0, 0), (0, 0), (0, 120)))
    dtg_t = dt32.transpose(1, 2, 0)
    alog = jnp.pad(ps['ssd_a_log'].reshape(SSD_N_GROUPS, 1, 8), ((0, 0), (0, 0), (0, 120)))
    alog_t = ps['ssd_a_log'].reshape(SSD_N_GROUPS, 8, 1)
    d_exp = jnp.repeat(ps['ssd_d'], SSD_HEAD_DIM, axis=1)
    y, hs = _ssd_fwd(xc, dtg, dtg_t, alog, alog_t, d_exp)
    gn = _gated_norm_fwd(y, proj, ps['ssd_norm_w'], tm=tm)
    qr, kp, vp = _rope_qkv(proj, cos_t, sin_t, tm=tm)
    sinks = jnp.pad(ps['attn_sinks'], ((0, 0), (0, 112)))
    ao, lse = _attn_fwd(qr, kp, vp, sinks)
    y_attn = _mm_plain("attn_out", ao, wb['ao'], tm=tmw, tn=512, tk=1024)

    def merge_ep(acc, i, j, ins, outs):
        gs, ga, ya = ins
        outs[0][...] = (_sigmoid(gs[...]) * acc + _sigmoid(ga[...]) * ya[...]).astype(BF16)
        outs[1][...] = acc

    merged, y_ssd = _mm_call(
        "ssd_out_merge", gn, wb['so'], tm=tmw, tn=512, tk=2048, epilogue=merge_ep,
        extra_in=[(proj, (tmw, 512), lambda i, j, k: (i, OFF_GS // 512 + j)),
                  (proj, (tmw, 512), lambda i, j, k: (i, OFF_GA // 512 + j)), (y_attn, (tmw, 512), ij)],
        outs=[((t, D_MODEL), BF16, (tmw, 512), ij), ((t, D_MODEL), F32, (tmw, 512), ij)])

    def mix_ep(acc, i, j, ins, outs):
        xv, wn = ins
        r = lax.rsqrt(jnp.mean(acc * acc, axis=-1, keepdims=True) + NORM_EPS)
        outs[0][...] = xv[...] + acc * r * wn[...]
        outs[1][...] = acc

    x1, mmix = _mm_call(
        "mix_out", merged, wb['mix'], tm=tm, tn=D_MODEL, tk=1024, epilogue=mix_ep,
        extra_in=[(x, (tm, D_MODEL), i0), (ps['norm_mix_post_w'], (1, D_MODEL), c0)],
        outs=[((t, D_MODEL), F32, (tm, D_MODEL), i0), ((t, D_MODEL), F32, (tm, D_MODEL), i0)])

    up_raw, h = _norm_mm("ffn_up", x1, ps['norm_ffn_pre_w'], wb['up'], tm=tmw, tn=512)
    act = _ffn_act_fwd(up_raw, ps['ffn_conv_w'], ps['ffn_conv_b'], tm=min(256, t))

    def loss_ep(acc, i, j, ins, outs):
        x1v, tg, wn = ins
        d_ff_ref, dout_ref, loss_ref, dw_ref = outs
        wv = wn[...]
        r = lax.rsqrt(jnp.mean(acc * acc, axis=-1, keepdims=True) + NORM_EPS)
        err = x1v[...] + acc * r * wv - tg[...]
        dout = err * (1.0 / D_MODEL)
        dout_ref[...] = dout
        d_ff, dw = _rms_bwd(acc, wv, dout)
        d_ff_ref[...] = d_ff.astype(BF16)
        _accumulate(dw_ref, i == 0, dw)
        _accumulate(loss_ref, i == 0, jnp.sum(err * err, keepdims=True) * (0.5 / D_MODEL))

    d_ff, dout, loss, g_norm_ffn_post = _mm_call(
        "ffn_down_loss", act, wb['dn'], tm=tm, tn=D_MODEL, tk=1408, epilogue=loss_ep,
        extra_in=[(x1, (tm, D_MODEL), i0), (tgt, (tm, D_MODEL), i0), (ps['norm_ffn_post_w'], (1, D_MODEL), c0)],
        outs=[((t, D_MODEL), BF16, (tm, D_MODEL), i0), ((t, D_MODEL), F32, (tm, D_MODEL), i0),
              ((1, 1), F32, (1, 1), c0), ((1, D_MODEL), F32, (1, D_MODEL), c0)])

    d_act = _mm_plain("d_act", d_ff, wb['dn_t'], tm=tmw, tn=1408, tk=1024)
    g_w_down = _mm_plain("g_w_down", act, d_ff, tm=1408, tn=1024, tk=tm, trans_a=True)
    d_gate, d_val, db_g, db_v = _ffn_act_bwd(up_raw, d_act, ps['ffn_conv_w'], ps['ffn_conv_b'], tm=min(256, t))
    d_up = jnp.concatenate([d_gate, d_val], axis=1)
    d_up_raw, g_ffn_conv_w = _conv_bwd2("ffn_conv_bwd2", d_up, up_raw, 0, ps['ffn_conv_w'], tm=min(256, t), tc=1408)

    def dx1_ep(acc, i, j, ins, outs):
        x1v, wpre, dout_v, mmv, wpost = ins
        d_x1_ref, d_mm_ref, dwpre_ref, dwpost_ref = outs
        d_n, dw_pre = _rms_bwd(x1v[...], wpre[...], acc)
        d_x1 = dout_v[...] + d_n
        d_x1_ref[...] = d_x1
        d_mm, dw_post = _rms_bwd(mmv[...], wpost[...], d_x1)
        d_mm_ref[...] = d_mm.astype(BF16)
        _accumulate(dwpre_ref, i == 0, dw_pre)
        _accumulate(dwpost_ref, i == 0, dw_post)

    d_x1, d_mm, g_norm_ffn_pre, g_norm_mix_post = _mm_call(
        "d_h", d_up_raw, wb['up_t'], tm=tm, tn=D_MODEL, tk=1408, epilogue=dx1_ep,
        extra_in=[(x1, (tm, D_MODEL), i0), (ps['norm_ffn_pre_w'], (1, D_MODEL), c0), (dout, (tm, D_MODEL), i0),
                  (mmix, (tm, D_MODEL), i0), (ps['norm_mix_post_w'], (1, D_MODEL), c0)],
        outs=[((t, D_MODEL), F32, (tm, D_MODEL), i0), ((t, D_MODEL), BF16, (tm, D_MODEL), i0),
              ((1, D_MODEL), F32, (1, D_MODEL), c0), ((1, D_MODEL), F32, (1, D_MODEL), c0)])
    g_w_up = _mm_plain("g_w_up", h, d_up_raw, tm=1024, tn=512, tk=tm, trans_a=True)

    def dmerge_ep(acc, i, j, ins, outs):
        gs, ga, ys, ya = ins
        sg_s, sg_a = _sigmoid(gs[...]), _sigmoid(ga[...])
        outs[0][...] = (acc * sg_s).astype(BF16)
        outs[1][...] = (acc * sg_a).astype(BF16)
        outs[2][...] = (acc * ys[...] * sg_s * (1.0 - sg_s)).astype(BF16)
        outs[3][...] = (acc * ya[...] * sg_a * (1.0 - sg_a)).astype(BF16)

    d_yssd, d_yattn, d_gs, d_ga = _mm_call(
        "d_merged", d_mm, wb['mix_t'], tm=tmw, tn=512, tk=1024, epilogue=dmerge_ep,
        extra_in=[(proj, (tmw, 512), lambda i, j, k: (i, OFF_GS // 512 + j)),
                  (proj, (tmw, 512), lambda i, j, k: (i, OFF_GA // 512 + j)), (y_ssd, (tmw, 512), ij), (y_attn, (tmw, 512), ij)],
        outs=[((t, D_MODEL), BF16, (tmw, 512), ij)] * 4)
    g_w_mix = _mm_plain("g_w_mix", merged, d_mm, tm=1024, tn=512, tk=tm, trans_a=True)

    def dgn_ep(acc, i, j, ins, outs):
        yv, zv, wn = ins
        d_y_ref, d_z_ref, dw_ref = outs
        zz = zv[...]
        sz = _sigmoid(zz)
        silu = zz * sz
        gv = yv[...] * silu
        r = lax.rsqrt(jnp.mean(gv * gv, axis=-1, keepdims=True) + NORM_EPS)
        gh = gv * r
        dgh = acc * wn[...]
        dg = r * (dgh - gh * jnp.mean(dgh * gh, axis=-1, keepdims=True))
        d_y_ref[...] = dg * silu
        d_z_ref[...] = (dg * yv[...] * (sz * (1.0 + zz * (1.0 - sz)))).astype(BF16)
        dw = jnp.sum(acc * gh, axis=0, keepdims=True)

        @pl.when(i == 0)
        def _():
            dw_ref[j] = dw

        @pl.when(i > 0)
        def _():
            dw_ref[j] += dw

    d_y, d_z, g_ssd_norm = _mm_call(
        "d_gn", d_yssd, wb['so_t'], tm=tm, tn=512, tk=1024, epilogue=dgn_ep,
        extra_in=[(y, (tm, 512), ij), (proj, (tm, 512), ij), (ps['ssd_norm_w'], (1, 512), cj)],
        outs=[((t, SSD_D_INNER), F32, (tm, 512), ij), ((t, SSD_D_INNER), BF16, (tm, 512), ij),
              ((SSD_N_GROUPS, 1, 512), F32, (SSD_N_GROUPS, 1, 512), lambda i, j, k: (0, 0, 0))])
    g_ssd_norm = g_ssd_norm.reshape(1, SSD_D_INNER)
    g_w_so = _mm_plain("g_w_so", gn, d_yssd, tm=1024, tn=512, tk=tm, trans_a=True)
    dxs, d_bm, d_cm, d_dtg, d_alog, d_dd = _ssd_bwd(xc, dtg, dtg_t, alog, alog_t, d_exp, d_y, hs)
    d_xc = jnp.concatenate([dxs, d_bm, d_cm], axis=1)
    d_pre, g_ssd_conv_b = _conv_silu_bwd1(d_xc, proj, ps['ssd_conv_w'], ps['ssd_conv_b'], tm=tm)
    d_xbc, g_ssd_conv_w = _conv_bwd2("ssd_conv_bwd2", d_pre, proj, OFF_XBC, ps['ssd_conv_w'], tm=tm, tc=512)
    d_dt = jnp.pad(d_dtg[:, :, :8].transpose(1, 0, 2).reshape(t, SSD_N_HEADS), ((0, 0), (0, 96)))
    d_dtraw, g_dt_bias = _dt_bwd(d_dt, proj, bias_pad, tm=tmw)

    d_ao = _mm_plain("d_ao", d_yattn, wb['ao_t'], tm=tmw, tn=512, tk=1024, out_dtype=BF16)
    g_w_ao = _mm_plain("g_w_ao", ao, d_yattn, tm=1024, tn=512, tk=tm, trans_a=True)
    d_q, d_k, d_v, g_sinks = _attn_bwd(qr, kp, vp, d_ao, ao, lse, sinks, cos_t, sin_t)

    d_proj = jnp.concatenate([d_z, d_xbc, d_q, d_k, d_v, d_gs, d_ga, d_dtraw], axis=1)

    def dx_ep(acc, i, j, ins, outs):
        xv, wn, dx1v = ins
        d_n, dw = _rms_bwd(xv[...], wn[...], acc)
        outs[0][...] = dx1v[...] + d_n
        _accumulate(outs[1], i == 0, dw)

    grad_x, g_norm_mix_pre = _mm_call(
        "d_u", d_proj, wb['cat_t'], tm=tm, tn=D_MODEL, tk=1024, epilogue=dx_ep,
        extra_in=[(x, (tm, D_MODEL), i0), (ps['norm_mix_pre_w'], (1, D_MODEL), c0), (d_x1, (tm, D_MODEL), i0)],
        outs=[((t, D_MODEL), F32, (tm, D_MODEL), i0), ((1, D_MODEL), F32, (1, D_MODEL), c0)])
    g_cat = _mm_plain("g_w_in", u, d_proj, tm=1024, tn=1024, tk=tm, trans_a=True)
    g_w_in = jnp.concatenate([g_cat[:, :OFF_Q], g_cat[:, OFF_DT:OFF_DT + SSD_N_HEADS], g_cat[:, OFF_Q:OFF_DT]], axis=1)

    grads = {
        'norm_mix_pre_w': g_norm_mix_pre, 'w_in': g_w_in, 'ssd_conv_w': g_ssd_conv_w, 'ssd_conv_b': g_ssd_conv_b,
        'ssd_dt_bias': g_dt_bias[:, :SSD_N_HEADS], 'ssd_a_log': d_alog[:, 0, :8].reshape(1, SSD_N_HEADS),
        'ssd_d': d_dd[:, 0, :8].reshape(1, SSD_N_HEADS), 'ssd_norm_w': g_ssd_norm, 'ssd_w_out': g_w_so,
        'attn_sinks': g_sinks[:, :ATTN_N_HEADS], 'attn_w_out': g_w_ao, 'w_mix_out': g_w_mix,
        'norm_mix_post_w': g_norm_mix_post, 'norm_ffn_pre_w': g_norm_ffn_pre, 'ffn_w_up': g_w_up,
        'ffn_conv_w': g_ffn_conv_w, 'ffn_conv_b': jnp.concatenate([db_g, db_v], axis=1), 'ffn_w_down': g_w_down,
        'norm_ffn_post_w': g_norm_ffn_post,
    }
    return loss, grad_x, grads


def _rope_tables(positions):
    half = 32
    inv_freq = ROPE_THETA ** (-jnp.arange(half, dtype=F32) * 2.0 / 64)
    ang = positions.astype(F32)[:, None] * inv_freq
    cos, sin = jnp.cos(ang), jnp.sin(ang)
    return jnp.concatenate([cos, cos, cos, cos], axis=1), jnp.concatenate([-sin, sin, -sin, sin], axis=1)


def _matmul_weights(full):
    w_in = full['w_in']
    z, xbc, dtc, rest = w_in[:, :2048], w_in[:, 2048:5120], w_in[:, 5120:5152], w_in[:, 5152:]
    cat = jnp.concatenate([z, xbc, rest, dtc, jnp.zeros((D_MODEL, PROJ_W - IN_PROJ_DIM), w_in.dtype)], axis=1).astype(BF16)
    wb = {'cat': cat, 'so': full['ssd_w_out'].astype(BF16), 'ao': full['attn_w_out'].astype(BF16),
          'mix': full['w_mix_out'].astype(BF16), 'up': full['ffn_w_up'].astype(BF16), 'dn': full['ffn_w_down'].astype(BF16)}
    for k in ('cat', 'so', 'ao', 'mix', 'up', 'dn'):
        wb[k + '_t'] = wb[k].T
    return wb


ANY = pl.BlockSpec(memory_space=pl.ANY)
MESH = pl.DeviceIdType.MESH
ROW_ALIGN = 256


def _mesh_pos():
    return lax.axis_index("x"), lax.axis_index("y"), lax.axis_index("c")


def _other_chips(x, y):
    return [(1 - x, y), (x, 1 - y), (1 - x, 1 - y)]


def _remote(src, dst, send_sems, recv_sems, k, to):
    return pltpu.make_async_remote_copy(src_ref=src, dst_ref=dst, send_sem=send_sems.at[k], recv_sem=recv_sems.at[k],
                                        device_id=to, device_id_type=MESH)


def _half(c, rh):
    return pl.ds(pl.multiple_of(c * rh, 128), rh)


def _all_gather(shard):
    r = shard.shape[0]
    rh = r // 2

    def body(w_ref, out_ref, send_sems, recv_sems, local_sem):
        x, y, c = _mesh_pos()
        p = 2 * x + y
        sib = (x, y, 1 - c)
        mine, other = _half(c, rh), _half(1 - c, rh)
        chips = _other_chips(x, y)
        local = pltpu.make_async_copy(w_ref, out_ref.at[p], local_sem)
        local.start()
        first = [_remote(w_ref.at[mine], out_ref.at[p, mine], send_sems, recv_sems, j, (cx, cy, c))
                 for j, (cx, cy) in enumerate(chips)]
        for cp in first:
            cp.start()
        passed = []
        for j, (cx, cy) in enumerate(chips):
            slab = out_ref.at[2 * cx + cy, mine]
            _remote(slab, slab, send_sems, recv_sems, j, sib).wait_recv()
            fwd = _remote(slab, slab, send_sems, recv_sems, 3 + j, sib)
            fwd.start()
            passed.append(fwd)
        for j, (cx, cy) in enumerate(chips):
            slab = out_ref.at[2 * cx + cy, other]
            _remote(slab, slab, send_sems, recv_sems, 3 + j, sib).wait_recv()
        for cp in first + passed:
            cp.wait_send()
        local.wait()

    return pl.pallas_call(
        body, name="weights_all_gather", in_specs=[ANY], out_specs=ANY,
        out_shape=SDS((N_CHIPS, r, COMM_LANES), shard.dtype),
        scratch_shapes=[pltpu.SemaphoreType.DMA((6,)), pltpu.SemaphoreType.DMA((6,)), pltpu.SemaphoreType.DMA],
    )(shard)


def _pair_swap(g):
    rh = g.shape[1] // 2

    def body(g_ref, out_ref, send_sems, recv_sems):
        x, y, c = _mesh_pos()
        cp = _remote(g_ref.at[:, _half(1 - c, rh)], out_ref, send_sems, recv_sems, 0, (x, y, 1 - c))
        cp.start()
        cp.wait()

    return pl.pallas_call(
        body, name="grad_pair_swap", in_specs=[ANY], out_specs=ANY,
        out_shape=SDS((N_CHIPS, rh, COMM_LANES), g.dtype),
        scratch_shapes=[pltpu.SemaphoreType.DMA((1,)), pltpu.SemaphoreType.DMA((1,))],
    )(g)


def _chip_exchange(part):
    rh = part.shape[1]

    def body(p_ref, out_ref, send_sems, recv_sems):
        x, y, c = _mesh_pos()
        cps = [_remote(p_ref.at[2 * cx + cy], out_ref.at[j], send_sems, recv_sems, j, (cx, cy, c))
               for j, (cx, cy) in enumerate(_other_chips(x, y))]
        for cp in cps:
            cp.start()
        for cp in cps:
            cp.wait()

    return pl.pallas_call(
        body, name="grad_chip_exchange", in_specs=[ANY], out_specs=ANY,
        out_shape=SDS((3, rh, COMM_LANES), part.dtype),
        scratch_shapes=[pltpu.SemaphoreType.DMA((3,)), pltpu.SemaphoreType.DMA((3,))],
    )(part)


def _pair_gather(halfbuf):
    rh = halfbuf.shape[0]

    def body(h_ref, out_ref, send_sems, recv_sems, local_sem):
        x, y, c = _mesh_pos()
        local = pltpu.make_async_copy(h_ref, out_ref.at[c], local_sem)
        local.start()
        cp = _remote(h_ref, out_ref.at[c], send_sems, recv_sems, 0, (x, y, 1 - c))
        cp.start()
        _remote(h_ref, out_ref.at[1 - c], send_sems, recv_sems, 0, (x, y, 1 - c)).wait_recv()
        cp.wait_send()
        local.wait()

    return pl.pallas_call(
        body, name="grad_pair_gather", in_specs=[ANY], out_specs=ANY,
        out_shape=SDS((2, rh, COMM_LANES), halfbuf.dtype),
        scratch_shapes=[pltpu.SemaphoreType.DMA((1,)), pltpu.SemaphoreType.DMA((1,)), pltpu.SemaphoreType.DMA],
    )(halfbuf)


def _pair_sum(g, got, c_idx, *, tr=384):
    rh = got.shape[1]
    nb = rh // tr

    def body(c_ref, a_ref, b_ref, o_ref):
        o_ref[...] = a_ref[...] + b_ref[...]

    return pl.pallas_call(
        body, name="grad_pair_sum",
        grid_spec=pltpu.PrefetchScalarGridSpec(
            num_scalar_prefetch=1, grid=(N_CHIPS, nb),
            in_specs=[pl.BlockSpec((1, tr, COMM_LANES), lambda s, i, c_ref: (s, c_ref[0] * nb + i, 0)),
                      pl.BlockSpec((1, tr, COMM_LANES), lambda s, i, c_ref: (s, i, 0))],
            out_specs=pl.BlockSpec((1, tr, COMM_LANES), lambda s, i, c_ref: (s, i, 0))),
        out_shape=SDS(got.shape, F32), compiler_params=_cp(dimension_semantics=("arbitrary", "arbitrary")),
    )(c_idx, g, got)


def _chip_sum(part, got, p_idx, *, tr=384):
    rh = part.shape[1]

    def body(p_ref, own_ref, r0_ref, r1_ref, r2_ref, o_ref):
        p = p_ref[0]
        own, r0, r1, r2 = own_ref[0], r0_ref[0], r1_ref[0], r2_ref[0]

        def term(q):
            code = p ^ q
            return jnp.where(code == 0, own, jnp.where(code == 2, r0, jnp.where(code == 1, r1, r2)))

        o_ref[...] = ((term(0) + term(1)) + term(2)) + term(3)

    slab = lambda j: pl.BlockSpec((1, tr, COMM_LANES), lambda i, p_ref: (j, i, 0))
    return pl.pallas_call(
        body, name="grad_chip_sum",
        grid_spec=pltpu.PrefetchScalarGridSpec(
            num_scalar_prefetch=1, grid=(rh // tr,),
            in_specs=[pl.BlockSpec((1, tr, COMM_LANES), lambda i, p_ref: (p_ref[0], i, 0)), slab(0), slab(1), slab(2)],
            out_specs=pl.BlockSpec((tr, COMM_LANES), lambda i, p_ref: (i, 0))),
        out_shape=SDS((rh, COMM_LANES), F32), compiler_params=_cp(dimension_semantics=("arbitrary",)),
    )(p_idx, part, got, got, got)


def _pack_rows(pieces):
    flat = jnp.concatenate([p.reshape(-1) for p in pieces])
    r = -(-flat.shape[0] // (COMM_LANES * ROW_ALIGN)) * ROW_ALIGN
    return jnp.pad(flat, (0, r * COMM_LANES - flat.shape[0])).reshape(r, COMM_LANES)


def _take(flat, off, shape):
    n = 1
    for d in shape:
        n *= d
    return flat[off:off + n].reshape(shape), off + n


def _shard_of(g, axis, shape, s):
    return g[:, s * shape[1]:(s + 1) * shape[1]] if axis == 1 else g[s * shape[0]:(s + 1) * shape[0]]


def kernel(x, positions, norm_mix_pre_w, w_in, ssd_conv_w, ssd_conv_b, ssd_dt_bias, ssd_a_log, ssd_d, ssd_norm_w, ssd_w_out, attn_sinks, attn_w_out, w_mix_out, norm_mix_post_w, norm_ffn_pre_w, ffn_w_up, ffn_conv_w, ffn_conv_b, ffn_w_down, norm_ffn_post_w, loss_target, m_norm_mix_pre_w, m_w_in, m_ssd_conv_w, m_ssd_conv_b, m_ssd_dt_bias, m_ssd_a_log, m_ssd_d, m_ssd_norm_w, m_ssd_w_out, m_attn_sinks, m_attn_w_out, m_w_mix_out, m_norm_mix_post_w, m_norm_ffn_pre_w, m_ffn_w_up, m_ffn_conv_w, m_ffn_conv_b, m_ffn_w_down, m_norm_ffn_post_w, v_norm_mix_pre_w, v_w_in, v_ssd_conv_w, v_ssd_conv_b, v_ssd_dt_bias, v_ssd_a_log, v_ssd_d, v_ssd_norm_w, v_ssd_w_out, v_attn_sinks, v_attn_w_out, v_w_mix_out, v_norm_mix_post_w, v_norm_ffn_pre_w, v_ffn_w_up, v_ffn_conv_w, v_ffn_conv_b, v_ffn_w_down, v_norm_ffn_post_w):
    given = dict(locals())
    w = {n: given[n][0] for n in WEIGHTS}
    w = {n: (a if a.ndim == 2 else a[None]) for n, a in w.items()}
    mom_m = {n: given['m_' + n].reshape(w[n].shape) for n in WEIGHTS}
    mom_v = {n: given['v_' + n].reshape(w[n].shape) for n in WEIGHTS}
    cx, cy, cc = _mesh_pos()
    c_idx = cc.astype(jnp.int32).reshape(1)
    p_idx = (2 * cx + cy).astype(jnp.int32).reshape(1)

    pieces = [w[n].astype(BF16) for n in MATMUL_WEIGHTS]
    pieces += [lax.bitcast_convert_type(w[n], BF16) for n in ('ssd_conv_w', 'ffn_conv_w')]
    gathered = _all_gather(_pack_rows(pieces))
    shapes = {n: sh for n, _, sh in SHARDED}
    axes = {n: ax for n, ax, _ in SHARDED}
    parts = {n: [] for n in shapes}
    for s in range(N_CHIPS):
        flat, off = gathered[s].reshape(-1), 0
        for n in MATMUL_WEIGHTS:
            a, off = _take(flat, off, shapes[n])
            parts[n].append(a)
        for n in ('ssd_conv_w', 'ffn_conv_w'):
            a, off = _take(flat, off, shapes[n] + (2,))
            parts[n].append(lax.bitcast_convert_type(a, F32))
    full = {n: jnp.concatenate(parts[n], axis=axes[n]) for n in shapes}

    ps = {n: w[n] for n in REPLICATED}
    ps['ssd_conv_w'], ps['ffn_conv_w'] = full['ssd_conv_w'], full['ffn_conv_w']
    cos_t, sin_t = _rope_tables(positions[0])
    loss, grad_x, grads = _local_step(x[0], cos_t, sin_t, loss_target[0], _matmul_weights(full), ps)

    small = [grads[n] for n in REPLICATED]
    gbuf = jnp.stack([_pack_rows([_shard_of(grads[n], ax, sh, s) for n, ax, sh in SHARDED] + small)
                      for s in range(N_CHIPS)])
    pair = _pair_sum(gbuf, _pair_swap(gbuf), c_idx)
    mine = _chip_sum(pair, _chip_exchange(pair), p_idx)
    flat, off = _pair_gather(mine).reshape(-1), 0
    g_red = {}
    for n, _, sh in SHARDED:
        g_red[n], off = _take(flat, off, sh)
    for n in REPLICATED:
        g_red[n], off = _take(flat, off, w[n].shape)

    small_names = [n for n in WEIGHTS if n not in MATMUL_WEIGHTS]
    delta, new_m, new_v = {}, {}, {}
    for n in MATMUL_WEIGHTS:
        delta[n], new_m[n], new_v[n] = _adamw("adamw_" + n, w[n], g_red[n], mom_m[n], mom_v[n], tr=64)
    packed = [_pack_small([d[n] for n in small_names]) for d in (w, g_red, mom_m, mom_v)]
    outs = _adamw("adamw_small", *packed, tr=packed[0].shape[0])
    for res, o in zip((delta, new_m, new_v), outs):
        fl, off = o.reshape(-1), 0
        for n in small_names:
            res[n], off = _take(fl, off, w[n].shape)

    loss_all = lax.psum(loss[0, 0], ("x", "y", "c"))
    shaped = lambda d: [d[n].reshape(given[n].shape) for n in WEIGHTS]
    return (loss_all, grad_x[None], *shaped(g_red), *shaped(delta), *shaped(new_m), *shaped(new_v))


def _pack_small(pieces):
    flat = jnp.concatenate([p.reshape(-1) for p in pieces])
    rows = -(-flat.shape[0] // (128 * 8)) * 8
    return jnp.pad(flat, (0, rows * 128 - flat.shape[0])).reshape(rows, 128)
```

```python
import functools

import jax
import jax.numpy as jnp
from jax import lax
from jax.experimental import pallas as pl
from jax.experimental.pallas import tpu as pltpu

F32 = jnp.float32
BF16 = jnp.bfloat16
SDS = jax.ShapeDtypeStruct
HIGHEST = lax.Precision.HIGHEST

D_MODEL = 1024
SSD_D_INNER = 2048
SSD_N_HEADS = 32
SSD_HEAD_DIM = 64
SSD_N_GROUPS = 4
SSD_HEADS_PER_GROUP = 8
SSD_D_STATE = 128
SSD_CONV_DIM = 3072
CHUNK = 128
ATTN_N_HEADS = 16
KV_WIDTH = 256
FFN_D_FF = 2816
IN_PROJ_DIM = 8736
ROPE_THETA = 10000.0
NORM_EPS = 1e-6
ADAM_LR, ADAM_B1, ADAM_B2, ADAM_EPS, ADAM_WD, ADAM_STEP = 0.001, 0.9, 0.999, 1e-08, 0.01, 10

PROJ_W = 9216
OFF_Q, OFF_K, OFF_V, OFF_Z, OFF_DT, OFF_GS, OFF_GA, OFF_XBC = 0, 1024, 1280, 1536, 3584, 4096, 5120, 6144
GROUP_W = 768
PROJ_SEGS = ([(0, 2048, OFF_Z)]
             + [(2048 + 512 * g, 512, OFF_XBC + GROUP_W * g) for g in range(4)]
             + [(4096 + 128 * g, 128, OFF_XBC + GROUP_W * g + 512) for g in range(4)]
             + [(4608 + 128 * g, 128, OFF_XBC + GROUP_W * g + 640) for g in range(4)]
             + [(5120, 32, OFF_DT), (5152, 1024, OFF_Q), (6176, 256, OFF_K), (6432, 256, OFF_V),
                (6688, 1024, OFF_GS), (7712, 1024, OFF_GA)])
VMEM_LIMIT_MB = 48
NEG = -1e30

WEIGHTS = ('norm_mix_pre_w', 'w_in', 'ssd_conv_w', 'ssd_conv_b', 'ssd_dt_bias', 'ssd_a_log', 'ssd_d', 'ssd_norm_w',
           'ssd_w_out', 'attn_sinks', 'attn_w_out', 'w_mix_out', 'norm_mix_post_w', 'norm_ffn_pre_w', 'ffn_w_up',
           'ffn_conv_w', 'ffn_conv_b', 'ffn_w_down', 'norm_ffn_post_w')
SHARDED = (('w_in', 1, (1024, 2184)), ('ssd_conv_w', 1, (4, 768)), ('ssd_w_out', 0, (512, 1024)),
           ('attn_w_out', 0, (256, 1024)), ('w_mix_out', 0, (256, 1024)), ('ffn_w_up', 1, (1024, 1408)),
           ('ffn_conv_w', 1, (3, 1408)), ('ffn_w_down', 0, (704, 1024)))
MATMUL_WEIGHTS = ('w_in', 'ssd_w_out', 'attn_w_out', 'w_mix_out', 'ffn_w_up', 'ffn_w_down')
REPLICATED = tuple(n for n in WEIGHTS if n not in {s[0] for s in SHARDED})
N_CHIPS = 4
COMM_LANES = 1024


def _cp(vmem_mb=VMEM_LIMIT_MB, **kw):
    return pltpu.CompilerParams(vmem_limit_bytes=vmem_mb << 20, **kw)


def _iota(shape, axis):
    return lax.broadcasted_iota(jnp.int32, shape, axis)


def _sigmoid(v):
    return 1.0 / (1.0 + jnp.exp(-v))


def _mm_call(name, a, b, *, tm, tn, tk, epilogue, outs, extra_in=(), trans_a=False, fill=None):
    if trans_a:
        kdim, m = a.shape
    else:
        m, kdim = a.shape
    n = b.shape[1]
    assert b.shape[0] == kdim and m % tm == 0 and n % tn == 0 and kdim % tk == 0, (name, a.shape, b.shape, tm, tn, tk)
    gi, gj, gk = m // tm, n // tn, kdim // tk
    n_in, n_out = len(extra_in), len(outs)

    n_fill = 0 if fill is None else 1

    def body(a_ref, b_ref, *rest):
        ins = rest[:n_in]
        rest = rest[n_in + n_fill:]
        out_refs, scratch = rest[:n_out], rest[n_out:]
        i, j, k = pl.program_id(0), pl.program_id(1), pl.program_id(2)
        av = a_ref[...].astype(BF16)
        bv = b_ref[...].astype(BF16)
        if trans_a:
            part = lax.dot_general(av, bv, (((0,), (0,)), ((), ())), preferred_element_type=F32)
        else:
            part = jnp.dot(av, bv, preferred_element_type=F32)
        if gk == 1:
            epilogue(part, i, j, ins, out_refs)
        else:
            acc = scratch[0]

            @pl.when(k == 0)
            def _():
                acc[...] = part

            @pl.when(k > 0)
            def _():
                acc[...] += part

            @pl.when(k == gk - 1)
            def _():
                epilogue(acc[...], i, j, ins, out_refs)

    a_spec = pl.BlockSpec((tk, tm), lambda i, j, k: (k, i)) if trans_a else pl.BlockSpec((tm, tk), lambda i, j, k: (i, k))
    in_specs = [a_spec, pl.BlockSpec((tk, tn), lambda i, j, k: (k, j))]
    in_specs += [pl.BlockSpec(bs, im) for _, bs, im in extra_in]
    operands = [a, b] + [e[0] for e in extra_in]
    aliases = {}
    if fill is not None:
        in_specs.append(pl.BlockSpec(memory_space=pl.ANY))
        aliases = {len(operands): fill[1]}
        operands.append(fill[0])
    return pl.pallas_call(
        body, name=name, grid=(gi, gj, gk), in_specs=in_specs,
        out_specs=[pl.BlockSpec(bs, im) for _, _, bs, im in outs],
        out_shape=[SDS(s, d) for s, d, _, _ in outs],
        scratch_shapes=[pltpu.VMEM((tm, tn), F32)] if gk > 1 else [],
        input_output_aliases=aliases,
        compiler_params=_cp(dimension_semantics=("arbitrary", "arbitrary", "arbitrary")),
    )(*operands)


def _mm_plain(name, a, b, *, tm, tn, tk, out_dtype=F32, trans_a=False):
    m = a.shape[1] if trans_a else a.shape[0]

    def epilogue(acc, i, j, ins, outs):
        outs[0][...] = acc.astype(out_dtype)

    return _mm_call(name, a, b, tm=tm, tn=tn, tk=tk, epilogue=epilogue, trans_a=trans_a,
                    outs=[((m, b.shape[1]), out_dtype, (tm, tn), lambda i, j, k: (i, j))])[0]


def _accumulate(ref, first, value):
    @pl.when(first)
    def _():
        ref[...] = value

    @pl.when(jnp.logical_not(first))
    def _():
        ref[...] += value


def _rms_bwd(xv, w, dy):
    r = lax.rsqrt(jnp.mean(xv * xv, axis=-1, keepdims=True) + NORM_EPS)
    xn = xv * r
    dxh = dy * w
    dx = r * (dxh - xn * jnp.mean(dxh * xn, axis=-1, keepdims=True))
    return dx, jnp.sum(dy * xn, axis=0, keepdims=True)


def _norm_mm(name, x, wn, w, *, tm, tn):
    t, dm = x.shape
    n = w.shape[1]
    tm = min(tm, t)

    def body(x_ref, wn_ref, w_ref, o_ref, u_ref):
        @pl.when(pl.program_id(1) == 0)
        def _():
            xv = x_ref[...]
            r = lax.rsqrt(jnp.mean(xv * xv, axis=-1, keepdims=True) + NORM_EPS)
            u_ref[...] = (xv * r * wn_ref[...]).astype(BF16)

        o_ref[...] = jnp.dot(u_ref[...], w_ref[...], preferred_element_type=F32)

    return pl.pallas_call(
        body, name=name, grid=(t // tm, n // tn),
        in_specs=[pl.BlockSpec((tm, dm), lambda i, j: (i, 0)), pl.BlockSpec((1, dm), lambda i, j: (0, 0)),
                  pl.BlockSpec((dm, tn), lambda i, j: (0, j))],
        out_specs=[pl.BlockSpec((tm, tn), lambda i, j: (i, j)), pl.BlockSpec((tm, dm), lambda i, j: (i, 0))],
        out_shape=[SDS((t, n), F32), SDS((t, dm), BF16)],
        compiler_params=_cp(dimension_semantics=("arbitrary", "arbitrary")),
    )(x, wn, w)


def _shift_down(tile, halo, s):
    if s == 0:
        return tile
    r = pltpu.roll(tile, s, axis=0)
    h = pltpu.roll(halo, s, axis=0)
    head = jnp.where(_iota(h.shape, 0) < s, h, r[0:8])
    return jnp.concatenate([head, r[8:]], axis=0)


def _shift_up(tile, halo, s):
    if s == 0:
        return tile
    n = tile.shape[0]
    r = pltpu.roll(tile, n - s, axis=0)
    h = pltpu.roll(halo, 8 - s, axis=0)
    tail = jnp.where(_iota(h.shape, 0) >= 8 - s, h, r[n - 8:])
    return jnp.concatenate([r[:n - 8], tail], axis=0)


def _conv_apply(tile, halo, wv, bv, kw):
    acc = bv + wv[kw - 1:kw, :] * tile
    for k in range(kw - 1):
        acc = acc + wv[k:k + 1, :] * _shift_down(tile, halo, kw - 1 - k)
    return acc


def _prev_halo_spec(tm, tc, col0):
    return pl.BlockSpec((8, tc), lambda i, j: (jnp.maximum(i * (tm // 8) - 1, 0), col0 + j))


def _silu_parts(pre):
    sg = _sigmoid(pre)
    return pre * sg, sg * (1.0 + pre * (1.0 - sg))


def _conv_silu_fwd(proj, w, b, *, tm, tc=512):
    t = proj.shape[0]
    c = w.shape[1]
    tm = min(tm, t)
    col0 = OFF_XBC // tc

    def body(x_ref, h_ref, w_ref, b_ref, o_ref):
        halo = jnp.where(pl.program_id(0) > 0, h_ref[...], 0.0)
        o_ref[...] = _silu_parts(_conv_apply(x_ref[...], halo, w_ref[...], b_ref[...], 4))[0]

    return pl.pallas_call(
        body, name="ssd_conv_fwd", grid=(t // tm, c // tc),
        in_specs=[pl.BlockSpec((tm, tc), lambda i, j: (i, col0 + j)), _prev_halo_spec(tm, tc, col0),
                  pl.BlockSpec((4, tc), lambda i, j: (0, j)), pl.BlockSpec((1, tc), lambda i, j: (0, j))],
        out_specs=pl.BlockSpec((tm, tc), lambda i, j: (i, j)),
        out_shape=SDS((t, c), F32),
        compiler_params=_cp(dimension_semantics=("arbitrary", "arbitrary")),
    )(proj, proj, w, b)


def _conv_silu_bwd1(d_out, proj, w, b, *, tm, tc=512):
    t = proj.shape[0]
    c = w.shape[1]
    tm = min(tm, t)
    col0 = OFF_XBC // tc

    def body(g_ref, x_ref, h_ref, w_ref, b_ref, o_ref, db_ref):
        i = pl.program_id(1)
        halo = jnp.where(i > 0, h_ref[...], 0.0)
        d_pre = g_ref[...] * _silu_parts(_conv_apply(x_ref[...], halo, w_ref[...], b_ref[...], 4))[1]
        o_ref[...] = d_pre
        _accumulate(db_ref, i == 0, jnp.sum(d_pre, axis=0, keepdims=True))

    return pl.pallas_call(
        body, name="ssd_conv_bwd1", grid=(c // tc, t // tm),
        in_specs=[pl.BlockSpec((tm, tc), lambda j, i: (i, j)), pl.BlockSpec((tm, tc), lambda j, i: (i, col0 + j)),
                  pl.BlockSpec((8, tc), lambda j, i: (jnp.maximum(i * (tm // 8) - 1, 0), col0 + j)),
                  pl.BlockSpec((4, tc), lambda j, i: (0, j)), pl.BlockSpec((1, tc), lambda j, i: (0, j))],
        out_specs=[pl.BlockSpec((tm, tc), lambda j, i: (i, j)), pl.BlockSpec((1, tc), lambda j, i: (0, j))],
        out_shape=[SDS((t, c), F32), SDS((1, c), F32)],
        compiler_params=_cp(dimension_semantics=("arbitrary", "arbitrary")),
    )(d_out, proj, proj, w, b)


def _conv_bwd2(name, d_pre, src, src_col0, w, *, tm, tc, out_cols, out_col0, fill=None):
    t, c = d_pre.shape
    kw = w.shape[0]
    tm = min(tm, t)
    ni = t // tm
    col0 = src_col0 // tc
    ocol0 = out_col0 // tc

    def body(g_ref, gn_ref, x_ref, xh_ref, w_ref, *rest):
        o_ref, dw_ref = rest[-2:]
        i = pl.program_id(1)
        g = g_ref[...]
        g_next = jnp.where(i < ni - 1, gn_ref[...], 0.0)
        x_prev = jnp.where(i > 0, xh_ref[...], 0.0)
        xv = x_ref[...]
        wv = w_ref[...]
        d_in = wv[kw - 1:kw, :] * g
        for k in range(kw - 1):
            d_in = d_in + wv[k:k + 1, :] * _shift_up(g, g_next, kw - 1 - k)
        o_ref[...] = d_in.astype(o_ref.dtype)
        rows = [jnp.sum(g * _shift_down(xv, x_prev, kw - 1 - k), axis=0, keepdims=True) for k in range(kw)]

        @pl.when(i == 0)
        def _():
            for k in range(kw):
                dw_ref[k:k + 1, :] = rows[k]

        @pl.when(i > 0)
        def _():
            for k in range(kw):
                dw_ref[k:k + 1, :] += rows[k]

    in_specs = [pl.BlockSpec((tm, tc), lambda j, i: (i, j)),
                pl.BlockSpec((8, tc), lambda j, i: (jnp.minimum((i + 1) * (tm // 8), t // 8 - 1), j)),
                pl.BlockSpec((tm, tc), lambda j, i: (i, col0 + j)),
                pl.BlockSpec((8, tc), lambda j, i: (jnp.maximum(i * (tm // 8) - 1, 0), col0 + j)),
                pl.BlockSpec((kw, tc), lambda j, i: (0, j))]
    operands = [d_pre, d_pre, src, src, w]
    if fill is not None:
        in_specs.append(pl.BlockSpec(memory_space=pl.ANY))
        operands.append(fill)
    return pl.pallas_call(
        body, name=name, grid=(c // tc, ni), in_specs=in_specs,
        out_specs=[pl.BlockSpec((tm, tc), lambda j, i: (i, ocol0 + j)), pl.BlockSpec((kw, tc), lambda j, i: (0, j))],
        out_shape=[SDS((t, out_cols), BF16), SDS((kw, c), F32)],
        input_output_aliases={} if fill is None else {5: 0},
        compiler_params=_cp(dimension_semantics=("arbitrary", "arbitrary")),
    )(*operands)


GELU_C = 0.7978845608028654


def _gelu_parts(v):
    inner = GELU_C * (v + 0.044715 * v * v * v)
    th = jnp.tanh(inner)
    val = 0.5 * v * (1.0 + th)
    grad = 0.5 * (1.0 + th) + 0.5 * v * (1.0 - th * th) * GELU_C * (1.0 + 3.0 * 0.044715 * v * v)
    return val, grad


def _ffn_act_specs(tm, tc, nj, order):
    def im(f):
        return (lambda i, j: f(i, j)) if order == "ij" else (lambda j, i: f(i, j))
    halo = lambda i: jnp.maximum(i * (tm // 8) - 1, 0)
    return [pl.BlockSpec((tm, tc), im(lambda i, j: (i, j))), pl.BlockSpec((8, tc), im(lambda i, j: (halo(i), j))),
            pl.BlockSpec((tm, tc), im(lambda i, j: (i, nj + j))), pl.BlockSpec((8, tc), im(lambda i, j: (halo(i), nj + j))),
            pl.BlockSpec((3, tc), im(lambda i, j: (0, j))), pl.BlockSpec((3, tc), im(lambda i, j: (0, nj + j))),
            pl.BlockSpec((1, tc), im(lambda i, j: (0, j))), pl.BlockSpec((1, tc), im(lambda i, j: (0, nj + j)))]


def _ffn_act_fwd(up_raw, w, b, *, tm, tc=1408):
    t = up_raw.shape[0]
    tm = min(tm, t)
    nj = FFN_D_FF // tc

    def body(g_ref, gh_ref, v_ref, vh_ref, wg_ref, wv_ref, bg_ref, bv_ref, o_ref):
        first = pl.program_id(0) > 0
        gate = _conv_apply(g_ref[...], jnp.where(first, gh_ref[...], 0.0), wg_ref[...], bg_ref[...], 3)
        val = _conv_apply(v_ref[...], jnp.where(first, vh_ref[...], 0.0), wv_ref[...], bv_ref[...], 3)
        o_ref[...] = (_gelu_parts(gate)[0] * val).astype(BF16)

    return pl.pallas_call(
        body, name="ffn_act_fwd", grid=(t // tm, nj), in_specs=_ffn_act_specs(tm, tc, nj, "ij"),
        out_specs=pl.BlockSpec((tm, tc), lambda i, j: (i, j)), out_shape=SDS((t, FFN_D_FF), BF16),
        compiler_params=_cp(dimension_semantics=("arbitrary", "arbitrary")),
    )(up_raw, up_raw, up_raw, up_raw, w, w, b, b)


def _ffn_act_bwd(up_raw, d_act, w, b, *, tm, tc=1408):
    t = up_raw.shape[0]
    tm = min(tm, t)
    nj = FFN_D_FF // tc

    def body(g_ref, gh_ref, v_ref, vh_ref, wg_ref, wv_ref, bg_ref, bv_ref, da_ref, dg_ref, dv_ref, dbg_ref, dbv_ref):
        i = pl.program_id(1)
        gate = _conv_apply(g_ref[...], jnp.where(i > 0, gh_ref[...], 0.0), wg_ref[...], bg_ref[...], 3)
        val = _conv_apply(v_ref[...], jnp.where(i > 0, vh_ref[...], 0.0), wv_ref[...], bv_ref[...], 3)
        ge, dge = _gelu_parts(gate)
        da = da_ref[...]
        d_gate = da * val * dge
        d_val = da * ge
        dg_ref[...] = d_gate
        dv_ref[...] = d_val
        _accumulate(dbg_ref, i == 0, jnp.sum(d_gate, axis=0, keepdims=True))
        _accumulate(dbv_ref, i == 0, jnp.sum(d_val, axis=0, keepdims=True))

    tile = pl.BlockSpec((tm, tc), lambda j, i: (i, j))
    row = pl.BlockSpec((1, tc), lambda j, i: (0, j))
    return pl.pallas_call(
        body, name="ffn_act_bwd", grid=(nj, t // tm), in_specs=_ffn_act_specs(tm, tc, nj, "ji") + [tile],
        out_specs=[tile, tile, row, row],
        out_shape=[SDS((t, FFN_D_FF), F32), SDS((t, FFN_D_FF), F32), SDS((1, FFN_D_FF), F32), SDS((1, FFN_D_FF), F32)],
        compiler_params=_cp(dimension_semantics=("arbitrary", "arbitrary")),
    )(up_raw, up_raw, up_raw, up_raw, w, w, b, b, d_act)


def _softplus(v):
    e = jnp.exp(-jnp.abs(v))
    small = e * (1.0 - 0.5 * e)
    return jnp.maximum(v, 0.0) + jnp.where(e < 1e-4, small, jnp.log(1.0 + e))


def _dt_fwd(proj, bias_pad, *, tm):
    t = proj.shape[0]
    tm = min(tm, t)

    def body(x_ref, b_ref, o_ref):
        o_ref[...] = _softplus(x_ref[...] + b_ref[...])

    return pl.pallas_call(
        body, name="dt_fwd", grid=(t // tm,),
        in_specs=[pl.BlockSpec((tm, 128), lambda i: (i, OFF_DT // 128)), pl.BlockSpec((1, 128), lambda i: (0, 0))],
        out_specs=pl.BlockSpec((tm, 128), lambda i: (i, 0)), out_shape=SDS((t, 128), F32),
        compiler_params=_cp(dimension_semantics=("arbitrary",)),
    )(proj, bias_pad)


def _dt_bwd(d_dt, proj, bias_pad, d_proj, *, tm):
    t = proj.shape[0]
    tm = min(tm, t)

    def body(g_ref, x_ref, b_ref, _, o_ref, db_ref):
        d_raw = g_ref[...] * _sigmoid(x_ref[...] + b_ref[...])
        o_ref[:, 0:128] = d_raw.astype(BF16)
        o_ref[:, 128:512] = jnp.zeros((tm, 384), BF16)
        _accumulate(db_ref, pl.program_id(0) == 0, jnp.sum(d_raw, axis=0, keepdims=True))

    return pl.pallas_call(
        body, name="dt_bwd", grid=(t // tm,),
        in_specs=[pl.BlockSpec((tm, 128), lambda i: (i, 0)), pl.BlockSpec((tm, 128), lambda i: (i, OFF_DT // 128)),
                  pl.BlockSpec((1, 128), lambda i: (0, 0)), pl.BlockSpec(memory_space=pl.ANY)],
        out_specs=[pl.BlockSpec((tm, 512), lambda i: (i, OFF_DT // 512)), pl.BlockSpec((1, 128), lambda i: (0, 0))],
        out_shape=[SDS((t, PROJ_W), BF16), SDS((1, 128), F32)],
        input_output_aliases={3: 0},
        compiler_params=_cp(dimension_semantics=("arbitrary",)),
    )(d_dt, proj, bias_pad, d_proj)


def _ssd_decay(dt_ref, dtT_ref, al_ref, alT_ref):
    tril = (_iota((CHUNK, CHUNK), 0) >= _iota((CHUNK, CHUNK), 1)).astype(F32)
    dt = dt_ref[0]
    a_row = -jnp.exp(al_ref[0])
    adt = dt * a_row
    adt_t = dtT_ref[0] * (-jnp.exp(alT_ref[0]))
    acs = jnp.dot(tril, adt, precision=HIGHEST, preferred_element_type=F32)
    acs_t = lax.dot_general(adt_t, tril, (((1,), (1,)), ((), ())), precision=HIGHEST, preferred_element_type=F32)
    return dt, a_row, acs, acs_t


def _ssd_specs(nc, rev):
    ci = (lambda c: nc - 1 - c) if rev else (lambda c: c)
    return [pl.BlockSpec((CHUNK, GROUP_W), lambda c, g: (ci(c), g)),
            pl.BlockSpec((1, CHUNK, 128), lambda c, g: (g, ci(c), 0)),
            pl.BlockSpec((1, 8, CHUNK), lambda c, g: (g, 0, ci(c))),
            pl.BlockSpec((1, 1, 128), lambda c, g: (g, 0, 0)),
            pl.BlockSpec((1, 8, 1), lambda c, g: (g, 0, 0)),
            pl.BlockSpec((1, 512), lambda c, g: (0, g))]


def _ssd_fwd(xc, dtg, dtg_t, alog, alog_t, d_exp):
    t = xc.shape[0]
    nc = t // CHUNK

    def body(x_ref, dt_ref, dtT_ref, al_ref, alT_ref, d_ref, y_ref, hs_ref, h_scr):
        c, g = pl.program_id(0), pl.program_id(1)

        @pl.when(c == 0)
        def _():
            h_scr[g] = jnp.zeros((8, SSD_HEAD_DIM, SSD_D_STATE), F32)

        dt, _, acs, acs_t = _ssd_decay(dt_ref, dtT_ref, al_ref, alT_ref)
        causal = _iota((CHUNK, CHUNK), 0) >= _iota((CHUNK, CHUNK), 1)
        bb = x_ref[:, 512:640].astype(BF16)
        cb16 = x_ref[:, 640:768].astype(BF16)
        cb = lax.dot_general(cb16, bb, (((1,), (1,)), ((), ())), preferred_element_type=F32)
        a_end = acs[CHUNK - 1:CHUNK, :]
        for j in range(8):
            sl = slice(j * 64, (j + 1) * 64)
            xj = x_ref[:, sl]
            a_col = acs[:, j:j + 1]
            xdt = xj * dt[:, j:j + 1]
            lmat = jnp.where(causal, jnp.exp(jnp.minimum(a_col - acs_t[j:j + 1, :], 0.0)), 0.0)
            y_diag = jnp.dot((cb * lmat).astype(BF16), xdt.astype(BF16), preferred_element_type=F32)
            hj = h_scr[g, j]
            y_off = lax.dot_general(cb16, hj.astype(BF16), (((1,), (1,)), ((), ())), preferred_element_type=F32)
            y_ref[:, sl] = y_diag + jnp.exp(a_col) * y_off + d_ref[:, sl] * xj
            decay = jnp.exp(a_end[:, j:j + 1] - a_col)
            st = lax.dot_general((xdt * decay).astype(BF16), bb, (((0,), (0,)), ((), ())), preferred_element_type=F32)
            hs_ref[0, j] = hj
            h_scr[g, j] = hj * jnp.exp(a_end[:, j:j + 1]) + st

    return pl.pallas_call(
        body, name="ssd_fwd", grid=(nc, SSD_N_GROUPS), in_specs=_ssd_specs(nc, False),
        out_specs=[pl.BlockSpec((CHUNK, 512), lambda c, g: (c, g)),
                   pl.BlockSpec((1, 8, SSD_HEAD_DIM, SSD_D_STATE), lambda c, g: (c, g, 0, 0))],
        out_shape=[SDS((t, SSD_D_INNER), F32), SDS((nc, SSD_N_HEADS, SSD_HEAD_DIM, SSD_D_STATE), F32)],
        scratch_shapes=[pltpu.VMEM((SSD_N_GROUPS, 8, SSD_HEAD_DIM, SSD_D_STATE), F32)],
        compiler_params=_cp(dimension_semantics=("arbitrary", "arbitrary")),
    )(xc, dtg, dtg_t, alog, alog_t, d_exp)


def _ssd_bwd(xc, dtg, dtg_t, alog, alog_t, d_exp, d_y, hs):
    t = xc.shape[0]
    nc = t // CHUNK

    def body(x_ref, dt_ref, dtT_ref, al_ref, alT_ref, d_ref, dy_ref, hs_ref,
             dx_ref, ddt_ref, dal_ref, dd_ref, g_scr):
        c, g = pl.program_id(0), pl.program_id(1)

        @pl.when(c == 0)
        def _():
            g_scr[g] = jnp.zeros((8, SSD_HEAD_DIM, SSD_D_STATE), F32)

        dt, a_row, acs, acs_t = _ssd_decay(dt_ref, dtT_ref, al_ref, alT_ref)
        row, col = _iota((CHUNK, CHUNK), 0), _iota((CHUNK, CHUNK), 1)
        causal = row >= col
        triu = (row <= col).astype(F32)
        bb = x_ref[:, 512:640].astype(BF16)
        cb16 = x_ref[:, 640:768].astype(BF16)
        cb = lax.dot_general(cb16, bb, (((1,), (1,)), ((), ())), preferred_element_type=F32)
        a_end = acs[CHUNK - 1:CHUNK, :]
        lane1 = _iota((1, 128), 1)
        zero = jnp.zeros((CHUNK, CHUNK), F32)
        d_cb, d_c, d_b = zero, zero, zero
        da_col, da_row, dxx = zero, zero, zero
        dd_row = jnp.zeros((1, 128), F32)
        for j in range(8):
            sl = slice(j * 64, (j + 1) * 64)
            xj, dyj = x_ref[:, sl], dy_ref[:, sl]
            dtc = dt[:, j:j + 1]
            a_col = acs[:, j:j + 1]
            xdt = xj * dtc
            xdt16 = xdt.astype(BF16)
            lmat = jnp.where(causal, jnp.exp(jnp.minimum(a_col - acs_t[j:j + 1, :], 0.0)), 0.0)
            mmat = cb * lmat
            e_a = jnp.exp(a_col)
            e_end = jnp.exp(a_end[:, j:j + 1])
            decay = jnp.exp(a_end[:, j:j + 1] - a_col)
            hj = hs_ref[0, j]
            h16 = hj.astype(BF16)
            gj = g_scr[g, j]
            g16 = gj.astype(BF16)
            dy16 = dyj.astype(BF16)
            dye16 = (dyj * e_a).astype(BF16)
            d_m = jnp.where(causal, lax.dot_general(dy16, xdt16, (((1,), (1,)), ((), ())), preferred_element_type=F32), 0.0)
            d_x = lax.dot_general(mmat.astype(BF16), dy16, (((0,), (0,)), ((), ())), preferred_element_type=F32)
            d_seg = d_m * mmat
            d_cb = d_cb + d_m * lmat
            ch = lax.dot_general(cb16, h16, (((1,), (1,)), ((), ())), preferred_element_type=F32)
            d_a = jnp.sum(dyj * e_a * ch, axis=1, keepdims=True)
            d_c = d_c + jnp.dot(dye16, h16, preferred_element_type=F32)
            d_h = gj * e_end + lax.dot_general(dye16, cb16, (((0,), (0,)), ((), ())), preferred_element_type=F32)
            bg = lax.dot_general(bb, g16, (((1,), (1,)), ((), ())), preferred_element_type=F32)
            d_x = d_x + decay * bg
            d_decay = jnp.sum(xdt * bg, axis=1, keepdims=True) * decay
            d_b = d_b + jnp.dot((xdt * decay).astype(BF16), g16, preferred_element_type=F32)
            d_end = e_end * jnp.sum(gj * hj) + jnp.sum(d_decay)
            d_a = d_a - d_decay + jnp.sum(d_seg, axis=1, keepdims=True)
            d_a = d_a + jnp.where(_iota((CHUNK, 1), 0) == CHUNK - 1, d_end, 0.0)
            da_col = jnp.where(col == j, d_a, da_col)
            da_row = jnp.where(row == j, -jnp.sum(d_seg, axis=0, keepdims=True), da_row)
            dxx = jnp.where(col == j, jnp.sum(d_x * xj, axis=1, keepdims=True), dxx)
            dx_ref[:, sl] = d_x * dtc + d_ref[:, sl] * dyj
            dd_row = jnp.where(lane1 == j, jnp.sum(dyj * xj), dd_row)
            g_scr[g, j] = d_h
        d_cb16 = d_cb.astype(BF16)
        dx_ref[:, 640:768] = d_c + jnp.dot(d_cb16, bb, preferred_element_type=F32)
        dx_ref[:, 512:640] = d_b + lax.dot_general(d_cb16, cb16, (((0,), (0,)), ((), ())), preferred_element_type=F32)
        d_adt = (jnp.dot(triu, da_col, precision=HIGHEST, preferred_element_type=F32)
                 + lax.dot_general(triu, da_row, (((1,), (1,)), ((), ())), precision=HIGHEST, preferred_element_type=F32))
        ddt_ref[0] = d_adt * a_row + dxx
        d_alog = jnp.sum(d_adt * dt, axis=0, keepdims=True) * a_row
        first = c == 0

        @pl.when(first)
        def _():
            dal_ref[g] = d_alog
            dd_ref[g] = dd_row

        @pl.when(jnp.logical_not(first))
        def _():
            dal_ref[g] += d_alog
            dd_ref[g] += dd_row

    rc = lambda c: nc - 1 - c
    whole = pl.BlockSpec((SSD_N_GROUPS, 1, 128), lambda c, g: (0, 0, 0))
    return pl.pallas_call(
        body, name="ssd_bwd", grid=(nc, SSD_N_GROUPS),
        in_specs=_ssd_specs(nc, True) + [pl.BlockSpec((CHUNK, 512), lambda c, g: (rc(c), g)),
                                        pl.BlockSpec((1, 8, SSD_HEAD_DIM, SSD_D_STATE), lambda c, g: (rc(c), g, 0, 0))],
        out_specs=[pl.BlockSpec((CHUNK, GROUP_W), lambda c, g: (rc(c), g)),
                   pl.BlockSpec((1, CHUNK, 128), lambda c, g: (g, rc(c), 0)), whole, whole],
        out_shape=[SDS((t, SSD_CONV_DIM), F32), SDS((SSD_N_GROUPS, t, 128), F32),
                   SDS((SSD_N_GROUPS, 1, 128), F32), SDS((SSD_N_GROUPS, 1, 128), F32)],
        scratch_shapes=[pltpu.VMEM((SSD_N_GROUPS, 8, SSD_HEAD_DIM, SSD_D_STATE), F32)],
        compiler_params=_cp(dimension_semantics=("arbitrary", "arbitrary")),
    )(xc, dtg, dtg_t, alog, alog_t, d_exp, d_y, hs)


def _gated_norm_fwd(y, proj, w, *, tm):
    t = y.shape[0]
    tm = min(tm, t)

    def body(y_ref, z_ref, w_ref, o_ref):
        gv = y_ref[...] * _silu_parts(z_ref[...])[0]
        r = lax.rsqrt(jnp.mean(gv * gv, axis=-1, keepdims=True) + NORM_EPS)
        o_ref[...] = (gv * r * w_ref[...]).astype(BF16)

    tile = pl.BlockSpec((tm, 512), lambda i, g: (i, g))
    return pl.pallas_call(
        body, name="gated_norm_fwd", grid=(t // tm, SSD_N_GROUPS),
        in_specs=[tile, pl.BlockSpec((tm, 512), lambda i, g: (i, OFF_Z // 512 + g)),
                  pl.BlockSpec((1, 512), lambda i, g: (0, g))], out_specs=tile,
        out_shape=SDS((t, SSD_D_INNER), BF16),
        compiler_params=_cp(dimension_semantics=("arbitrary", "arbitrary")),
    )(y, proj, w)


def _rope(ch, cos_t, sin_t):
    first = (_iota(ch.shape, 1) & 32) == 0
    partner = jnp.where(first, pltpu.roll(ch, 96, axis=1), pltpu.roll(ch, 32, axis=1))
    return ch * cos_t + partner * sin_t


def _rope_qkv(proj, cos_t, sin_t, *, tm):
    t = proj.shape[0]
    tm = min(tm, t)

    def body(q_ref, k_ref, v_ref, c_ref, s_ref, qr_ref, kp_ref, vp_ref):
        cv, sv = c_ref[...], s_ref[...]
        lo = _iota((tm, 128), 1) < 64
        for m in range(8):
            sl = slice(m * 128, (m + 1) * 128)
            qr_ref[:, sl] = (_rope(q_ref[:, sl], cv, sv) * 0.125).astype(BF16)
        for m2 in range(2):
            sl = slice(m2 * 128, (m2 + 1) * 128)
            for src, dst in ((_rope(k_ref[:, sl], cv, sv), kp_ref), (v_ref[:, sl], vp_ref)):
                sw = pltpu.roll(src, 64, axis=1)
                base = 4 * m2 * 128
                dst[:, base:base + 128] = jnp.where(lo, src, 0.0).astype(BF16)
                dst[:, base + 128:base + 256] = jnp.where(lo, 0.0, sw).astype(BF16)
                dst[:, base + 256:base + 384] = jnp.where(lo, sw, 0.0).astype(BF16)
                dst[:, base + 384:base + 512] = jnp.where(lo, 0.0, src).astype(BF16)

    return pl.pallas_call(
        body, name="rope_qkv", grid=(t // tm,),
        in_specs=[pl.BlockSpec((tm, 1024), lambda i: (i, OFF_Q // 1024)), pl.BlockSpec((tm, 256), lambda i: (i, OFF_K // 256)),
                  pl.BlockSpec((tm, 256), lambda i: (i, OFF_V // 256)), pl.BlockSpec((tm, 128), lambda i: (i, 0)),
                  pl.BlockSpec((tm, 128), lambda i: (i, 0))],
        out_specs=[pl.BlockSpec((tm, 1024), lambda i: (i, 0))] * 3,
        out_shape=[SDS((t, 1024), BF16)] * 3,
        compiler_params=_cp(dimension_semantics=("arbitrary",)),
    )(proj, proj, proj, cos_t, sin_t)


def _attn_valid(n):
    qi, kj = _iota((CHUNK, 2 * CHUNK), 0), _iota((CHUNK, 2 * CHUNK), 1)
    return (kj > qi) & (kj <= qi + CHUNK) & ((n > 0) | (kj >= CHUNK))


def _attn_fwd(qr, kp, vp, sinks):
    t = qr.shape[0]
    nb = t // CHUNK

    def body(q_ref, kc_ref, kprev_ref, vc_ref, vprev_ref, sk_ref, o_ref, lse_ref):
        n = pl.program_id(0)
        valid = _attn_valid(n)
        lane = _iota((CHUNK, 128), 1)
        lse_all = jnp.zeros((CHUNK, 128), F32)
        for m in range(8):
            g = m // 2
            qch = q_ref[:, m * 128:(m + 1) * 128]
            o_pair = jnp.zeros((CHUNK, 128), F32)
            for e in range(2):
                h = 2 * m + e
                sl = slice((2 * g + e) * 128, (2 * g + e + 1) * 128)
                kk = jnp.concatenate([kprev_ref[:, sl], kc_ref[:, sl]], axis=0)
                vv = jnp.concatenate([vprev_ref[:, sl], vc_ref[:, sl]], axis=0)
                s = lax.dot_general(qch, kk, (((1,), (1,)), ((), ())), preferred_element_type=F32)
                s = jnp.where(valid, s, NEG)
                sink = sk_ref[0:1, h:h + 1]
                mx = jnp.maximum(jnp.max(s, axis=1, keepdims=True), sink)
                p = jnp.exp(s - mx)
                den = jnp.sum(p, axis=1, keepdims=True) + jnp.exp(sink - mx)
                o_pair = o_pair + jnp.dot((p / den).astype(BF16), vv, preferred_element_type=F32)
                lse_all = jnp.where(lane == h, mx + jnp.log(den), lse_all)
            o_ref[:, m * 128:(m + 1) * 128] = o_pair.astype(BF16)
        lse_ref[...] = lse_all

    cur = pl.BlockSpec((CHUNK, 1024), lambda n: (n, 0))
    prev = pl.BlockSpec((CHUNK, 1024), lambda n: (jnp.maximum(n - 1, 0), 0))
    return pl.pallas_call(
        body, name="attn_fwd", grid=(nb,),
        in_specs=[cur, cur, prev, cur, prev, pl.BlockSpec((1, 128), lambda n: (0, 0))],
        out_specs=[cur, pl.BlockSpec((CHUNK, 128), lambda n: (n, 0))],
        out_shape=[SDS((t, 1024), BF16), SDS((t, 128), F32)],
        compiler_params=_cp(dimension_semantics=("arbitrary",)),
    )(qr, kp, kp, vp, vp, sinks)


def _attn_bwd(qr, kp, vp, d_o, o, lse, sinks, cos_t, sin_t, d_proj):
    t = qr.shape[0]
    nb = t // CHUNK

    def body(q_ref, kc_ref, kprev_ref, vc_ref, vprev_ref, do_ref, o_ref, lse_ref, sk_ref, c_ref, s_ref, cp_ref, sp_ref,
             _, dqkv_ref, dsk_ref, acc_k, acc_v, dq_scr):
        n = pl.program_id(0)
        lane = _iota((CHUNK, 128), 1)
        lo = lane < 64
        lane1 = _iota((1, 128), 1)

        @pl.when(n == 0)
        def _():
            acc_k[...] = jnp.zeros_like(acc_k)
            acc_v[...] = jnp.zeros_like(acc_v)
            dsk_ref[...] = jnp.zeros((1, 128), F32)

        @pl.when(n > 0)
        def _():
            dqkv_ref[:, 0:1024] = dq_scr[...]
            for r in range(8):
                acc_k[r, 0:CHUNK] = acc_k[r, CHUNK:2 * CHUNK]
                acc_v[r, 0:CHUNK] = acc_v[r, CHUNK:2 * CHUNK]
                acc_k[r, CHUNK:2 * CHUNK] = jnp.zeros((CHUNK, 128), F32)
                acc_v[r, CHUNK:2 * CHUNK] = jnp.zeros((CHUNK, 128), F32)

        @pl.when(n < nb)
        def _():
            valid = _attn_valid(n)
            lse_all = lse_ref[...]
            dsk = jnp.zeros((1, 128), F32)
            for m in range(8):
                g = m // 2
                csl = slice(m * 128, (m + 1) * 128)
                qch = q_ref[:, csl]
                doch = do_ref[:, csl]
                prod = doch.astype(F32) * o_ref[:, csl].astype(F32)
                dq_pair = jnp.zeros((CHUNK, 128), F32)
                for e in range(2):
                    h = 2 * m + e
                    sl = slice((2 * g + e) * 128, (2 * g + e + 1) * 128)
                    kk = jnp.concatenate([kprev_ref[:, sl], kc_ref[:, sl]], axis=0)
                    vv = jnp.concatenate([vprev_ref[:, sl], vc_ref[:, sl]], axis=0)
                    lse_h = lse_all[:, h:h + 1]
                    s = lax.dot_general(qch, kk, (((1,), (1,)), ((), ())), preferred_element_type=F32)
                    p = jnp.exp(jnp.where(valid, s, NEG) - lse_h)
                    delta = jnp.sum(jnp.where(lo if e == 0 else jnp.logical_not(lo), prod, 0.0), axis=1, keepdims=True)
                    d_p = lax.dot_general(doch, vv, (((1,), (1,)), ((), ())), preferred_element_type=F32)
                    d_s16 = (p * (d_p - delta)).astype(BF16)
                    dq_pair = dq_pair + jnp.dot(d_s16, kk, preferred_element_type=F32)
                    acc_k[2 * g + e] += lax.dot_general(d_s16, qch, (((0,), (0,)), ((), ())), preferred_element_type=F32)
                    acc_v[2 * g + e] += lax.dot_general(p.astype(BF16), doch, (((0,), (0,)), ((), ())), preferred_element_type=F32)
                    p_sink = jnp.exp(sk_ref[0:1, h:h + 1] - lse_h)
                    dsk = jnp.where(lane1 == h, -jnp.sum(p_sink * delta), dsk)
                dq_scr[:, csl] = (_rope(dq_pair, c_ref[...], -s_ref[...]) * 0.125).astype(BF16)
            dsk_ref[...] += dsk

        @pl.when(n > 0)
        def _():
            for m2 in range(2):
                halves = []
                for g in (2 * m2, 2 * m2 + 1):
                    for acc in (acc_k, acc_v):
                        comb = jnp.where(lo, acc[2 * g, 0:CHUNK], acc[2 * g + 1, 0:CHUNK])
                        halves.append(comb + pltpu.roll(comb, 64, axis=1))
                d_kr = jnp.where(lo, halves[0], halves[2])
                d_v = jnp.where(lo, halves[1], halves[3])
                dqkv_ref[:, OFF_K + m2 * 128:OFF_K + (m2 + 1) * 128] = _rope(d_kr, cp_ref[...], -sp_ref[...]).astype(BF16)
                dqkv_ref[:, OFF_V + m2 * 128:OFF_V + (m2 + 1) * 128] = d_v.astype(BF16)

    qn = lambda n: jnp.minimum(n, nb - 1)
    pn = lambda n: jnp.maximum(jnp.minimum(n, nb) - 1, 0)
    cur = pl.BlockSpec((CHUNK, 1024), lambda n: (qn(n), 0))
    prev = pl.BlockSpec((CHUNK, 1024), lambda n: (pn(n), 0))
    cur128 = pl.BlockSpec((CHUNK, 128), lambda n: (qn(n), 0))
    prev128 = pl.BlockSpec((CHUNK, 128), lambda n: (pn(n), 0))
    one = pl.BlockSpec((1, 128), lambda n: (0, 0))
    return pl.pallas_call(
        body, name="attn_bwd", grid=(nb + 1,),
        in_specs=[cur, cur, prev, cur, prev, cur, cur, cur128, one, cur128, cur128, prev128, prev128,
                  pl.BlockSpec(memory_space=pl.ANY)],
        out_specs=[pl.BlockSpec((CHUNK, 1536), lambda n: (pn(n), 0)), one],
        out_shape=[SDS((t, PROJ_W), BF16), SDS((1, 128), F32)],
        scratch_shapes=[pltpu.VMEM((8, 2 * CHUNK, 128), F32), pltpu.VMEM((8, 2 * CHUNK, 128), F32),
                        pltpu.VMEM((CHUNK, 1024), BF16)],
        input_output_aliases={13: 0},
        compiler_params=_cp(dimension_semantics=("arbitrary",)),
    )(qr, kp, kp, vp, vp, d_o, o, lse, sinks, cos_t, sin_t, cos_t, sin_t, d_proj)


def _adamw(name, w, g, m, v, *, tr):
    rows, cols = w.shape
    tr = min(tr, rows)
    assert rows % tr == 0

    def body(w_ref, g_ref, m_ref, v_ref, d_ref, nm_ref, nv_ref):
        gv = g_ref[...]
        nm = ADAM_B1 * m_ref[...] + (1.0 - ADAM_B1) * gv
        nv = ADAM_B2 * v_ref[...] + (1.0 - ADAM_B2) * (gv * gv)
        m_hat = nm / (1.0 - ADAM_B1 ** ADAM_STEP)
        v_hat = nv / (1.0 - ADAM_B2 ** ADAM_STEP)
        d_ref[...] = -ADAM_LR * (m_hat / (jnp.sqrt(v_hat) + ADAM_EPS) + ADAM_WD * w_ref[...])
        nm_ref[...] = nm
        nv_ref[...] = nv

    tile = pl.BlockSpec((tr, cols), lambda i: (i, 0))
    return pl.pallas_call(
        body, name=name, grid=(rows // tr,), in_specs=[tile] * 4, out_specs=[tile] * 3,
        out_shape=[SDS((rows, cols), F32)] * 3, compiler_params=_cp(dimension_semantics=("arbitrary",)),
    )(w, g, m, v)


def _local_step(x, cos_t, sin_t, tgt, wb, ps):
    t = x.shape[0]
    tm = min(512, t)
    tmw = min(1024, t)
    ij = lambda i, j, k: (i, j)
    i0 = lambda i, j, k: (i, 0)
    c0 = lambda i, j, k: (0, 0)
    cj = lambda i, j, k: (0, j)

    proj, u = _norm_mm("in_proj", x, ps['norm_mix_pre_w'], wb['cat'], tm=tmw, tn=512)
    xc = _conv_silu_fwd(proj, ps['ssd_conv_w'], ps['ssd_conv_b'], tm=tm)
    bias_pad = jnp.pad(ps['ssd_dt_bias'], ((0, 0), (0, 96)))
    dt = _dt_fwd(proj, bias_pad, tm=tmw)
    dt32 = dt[:, :SSD_N_HEADS].reshape(t, SSD_N_GROUPS, 8)
    dtg = jnp.pad(dt32.transpose(1, 0, 2), ((0, 0), (0, 0), (0, 120)))
    dtg_t = dt32.transpose(1, 2, 0)
    alog = jnp.pad(ps['ssd_a_log'].reshape(SSD_N_GROUPS, 1, 8), ((0, 0), (0, 0), (0, 120)))
    alog_t = ps['ssd_a_log'].reshape(SSD_N_GROUPS, 8, 1)
    d_exp = jnp.repeat(ps['ssd_d'], SSD_HEAD_DIM, axis=1)
    y, hs = _ssd_fwd(xc, dtg, dtg_t, alog, alog_t, d_exp)
    gn = _gated_norm_fwd(y, proj, ps['ssd_norm_w'], tm=tm)
    qr, kp, vp = _rope_qkv(proj, cos_t, sin_t, tm=tm)
    sinks = jnp.pad(ps['attn_sinks'], ((0, 0), (0, 112)))
    ao, lse = _attn_fwd(qr, kp, vp, sinks)
    y_attn = _mm_plain("attn_out", ao, wb['ao'], tm=tmw, tn=512, tk=1024)

    def merge_ep(acc, i, j, ins, outs):
        gs, ga, ya = ins
        outs[0][...] = (_sigmoid(gs[...]) * acc + _sigmoid(ga[...]) * ya[...]).astype(BF16)
        outs[1][...] = acc

    merged, y_ssd = _mm_call(
        "ssd_out_merge", gn, wb['so'], tm=tmw, tn=512, tk=2048, epilogue=merge_ep,
        extra_in=[(proj, (tmw, 512), lambda i, j, k: (i, OFF_GS // 512 + j)),
                  (proj, (tmw, 512), lambda i, j, k: (i, OFF_GA // 512 + j)), (y_attn, (tmw, 512), ij)],
        outs=[((t, D_MODEL), BF16, (tmw, 512), ij), ((t, D_MODEL), F32, (tmw, 512), ij)])

    def mix_ep(acc, i, j, ins, outs):
        xv, wn = ins
        r = lax.rsqrt(jnp.mean(acc * acc, axis=-1, keepdims=True) + NORM_EPS)
        outs[0][...] = xv[...] + acc * r * wn[...]
        outs[1][...] = acc

    x1, mmix = _mm_call(
        "mix_out", merged, wb['mix'], tm=tm, tn=D_MODEL, tk=1024, epilogue=mix_ep,
        extra_in=[(x, (tm, D_MODEL), i0), (ps['norm_mix_post_w'], (1, D_MODEL), c0)],
        outs=[((t, D_MODEL), F32, (tm, D_MODEL), i0), ((t, D_MODEL), F32, (tm, D_MODEL), i0)])

    up_raw, h = _norm_mm("ffn_up", x1, ps['norm_ffn_pre_w'], wb['up'], tm=tmw, tn=512)
    act = _ffn_act_fwd(up_raw, ps['ffn_conv_w'], ps['ffn_conv_b'], tm=min(256, t))

    def loss_ep(acc, i, j, ins, outs):
        x1v, tg, wn = ins
        d_ff_ref, dout_ref, loss_ref, dw_ref = outs
        wv = wn[...]
        r = lax.rsqrt(jnp.mean(acc * acc, axis=-1, keepdims=True) + NORM_EPS)
        err = x1v[...] + acc * r * wv - tg[...]
        dout = err * (1.0 / D_MODEL)
        dout_ref[...] = dout
        d_ff, dw = _rms_bwd(acc, wv, dout)
        d_ff_ref[...] = d_ff.astype(BF16)
        _accumulate(dw_ref, i == 0, dw)
        _accumulate(loss_ref, i == 0, jnp.sum(err * err, keepdims=True) * (0.5 / D_MODEL))

    d_ff, dout, loss, g_norm_ffn_post = _mm_call(
        "ffn_down_loss", act, wb['dn'], tm=tm, tn=D_MODEL, tk=1408, epilogue=loss_ep,
        extra_in=[(x1, (tm, D_MODEL), i0), (tgt, (tm, D_MODEL), i0), (ps['norm_ffn_post_w'], (1, D_MODEL), c0)],
        outs=[((t, D_MODEL), BF16, (tm, D_MODEL), i0), ((t, D_MODEL), F32, (tm, D_MODEL), i0),
              ((1, 1), F32, (1, 1), c0), ((1, D_MODEL), F32, (1, D_MODEL), c0)])

    d_act = _mm_plain("d_act", d_ff, wb['dn_t'], tm=tmw, tn=1408, tk=1024)
    g_w_down = _mm_plain("g_w_down", act, d_ff, tm=1408, tn=1024, tk=tm, trans_a=True)
    d_gate, d_val, db_g, db_v = _ffn_act_bwd(up_raw, d_act, ps['ffn_conv_w'], ps['ffn_conv_b'], tm=min(256, t))
    d_up_raw, gcw_g = _conv_bwd2("ffn_conv_bwd2_gate", d_gate, up_raw, 0, ps['ffn_conv_w'][:, :FFN_D_FF], tm=min(256, t),
                                 tc=1408, out_cols=2 * FFN_D_FF, out_col0=0)
    d_up_raw, gcw_v = _conv_bwd2("ffn_conv_bwd2_val", d_val, up_raw, FFN_D_FF, ps['ffn_conv_w'][:, FFN_D_FF:], tm=min(256, t),
                                 tc=1408, out_cols=2 * FFN_D_FF, out_col0=FFN_D_FF, fill=d_up_raw)
    g_ffn_conv_w = jnp.concatenate([gcw_g, gcw_v], axis=1)

    def dx1_ep(acc, i, j, ins, outs):
        x1v, wpre, dout_v, mmv, wpost = ins
        d_x1_ref, d_mm_ref, dwpre_ref, dwpost_ref = outs
        d_n, dw_pre = _rms_bwd(x1v[...], wpre[...], acc)
        d_x1 = dout_v[...] + d_n
        d_x1_ref[...] = d_x1
        d_mm, dw_post = _rms_bwd(mmv[...], wpost[...], d_x1)
        d_mm_ref[...] = d_mm.astype(BF16)
        _accumulate(dwpre_ref, i == 0, dw_pre)
        _accumulate(dwpost_ref, i == 0, dw_post)

    d_x1, d_mm, g_norm_ffn_pre, g_norm_mix_post = _mm_call(
        "d_h", d_up_raw, wb['up_t'], tm=tm, tn=D_MODEL, tk=1408, epilogue=dx1_ep,
        extra_in=[(x1, (tm, D_MODEL), i0), (ps['norm_ffn_pre_w'], (1, D_MODEL), c0), (dout, (tm, D_MODEL), i0),
                  (mmix, (tm, D_MODEL), i0), (ps['norm_mix_post_w'], (1, D_MODEL), c0)],
        outs=[((t, D_MODEL), F32, (tm, D_MODEL), i0), ((t, D_MODEL), BF16, (tm, D_MODEL), i0),
              ((1, D_MODEL), F32, (1, D_MODEL), c0), ((1, D_MODEL), F32, (1, D_MODEL), c0)])
    g_w_up_t = _mm_plain("g_w_up", d_up_raw, h, tm=512, tn=1024, tk=tm, trans_a=True)

    def dmerge_ep(acc, i, j, ins, outs):
        gs, ga, ys, ya = ins
        sg_s, sg_a = _sigmoid(gs[...]), _sigmoid(ga[...])
        outs[0][...] = (acc * sg_s).astype(BF16)
        outs[1][...] = (acc * sg_a).astype(BF16)
        outs[2][:, 0:D_MODEL] = (acc * ys[...] * sg_s * (1.0 - sg_s)).astype(BF16)
        outs[2][:, D_MODEL:2 * D_MODEL] = (acc * ya[...] * sg_a * (1.0 - sg_a)).astype(BF16)

    d_yssd, d_yattn, d_proj = _mm_call(
        "d_merged", d_mm, wb['mix_t'], tm=tm, tn=D_MODEL, tk=1024, epilogue=dmerge_ep,
        extra_in=[(proj, (tm, D_MODEL), lambda i, j, k: (i, OFF_GS // D_MODEL)),
                  (proj, (tm, D_MODEL), lambda i, j, k: (i, OFF_GA // D_MODEL)), (y_ssd, (tm, D_MODEL), i0), (y_attn, (tm, D_MODEL), i0)],
        outs=[((t, D_MODEL), BF16, (tm, D_MODEL), i0), ((t, D_MODEL), BF16, (tm, D_MODEL), i0),
              ((t, PROJ_W), BF16, (tm, 2 * D_MODEL), lambda i, j, k: (i, OFF_GS // (2 * D_MODEL)))])
    g_w_mix = _mm_plain("g_w_mix", merged, d_mm, tm=1024, tn=512, tk=tm, trans_a=True)

    def dgn_ep(acc, i, j, ins, outs):
        yv, zv, wn = ins
        d_y_ref, d_z_ref, dw_ref = outs
        zz = zv[...]
        sz = _sigmoid(zz)
        silu = zz * sz
        gv = yv[...] * silu
        r = lax.rsqrt(jnp.mean(gv * gv, axis=-1, keepdims=True) + NORM_EPS)
        gh = gv * r
        dgh = acc * wn[...]
        dg = r * (dgh - gh * jnp.mean(dgh * gh, axis=-1, keepdims=True))
        d_y_ref[...] = dg * silu
        d_z_ref[...] = (dg * yv[...] * (sz * (1.0 + zz * (1.0 - sz)))).astype(BF16)
        dw = jnp.sum(acc * gh, axis=0, keepdims=True)

        @pl.when(i == 0)
        def _():
            dw_ref[j] = dw

        @pl.when(i > 0)
        def _():
            dw_ref[j] += dw

    d_y, d_proj, g_ssd_norm = _mm_call(
        "d_gn", d_yssd, wb['so_t'], tm=tm, tn=512, tk=1024, epilogue=dgn_ep, fill=(d_proj, 1),
        extra_in=[(y, (tm, 512), ij), (proj, (tm, 512), lambda i, j, k: (i, OFF_Z // 512 + j)), (ps['ssd_norm_w'], (1, 512), cj)],
        outs=[((t, SSD_D_INNER), F32, (tm, 512), ij), ((t, PROJ_W), BF16, (tm, 512), lambda i, j, k: (i, OFF_Z // 512 + j)),
              ((SSD_N_GROUPS, 1, 512), F32, (SSD_N_GROUPS, 1, 512), lambda i, j, k: (0, 0, 0))])
    g_ssd_norm = g_ssd_norm.reshape(1, SSD_D_INNER)
    g_w_so = _mm_plain("g_w_so", gn, d_yssd, tm=1024, tn=512, tk=tm, trans_a=True)
    d_xc, d_dtg, d_alog, d_dd = _ssd_bwd(xc, dtg, dtg_t, alog, alog_t, d_exp, d_y, hs)
    d_pre, g_ssd_conv_b = _conv_silu_bwd1(d_xc, proj, ps['ssd_conv_w'], ps['ssd_conv_b'], tm=tm)
    d_proj, g_ssd_conv_w = _conv_bwd2("ssd_conv_bwd2", d_pre, proj, OFF_XBC, ps['ssd_conv_w'], tm=tm, tc=512,
                                      out_cols=PROJ_W, out_col0=OFF_XBC, fill=d_proj)
    d_dt = jnp.pad(d_dtg[:, :, :8].transpose(1, 0, 2).reshape(t, SSD_N_HEADS), ((0, 0), (0, 96)))
    d_proj, g_dt_bias = _dt_bwd(d_dt, proj, bias_pad, d_proj, tm=tmw)

    d_ao = _mm_plain("d_ao", d_yattn, wb['ao_t'], tm=tmw, tn=512, tk=1024, out_dtype=BF16)
    g_w_ao = _mm_plain("g_w_ao", ao, d_yattn, tm=1024, tn=512, tk=tm, trans_a=True)
    d_proj, g_sinks = _attn_bwd(qr, kp, vp, d_ao, ao, lse, sinks, cos_t, sin_t, d_proj)

    def dx_ep(acc, i, j, ins, outs):
        xv, wn, dx1v = ins
        d_n, dw = _rms_bwd(xv[...], wn[...], acc)
        outs[0][...] = dx1v[...] + d_n
        _accumulate(outs[1], i == 0, dw)

    grad_x, g_norm_mix_pre = _mm_call(
        "d_u", d_proj, wb['cat_t'], tm=tm, tn=D_MODEL, tk=1024, epilogue=dx_ep,
        extra_in=[(x, (tm, D_MODEL), i0), (ps['norm_mix_pre_w'], (1, D_MODEL), c0), (d_x1, (tm, D_MODEL), i0)],
        outs=[((t, D_MODEL), F32, (tm, D_MODEL), i0), ((1, D_MODEL), F32, (1, D_MODEL), c0)])
    g_cat_t = _mm_plain("g_w_in", d_proj, u, tm=1024, tn=1024, tk=tm, trans_a=True)

    grads = {
        'norm_mix_pre_w': g_norm_mix_pre, 'w_in': g_cat_t, 'ssd_conv_w': g_ssd_conv_w, 'ssd_conv_b': g_ssd_conv_b,
        'ssd_dt_bias': g_dt_bias[:, :SSD_N_HEADS], 'ssd_a_log': d_alog[:, 0, :8].reshape(1, SSD_N_HEADS),
        'ssd_d': d_dd[:, 0, :8].reshape(1, SSD_N_HEADS), 'ssd_norm_w': g_ssd_norm, 'ssd_w_out': g_w_so,
        'attn_sinks': g_sinks[:, :ATTN_N_HEADS], 'attn_w_out': g_w_ao, 'w_mix_out': g_w_mix,
        'norm_mix_post_w': g_norm_mix_post, 'norm_ffn_pre_w': g_norm_ffn_pre, 'ffn_w_up': g_w_up_t,
        'ffn_conv_w': g_ffn_conv_w, 'ffn_conv_b': jnp.concatenate([db_g, db_v], axis=1), 'ffn_w_down': g_w_down,
        'norm_ffn_post_w': g_norm_ffn_post,
    }
    return loss, grad_x, grads


def _group_channels(a):
    parts = []
    for g in range(SSD_N_GROUPS):
        parts += [a[..., 512 * g:512 * (g + 1)], a[..., 2048 + 128 * g:2048 + 128 * (g + 1)],
                  a[..., 2560 + 128 * g:2560 + 128 * (g + 1)]]
    return jnp.concatenate(parts, axis=-1)


def _ungroup_channels(a):
    xs = [a[..., GROUP_W * g:GROUP_W * g + 512] for g in range(SSD_N_GROUPS)]
    bs = [a[..., GROUP_W * g + 512:GROUP_W * g + 640] for g in range(SSD_N_GROUPS)]
    cs = [a[..., GROUP_W * g + 640:GROUP_W * (g + 1)] for g in range(SSD_N_GROUPS)]
    return jnp.concatenate(xs + bs + cs, axis=-1)


def _proj_rows(a_t, lo, hi):
    out = []
    for start, length, dst in sorted(PROJ_SEGS):
        s, e = max(lo, start), min(hi, start + length)
        if s < e:
            out.append(a_t[dst + s - start:dst + e - start])
    return out


def _to_proj_layout(w_in_t):
    pieces, pos = [], 0
    for start, length, dst in sorted(PROJ_SEGS, key=lambda s: s[2]):
        if dst > pos:
            pieces.append(jnp.zeros((dst - pos, w_in_t.shape[1]), w_in_t.dtype))
        pieces.append(w_in_t[start:start + length])
        pos = dst + length
    if pos < PROJ_W:
        pieces.append(jnp.zeros((PROJ_W - pos, w_in_t.shape[1]), w_in_t.dtype))
    return jnp.concatenate(pieces, axis=0)


def _rope_tables(positions):
    half = 32
    inv_freq = ROPE_THETA ** (-jnp.arange(half, dtype=F32) * 2.0 / 64)
    ang = positions.astype(F32)[:, None] * inv_freq
    cos, sin = jnp.cos(ang), jnp.sin(ang)
    return jnp.concatenate([cos, cos, cos, cos], axis=1), jnp.concatenate([-sin, sin, -sin, sin], axis=1)


def _matmul_weights(w_in_t, so, ao, mix, up_t, dn):
    cat_t = _to_proj_layout(w_in_t)
    return {'cat': cat_t.T, 'cat_t': cat_t, 'so': so, 'so_t': so.T, 'ao': ao, 'ao_t': ao.T, 'mix': mix, 'mix_t': mix.T,
            'up': up_t.T, 'up_t': up_t, 'dn': dn, 'dn_t': dn.T}


ANY = pl.BlockSpec(memory_space=pl.ANY)
MESH = pl.DeviceIdType.MESH
ROW_ALIGN = 256


def _mesh_pos():
    return lax.axis_index("x"), lax.axis_index("y"), lax.axis_index("c")


def _other_chips(x, y):
    return [(1 - x, y), (x, 1 - y), (1 - x, 1 - y)]


def _remote(src, dst, send_sems, recv_sems, k, to):
    return pltpu.make_async_remote_copy(src_ref=src, dst_ref=dst, send_sem=send_sems.at[k], recv_sem=recv_sems.at[k],
                                        device_id=to, device_id_type=MESH)


def _half(c, rh):
    return pl.ds(pl.multiple_of(c * rh, 128), rh)


def _all_gather(shard):
    r = shard.shape[0]
    rh = r // 2

    def body(w_ref, out_ref, send_sems, recv_sems, local_sem):
        x, y, c = _mesh_pos()
        p = 2 * x + y
        sib = (x, y, 1 - c)
        mine, other = _half(c, rh), _half(1 - c, rh)
        chips = _other_chips(x, y)
        local = pltpu.make_async_copy(w_ref, out_ref.at[p], local_sem)
        local.start()
        first = [_remote(w_ref.at[mine], out_ref.at[p, mine], send_sems, recv_sems, j, (cx, cy, c))
                 for j, (cx, cy) in enumerate(chips)]
        for cp in first:
            cp.start()
        passed = []
        for j, (cx, cy) in enumerate(chips):
            slab = out_ref.at[2 * cx + cy, mine]
            _remote(slab, slab, send_sems, recv_sems, j, sib).wait_recv()
            fwd = _remote(slab, slab, send_sems, recv_sems, 3 + j, sib)
            fwd.start()
            passed.append(fwd)
        for j, (cx, cy) in enumerate(chips):
            slab = out_ref.at[2 * cx + cy, other]
            _remote(slab, slab, send_sems, recv_sems, 3 + j, sib).wait_recv()
        for cp in first + passed:
            cp.wait_send()
        local.wait()

    return pl.pallas_call(
        body, name="weights_all_gather", in_specs=[ANY], out_specs=ANY,
        out_shape=SDS((N_CHIPS, r, COMM_LANES), shard.dtype),
        scratch_shapes=[pltpu.SemaphoreType.DMA((6,)), pltpu.SemaphoreType.DMA((6,)), pltpu.SemaphoreType.DMA],
    )(shard)


def _pair_swap(g):
    rh = g.shape[1] // 2

    def body(g_ref, out_ref, send_sems, recv_sems):
        x, y, c = _mesh_pos()
        cp = _remote(g_ref.at[:, _half(1 - c, rh)], out_ref, send_sems, recv_sems, 0, (x, y, 1 - c))
        cp.start()
        cp.wait()

    return pl.pallas_call(
        body, name="grad_pair_swap", in_specs=[ANY], out_specs=ANY,
        out_shape=SDS((N_CHIPS, rh, COMM_LANES), g.dtype),
        scratch_shapes=[pltpu.SemaphoreType.DMA((1,)), pltpu.SemaphoreType.DMA((1,))],
    )(g)


def _chip_exchange(part):
    rh = part.shape[1]

    def body(p_ref, out_ref, send_sems, recv_sems):
        x, y, c = _mesh_pos()
        cps = [_remote(p_ref.at[2 * cx + cy], out_ref.at[j], send_sems, recv_sems, j, (cx, cy, c))
               for j, (cx, cy) in enumerate(_other_chips(x, y))]
        for cp in cps:
            cp.start()
        for cp in cps:
            cp.wait()

    return pl.pallas_call(
        body, name="grad_chip_exchange", in_specs=[ANY], out_specs=ANY,
        out_shape=SDS((3, rh, COMM_LANES), part.dtype),
        scratch_shapes=[pltpu.SemaphoreType.DMA((3,)), pltpu.SemaphoreType.DMA((3,))],
    )(part)


def _pair_gather(halfbuf):
    rh = halfbuf.shape[0]

    def body(h_ref, out_ref, send_sems, recv_sems, local_sem):
        x, y, c = _mesh_pos()
        local = pltpu.make_async_copy(h_ref, out_ref.at[c], local_sem)
        local.start()
        cp = _remote(h_ref, out_ref.at[c], send_sems, recv_sems, 0, (x, y, 1 - c))
        cp.start()
        _remote(h_ref, out_ref.at[1 - c], send_sems, recv_sems, 0, (x, y, 1 - c)).wait_recv()
        cp.wait_send()
        local.wait()

    return pl.pallas_call(
        body, name="grad_pair_gather", in_specs=[ANY], out_specs=ANY,
        out_shape=SDS((2, rh, COMM_LANES), halfbuf.dtype),
        scratch_shapes=[pltpu.SemaphoreType.DMA((1,)), pltpu.SemaphoreType.DMA((1,)), pltpu.SemaphoreType.DMA],
    )(halfbuf)


def _pair_sum(g, got, c_idx, *, tr=384):
    rh = got.shape[1]
    nb = rh // tr

    def body(c_ref, a_ref, b_ref, o_ref):
        o_ref[...] = a_ref[...] + b_ref[...]

    return pl.pallas_call(
        body, name="grad_pair_sum",
        grid_spec=pltpu.PrefetchScalarGridSpec(
            num_scalar_prefetch=1, grid=(N_CHIPS, nb),
            in_specs=[pl.BlockSpec((1, tr, COMM_LANES), lambda s, i, c_ref: (s, c_ref[0] * nb + i, 0)),
                      pl.BlockSpec((1, tr, COMM_LANES), lambda s, i, c_ref: (s, i, 0))],
            out_specs=pl.BlockSpec((1, tr, COMM_LANES), lambda s, i, c_ref: (s, i, 0))),
        out_shape=SDS(got.shape, F32), compiler_params=_cp(dimension_semantics=("arbitrary", "arbitrary")),
    )(c_idx, g, got)


def _chip_sum(part, got, p_idx, *, tr=384):
    rh = part.shape[1]

    def body(p_ref, own_ref, r0_ref, r1_ref, r2_ref, o_ref):
        p = p_ref[0]
        own, r0, r1, r2 = own_ref[0], r0_ref[0], r1_ref[0], r2_ref[0]

        def term(q):
            code = p ^ q
            return jnp.where(code == 0, own, jnp.where(code == 2, r0, jnp.where(code == 1, r1, r2)))

        o_ref[...] = ((term(0) + term(1)) + term(2)) + term(3)

    slab = lambda j: pl.BlockSpec((1, tr, COMM_LANES), lambda i, p_ref: (j, i, 0))
    return pl.pallas_call(
        body, name="grad_chip_sum",
        grid_spec=pltpu.PrefetchScalarGridSpec(
            num_scalar_prefetch=1, grid=(rh // tr,),
            in_specs=[pl.BlockSpec((1, tr, COMM_LANES), lambda i, p_ref: (p_ref[0], i, 0)), slab(0), slab(1), slab(2)],
            out_specs=pl.BlockSpec((tr, COMM_LANES), lambda i, p_ref: (i, 0))),
        out_shape=SDS((rh, COMM_LANES), F32), compiler_params=_cp(dimension_semantics=("arbitrary",)),
    )(p_idx, part, got, got, got)


def _pack_rows(big, small):
    flat = jnp.concatenate([p.reshape(-1) for p in small])
    k = -(-flat.shape[0] // COMM_LANES)
    tail = jnp.pad(flat, (0, k * COMM_LANES - flat.shape[0])).reshape(k, COMM_LANES)
    rows = sum(p.shape[0] for p in big) + k
    pad = -rows % ROW_ALIGN
    return jnp.concatenate(list(big) + [tail, jnp.zeros((pad, COMM_LANES), tail.dtype)], axis=0)


def _take(flat, off, shape):
    n = 1
    for d in shape:
        n *= d
    return flat[off:off + n].reshape(shape), off + n


BIG_ROWS = (('w_in', 2184), ('ssd_w_out', 512), ('attn_w_out', 256), ('w_mix_out', 256), ('ffn_w_up', 1408), ('ffn_w_down', 704))
TRANSPOSED = ('w_in', 'ffn_w_up')


def kernel(x, positions, norm_mix_pre_w, w_in, ssd_conv_w, ssd_conv_b, ssd_dt_bias, ssd_a_log, ssd_d, ssd_norm_w, ssd_w_out, attn_sinks, attn_w_out, w_mix_out, norm_mix_post_w, norm_ffn_pre_w, ffn_w_up, ffn_conv_w, ffn_conv_b, ffn_w_down, norm_ffn_post_w, loss_target, m_norm_mix_pre_w, m_w_in, m_ssd_conv_w, m_ssd_conv_b, m_ssd_dt_bias, m_ssd_a_log, m_ssd_d, m_ssd_norm_w, m_ssd_w_out, m_attn_sinks, m_attn_w_out, m_w_mix_out, m_norm_mix_post_w, m_norm_ffn_pre_w, m_ffn_w_up, m_ffn_conv_w, m_ffn_conv_b, m_ffn_w_down, m_norm_ffn_post_w, v_norm_mix_pre_w, v_w_in, v_ssd_conv_w, v_ssd_conv_b, v_ssd_dt_bias, v_ssd_a_log, v_ssd_d, v_ssd_norm_w, v_ssd_w_out, v_attn_sinks, v_attn_w_out, v_w_mix_out, v_norm_mix_post_w, v_norm_ffn_pre_w, v_ffn_w_up, v_ffn_conv_w, v_ffn_conv_b, v_ffn_w_down, v_norm_ffn_post_w):
    given = dict(locals())
    w = {n: given[n][0] for n in WEIGHTS}
    w = {n: (a if a.ndim == 2 else a[None]) for n, a in w.items()}
    mom_m = {n: given['m_' + n].reshape(w[n].shape) for n in WEIGHTS}
    mom_v = {n: given['v_' + n].reshape(w[n].shape) for n in WEIGHTS}
    cx, cy, cc = _mesh_pos()
    c_idx = cc.astype(jnp.int32).reshape(1)
    p_idx = (2 * cx + cy).astype(jnp.int32).reshape(1)

    big = [(w[n].T if n in TRANSPOSED else w[n]).astype(BF16) for n, _ in BIG_ROWS]
    taps = [lax.bitcast_convert_type(w[n], BF16) for n in ('ssd_conv_w', 'ffn_conv_w')]
    gathered = _all_gather(_pack_rows(big, taps))
    rows = {n: [] for n, _ in BIG_ROWS}
    conv = {'ssd_conv_w': [], 'ffn_conv_w': []}
    for s in range(N_CHIPS):
        r0 = 0
        for n, nr in BIG_ROWS:
            rows[n].append(gathered[s, r0:r0 + nr])
            r0 += nr
        flat, off = gathered[s, r0:r0 + 16].reshape(-1), 0
        for n in conv:
            a, off = _take(flat, off, w[n].shape + (2,))
            conv[n].append(lax.bitcast_convert_type(a, F32))
    wfull = {n: jnp.concatenate(rows[n], axis=0) for n in rows}
    wb = _matmul_weights(wfull['w_in'], wfull['ssd_w_out'], wfull['attn_w_out'], wfull['w_mix_out'], wfull['ffn_w_up'],
                         wfull['ffn_w_down'])

    ps = {n: w[n] for n in REPLICATED}
    ps['ssd_conv_w'] = _group_channels(jnp.concatenate(conv['ssd_conv_w'], axis=1))
    ps['ssd_conv_b'] = _group_channels(w['ssd_conv_b'])
    ps['ffn_conv_w'] = jnp.concatenate(conv['ffn_conv_w'], axis=1)
    cos_t, sin_t = _rope_tables(positions[0])
    loss, grad_x, grads = _local_step(x[0], cos_t, sin_t, loss_target[0], wb, ps)
    grads['ssd_conv_w'] = _ungroup_channels(grads['ssd_conv_w'])
    grads['ssd_conv_b'] = _ungroup_channels(grads['ssd_conv_b'])

    shard_shapes = {n: sh for n, _, sh in SHARDED}
    slabs = []
    for s in range(N_CHIPS):
        bigs = []
        for n, nr in BIG_ROWS:
            bigs += _proj_rows(grads[n], nr * s, nr * (s + 1)) if n == 'w_in' else [grads[n][nr * s:nr * (s + 1)]]
        small = [grads[n][:, shard_shapes[n][1] * s:shard_shapes[n][1] * (s + 1)] for n in ('ssd_conv_w', 'ffn_conv_w')]
        slabs.append(_pack_rows(bigs, small + [grads[n] for n in REPLICATED]))
    gbuf = jnp.stack(slabs)
    pair = _pair_sum(gbuf, _pair_swap(gbuf), c_idx)
    mine = _chip_sum(pair, _chip_exchange(pair), p_idx)
    red = _pair_gather(mine).reshape(gbuf.shape[1], COMM_LANES)
    g_red, r0 = {}, 0
    for n, nr in BIG_ROWS:
        g_red[n] = red[r0:r0 + nr].T if n in TRANSPOSED else red[r0:r0 + nr]
        r0 += nr
    flat, off = red[r0:].reshape(-1), 0
    for n in ('ssd_conv_w', 'ffn_conv_w') + REPLICATED:
        g_red[n], off = _take(flat, off, w[n].shape)

    small_names = [n for n in WEIGHTS if n not in MATMUL_WEIGHTS]
    delta, new_m, new_v = {}, {}, {}
    for n in MATMUL_WEIGHTS:
        delta[n], new_m[n], new_v[n] = _adamw("adamw_" + n, w[n], g_red[n], mom_m[n], mom_v[n], tr=64)
    packed = [_pack_small([d[n] for n in small_names]) for d in (w, g_red, mom_m, mom_v)]
    outs = _adamw("adamw_small", *packed, tr=packed[0].shape[0])
    for res, o in zip((delta, new_m, new_v), outs):
        fl, off = o.reshape(-1), 0
        for n in small_names:
            res[n], off = _take(fl, off, w[n].shape)

    loss_all = lax.psum(loss[0, 0], ("x", "y", "c"))
    shaped = lambda d: [d[n].reshape(given[n].shape) for n in WEIGHTS]
    return (loss_all, grad_x[None], *shaped(g_red), *shaped(delta), *shaped(new_m), *shaped(new_v))


def _pack_small(pieces):
    flat = jnp.concatenate([p.reshape(-1) for p in pieces])
    rows = -(-flat.shape[0] // (128 * 8)) * 8
    return jnp.pad(flat, (0, rows * 128 - flat.shape[0])).reshape(rows, 128)
```

```python
import functools

import jax
import jax.numpy as jnp
from jax import lax
from jax.experimental import pallas as pl
from jax.experimental.pallas import tpu as pltpu

F32 = jnp.float32
BF16 = jnp.bfloat16
SDS = jax.ShapeDtypeStruct
HIGHEST = lax.Precision.HIGHEST

D_MODEL = 1024
SSD_D_INNER = 2048
SSD_N_HEADS = 32
SSD_HEAD_DIM = 64
SSD_N_GROUPS = 4
SSD_HEADS_PER_GROUP = 8
SSD_D_STATE = 128
SSD_CONV_DIM = 3072
CHUNK = 128
ATTN_N_HEADS = 16
KV_WIDTH = 256
FFN_D_FF = 2816
IN_PROJ_DIM = 8736
ROPE_THETA = 10000.0
NORM_EPS = 1e-6
ADAM_LR, ADAM_B1, ADAM_B2, ADAM_EPS, ADAM_WD, ADAM_STEP = 0.001, 0.9, 0.999, 1e-08, 0.01, 10

PROJ_W = 9216
OFF_Q, OFF_K, OFF_V, OFF_Z, OFF_DT, OFF_GS, OFF_GA, OFF_XBC = 0, 1024, 1280, 1536, 3584, 4096, 5120, 6144
GROUP_W = 768
PROJ_SEGS = ([(0, 2048, OFF_Z)]
             + [(2048 + 512 * g, 512, OFF_XBC + GROUP_W * g) for g in range(4)]
             + [(4096 + 128 * g, 128, OFF_XBC + GROUP_W * g + 512) for g in range(4)]
             + [(4608 + 128 * g, 128, OFF_XBC + GROUP_W * g + 640) for g in range(4)]
             + [(5120, 32, OFF_DT), (5152, 1024, OFF_Q), (6176, 256, OFF_K), (6432, 256, OFF_V),
                (6688, 1024, OFF_GS), (7712, 1024, OFF_GA)])
VMEM_LIMIT_MB = 48
NEG = -1e30

WEIGHTS = ('norm_mix_pre_w', 'w_in', 'ssd_conv_w', 'ssd_conv_b', 'ssd_dt_bias', 'ssd_a_log', 'ssd_d', 'ssd_norm_w',
           'ssd_w_out', 'attn_sinks', 'attn_w_out', 'w_mix_out', 'norm_mix_post_w', 'norm_ffn_pre_w', 'ffn_w_up',
           'ffn_conv_w', 'ffn_conv_b', 'ffn_w_down', 'norm_ffn_post_w')
SHARDED = (('w_in', 1, (1024, 2184)), ('ssd_conv_w', 1, (4, 768)), ('ssd_w_out', 0, (512, 1024)),
           ('attn_w_out', 0, (256, 1024)), ('w_mix_out', 0, (256, 1024)), ('ffn_w_up', 1, (1024, 1408)),
           ('ffn_conv_w', 1, (3, 1408)), ('ffn_w_down', 0, (704, 1024)))
MATMUL_WEIGHTS = ('w_in', 'ssd_w_out', 'attn_w_out', 'w_mix_out', 'ffn_w_up', 'ffn_w_down')
REPLICATED = tuple(n for n in WEIGHTS if n not in {s[0] for s in SHARDED})
N_CHIPS = 4
COMM_LANES = 1024


def _cp(vmem_mb=VMEM_LIMIT_MB, **kw):
    return pltpu.CompilerParams(vmem_limit_bytes=vmem_mb << 20, **kw)


def _iota(shape, axis):
    return lax.broadcasted_iota(jnp.int32, shape, axis)


def _sigmoid(v):
    return 1.0 / (1.0 + jnp.exp(-v))


def _mm_call(name, a, b, *, tm, tn, tk, epilogue, outs, extra_in=(), trans_a=False, fill=None):
    if trans_a:
        kdim, m = a.shape
    else:
        m, kdim = a.shape
    n = b.shape[1]
    assert b.shape[0] == kdim and m % tm == 0 and n % tn == 0 and kdim % tk == 0, (name, a.shape, b.shape, tm, tn, tk)
    gi, gj, gk = m // tm, n // tn, kdim // tk
    n_in, n_out = len(extra_in), len(outs)

    n_fill = 0 if fill is None else 1

    def body(a_ref, b_ref, *rest):
        ins = rest[:n_in]
        rest = rest[n_in + n_fill:]
        out_refs, scratch = rest[:n_out], rest[n_out:]
        i, j, k = pl.program_id(0), pl.program_id(1), pl.program_id(2)
        av = a_ref[...].astype(BF16)
        bv = b_ref[...].astype(BF16)
        if trans_a:
            part = lax.dot_general(av, bv, (((0,), (0,)), ((), ())), preferred_element_type=F32)
        else:
            part = jnp.dot(av, bv, preferred_element_type=F32)
        if gk == 1:
            epilogue(part, i, j, ins, out_refs)
        else:
            acc = scratch[0]

            @pl.when(k == 0)
            def _():
                acc[...] = part

            @pl.when(k > 0)
            def _():
                acc[...] += part

            @pl.when(k == gk - 1)
            def _():
                epilogue(acc[...], i, j, ins, out_refs)

    a_spec = pl.BlockSpec((tk, tm), lambda i, j, k: (k, i)) if trans_a else pl.BlockSpec((tm, tk), lambda i, j, k: (i, k))
    in_specs = [a_spec, pl.BlockSpec((tk, tn), lambda i, j, k: (k, j))]
    in_specs += [pl.BlockSpec(bs, im) for _, bs, im in extra_in]
    operands = [a, b] + [e[0] for e in extra_in]
    aliases = {}
    if fill is not None:
        in_specs.append(pl.BlockSpec(memory_space=pl.ANY))
        aliases = {len(operands): fill[1]}
        operands.append(fill[0])
    return pl.pallas_call(
        body, name=name, grid=(gi, gj, gk), in_specs=in_specs,
        out_specs=[pl.BlockSpec(bs, im) for _, _, bs, im in outs],
        out_shape=[SDS(s, d) for s, d, _, _ in outs],
        scratch_shapes=[pltpu.VMEM((tm, tn), F32)] if gk > 1 else [],
        input_output_aliases=aliases,
        compiler_params=_cp(dimension_semantics=("arbitrary", "arbitrary", "arbitrary")),
    )(*operands)


def _mm_plain(name, a, b, *, tm, tn, tk, out_dtype=F32, trans_a=False):
    m = a.shape[1] if trans_a else a.shape[0]

    def epilogue(acc, i, j, ins, outs):
        outs[0][...] = acc.astype(out_dtype)

    return _mm_call(name, a, b, tm=tm, tn=tn, tk=tk, epilogue=epilogue, trans_a=trans_a,
                    outs=[((m, b.shape[1]), out_dtype, (tm, tn), lambda i, j, k: (i, j))])[0]


def _accumulate(ref, first, value):
    @pl.when(first)
    def _():
        ref[...] = value

    @pl.when(jnp.logical_not(first))
    def _():
        ref[...] += value


def _rms_bwd(xv, w, dy):
    r = lax.rsqrt(jnp.mean(xv * xv, axis=-1, keepdims=True) + NORM_EPS)
    xn = xv * r
    dxh = dy * w
    dx = r * (dxh - xn * jnp.mean(dxh * xn, axis=-1, keepdims=True))
    return dx, jnp.sum(dy * xn, axis=0, keepdims=True)


def _norm_mm(name, x, wn, w, *, tm, tn):
    t, dm = x.shape
    n = w.shape[1]
    tm = min(tm, t)

    def body(x_ref, wn_ref, w_ref, o_ref, u_ref):
        @pl.when(pl.program_id(1) == 0)
        def _():
            xv = x_ref[...]
            r = lax.rsqrt(jnp.mean(xv * xv, axis=-1, keepdims=True) + NORM_EPS)
            u_ref[...] = (xv * r * wn_ref[...]).astype(BF16)

        o_ref[...] = jnp.dot(u_ref[...], w_ref[...], preferred_element_type=F32)

    return pl.pallas_call(
        body, name=name, grid=(t // tm, n // tn),
        in_specs=[pl.BlockSpec((tm, dm), lambda i, j: (i, 0)), pl.BlockSpec((1, dm), lambda i, j: (0, 0)),
                  pl.BlockSpec((dm, tn), lambda i, j: (0, j))],
        out_specs=[pl.BlockSpec((tm, tn), lambda i, j: (i, j)), pl.BlockSpec((tm, dm), lambda i, j: (i, 0))],
        out_shape=[SDS((t, n), F32), SDS((t, dm), BF16)],
        compiler_params=_cp(dimension_semantics=("arbitrary", "arbitrary")),
    )(x, wn, w)


def _shift_down(tile, halo, s):
    if s == 0:
        return tile
    r = pltpu.roll(tile, s, axis=0)
    h = pltpu.roll(halo, s, axis=0)
    head = jnp.where(_iota(h.shape, 0) < s, h, r[0:8])
    return jnp.concatenate([head, r[8:]], axis=0)


def _shift_up(tile, halo, s):
    if s == 0:
        return tile
    n = tile.shape[0]
    r = pltpu.roll(tile, n - s, axis=0)
    h = pltpu.roll(halo, 8 - s, axis=0)
    tail = jnp.where(_iota(h.shape, 0) >= 8 - s, h, r[n - 8:])
    return jnp.concatenate([r[:n - 8], tail], axis=0)


def _conv_apply(tile, halo, wv, bv, kw):
    acc = bv + wv[kw - 1:kw, :] * tile
    for k in range(kw - 1):
        acc = acc + wv[k:k + 1, :] * _shift_down(tile, halo, kw - 1 - k)
    return acc


def _prev_halo_spec(tm, tc, col0):
    return pl.BlockSpec((8, tc), lambda i, j: (jnp.maximum(i * (tm // 8) - 1, 0), col0 + j))


def _silu_parts(pre):
    sg = _sigmoid(pre)
    return pre * sg, sg * (1.0 + pre * (1.0 - sg))


def _conv_silu_fwd(proj, w, b, *, tm, tc=512):
    t = proj.shape[0]
    c = w.shape[1]
    tm = min(tm, t)
    col0 = OFF_XBC // tc

    def body(x_ref, h_ref, w_ref, b_ref, o_ref):
        halo = jnp.where(pl.program_id(0) > 0, h_ref[...], 0.0)
        o_ref[...] = _silu_parts(_conv_apply(x_ref[...], halo, w_ref[...], b_ref[...], 4))[0]

    return pl.pallas_call(
        body, name="ssd_conv_fwd", grid=(t // tm, c // tc),
        in_specs=[pl.BlockSpec((tm, tc), lambda i, j: (i, col0 + j)), _prev_halo_spec(tm, tc, col0),
                  pl.BlockSpec((4, tc), lambda i, j: (0, j)), pl.BlockSpec((1, tc), lambda i, j: (0, j))],
        out_specs=pl.BlockSpec((tm, tc), lambda i, j: (i, j)),
        out_shape=SDS((t, c), F32),
        compiler_params=_cp(dimension_semantics=("arbitrary", "arbitrary")),
    )(proj, proj, w, b)


def _conv_silu_bwd1(d_out, proj, w, b, *, tm, tc=512):
    t = proj.shape[0]
    c = w.shape[1]
    tm = min(tm, t)
    col0 = OFF_XBC // tc

    def body(g_ref, x_ref, h_ref, w_ref, b_ref, o_ref, db_ref):
        i = pl.program_id(1)
        halo = jnp.where(i > 0, h_ref[...], 0.0)
        d_pre = g_ref[...] * _silu_parts(_conv_apply(x_ref[...], halo, w_ref[...], b_ref[...], 4))[1]
        o_ref[...] = d_pre
        _accumulate(db_ref, i == 0, jnp.sum(d_pre, axis=0, keepdims=True))

    return pl.pallas_call(
        body, name="ssd_conv_bwd1", grid=(c // tc, t // tm),
        in_specs=[pl.BlockSpec((tm, tc), lambda j, i: (i, j)), pl.BlockSpec((tm, tc), lambda j, i: (i, col0 + j)),
                  pl.BlockSpec((8, tc), lambda j, i: (jnp.maximum(i * (tm // 8) - 1, 0), col0 + j)),
                  pl.BlockSpec((4, tc), lambda j, i: (0, j)), pl.BlockSpec((1, tc), lambda j, i: (0, j))],
        out_specs=[pl.BlockSpec((tm, tc), lambda j, i: (i, j)), pl.BlockSpec((1, tc), lambda j, i: (0, j))],
        out_shape=[SDS((t, c), F32), SDS((1, c), F32)],
        compiler_params=_cp(dimension_semantics=("arbitrary", "arbitrary")),
    )(d_out, proj, proj, w, b)


def _conv_bwd2(name, d_pre, src, src_col0, w, *, tm, tc, out_cols, out_col0, fill=None):
    t, c = d_pre.shape
    kw = w.shape[0]
    tm = min(tm, t)
    ni = t // tm
    col0 = src_col0 // tc
    ocol0 = out_col0 // tc

    def body(g_ref, gn_ref, x_ref, xh_ref, w_ref, *rest):
        o_ref, dw_ref = rest[-2:]
        i = pl.program_id(1)
        g = g_ref[...]
        g_next = jnp.where(i < ni - 1, gn_ref[...], 0.0)
        x_prev = jnp.where(i > 0, xh_ref[...], 0.0)
        xv = x_ref[...]
        wv = w_ref[...]
        d_in = wv[kw - 1:kw, :] * g
        for k in range(kw - 1):
            d_in = d_in + wv[k:k + 1, :] * _shift_up(g, g_next, kw - 1 - k)
        o_ref[...] = d_in.astype(o_ref.dtype)
        rows = [jnp.sum(g * _shift_down(xv, x_prev, kw - 1 - k), axis=0, keepdims=True) for k in range(kw)]

        @pl.when(i == 0)
        def _():
            for k in range(kw):
                dw_ref[k:k + 1, :] = rows[k]

        @pl.when(i > 0)
        def _():
            for k in range(kw):
                dw_ref[k:k + 1, :] += rows[k]

    in_specs = [pl.BlockSpec((tm, tc), lambda j, i: (i, j)),
                pl.BlockSpec((8, tc), lambda j, i: (jnp.minimum((i + 1) * (tm // 8), t // 8 - 1), j)),
                pl.BlockSpec((tm, tc), lambda j, i: (i, col0 + j)),
                pl.BlockSpec((8, tc), lambda j, i: (jnp.maximum(i * (tm // 8) - 1, 0), col0 + j)),
                pl.BlockSpec((kw, tc), lambda j, i: (0, j))]
    operands = [d_pre, d_pre, src, src, w]
    if fill is not None:
        in_specs.append(pl.BlockSpec(memory_space=pl.ANY))
        operands.append(fill)
    return pl.pallas_call(
        body, name=name, grid=(c // tc, ni), in_specs=in_specs,
        out_specs=[pl.BlockSpec((tm, tc), lambda j, i: (i, ocol0 + j)), pl.BlockSpec((kw, tc), lambda j, i: (0, j))],
        out_shape=[SDS((t, out_cols), BF16), SDS((kw, c), F32)],
        input_output_aliases={} if fill is None else {5: 0},
        compiler_params=_cp(dimension_semantics=("arbitrary", "arbitrary")),
    )(*operands)


GELU_C = 0.7978845608028654


def _gelu_parts(v):
    inner = GELU_C * (v + 0.044715 * v * v * v)
    th = jnp.tanh(inner)
    val = 0.5 * v * (1.0 + th)
    grad = 0.5 * (1.0 + th) + 0.5 * v * (1.0 - th * th) * GELU_C * (1.0 + 3.0 * 0.044715 * v * v)
    return val, grad


def _ffn_act_specs(tm, tc, nj, order):
    def im(f):
        return (lambda i, j: f(i, j)) if order == "ij" else (lambda j, i: f(i, j))
    halo = lambda i: jnp.maximum(i * (tm // 8) - 1, 0)
    return [pl.BlockSpec((tm, tc), im(lambda i, j: (i, j))), pl.BlockSpec((8, tc), im(lambda i, j: (halo(i), j))),
            pl.BlockSpec((tm, tc), im(lambda i, j: (i, nj + j))), pl.BlockSpec((8, tc), im(lambda i, j: (halo(i), nj + j))),
            pl.BlockSpec((3, tc), im(lambda i, j: (0, j))), pl.BlockSpec((3, tc), im(lambda i, j: (0, nj + j))),
            pl.BlockSpec((1, tc), im(lambda i, j: (0, j))), pl.BlockSpec((1, tc), im(lambda i, j: (0, nj + j)))]


def _ffn_act_fwd(up_raw, w, b, *, tm, tc=1408):
    t = up_raw.shape[0]
    tm = min(tm, t)
    nj = FFN_D_FF // tc

    def body(g_ref, gh_ref, v_ref, vh_ref, wg_ref, wv_ref, bg_ref, bv_ref, o_ref):
        first = pl.program_id(0) > 0
        gate = _conv_apply(g_ref[...], jnp.where(first, gh_ref[...], 0.0), wg_ref[...], bg_ref[...], 3)
        val = _conv_apply(v_ref[...], jnp.where(first, vh_ref[...], 0.0), wv_ref[...], bv_ref[...], 3)
        o_ref[...] = (_gelu_parts(gate)[0] * val).astype(BF16)

    return pl.pallas_call(
        body, name="ffn_act_fwd", grid=(t // tm, nj), in_specs=_ffn_act_specs(tm, tc, nj, "ij"),
        out_specs=pl.BlockSpec((tm, tc), lambda i, j: (i, j)), out_shape=SDS((t, FFN_D_FF), BF16),
        compiler_params=_cp(dimension_semantics=("arbitrary", "arbitrary")),
    )(up_raw, up_raw, up_raw, up_raw, w, w, b, b)


def _ffn_act_bwd(up_raw, d_act, w, b, *, tm, tc=1408):
    t = up_raw.shape[0]
    tm = min(tm, t)
    nj = FFN_D_FF // tc

    def body(g_ref, gh_ref, v_ref, vh_ref, wg_ref, wv_ref, bg_ref, bv_ref, da_ref, dg_ref, dv_ref, dbg_ref, dbv_ref):
        i = pl.program_id(1)
        gate = _conv_apply(g_ref[...], jnp.where(i > 0, gh_ref[...], 0.0), wg_ref[...], bg_ref[...], 3)
        val = _conv_apply(v_ref[...], jnp.where(i > 0, vh_ref[...], 0.0), wv_ref[...], bv_ref[...], 3)
        ge, dge = _gelu_parts(gate)
        da = da_ref[...]
        d_gate = da * val * dge
        d_val = da * ge
        dg_ref[...] = d_gate
        dv_ref[...] = d_val
        _accumulate(dbg_ref, i == 0, jnp.sum(d_gate, axis=0, keepdims=True))
        _accumulate(dbv_ref, i == 0, jnp.sum(d_val, axis=0, keepdims=True))

    tile = pl.BlockSpec((tm, tc), lambda j, i: (i, j))
    row = pl.BlockSpec((1, tc), lambda j, i: (0, j))
    return pl.pallas_call(
        body, name="ffn_act_bwd", grid=(nj, t // tm), in_specs=_ffn_act_specs(tm, tc, nj, "ji") + [tile],
        out_specs=[tile, tile, row, row],
        out_shape=[SDS((t, FFN_D_FF), F32), SDS((t, FFN_D_FF), F32), SDS((1, FFN_D_FF), F32), SDS((1, FFN_D_FF), F32)],
        compiler_params=_cp(dimension_semantics=("arbitrary", "arbitrary")),
    )(up_raw, up_raw, up_raw, up_raw, w, w, b, b, d_act)


def _softplus(v):
    e = jnp.exp(-jnp.abs(v))
    small = e * (1.0 - 0.5 * e)
    return jnp.maximum(v, 0.0) + jnp.where(e < 1e-4, small, jnp.log(1.0 + e))


def _dt_fwd(proj, bias_pad, *, tm):
    t = proj.shape[0]
    tm = min(tm, t)

    def body(x_ref, b_ref, o_ref):
        o_ref[...] = _softplus(x_ref[...] + b_ref[...])

    return pl.pallas_call(
        body, name="dt_fwd", grid=(t // tm,),
        in_specs=[pl.BlockSpec((tm, 128), lambda i: (i, OFF_DT // 128)), pl.BlockSpec((1, 128), lambda i: (0, 0))],
        out_specs=pl.BlockSpec((tm, 128), lambda i: (i, 0)), out_shape=SDS((t, 128), F32),
        compiler_params=_cp(dimension_semantics=("arbitrary",)),
    )(proj, bias_pad)


def _dt_bwd(d_dt, proj, bias_pad, d_proj, *, tm):
    t = proj.shape[0]
    tm = min(tm, t)

    def body(g_ref, x_ref, b_ref, _, o_ref, db_ref):
        d_raw = g_ref[...] * _sigmoid(x_ref[...] + b_ref[...])
        o_ref[:, 0:128] = d_raw.astype(BF16)
        o_ref[:, 128:512] = jnp.zeros((tm, 384), BF16)
        _accumulate(db_ref, pl.program_id(0) == 0, jnp.sum(d_raw, axis=0, keepdims=True))

    return pl.pallas_call(
        body, name="dt_bwd", grid=(t // tm,),
        in_specs=[pl.BlockSpec((tm, 128), lambda i: (i, 0)), pl.BlockSpec((tm, 128), lambda i: (i, OFF_DT // 128)),
                  pl.BlockSpec((1, 128), lambda i: (0, 0)), pl.BlockSpec(memory_space=pl.ANY)],
        out_specs=[pl.BlockSpec((tm, 512), lambda i: (i, OFF_DT // 512)), pl.BlockSpec((1, 128), lambda i: (0, 0))],
        out_shape=[SDS((t, PROJ_W), BF16), SDS((1, 128), F32)],
        input_output_aliases={3: 0},
        compiler_params=_cp(dimension_semantics=("arbitrary",)),
    )(d_dt, proj, bias_pad, d_proj)


def _split3(v):
    hi = v.astype(BF16)
    r1 = v - hi.astype(F32)
    mid = r1.astype(BF16)
    return hi, mid, (r1 - mid.astype(F32)).astype(BF16)


def _times01(v, m3):
    return jnp.dot(jnp.concatenate(_split3(v), axis=1), m3, preferred_element_type=F32)


def _01times(m3, v):
    return jnp.dot(m3, jnp.concatenate(_split3(v), axis=0), preferred_element_type=F32)


def _ssd_decay(dt_ref, dtT_ref, al_ref, alT_ref, k):
    dt = dt_ref[0]
    a_row = -jnp.exp(al_ref[0])
    adt_t = dtT_ref[0] * (-jnp.exp(alT_ref[0]))
    return dt, a_row, _01times(k['low3'][...], dt * a_row), _times01(adt_t, k['up3v'][...])


def _ssd_specs(nc, rev):
    ci = (lambda c: nc - 1 - c) if rev else (lambda c: c)
    return [pl.BlockSpec((CHUNK, GROUP_W), lambda c, g: (ci(c), g)),
            pl.BlockSpec((1, CHUNK, 128), lambda c, g: (g, ci(c), 0)),
            pl.BlockSpec((1, 8, CHUNK), lambda c, g: (g, 0, ci(c))),
            pl.BlockSpec((1, 1, 128), lambda c, g: (g, 0, 0)),
            pl.BlockSpec((1, 8, 1), lambda c, g: (g, 0, 0)),
            pl.BlockSpec((1, 512), lambda c, g: (0, g))]


NT = (((1,), (1,)), ((), ()))
WIDE = 8 * CHUNK
SSD_CONST_NAMES = ('e128', 'e64', 's64', 'mlo', 'mup', 'low3', 'up3', 'up3v')
SSD_CONST_SHAPES = [pltpu.VMEM((3 * CHUNK, WIDE), BF16), pltpu.VMEM((3 * CHUNK, 512), BF16), pltpu.VMEM((512, CHUNK), BF16),
                    pltpu.VMEM((CHUNK, WIDE), F32), pltpu.VMEM((CHUNK, WIDE), F32), pltpu.VMEM((CHUNK, 3 * CHUNK), BF16),
                    pltpu.VMEM((CHUNK, 3 * CHUNK), BF16), pltpu.VMEM((3 * CHUNK, CHUNK), BF16)]


def _ssd_init_consts(k):
    row, col = _iota((3 * CHUNK, WIDE), 0), _iota((3 * CHUNK, WIDE), 1)
    k['e128'][...] = ((col >> 7) == (row & 127)).astype(BF16)
    k['e64'][...] = ((_iota((3 * CHUNK, 512), 1) >> 6) == (_iota((3 * CHUNK, 512), 0) & 127)).astype(BF16)
    k['s64'][...] = ((_iota((512, CHUNK), 0) >> 6) == _iota((512, CHUNK), 1)).astype(BF16)
    row, col = _iota((CHUNK, WIDE), 0), _iota((CHUNK, WIDE), 1)
    k['mlo'][...] = (row >= (col & 127)).astype(F32)
    k['mup'][...] = (row <= (col & 127)).astype(F32)
    row, col = _iota((CHUNK, 3 * CHUNK), 0), _iota((CHUNK, 3 * CHUNK), 1) & 127
    k['low3'][...] = (row >= col).astype(BF16)
    k['up3'][...] = (row <= col).astype(BF16)
    row, col = _iota((3 * CHUNK, CHUNK), 0) & 127, _iota((3 * CHUNK, CHUNK), 1)
    k['up3v'][...] = (row <= col).astype(BF16)


def _ssd_common(x_ref, dt_ref, dtT_ref, al_ref, alT_ref, k):
    dt, a_row, acs, acs_t = _ssd_decay(dt_ref, dtT_ref, al_ref, alT_ref, k)
    ecol = _times01(acs, k['e128'][...])
    rrow = jnp.concatenate([jnp.broadcast_to(acs_t[j:j + 1, :], (CHUNK, CHUNK)) for j in range(8)], axis=1)
    a64 = _times01(acs, k['e64'][...])
    dt64 = _times01(dt, k['e64'][...])
    a_end64 = a64[CHUNK - 1:CHUNK, :]
    xs = x_ref[:, 0:512]
    return dict(dt=dt, a_row=a_row, acs=acs, seg=ecol - rrow, dt64=dt64, e_a=jnp.exp(a64), decay=jnp.exp(a_end64 - a64),
                e_end64=jnp.exp(a_end64), xs=xs, xdt=xs * dt64, bm=x_ref[:, 512:640], cm=x_ref[:, 640:768])


def _pair_blocks(v):
    lo = _iota((CHUNK, 128), 1) < 64
    out = []
    for i in range(4):
        ch = v[:, i * 128:(i + 1) * 128]
        out.append(jnp.concatenate([jnp.where(lo, ch, 0.0), jnp.where(lo, 0.0, ch)], axis=0).astype(BF16))
    return out


def _tile8(m):
    return jnp.concatenate([m] * 8, axis=1)


def _ssd_fwd(xc, dtg, dtg_t, alog, alog_t, d_exp):
    t = xc.shape[0]
    nc = t // CHUNK

    def body(x_ref, dt_ref, dtT_ref, al_ref, alT_ref, d_ref, y_ref, hs_ref, h_scr, *consts):
        c, g = pl.program_id(0), pl.program_id(1)
        k = dict(zip(SSD_CONST_NAMES, consts))

        @pl.when(jnp.logical_and(c == 0, g == 0))
        def _():
            _ssd_init_consts(k)

        @pl.when(c == 0)
        def _():
            h_scr[g] = jnp.zeros((SSD_D_STATE, 512), F32)

        v = _ssd_common(x_ref, dt_ref, dtT_ref, al_ref, alT_ref, k)
        b16, c16 = v['bm'].astype(BF16), v['cm'].astype(BF16)
        cb = lax.dot_general(c16, b16, NT, preferred_element_type=F32)
        m16 = (jnp.exp(jnp.minimum(v['seg'], 0.0)) * k['mlo'][...] * _tile8(cb)).astype(BF16)
        xbd = _pair_blocks(v['xdt'])
        y_diag = jnp.concatenate([jnp.dot(m16[:, i * 256:(i + 1) * 256], xbd[i], preferred_element_type=F32)
                                  for i in range(4)], axis=1)
        ht = h_scr[g]
        y_off = jnp.dot(c16, ht.astype(BF16), preferred_element_type=F32)
        y_ref[...] = y_diag + v['e_a'] * y_off + d_ref[...] * v['xs']
        st = jnp.dot(v['bm'].T.astype(BF16), (v['xdt'] * v['decay']).astype(BF16), preferred_element_type=F32)
        hs_ref[0, 0] = ht
        h_scr[g] = ht * v['e_end64'] + st

    return pl.pallas_call(
        body, name="ssd_fwd", grid=(nc, SSD_N_GROUPS), in_specs=_ssd_specs(nc, False),
        out_specs=[pl.BlockSpec((CHUNK, 512), lambda c, g: (c, g)),
                   pl.BlockSpec((1, 1, SSD_D_STATE, 512), lambda c, g: (c, g, 0, 0))],
        out_shape=[SDS((t, SSD_D_INNER), F32), SDS((nc, SSD_N_GROUPS, SSD_D_STATE, 512), F32)],
        scratch_shapes=[pltpu.VMEM((SSD_N_GROUPS, SSD_D_STATE, 512), F32)] + SSD_CONST_SHAPES,
        compiler_params=_cp(dimension_semantics=("arbitrary", "arbitrary")),
    )(xc, dtg, dtg_t, alog, alog_t, d_exp)


def _ssd_bwd(xc, dtg, dtg_t, alog, alog_t, d_exp, d_y, hs):
    t = xc.shape[0]
    nc = t // CHUNK

    def body(x_ref, dt_ref, dtT_ref, al_ref, alT_ref, d_ref, dy_ref, hs_ref,
             dx_ref, ddt_ref, dal_ref, dd_ref, g_scr, *consts):
        c, g = pl.program_id(0), pl.program_id(1)
        k = dict(zip(SSD_CONST_NAMES, consts))
        s64, mlo, mup = k['s64'], k['mlo'], k['mup']

        @pl.when(jnp.logical_and(c == 0, g == 0))
        def _():
            _ssd_init_consts(k)

        @pl.when(c == 0)
        def _():
            g_scr[g] = jnp.zeros((SSD_D_STATE, 512), F32)

        v = _ssd_common(x_ref, dt_ref, dtT_ref, al_ref, alT_ref, k)
        dt, a_row, xs, xdt, e_a, decay = v['dt'], v['a_row'], v['xs'], v['xdt'], v['e_a'], v['decay']
        row, col = _iota((CHUNK, CHUNK), 0), _iota((CHUNK, CHUNK), 1)
        b16, c16 = v['bm'].astype(BF16), v['cm'].astype(BF16)
        ct16 = v['cm'].T.astype(BF16)
        cb = lax.dot_general(c16, b16, NT, preferred_element_type=F32)
        cbt = lax.dot_general(b16, c16, NT, preferred_element_type=F32)
        lmat = jnp.exp(jnp.minimum(v['seg'], 0.0)) * mlo[...]
        lmat_t = jnp.exp(jnp.minimum(-v['seg'], 0.0)) * mup[...]
        mmat, mmat_t = lmat * _tile8(cb), lmat_t * _tile8(cbt)
        mt16 = mmat_t.astype(BF16)
        dy = dy_ref[...]
        dye, xdec = dy * e_a, xdt * decay
        dy16, dye16, xdec16 = dy.astype(BF16), dye.astype(BF16), xdec.astype(BF16)
        xdt16 = xdt.astype(BF16)
        ht, gt = hs_ref[0, 0], g_scr[g]
        ht16, gt16 = ht.astype(BF16), gt.astype(BF16)
        xbd, dybd = _pair_blocks(xdt), _pair_blocks(dy)
        d_m, d_mt, d_x = [], [], []
        for i in range(4):
            csl = slice(i * 128, (i + 1) * 128)
            d_m.append(lax.dot_general(dy16[:, csl], xbd[i], NT, preferred_element_type=F32))
            d_mt.append(lax.dot_general(xdt16[:, csl], dybd[i], NT, preferred_element_type=F32))
            d_x.append(jnp.dot(mt16[:, i * 256:(i + 1) * 256], dybd[i], preferred_element_type=F32))
        d_m, d_mt, d_x = jnp.concatenate(d_m, axis=1), jnp.concatenate(d_mt, axis=1), jnp.concatenate(d_x, axis=1)

        def head_sum(m):
            acc = m[:, 0:CHUNK]
            for j in range(1, 8):
                acc = acc + m[:, j * CHUNK:(j + 1) * CHUNK]
            return acc

        def seg64(p):
            return jnp.dot(p.astype(BF16), s64[...], preferred_element_type=F32)

        d_cb16 = head_sum(d_m * lmat).astype(BF16)
        d_cbt16 = head_sum(d_mt * lmat_t).astype(BF16)
        dseg = d_m * mmat - d_mt * mmat_t
        da_seg = jnp.zeros((CHUNK, CHUNK), F32)
        for j in range(8):
            da_seg = jnp.where(col == j, jnp.sum(dseg[:, j * CHUNK:(j + 1) * CHUNK], axis=1, keepdims=True), da_seg)
        ch = jnp.dot(c16, ht16, preferred_element_type=F32)
        bg = jnp.dot(b16, gt16, preferred_element_type=F32)
        d_x = d_x + decay * bg
        d_decay = seg64(xdec * bg)
        e_end = jnp.exp(v['acs'][CHUNK - 1:CHUNK, :])
        d_end = e_end * jnp.sum(seg64(gt * ht), axis=0, keepdims=True) + jnp.sum(d_decay, axis=0, keepdims=True)
        d_a = seg64(dye * ch) - d_decay + da_seg + jnp.where(row == CHUNK - 1, d_end, 0.0)
        dx_ref[:, 0:512] = d_x * v['dt64'] + d_ref[...] * dy
        dx_ref[:, 640:768] = (lax.dot_general(dye16, ht16, NT, preferred_element_type=F32)
                              + jnp.dot(d_cb16, b16, preferred_element_type=F32))
        dx_ref[:, 512:640] = (lax.dot_general(xdec16, gt16, NT, preferred_element_type=F32)
                              + jnp.dot(d_cbt16, c16, preferred_element_type=F32))
        g_scr[g] = gt * v['e_end64'] + jnp.dot(ct16, dye16, preferred_element_type=F32)
        d_adt = _01times(k['up3'][...], d_a)
        ddt_ref[0] = d_adt * a_row + seg64(d_x * xs)
        d_alog = jnp.sum(d_adt * dt, axis=0, keepdims=True) * a_row
        dd_row = jnp.sum(seg64(dy * xs), axis=0, keepdims=True)
        first = c == 0

        @pl.when(first)
        def _():
            dal_ref[g] = d_alog
            dd_ref[g] = dd_row

        @pl.when(jnp.logical_not(first))
        def _():
            dal_ref[g] += d_alog
            dd_ref[g] += dd_row

    rc = lambda c: nc - 1 - c
    whole = pl.BlockSpec((SSD_N_GROUPS, 1, 128), lambda c, g: (0, 0, 0))
    return pl.pallas_call(
        body, name="ssd_bwd", grid=(nc, SSD_N_GROUPS),
        in_specs=_ssd_specs(nc, True) + [pl.BlockSpec((CHUNK, 512), lambda c, g: (rc(c), g)),
                                        pl.BlockSpec((1, 1, SSD_D_STATE, 512), lambda c, g: (rc(c), g, 0, 0))],
        out_specs=[pl.BlockSpec((CHUNK, GROUP_W), lambda c, g: (rc(c), g)),
                   pl.BlockSpec((1, CHUNK, 128), lambda c, g: (g, rc(c), 0)), whole, whole],
        out_shape=[SDS((t, SSD_CONV_DIM), F32), SDS((SSD_N_GROUPS, t, 128), F32),
                   SDS((SSD_N_GROUPS, 1, 128), F32), SDS((SSD_N_GROUPS, 1, 128), F32)],
        scratch_shapes=[pltpu.VMEM((SSD_N_GROUPS, SSD_D_STATE, 512), F32)] + SSD_CONST_SHAPES,
        compiler_params=_cp(dimension_semantics=("arbitrary", "arbitrary")),
    )(xc, dtg, dtg_t, alog, alog_t, d_exp, d_y, hs)


def _gated_norm_fwd(y, proj, w, *, tm):
    t = y.shape[0]
    tm = min(tm, t)

    def body(y_ref, z_ref, w_ref, o_ref):
        gv = y_ref[...] * _silu_parts(z_ref[...])[0]
        r = lax.rsqrt(jnp.mean(gv * gv, axis=-1, keepdims=True) + NORM_EPS)
        o_ref[...] = (gv * r * w_ref[...]).astype(BF16)

    tile = pl.BlockSpec((tm, 512), lambda i, g: (i, g))
    return pl.pallas_call(
        body, name="gated_norm_fwd", grid=(t // tm, SSD_N_GROUPS),
        in_specs=[tile, pl.BlockSpec((tm, 512), lambda i, g: (i, OFF_Z // 512 + g)),
                  pl.BlockSpec((1, 512), lambda i, g: (0, g))], out_specs=tile,
        out_shape=SDS((t, SSD_D_INNER), BF16),
        compiler_params=_cp(dimension_semantics=("arbitrary", "arbitrary")),
    )(y, proj, w)


def _rope(ch, cos_t, sin_t):
    first = (_iota(ch.shape, 1) & 32) == 0
    partner = jnp.where(first, pltpu.roll(ch, 96, axis=1), pltpu.roll(ch, 32, axis=1))
    return ch * cos_t + partner * sin_t


def _rope_qkv(proj, cos_t, sin_t, *, tm):
    t = proj.shape[0]
    tm = min(tm, t)

    def body(q_ref, k_ref, v_ref, c_ref, s_ref, qr_ref, kp_ref, vp_ref):
        cv, sv = c_ref[...], s_ref[...]
        lo = _iota((tm, 128), 1) < 64
        for m in range(8):
            sl = slice(m * 128, (m + 1) * 128)
            qr_ref[:, sl] = (_rope(q_ref[:, sl], cv, sv) * 0.125).astype(BF16)
        for m2 in range(2):
            sl = slice(m2 * 128, (m2 + 1) * 128)
            for src, dst in ((_rope(k_ref[:, sl], cv, sv), kp_ref), (v_ref[:, sl], vp_ref)):
                sw = pltpu.roll(src, 64, axis=1)
                base = 4 * m2 * 128
                dst[:, base:base + 128] = jnp.where(lo, src, 0.0).astype(BF16)
                dst[:, base + 128:base + 256] = jnp.where(lo, 0.0, sw).astype(BF16)
                dst[:, base + 256:base + 384] = jnp.where(lo, sw, 0.0).astype(BF16)
                dst[:, base + 384:base + 512] = jnp.where(lo, 0.0, src).astype(BF16)

    return pl.pallas_call(
        body, name="rope_qkv", grid=(t // tm,),
        in_specs=[pl.BlockSpec((tm, 1024), lambda i: (i, OFF_Q // 1024)), pl.BlockSpec((tm, 256), lambda i: (i, OFF_K // 256)),
                  pl.BlockSpec((tm, 256), lambda i: (i, OFF_V // 256)), pl.BlockSpec((tm, 128), lambda i: (i, 0)),
                  pl.BlockSpec((tm, 128), lambda i: (i, 0))],
        out_specs=[pl.BlockSpec((tm, 1024), lambda i: (i, 0))] * 3,
        out_shape=[SDS((t, 1024), BF16)] * 3,
        compiler_params=_cp(dimension_semantics=("arbitrary",)),
    )(proj, proj, proj, cos_t, sin_t)


def _attn_valid(n):
    qi, kj = _iota((CHUNK, 2 * CHUNK), 0), _iota((CHUNK, 2 * CHUNK), 1)
    return (kj > qi) & (kj <= qi + CHUNK) & ((n > 0) | (kj >= CHUNK))


def _attn_fwd(qr, kp, vp, sinks):
    t = qr.shape[0]
    nb = t // CHUNK

    def body(q_ref, kc_ref, kprev_ref, vc_ref, vprev_ref, sk_ref, o_ref, lse_ref):
        n = pl.program_id(0)
        valid = _attn_valid(n)
        lane = _iota((CHUNK, 128), 1)
        lse_all = jnp.zeros((CHUNK, 128), F32)
        for m in range(8):
            g = m // 2
            qch = q_ref[:, m * 128:(m + 1) * 128]
            o_pair = jnp.zeros((CHUNK, 128), F32)
            for e in range(2):
                h = 2 * m + e
                sl = slice((2 * g + e) * 128, (2 * g + e + 1) * 128)
                kk = jnp.concatenate([kprev_ref[:, sl], kc_ref[:, sl]], axis=0)
                vv = jnp.concatenate([vprev_ref[:, sl], vc_ref[:, sl]], axis=0)
                s = lax.dot_general(qch, kk, (((1,), (1,)), ((), ())), preferred_element_type=F32)
                s = jnp.where(valid, s, NEG)
                sink = sk_ref[0:1, h:h + 1]
                mx = jnp.maximum(jnp.max(s, axis=1, keepdims=True), sink)
                p = jnp.exp(s - mx)
                den = jnp.sum(p, axis=1, keepdims=True) + jnp.exp(sink - mx)
                o_pair = o_pair + jnp.dot((p / den).astype(BF16), vv, preferred_element_type=F32)
                lse_all = jnp.where(lane == h, mx + jnp.log(den), lse_all)
            o_ref[:, m * 128:(m + 1) * 128] = o_pair.astype(BF16)
        lse_ref[...] = lse_all

    cur = pl.BlockSpec((CHUNK, 1024), lambda n: (n, 0))
    prev = pl.BlockSpec((CHUNK, 1024), lambda n: (jnp.maximum(n - 1, 0), 0))
    return pl.pallas_call(
        body, name="attn_fwd", grid=(nb,),
        in_specs=[cur, cur, prev, cur, prev, pl.BlockSpec((1, 128), lambda n: (0, 0))],
        out_specs=[cur, pl.BlockSpec((CHUNK, 128), lambda n: (n, 0))],
        out_shape=[SDS((t, 1024), BF16), SDS((t, 128), F32)],
        compiler_params=_cp(dimension_semantics=("arbitrary",)),
    )(qr, kp, kp, vp, vp, sinks)


def _attn_bwd(qr, kp, vp, d_o, o, lse, sinks, cos_t, sin_t, d_proj):
    t = qr.shape[0]
    nb = t // CHUNK

    def body(q_ref, kc_ref, kprev_ref, vc_ref, vprev_ref, do_ref, o_ref, lse_ref, sk_ref, c_ref, s_ref, cp_ref, sp_ref,
             _, dqkv_ref, dsk_ref, acc_k, acc_v, dq_scr):
        n = pl.program_id(0)
        lane = _iota((CHUNK, 128), 1)
        lo = lane < 64
        lane1 = _iota((1, 128), 1)

        @pl.when(n == 0)
        def _():
            acc_k[...] = jnp.zeros_like(acc_k)
            acc_v[...] = jnp.zeros_like(acc_v)
            dsk_ref[...] = jnp.zeros((1, 128), F32)

        @pl.when(n > 0)
        def _():
            dqkv_ref[:, 0:1024] = dq_scr[...]
            for r in range(8):
                acc_k[r, 0:CHUNK] = acc_k[r, CHUNK:2 * CHUNK]
                acc_v[r, 0:CHUNK] = acc_v[r, CHUNK:2 * CHUNK]
                acc_k[r, CHUNK:2 * CHUNK] = jnp.zeros((CHUNK, 128), F32)
                acc_v[r, CHUNK:2 * CHUNK] = jnp.zeros((CHUNK, 128), F32)

        @pl.when(n < nb)
        def _():
            valid = _attn_valid(n)
            lse_all = lse_ref[...]
            dsk = jnp.zeros((1, 128), F32)
            for m in range(8):
                g = m // 2
                csl = slice(m * 128, (m + 1) * 128)
                qch = q_ref[:, csl]
                doch = do_ref[:, csl]
                prod = doch.astype(F32) * o_ref[:, csl].astype(F32)
                dq_pair = jnp.zeros((CHUNK, 128), F32)
                for e in range(2):
                    h = 2 * m + e
                    sl = slice((2 * g + e) * 128, (2 * g + e + 1) * 128)
                    kk = jnp.concatenate([kprev_ref[:, sl], kc_ref[:, sl]], axis=0)
                    vv = jnp.concatenate([vprev_ref[:, sl], vc_ref[:, sl]], axis=0)
                    lse_h = lse_all[:, h:h + 1]
                    s = lax.dot_general(qch, kk, (((1,), (1,)), ((), ())), preferred_element_type=F32)
                    p = jnp.exp(jnp.where(valid, s, NEG) - lse_h)
                    delta = jnp.sum(jnp.where(lo if e == 0 else jnp.logical_not(lo), prod, 0.0), axis=1, keepdims=True)
                    d_p = lax.dot_general(doch, vv, (((1,), (1,)), ((), ())), preferred_element_type=F32)
                    d_s16 = (p * (d_p - delta)).astype(BF16)
                    dq_pair = dq_pair + jnp.dot(d_s16, kk, preferred_element_type=F32)
                    acc_k[2 * g + e] += lax.dot_general(d_s16, qch, (((0,), (0,)), ((), ())), preferred_element_type=F32)
                    acc_v[2 * g + e] += lax.dot_general(p.astype(BF16), doch, (((0,), (0,)), ((), ())), preferred_element_type=F32)
                    p_sink = jnp.exp(sk_ref[0:1, h:h + 1] - lse_h)
                    dsk = jnp.where(lane1 == h, -jnp.sum(p_sink * delta), dsk)
                dq_scr[:, csl] = (_rope(dq_pair, c_ref[...], -s_ref[...]) * 0.125).astype(BF16)
            dsk_ref[...] += dsk

        @pl.when(n > 0)
        def _():
            for m2 in range(2):
                halves = []
                for g in (2 * m2, 2 * m2 + 1):
                    for acc in (acc_k, acc_v):
                        comb = jnp.where(lo, acc[2 * g, 0:CHUNK], acc[2 * g + 1, 0:CHUNK])
                        halves.append(comb + pltpu.roll(comb, 64, axis=1))
                d_kr = jnp.where(lo, halves[0], halves[2])
                d_v = jnp.where(lo, halves[1], halves[3])
                dqkv_ref[:, OFF_K + m2 * 128:OFF_K + (m2 + 1) * 128] = _rope(d_kr, cp_ref[...], -sp_ref[...]).astype(BF16)
                dqkv_ref[:, OFF_V + m2 * 128:OFF_V + (m2 + 1) * 128] = d_v.astype(BF16)

    qn = lambda n: jnp.minimum(n, nb - 1)
    pn = lambda n: jnp.maximum(jnp.minimum(n, nb) - 1, 0)
    cur = pl.BlockSpec((CHUNK, 1024), lambda n: (qn(n), 0))
    prev = pl.BlockSpec((CHUNK, 1024), lambda n: (pn(n), 0))
    cur128 = pl.BlockSpec((CHUNK, 128), lambda n: (qn(n), 0))
    prev128 = pl.BlockSpec((CHUNK, 128), lambda n: (pn(n), 0))
    one = pl.BlockSpec((1, 128), lambda n: (0, 0))
    return pl.pallas_call(
        body, name="attn_bwd", grid=(nb + 1,),
        in_specs=[cur, cur, prev, cur, prev, cur, cur, cur128, one, cur128, cur128, prev128, prev128,
                  pl.BlockSpec(memory_space=pl.ANY)],
        out_specs=[pl.BlockSpec((CHUNK, 1536), lambda n: (pn(n), 0)), one],
        out_shape=[SDS((t, PROJ_W), BF16), SDS((1, 128), F32)],
        scratch_shapes=[pltpu.VMEM((8, 2 * CHUNK, 128), F32), pltpu.VMEM((8, 2 * CHUNK, 128), F32),
                        pltpu.VMEM((CHUNK, 1024), BF16)],
        input_output_aliases={13: 0},
        compiler_params=_cp(dimension_semantics=("arbitrary",)),
    )(qr, kp, kp, vp, vp, d_o, o, lse, sinks, cos_t, sin_t, cos_t, sin_t, d_proj)


def _adamw(name, w, g, m, v, *, tr):
    rows, cols = w.shape
    tr = min(tr, rows)
    assert rows % tr == 0

    def body(w_ref, g_ref, m_ref, v_ref, d_ref, nm_ref, nv_ref):
        gv = g_ref[...]
        nm = ADAM_B1 * m_ref[...] + (1.0 - ADAM_B1) * gv
        nv = ADAM_B2 * v_ref[...] + (1.0 - ADAM_B2) * (gv * gv)
        m_hat = nm / (1.0 - ADAM_B1 ** ADAM_STEP)
        v_hat = nv / (1.0 - ADAM_B2 ** ADAM_STEP)
        d_ref[...] = -ADAM_LR * (m_hat / (jnp.sqrt(v_hat) + ADAM_EPS) + ADAM_WD * w_ref[...])
        nm_ref[...] = nm
        nv_ref[...] = nv

    tile = pl.BlockSpec((tr, cols), lambda i: (i, 0))
    return pl.pallas_call(
        body, name=name, grid=(rows // tr,), in_specs=[tile] * 4, out_specs=[tile] * 3,
        out_shape=[SDS((rows, cols), F32)] * 3, compiler_params=_cp(dimension_semantics=("arbitrary",)),
    )(w, g, m, v)


def _local_step(x, cos_t, sin_t, tgt, wb, ps):
    t = x.shape[0]
    tm = min(512, t)
    tmw = min(1024, t)
    ij = lambda i, j, k: (i, j)
    i0 = lambda i, j, k: (i, 0)
    c0 = lambda i, j, k: (0, 0)
    cj = lambda i, j, k: (0, j)

    proj, u = _norm_mm("in_proj", x, ps['norm_mix_pre_w'], wb['cat'], tm=tmw, tn=512)
    xc = _conv_silu_fwd(proj, ps['ssd_conv_w'], ps['ssd_conv_b'], tm=tm)
    bias_pad = jnp.pad(ps['ssd_dt_bias'], ((0, 0), (0, 96)))
    dt = _dt_fwd(proj, bias_pad, tm=tmw)
    dt32 = dt[:, :SSD_N_HEADS].reshape(t, SSD_N_GROUPS, 8)
    dtg = jnp.pad(dt32.transpose(1, 0, 2), ((0, 0), (0, 0), (0, 120)))
    dtg_t = dt32.transpose(1, 2, 0)
    alog = jnp.pad(ps['ssd_a_log'].reshape(SSD_N_GROUPS, 1, 8), ((0, 0), (0, 0), (0, 120)))
    alog_t = ps['ssd_a_log'].reshape(SSD_N_GROUPS, 8, 1)
    d_exp = jnp.repeat(ps['ssd_d'], SSD_HEAD_DIM, axis=1)
    y, hs = _ssd_fwd(xc, dtg, dtg_t, alog, alog_t, d_exp)
    gn = _gated_norm_fwd(y, proj, ps['ssd_norm_w'], tm=tm)
    qr, kp, vp = _rope_qkv(proj, cos_t, sin_t, tm=tm)
    sinks = jnp.pad(ps['attn_sinks'], ((0, 0), (0, 112)))
    ao, lse = _attn_fwd(qr, kp, vp, sinks)
    y_attn = _mm_plain("attn_out", ao, wb['ao'], tm=tmw, tn=512, tk=1024)

    def merge_ep(acc, i, j, ins, outs):
        gs, ga, ya = ins
        outs[0][...] = (_sigmoid(gs[...]) * acc + _sigmoid(ga[...]) * ya[...]).astype(BF16)
        outs[1][...] = acc

    merged, y_ssd = _mm_call(
        "ssd_out_merge", gn, wb['so'], tm=tmw, tn=512, tk=2048, epilogue=merge_ep,
        extra_in=[(proj, (tmw, 512), lambda i, j, k: (i, OFF_GS // 512 + j)),
                  (proj, (tmw, 512), lambda i, j, k: (i, OFF_GA // 512 + j)), (y_attn, (tmw, 512), ij)],
        outs=[((t, D_MODEL), BF16, (tmw, 512), ij), ((t, D_MODEL), F32, (tmw, 512), ij)])

    def mix_ep(acc, i, j, ins, outs):
        xv, wn = ins
        r = lax.rsqrt(jnp.mean(acc * acc, axis=-1, keepdims=True) + NORM_EPS)
        outs[0][...] = xv[...] + acc * r * wn[...]
        outs[1][...] = acc

    x1, mmix = _mm_call(
        "mix_out", merged, wb['mix'], tm=tm, tn=D_MODEL, tk=1024, epilogue=mix_ep,
        extra_in=[(x, (tm, D_MODEL), i0), (ps['norm_mix_post_w'], (1, D_MODEL), c0)],
        outs=[((t, D_MODEL), F32, (tm, D_MODEL), i0), ((t, D_MODEL), F32, (tm, D_MODEL), i0)])

    up_raw, h = _norm_mm("ffn_up", x1, ps['norm_ffn_pre_w'], wb['up'], tm=tmw, tn=512)
    act = _ffn_act_fwd(up_raw, ps['ffn_conv_w'], ps['ffn_conv_b'], tm=min(256, t))

    def loss_ep(acc, i, j, ins, outs):
        x1v, tg, wn = ins
        d_ff_ref, dout_ref, loss_ref, dw_ref = outs
        wv = wn[...]
        r = lax.rsqrt(jnp.mean(acc * acc, axis=-1, keepdims=True) + NORM_EPS)
        err = x1v[...] + acc * r * wv - tg[...]
        dout = err * (1.0 / D_MODEL)
        dout_ref[...] = dout
        d_ff, dw = _rms_bwd(acc, wv, dout)
        d_ff_ref[...] = d_ff.astype(BF16)
        _accumulate(dw_ref, i == 0, dw)
        _accumulate(loss_ref, i == 0, jnp.sum(err * err, keepdims=True) * (0.5 / D_MODEL))

    d_ff, dout, loss, g_norm_ffn_post = _mm_call(
        "ffn_down_loss", act, wb['dn'], tm=tm, tn=D_MODEL, tk=1408, epilogue=loss_ep,
        extra_in=[(x1, (tm, D_MODEL), i0), (tgt, (tm, D_MODEL), i0), (ps['norm_ffn_post_w'], (1, D_MODEL), c0)],
        outs=[((t, D_MODEL), BF16, (tm, D_MODEL), i0), ((t, D_MODEL), F32, (tm, D_MODEL), i0),
              ((1, 1), F32, (1, 1), c0), ((1, D_MODEL), F32, (1, D_MODEL), c0)])

    d_act = _mm_plain("d_act", d_ff, wb['dn_t'], tm=tmw, tn=1408, tk=1024)
    g_w_down = _mm_plain("g_w_down", act, d_ff, tm=1408, tn=1024, tk=tm, trans_a=True)
    d_gate, d_val, db_g, db_v = _ffn_act_bwd(up_raw, d_act, ps['ffn_conv_w'], ps['ffn_conv_b'], tm=min(256, t))
    d_up_raw, gcw_g = _conv_bwd2("ffn_conv_bwd2_gate", d_gate, up_raw, 0, ps['ffn_conv_w'][:, :FFN_D_FF], tm=min(256, t),
                                 tc=1408, out_cols=2 * FFN_D_FF, out_col0=0)
    d_up_raw, gcw_v = _conv_bwd2("ffn_conv_bwd2_val", d_val, up_raw, FFN_D_FF, ps['ffn_conv_w'][:, FFN_D_FF:], tm=min(256, t),
                                 tc=1408, out_cols=2 * FFN_D_FF, out_col0=FFN_D_FF, fill=d_up_raw)
    g_ffn_conv_w = jnp.concatenate([gcw_g, gcw_v], axis=1)

    def dx1_ep(acc, i, j, ins, outs):
        x1v, wpre, dout_v, mmv, wpost = ins
        d_x1_ref, d_mm_ref, dwpre_ref, dwpost_ref = outs
        d_n, dw_pre = _rms_bwd(x1v[...], wpre[...], acc)
        d_x1 = dout_v[...] + d_n
        d_x1_ref[...] = d_x1
        d_mm, dw_post = _rms_bwd(mmv[...], wpost[...], d_x1)
        d_mm_ref[...] = d_mm.astype(BF16)
        _accumulate(dwpre_ref, i == 0, dw_pre)
        _accumulate(dwpost_ref, i == 0, dw_post)

    d_x1, d_mm, g_norm_ffn_pre, g_norm_mix_post = _mm_call(
        "d_h", d_up_raw, wb['up_t'], tm=tm, tn=D_MODEL, tk=1408, epilogue=dx1_ep,
        extra_in=[(x1, (tm, D_MODEL), i0), (ps['norm_ffn_pre_w'], (1, D_MODEL), c0), (dout, (tm, D_MODEL), i0),
                  (mmix, (tm, D_MODEL), i0), (ps['norm_mix_post_w'], (1, D_MODEL), c0)],
        outs=[((t, D_MODEL), F32, (tm, D_MODEL), i0), ((t, D_MODEL), BF16, (tm, D_MODEL), i0),
              ((1, D_MODEL), F32, (1, D_MODEL), c0), ((1, D_MODEL), F32, (1, D_MODEL), c0)])
    g_w_up_t = _mm_plain("g_w_up", d_up_raw, h, tm=512, tn=1024, tk=tm, trans_a=True)

    def dmerge_ep(acc, i, j, ins, outs):
        gs, ga, ys, ya = ins
        sg_s, sg_a = _sigmoid(gs[...]), _sigmoid(ga[...])
        outs[0][...] = (acc * sg_s).astype(BF16)
        outs[1][...] = (acc * sg_a).astype(BF16)
        outs[2][:, 0:D_MODEL] = (acc * ys[...] * sg_s * (1.0 - sg_s)).astype(BF16)
        outs[2][:, D_MODEL:2 * D_MODEL] = (acc * ya[...] * sg_a * (1.0 - sg_a)).astype(BF16)

    d_yssd, d_yattn, d_proj = _mm_call(
        "d_merged", d_mm, wb['mix_t'], tm=tm, tn=D_MODEL, tk=1024, epilogue=dmerge_ep,
        extra_in=[(proj, (tm, D_MODEL), lambda i, j, k: (i, OFF_GS // D_MODEL)),
                  (proj, (tm, D_MODEL), lambda i, j, k: (i, OFF_GA // D_MODEL)), (y_ssd, (tm, D_MODEL), i0), (y_attn, (tm, D_MODEL), i0)],
        outs=[((t, D_MODEL), BF16, (tm, D_MODEL), i0), ((t, D_MODEL), BF16, (tm, D_MODEL), i0),
              ((t, PROJ_W), BF16, (tm, 2 * D_MODEL), lambda i, j, k: (i, OFF_GS // (2 * D_MODEL)))])
    g_w_mix = _mm_plain("g_w_mix", merged, d_mm, tm=1024, tn=512, tk=tm, trans_a=True)

    def dgn_ep(acc, i, j, ins, outs):
        yv, zv, wn = ins
        d_y_ref, d_z_ref, dw_ref = outs
        zz = zv[...]
        sz = _sigmoid(zz)
        silu = zz * sz
        gv = yv[...] * silu
        r = lax.rsqrt(jnp.mean(gv * gv, axis=-1, keepdims=True) + NORM_EPS)
        gh = gv * r
        dgh = acc * wn[...]
        dg = r * (dgh - gh * jnp.mean(dgh * gh, axis=-1, keepdims=True))
        d_y_ref[...] = dg * silu
        d_z_ref[...] = (dg * yv[...] * (sz * (1.0 + zz * (1.0 - sz)))).astype(BF16)
        dw = jnp.sum(acc * gh, axis=0, keepdims=True)

        @pl.when(i == 0)
        def _():
            dw_ref[j] = dw

        @pl.when(i > 0)
        def _():
            dw_ref[j] += dw

    d_y, d_proj, g_ssd_norm = _mm_call(
        "d_gn", d_yssd, wb['so_t'], tm=tm, tn=512, tk=1024, epilogue=dgn_ep, fill=(d_proj, 1),
        extra_in=[(y, (tm, 512), ij), (proj, (tm, 512), lambda i, j, k: (i, OFF_Z // 512 + j)), (ps['ssd_norm_w'], (1, 512), cj)],
        outs=[((t, SSD_D_INNER), F32, (tm, 512), ij), ((t, PROJ_W), BF16, (tm, 512), lambda i, j, k: (i, OFF_Z // 512 + j)),
              ((SSD_N_GROUPS, 1, 512), F32, (SSD_N_GROUPS, 1, 512), lambda i, j, k: (0, 0, 0))])
    g_ssd_norm = g_ssd_norm.reshape(1, SSD_D_INNER)
    g_w_so = _mm_plain("g_w_so", gn, d_yssd, tm=1024, tn=512, tk=tm, trans_a=True)
    d_xc, d_dtg, d_alog, d_dd = _ssd_bwd(xc, dtg, dtg_t, alog, alog_t, d_exp, d_y, hs)
    d_pre, g_ssd_conv_b = _conv_silu_bwd1(d_xc, proj, ps['ssd_conv_w'], ps['ssd_conv_b'], tm=tm)
    d_proj, g_ssd_conv_w = _conv_bwd2("ssd_conv_bwd2", d_pre, proj, OFF_XBC, ps['ssd_conv_w'], tm=tm, tc=512,
                                      out_cols=PROJ_W, out_col0=OFF_XBC, fill=d_proj)
    d_dt = jnp.pad(d_dtg[:, :, :8].transpose(1, 0, 2).reshape(t, SSD_N_HEADS), ((0, 0), (0, 96)))
    d_proj, g_dt_bias = _dt_bwd(d_dt, proj, bias_pad, d_proj, tm=tmw)

    d_ao = _mm_plain("d_ao", d_yattn, wb['ao_t'], tm=tmw, tn=512, tk=1024, out_dtype=BF16)
    g_w_ao = _mm_plain("g_w_ao", ao, d_yattn, tm=1024, tn=512, tk=tm, trans_a=True)
    d_proj, g_sinks = _attn_bwd(qr, kp, vp, d_ao, ao, lse, sinks, cos_t, sin_t, d_proj)

    def dx_ep(acc, i, j, ins, outs):
        xv, wn, dx1v = ins
        d_n, dw = _rms_bwd(xv[...], wn[...], acc)
        outs[0][...] = dx1v[...] + d_n
        _accumulate(outs[1], i == 0, dw)

    grad_x, g_norm_mix_pre = _mm_call(
        "d_u", d_proj, wb['cat_t'], tm=tm, tn=D_MODEL, tk=1024, epilogue=dx_ep,
        extra_in=[(x, (tm, D_MODEL), i0), (ps['norm_mix_pre_w'], (1, D_MODEL), c0), (d_x1, (tm, D_MODEL), i0)],
        outs=[((t, D_MODEL), F32, (tm, D_MODEL), i0), ((1, D_MODEL), F32, (1, D_MODEL), c0)])
    g_cat_t = _mm_plain("g_w_in", d_proj, u, tm=1024, tn=1024, tk=tm, trans_a=True)

    grads = {
        'norm_mix_pre_w': g_norm_mix_pre, 'w_in': g_cat_t, 'ssd_conv_w': g_ssd_conv_w, 'ssd_conv_b': g_ssd_conv_b,
        'ssd_dt_bias': g_dt_bias[:, :SSD_N_HEADS], 'ssd_a_log': d_alog[:, 0, :8].reshape(1, SSD_N_HEADS),
        'ssd_d': d_dd[:, 0, :8].reshape(1, SSD_N_HEADS), 'ssd_norm_w': g_ssd_norm, 'ssd_w_out': g_w_so,
        'attn_sinks': g_sinks[:, :ATTN_N_HEADS], 'attn_w_out': g_w_ao, 'w_mix_out': g_w_mix,
        'norm_mix_post_w': g_norm_mix_post, 'norm_ffn_pre_w': g_norm_ffn_pre, 'ffn_w_up': g_w_up_t,
        'ffn_conv_w': g_ffn_conv_w, 'ffn_conv_b': jnp.concatenate([db_g, db_v], axis=1), 'ffn_w_down': g_w_down,
        'norm_ffn_post_w': g_norm_ffn_post,
    }
    return loss, grad_x, grads


def _group_channels(a):
    parts = []
    for g in range(SSD_N_GROUPS):
        parts += [a[..., 512 * g:512 * (g + 1)], a[..., 2048 + 128 * g:2048 + 128 * (g + 1)],
                  a[..., 2560 + 128 * g:2560 + 128 * (g + 1)]]
    return jnp.concatenate(parts, axis=-1)


def _ungroup_channels(a):
    xs = [a[..., GROUP_W * g:GROUP_W * g + 512] for g in range(SSD_N_GROUPS)]
    bs = [a[..., GROUP_W * g + 512:GROUP_W * g + 640] for g in range(SSD_N_GROUPS)]
    cs = [a[..., GROUP_W * g + 640:GROUP_W * (g + 1)] for g in range(SSD_N_GROUPS)]
    return jnp.concatenate(xs + bs + cs, axis=-1)


def _proj_rows(a_t, lo, hi):
    out = []
    for start, length, dst in sorted(PROJ_SEGS):
        s, e = max(lo, start), min(hi, start + length)
        if s < e:
            out.append(a_t[dst + s - start:dst + e - start])
    return out


def _to_proj_layout(w_in_t):
    pieces, pos = [], 0
    for start, length, dst in sorted(PROJ_SEGS, key=lambda s: s[2]):
        if dst > pos:
            pieces.append(jnp.zeros((dst - pos, w_in_t.shape[1]), w_in_t.dtype))
        pieces.append(w_in_t[start:start + length])
        pos = dst + length
    if pos < PROJ_W:
        pieces.append(jnp.zeros((PROJ_W - pos, w_in_t.shape[1]), w_in_t.dtype))
    return jnp.concatenate(pieces, axis=0)


def _rope_tables(positions):
    half = 32
    inv_freq = ROPE_THETA ** (-jnp.arange(half, dtype=F32) * 2.0 / 64)
    ang = positions.astype(F32)[:, None] * inv_freq
    cos, sin = jnp.cos(ang), jnp.sin(ang)
    return jnp.concatenate([cos, cos, cos, cos], axis=1), jnp.concatenate([-sin, sin, -sin, sin], axis=1)


def _matmul_weights(w_in_t, so, ao, mix, up_t, dn):
    cat_t = _to_proj_layout(w_in_t)
    return {'cat': cat_t.T, 'cat_t': cat_t, 'so': so, 'so_t': so.T, 'ao': ao, 'ao_t': ao.T, 'mix': mix, 'mix_t': mix.T,
            'up': up_t.T, 'up_t': up_t, 'dn': dn, 'dn_t': dn.T}


ANY = pl.BlockSpec(memory_space=pl.ANY)
MESH = pl.DeviceIdType.MESH
ROW_ALIGN = 256


def _mesh_pos():
    return lax.axis_index("x"), lax.axis_index("y"), lax.axis_index("c")


def _other_chips(x, y):
    return [(1 - x, y), (x, 1 - y), (1 - x, 1 - y)]


def _remote(src, dst, send_sems, recv_sems, k, to):
    return pltpu.make_async_remote_copy(src_ref=src, dst_ref=dst, send_sem=send_sems.at[k], recv_sem=recv_sems.at[k],
                                        device_id=to, device_id_type=MESH)


def _half(c, rh):
    return pl.ds(pl.multiple_of(c * rh, 128), rh)


def _all_gather(shard):
    r = shard.shape[0]
    rh = r // 2

    def body(w_ref, out_ref, send_sems, recv_sems, local_sem):
        x, y, c = _mesh_pos()
        p = 2 * x + y
        sib = (x, y, 1 - c)
        mine, other = _half(c, rh), _half(1 - c, rh)
        chips = _other_chips(x, y)
        local = pltpu.make_async_copy(w_ref, out_ref.at[p], local_sem)
        local.start()
        first = [_remote(w_ref.at[mine], out_ref.at[p, mine], send_sems, recv_sems, j, (cx, cy, c))
                 for j, (cx, cy) in enumerate(chips)]
        for cp in first:
            cp.start()
        passed = []
        for j, (cx, cy) in enumerate(chips):
            slab = out_ref.at[2 * cx + cy, mine]
            _remote(slab, slab, send_sems, recv_sems, j, sib).wait_recv()
            fwd = _remote(slab, slab, send_sems, recv_sems, 3 + j, sib)
            fwd.start()
            passed.append(fwd)
        for j, (cx, cy) in enumerate(chips):
            slab = out_ref.at[2 * cx + cy, other]
            _remote(slab, slab, send_sems, recv_sems, 3 + j, sib).wait_recv()
        for cp in first + passed:
            cp.wait_send()
        local.wait()

    return pl.pallas_call(
        body, name="weights_all_gather", in_specs=[ANY], out_specs=ANY,
        out_shape=SDS((N_CHIPS, r, COMM_LANES), shard.dtype),
        scratch_shapes=[pltpu.SemaphoreType.DMA((6,)), pltpu.SemaphoreType.DMA((6,)), pltpu.SemaphoreType.DMA],
    )(shard)


def _pair_swap(g):
    rh = g.shape[1] // 2

    def body(g_ref, out_ref, send_sems, recv_sems):
        x, y, c = _mesh_pos()
        cp = _remote(g_ref.at[:, _half(1 - c, rh)], out_ref, send_sems, recv_sems, 0, (x, y, 1 - c))
        cp.start()
        cp.wait()

    return pl.pallas_call(
        body, name="grad_pair_swap", in_specs=[ANY], out_specs=ANY,
        out_shape=SDS((N_CHIPS, rh, COMM_LANES), g.dtype),
        scratch_shapes=[pltpu.SemaphoreType.DMA((1,)), pltpu.SemaphoreType.DMA((1,))],
    )(g)


def _chip_exchange(part):
    rh = part.shape[1]

    def body(p_ref, out_ref, send_sems, recv_sems):
        x, y, c = _mesh_pos()
        cps = [_remote(p_ref.at[2 * cx + cy], out_ref.at[j], send_sems, recv_sems, j, (cx, cy, c))
               for j, (cx, cy) in enumerate(_other_chips(x, y))]
        for cp in cps:
            cp.start()
        for cp in cps:
            cp.wait()

    return pl.pallas_call(
        body, name="grad_chip_exchange", in_specs=[ANY], out_specs=ANY,
        out_shape=SDS((3, rh, COMM_LANES), part.dtype),
        scratch_shapes=[pltpu.SemaphoreType.DMA((3,)), pltpu.SemaphoreType.DMA((3,))],
    )(part)


def _pair_gather(halfbuf):
    rh = halfbuf.shape[0]

    def body(h_ref, out_ref, send_sems, recv_sems, local_sem):
        x, y, c = _mesh_pos()
        local = pltpu.make_async_copy(h_ref, out_ref.at[c], local_sem)
        local.start()
        cp = _remote(h_ref, out_ref.at[c], send_sems, recv_sems, 0, (x, y, 1 - c))
        cp.start()
        _remote(h_ref, out_ref.at[1 - c], send_sems, recv_sems, 0, (x, y, 1 - c)).wait_recv()
        cp.wait_send()
        local.wait()

    return pl.pallas_call(
        body, name="grad_pair_gather", in_specs=[ANY], out_specs=ANY,
        out_shape=SDS((2, rh, COMM_LANES), halfbuf.dtype),
        scratch_shapes=[pltpu.SemaphoreType.DMA((1,)), pltpu.SemaphoreType.DMA((1,)), pltpu.SemaphoreType.DMA],
    )(halfbuf)


def _pair_sum(g, got, c_idx, *, tr=384):
    rh = got.shape[1]
    nb = rh // tr

    def body(c_ref, a_ref, b_ref, o_ref):
        o_ref[...] = (a_ref[...] + b_ref[...]).astype(BF16)

    return pl.pallas_call(
        body, name="grad_pair_sum",
        grid_spec=pltpu.PrefetchScalarGridSpec(
            num_scalar_prefetch=1, grid=(N_CHIPS, nb),
            in_specs=[pl.BlockSpec((1, tr, COMM_LANES), lambda s, i, c_ref: (s, c_ref[0] * nb + i, 0)),
                      pl.BlockSpec((1, tr, COMM_LANES), lambda s, i, c_ref: (s, i, 0))],
            out_specs=pl.BlockSpec((1, tr, COMM_LANES), lambda s, i, c_ref: (s, i, 0))),
        out_shape=SDS(got.shape, BF16), compiler_params=_cp(dimension_semantics=("arbitrary", "arbitrary")),
    )(c_idx, g, got)


def _chip_sum(part, got, p_idx, *, tr=384):
    rh = part.shape[1]

    def body(p_ref, own_ref, r0_ref, r1_ref, r2_ref, o_ref):
        p = p_ref[0]
        own, r0, r1, r2 = (r[0].astype(F32) for r in (own_ref, r0_ref, r1_ref, r2_ref))

        def term(q):
            code = p ^ q
            return jnp.where(code == 0, own, jnp.where(code == 2, r0, jnp.where(code == 1, r1, r2)))

        o_ref[...] = ((term(0) + term(1)) + term(2)) + term(3)

    slab = lambda j: pl.BlockSpec((1, tr, COMM_LANES), lambda i, p_ref: (j, i, 0))
    return pl.pallas_call(
        body, name="grad_chip_sum",
        grid_spec=pltpu.PrefetchScalarGridSpec(
            num_scalar_prefetch=1, grid=(rh // tr,),
            in_specs=[pl.BlockSpec((1, tr, COMM_LANES), lambda i, p_ref: (p_ref[0], i, 0)), slab(0), slab(1), slab(2)],
            out_specs=pl.BlockSpec((tr, COMM_LANES), lambda i, p_ref: (i, 0))),
        out_shape=SDS((rh, COMM_LANES), F32), compiler_params=_cp(dimension_semantics=("arbitrary",)),
    )(p_idx, part, got, got, got)


def _pack_rows(big, small):
    flat = jnp.concatenate([p.reshape(-1) for p in small])
    k = -(-flat.shape[0] // COMM_LANES)
    tail = jnp.pad(flat, (0, k * COMM_LANES - flat.shape[0])).reshape(k, COMM_LANES)
    rows = sum(p.shape[0] for p in big) + k
    pad = -rows % ROW_ALIGN
    return jnp.concatenate(list(big) + [tail, jnp.zeros((pad, COMM_LANES), tail.dtype)], axis=0)


def _take(flat, off, shape):
    n = 1
    for d in shape:
        n *= d
    return flat[off:off + n].reshape(shape), off + n


BIG_ROWS = (('w_in', 2184), ('ssd_w_out', 512), ('attn_w_out', 256), ('w_mix_out', 256), ('ffn_w_up', 1408), ('ffn_w_down', 704))
TRANSPOSED = ('w_in', 'ffn_w_up')


def kernel(x, positions, norm_mix_pre_w, w_in, ssd_conv_w, ssd_conv_b, ssd_dt_bias, ssd_a_log, ssd_d, ssd_norm_w, ssd_w_out, attn_sinks, attn_w_out, w_mix_out, norm_mix_post_w, norm_ffn_pre_w, ffn_w_up, ffn_conv_w, ffn_conv_b, ffn_w_down, norm_ffn_post_w, loss_target, m_norm_mix_pre_w, m_w_in, m_ssd_conv_w, m_ssd_conv_b, m_ssd_dt_bias, m_ssd_a_log, m_ssd_d, m_ssd_norm_w, m_ssd_w_out, m_attn_sinks, m_attn_w_out, m_w_mix_out, m_norm_mix_post_w, m_norm_ffn_pre_w, m_ffn_w_up, m_ffn_conv_w, m_ffn_conv_b, m_ffn_w_down, m_norm_ffn_post_w, v_norm_mix_pre_w, v_w_in, v_ssd_conv_w, v_ssd_conv_b, v_ssd_dt_bias, v_ssd_a_log, v_ssd_d, v_ssd_norm_w, v_ssd_w_out, v_attn_sinks, v_attn_w_out, v_w_mix_out, v_norm_mix_post_w, v_norm_ffn_pre_w, v_ffn_w_up, v_ffn_conv_w, v_ffn_conv_b, v_ffn_w_down, v_norm_ffn_post_w):
    given = dict(locals())
    w = {n: given[n][0] for n in WEIGHTS}
    w = {n: (a if a.ndim == 2 else a[None]) for n, a in w.items()}
    mom_m = {n: given['m_' + n].reshape(w[n].shape) for n in WEIGHTS}
    mom_v = {n: given['v_' + n].reshape(w[n].shape) for n in WEIGHTS}
    cx, cy, cc = _mesh_pos()
    c_idx = cc.astype(jnp.int32).reshape(1)
    p_idx = (2 * cx + cy).astype(jnp.int32).reshape(1)

    big = [(w[n].T if n in TRANSPOSED else w[n]).astype(BF16) for n, _ in BIG_ROWS]
    taps = [lax.bitcast_convert_type(w[n], BF16) for n in ('ssd_conv_w', 'ffn_conv_w')]
    gathered = _all_gather(_pack_rows(big, taps))
    rows = {n: [] for n, _ in BIG_ROWS}
    conv = {'ssd_conv_w': [], 'ffn_conv_w': []}
    for s in range(N_CHIPS):
        r0 = 0
        for n, nr in BIG_ROWS:
            rows[n].append(gathered[s, r0:r0 + nr])
            r0 += nr
        flat, off = gathered[s, r0:r0 + 16].reshape(-1), 0
        for n in conv:
            a, off = _take(flat, off, w[n].shape + (2,))
            conv[n].append(lax.bitcast_convert_type(a, F32))
    wfull = {n: jnp.concatenate(rows[n], axis=0) for n in rows}
    wb = _matmul_weights(wfull['w_in'], wfull['ssd_w_out'], wfull['attn_w_out'], wfull['w_mix_out'], wfull['ffn_w_up'],
                         wfull['ffn_w_down'])

    ps = {n: w[n] for n in REPLICATED}
    ps['ssd_conv_w'] = _group_channels(jnp.concatenate(conv['ssd_conv_w'], axis=1))
    ps['ssd_conv_b'] = _group_channels(w['ssd_conv_b'])
    ps['ffn_conv_w'] = jnp.concatenate(conv['ffn_conv_w'], axis=1)
    cos_t, sin_t = _rope_tables(positions[0])
    loss, grad_x, grads = _local_step(x[0], cos_t, sin_t, loss_target[0], wb, ps)
    grads['ssd_conv_w'] = _ungroup_channels(grads['ssd_conv_w'])
    grads['ssd_conv_b'] = _ungroup_channels(grads['ssd_conv_b'])

    shard_shapes = {n: sh for n, _, sh in SHARDED}
    slabs = []
    for s in range(N_CHIPS):
        bigs = []
        for n, nr in BIG_ROWS:
            bigs += _proj_rows(grads[n], nr * s, nr * (s + 1)) if n == 'w_in' else [grads[n][nr * s:nr * (s + 1)]]
        small = [grads[n][:, shard_shapes[n][1] * s:shard_shapes[n][1] * (s + 1)] for n in ('ssd_conv_w', 'ffn_conv_w')]
        slabs.append(_pack_rows(bigs, small + [grads[n] for n in REPLICATED]))
    gbuf = jnp.stack(slabs)
    pair = _pair_sum(gbuf, _pair_swap(gbuf), c_idx)
    mine = _chip_sum(pair, _chip_exchange(pair), p_idx)
    red = _pair_gather(mine).reshape(gbuf.shape[1], COMM_LANES)
    g_red, r0 = {}, 0
    for n, nr in BIG_ROWS:
        g_red[n] = red[r0:r0 + nr].T if n in TRANSPOSED else red[r0:r0 + nr]
        r0 += nr
    flat, off = red[r0:].reshape(-1), 0
    for n in ('ssd_conv_w', 'ffn_conv_w') + REPLICATED:
        g_red[n], off = _take(flat, off, w[n].shape)

    small_names = [n for n in WEIGHTS if n not in MATMUL_WEIGHTS]
    delta, new_m, new_v = {}, {}, {}
    for n in MATMUL_WEIGHTS:
        delta[n], new_m[n], new_v[n] = _adamw("adamw_" + n, w[n], g_red[n], mom_m[n], mom_v[n], tr=64)
    packed = [_pack_small([d[n] for n in small_names]) for d in (w, g_red, mom_m, mom_v)]
    outs = _adamw("adamw_small", *packed, tr=packed[0].shape[0])
    for res, o in zip((delta, new_m, new_v), outs):
        fl, off = o.reshape(-1), 0
        for n in small_names:
            res[n], off = _take(fl, off, w[n].shape)

    loss_all = lax.psum(loss[0, 0], ("x", "y", "c"))
    shaped = lambda d: [d[n].reshape(given[n].shape) for n in WEIGHTS]
    return (loss_all, grad_x[None], *shaped(g_red), *shaped(delta), *shaped(new_m), *shaped(new_v))


def _pack_small(pieces):
    flat = jnp.concatenate([p.reshape(-1) for p in pieces])
    rows = -(-flat.shape[0] // (128 * 8)) * 8
    return jnp.pad(flat, (0, rows * 128 - flat.shape[0])).reshape(rows, 128)
```

```python
import functools

import jax
import jax.numpy as jnp
from jax import lax
from jax.experimental import pallas as pl
from jax.experimental.pallas import tpu as pltpu

F32 = jnp.float32
BF16 = jnp.bfloat16
SDS = jax.ShapeDtypeStruct
HIGHEST = lax.Precision.HIGHEST

D_MODEL = 1024
SSD_D_INNER = 2048
SSD_N_HEADS = 32
SSD_HEAD_DIM = 64
SSD_N_GROUPS = 4
SSD_HEADS_PER_GROUP = 8
SSD_D_STATE = 128
SSD_CONV_DIM = 3072
CHUNK = 128
ATTN_N_HEADS = 16
KV_WIDTH = 256
FFN_D_FF = 2816
IN_PROJ_DIM = 8736
ROPE_THETA = 10000.0
NORM_EPS = 1e-6
ADAM_LR, ADAM_B1, ADAM_B2, ADAM_EPS, ADAM_WD, ADAM_STEP = 0.001, 0.9, 0.999, 1e-08, 0.01, 10

PROJ_W = 9216
OFF_Q, OFF_K, OFF_V, OFF_Z, OFF_DT, OFF_GS, OFF_GA, OFF_XBC = 0, 1024, 1280, 1536, 3584, 4096, 5120, 6144
GROUP_W = 768
PROJ_SEGS = ([(0, 2048, OFF_Z)]
             + [(2048 + 512 * g, 512, OFF_XBC + GROUP_W * g) for g in range(4)]
             + [(4096 + 128 * g, 128, OFF_XBC + GROUP_W * g + 512) for g in range(4)]
             + [(4608 + 128 * g, 128, OFF_XBC + GROUP_W * g + 640) for g in range(4)]
             + [(5120, 32, OFF_DT), (5152, 1024, OFF_Q), (6176, 256, OFF_K), (6432, 256, OFF_V),
                (6688, 1024, OFF_GS), (7712, 1024, OFF_GA)])
VMEM_LIMIT_MB = 48
NEG = -1e30

WEIGHTS = ('norm_mix_pre_w', 'w_in', 'ssd_conv_w', 'ssd_conv_b', 'ssd_dt_bias', 'ssd_a_log', 'ssd_d', 'ssd_norm_w',
           'ssd_w_out', 'attn_sinks', 'attn_w_out', 'w_mix_out', 'norm_mix_post_w', 'norm_ffn_pre_w', 'ffn_w_up',
           'ffn_conv_w', 'ffn_conv_b', 'ffn_w_down', 'norm_ffn_post_w')
SHARDED = (('w_in', 1, (1024, 2184)), ('ssd_conv_w', 1, (4, 768)), ('ssd_w_out', 0, (512, 1024)),
           ('attn_w_out', 0, (256, 1024)), ('w_mix_out', 0, (256, 1024)), ('ffn_w_up', 1, (1024, 1408)),
           ('ffn_conv_w', 1, (3, 1408)), ('ffn_w_down', 0, (704, 1024)))
MATMUL_WEIGHTS = ('w_in', 'ssd_w_out', 'attn_w_out', 'w_mix_out', 'ffn_w_up', 'ffn_w_down')
REPLICATED = tuple(n for n in WEIGHTS if n not in {s[0] for s in SHARDED})
N_CHIPS = 4
COMM_LANES = 1024


def _cp(vmem_mb=VMEM_LIMIT_MB, **kw):
    return pltpu.CompilerParams(vmem_limit_bytes=vmem_mb << 20, **kw)


def _iota(shape, axis):
    return lax.broadcasted_iota(jnp.int32, shape, axis)


def _sigmoid(v):
    return 1.0 / (1.0 + jnp.exp(-v))


def _mm_call(name, a, b, *, tm, tn, tk, epilogue, outs, extra_in=(), trans_a=False, fill=None):
    if trans_a:
        kdim, m = a.shape
    else:
        m, kdim = a.shape
    n = b.shape[1]
    assert b.shape[0] == kdim and m % tm == 0 and n % tn == 0 and kdim % tk == 0, (name, a.shape, b.shape, tm, tn, tk)
    gi, gj, gk = m // tm, n // tn, kdim // tk
    n_in, n_out = len(extra_in), len(outs)

    n_fill = 0 if fill is None else 1

    def body(a_ref, b_ref, *rest):
        ins = rest[:n_in]
        rest = rest[n_in + n_fill:]
        out_refs, scratch = rest[:n_out], rest[n_out:]
        i, j, k = pl.program_id(0), pl.program_id(1), pl.program_id(2)
        av = a_ref[...].astype(BF16)
        bv = b_ref[...].astype(BF16)
        if trans_a:
            part = lax.dot_general(av, bv, (((0,), (0,)), ((), ())), preferred_element_type=F32)
        else:
            part = jnp.dot(av, bv, preferred_element_type=F32)
        if gk == 1:
            epilogue(part, i, j, ins, out_refs)
        else:
            acc = scratch[0]

            @pl.when(k == 0)
            def _():
                acc[...] = part

            @pl.when(k > 0)
            def _():
                acc[...] += part

            @pl.when(k == gk - 1)
            def _():
                epilogue(acc[...], i, j, ins, out_refs)

    a_spec = pl.BlockSpec((tk, tm), lambda i, j, k: (k, i)) if trans_a else pl.BlockSpec((tm, tk), lambda i, j, k: (i, k))
    in_specs = [a_spec, pl.BlockSpec((tk, tn), lambda i, j, k: (k, j))]
    in_specs += [pl.BlockSpec(bs, im) for _, bs, im in extra_in]
    operands = [a, b] + [e[0] for e in extra_in]
    aliases = {}
    if fill is not None:
        in_specs.append(pl.BlockSpec(memory_space=pl.ANY))
        aliases = {len(operands): fill[1]}
        operands.append(fill[0])
    return pl.pallas_call(
        body, name=name, grid=(gi, gj, gk), in_specs=in_specs,
        out_specs=[pl.BlockSpec(bs, im) for _, _, bs, im in outs],
        out_shape=[SDS(s, d) for s, d, _, _ in outs],
        scratch_shapes=[pltpu.VMEM((tm, tn), F32)] if gk > 1 else [],
        input_output_aliases=aliases,
        compiler_params=_cp(dimension_semantics=("arbitrary", "arbitrary", "arbitrary")),
    )(*operands)


def _mm_plain(name, a, b, *, tm, tn, tk, out_dtype=F32, trans_a=False):
    m = a.shape[1] if trans_a else a.shape[0]

    def epilogue(acc, i, j, ins, outs):
        outs[0][...] = acc.astype(out_dtype)

    return _mm_call(name, a, b, tm=tm, tn=tn, tk=tk, epilogue=epilogue, trans_a=trans_a,
                    outs=[((m, b.shape[1]), out_dtype, (tm, tn), lambda i, j, k: (i, j))])[0]


def _accumulate(ref, first, value):
    @pl.when(first)
    def _():
        ref[...] = value

    @pl.when(jnp.logical_not(first))
    def _():
        ref[...] += value


def _rms_bwd(xv, w, dy):
    r = lax.rsqrt(jnp.mean(xv * xv, axis=-1, keepdims=True) + NORM_EPS)
    xn = xv * r
    dxh = dy * w
    dx = r * (dxh - xn * jnp.mean(dxh * xn, axis=-1, keepdims=True))
    return dx, jnp.sum(dy * xn, axis=0, keepdims=True)


def _norm_mm(name, x, wn, w, *, tm, tn):
    t, dm = x.shape
    n = w.shape[1]
    tm = min(tm, t)

    def body(x_ref, wn_ref, w_ref, o_ref, u_ref):
        @pl.when(pl.program_id(1) == 0)
        def _():
            xv = x_ref[...]
            r = lax.rsqrt(jnp.mean(xv * xv, axis=-1, keepdims=True) + NORM_EPS)
            u_ref[...] = (xv * r * wn_ref[...]).astype(BF16)

        o_ref[...] = jnp.dot(u_ref[...], w_ref[...], preferred_element_type=F32)

    return pl.pallas_call(
        body, name=name, grid=(t // tm, n // tn),
        in_specs=[pl.BlockSpec((tm, dm), lambda i, j: (i, 0)), pl.BlockSpec((1, dm), lambda i, j: (0, 0)),
                  pl.BlockSpec((dm, tn), lambda i, j: (0, j))],
        out_specs=[pl.BlockSpec((tm, tn), lambda i, j: (i, j)), pl.BlockSpec((tm, dm), lambda i, j: (i, 0))],
        out_shape=[SDS((t, n), F32), SDS((t, dm), BF16)],
        compiler_params=_cp(dimension_semantics=("arbitrary", "arbitrary")),
    )(x, wn, w)


def _shift_down(tile, halo, s):
    if s == 0:
        return tile
    r = pltpu.roll(tile, s, axis=0)
    h = pltpu.roll(halo, s, axis=0)
    head = jnp.where(_iota(h.shape, 0) < s, h, r[0:8])
    return jnp.concatenate([head, r[8:]], axis=0)


def _shift_up(tile, halo, s):
    if s == 0:
        return tile
    n = tile.shape[0]
    r = pltpu.roll(tile, n - s, axis=0)
    h = pltpu.roll(halo, 8 - s, axis=0)
    tail = jnp.where(_iota(h.shape, 0) >= 8 - s, h, r[n - 8:])
    return jnp.concatenate([r[:n - 8], tail], axis=0)


def _conv_apply(tile, halo, wv, bv, kw):
    acc = bv + wv[kw - 1:kw, :] * tile
    for k in range(kw - 1):
        acc = acc + wv[k:k + 1, :] * _shift_down(tile, halo, kw - 1 - k)
    return acc


def _prev_halo_spec(tm, tc, col0):
    return pl.BlockSpec((8, tc), lambda i, j: (jnp.maximum(i * (tm // 8) - 1, 0), col0 + j))


def _silu_parts(pre):
    sg = _sigmoid(pre)
    return pre * sg, sg * (1.0 + pre * (1.0 - sg))


def _conv_silu_fwd(proj, w, b, *, tm, tc=512):
    t = proj.shape[0]
    c = w.shape[1]
    tm = min(tm, t)
    col0 = OFF_XBC // tc

    def body(x_ref, h_ref, w_ref, b_ref, o_ref):
        halo = jnp.where(pl.program_id(0) > 0, h_ref[...], 0.0)
        o_ref[...] = _silu_parts(_conv_apply(x_ref[...], halo, w_ref[...], b_ref[...], 4))[0]

    return pl.pallas_call(
        body, name="ssd_conv_fwd", grid=(t // tm, c // tc),
        in_specs=[pl.BlockSpec((tm, tc), lambda i, j: (i, col0 + j)), _prev_halo_spec(tm, tc, col0),
                  pl.BlockSpec((4, tc), lambda i, j: (0, j)), pl.BlockSpec((1, tc), lambda i, j: (0, j))],
        out_specs=pl.BlockSpec((tm, tc), lambda i, j: (i, j)),
        out_shape=SDS((t, c), F32),
        compiler_params=_cp(dimension_semantics=("arbitrary", "arbitrary")),
    )(proj, proj, w, b)


def _conv_silu_bwd1(d_out, proj, w, b, *, tm, tc=512):
    t = proj.shape[0]
    c = w.shape[1]
    tm = min(tm, t)
    col0 = OFF_XBC // tc

    def body(g_ref, x_ref, h_ref, w_ref, b_ref, o_ref, db_ref):
        i = pl.program_id(1)
        halo = jnp.where(i > 0, h_ref[...], 0.0)
        d_pre = g_ref[...] * _silu_parts(_conv_apply(x_ref[...], halo, w_ref[...], b_ref[...], 4))[1]
        o_ref[...] = d_pre
        _accumulate(db_ref, i == 0, jnp.sum(d_pre, axis=0, keepdims=True))

    return pl.pallas_call(
        body, name="ssd_conv_bwd1", grid=(c // tc, t // tm),
        in_specs=[pl.BlockSpec((tm, tc), lambda j, i: (i, j)), pl.BlockSpec((tm, tc), lambda j, i: (i, col0 + j)),
                  pl.BlockSpec((8, tc), lambda j, i: (jnp.maximum(i * (tm // 8) - 1, 0), col0 + j)),
                  pl.BlockSpec((4, tc), lambda j, i: (0, j)), pl.BlockSpec((1, tc), lambda j, i: (0, j))],
        out_specs=[pl.BlockSpec((tm, tc), lambda j, i: (i, j)), pl.BlockSpec((1, tc), lambda j, i: (0, j))],
        out_shape=[SDS((t, c), F32), SDS((1, c), F32)],
        compiler_params=_cp(dimension_semantics=("arbitrary", "arbitrary")),
    )(d_out, proj, proj, w, b)


def _conv_bwd2(name, d_pre, src, src_col0, w, *, tm, tc, out_cols, out_col0, fill=None):
    t, c = d_pre.shape
    kw = w.shape[0]
    tm = min(tm, t)
    ni = t // tm
    col0 = src_col0 // tc
    ocol0 = out_col0 // tc

    def body(g_ref, gn_ref, x_ref, xh_ref, w_ref, *rest):
        o_ref, dw_ref = rest[-2:]
        i = pl.program_id(1)
        g = g_ref[...]
        g_next = jnp.where(i < ni - 1, gn_ref[...], 0.0)
        x_prev = jnp.where(i > 0, xh_ref[...], 0.0)
        xv = x_ref[...]
        wv = w_ref[...]
        d_in = wv[kw - 1:kw, :] * g
        for k in range(kw - 1):
            d_in = d_in + wv[k:k + 1, :] * _shift_up(g, g_next, kw - 1 - k)
        o_ref[...] = d_in.astype(o_ref.dtype)
        rows = [jnp.sum(g * _shift_down(xv, x_prev, kw - 1 - k), axis=0, keepdims=True) for k in range(kw)]

        @pl.when(i == 0)
        def _():
            for k in range(kw):
                dw_ref[k:k + 1, :] = rows[k]

        @pl.when(i > 0)
        def _():
            for k in range(kw):
                dw_ref[k:k + 1, :] += rows[k]

    in_specs = [pl.BlockSpec((tm, tc), lambda j, i: (i, j)),
                pl.BlockSpec((8, tc), lambda j, i: (jnp.minimum((i + 1) * (tm // 8), t // 8 - 1), j)),
                pl.BlockSpec((tm, tc), lambda j, i: (i, col0 + j)),
                pl.BlockSpec((8, tc), lambda j, i: (jnp.maximum(i * (tm // 8) - 1, 0), col0 + j)),
                pl.BlockSpec((kw, tc), lambda j, i: (0, j))]
    operands = [d_pre, d_pre, src, src, w]
    if fill is not None:
        in_specs.append(pl.BlockSpec(memory_space=pl.ANY))
        operands.append(fill)
    return pl.pallas_call(
        body, name=name, grid=(c // tc, ni), in_specs=in_specs,
        out_specs=[pl.BlockSpec((tm, tc), lambda j, i: (i, ocol0 + j)), pl.BlockSpec((kw, tc), lambda j, i: (0, j))],
        out_shape=[SDS((t, out_cols), BF16), SDS((kw, c), F32)],
        input_output_aliases={} if fill is None else {5: 0},
        compiler_params=_cp(dimension_semantics=("arbitrary", "arbitrary")),
    )(*operands)


GELU_C = 0.7978845608028654


def _gelu_parts(v):
    inner = GELU_C * (v + 0.044715 * v * v * v)
    th = jnp.tanh(inner)
    val = 0.5 * v * (1.0 + th)
    grad = 0.5 * (1.0 + th) + 0.5 * v * (1.0 - th * th) * GELU_C * (1.0 + 3.0 * 0.044715 * v * v)
    return val, grad


def _ffn_act_specs(tm, tc, nj, order):
    def im(f):
        return (lambda i, j: f(i, j)) if order == "ij" else (lambda j, i: f(i, j))
    halo = lambda i: jnp.maximum(i * (tm // 8) - 1, 0)
    return [pl.BlockSpec((tm, tc), im(lambda i, j: (i, j))), pl.BlockSpec((8, tc), im(lambda i, j: (halo(i), j))),
            pl.BlockSpec((tm, tc), im(lambda i, j: (i, nj + j))), pl.BlockSpec((8, tc), im(lambda i, j: (halo(i), nj + j))),
            pl.BlockSpec((3, tc), im(lambda i, j: (0, j))), pl.BlockSpec((3, tc), im(lambda i, j: (0, nj + j))),
            pl.BlockSpec((1, tc), im(lambda i, j: (0, j))), pl.BlockSpec((1, tc), im(lambda i, j: (0, nj + j)))]


def _ffn_act_fwd(up_raw, w, b, *, tm, tc=1408):
    t = up_raw.shape[0]
    tm = min(tm, t)
    nj = FFN_D_FF // tc

    def body(g_ref, gh_ref, v_ref, vh_ref, wg_ref, wv_ref, bg_ref, bv_ref, o_ref):
        first = pl.program_id(0) > 0
        gate = _conv_apply(g_ref[...], jnp.where(first, gh_ref[...], 0.0), wg_ref[...], bg_ref[...], 3)
        val = _conv_apply(v_ref[...], jnp.where(first, vh_ref[...], 0.0), wv_ref[...], bv_ref[...], 3)
        o_ref[...] = (_gelu_parts(gate)[0] * val).astype(BF16)

    return pl.pallas_call(
        body, name="ffn_act_fwd", grid=(t // tm, nj), in_specs=_ffn_act_specs(tm, tc, nj, "ij"),
        out_specs=pl.BlockSpec((tm, tc), lambda i, j: (i, j)), out_shape=SDS((t, FFN_D_FF), BF16),
        compiler_params=_cp(dimension_semantics=("arbitrary", "arbitrary")),
    )(up_raw, up_raw, up_raw, up_raw, w, w, b, b)


def _ffn_act_bwd(up_raw, d_act, w, b, *, tm, tc=1408):
    t = up_raw.shape[0]
    tm = min(tm, t)
    nj = FFN_D_FF // tc

    def body(g_ref, gh_ref, v_ref, vh_ref, wg_ref, wv_ref, bg_ref, bv_ref, da_ref, dg_ref, dv_ref, dbg_ref, dbv_ref):
        i = pl.program_id(1)
        gate = _conv_apply(g_ref[...], jnp.where(i > 0, gh_ref[...], 0.0), wg_ref[...], bg_ref[...], 3)
        val = _conv_apply(v_ref[...], jnp.where(i > 0, vh_ref[...], 0.0), wv_ref[...], bv_ref[...], 3)
        ge, dge = _gelu_parts(gate)
        da = da_ref[...]
        d_gate = da * val * dge
        d_val = da * ge
        dg_ref[...] = d_gate
        dv_ref[...] = d_val
        _accumulate(dbg_ref, i == 0, jnp.sum(d_gate, axis=0, keepdims=True))
        _accumulate(dbv_ref, i == 0, jnp.sum(d_val, axis=0, keepdims=True))

    tile = pl.BlockSpec((tm, tc), lambda j, i: (i, j))
    row = pl.BlockSpec((1, tc), lambda j, i: (0, j))
    return pl.pallas_call(
        body, name="ffn_act_bwd", grid=(nj, t // tm), in_specs=_ffn_act_specs(tm, tc, nj, "ji") + [tile],
        out_specs=[tile, tile, row, row],
        out_shape=[SDS((t, FFN_D_FF), F32), SDS((t, FFN_D_FF), F32), SDS((1, FFN_D_FF), F32), SDS((1, FFN_D_FF), F32)],
        compiler_params=_cp(dimension_semantics=("arbitrary", "arbitrary")),
    )(up_raw, up_raw, up_raw, up_raw, w, w, b, b, d_act)


def _softplus(v):
    e = jnp.exp(-jnp.abs(v))
    small = e * (1.0 - 0.5 * e)
    return jnp.maximum(v, 0.0) + jnp.where(e < 1e-4, small, jnp.log(1.0 + e))


def _dt_fwd(proj, bias_pad, *, tm):
    t = proj.shape[0]
    tm = min(tm, t)

    def body(x_ref, b_ref, o_ref):
        o_ref[...] = _softplus(x_ref[...] + b_ref[...])

    return pl.pallas_call(
        body, name="dt_fwd", grid=(t // tm,),
        in_specs=[pl.BlockSpec((tm, 128), lambda i: (i, OFF_DT // 128)), pl.BlockSpec((1, 128), lambda i: (0, 0))],
        out_specs=pl.BlockSpec((tm, 128), lambda i: (i, 0)), out_shape=SDS((t, 128), F32),
        compiler_params=_cp(dimension_semantics=("arbitrary",)),
    )(proj, bias_pad)


def _dt_bwd(d_dt, proj, bias_pad, d_proj, *, tm):
    t = proj.shape[0]
    tm = min(tm, t)

    def body(g_ref, x_ref, b_ref, _, o_ref, db_ref):
        d_raw = g_ref[...] * _sigmoid(x_ref[...] + b_ref[...])
        o_ref[:, 0:128] = d_raw.astype(BF16)
        o_ref[:, 128:512] = jnp.zeros((tm, 384), BF16)
        _accumulate(db_ref, pl.program_id(0) == 0, jnp.sum(d_raw, axis=0, keepdims=True))

    return pl.pallas_call(
        body, name="dt_bwd", grid=(t // tm,),
        in_specs=[pl.BlockSpec((tm, 128), lambda i: (i, 0)), pl.BlockSpec((tm, 128), lambda i: (i, OFF_DT // 128)),
                  pl.BlockSpec((1, 128), lambda i: (0, 0)), pl.BlockSpec(memory_space=pl.ANY)],
        out_specs=[pl.BlockSpec((tm, 512), lambda i: (i, OFF_DT // 512)), pl.BlockSpec((1, 128), lambda i: (0, 0))],
        out_shape=[SDS((t, PROJ_W), BF16), SDS((1, 128), F32)],
        input_output_aliases={3: 0},
        compiler_params=_cp(dimension_semantics=("arbitrary",)),
    )(d_dt, proj, bias_pad, d_proj)


def _split3(v):
    hi = v.astype(BF16)
    r1 = v - hi.astype(F32)
    mid = r1.astype(BF16)
    return hi, mid, (r1 - mid.astype(F32)).astype(BF16)


def _times01(v, m3):
    return jnp.dot(jnp.concatenate(_split3(v), axis=1), m3, preferred_element_type=F32)


def _01times(m3, v):
    return jnp.dot(m3, jnp.concatenate(_split3(v), axis=0), preferred_element_type=F32)


def _ssd_decay(dt_ref, dtT_ref, al_ref, alT_ref, k):
    dt = dt_ref[0]
    a_row = -jnp.exp(al_ref[0])
    adt_t = dtT_ref[0] * (-jnp.exp(alT_ref[0]))
    return dt, a_row, _01times(k['low3'][...], dt * a_row), _times01(adt_t, k['up3v'][...])


def _ssd_specs(nc, rev):
    ci = (lambda c: nc - 1 - c) if rev else (lambda c: c)
    return [pl.BlockSpec((CHUNK, GROUP_W), lambda c, g: (ci(c), g)),
            pl.BlockSpec((1, CHUNK, 128), lambda c, g: (g, ci(c), 0)),
            pl.BlockSpec((1, 8, CHUNK), lambda c, g: (g, 0, ci(c))),
            pl.BlockSpec((1, 1, 128), lambda c, g: (g, 0, 0)),
            pl.BlockSpec((1, 8, 1), lambda c, g: (g, 0, 0)),
            pl.BlockSpec((1, 512), lambda c, g: (0, g))]


NT = (((1,), (1,)), ((), ()))
WIDE = 8 * CHUNK
SSD_CONST_NAMES = ('e128', 'e64', 's64', 'mlo', 'mup', 'low3', 'up3', 'up3v')
SSD_CONST_SHAPES = [pltpu.VMEM((3 * CHUNK, WIDE), BF16), pltpu.VMEM((3 * CHUNK, 512), BF16), pltpu.VMEM((512, CHUNK), BF16),
                    pltpu.VMEM((CHUNK, WIDE), F32), pltpu.VMEM((CHUNK, WIDE), F32), pltpu.VMEM((CHUNK, 3 * CHUNK), BF16),
                    pltpu.VMEM((CHUNK, 3 * CHUNK), BF16), pltpu.VMEM((3 * CHUNK, CHUNK), BF16)]


def _ssd_init_consts(k):
    row, col = _iota((3 * CHUNK, WIDE), 0), _iota((3 * CHUNK, WIDE), 1)
    k['e128'][...] = ((col >> 7) == (row & 127)).astype(BF16)
    k['e64'][...] = ((_iota((3 * CHUNK, 512), 1) >> 6) == (_iota((3 * CHUNK, 512), 0) & 127)).astype(BF16)
    k['s64'][...] = ((_iota((512, CHUNK), 0) >> 6) == _iota((512, CHUNK), 1)).astype(BF16)
    row, col = _iota((CHUNK, WIDE), 0), _iota((CHUNK, WIDE), 1)
    k['mlo'][...] = (row >= (col & 127)).astype(F32)
    k['mup'][...] = (row <= (col & 127)).astype(F32)
    row, col = _iota((CHUNK, 3 * CHUNK), 0), _iota((CHUNK, 3 * CHUNK), 1) & 127
    k['low3'][...] = (row >= col).astype(BF16)
    k['up3'][...] = (row <= col).astype(BF16)
    row, col = _iota((3 * CHUNK, CHUNK), 0) & 127, _iota((3 * CHUNK, CHUNK), 1)
    k['up3v'][...] = (row <= col).astype(BF16)


def _ssd_common(x_ref, dt_ref, dtT_ref, al_ref, alT_ref, k):
    dt, a_row, acs, acs_t = _ssd_decay(dt_ref, dtT_ref, al_ref, alT_ref, k)
    ecol = _times01(acs, k['e128'][...])
    rrow = jnp.concatenate([jnp.broadcast_to(acs_t[j:j + 1, :], (CHUNK, CHUNK)) for j in range(8)], axis=1)
    a64 = _times01(acs, k['e64'][...])
    dt64 = _times01(dt, k['e64'][...])
    a_end64 = a64[CHUNK - 1:CHUNK, :]
    xs = x_ref[:, 0:512]
    return dict(dt=dt, a_row=a_row, acs=acs, seg=ecol - rrow, dt64=dt64, e_a=jnp.exp(a64), decay=jnp.exp(a_end64 - a64),
                e_end64=jnp.exp(a_end64), xs=xs, xdt=xs * dt64, bm=x_ref[:, 512:640], cm=x_ref[:, 640:768])


def _pair_blocks(v):
    lo = _iota((CHUNK, 128), 1) < 64
    out = []
    for i in range(4):
        ch = v[:, i * 128:(i + 1) * 128]
        out.append(jnp.concatenate([jnp.where(lo, ch, 0.0), jnp.where(lo, 0.0, ch)], axis=0).astype(BF16))
    return out


def _tile8(m):
    return jnp.concatenate([m] * 8, axis=1)


def _ssd_fwd(xc, dtg, dtg_t, alog, alog_t, d_exp):
    t = xc.shape[0]
    nc = t // CHUNK

    def body(x_ref, dt_ref, dtT_ref, al_ref, alT_ref, d_ref, y_ref, hs_ref, h_scr, *consts):
        c, g = pl.program_id(0), pl.program_id(1)
        k = dict(zip(SSD_CONST_NAMES, consts))

        @pl.when(jnp.logical_and(c == 0, g == 0))
        def _():
            _ssd_init_consts(k)

        @pl.when(c == 0)
        def _():
            h_scr[g] = jnp.zeros((SSD_D_STATE, 512), F32)

        v = _ssd_common(x_ref, dt_ref, dtT_ref, al_ref, alT_ref, k)
        b16, c16 = v['bm'].astype(BF16), v['cm'].astype(BF16)
        cb = lax.dot_general(c16, b16, NT, preferred_element_type=F32)
        m16 = (jnp.exp(jnp.minimum(v['seg'], 0.0)) * k['mlo'][...] * _tile8(cb)).astype(BF16)
        xbd = _pair_blocks(v['xdt'])
        y_diag = jnp.concatenate([jnp.dot(m16[:, i * 256:(i + 1) * 256], xbd[i], preferred_element_type=F32)
                                  for i in range(4)], axis=1)
        ht = h_scr[g]
        y_off = jnp.dot(c16, ht.astype(BF16), preferred_element_type=F32)
        y_ref[...] = y_diag + v['e_a'] * y_off + d_ref[...] * v['xs']
        st = jnp.dot(v['bm'].T.astype(BF16), (v['xdt'] * v['decay']).astype(BF16), preferred_element_type=F32)
        hs_ref[0, 0] = ht
        h_scr[g] = ht * v['e_end64'] + st

    return pl.pallas_call(
        body, name="ssd_fwd", grid=(nc, SSD_N_GROUPS), in_specs=_ssd_specs(nc, False),
        out_specs=[pl.BlockSpec((CHUNK, 512), lambda c, g: (c, g)),
                   pl.BlockSpec((1, 1, SSD_D_STATE, 512), lambda c, g: (c, g, 0, 0))],
        out_shape=[SDS((t, SSD_D_INNER), F32), SDS((nc, SSD_N_GROUPS, SSD_D_STATE, 512), F32)],
        scratch_shapes=[pltpu.VMEM((SSD_N_GROUPS, SSD_D_STATE, 512), F32)] + SSD_CONST_SHAPES,
        compiler_params=_cp(dimension_semantics=("arbitrary", "arbitrary")),
    )(xc, dtg, dtg_t, alog, alog_t, d_exp)


def _ssd_bwd(xc, dtg, dtg_t, alog, alog_t, d_exp, d_y, hs):
    t = xc.shape[0]
    nc = t // CHUNK

    def body(x_ref, dt_ref, dtT_ref, al_ref, alT_ref, d_ref, dy_ref, hs_ref,
             dx_ref, ddt_ref, dal_ref, dd_ref, g_scr, *consts):
        c, g = pl.program_id(0), pl.program_id(1)
        k = dict(zip(SSD_CONST_NAMES, consts))
        s64, mlo, mup = k['s64'], k['mlo'], k['mup']

        @pl.when(jnp.logical_and(c == 0, g == 0))
        def _():
            _ssd_init_consts(k)

        @pl.when(c == 0)
        def _():
            g_scr[g] = jnp.zeros((SSD_D_STATE, 512), F32)

        v = _ssd_common(x_ref, dt_ref, dtT_ref, al_ref, alT_ref, k)
        dt, a_row, xs, xdt, e_a, decay = v['dt'], v['a_row'], v['xs'], v['xdt'], v['e_a'], v['decay']
        row, col = _iota((CHUNK, CHUNK), 0), _iota((CHUNK, CHUNK), 1)
        b16, c16 = v['bm'].astype(BF16), v['cm'].astype(BF16)
        ct16 = v['cm'].T.astype(BF16)
        cb = lax.dot_general(c16, b16, NT, preferred_element_type=F32)
        cbt = lax.dot_general(b16, c16, NT, preferred_element_type=F32)
        lmat = jnp.exp(jnp.minimum(v['seg'], 0.0)) * mlo[...]
        lmat_t = jnp.exp(jnp.minimum(-v['seg'], 0.0)) * mup[...]
        mmat, mmat_t = lmat * _tile8(cb), lmat_t * _tile8(cbt)
        mt16 = mmat_t.astype(BF16)
        dy = dy_ref[...]
        dye, xdec = dy * e_a, xdt * decay
        dy16, dye16, xdec16 = dy.astype(BF16), dye.astype(BF16), xdec.astype(BF16)
        xdt16 = xdt.astype(BF16)
        ht, gt = hs_ref[0, 0], g_scr[g]
        ht16, gt16 = ht.astype(BF16), gt.astype(BF16)
        xbd, dybd = _pair_blocks(xdt), _pair_blocks(dy)
        d_m, d_mt, d_x = [], [], []
        for i in range(4):
            csl = slice(i * 128, (i + 1) * 128)
            d_m.append(lax.dot_general(dy16[:, csl], xbd[i], NT, preferred_element_type=F32))
            d_mt.append(lax.dot_general(xdt16[:, csl], dybd[i], NT, preferred_element_type=F32))
            d_x.append(jnp.dot(mt16[:, i * 256:(i + 1) * 256], dybd[i], preferred_element_type=F32))
        d_m, d_mt, d_x = jnp.concatenate(d_m, axis=1), jnp.concatenate(d_mt, axis=1), jnp.concatenate(d_x, axis=1)

        def head_sum(m):
            acc = m[:, 0:CHUNK]
            for j in range(1, 8):
                acc = acc + m[:, j * CHUNK:(j + 1) * CHUNK]
            return acc

        def seg64(p):
            return jnp.dot(p.astype(BF16), s64[...], preferred_element_type=F32)

        d_cb16 = head_sum(d_m * lmat).astype(BF16)
        d_cbt16 = head_sum(d_mt * lmat_t).astype(BF16)
        dseg = d_m * mmat - d_mt * mmat_t
        da_seg = jnp.zeros((CHUNK, CHUNK), F32)
        for j in range(8):
            da_seg = jnp.where(col == j, jnp.sum(dseg[:, j * CHUNK:(j + 1) * CHUNK], axis=1, keepdims=True), da_seg)
        ch = jnp.dot(c16, ht16, preferred_element_type=F32)
        bg = jnp.dot(b16, gt16, preferred_element_type=F32)
        d_x = d_x + decay * bg
        d_decay = seg64(xdec * bg)
        e_end = jnp.exp(v['acs'][CHUNK - 1:CHUNK, :])
        d_end = e_end * jnp.sum(seg64(gt * ht), axis=0, keepdims=True) + jnp.sum(d_decay, axis=0, keepdims=True)
        d_a = seg64(dye * ch) - d_decay + da_seg + jnp.where(row == CHUNK - 1, d_end, 0.0)
        dx_ref[:, 0:512] = d_x * v['dt64'] + d_ref[...] * dy
        dx_ref[:, 640:768] = (lax.dot_general(dye16, ht16, NT, preferred_element_type=F32)
                              + jnp.dot(d_cb16, b16, preferred_element_type=F32))
        dx_ref[:, 512:640] = (lax.dot_general(xdec16, gt16, NT, preferred_element_type=F32)
                              + jnp.dot(d_cbt16, c16, preferred_element_type=F32))
        g_scr[g] = gt * v['e_end64'] + jnp.dot(ct16, dye16, preferred_element_type=F32)
        d_adt = _01times(k['up3'][...], d_a)
        ddt_ref[0] = d_adt * a_row + seg64(d_x * xs)
        d_alog = jnp.sum(d_adt * dt, axis=0, keepdims=True) * a_row
        dd_row = jnp.sum(seg64(dy * xs), axis=0, keepdims=True)
        first = c == 0

        @pl.when(first)
        def _():
            dal_ref[g] = d_alog
            dd_ref[g] = dd_row

        @pl.when(jnp.logical_not(first))
        def _():
            dal_ref[g] += d_alog
            dd_ref[g] += dd_row

    rc = lambda c: nc - 1 - c
    whole = pl.BlockSpec((SSD_N_GROUPS, 1, 128), lambda c, g: (0, 0, 0))
    return pl.pallas_call(
        body, name="ssd_bwd", grid=(nc, SSD_N_GROUPS),
        in_specs=_ssd_specs(nc, True) + [pl.BlockSpec((CHUNK, 512), lambda c, g: (rc(c), g)),
                                        pl.BlockSpec((1, 1, SSD_D_STATE, 512), lambda c, g: (rc(c), g, 0, 0))],
        out_specs=[pl.BlockSpec((CHUNK, GROUP_W), lambda c, g: (rc(c), g)),
                   pl.BlockSpec((1, CHUNK, 128), lambda c, g: (g, rc(c), 0)), whole, whole],
        out_shape=[SDS((t, SSD_CONV_DIM), F32), SDS((SSD_N_GROUPS, t, 128), F32),
                   SDS((SSD_N_GROUPS, 1, 128), F32), SDS((SSD_N_GROUPS, 1, 128), F32)],
        scratch_shapes=[pltpu.VMEM((SSD_N_GROUPS, SSD_D_STATE, 512), F32)] + SSD_CONST_SHAPES,
        compiler_params=_cp(dimension_semantics=("arbitrary", "arbitrary")),
    )(xc, dtg, dtg_t, alog, alog_t, d_exp, d_y, hs)


def _gated_norm_fwd(y, proj, w, *, tm):
    t = y.shape[0]
    tm = min(tm, t)

    def body(y_ref, z_ref, w_ref, o_ref):
        gv = y_ref[...] * _silu_parts(z_ref[...])[0]
        r = lax.rsqrt(jnp.mean(gv * gv, axis=-1, keepdims=True) + NORM_EPS)
        o_ref[...] = (gv * r * w_ref[...]).astype(BF16)

    tile = pl.BlockSpec((tm, 512), lambda i, g: (i, g))
    return pl.pallas_call(
        body, name="gated_norm_fwd", grid=(t // tm, SSD_N_GROUPS),
        in_specs=[tile, pl.BlockSpec((tm, 512), lambda i, g: (i, OFF_Z // 512 + g)),
                  pl.BlockSpec((1, 512), lambda i, g: (0, g))], out_specs=tile,
        out_shape=SDS((t, SSD_D_INNER), BF16),
        compiler_params=_cp(dimension_semantics=("arbitrary", "arbitrary")),
    )(y, proj, w)


def _rope(ch, cos_t, sin_t):
    first = (_iota(ch.shape, 1) & 32) == 0
    partner = jnp.where(first, pltpu.roll(ch, 96, axis=1), pltpu.roll(ch, 32, axis=1))
    return ch * cos_t + partner * sin_t


def _rope_qkv(proj, cos_t, sin_t, *, tm):
    t = proj.shape[0]
    tm = min(tm, t)

    def body(q_ref, k_ref, v_ref, c_ref, s_ref, qr_ref, kp_ref, vp_ref):
        cv, sv = c_ref[...], s_ref[...]
        lo = _iota((tm, 128), 1) < 64
        for m in range(8):
            sl = slice(m * 128, (m + 1) * 128)
            qr_ref[:, sl] = (_rope(q_ref[:, sl], cv, sv) * 0.125).astype(BF16)
        for m2 in range(2):
            sl = slice(m2 * 128, (m2 + 1) * 128)
            for src, dst in ((_rope(k_ref[:, sl], cv, sv), kp_ref), (v_ref[:, sl], vp_ref)):
                sw = pltpu.roll(src, 64, axis=1)
                base = 4 * m2 * 128
                dst[:, base:base + 128] = jnp.where(lo, src, 0.0).astype(BF16)
                dst[:, base + 128:base + 256] = jnp.where(lo, 0.0, sw).astype(BF16)
                dst[:, base + 256:base + 384] = jnp.where(lo, sw, 0.0).astype(BF16)
                dst[:, base + 384:base + 512] = jnp.where(lo, 0.0, src).astype(BF16)

    return pl.pallas_call(
        body, name="rope_qkv", grid=(t // tm,),
        in_specs=[pl.BlockSpec((tm, 1024), lambda i: (i, OFF_Q // 1024)), pl.BlockSpec((tm, 256), lambda i: (i, OFF_K // 256)),
                  pl.BlockSpec((tm, 256), lambda i: (i, OFF_V // 256)), pl.BlockSpec((tm, 128), lambda i: (i, 0)),
                  pl.BlockSpec((tm, 128), lambda i: (i, 0))],
        out_specs=[pl.BlockSpec((tm, 1024), lambda i: (i, 0))] * 3,
        out_shape=[SDS((t, 1024), BF16)] * 3,
        compiler_params=_cp(dimension_semantics=("arbitrary",)),
    )(proj, proj, proj, cos_t, sin_t)


def _attn_valid(n):
    qi, kj = _iota((CHUNK, 2 * CHUNK), 0), _iota((CHUNK, 2 * CHUNK), 1)
    return (kj > qi) & (kj <= qi + CHUNK) & ((n > 0) | (kj >= CHUNK))


def _attn_fwd(qr, kp, vp, sinks):
    t = qr.shape[0]
    nb = t // CHUNK

    def body(q_ref, kc_ref, kprev_ref, vc_ref, vprev_ref, sk_ref, o_ref, lse_ref):
        n = pl.program_id(0)
        valid = _attn_valid(n)
        lane = _iota((CHUNK, 128), 1)
        lse_all = jnp.zeros((CHUNK, 128), F32)
        for m in range(8):
            g = m // 2
            qch = q_ref[:, m * 128:(m + 1) * 128]
            o_pair = jnp.zeros((CHUNK, 128), F32)
            for e in range(2):
                h = 2 * m + e
                sl = slice((2 * g + e) * 128, (2 * g + e + 1) * 128)
                kk = jnp.concatenate([kprev_ref[:, sl], kc_ref[:, sl]], axis=0)
                vv = jnp.concatenate([vprev_ref[:, sl], vc_ref[:, sl]], axis=0)
                s = lax.dot_general(qch, kk, (((1,), (1,)), ((), ())), preferred_element_type=F32)
                s = jnp.where(valid, s, NEG)
                sink = sk_ref[0:1, h:h + 1]
                mx = jnp.maximum(jnp.max(s, axis=1, keepdims=True), sink)
                p = jnp.exp(s - mx)
                den = jnp.sum(p, axis=1, keepdims=True) + jnp.exp(sink - mx)
                o_pair = o_pair + jnp.dot((p / den).astype(BF16), vv, preferred_element_type=F32)
                lse_all = jnp.where(lane == h, mx + jnp.log(den), lse_all)
            o_ref[:, m * 128:(m + 1) * 128] = o_pair.astype(BF16)
        lse_ref[...] = lse_all

    cur = pl.BlockSpec((CHUNK, 1024), lambda n: (n, 0))
    prev = pl.BlockSpec((CHUNK, 1024), lambda n: (jnp.maximum(n - 1, 0), 0))
    return pl.pallas_call(
        body, name="attn_fwd", grid=(nb,),
        in_specs=[cur, cur, prev, cur, prev, pl.BlockSpec((1, 128), lambda n: (0, 0))],
        out_specs=[cur, pl.BlockSpec((CHUNK, 128), lambda n: (n, 0))],
        out_shape=[SDS((t, 1024), BF16), SDS((t, 128), F32)],
        compiler_params=_cp(dimension_semantics=("arbitrary",)),
    )(qr, kp, kp, vp, vp, sinks)


def _attn_bwd(qr, kp, vp, d_o, o, lse, sinks, cos_t, sin_t, d_proj):
    t = qr.shape[0]
    nb = t // CHUNK

    def body(q_ref, kc_ref, kprev_ref, vc_ref, vprev_ref, do_ref, o_ref, lse_ref, sk_ref, c_ref, s_ref, cp_ref, sp_ref,
             _, dqkv_ref, dsk_ref, acc_k, acc_v, dq_scr):
        n = pl.program_id(0)
        lane = _iota((CHUNK, 128), 1)
        lo = lane < 64
        lane1 = _iota((1, 128), 1)

        @pl.when(n == 0)
        def _():
            acc_k[...] = jnp.zeros_like(acc_k)
            acc_v[...] = jnp.zeros_like(acc_v)
            dsk_ref[...] = jnp.zeros((1, 128), F32)

        @pl.when(n > 0)
        def _():
            dqkv_ref[:, 0:1024] = dq_scr[...]
            for r in range(8):
                acc_k[r, 0:CHUNK] = acc_k[r, CHUNK:2 * CHUNK]
                acc_v[r, 0:CHUNK] = acc_v[r, CHUNK:2 * CHUNK]
                acc_k[r, CHUNK:2 * CHUNK] = jnp.zeros((CHUNK, 128), F32)
                acc_v[r, CHUNK:2 * CHUNK] = jnp.zeros((CHUNK, 128), F32)

        @pl.when(n < nb)
        def _():
            valid = _attn_valid(n)
            lse_all = lse_ref[...]
            dsk = jnp.zeros((1, 128), F32)
            for m in range(8):
                g = m // 2
                csl = slice(m * 128, (m + 1) * 128)
                qch = q_ref[:, csl]
                doch = do_ref[:, csl]
                prod = doch.astype(F32) * o_ref[:, csl].astype(F32)
                dq_pair = jnp.zeros((CHUNK, 128), F32)
                for e in range(2):
                    h = 2 * m + e
                    sl = slice((2 * g + e) * 128, (2 * g + e + 1) * 128)
                    kk = jnp.concatenate([kprev_ref[:, sl], kc_ref[:, sl]], axis=0)
                    vv = jnp.concatenate([vprev_ref[:, sl], vc_ref[:, sl]], axis=0)
                    lse_h = lse_all[:, h:h + 1]
                    s = lax.dot_general(qch, kk, (((1,), (1,)), ((), ())), preferred_element_type=F32)
                    p = jnp.exp(jnp.where(valid, s, NEG) - lse_h)
                    delta = jnp.sum(jnp.where(lo if e == 0 else jnp.logical_not(lo), prod, 0.0), axis=1, keepdims=True)
                    d_p = lax.dot_general(doch, vv, (((1,), (1,)), ((), ())), preferred_element_type=F32)
                    d_s16 = (p * (d_p - delta)).astype(BF16)
                    dq_pair = dq_pair + jnp.dot(d_s16, kk, preferred_element_type=F32)
                    acc_k[2 * g + e] += lax.dot_general(d_s16, qch, (((0,), (0,)), ((), ())), preferred_element_type=F32)
                    acc_v[2 * g + e] += lax.dot_general(p.astype(BF16), doch, (((0,), (0,)), ((), ())), preferred_element_type=F32)
                    p_sink = jnp.exp(sk_ref[0:1, h:h + 1] - lse_h)
                    dsk = jnp.where(lane1 == h, -jnp.sum(p_sink * delta), dsk)
                dq_scr[:, csl] = (_rope(dq_pair, c_ref[...], -s_ref[...]) * 0.125).astype(BF16)
            dsk_ref[...] += dsk

        @pl.when(n > 0)
        def _():
            for m2 in range(2):
                halves = []
                for g in (2 * m2, 2 * m2 + 1):
                    for acc in (acc_k, acc_v):
                        comb = jnp.where(lo, acc[2 * g, 0:CHUNK], acc[2 * g + 1, 0:CHUNK])
                        halves.append(comb + pltpu.roll(comb, 64, axis=1))
                d_kr = jnp.where(lo, halves[0], halves[2])
                d_v = jnp.where(lo, halves[1], halves[3])
                dqkv_ref[:, OFF_K + m2 * 128:OFF_K + (m2 + 1) * 128] = _rope(d_kr, cp_ref[...], -sp_ref[...]).astype(BF16)
                dqkv_ref[:, OFF_V + m2 * 128:OFF_V + (m2 + 1) * 128] = d_v.astype(BF16)

    qn = lambda n: jnp.minimum(n, nb - 1)
    pn = lambda n: jnp.maximum(jnp.minimum(n, nb) - 1, 0)
    cur = pl.BlockSpec((CHUNK, 1024), lambda n: (qn(n), 0))
    prev = pl.BlockSpec((CHUNK, 1024), lambda n: (pn(n), 0))
    cur128 = pl.BlockSpec((CHUNK, 128), lambda n: (qn(n), 0))
    prev128 = pl.BlockSpec((CHUNK, 128), lambda n: (pn(n), 0))
    one = pl.BlockSpec((1, 128), lambda n: (0, 0))
    return pl.pallas_call(
        body, name="attn_bwd", grid=(nb + 1,),
        in_specs=[cur, cur, prev, cur, prev, cur, cur, cur128, one, cur128, cur128, prev128, prev128,
                  pl.BlockSpec(memory_space=pl.ANY)],
        out_specs=[pl.BlockSpec((CHUNK, 1536), lambda n: (pn(n), 0)), one],
        out_shape=[SDS((t, PROJ_W), BF16), SDS((1, 128), F32)],
        scratch_shapes=[pltpu.VMEM((8, 2 * CHUNK, 128), F32), pltpu.VMEM((8, 2 * CHUNK, 128), F32),
                        pltpu.VMEM((CHUNK, 1024), BF16)],
        input_output_aliases={13: 0},
        compiler_params=_cp(dimension_semantics=("arbitrary",)),
    )(qr, kp, kp, vp, vp, d_o, o, lse, sinks, cos_t, sin_t, cos_t, sin_t, d_proj)


def _adamw(name, w, g, m, v, *, tr):
    rows, cols = w.shape
    tr = min(tr, rows)
    assert rows % tr == 0

    def body(w_ref, g_ref, m_ref, v_ref, d_ref, nm_ref, nv_ref):
        gv = g_ref[...]
        nm = ADAM_B1 * m_ref[...] + (1.0 - ADAM_B1) * gv
        nv = ADAM_B2 * v_ref[...] + (1.0 - ADAM_B2) * (gv * gv)
        m_hat = nm / (1.0 - ADAM_B1 ** ADAM_STEP)
        v_hat = nv / (1.0 - ADAM_B2 ** ADAM_STEP)
        d_ref[...] = -ADAM_LR * (m_hat / (jnp.sqrt(v_hat) + ADAM_EPS) + ADAM_WD * w_ref[...])
        nm_ref[...] = nm
        nv_ref[...] = nv

    tile = pl.BlockSpec((tr, cols), lambda i: (i, 0))
    return pl.pallas_call(
        body, name=name, grid=(rows // tr,), in_specs=[tile] * 4, out_specs=[tile] * 3,
        out_shape=[SDS((rows, cols), F32)] * 3, compiler_params=_cp(dimension_semantics=("arbitrary",)),
    )(w, g, m, v)


def _local_step(x, cos_t, sin_t, tgt, wb, ps):
    t = x.shape[0]
    tm = min(512, t)
    tmw = min(1024, t)
    ij = lambda i, j, k: (i, j)
    i0 = lambda i, j, k: (i, 0)
    c0 = lambda i, j, k: (0, 0)
    cj = lambda i, j, k: (0, j)

    tkt = min(2048, t)
    proj, u = _norm_mm("in_proj", x, ps['norm_mix_pre_w'], wb['cat'], tm=tmw, tn=1024)
    xc = _conv_silu_fwd(proj, ps['ssd_conv_w'], ps['ssd_conv_b'], tm=tm)
    bias_pad = jnp.pad(ps['ssd_dt_bias'], ((0, 0), (0, 96)))
    dt = _dt_fwd(proj, bias_pad, tm=tmw)
    dt32 = dt[:, :SSD_N_HEADS].reshape(t, SSD_N_GROUPS, 8)
    dtg = jnp.pad(dt32.transpose(1, 0, 2), ((0, 0), (0, 0), (0, 120)))
    dtg_t = dt32.transpose(1, 2, 0)
    alog = jnp.pad(ps['ssd_a_log'].reshape(SSD_N_GROUPS, 1, 8), ((0, 0), (0, 0), (0, 120)))
    alog_t = ps['ssd_a_log'].reshape(SSD_N_GROUPS, 8, 1)
    d_exp = jnp.repeat(ps['ssd_d'], SSD_HEAD_DIM, axis=1)
    y, hs = _ssd_fwd(xc, dtg, dtg_t, alog, alog_t, d_exp)
    gn = _gated_norm_fwd(y, proj, ps['ssd_norm_w'], tm=tm)
    qr, kp, vp = _rope_qkv(proj, cos_t, sin_t, tm=tm)
    sinks = jnp.pad(ps['attn_sinks'], ((0, 0), (0, 112)))
    ao, lse = _attn_fwd(qr, kp, vp, sinks)
    y_attn = _mm_plain("attn_out", ao, wb['ao'], tm=tmw, tn=512, tk=1024)

    def merge_ep(acc, i, j, ins, outs):
        gs, ga, ya = ins
        outs[0][...] = (_sigmoid(gs[...]) * acc + _sigmoid(ga[...]) * ya[...]).astype(BF16)
        outs[1][...] = acc

    merged, y_ssd = _mm_call(
        "ssd_out_merge", gn, wb['so'], tm=tmw, tn=512, tk=2048, epilogue=merge_ep,
        extra_in=[(proj, (tmw, 512), lambda i, j, k: (i, OFF_GS // 512 + j)),
                  (proj, (tmw, 512), lambda i, j, k: (i, OFF_GA // 512 + j)), (y_attn, (tmw, 512), ij)],
        outs=[((t, D_MODEL), BF16, (tmw, 512), ij), ((t, D_MODEL), F32, (tmw, 512), ij)])

    def mix_ep(acc, i, j, ins, outs):
        xv, wn = ins
        r = lax.rsqrt(jnp.mean(acc * acc, axis=-1, keepdims=True) + NORM_EPS)
        outs[0][...] = xv[...] + acc * r * wn[...]
        outs[1][...] = acc

    x1, mmix = _mm_call(
        "mix_out", merged, wb['mix'], tm=tm, tn=D_MODEL, tk=1024, epilogue=mix_ep,
        extra_in=[(x, (tm, D_MODEL), i0), (ps['norm_mix_post_w'], (1, D_MODEL), c0)],
        outs=[((t, D_MODEL), F32, (tm, D_MODEL), i0), ((t, D_MODEL), F32, (tm, D_MODEL), i0)])

    up_raw, h = _norm_mm("ffn_up", x1, ps['norm_ffn_pre_w'], wb['up'], tm=tmw, tn=1408)
    act = _ffn_act_fwd(up_raw, ps['ffn_conv_w'], ps['ffn_conv_b'], tm=min(256, t))

    def loss_ep(acc, i, j, ins, outs):
        x1v, tg, wn = ins
        d_ff_ref, dout_ref, loss_ref, dw_ref = outs
        wv = wn[...]
        r = lax.rsqrt(jnp.mean(acc * acc, axis=-1, keepdims=True) + NORM_EPS)
        err = x1v[...] + acc * r * wv - tg[...]
        dout = err * (1.0 / D_MODEL)
        dout_ref[...] = dout
        d_ff, dw = _rms_bwd(acc, wv, dout)
        d_ff_ref[...] = d_ff.astype(BF16)
        _accumulate(dw_ref, i == 0, dw)
        _accumulate(loss_ref, i == 0, jnp.sum(err * err, keepdims=True) * (0.5 / D_MODEL))

    d_ff, dout, loss, g_norm_ffn_post = _mm_call(
        "ffn_down_loss", act, wb['dn'], tm=tm, tn=D_MODEL, tk=FFN_D_FF, epilogue=loss_ep,
        extra_in=[(x1, (tm, D_MODEL), i0), (tgt, (tm, D_MODEL), i0), (ps['norm_ffn_post_w'], (1, D_MODEL), c0)],
        outs=[((t, D_MODEL), BF16, (tm, D_MODEL), i0), ((t, D_MODEL), F32, (tm, D_MODEL), i0),
              ((1, 1), F32, (1, 1), c0), ((1, D_MODEL), F32, (1, D_MODEL), c0)])

    d_act = _mm_plain("d_act", d_ff, wb['dn_t'], tm=tmw, tn=1408, tk=1024)
    g_w_down = _mm_plain("g_w_down", act, d_ff, tm=1408, tn=1024, tk=tkt, trans_a=True)
    d_gate, d_val, db_g, db_v = _ffn_act_bwd(up_raw, d_act, ps['ffn_conv_w'], ps['ffn_conv_b'], tm=min(256, t))
    d_up_raw, gcw_g = _conv_bwd2("ffn_conv_bwd2_gate", d_gate, up_raw, 0, ps['ffn_conv_w'][:, :FFN_D_FF], tm=min(256, t),
                                 tc=1408, out_cols=2 * FFN_D_FF, out_col0=0)
    d_up_raw, gcw_v = _conv_bwd2("ffn_conv_bwd2_val", d_val, up_raw, FFN_D_FF, ps['ffn_conv_w'][:, FFN_D_FF:], tm=min(256, t),
                                 tc=1408, out_cols=2 * FFN_D_FF, out_col0=FFN_D_FF, fill=d_up_raw)
    g_ffn_conv_w = jnp.concatenate([gcw_g, gcw_v], axis=1)

    def dx1_ep(acc, i, j, ins, outs):
        x1v, wpre, dout_v, mmv, wpost = ins
        d_x1_ref, d_mm_ref, dwpre_ref, dwpost_ref = outs
        d_n, dw_pre = _rms_bwd(x1v[...], wpre[...], acc)
        d_x1 = dout_v[...] + d_n
        d_x1_ref[...] = d_x1
        d_mm, dw_post = _rms_bwd(mmv[...], wpost[...], d_x1)
        d_mm_ref[...] = d_mm.astype(BF16)
        _accumulate(dwpre_ref, i == 0, dw_pre)
        _accumulate(dwpost_ref, i == 0, dw_post)

    d_x1, d_mm, g_norm_ffn_pre, g_norm_mix_post = _mm_call(
        "d_h", d_up_raw, wb['up_t'], tm=tm, tn=D_MODEL, tk=FFN_D_FF, epilogue=dx1_ep,
        extra_in=[(x1, (tm, D_MODEL), i0), (ps['norm_ffn_pre_w'], (1, D_MODEL), c0), (dout, (tm, D_MODEL), i0),
                  (mmix, (tm, D_MODEL), i0), (ps['norm_mix_post_w'], (1, D_MODEL), c0)],
        outs=[((t, D_MODEL), F32, (tm, D_MODEL), i0), ((t, D_MODEL), BF16, (tm, D_MODEL), i0),
              ((1, D_MODEL), F32, (1, D_MODEL), c0), ((1, D_MODEL), F32, (1, D_MODEL), c0)])
    g_w_up_t = _mm_plain("g_w_up", d_up_raw, h, tm=1408, tn=1024, tk=tkt, trans_a=True)

    def dmerge_ep(acc, i, j, ins, outs):
        gs, ga, ys, ya = ins
        sg_s, sg_a = _sigmoid(gs[...]), _sigmoid(ga[...])
        outs[0][...] = (acc * sg_s).astype(BF16)
        outs[1][...] = (acc * sg_a).astype(BF16)
        outs[2][:, 0:D_MODEL] = (acc * ys[...] * sg_s * (1.0 - sg_s)).astype(BF16)
        outs[2][:, D_MODEL:2 * D_MODEL] = (acc * ya[...] * sg_a * (1.0 - sg_a)).astype(BF16)

    d_yssd, d_yattn, d_proj = _mm_call(
        "d_merged", d_mm, wb['mix_t'], tm=tm, tn=D_MODEL, tk=1024, epilogue=dmerge_ep,
        extra_in=[(proj, (tm, D_MODEL), lambda i, j, k: (i, OFF_GS // D_MODEL)),
                  (proj, (tm, D_MODEL), lambda i, j, k: (i, OFF_GA // D_MODEL)), (y_ssd, (tm, D_MODEL), i0), (y_attn, (tm, D_MODEL), i0)],
        outs=[((t, D_MODEL), BF16, (tm, D_MODEL), i0), ((t, D_MODEL), BF16, (tm, D_MODEL), i0),
              ((t, PROJ_W), BF16, (tm, 2 * D_MODEL), lambda i, j, k: (i, OFF_GS // (2 * D_MODEL)))])
    g_w_mix = _mm_plain("g_w_mix", merged, d_mm, tm=1024, tn=1024, tk=tkt, trans_a=True)

    def dgn_ep(acc, i, j, ins, outs):
        yv, zv, wn = ins
        d_y_ref, d_z_ref, dw_ref = outs
        zz = zv[...]
        sz = _sigmoid(zz)
        silu = zz * sz
        gv = yv[...] * silu
        r = lax.rsqrt(jnp.mean(gv * gv, axis=-1, keepdims=True) + NORM_EPS)
        gh = gv * r
        dgh = acc * wn[...]
        dg = r * (dgh - gh * jnp.mean(dgh * gh, axis=-1, keepdims=True))
        d_y_ref[...] = dg * silu
        d_z_ref[...] = (dg * yv[...] * (sz * (1.0 + zz * (1.0 - sz)))).astype(BF16)
        dw = jnp.sum(acc * gh, axis=0, keepdims=True)

        @pl.when(i == 0)
        def _():
            dw_ref[j] = dw

        @pl.when(i > 0)
        def _():
            dw_ref[j] += dw

    d_y, d_proj, g_ssd_norm = _mm_call(
        "d_gn", d_yssd, wb['so_t'], tm=tm, tn=512, tk=1024, epilogue=dgn_ep, fill=(d_proj, 1),
        extra_in=[(y, (tm, 512), ij), (proj, (tm, 512), lambda i, j, k: (i, OFF_Z // 512 + j)), (ps['ssd_norm_w'], (1, 512), cj)],
        outs=[((t, SSD_D_INNER), F32, (tm, 512), ij), ((t, PROJ_W), BF16, (tm, 512), lambda i, j, k: (i, OFF_Z // 512 + j)),
              ((SSD_N_GROUPS, 1, 512), F32, (SSD_N_GROUPS, 1, 512), lambda i, j, k: (0, 0, 0))])
    g_ssd_norm = g_ssd_norm.reshape(1, SSD_D_INNER)
    g_w_so = _mm_plain("g_w_so", gn, d_yssd, tm=1024, tn=1024, tk=tkt, trans_a=True)
    d_xc, d_dtg, d_alog, d_dd = _ssd_bwd(xc, dtg, dtg_t, alog, alog_t, d_exp, d_y, hs)
    d_pre, g_ssd_conv_b = _conv_silu_bwd1(d_xc, proj, ps['ssd_conv_w'], ps['ssd_conv_b'], tm=tm)
    d_proj, g_ssd_conv_w = _conv_bwd2("ssd_conv_bwd2", d_pre, proj, OFF_XBC, ps['ssd_conv_w'], tm=tm, tc=512,
                                      out_cols=PROJ_W, out_col0=OFF_XBC, fill=d_proj)
    d_dt = jnp.pad(d_dtg[:, :, :8].transpose(1, 0, 2).reshape(t, SSD_N_HEADS), ((0, 0), (0, 96)))
    d_proj, g_dt_bias = _dt_bwd(d_dt, proj, bias_pad, d_proj, tm=tmw)

    d_ao = _mm_plain("d_ao", d_yattn, wb['ao_t'], tm=tmw, tn=512, tk=1024, out_dtype=BF16)
    g_w_ao = _mm_plain("g_w_ao", ao, d_yattn, tm=1024, tn=1024, tk=tkt, trans_a=True)
    d_proj, g_sinks = _attn_bwd(qr, kp, vp, d_ao, ao, lse, sinks, cos_t, sin_t, d_proj)

    def dx_ep(acc, i, j, ins, outs):
        xv, wn, dx1v = ins
        d_n, dw = _rms_bwd(xv[...], wn[...], acc)
        outs[0][...] = dx1v[...] + d_n
        _accumulate(outs[1], i == 0, dw)

    grad_x, g_norm_mix_pre = _mm_call(
        "d_u", d_proj, wb['cat_t'], tm=tm, tn=D_MODEL, tk=2304, epilogue=dx_ep,
        extra_in=[(x, (tm, D_MODEL), i0), (ps['norm_mix_pre_w'], (1, D_MODEL), c0), (d_x1, (tm, D_MODEL), i0)],
        outs=[((t, D_MODEL), F32, (tm, D_MODEL), i0), ((1, D_MODEL), F32, (1, D_MODEL), c0)])
    g_cat_t = _mm_plain("g_w_in", d_proj, u, tm=1024, tn=1024, tk=tkt, trans_a=True)

    grads = {
        'norm_mix_pre_w': g_norm_mix_pre, 'w_in': g_cat_t, 'ssd_conv_w': g_ssd_conv_w, 'ssd_conv_b': g_ssd_conv_b,
        'ssd_dt_bias': g_dt_bias[:, :SSD_N_HEADS], 'ssd_a_log': d_alog[:, 0, :8].reshape(1, SSD_N_HEADS),
        'ssd_d': d_dd[:, 0, :8].reshape(1, SSD_N_HEADS), 'ssd_norm_w': g_ssd_norm, 'ssd_w_out': g_w_so,
        'attn_sinks': g_sinks[:, :ATTN_N_HEADS], 'attn_w_out': g_w_ao, 'w_mix_out': g_w_mix,
        'norm_mix_post_w': g_norm_mix_post, 'norm_ffn_pre_w': g_norm_ffn_pre, 'ffn_w_up': g_w_up_t,
        'ffn_conv_w': g_ffn_conv_w, 'ffn_conv_b': jnp.concatenate([db_g, db_v], axis=1), 'ffn_w_down': g_w_down,
        'norm_ffn_post_w': g_norm_ffn_post,
    }
    return loss, grad_x, grads


def _group_channels(a):
    parts = []
    for g in range(SSD_N_GROUPS):
        parts += [a[..., 512 * g:512 * (g + 1)], a[..., 2048 + 128 * g:2048 + 128 * (g + 1)],
                  a[..., 2560 + 128 * g:2560 + 128 * (g + 1)]]
    return jnp.concatenate(parts, axis=-1)


def _ungroup_channels(a):
    xs = [a[..., GROUP_W * g:GROUP_W * g + 512] for g in range(SSD_N_GROUPS)]
    bs = [a[..., GROUP_W * g + 512:GROUP_W * g + 640] for g in range(SSD_N_GROUPS)]
    cs = [a[..., GROUP_W * g + 640:GROUP_W * (g + 1)] for g in range(SSD_N_GROUPS)]
    return jnp.concatenate(xs + bs + cs, axis=-1)


def _proj_rows(a_t, lo, hi):
    out = []
    for start, length, dst in sorted(PROJ_SEGS):
        s, e = max(lo, start), min(hi, start + length)
        if s < e:
            out.append(a_t[dst + s - start:dst + e - start])
    return out


def _to_proj_layout(w_in_t):
    pieces, pos = [], 0
    for start, length, dst in sorted(PROJ_SEGS, key=lambda s: s[2]):
        if dst > pos:
            pieces.append(jnp.zeros((dst - pos, w_in_t.shape[1]), w_in_t.dtype))
        pieces.append(w_in_t[start:start + length])
        pos = dst + length
    if pos < PROJ_W:
        pieces.append(jnp.zeros((PROJ_W - pos, w_in_t.shape[1]), w_in_t.dtype))
    return jnp.concatenate(pieces, axis=0)


def _rope_tables(positions):
    half = 32
    inv_freq = ROPE_THETA ** (-jnp.arange(half, dtype=F32) * 2.0 / 64)
    ang = positions.astype(F32)[:, None] * inv_freq
    cos, sin = jnp.cos(ang), jnp.sin(ang)
    return jnp.concatenate([cos, cos, cos, cos], axis=1), jnp.concatenate([-sin, sin, -sin, sin], axis=1)


def _matmul_weights(w_in_t, so, ao, mix, up_t, dn):
    cat_t = _to_proj_layout(w_in_t)
    return {'cat': cat_t.T, 'cat_t': cat_t, 'so': so, 'so_t': so.T, 'ao': ao, 'ao_t': ao.T, 'mix': mix, 'mix_t': mix.T,
            'up': up_t.T, 'up_t': up_t, 'dn': dn, 'dn_t': dn.T}


ANY = pl.BlockSpec(memory_space=pl.ANY)
MESH = pl.DeviceIdType.MESH
ROW_ALIGN = 256


def _mesh_pos():
    return lax.axis_index("x"), lax.axis_index("y"), lax.axis_index("c")


def _other_chips(x, y):
    return [(1 - x, y), (x, 1 - y), (1 - x, 1 - y)]


def _remote(src, dst, send_sems, recv_sems, k, to):
    return pltpu.make_async_remote_copy(src_ref=src, dst_ref=dst, send_sem=send_sems.at[k], recv_sem=recv_sems.at[k],
                                        device_id=to, device_id_type=MESH)


def _half(c, rh):
    return pl.ds(pl.multiple_of(c * rh, 128), rh)


def _all_gather(shard):
    r = shard.shape[0]
    rh = r // 2

    def body(w_ref, out_ref, send_sems, recv_sems):
        x, y, c = _mesh_pos()
        p = 2 * x + y
        sib = (x, y, 1 - c)
        mine, other = _half(c, rh), _half(1 - c, rh)
        chips = _other_chips(x, y)
        first = [_remote(w_ref, out_ref.at[p], send_sems, recv_sems, 6, sib)]
        first += [_remote(w_ref.at[mine], out_ref.at[p, mine], send_sems, recv_sems, j, (cx, cy, c))
                  for j, (cx, cy) in enumerate(chips)]
        for cp in first:
            cp.start()
        passed = []
        for j, (cx, cy) in enumerate(chips):
            slab = out_ref.at[2 * cx + cy, mine]
            _remote(slab, slab, send_sems, recv_sems, j, sib).wait_recv()
            fwd = _remote(slab, slab, send_sems, recv_sems, 3 + j, sib)
            fwd.start()
            passed.append(fwd)
        for j, (cx, cy) in enumerate(chips):
            slab = out_ref.at[2 * cx + cy, other]
            _remote(slab, slab, send_sems, recv_sems, 3 + j, sib).wait_recv()
        _remote(w_ref, out_ref.at[p], send_sems, recv_sems, 6, sib).wait_recv()
        for cp in first + passed:
            cp.wait_send()

    return pl.pallas_call(
        body, name="weights_all_gather", in_specs=[ANY], out_specs=ANY,
        out_shape=SDS((N_CHIPS, r, COMM_LANES), shard.dtype),
        scratch_shapes=[pltpu.SemaphoreType.DMA((7,)), pltpu.SemaphoreType.DMA((7,))],
    )(shard)


def _pair_swap(g):
    rh = g.shape[1] // 2

    def body(g_ref, out_ref, send_sems, recv_sems):
        x, y, c = _mesh_pos()
        cp = _remote(g_ref.at[:, _half(1 - c, rh)], out_ref, send_sems, recv_sems, 0, (x, y, 1 - c))
        cp.start()
        cp.wait()

    return pl.pallas_call(
        body, name="grad_pair_swap", in_specs=[ANY], out_specs=ANY,
        out_shape=SDS((N_CHIPS, rh, COMM_LANES), g.dtype),
        scratch_shapes=[pltpu.SemaphoreType.DMA((1,)), pltpu.SemaphoreType.DMA((1,))],
    )(g)


def _chip_exchange(part):
    rh = part.shape[1]

    def body(p_ref, out_ref, send_sems, recv_sems):
        x, y, c = _mesh_pos()
        cps = [_remote(p_ref.at[2 * cx + cy], out_ref.at[j], send_sems, recv_sems, j, (cx, cy, c))
               for j, (cx, cy) in enumerate(_other_chips(x, y))]
        for cp in cps:
            cp.start()
        for cp in cps:
            cp.wait()

    return pl.pallas_call(
        body, name="grad_chip_exchange", in_specs=[ANY], out_specs=ANY,
        out_shape=SDS((3, rh, COMM_LANES), part.dtype),
        scratch_shapes=[pltpu.SemaphoreType.DMA((3,)), pltpu.SemaphoreType.DMA((3,))],
    )(part)


def _pair_gather(buf):
    def body(_, out_ref, send_sems, recv_sems):
        x, y, c = _mesh_pos()
        cp = _remote(out_ref.at[c], out_ref.at[c], send_sems, recv_sems, 0, (x, y, 1 - c))
        cp.start()
        _remote(out_ref.at[1 - c], out_ref.at[1 - c], send_sems, recv_sems, 0, (x, y, 1 - c)).wait_recv()
        cp.wait_send()

    return pl.pallas_call(
        body, name="grad_pair_gather", in_specs=[ANY], out_specs=ANY, out_shape=SDS(buf.shape, buf.dtype),
        scratch_shapes=[pltpu.SemaphoreType.DMA((1,)), pltpu.SemaphoreType.DMA((1,))],
        input_output_aliases={0: 0},
    )(buf)


def _pair_sum(g, got, c_idx, *, tr=384):
    rh = got.shape[1]
    nb = rh // tr

    def body(c_ref, a_ref, b_ref, o_ref):
        o_ref[...] = (a_ref[...] + b_ref[...]).astype(BF16)

    return pl.pallas_call(
        body, name="grad_pair_sum",
        grid_spec=pltpu.PrefetchScalarGridSpec(
            num_scalar_prefetch=1, grid=(N_CHIPS, nb),
            in_specs=[pl.BlockSpec((1, tr, COMM_LANES), lambda s, i, c_ref: (s, c_ref[0] * nb + i, 0)),
                      pl.BlockSpec((1, tr, COMM_LANES), lambda s, i, c_ref: (s, i, 0))],
            out_specs=pl.BlockSpec((1, tr, COMM_LANES), lambda s, i, c_ref: (s, i, 0))),
        out_shape=SDS(got.shape, BF16), compiler_params=_cp(dimension_semantics=("arbitrary", "arbitrary")),
    )(c_idx, g, got)


def _chip_sum(part, got, pc_idx, *, tr=384):
    rh = part.shape[1]

    def body(p_ref, own_ref, r0_ref, r1_ref, r2_ref, o_ref):
        p = p_ref[0]
        own, r0, r1, r2 = (r[0].astype(F32) for r in (own_ref, r0_ref, r1_ref, r2_ref))

        def term(q):
            code = p ^ q
            return jnp.where(code == 0, own, jnp.where(code == 2, r0, jnp.where(code == 1, r1, r2)))

        o_ref[0] = ((term(0) + term(1)) + term(2)) + term(3)

    slab = lambda j: pl.BlockSpec((1, tr, COMM_LANES), lambda i, p_ref: (j, i, 0))
    return pl.pallas_call(
        body, name="grad_chip_sum",
        grid_spec=pltpu.PrefetchScalarGridSpec(
            num_scalar_prefetch=1, grid=(rh // tr,),
            in_specs=[pl.BlockSpec((1, tr, COMM_LANES), lambda i, p_ref: (p_ref[0], i, 0)), slab(0), slab(1), slab(2)],
            out_specs=pl.BlockSpec((1, tr, COMM_LANES), lambda i, p_ref: (p_ref[1], i, 0))),
        out_shape=SDS((2, rh, COMM_LANES), F32), compiler_params=_cp(dimension_semantics=("arbitrary",)),
    )(pc_idx, part, got, got, got)


def _pack_rows(big, small):
    flat = jnp.concatenate([p.reshape(-1) for p in small])
    k = -(-flat.shape[0] // COMM_LANES)
    tail = jnp.pad(flat, (0, k * COMM_LANES - flat.shape[0])).reshape(k, COMM_LANES)
    rows = sum(p.shape[0] for p in big) + k
    pad = -rows % ROW_ALIGN
    return jnp.concatenate(list(big) + [tail, jnp.zeros((pad, COMM_LANES), tail.dtype)], axis=0)


def _take(flat, off, shape):
    n = 1
    for d in shape:
        n *= d
    return flat[off:off + n].reshape(shape), off + n


BIG_ROWS = (('w_in', 2184), ('ssd_w_out', 512), ('attn_w_out', 256), ('w_mix_out', 256), ('ffn_w_up', 1408), ('ffn_w_down', 704))
TRANSPOSED = ('w_in', 'ffn_w_up')


def kernel(x, positions, norm_mix_pre_w, w_in, ssd_conv_w, ssd_conv_b, ssd_dt_bias, ssd_a_log, ssd_d, ssd_norm_w, ssd_w_out, attn_sinks, attn_w_out, w_mix_out, norm_mix_post_w, norm_ffn_pre_w, ffn_w_up, ffn_conv_w, ffn_conv_b, ffn_w_down, norm_ffn_post_w, loss_target, m_norm_mix_pre_w, m_w_in, m_ssd_conv_w, m_ssd_conv_b, m_ssd_dt_bias, m_ssd_a_log, m_ssd_d, m_ssd_norm_w, m_ssd_w_out, m_attn_sinks, m_attn_w_out, m_w_mix_out, m_norm_mix_post_w, m_norm_ffn_pre_w, m_ffn_w_up, m_ffn_conv_w, m_ffn_conv_b, m_ffn_w_down, m_norm_ffn_post_w, v_norm_mix_pre_w, v_w_in, v_ssd_conv_w, v_ssd_conv_b, v_ssd_dt_bias, v_ssd_a_log, v_ssd_d, v_ssd_norm_w, v_ssd_w_out, v_attn_sinks, v_attn_w_out, v_w_mix_out, v_norm_mix_post_w, v_norm_ffn_pre_w, v_ffn_w_up, v_ffn_conv_w, v_ffn_conv_b, v_ffn_w_down, v_norm_ffn_post_w):
    given = dict(locals())
    w = {n: given[n][0] for n in WEIGHTS}
    w = {n: (a if a.ndim == 2 else a[None]) for n, a in w.items()}
    mom_m = {n: given['m_' + n].reshape(w[n].shape) for n in WEIGHTS}
    mom_v = {n: given['v_' + n].reshape(w[n].shape) for n in WEIGHTS}
    cx, cy, cc = _mesh_pos()
    c_idx = cc.astype(jnp.int32).reshape(1)
    pc_idx = jnp.stack([2 * cx + cy, cc]).astype(jnp.int32)

    big = [(w[n].T if n in TRANSPOSED else w[n]).astype(BF16) for n, _ in BIG_ROWS]
    taps = [lax.bitcast_convert_type(w[n], BF16) for n in ('ssd_conv_w', 'ffn_conv_w')]
    gathered = _all_gather(_pack_rows(big, taps))
    rows = {n: [] for n, _ in BIG_ROWS}
    conv = {'ssd_conv_w': [], 'ffn_conv_w': []}
    for s in range(N_CHIPS):
        r0 = 0
        for n, nr in BIG_ROWS:
            rows[n].append(gathered[s, r0:r0 + nr])
            r0 += nr
        flat, off = gathered[s, r0:r0 + 16].reshape(-1), 0
        for n in conv:
            a, off = _take(flat, off, w[n].shape + (2,))
            conv[n].append(lax.bitcast_convert_type(a, F32))
    wfull = {n: jnp.concatenate(rows[n], axis=0) for n in rows}
    wb = _matmul_weights(wfull['w_in'], wfull['ssd_w_out'], wfull['attn_w_out'], wfull['w_mix_out'], wfull['ffn_w_up'],
                         wfull['ffn_w_down'])

    ps = {n: w[n] for n in REPLICATED}
    ps['ssd_conv_w'] = _group_channels(jnp.concatenate(conv['ssd_conv_w'], axis=1))
    ps['ssd_conv_b'] = _group_channels(w['ssd_conv_b'])
    ps['ffn_conv_w'] = jnp.concatenate(conv['ffn_conv_w'], axis=1)
    cos_t, sin_t = _rope_tables(positions[0])
    loss, grad_x, grads = _local_step(x[0], cos_t, sin_t, loss_target[0], wb, ps)
    grads['ssd_conv_w'] = _ungroup_channels(grads['ssd_conv_w'])
    grads['ssd_conv_b'] = _ungroup_channels(grads['ssd_conv_b'])

    shard_shapes = {n: sh for n, _, sh in SHARDED}
    slabs = []
    for s in range(N_CHIPS):
        bigs = []
        for n, nr in BIG_ROWS:
            bigs += _proj_rows(grads[n], nr * s, nr * (s + 1)) if n == 'w_in' else [grads[n][nr * s:nr * (s + 1)]]
        small = [grads[n][:, shard_shapes[n][1] * s:shard_shapes[n][1] * (s + 1)] for n in ('ssd_conv_w', 'ffn_conv_w')]
        slabs.append(_pack_rows(bigs, small + [grads[n] for n in REPLICATED]))
    gbuf = jnp.stack(slabs)
    pair = _pair_sum(gbuf, _pair_swap(gbuf), c_idx)
    mine = _chip_sum(pair, _chip_exchange(pair), pc_idx)
    red = _pair_gather(mine).reshape(gbuf.shape[1], COMM_LANES)
    g_red, r0 = {}, 0
    for n, nr in BIG_ROWS:
        g_red[n] = red[r0:r0 + nr].T if n in TRANSPOSED else red[r0:r0 + nr]
        r0 += nr
    flat, off = red[r0:].reshape(-1), 0
    for n in ('ssd_conv_w', 'ffn_conv_w') + REPLICATED:
        g_red[n], off = _take(flat, off, w[n].shape)

    small_names = [n for n in WEIGHTS if n not in MATMUL_WEIGHTS]
    delta, new_m, new_v = {}, {}, {}
    for n in MATMUL_WEIGHTS:
        delta[n], new_m[n], new_v[n] = _adamw("adamw_" + n, w[n], g_red[n], mom_m[n], mom_v[n], tr=64)
    packed = [_pack_small([d[n] for n in small_names]) for d in (w, g_red, mom_m, mom_v)]
    outs = _adamw("adamw_small", *packed, tr=packed[0].shape[0])
    for res, o in zip((delta, new_m, new_v), outs):
        fl, off = o.reshape(-1), 0
        for n in small_names:
            res[n], off = _take(fl, off, w[n].shape)

    loss_all = lax.psum(loss[0, 0], ("x", "y", "c"))
    shaped = lambda d: [d[n].reshape(given[n].shape) for n in WEIGHTS]
    return (loss_all, grad_x[None], *shaped(g_red), *shaped(delta), *shaped(new_m), *shaped(new_v))


def _pack_small(pieces):
    flat = jnp.concatenate([p.reshape(-1) for p in pieces])
    rows = -(-flat.shape[0] // (128 * 8)) * 8
    return jnp.pad(flat, (0, rows * 128 - flat.shape[0])).reshape(rows, 128)
```

```python
import functools

import jax
import jax.numpy as jnp
from jax import lax
from jax.experimental import pallas as pl
from jax.experimental.pallas import tpu as pltpu

F32 = jnp.float32
BF16 = jnp.bfloat16
SDS = jax.ShapeDtypeStruct
HIGHEST = lax.Precision.HIGHEST

D_MODEL = 1024
SSD_D_INNER = 2048
SSD_N_HEADS = 32
SSD_HEAD_DIM = 64
SSD_N_GROUPS = 4
SSD_HEADS_PER_GROUP = 8
SSD_D_STATE = 128
SSD_CONV_DIM = 3072
CHUNK = 128
ATTN_N_HEADS = 16
KV_WIDTH = 256
FFN_D_FF = 2816
IN_PROJ_DIM = 8736
ROPE_THETA = 10000.0
NORM_EPS = 1e-6
ADAM_LR, ADAM_B1, ADAM_B2, ADAM_EPS, ADAM_WD, ADAM_STEP = 0.001, 0.9, 0.999, 1e-08, 0.01, 10

PROJ_W = 9216
OFF_Q, OFF_K, OFF_V, OFF_Z, OFF_DT, OFF_GS, OFF_GA, OFF_XBC = 0, 1024, 1280, 1536, 3584, 4096, 5120, 6144
GROUP_W = 768
PROJ_SEGS = ([(0, 2048, OFF_Z)]
             + [(2048 + 512 * g, 512, OFF_XBC + GROUP_W * g) for g in range(4)]
             + [(4096 + 128 * g, 128, OFF_XBC + GROUP_W * g + 512) for g in range(4)]
             + [(4608 + 128 * g, 128, OFF_XBC + GROUP_W * g + 640) for g in range(4)]
             + [(5120, 32, OFF_DT), (5152, 1024, OFF_Q), (6176, 256, OFF_K), (6432, 256, OFF_V),
                (6688, 1024, OFF_GS), (7712, 1024, OFF_GA)])
VMEM_LIMIT_MB = 48
NEG = -1e30

WEIGHTS = ('norm_mix_pre_w', 'w_in', 'ssd_conv_w', 'ssd_conv_b', 'ssd_dt_bias', 'ssd_a_log', 'ssd_d', 'ssd_norm_w',
           'ssd_w_out', 'attn_sinks', 'attn_w_out', 'w_mix_out', 'norm_mix_post_w', 'norm_ffn_pre_w', 'ffn_w_up',
           'ffn_conv_w', 'ffn_conv_b', 'ffn_w_down', 'norm_ffn_post_w')
SHARDED = (('w_in', 1, (1024, 2184)), ('ssd_conv_w', 1, (4, 768)), ('ssd_w_out', 0, (512, 1024)),
           ('attn_w_out', 0, (256, 1024)), ('w_mix_out', 0, (256, 1024)), ('ffn_w_up', 1, (1024, 1408)),
           ('ffn_conv_w', 1, (3, 1408)), ('ffn_w_down', 0, (704, 1024)))
MATMUL_WEIGHTS = ('w_in', 'ssd_w_out', 'attn_w_out', 'w_mix_out', 'ffn_w_up', 'ffn_w_down')
REPLICATED = tuple(n for n in WEIGHTS if n not in {s[0] for s in SHARDED})
N_CHIPS = 4
COMM_LANES = 1024


def _cp(vmem_mb=VMEM_LIMIT_MB, **kw):
    return pltpu.CompilerParams(vmem_limit_bytes=vmem_mb << 20, **kw)


def _iota(shape, axis):
    return lax.broadcasted_iota(jnp.int32, shape, axis)


def _sigmoid(v):
    return 1.0 / (1.0 + jnp.exp(-v))


def _mm_call(name, a, b, *, tm, tn, tk, epilogue, outs, extra_in=(), trans_a=False, fill=None):
    if trans_a:
        kdim, m = a.shape
    else:
        m, kdim = a.shape
    n = b.shape[1]
    assert b.shape[0] == kdim and m % tm == 0 and n % tn == 0 and kdim % tk == 0, (name, a.shape, b.shape, tm, tn, tk)
    gi, gj, gk = m // tm, n // tn, kdim // tk
    n_in, n_out = len(extra_in), len(outs)

    n_fill = 0 if fill is None else 1

    def body(a_ref, b_ref, *rest):
        ins = rest[:n_in]
        rest = rest[n_in + n_fill:]
        out_refs, scratch = rest[:n_out], rest[n_out:]
        i, j, k = pl.program_id(0), pl.program_id(1), pl.program_id(2)
        av = a_ref[...].astype(BF16)
        bv = b_ref[...].astype(BF16)
        if trans_a:
            part = lax.dot_general(av, bv, (((0,), (0,)), ((), ())), preferred_element_type=F32)
        else:
            part = jnp.dot(av, bv, preferred_element_type=F32)
        if gk == 1:
            epilogue(part, i, j, ins, out_refs)
        else:
            acc = scratch[0]

            @pl.when(k == 0)
            def _():
                acc[...] = part

            @pl.when(k > 0)
            def _():
                acc[...] += part

            @pl.when(k == gk - 1)
            def _():
                epilogue(acc[...], i, j, ins, out_refs)

    a_spec = pl.BlockSpec((tk, tm), lambda i, j, k: (k, i)) if trans_a else pl.BlockSpec((tm, tk), lambda i, j, k: (i, k))
    in_specs = [a_spec, pl.BlockSpec((tk, tn), lambda i, j, k: (k, j))]
    in_specs += [pl.BlockSpec(bs, im) for _, bs, im in extra_in]
    operands = [a, b] + [e[0] for e in extra_in]
    aliases = {}
    if fill is not None:
        in_specs.append(pl.BlockSpec(memory_space=pl.ANY))
        aliases = {len(operands): fill[1]}
        operands.append(fill[0])
    return pl.pallas_call(
        body, name=name, grid=(gi, gj, gk), in_specs=in_specs,
        out_specs=[pl.BlockSpec(bs, im) for _, _, bs, im in outs],
        out_shape=[SDS(s, d) for s, d, _, _ in outs],
        scratch_shapes=[pltpu.VMEM((tm, tn), F32)] if gk > 1 else [],
        input_output_aliases=aliases,
        compiler_params=_cp(dimension_semantics=("arbitrary", "arbitrary", "arbitrary")),
    )(*operands)


def _mm_plain(name, a, b, *, tm, tn, tk, out_dtype=F32, trans_a=False):
    m = a.shape[1] if trans_a else a.shape[0]

    def epilogue(acc, i, j, ins, outs):
        outs[0][...] = acc.astype(out_dtype)

    return _mm_call(name, a, b, tm=tm, tn=tn, tk=tk, epilogue=epilogue, trans_a=trans_a,
                    outs=[((m, b.shape[1]), out_dtype, (tm, tn), lambda i, j, k: (i, j))])[0]


def _accumulate(ref, first, value):
    @pl.when(first)
    def _():
        ref[...] = value

    @pl.when(jnp.logical_not(first))
    def _():
        ref[...] += value


def _rms_bwd(xv, w, dy):
    r = lax.rsqrt(jnp.mean(xv * xv, axis=-1, keepdims=True) + NORM_EPS)
    xn = xv * r
    dxh = dy * w
    dx = r * (dxh - xn * jnp.mean(dxh * xn, axis=-1, keepdims=True))
    return dx, jnp.sum(dy * xn, axis=0, keepdims=True)


def _norm_mm(name, x, wn, w, *, tm, tn):
    t, dm = x.shape
    n = w.shape[1]
    tm = min(tm, t)

    def body(x_ref, wn_ref, w_ref, o_ref, u_ref):
        @pl.when(pl.program_id(1) == 0)
        def _():
            xv = x_ref[...]
            r = lax.rsqrt(jnp.mean(xv * xv, axis=-1, keepdims=True) + NORM_EPS)
            u_ref[...] = (xv * r * wn_ref[...]).astype(BF16)

        o_ref[...] = jnp.dot(u_ref[...], w_ref[...], preferred_element_type=F32)

    return pl.pallas_call(
        body, name=name, grid=(t // tm, n // tn),
        in_specs=[pl.BlockSpec((tm, dm), lambda i, j: (i, 0)), pl.BlockSpec((1, dm), lambda i, j: (0, 0)),
                  pl.BlockSpec((dm, tn), lambda i, j: (0, j))],
        out_specs=[pl.BlockSpec((tm, tn), lambda i, j: (i, j)), pl.BlockSpec((tm, dm), lambda i, j: (i, 0))],
        out_shape=[SDS((t, n), F32), SDS((t, dm), BF16)],
        compiler_params=_cp(dimension_semantics=("arbitrary", "arbitrary")),
    )(x, wn, w)


def _shift_down(tile, halo, s):
    if s == 0:
        return tile
    r = pltpu.roll(tile, s, axis=0)
    h = pltpu.roll(halo, s, axis=0)
    head = jnp.where(_iota(h.shape, 0) < s, h, r[0:8])
    return jnp.concatenate([head, r[8:]], axis=0)


def _shift_up(tile, halo, s):
    if s == 0:
        return tile
    n = tile.shape[0]
    r = pltpu.roll(tile, n - s, axis=0)
    h = pltpu.roll(halo, 8 - s, axis=0)
    tail = jnp.where(_iota(h.shape, 0) >= 8 - s, h, r[n - 8:])
    return jnp.concatenate([r[:n - 8], tail], axis=0)


def _conv_apply(tile, halo, wv, bv, kw):
    acc = bv + wv[kw - 1:kw, :] * tile
    for k in range(kw - 1):
        acc = acc + wv[k:k + 1, :] * _shift_down(tile, halo, kw - 1 - k)
    return acc


def _prev_halo_spec(tm, tc, col0):
    return pl.BlockSpec((8, tc), lambda i, j: (jnp.maximum(i * (tm // 8) - 1, 0), col0 + j))


def _silu_parts(pre):
    sg = _sigmoid(pre)
    return pre * sg, sg * (1.0 + pre * (1.0 - sg))


def _conv_silu_fwd(proj, w, b, *, tm, tc=512):
    t = proj.shape[0]
    c = w.shape[1]
    tm = min(tm, t)
    col0 = OFF_XBC // tc

    def body(x_ref, h_ref, w_ref, b_ref, o_ref):
        halo = jnp.where(pl.program_id(0) > 0, h_ref[...], 0.0)
        o_ref[...] = _silu_parts(_conv_apply(x_ref[...], halo, w_ref[...], b_ref[...], 4))[0]

    return pl.pallas_call(
        body, name="ssd_conv_fwd", grid=(t // tm, c // tc),
        in_specs=[pl.BlockSpec((tm, tc), lambda i, j: (i, col0 + j)), _prev_halo_spec(tm, tc, col0),
                  pl.BlockSpec((4, tc), lambda i, j: (0, j)), pl.BlockSpec((1, tc), lambda i, j: (0, j))],
        out_specs=pl.BlockSpec((tm, tc), lambda i, j: (i, j)),
        out_shape=SDS((t, c), F32),
        compiler_params=_cp(dimension_semantics=("arbitrary", "arbitrary")),
    )(proj, proj, w, b)


def _conv_silu_bwd1(d_out, proj, w, b, *, tm, tc=512):
    t = proj.shape[0]
    c = w.shape[1]
    tm = min(tm, t)
    col0 = OFF_XBC // tc

    def body(g_ref, x_ref, h_ref, w_ref, b_ref, o_ref, db_ref):
        i = pl.program_id(1)
        halo = jnp.where(i > 0, h_ref[...], 0.0)
        d_pre = g_ref[...] * _silu_parts(_conv_apply(x_ref[...], halo, w_ref[...], b_ref[...], 4))[1]
        o_ref[...] = d_pre.astype(BF16)
        _accumulate(db_ref, i == 0, jnp.sum(d_pre, axis=0, keepdims=True))

    return pl.pallas_call(
        body, name="ssd_conv_bwd1", grid=(c // tc, t // tm),
        in_specs=[pl.BlockSpec((tm, tc), lambda j, i: (i, j)), pl.BlockSpec((tm, tc), lambda j, i: (i, col0 + j)),
                  pl.BlockSpec((8, tc), lambda j, i: (jnp.maximum(i * (tm // 8) - 1, 0), col0 + j)),
                  pl.BlockSpec((4, tc), lambda j, i: (0, j)), pl.BlockSpec((1, tc), lambda j, i: (0, j))],
        out_specs=[pl.BlockSpec((tm, tc), lambda j, i: (i, j)), pl.BlockSpec((1, tc), lambda j, i: (0, j))],
        out_shape=[SDS((t, c), BF16), SDS((1, c), F32)],
        compiler_params=_cp(dimension_semantics=("arbitrary", "arbitrary")),
    )(d_out, proj, proj, w, b)


def _conv_bwd2(name, d_pre, src, src_col0, w, *, tm, tc, out_cols, out_col0, fill=None):
    t, c = d_pre.shape
    kw = w.shape[0]
    tm = min(tm, t)
    ni = t // tm
    col0 = src_col0 // tc
    ocol0 = out_col0 // tc

    def body(g_ref, gn_ref, x_ref, xh_ref, w_ref, *rest):
        o_ref, dw_ref = rest[-2:]
        i = pl.program_id(1)
        g = g_ref[...].astype(F32)
        g_next = jnp.where(i < ni - 1, gn_ref[...].astype(F32)[0:8], 0.0)
        x_prev = jnp.where(i > 0, xh_ref[...], 0.0)
        xv = x_ref[...]
        wv = w_ref[...]
        d_in = wv[kw - 1:kw, :] * g
        for k in range(kw - 1):
            d_in = d_in + wv[k:k + 1, :] * _shift_up(g, g_next, kw - 1 - k)
        o_ref[...] = d_in.astype(o_ref.dtype)
        rows = [jnp.sum(g * _shift_down(xv, x_prev, kw - 1 - k), axis=0, keepdims=True) for k in range(kw)]

        @pl.when(i == 0)
        def _():
            for k in range(kw):
                dw_ref[k:k + 1, :] = rows[k]

        @pl.when(i > 0)
        def _():
            for k in range(kw):
                dw_ref[k:k + 1, :] += rows[k]

    in_specs = [pl.BlockSpec((tm, tc), lambda j, i: (i, j)),
                pl.BlockSpec((16, tc), lambda j, i: (jnp.minimum((i + 1) * (tm // 16), t // 16 - 1), j)),
                pl.BlockSpec((tm, tc), lambda j, i: (i, col0 + j)),
                pl.BlockSpec((8, tc), lambda j, i: (jnp.maximum(i * (tm // 8) - 1, 0), col0 + j)),
                pl.BlockSpec((kw, tc), lambda j, i: (0, j))]
    operands = [d_pre, d_pre, src, src, w]
    if fill is not None:
        in_specs.append(pl.BlockSpec(memory_space=pl.ANY))
        operands.append(fill)
    return pl.pallas_call(
        body, name=name, grid=(c // tc, ni), in_specs=in_specs,
        out_specs=[pl.BlockSpec((tm, tc), lambda j, i: (i, ocol0 + j)), pl.BlockSpec((kw, tc), lambda j, i: (0, j))],
        out_shape=[SDS((t, out_cols), BF16), SDS((kw, c), F32)],
        input_output_aliases={} if fill is None else {5: 0},
        compiler_params=_cp(dimension_semantics=("arbitrary", "arbitrary")),
    )(*operands)


GELU_C = 0.7978845608028654


def _gelu_parts(v):
    inner = GELU_C * (v + 0.044715 * v * v * v)
    th = jnp.tanh(inner)
    val = 0.5 * v * (1.0 + th)
    grad = 0.5 * (1.0 + th) + 0.5 * v * (1.0 - th * th) * GELU_C * (1.0 + 3.0 * 0.044715 * v * v)
    return val, grad


def _ffn_act_specs(tm, tc, nj, order):
    def im(f):
        return (lambda i, j: f(i, j)) if order == "ij" else (lambda j, i: f(i, j))
    halo = lambda i: jnp.maximum(i * (tm // 8) - 1, 0)
    return [pl.BlockSpec((tm, tc), im(lambda i, j: (i, j))), pl.BlockSpec((8, tc), im(lambda i, j: (halo(i), j))),
            pl.BlockSpec((tm, tc), im(lambda i, j: (i, nj + j))), pl.BlockSpec((8, tc), im(lambda i, j: (halo(i), nj + j))),
            pl.BlockSpec((3, tc), im(lambda i, j: (0, j))), pl.BlockSpec((3, tc), im(lambda i, j: (0, nj + j))),
            pl.BlockSpec((1, tc), im(lambda i, j: (0, j))), pl.BlockSpec((1, tc), im(lambda i, j: (0, nj + j)))]


def _ffn_act_fwd(up_raw, w, b, *, tm, tc=1408):
    t = up_raw.shape[0]
    tm = min(tm, t)
    nj = FFN_D_FF // tc

    def body(g_ref, gh_ref, v_ref, vh_ref, wg_ref, wv_ref, bg_ref, bv_ref, o_ref):
        first = pl.program_id(0) > 0
        gate = _conv_apply(g_ref[...], jnp.where(first, gh_ref[...], 0.0), wg_ref[...], bg_ref[...], 3)
        val = _conv_apply(v_ref[...], jnp.where(first, vh_ref[...], 0.0), wv_ref[...], bv_ref[...], 3)
        o_ref[...] = (_gelu_parts(gate)[0] * val).astype(BF16)

    return pl.pallas_call(
        body, name="ffn_act_fwd", grid=(t // tm, nj), in_specs=_ffn_act_specs(tm, tc, nj, "ij"),
        out_specs=pl.BlockSpec((tm, tc), lambda i, j: (i, j)), out_shape=SDS((t, FFN_D_FF), BF16),
        compiler_params=_cp(dimension_semantics=("arbitrary", "arbitrary")),
    )(up_raw, up_raw, up_raw, up_raw, w, w, b, b)


def _ffn_act_bwd(up_raw, d_act, w, b, *, tm, tc=1408):
    t = up_raw.shape[0]
    tm = min(tm, t)
    nj = FFN_D_FF // tc

    def body(g_ref, gh_ref, v_ref, vh_ref, wg_ref, wv_ref, bg_ref, bv_ref, da_ref, dg_ref, dv_ref, dbg_ref, dbv_ref):
        i = pl.program_id(1)
        gate = _conv_apply(g_ref[...], jnp.where(i > 0, gh_ref[...], 0.0), wg_ref[...], bg_ref[...], 3)
        val = _conv_apply(v_ref[...], jnp.where(i > 0, vh_ref[...], 0.0), wv_ref[...], bv_ref[...], 3)
        ge, dge = _gelu_parts(gate)
        da = da_ref[...].astype(F32)
        d_gate = da * val * dge
        d_val = da * ge
        dg_ref[...] = d_gate.astype(BF16)
        dv_ref[...] = d_val.astype(BF16)
        _accumulate(dbg_ref, i == 0, jnp.sum(d_gate, axis=0, keepdims=True))
        _accumulate(dbv_ref, i == 0, jnp.sum(d_val, axis=0, keepdims=True))

    tile = pl.BlockSpec((tm, tc), lambda j, i: (i, j))
    row = pl.BlockSpec((1, tc), lambda j, i: (0, j))
    return pl.pallas_call(
        body, name="ffn_act_bwd", grid=(nj, t // tm), in_specs=_ffn_act_specs(tm, tc, nj, "ji") + [tile],
        out_specs=[tile, tile, row, row],
        out_shape=[SDS((t, FFN_D_FF), BF16), SDS((t, FFN_D_FF), BF16), SDS((1, FFN_D_FF), F32), SDS((1, FFN_D_FF), F32)],
        compiler_params=_cp(dimension_semantics=("arbitrary", "arbitrary")),
    )(up_raw, up_raw, up_raw, up_raw, w, w, b, b, d_act)


def _softplus(v):
    e = jnp.exp(-jnp.abs(v))
    small = e * (1.0 - 0.5 * e)
    return jnp.maximum(v, 0.0) + jnp.where(e < 1e-4, small, jnp.log(1.0 + e))


def _dt_fwd(proj, bias_pad, *, tm):
    t = proj.shape[0]
    tm = min(tm, t)

    def body(x_ref, b_ref, o_ref):
        o_ref[...] = _softplus(x_ref[...] + b_ref[...])

    return pl.pallas_call(
        body, name="dt_fwd", grid=(t // tm,),
        in_specs=[pl.BlockSpec((tm, 128), lambda i: (i, OFF_DT // 128)), pl.BlockSpec((1, 128), lambda i: (0, 0))],
        out_specs=pl.BlockSpec((tm, 128), lambda i: (i, 0)), out_shape=SDS((t, 128), F32),
        compiler_params=_cp(dimension_semantics=("arbitrary",)),
    )(proj, bias_pad)


def _dt_bwd(d_dt, proj, bias_pad, d_proj, *, tm):
    t = proj.shape[0]
    tm = min(tm, t)

    def body(g_ref, x_ref, b_ref, _, o_ref, db_ref):
        d_raw = g_ref[...] * _sigmoid(x_ref[...] + b_ref[...])
        o_ref[:, 0:128] = d_raw.astype(BF16)
        o_ref[:, 128:512] = jnp.zeros((tm, 384), BF16)
        _accumulate(db_ref, pl.program_id(0) == 0, jnp.sum(d_raw, axis=0, keepdims=True))

    return pl.pallas_call(
        body, name="dt_bwd", grid=(t // tm,),
        in_specs=[pl.BlockSpec((tm, 128), lambda i: (i, 0)), pl.BlockSpec((tm, 128), lambda i: (i, OFF_DT // 128)),
                  pl.BlockSpec((1, 128), lambda i: (0, 0)), pl.BlockSpec(memory_space=pl.ANY)],
        out_specs=[pl.BlockSpec((tm, 512), lambda i: (i, OFF_DT // 512)), pl.BlockSpec((1, 128), lambda i: (0, 0))],
        out_shape=[SDS((t, PROJ_W), BF16), SDS((1, 128), F32)],
        input_output_aliases={3: 0},
        compiler_params=_cp(dimension_semantics=("arbitrary",)),
    )(d_dt, proj, bias_pad, d_proj)


def _split3(v):
    hi = v.astype(BF16)
    r1 = v - hi.astype(F32)
    mid = r1.astype(BF16)
    return hi, mid, (r1 - mid.astype(F32)).astype(BF16)


def _times01(v, m3):
    return jnp.dot(jnp.concatenate(_split3(v), axis=1), m3, preferred_element_type=F32)


def _01times(m3, v):
    return jnp.dot(m3, jnp.concatenate(_split3(v), axis=0), preferred_element_type=F32)


def _ssd_decay(dt_ref, dtT_ref, al_ref, alT_ref, k):
    dt = dt_ref[0]
    a_row = -jnp.exp(al_ref[0])
    adt_t = dtT_ref[0] * (-jnp.exp(alT_ref[0]))
    return dt, a_row, _01times(k['low3'][...], dt * a_row), _times01(adt_t, k['up3v'][...])


def _ssd_specs(nc, rev):
    ci = (lambda c: nc - 1 - c) if rev else (lambda c: c)
    return [pl.BlockSpec((CHUNK, GROUP_W), lambda c, g: (ci(c), g)),
            pl.BlockSpec((1, CHUNK, 128), lambda c, g: (g, ci(c), 0)),
            pl.BlockSpec((1, 8, CHUNK), lambda c, g: (g, 0, ci(c))),
            pl.BlockSpec((1, 1, 128), lambda c, g: (g, 0, 0)),
            pl.BlockSpec((1, 8, 1), lambda c, g: (g, 0, 0)),
            pl.BlockSpec((1, 512), lambda c, g: (0, g))]


NT = (((1,), (1,)), ((), ()))
WIDE = 8 * CHUNK
SSD_CONST_NAMES = ('e128', 'e64', 's64', 'mlo', 'mup', 'low3', 'up3', 'up3v')
SSD_CONST_SHAPES = [pltpu.VMEM((3 * CHUNK, WIDE), BF16), pltpu.VMEM((3 * CHUNK, 512), BF16), pltpu.VMEM((512, CHUNK), BF16),
                    pltpu.VMEM((CHUNK, WIDE), F32), pltpu.VMEM((CHUNK, WIDE), F32), pltpu.VMEM((CHUNK, 3 * CHUNK), BF16),
                    pltpu.VMEM((CHUNK, 3 * CHUNK), BF16), pltpu.VMEM((3 * CHUNK, CHUNK), BF16)]


def _ssd_init_consts(k):
    row, col = _iota((3 * CHUNK, WIDE), 0), _iota((3 * CHUNK, WIDE), 1)
    k['e128'][...] = ((col >> 7) == (row & 127)).astype(BF16)
    k['e64'][...] = ((_iota((3 * CHUNK, 512), 1) >> 6) == (_iota((3 * CHUNK, 512), 0) & 127)).astype(BF16)
    k['s64'][...] = ((_iota((512, CHUNK), 0) >> 6) == _iota((512, CHUNK), 1)).astype(BF16)
    row, col = _iota((CHUNK, WIDE), 0), _iota((CHUNK, WIDE), 1)
    k['mlo'][...] = (row >= (col & 127)).astype(F32)
    k['mup'][...] = (row <= (col & 127)).astype(F32)
    row, col = _iota((CHUNK, 3 * CHUNK), 0), _iota((CHUNK, 3 * CHUNK), 1) & 127
    k['low3'][...] = (row >= col).astype(BF16)
    k['up3'][...] = (row <= col).astype(BF16)
    row, col = _iota((3 * CHUNK, CHUNK), 0) & 127, _iota((3 * CHUNK, CHUNK), 1)
    k['up3v'][...] = (row <= col).astype(BF16)


def _ssd_common(x_ref, dt_ref, dtT_ref, al_ref, alT_ref, k):
    dt, a_row, acs, acs_t = _ssd_decay(dt_ref, dtT_ref, al_ref, alT_ref, k)
    ecol = _times01(acs, k['e128'][...])
    rrow = jnp.concatenate([jnp.broadcast_to(acs_t[j:j + 1, :], (CHUNK, CHUNK)) for j in range(8)], axis=1)
    a64 = _times01(acs, k['e64'][...])
    dt64 = _times01(dt, k['e64'][...])
    a_end64 = a64[CHUNK - 1:CHUNK, :]
    xs = x_ref[:, 0:512]
    return dict(dt=dt, a_row=a_row, acs=acs, seg=ecol - rrow, dt64=dt64, e_a=jnp.exp(a64), decay=jnp.exp(a_end64 - a64),
                e_end64=jnp.exp(a_end64), xs=xs, xdt=xs * dt64, bm=x_ref[:, 512:640], cm=x_ref[:, 640:768])


def _pair_blocks(v):
    lo = _iota((CHUNK, 128), 1) < 64
    out = []
    for i in range(4):
        ch = v[:, i * 128:(i + 1) * 128]
        out.append(jnp.concatenate([jnp.where(lo, ch, 0.0), jnp.where(lo, 0.0, ch)], axis=0).astype(BF16))
    return out


def _tile8(m):
    return jnp.concatenate([m] * 8, axis=1)


def _ssd_fwd(xc, dtg, dtg_t, alog, alog_t, d_exp):
    t = xc.shape[0]
    nc = t // CHUNK

    def body(x_ref, dt_ref, dtT_ref, al_ref, alT_ref, d_ref, y_ref, hs_ref, h_scr, *consts):
        c, g = pl.program_id(0), pl.program_id(1)
        k = dict(zip(SSD_CONST_NAMES, consts))

        @pl.when(jnp.logical_and(c == 0, g == 0))
        def _():
            _ssd_init_consts(k)

        @pl.when(c == 0)
        def _():
            h_scr[g] = jnp.zeros((SSD_D_STATE, 512), F32)

        v = _ssd_common(x_ref, dt_ref, dtT_ref, al_ref, alT_ref, k)
        b16, c16 = v['bm'].astype(BF16), v['cm'].astype(BF16)
        cb = lax.dot_general(c16, b16, NT, preferred_element_type=F32)
        m16 = (jnp.exp(jnp.minimum(v['seg'], 0.0)) * k['mlo'][...] * _tile8(cb)).astype(BF16)
        xbd = _pair_blocks(v['xdt'])
        y_diag = jnp.concatenate([jnp.dot(m16[:, i * 256:(i + 1) * 256], xbd[i], preferred_element_type=F32)
                                  for i in range(4)], axis=1)
        ht = h_scr[g]
        y_off = jnp.dot(c16, ht.astype(BF16), preferred_element_type=F32)
        y_ref[...] = y_diag + v['e_a'] * y_off + d_ref[...] * v['xs']
        st = jnp.dot(v['bm'].T.astype(BF16), (v['xdt'] * v['decay']).astype(BF16), preferred_element_type=F32)
        hs_ref[0, 0] = ht
        h_scr[g] = ht * v['e_end64'] + st

    return pl.pallas_call(
        body, name="ssd_fwd", grid=(nc, SSD_N_GROUPS), in_specs=_ssd_specs(nc, False),
        out_specs=[pl.BlockSpec((CHUNK, 512), lambda c, g: (c, g)),
                   pl.BlockSpec((1, 1, SSD_D_STATE, 512), lambda c, g: (c, g, 0, 0))],
        out_shape=[SDS((t, SSD_D_INNER), F32), SDS((nc, SSD_N_GROUPS, SSD_D_STATE, 512), F32)],
        scratch_shapes=[pltpu.VMEM((SSD_N_GROUPS, SSD_D_STATE, 512), F32)] + SSD_CONST_SHAPES,
        compiler_params=_cp(dimension_semantics=("arbitrary", "arbitrary")),
    )(xc, dtg, dtg_t, alog, alog_t, d_exp)


def _ssd_bwd(xc, dtg, dtg_t, alog, alog_t, d_exp, d_y, hs):
    t = xc.shape[0]
    nc = t // CHUNK

    def body(x_ref, dt_ref, dtT_ref, al_ref, alT_ref, d_ref, dy_ref, hs_ref,
             dx_ref, ddt_ref, dal_ref, dd_ref, g_scr, *consts):
        c, g = pl.program_id(0), pl.program_id(1)
        k = dict(zip(SSD_CONST_NAMES, consts))
        s64, mlo, mup = k['s64'], k['mlo'], k['mup']

        @pl.when(jnp.logical_and(c == 0, g == 0))
        def _():
            _ssd_init_consts(k)

        @pl.when(c == 0)
        def _():
            g_scr[g] = jnp.zeros((SSD_D_STATE, 512), F32)

        v = _ssd_common(x_ref, dt_ref, dtT_ref, al_ref, alT_ref, k)
        dt, a_row, xs, xdt, e_a, decay = v['dt'], v['a_row'], v['xs'], v['xdt'], v['e_a'], v['decay']
        row, col = _iota((CHUNK, CHUNK), 0), _iota((CHUNK, CHUNK), 1)
        b16, c16 = v['bm'].astype(BF16), v['cm'].astype(BF16)
        ct16 = v['cm'].T.astype(BF16)
        cb = lax.dot_general(c16, b16, NT, preferred_element_type=F32)
        cbt = lax.dot_general(b16, c16, NT, preferred_element_type=F32)
        lmat = jnp.exp(jnp.minimum(v['seg'], 0.0)) * mlo[...]
        lmat_t = jnp.exp(jnp.minimum(-v['seg'], 0.0)) * mup[...]
        mmat, mmat_t = lmat * _tile8(cb), lmat_t * _tile8(cbt)
        mt16 = mmat_t.astype(BF16)
        dy = dy_ref[...]
        dye, xdec = dy * e_a, xdt * decay
        dy16, dye16, xdec16 = dy.astype(BF16), dye.astype(BF16), xdec.astype(BF16)
        xdt16 = xdt.astype(BF16)
        ht, gt = hs_ref[0, 0], g_scr[g]
        ht16, gt16 = ht.astype(BF16), gt.astype(BF16)
        xbd, dybd = _pair_blocks(xdt), _pair_blocks(dy)
        d_m, d_mt, d_x = [], [], []
        for i in range(4):
            csl = slice(i * 128, (i + 1) * 128)
            d_m.append(lax.dot_general(dy16[:, csl], xbd[i], NT, preferred_element_type=F32))
            d_mt.append(lax.dot_general(xdt16[:, csl], dybd[i], NT, preferred_element_type=F32))
            d_x.append(jnp.dot(mt16[:, i * 256:(i + 1) * 256], dybd[i], preferred_element_type=F32))
        d_m, d_mt, d_x = jnp.concatenate(d_m, axis=1), jnp.concatenate(d_mt, axis=1), jnp.concatenate(d_x, axis=1)

        def head_sum(m):
            acc = m[:, 0:CHUNK]
            for j in range(1, 8):
                acc = acc + m[:, j * CHUNK:(j + 1) * CHUNK]
            return acc

        def seg64(p):
            return jnp.dot(p.astype(BF16), s64[...], preferred_element_type=F32)

        d_cb16 = head_sum(d_m * lmat).astype(BF16)
        d_cbt16 = head_sum(d_mt * lmat_t).astype(BF16)
        dseg = d_m * mmat - d_mt * mmat_t
        da_seg = jnp.zeros((CHUNK, CHUNK), F32)
        for j in range(8):
            da_seg = jnp.where(col == j, jnp.sum(dseg[:, j * CHUNK:(j + 1) * CHUNK], axis=1, keepdims=True), da_seg)
        ch = jnp.dot(c16, ht16, preferred_element_type=F32)
        bg = jnp.dot(b16, gt16, preferred_element_type=F32)
        d_x = d_x + decay * bg
        d_decay = seg64(xdec * bg)
        e_end = jnp.exp(v['acs'][CHUNK - 1:CHUNK, :])
        d_end = e_end * jnp.sum(seg64(gt * ht), axis=0, keepdims=True) + jnp.sum(d_decay, axis=0, keepdims=True)
        d_a = seg64(dye * ch) - d_decay + da_seg + jnp.where(row == CHUNK - 1, d_end, 0.0)
        dx_ref[:, 0:512] = d_x * v['dt64'] + d_ref[...] * dy
        dx_ref[:, 640:768] = (lax.dot_general(dye16, ht16, NT, preferred_element_type=F32)
                              + jnp.dot(d_cb16, b16, preferred_element_type=F32))
        dx_ref[:, 512:640] = (lax.dot_general(xdec16, gt16, NT, preferred_element_type=F32)
                              + jnp.dot(d_cbt16, c16, preferred_element_type=F32))
        g_scr[g] = gt * v['e_end64'] + jnp.dot(ct16, dye16, preferred_element_type=F32)
        d_adt = _01times(k['up3'][...], d_a)
        ddt_ref[0] = d_adt * a_row + seg64(d_x * xs)
        d_alog = jnp.sum(d_adt * dt, axis=0, keepdims=True) * a_row
        dd_row = jnp.sum(seg64(dy * xs), axis=0, keepdims=True)
        first = c == 0

        @pl.when(first)
        def _():
            dal_ref[g] = d_alog
            dd_ref[g] = dd_row

        @pl.when(jnp.logical_not(first))
        def _():
            dal_ref[g] += d_alog
            dd_ref[g] += dd_row

    rc = lambda c: nc - 1 - c
    whole = pl.BlockSpec((SSD_N_GROUPS, 1, 128), lambda c, g: (0, 0, 0))
    return pl.pallas_call(
        body, name="ssd_bwd", grid=(nc, SSD_N_GROUPS),
        in_specs=_ssd_specs(nc, True) + [pl.BlockSpec((CHUNK, 512), lambda c, g: (rc(c), g)),
                                        pl.BlockSpec((1, 1, SSD_D_STATE, 512), lambda c, g: (rc(c), g, 0, 0))],
        out_specs=[pl.BlockSpec((CHUNK, GROUP_W), lambda c, g: (rc(c), g)),
                   pl.BlockSpec((1, CHUNK, 128), lambda c, g: (g, rc(c), 0)), whole, whole],
        out_shape=[SDS((t, SSD_CONV_DIM), F32), SDS((SSD_N_GROUPS, t, 128), F32),
                   SDS((SSD_N_GROUPS, 1, 128), F32), SDS((SSD_N_GROUPS, 1, 128), F32)],
        scratch_shapes=[pltpu.VMEM((SSD_N_GROUPS, SSD_D_STATE, 512), F32)] + SSD_CONST_SHAPES,
        compiler_params=_cp(dimension_semantics=("arbitrary", "arbitrary")),
    )(xc, dtg, dtg_t, alog, alog_t, d_exp, d_y, hs)


def _gated_norm_fwd(y, proj, w, *, tm):
    t = y.shape[0]
    tm = min(tm, t)

    def body(y_ref, z_ref, w_ref, o_ref):
        gv = y_ref[...] * _silu_parts(z_ref[...])[0]
        r = lax.rsqrt(jnp.mean(gv * gv, axis=-1, keepdims=True) + NORM_EPS)
        o_ref[...] = (gv * r * w_ref[...]).astype(BF16)

    tile = pl.BlockSpec((tm, 512), lambda i, g: (i, g))
    return pl.pallas_call(
        body, name="gated_norm_fwd", grid=(t // tm, SSD_N_GROUPS),
        in_specs=[tile, pl.BlockSpec((tm, 512), lambda i, g: (i, OFF_Z // 512 + g)),
                  pl.BlockSpec((1, 512), lambda i, g: (0, g))], out_specs=tile,
        out_shape=SDS((t, SSD_D_INNER), BF16),
        compiler_params=_cp(dimension_semantics=("arbitrary", "arbitrary")),
    )(y, proj, w)


def _rope(ch, cos_t, sin_t):
    first = (_iota(ch.shape, 1) & 32) == 0
    partner = jnp.where(first, pltpu.roll(ch, 96, axis=1), pltpu.roll(ch, 32, axis=1))
    return ch * cos_t + partner * sin_t


def _rope_qkv(proj, cos_t, sin_t, *, tm):
    t = proj.shape[0]
    tm = min(tm, t)

    def body(q_ref, k_ref, v_ref, c_ref, s_ref, qr_ref, kp_ref, vp_ref):
        cv, sv = c_ref[...], s_ref[...]
        lo = _iota((tm, 128), 1) < 64
        for m in range(8):
            sl = slice(m * 128, (m + 1) * 128)
            qr_ref[:, sl] = (_rope(q_ref[:, sl], cv, sv) * 0.125).astype(BF16)
        for m2 in range(2):
            sl = slice(m2 * 128, (m2 + 1) * 128)
            for src, dst in ((_rope(k_ref[:, sl], cv, sv), kp_ref), (v_ref[:, sl], vp_ref)):
                sw = pltpu.roll(src, 64, axis=1)
                base = 4 * m2 * 128
                dst[:, base:base + 128] = jnp.where(lo, src, 0.0).astype(BF16)
                dst[:, base + 128:base + 256] = jnp.where(lo, 0.0, sw).astype(BF16)
                dst[:, base + 256:base + 384] = jnp.where(lo, sw, 0.0).astype(BF16)
                dst[:, base + 384:base + 512] = jnp.where(lo, 0.0, src).astype(BF16)

    return pl.pallas_call(
        body, name="rope_qkv", grid=(t // tm,),
        in_specs=[pl.BlockSpec((tm, 1024), lambda i: (i, OFF_Q // 1024)), pl.BlockSpec((tm, 256), lambda i: (i, OFF_K // 256)),
                  pl.BlockSpec((tm, 256), lambda i: (i, OFF_V // 256)), pl.BlockSpec((tm, 128), lambda i: (i, 0)),
                  pl.BlockSpec((tm, 128), lambda i: (i, 0))],
        out_specs=[pl.BlockSpec((tm, 1024), lambda i: (i, 0))] * 3,
        out_shape=[SDS((t, 1024), BF16)] * 3,
        compiler_params=_cp(dimension_semantics=("arbitrary",)),
    )(proj, proj, proj, cos_t, sin_t)


def _attn_valid(n):
    qi, kj = _iota((CHUNK, 2 * CHUNK), 0), _iota((CHUNK, 2 * CHUNK), 1)
    return (kj > qi) & (kj <= qi + CHUNK) & ((n > 0) | (kj >= CHUNK))


def _attn_fwd(qr, kp, vp, sinks):
    t = qr.shape[0]
    nb = t // CHUNK

    def body(q_ref, kc_ref, kprev_ref, vc_ref, vprev_ref, sk_ref, o_ref, lse_ref):
        n = pl.program_id(0)
        valid = _attn_valid(n)
        lane = _iota((CHUNK, 128), 1)
        lse_all = jnp.zeros((CHUNK, 128), F32)
        for m in range(8):
            g = m // 2
            qch = q_ref[:, m * 128:(m + 1) * 128]
            o_pair = jnp.zeros((CHUNK, 128), F32)
            for e in range(2):
                h = 2 * m + e
                sl = slice((2 * g + e) * 128, (2 * g + e + 1) * 128)
                kk = jnp.concatenate([kprev_ref[:, sl], kc_ref[:, sl]], axis=0)
                vv = jnp.concatenate([vprev_ref[:, sl], vc_ref[:, sl]], axis=0)
                s = lax.dot_general(qch, kk, (((1,), (1,)), ((), ())), preferred_element_type=F32)
                s = jnp.where(valid, s, NEG)
                sink = sk_ref[0:1, h:h + 1]
                mx = jnp.maximum(jnp.max(s, axis=1, keepdims=True), sink)
                p = jnp.exp(s - mx)
                den = jnp.sum(p, axis=1, keepdims=True) + jnp.exp(sink - mx)
                o_pair = o_pair + jnp.dot((p / den).astype(BF16), vv, preferred_element_type=F32)
                lse_all = jnp.where(lane == h, mx + jnp.log(den), lse_all)
            o_ref[:, m * 128:(m + 1) * 128] = o_pair.astype(BF16)
        lse_ref[...] = lse_all

    cur = pl.BlockSpec((CHUNK, 1024), lambda n: (n, 0))
    prev = pl.BlockSpec((CHUNK, 1024), lambda n: (jnp.maximum(n - 1, 0), 0))
    return pl.pallas_call(
        body, name="attn_fwd", grid=(nb,),
        in_specs=[cur, cur, prev, cur, prev, pl.BlockSpec((1, 128), lambda n: (0, 0))],
        out_specs=[cur, pl.BlockSpec((CHUNK, 128), lambda n: (n, 0))],
        out_shape=[SDS((t, 1024), BF16), SDS((t, 128), F32)],
        compiler_params=_cp(dimension_semantics=("arbitrary",)),
    )(qr, kp, kp, vp, vp, sinks)


def _attn_bwd(qr, kp, vp, d_o, o, lse, sinks, cos_t, sin_t, d_proj):
    t = qr.shape[0]
    nb = t // CHUNK

    def body(q_ref, kc_ref, kprev_ref, vc_ref, vprev_ref, do_ref, o_ref, lse_ref, sk_ref, c_ref, s_ref, cp_ref, sp_ref,
             _, dqkv_ref, dsk_ref, acc_k, acc_v, dq_scr):
        n = pl.program_id(0)
        lane = _iota((CHUNK, 128), 1)
        lo = lane < 64
        lane1 = _iota((1, 128), 1)

        @pl.when(n == 0)
        def _():
            acc_k[...] = jnp.zeros_like(acc_k)
            acc_v[...] = jnp.zeros_like(acc_v)
            dsk_ref[...] = jnp.zeros((1, 128), F32)

        @pl.when(n > 0)
        def _():
            dqkv_ref[:, 0:1024] = dq_scr[...]
            for r in range(8):
                acc_k[r, 0:CHUNK] = acc_k[r, CHUNK:2 * CHUNK]
                acc_v[r, 0:CHUNK] = acc_v[r, CHUNK:2 * CHUNK]
                acc_k[r, CHUNK:2 * CHUNK] = jnp.zeros((CHUNK, 128), F32)
                acc_v[r, CHUNK:2 * CHUNK] = jnp.zeros((CHUNK, 128), F32)

        @pl.when(n < nb)
        def _():
            valid = _attn_valid(n)
            lse_all = lse_ref[...]
            dsk = jnp.zeros((1, 128), F32)
            for m in range(8):
                g = m // 2
                csl = slice(m * 128, (m + 1) * 128)
                qch = q_ref[:, csl]
                doch = do_ref[:, csl]
                prod = doch.astype(F32) * o_ref[:, csl].astype(F32)
                dq_pair = jnp.zeros((CHUNK, 128), F32)
                for e in range(2):
                    h = 2 * m + e
                    sl = slice((2 * g + e) * 128, (2 * g + e + 1) * 128)
                    kk = jnp.concatenate([kprev_ref[:, sl], kc_ref[:, sl]], axis=0)
                    vv = jnp.concatenate([vprev_ref[:, sl], vc_ref[:, sl]], axis=0)
                    lse_h = lse_all[:, h:h + 1]
                    s = lax.dot_general(qch, kk, (((1,), (1,)), ((), ())), preferred_element_type=F32)
                    p = jnp.exp(jnp.where(valid, s, NEG) - lse_h)
                    delta = jnp.sum(jnp.where(lo if e == 0 else jnp.logical_not(lo), prod, 0.0), axis=1, keepdims=True)
                    d_p = lax.dot_general(doch, vv, (((1,), (1,)), ((), ())), preferred_element_type=F32)
                    d_s16 = (p * (d_p - delta)).astype(BF16)
                    dq_pair = dq_pair + jnp.dot(d_s16, kk, preferred_element_type=F32)
                    acc_k[2 * g + e] += lax.dot_general(d_s16, qch, (((0,), (0,)), ((), ())), preferred_element_type=F32)
                    acc_v[2 * g + e] += lax.dot_general(p.astype(BF16), doch, (((0,), (0,)), ((), ())), preferred_element_type=F32)
                    p_sink = jnp.exp(sk_ref[0:1, h:h + 1] - lse_h)
                    dsk = jnp.where(lane1 == h, -jnp.sum(p_sink * delta), dsk)
                dq_scr[:, csl] = (_rope(dq_pair, c_ref[...], -s_ref[...]) * 0.125).astype(BF16)
            dsk_ref[...] += dsk

        @pl.when(n > 0)
        def _():
            for m2 in range(2):
                halves = []
                for g in (2 * m2, 2 * m2 + 1):
                    for acc in (acc_k, acc_v):
                        comb = jnp.where(lo, acc[2 * g, 0:CHUNK], acc[2 * g + 1, 0:CHUNK])
                        halves.append(comb + pltpu.roll(comb, 64, axis=1))
                d_kr = jnp.where(lo, halves[0], halves[2])
                d_v = jnp.where(lo, halves[1], halves[3])
                dqkv_ref[:, OFF_K + m2 * 128:OFF_K + (m2 + 1) * 128] = _rope(d_kr, cp_ref[...], -sp_ref[...]).astype(BF16)
                dqkv_ref[:, OFF_V + m2 * 128:OFF_V + (m2 + 1) * 128] = d_v.astype(BF16)

    qn = lambda n: jnp.minimum(n, nb - 1)
    pn = lambda n: jnp.maximum(jnp.minimum(n, nb) - 1, 0)
    cur = pl.BlockSpec((CHUNK, 1024), lambda n: (qn(n), 0))
    prev = pl.BlockSpec((CHUNK, 1024), lambda n: (pn(n), 0))
    cur128 = pl.BlockSpec((CHUNK, 128), lambda n: (qn(n), 0))
    prev128 = pl.BlockSpec((CHUNK, 128), lambda n: (pn(n), 0))
    one = pl.BlockSpec((1, 128), lambda n: (0, 0))
    return pl.pallas_call(
        body, name="attn_bwd", grid=(nb + 1,),
        in_specs=[cur, cur, prev, cur, prev, cur, cur, cur128, one, cur128, cur128, prev128, prev128,
                  pl.BlockSpec(memory_space=pl.ANY)],
        out_specs=[pl.BlockSpec((CHUNK, 1536), lambda n: (pn(n), 0)), one],
        out_shape=[SDS((t, PROJ_W), BF16), SDS((1, 128), F32)],
        scratch_shapes=[pltpu.VMEM((8, 2 * CHUNK, 128), F32), pltpu.VMEM((8, 2 * CHUNK, 128), F32),
                        pltpu.VMEM((CHUNK, 1024), BF16)],
        input_output_aliases={13: 0},
        compiler_params=_cp(dimension_semantics=("arbitrary",)),
    )(qr, kp, kp, vp, vp, d_o, o, lse, sinks, cos_t, sin_t, cos_t, sin_t, d_proj)


def _adamw(name, w, g, m, v, *, tr):
    rows, cols = w.shape
    tr = min(tr, rows)
    assert rows % tr == 0

    def body(w_ref, g_ref, m_ref, v_ref, d_ref, nm_ref, nv_ref):
        gv = g_ref[...]
        nm = ADAM_B1 * m_ref[...] + (1.0 - ADAM_B1) * gv
        nv = ADAM_B2 * v_ref[...] + (1.0 - ADAM_B2) * (gv * gv)
        m_hat = nm / (1.0 - ADAM_B1 ** ADAM_STEP)
        v_hat = nv / (1.0 - ADAM_B2 ** ADAM_STEP)
        d_ref[...] = -ADAM_LR * (m_hat / (jnp.sqrt(v_hat) + ADAM_EPS) + ADAM_WD * w_ref[...])
        nm_ref[...] = nm
        nv_ref[...] = nv

    tile = pl.BlockSpec((tr, cols), lambda i: (i, 0))
    return pl.pallas_call(
        body, name=name, grid=(rows // tr,), in_specs=[tile] * 4, out_specs=[tile] * 3,
        out_shape=[SDS((rows, cols), F32)] * 3, compiler_params=_cp(dimension_semantics=("arbitrary",)),
    )(w, g, m, v)


def _local_step(x, cos_t, sin_t, tgt, wb, ps):
    t = x.shape[0]
    tm = min(512, t)
    tmw = min(1024, t)
    ij = lambda i, j, k: (i, j)
    i0 = lambda i, j, k: (i, 0)
    c0 = lambda i, j, k: (0, 0)
    cj = lambda i, j, k: (0, j)

    tkt = min(2048, t)
    proj, u = _norm_mm("in_proj", x, ps['norm_mix_pre_w'], wb['cat'], tm=tmw, tn=1024)
    xc = _conv_silu_fwd(proj, ps['ssd_conv_w'], ps['ssd_conv_b'], tm=tm)
    bias_pad = jnp.pad(ps['ssd_dt_bias'], ((0, 0), (0, 96)))
    dt = _dt_fwd(proj, bias_pad, tm=tmw)
    dt32 = dt[:, :SSD_N_HEADS].reshape(t, SSD_N_GROUPS, 8)
    dtg = jnp.pad(dt32.transpose(1, 0, 2), ((0, 0), (0, 0), (0, 120)))
    dtg_t = dt32.transpose(1, 2, 0)
    alog = jnp.pad(ps['ssd_a_log'].reshape(SSD_N_GROUPS, 1, 8), ((0, 0), (0, 0), (0, 120)))
    alog_t = ps['ssd_a_log'].reshape(SSD_N_GROUPS, 8, 1)
    d_exp = jnp.repeat(ps['ssd_d'], SSD_HEAD_DIM, axis=1)
    y, hs = _ssd_fwd(xc, dtg, dtg_t, alog, alog_t, d_exp)
    gn = _gated_norm_fwd(y, proj, ps['ssd_norm_w'], tm=tm)
    qr, kp, vp = _rope_qkv(proj, cos_t, sin_t, tm=tm)
    sinks = jnp.pad(ps['attn_sinks'], ((0, 0), (0, 112)))
    ao, lse = _attn_fwd(qr, kp, vp, sinks)
    y_attn = _mm_plain("attn_out", ao, wb['ao'], tm=tmw, tn=512, tk=1024)

    def merge_ep(acc, i, j, ins, outs):
        gs, ga, ya = ins
        outs[0][...] = (_sigmoid(gs[...]) * acc + _sigmoid(ga[...]) * ya[...]).astype(BF16)
        outs[1][...] = acc

    merged, y_ssd = _mm_call(
        "ssd_out_merge", gn, wb['so'], tm=tmw, tn=512, tk=2048, epilogue=merge_ep,
        extra_in=[(proj, (tmw, 512), lambda i, j, k: (i, OFF_GS // 512 + j)),
                  (proj, (tmw, 512), lambda i, j, k: (i, OFF_GA // 512 + j)), (y_attn, (tmw, 512), ij)],
        outs=[((t, D_MODEL), BF16, (tmw, 512), ij), ((t, D_MODEL), F32, (tmw, 512), ij)])

    def mix_ep(acc, i, j, ins, outs):
        xv, wn = ins
        r = lax.rsqrt(jnp.mean(acc * acc, axis=-1, keepdims=True) + NORM_EPS)
        outs[0][...] = xv[...] + acc * r * wn[...]
        outs[1][...] = acc

    x1, mmix = _mm_call(
        "mix_out", merged, wb['mix'], tm=tm, tn=D_MODEL, tk=1024, epilogue=mix_ep,
        extra_in=[(x, (tm, D_MODEL), i0), (ps['norm_mix_post_w'], (1, D_MODEL), c0)],
        outs=[((t, D_MODEL), F32, (tm, D_MODEL), i0), ((t, D_MODEL), F32, (tm, D_MODEL), i0)])

    up_raw, h = _norm_mm("ffn_up", x1, ps['norm_ffn_pre_w'], wb['up'], tm=tmw, tn=1408)
    act = _ffn_act_fwd(up_raw, ps['ffn_conv_w'], ps['ffn_conv_b'], tm=min(256, t))

    def loss_ep(acc, i, j, ins, outs):
        x1v, tg, wn = ins
        d_ff_ref, dout_ref, loss_ref, dw_ref = outs
        wv = wn[...]
        r = lax.rsqrt(jnp.mean(acc * acc, axis=-1, keepdims=True) + NORM_EPS)
        err = x1v[...] + acc * r * wv - tg[...]
        dout = err * (1.0 / D_MODEL)
        dout_ref[...] = dout
        d_ff, dw = _rms_bwd(acc, wv, dout)
        d_ff_ref[...] = d_ff.astype(BF16)
        _accumulate(dw_ref, i == 0, dw)
        _accumulate(loss_ref, i == 0, jnp.sum(err * err, keepdims=True) * (0.5 / D_MODEL))

    d_ff, dout, loss, g_norm_ffn_post = _mm_call(
        "ffn_down_loss", act, wb['dn'], tm=tm, tn=D_MODEL, tk=FFN_D_FF, epilogue=loss_ep,
        extra_in=[(x1, (tm, D_MODEL), i0), (tgt, (tm, D_MODEL), i0), (ps['norm_ffn_post_w'], (1, D_MODEL), c0)],
        outs=[((t, D_MODEL), BF16, (tm, D_MODEL), i0), ((t, D_MODEL), F32, (tm, D_MODEL), i0),
              ((1, 1), F32, (1, 1), c0), ((1, D_MODEL), F32, (1, D_MODEL), c0)])

    d_act = _mm_plain("d_act", d_ff, wb['dn_t'], tm=tmw, tn=1408, tk=1024, out_dtype=BF16)
    g_w_down = _mm_plain("g_w_down", act, d_ff, tm=1408, tn=1024, tk=tkt, trans_a=True)
    d_gate, d_val, db_g, db_v = _ffn_act_bwd(up_raw, d_act, ps['ffn_conv_w'], ps['ffn_conv_b'], tm=min(256, t))
    d_up_raw, gcw_g = _conv_bwd2("ffn_conv_bwd2_gate", d_gate, up_raw, 0, ps['ffn_conv_w'][:, :FFN_D_FF], tm=min(256, t),
                                 tc=1408, out_cols=2 * FFN_D_FF, out_col0=0)
    d_up_raw, gcw_v = _conv_bwd2("ffn_conv_bwd2_val", d_val, up_raw, FFN_D_FF, ps['ffn_conv_w'][:, FFN_D_FF:], tm=min(256, t),
                                 tc=1408, out_cols=2 * FFN_D_FF, out_col0=FFN_D_FF, fill=d_up_raw)
    g_ffn_conv_w = jnp.concatenate([gcw_g, gcw_v], axis=1)

    def dx1_ep(acc, i, j, ins, outs):
        x1v, wpre, dout_v, mmv, wpost = ins
        d_x1_ref, d_mm_ref, dwpre_ref, dwpost_ref = outs
        d_n, dw_pre = _rms_bwd(x1v[...], wpre[...], acc)
        d_x1 = dout_v[...] + d_n
        d_x1_ref[...] = d_x1
        d_mm, dw_post = _rms_bwd(mmv[...], wpost[...], d_x1)
        d_mm_ref[...] = d_mm.astype(BF16)
        _accumulate(dwpre_ref, i == 0, dw_pre)
        _accumulate(dwpost_ref, i == 0, dw_post)

    d_x1, d_mm, g_norm_ffn_pre, g_norm_mix_post = _mm_call(
        "d_h", d_up_raw, wb['up_t'], tm=tm, tn=D_MODEL, tk=FFN_D_FF, epilogue=dx1_ep,
        extra_in=[(x1, (tm, D_MODEL), i0), (ps['norm_ffn_pre_w'], (1, D_MODEL), c0), (dout, (tm, D_MODEL), i0),
                  (mmix, (tm, D_MODEL), i0), (ps['norm_mix_post_w'], (1, D_MODEL), c0)],
        outs=[((t, D_MODEL), F32, (tm, D_MODEL), i0), ((t, D_MODEL), BF16, (tm, D_MODEL), i0),
              ((1, D_MODEL), F32, (1, D_MODEL), c0), ((1, D_MODEL), F32, (1, D_MODEL), c0)])
    g_w_up_t = _mm_plain("g_w_up", d_up_raw, h, tm=1408, tn=1024, tk=tkt, trans_a=True)

    def dmerge_ep(acc, i, j, ins, outs):
        gs, ga, ys, ya = ins
        sg_s, sg_a = _sigmoid(gs[...]), _sigmoid(ga[...])
        outs[0][...] = (acc * sg_s).astype(BF16)
        outs[1][...] = (acc * sg_a).astype(BF16)
        outs[2][:, 0:D_MODEL] = (acc * ys[...] * sg_s * (1.0 - sg_s)).astype(BF16)
        outs[2][:, D_MODEL:2 * D_MODEL] = (acc * ya[...] * sg_a * (1.0 - sg_a)).astype(BF16)

    d_yssd, d_yattn, d_proj = _mm_call(
        "d_merged", d_mm, wb['mix_t'], tm=tm, tn=D_MODEL, tk=1024, epilogue=dmerge_ep,
        extra_in=[(proj, (tm, D_MODEL), lambda i, j, k: (i, OFF_GS // D_MODEL)),
                  (proj, (tm, D_MODEL), lambda i, j, k: (i, OFF_GA // D_MODEL)), (y_ssd, (tm, D_MODEL), i0), (y_attn, (tm, D_MODEL), i0)],
        outs=[((t, D_MODEL), BF16, (tm, D_MODEL), i0), ((t, D_MODEL), BF16, (tm, D_MODEL), i0),
              ((t, PROJ_W), BF16, (tm, 2 * D_MODEL), lambda i, j, k: (i, OFF_GS // (2 * D_MODEL)))])
    g_w_mix = _mm_plain("g_w_mix", merged, d_mm, tm=1024, tn=1024, tk=tkt, trans_a=True)

    def dgn_ep(acc, i, j, ins, outs):
        yv, zv, wn = ins
        d_y_ref, d_z_ref, dw_ref = outs
        zz = zv[...]
        sz = _sigmoid(zz)
        silu = zz * sz
        gv = yv[...] * silu
        r = lax.rsqrt(jnp.mean(gv * gv, axis=-1, keepdims=True) + NORM_EPS)
        gh = gv * r
        dgh = acc * wn[...]
        dg = r * (dgh - gh * jnp.mean(dgh * gh, axis=-1, keepdims=True))
        d_y_ref[...] = dg * silu
        d_z_ref[...] = (dg * yv[...] * (sz * (1.0 + zz * (1.0 - sz)))).astype(BF16)
        dw = jnp.sum(acc * gh, axis=0, keepdims=True)

        @pl.when(i == 0)
        def _():
            dw_ref[j] = dw

        @pl.when(i > 0)
        def _():
            dw_ref[j] += dw

    d_y, d_proj, g_ssd_norm = _mm_call(
        "d_gn", d_yssd, wb['so_t'], tm=tm, tn=512, tk=1024, epilogue=dgn_ep, fill=(d_proj, 1),
        extra_in=[(y, (tm, 512), ij), (proj, (tm, 512), lambda i, j, k: (i, OFF_Z // 512 + j)), (ps['ssd_norm_w'], (1, 512), cj)],
        outs=[((t, SSD_D_INNER), F32, (tm, 512), ij), ((t, PROJ_W), BF16, (tm, 512), lambda i, j, k: (i, OFF_Z // 512 + j)),
              ((SSD_N_GROUPS, 1, 512), F32, (SSD_N_GROUPS, 1, 512), lambda i, j, k: (0, 0, 0))])
    g_ssd_norm = g_ssd_norm.reshape(1, SSD_D_INNER)
    g_w_so = _mm_plain("g_w_so", gn, d_yssd, tm=1024, tn=1024, tk=tkt, trans_a=True)
    d_xc, d_dtg, d_alog, d_dd = _ssd_bwd(xc, dtg, dtg_t, alog, alog_t, d_exp, d_y, hs)
    d_pre, g_ssd_conv_b = _conv_silu_bwd1(d_xc, proj, ps['ssd_conv_w'], ps['ssd_conv_b'], tm=tm)
    d_proj, g_ssd_conv_w = _conv_bwd2("ssd_conv_bwd2", d_pre, proj, OFF_XBC, ps['ssd_conv_w'], tm=tm, tc=512,
                                      out_cols=PROJ_W, out_col0=OFF_XBC, fill=d_proj)
    d_dt = jnp.pad(d_dtg[:, :, :8].transpose(1, 0, 2).reshape(t, SSD_N_HEADS), ((0, 0), (0, 96)))
    d_proj, g_dt_bias = _dt_bwd(d_dt, proj, bias_pad, d_proj, tm=tmw)

    d_ao = _mm_plain("d_ao", d_yattn, wb['ao_t'], tm=tmw, tn=512, tk=1024, out_dtype=BF16)
    g_w_ao = _mm_plain("g_w_ao", ao, d_yattn, tm=1024, tn=1024, tk=tkt, trans_a=True)
    d_proj, g_sinks = _attn_bwd(qr, kp, vp, d_ao, ao, lse, sinks, cos_t, sin_t, d_proj)

    def dx_ep(acc, i, j, ins, outs):
        xv, wn, dx1v = ins
        d_n, dw = _rms_bwd(xv[...], wn[...], acc)
        outs[0][...] = dx1v[...] + d_n
        _accumulate(outs[1], i == 0, dw)

    grad_x, g_norm_mix_pre = _mm_call(
        "d_u", d_proj, wb['cat_t'], tm=tm, tn=D_MODEL, tk=2304, epilogue=dx_ep,
        extra_in=[(x, (tm, D_MODEL), i0), (ps['norm_mix_pre_w'], (1, D_MODEL), c0), (d_x1, (tm, D_MODEL), i0)],
        outs=[((t, D_MODEL), F32, (tm, D_MODEL), i0), ((1, D_MODEL), F32, (1, D_MODEL), c0)])
    g_cat_t = _mm_plain("g_w_in", d_proj, u, tm=1024, tn=1024, tk=tkt, trans_a=True)

    grads = {
        'norm_mix_pre_w': g_norm_mix_pre, 'w_in': g_cat_t, 'ssd_conv_w': g_ssd_conv_w, 'ssd_conv_b': g_ssd_conv_b,
        'ssd_dt_bias': g_dt_bias[:, :SSD_N_HEADS], 'ssd_a_log': d_alog[:, 0, :8].reshape(1, SSD_N_HEADS),
        'ssd_d': d_dd[:, 0, :8].reshape(1, SSD_N_HEADS), 'ssd_norm_w': g_ssd_norm, 'ssd_w_out': g_w_so,
        'attn_sinks': g_sinks[:, :ATTN_N_HEADS], 'attn_w_out': g_w_ao, 'w_mix_out': g_w_mix,
        'norm_mix_post_w': g_norm_mix_post, 'norm_ffn_pre_w': g_norm_ffn_pre, 'ffn_w_up': g_w_up_t,
        'ffn_conv_w': g_ffn_conv_w, 'ffn_conv_b': jnp.concatenate([db_g, db_v], axis=1), 'ffn_w_down': g_w_down,
        'norm_ffn_post_w': g_norm_ffn_post,
    }
    return loss, grad_x, grads


def _group_channels(a):
    parts = []
    for g in range(SSD_N_GROUPS):
        parts += [a[..., 512 * g:512 * (g + 1)], a[..., 2048 + 128 * g:2048 + 128 * (g + 1)],
                  a[..., 2560 + 128 * g:2560 + 128 * (g + 1)]]
    return jnp.concatenate(parts, axis=-1)


def _ungroup_channels(a):
    xs = [a[..., GROUP_W * g:GROUP_W * g + 512] for g in range(SSD_N_GROUPS)]
    bs = [a[..., GROUP_W * g + 512:GROUP_W * g + 640] for g in range(SSD_N_GROUPS)]
    cs = [a[..., GROUP_W * g + 640:GROUP_W * (g + 1)] for g in range(SSD_N_GROUPS)]
    return jnp.concatenate(xs + bs + cs, axis=-1)


def _proj_rows(a_t, lo, hi):
    out = []
    for start, length, dst in sorted(PROJ_SEGS):
        s, e = max(lo, start), min(hi, start + length)
        if s < e:
            out.append(a_t[dst + s - start:dst + e - start])
    return out


def _to_proj_layout(w_in_t):
    pieces, pos = [], 0
    for start, length, dst in sorted(PROJ_SEGS, key=lambda s: s[2]):
        if dst > pos:
            pieces.append(jnp.zeros((dst - pos, w_in_t.shape[1]), w_in_t.dtype))
        pieces.append(w_in_t[start:start + length])
        pos = dst + length
    if pos < PROJ_W:
        pieces.append(jnp.zeros((PROJ_W - pos, w_in_t.shape[1]), w_in_t.dtype))
    return jnp.concatenate(pieces, axis=0)


def _rope_tables(positions):
    half = 32
    inv_freq = ROPE_THETA ** (-jnp.arange(half, dtype=F32) * 2.0 / 64)
    ang = positions.astype(F32)[:, None] * inv_freq
    cos, sin = jnp.cos(ang), jnp.sin(ang)
    return jnp.concatenate([cos, cos, cos, cos], axis=1), jnp.concatenate([-sin, sin, -sin, sin], axis=1)


def _matmul_weights(w_in_t, so, ao, mix, up_t, dn):
    cat_t = _to_proj_layout(w_in_t)
    return {'cat': cat_t.T, 'cat_t': cat_t, 'so': so, 'so_t': so.T, 'ao': ao, 'ao_t': ao.T, 'mix': mix, 'mix_t': mix.T,
            'up': up_t.T, 'up_t': up_t, 'dn': dn, 'dn_t': dn.T}


ANY = pl.BlockSpec(memory_space=pl.ANY)
MESH = pl.DeviceIdType.MESH
ROW_ALIGN = 256


def _mesh_pos():
    return lax.axis_index("x"), lax.axis_index("y"), lax.axis_index("c")


def _other_chips(x, y):
    return [(1 - x, y), (x, 1 - y), (1 - x, 1 - y)]


def _remote(src, dst, send_sems, recv_sems, k, to):
    return pltpu.make_async_remote_copy(src_ref=src, dst_ref=dst, send_sem=send_sems.at[k], recv_sem=recv_sems.at[k],
                                        device_id=to, device_id_type=MESH)


def _half(c, rh):
    return pl.ds(pl.multiple_of(c * rh, 128), rh)


def _all_gather(shard):
    r = shard.shape[0]
    rh = r // 2

    def body(w_ref, out_ref, send_sems, recv_sems):
        x, y, c = _mesh_pos()
        p = 2 * x + y
        sib = (x, y, 1 - c)
        mine, other = _half(c, rh), _half(1 - c, rh)
        chips = _other_chips(x, y)
        first = [_remote(w_ref, out_ref.at[p], send_sems, recv_sems, 6, sib)]
        first += [_remote(w_ref.at[mine], out_ref.at[p, mine], send_sems, recv_sems, j, (cx, cy, c))
                  for j, (cx, cy) in enumerate(chips)]
        for cp in first:
            cp.start()
        passed = []
        for j, (cx, cy) in enumerate(chips):
            slab = out_ref.at[2 * cx + cy, mine]
            _remote(slab, slab, send_sems, recv_sems, j, sib).wait_recv()
            fwd = _remote(slab, slab, send_sems, recv_sems, 3 + j, sib)
            fwd.start()
            passed.append(fwd)
        for j, (cx, cy) in enumerate(chips):
            slab = out_ref.at[2 * cx + cy, other]
            _remote(slab, slab, send_sems, recv_sems, 3 + j, sib).wait_recv()
        _remote(w_ref, out_ref.at[p], send_sems, recv_sems, 6, sib).wait_recv()
        for cp in first + passed:
            cp.wait_send()

    return pl.pallas_call(
        body, name="weights_all_gather", in_specs=[ANY], out_specs=ANY,
        out_shape=SDS((N_CHIPS, r, COMM_LANES), shard.dtype),
        scratch_shapes=[pltpu.SemaphoreType.DMA((7,)), pltpu.SemaphoreType.DMA((7,))],
    )(shard)


def _pair_swap(slabs):
    rh = slabs[0].shape[0] // 2

    def body(*refs):
        out_ref, send_sems, recv_sems = refs[N_CHIPS:]
        x, y, c = _mesh_pos()
        cps = [_remote(refs[s].at[_half(1 - c, rh)], out_ref.at[s], send_sems, recv_sems, s, (x, y, 1 - c))
               for s in range(N_CHIPS)]
        for cp in cps:
            cp.start()
        for cp in cps:
            cp.wait()

    return pl.pallas_call(
        body, name="grad_pair_swap", in_specs=[ANY] * N_CHIPS, out_specs=ANY,
        out_shape=SDS((N_CHIPS, rh, COMM_LANES), slabs[0].dtype),
        scratch_shapes=[pltpu.SemaphoreType.DMA((N_CHIPS,)), pltpu.SemaphoreType.DMA((N_CHIPS,))],
    )(*slabs)


def _chip_exchange(part):
    rh = part.shape[1]

    def body(p_ref, out_ref, send_sems, recv_sems):
        x, y, c = _mesh_pos()
        cps = [_remote(p_ref.at[2 * cx + cy], out_ref.at[j], send_sems, recv_sems, j, (cx, cy, c))
               for j, (cx, cy) in enumerate(_other_chips(x, y))]
        for cp in cps:
            cp.start()
        for cp in cps:
            cp.wait()

    return pl.pallas_call(
        body, name="grad_chip_exchange", in_specs=[ANY], out_specs=ANY,
        out_shape=SDS((3, rh, COMM_LANES), part.dtype),
        scratch_shapes=[pltpu.SemaphoreType.DMA((3,)), pltpu.SemaphoreType.DMA((3,))],
    )(part)


def _pair_gather(buf):
    def body(_, out_ref, send_sems, recv_sems):
        x, y, c = _mesh_pos()
        cp = _remote(out_ref.at[c], out_ref.at[c], send_sems, recv_sems, 0, (x, y, 1 - c))
        cp.start()
        _remote(out_ref.at[1 - c], out_ref.at[1 - c], send_sems, recv_sems, 0, (x, y, 1 - c)).wait_recv()
        cp.wait_send()

    return pl.pallas_call(
        body, name="grad_pair_gather", in_specs=[ANY], out_specs=ANY, out_shape=SDS(buf.shape, buf.dtype),
        scratch_shapes=[pltpu.SemaphoreType.DMA((1,)), pltpu.SemaphoreType.DMA((1,))],
        input_output_aliases={0: 0},
    )(buf)


def _pair_sum(slabs, got, c_idx, *, tr=384):
    rh = got.shape[1]
    nb = rh // tr

    def body(c_ref, *refs):
        b_ref, o_ref = refs[N_CHIPS:]
        s = pl.program_id(0)
        for q in range(N_CHIPS):
            @pl.when(s == q)
            def _():
                o_ref[0] = (refs[q][...] + b_ref[0]).astype(BF16)

    slab_spec = lambda q: pl.BlockSpec((tr, COMM_LANES), lambda s, i, c_ref: (c_ref[0] * nb + jnp.where(s == q, i, 0), 0))
    return pl.pallas_call(
        body, name="grad_pair_sum",
        grid_spec=pltpu.PrefetchScalarGridSpec(
            num_scalar_prefetch=1, grid=(N_CHIPS, nb),
            in_specs=[slab_spec(q) for q in range(N_CHIPS)] + [pl.BlockSpec((1, tr, COMM_LANES), lambda s, i, c_ref: (s, i, 0))],
            out_specs=pl.BlockSpec((1, tr, COMM_LANES), lambda s, i, c_ref: (s, i, 0))),
        out_shape=SDS(got.shape, BF16), compiler_params=_cp(dimension_semantics=("arbitrary", "arbitrary")),
    )(c_idx, *slabs, got)


def _chip_sum(part, got, pc_idx, *, tr=384):
    rh = part.shape[1]

    def body(p_ref, own_ref, r0_ref, r1_ref, r2_ref, o_ref):
        p = p_ref[0]
        own, r0, r1, r2 = (r[0].astype(F32) for r in (own_ref, r0_ref, r1_ref, r2_ref))

        def term(q):
            code = p ^ q
            return jnp.where(code == 0, own, jnp.where(code == 2, r0, jnp.where(code == 1, r1, r2)))

        o_ref[0] = ((term(0) + term(1)) + term(2)) + term(3)

    slab = lambda j: pl.BlockSpec((1, tr, COMM_LANES), lambda i, p_ref: (j, i, 0))
    return pl.pallas_call(
        body, name="grad_chip_sum",
        grid_spec=pltpu.PrefetchScalarGridSpec(
            num_scalar_prefetch=1, grid=(rh // tr,),
            in_specs=[pl.BlockSpec((1, tr, COMM_LANES), lambda i, p_ref: (p_ref[0], i, 0)), slab(0), slab(1), slab(2)],
            out_specs=pl.BlockSpec((1, tr, COMM_LANES), lambda i, p_ref: (p_ref[1], i, 0))),
        out_shape=SDS((2, rh, COMM_LANES), F32), compiler_params=_cp(dimension_semantics=("arbitrary",)),
    )(pc_idx, part, got, got, got)


def _pack_rows(big, small):
    flat = jnp.concatenate([p.reshape(-1) for p in small])
    k = -(-flat.shape[0] // (16 * COMM_LANES)) * 16
    tail = jnp.pad(flat, (0, k * COMM_LANES - flat.shape[0])).reshape(k, COMM_LANES)
    rows = sum(p.shape[0] for p in big) + k
    pad = -rows % ROW_ALIGN
    return jnp.concatenate(list(big) + [tail, jnp.zeros((pad, COMM_LANES), tail.dtype)], axis=0)


def _take(flat, off, shape):
    n = 1
    for d in shape:
        n *= d
    return flat[off:off + n].reshape(shape), off + n


BIG_ROWS = (('w_in', 2184), ('ssd_w_out', 512), ('attn_w_out', 256), ('w_mix_out', 256), ('ffn_w_up', 1408), ('ffn_w_down', 704))
TRANSPOSED = ('w_in', 'ffn_w_up')


def kernel(x, positions, norm_mix_pre_w, w_in, ssd_conv_w, ssd_conv_b, ssd_dt_bias, ssd_a_log, ssd_d, ssd_norm_w, ssd_w_out, attn_sinks, attn_w_out, w_mix_out, norm_mix_post_w, norm_ffn_pre_w, ffn_w_up, ffn_conv_w, ffn_conv_b, ffn_w_down, norm_ffn_post_w, loss_target, m_norm_mix_pre_w, m_w_in, m_ssd_conv_w, m_ssd_conv_b, m_ssd_dt_bias, m_ssd_a_log, m_ssd_d, m_ssd_norm_w, m_ssd_w_out, m_attn_sinks, m_attn_w_out, m_w_mix_out, m_norm_mix_post_w, m_norm_ffn_pre_w, m_ffn_w_up, m_ffn_conv_w, m_ffn_conv_b, m_ffn_w_down, m_norm_ffn_post_w, v_norm_mix_pre_w, v_w_in, v_ssd_conv_w, v_ssd_conv_b, v_ssd_dt_bias, v_ssd_a_log, v_ssd_d, v_ssd_norm_w, v_ssd_w_out, v_attn_sinks, v_attn_w_out, v_w_mix_out, v_norm_mix_post_w, v_norm_ffn_pre_w, v_ffn_w_up, v_ffn_conv_w, v_ffn_conv_b, v_ffn_w_down, v_norm_ffn_post_w):
    given = dict(locals())
    w = {n: given[n][0] for n in WEIGHTS}
    w = {n: (a if a.ndim == 2 else a[None]) for n, a in w.items()}
    mom_m = {n: given['m_' + n].reshape(w[n].shape) for n in WEIGHTS}
    mom_v = {n: given['v_' + n].reshape(w[n].shape) for n in WEIGHTS}
    cx, cy, cc = _mesh_pos()
    c_idx = cc.astype(jnp.int32).reshape(1)
    pc_idx = jnp.stack([2 * cx + cy, cc]).astype(jnp.int32)

    big = [(w[n].T if n in TRANSPOSED else w[n]).astype(BF16) for n, _ in BIG_ROWS]
    taps = [lax.bitcast_convert_type(w[n], BF16) for n in ('ssd_conv_w', 'ffn_conv_w')]
    gathered = _all_gather(_pack_rows(big, taps))
    rows = {n: [] for n, _ in BIG_ROWS}
    conv = {'ssd_conv_w': [], 'ffn_conv_w': []}
    for s in range(N_CHIPS):
        r0 = 0
        for n, nr in BIG_ROWS:
            rows[n].append(gathered[s, r0:r0 + nr])
            r0 += nr
        flat, off = gathered[s, r0:r0 + 16].reshape(-1), 0
        for n in conv:
            a, off = _take(flat, off, w[n].shape + (2,))
            conv[n].append(lax.bitcast_convert_type(a, F32))
    wfull = {n: jnp.concatenate(rows[n], axis=0) for n in rows}
    wb = _matmul_weights(wfull['w_in'], wfull['ssd_w_out'], wfull['attn_w_out'], wfull['w_mix_out'], wfull['ffn_w_up'],
                         wfull['ffn_w_down'])

    ps = {n: w[n] for n in REPLICATED}
    ps['ssd_conv_w'] = _group_channels(jnp.concatenate(conv['ssd_conv_w'], axis=1))
    ps['ssd_conv_b'] = _group_channels(w['ssd_conv_b'])
    ps['ffn_conv_w'] = jnp.concatenate(conv['ffn_conv_w'], axis=1)
    cos_t, sin_t = _rope_tables(positions[0])
    loss, grad_x, grads = _local_step(x[0], cos_t, sin_t, loss_target[0], wb, ps)
    grads['ssd_conv_w'] = _ungroup_channels(grads['ssd_conv_w'])
    grads['ssd_conv_b'] = _ungroup_channels(grads['ssd_conv_b'])

    shard_shapes = {n: sh for n, _, sh in SHARDED}
    slabs = []
    for s in range(N_CHIPS):
        bigs = []
        for n, nr in BIG_ROWS:
            bigs += _proj_rows(grads[n], nr * s, nr * (s + 1)) if n == 'w_in' else [grads[n][nr * s:nr * (s + 1)]]
        small = [grads[n][:, shard_shapes[n][1] * s:shard_shapes[n][1] * (s + 1)] for n in ('ssd_conv_w', 'ffn_conv_w')]
        slabs.append(_pack_rows(bigs, small + [grads[n] for n in REPLICATED]))
    pair = _pair_sum(slabs, _pair_swap(slabs), c_idx)
    mine = _chip_sum(pair, _chip_exchange(pair), pc_idx)
    red = _pair_gather(mine).reshape(slabs[0].shape[0], COMM_LANES)
    g_red, r0 = {}, 0
    for n, nr in BIG_ROWS:
        g_red[n] = red[r0:r0 + nr].T if n in TRANSPOSED else red[r0:r0 + nr]
        r0 += nr
    flat, off = red[r0:].reshape(-1), 0
    for n in ('ssd_conv_w', 'ffn_conv_w') + REPLICATED:
        g_red[n], off = _take(flat, off, w[n].shape)

    small_names = [n for n in WEIGHTS if n not in MATMUL_WEIGHTS]
    delta, new_m, new_v = {}, {}, {}
    for n in MATMUL_WEIGHTS:
        delta[n], new_m[n], new_v[n] = _adamw("adamw_" + n, w[n], g_red[n], mom_m[n], mom_v[n], tr=64)
    packed = [_pack_small([d[n] for n in small_names]) for d in (w, g_red, mom_m, mom_v)]
    outs = _adamw("adamw_small", *packed, tr=packed[0].shape[0])
    for res, o in zip((delta, new_m, new_v), outs):
        fl, off = o.reshape(-1), 0
        for n in small_names:
            res[n], off = _take(fl, off, w[n].shape)

    loss_all = lax.psum(loss[0, 0], ("x", "y", "c"))
    shaped = lambda d: [d[n].reshape(given[n].shape) for n in WEIGHTS]
    return (loss_all, grad_x[None], *shaped(g_red), *shaped(delta), *shaped(new_m), *shaped(new_v))


def _pack_small(pieces):
    flat = jnp.concatenate([p.reshape(-1) for p in pieces])
    rows = -(-flat.shape[0] // (128 * 8)) * 8
    return jnp.pad(flat, (0, rows * 128 - flat.shape[0])).reshape(rows, 128)
```

```python
from typing import Callable, NamedTuple

import jax
import jax.numpy as jnp
from jax import lax
from jax.experimental import pallas as pl
from jax.experimental.pallas import tpu as pltpu

F32 = jnp.float32
BF16 = jnp.bfloat16
SDS = jax.ShapeDtypeStruct
HIGHEST = lax.Precision.HIGHEST

D_MODEL = 1024
SSD_D_INNER = 2048
SSD_N_HEADS = 32
SSD_HEAD_DIM = 64
SSD_N_GROUPS = 4
SSD_HEADS_PER_GROUP = 8
SSD_D_STATE = 128
SSD_CONV_DIM = 3072
CHUNK = 128
ATTN_N_HEADS = 16
KV_WIDTH = 256
FFN_D_FF = 2816
IN_PROJ_DIM = 8736
ROPE_THETA = 10000.0
NORM_EPS = 1e-6
ADAM_LR, ADAM_B1, ADAM_B2, ADAM_EPS, ADAM_WD, ADAM_STEP = 0.001, 0.9, 0.999, 1e-08, 0.01, 10

PROJ_W = 9216
OFF_Q, OFF_K, OFF_V, OFF_Z, OFF_DT, OFF_GS, OFF_GA, OFF_XBC = 0, 1024, 1280, 1536, 3584, 4096, 5120, 6144
GROUP_W = 768
PROJ_SEGS = ([(0, 2048, OFF_Z)]
             + [(2048 + 512 * g, 512, OFF_XBC + GROUP_W * g) for g in range(4)]
             + [(4096 + 128 * g, 128, OFF_XBC + GROUP_W * g + 512) for g in range(4)]
             + [(4608 + 128 * g, 128, OFF_XBC + GROUP_W * g + 640) for g in range(4)]
             + [(5120, 32, OFF_DT), (5152, 1024, OFF_Q), (6176, 256, OFF_K), (6432, 256, OFF_V),
                (6688, 1024, OFF_GS), (7712, 1024, OFF_GA)])
VMEM_LIMIT_MB = 48
NEG = -1e30

WEIGHTS = ('norm_mix_pre_w', 'w_in', 'ssd_conv_w', 'ssd_conv_b', 'ssd_dt_bias', 'ssd_a_log', 'ssd_d', 'ssd_norm_w',
           'ssd_w_out', 'attn_sinks', 'attn_w_out', 'w_mix_out', 'norm_mix_post_w', 'norm_ffn_pre_w', 'ffn_w_up',
           'ffn_conv_w', 'ffn_conv_b', 'ffn_w_down', 'norm_ffn_post_w')
SHARDED = (('w_in', 1, (1024, 2184)), ('ssd_conv_w', 1, (4, 768)), ('ssd_w_out', 0, (512, 1024)),
           ('attn_w_out', 0, (256, 1024)), ('w_mix_out', 0, (256, 1024)), ('ffn_w_up', 1, (1024, 1408)),
           ('ffn_conv_w', 1, (3, 1408)), ('ffn_w_down', 0, (704, 1024)))
MATMUL_WEIGHTS = ('w_in', 'ssd_w_out', 'attn_w_out', 'w_mix_out', 'ffn_w_up', 'ffn_w_down')
REPLICATED = tuple(n for n in WEIGHTS if n not in {s[0] for s in SHARDED})
N_CHIPS = 4
COMM_LANES = 1024


def _cp(vmem_mb=VMEM_LIMIT_MB, **kw):
    return pltpu.CompilerParams(vmem_limit_bytes=vmem_mb << 20, **kw)


class _Ride(NamedTuple):
    ins: tuple
    outs: tuple
    n_sems: int
    start: Callable
    finish: Callable


def _ride_parts(ride):
    if ride is None:
        return [], [], [], [], []
    hbm = pl.BlockSpec(memory_space=pl.ANY)
    return (list(ride.ins), [hbm] * len(ride.ins), list(ride.outs), [hbm] * len(ride.outs),
            [pltpu.SemaphoreType.DMA((ride.n_sems,)), pltpu.SemaphoreType.DMA((ride.n_sems,))])


def _ride_run(ride, first, last, in_refs, out_refs, sems):
    if ride is None:
        return

    @pl.when(first)
    def _():
        ride.start(in_refs, out_refs, *sems)

    @pl.when(last)
    def _():
        ride.finish(in_refs, out_refs, *sems)


def _iota(shape, axis):
    return lax.broadcasted_iota(jnp.int32, shape, axis)


def _sigmoid(v):
    return 1.0 / (1.0 + jnp.exp(-v))


def _mm_call(name, a, b, *, tm, tn, tk, epilogue, outs, extra_in=(), trans_a=False, fill=None, ride=None):
    if trans_a:
        kdim, m = a.shape
    else:
        m, kdim = a.shape
    n = b.shape[1]
    assert b.shape[0] == kdim and m % tm == 0 and n % tn == 0 and kdim % tk == 0, (name, a.shape, b.shape, tm, tn, tk)
    gi, gj, gk = m // tm, n // tn, kdim // tk
    n_in, n_out = len(extra_in), len(outs)

    n_fill = 0 if fill is None else 1
    r_ops, r_in_specs, r_outs, r_out_specs, r_scratch = _ride_parts(ride)

    def body(a_ref, b_ref, *rest):
        ins = rest[:n_in]
        rest = rest[n_in + n_fill:]
        r_in, rest = rest[:len(r_ops)], rest[len(r_ops):]
        out_refs, rest = rest[:n_out], rest[n_out:]
        r_out, scratch = rest[:len(r_outs)], rest[len(r_outs):]
        i, j, k = pl.program_id(0), pl.program_id(1), pl.program_id(2)
        _ride_run(ride, (i == 0) & (j == 0) & (k == 0), (i == gi - 1) & (j == gj - 1) & (k == gk - 1),
                  r_in, r_out, scratch[-2:])
        av = a_ref[...].astype(BF16)
        bv = b_ref[...].astype(BF16)
        if trans_a:
            part = lax.dot_general(av, bv, (((0,), (0,)), ((), ())), preferred_element_type=F32)
        else:
            part = jnp.dot(av, bv, preferred_element_type=F32)
        if gk == 1:
            epilogue(part, i, j, ins, out_refs)
        else:
            acc = scratch[0]

            @pl.when(k == 0)
            def _():
                acc[...] = part

            @pl.when(k > 0)
            def _():
                acc[...] += part

            @pl.when(k == gk - 1)
            def _():
                epilogue(acc[...], i, j, ins, out_refs)

    a_spec = pl.BlockSpec((tk, tm), lambda i, j, k: (k, i)) if trans_a else pl.BlockSpec((tm, tk), lambda i, j, k: (i, k))
    in_specs = [a_spec, pl.BlockSpec((tk, tn), lambda i, j, k: (k, j))]
    in_specs += [pl.BlockSpec(bs, im) for _, bs, im in extra_in]
    operands = [a, b] + [e[0] for e in extra_in]
    aliases = {}
    if fill is not None:
        in_specs.append(pl.BlockSpec(memory_space=pl.ANY))
        aliases = {len(operands): fill[1]}
        operands.append(fill[0])
    return pl.pallas_call(
        body, name=name, grid=(gi, gj, gk), in_specs=in_specs + r_in_specs,
        out_specs=[pl.BlockSpec(bs, im) for _, _, bs, im in outs] + r_out_specs,
        out_shape=[SDS(s, d) for s, d, _, _ in outs] + r_outs,
        scratch_shapes=([pltpu.VMEM((tm, tn), F32)] if gk > 1 else []) + r_scratch,
        input_output_aliases=aliases,
        compiler_params=_cp(dimension_semantics=("arbitrary", "arbitrary", "arbitrary")),
    )(*operands, *r_ops)


def _mm_plain(name, a, b, *, tm, tn, tk, out_dtype=F32, trans_a=False):
    m = a.shape[1] if trans_a else a.shape[0]

    def epilogue(acc, i, j, ins, outs):
        outs[0][...] = acc.astype(out_dtype)

    return _mm_call(name, a, b, tm=tm, tn=tn, tk=tk, epilogue=epilogue, trans_a=trans_a,
                    outs=[((m, b.shape[1]), out_dtype, (tm, tn), lambda i, j, k: (i, j))])[0]


def _accumulate(ref, first, value):
    @pl.when(first)
    def _():
        ref[...] = value

    @pl.when(jnp.logical_not(first))
    def _():
        ref[...] += value


def _rms_bwd(xv, w, dy):
    r = lax.rsqrt(jnp.mean(xv * xv, axis=-1, keepdims=True) + NORM_EPS)
    xn = xv * r
    dxh = dy * w
    dx = r * (dxh - xn * jnp.mean(dxh * xn, axis=-1, keepdims=True))
    return dx, jnp.sum(dy * xn, axis=0, keepdims=True)


def _norm_mm(name, x, wn, w, *, tm, tn, ride=None):
    t, dm = x.shape
    n = w.shape[1]
    tm = min(tm, t)
    gi, gj = t // tm, n // tn
    r_ops, r_in_specs, r_outs, r_out_specs, r_scratch = _ride_parts(ride)

    def body(x_ref, wn_ref, w_ref, *rest):
        r_in, rest = rest[:len(r_ops)], rest[len(r_ops):]
        o_ref, u_ref = rest[:2]
        r_out, sems = rest[2:2 + len(r_outs)], rest[2 + len(r_outs):]
        i, j = pl.program_id(0), pl.program_id(1)
        _ride_run(ride, (i == 0) & (j == 0), (i == gi - 1) & (j == gj - 1), r_in, r_out, sems)

        @pl.when(j == 0)
        def _():
            xv = x_ref[...]
            r = lax.rsqrt(jnp.mean(xv * xv, axis=-1, keepdims=True) + NORM_EPS)
            u_ref[...] = (xv * r * wn_ref[...]).astype(BF16)

        o_ref[...] = jnp.dot(u_ref[...], w_ref[...], preferred_element_type=F32)

    return pl.pallas_call(
        body, name=name, grid=(gi, gj),
        in_specs=[pl.BlockSpec((tm, dm), lambda i, j: (i, 0)), pl.BlockSpec((1, dm), lambda i, j: (0, 0)),
                  pl.BlockSpec((dm, tn), lambda i, j: (0, j))] + r_in_specs,
        out_specs=[pl.BlockSpec((tm, tn), lambda i, j: (i, j)), pl.BlockSpec((tm, dm), lambda i, j: (i, 0))] + r_out_specs,
        out_shape=[SDS((t, n), F32), SDS((t, dm), BF16)] + r_outs, scratch_shapes=r_scratch,
        compiler_params=_cp(dimension_semantics=("arbitrary", "arbitrary")),
    )(x, wn, w, *r_ops)


def _shift_down(tile, halo, s):
    if s == 0:
        return tile
    r = pltpu.roll(tile, s, axis=0)
    h = pltpu.roll(halo, s, axis=0)
    head = jnp.where(_iota(h.shape, 0) < s, h, r[0:8])
    return jnp.concatenate([head, r[8:]], axis=0)


def _shift_up(tile, halo, s):
    if s == 0:
        return tile
    n = tile.shape[0]
    r = pltpu.roll(tile, n - s, axis=0)
    h = pltpu.roll(halo, 8 - s, axis=0)
    tail = jnp.where(_iota(h.shape, 0) >= 8 - s, h, r[n - 8:])
    return jnp.concatenate([r[:n - 8], tail], axis=0)


def _conv_apply(tile, halo, wv, bv, kw):
    acc = bv + wv[kw - 1:kw, :] * tile
    for k in range(kw - 1):
        acc = acc + wv[k:k + 1, :] * _shift_down(tile, halo, kw - 1 - k)
    return acc


def _prev_halo_spec(tm, tc, col0):
    return pl.BlockSpec((8, tc), lambda i, j: (jnp.maximum(i * (tm // 8) - 1, 0), col0 + j))


def _silu_parts(pre):
    sg = _sigmoid(pre)
    return pre * sg, sg * (1.0 + pre * (1.0 - sg))


def _conv_silu_fwd(proj, w, b, *, tm, tc=512):
    t = proj.shape[0]
    c = w.shape[1]
    tm = min(tm, t)
    col0 = OFF_XBC // tc

    def body(x_ref, h_ref, w_ref, b_ref, o_ref):
        halo = jnp.where(pl.program_id(0) > 0, h_ref[...], 0.0)
        o_ref[...] = _silu_parts(_conv_apply(x_ref[...], halo, w_ref[...], b_ref[...], 4))[0]

    return pl.pallas_call(
        body, name="ssd_conv_fwd", grid=(t // tm, c // tc),
        in_specs=[pl.BlockSpec((tm, tc), lambda i, j: (i, col0 + j)), _prev_halo_spec(tm, tc, col0),
                  pl.BlockSpec((4, tc), lambda i, j: (0, j)), pl.BlockSpec((1, tc), lambda i, j: (0, j))],
        out_specs=pl.BlockSpec((tm, tc), lambda i, j: (i, j)),
        out_shape=SDS((t, c), F32),
        compiler_params=_cp(dimension_semantics=("arbitrary", "arbitrary")),
    )(proj, proj, w, b)


def _conv_silu_bwd1(d_out, proj, w, b, *, tm, tc=512):
    t = proj.shape[0]
    c = w.shape[1]
    tm = min(tm, t)
    col0 = OFF_XBC // tc

    def body(g_ref, x_ref, h_ref, w_ref, b_ref, o_ref, db_ref):
        i = pl.program_id(1)
        halo = jnp.where(i > 0, h_ref[...], 0.0)
        d_pre = g_ref[...] * _silu_parts(_conv_apply(x_ref[...], halo, w_ref[...], b_ref[...], 4))[1]
        o_ref[...] = d_pre.astype(BF16)
        _accumulate(db_ref, i == 0, jnp.sum(d_pre, axis=0, keepdims=True))

    return pl.pallas_call(
        body, name="ssd_conv_bwd1", grid=(c // tc, t // tm),
        in_specs=[pl.BlockSpec((tm, tc), lambda j, i: (i, j)), pl.BlockSpec((tm, tc), lambda j, i: (i, col0 + j)),
                  pl.BlockSpec((8, tc), lambda j, i: (jnp.maximum(i * (tm // 8) - 1, 0), col0 + j)),
                  pl.BlockSpec((4, tc), lambda j, i: (0, j)), pl.BlockSpec((1, tc), lambda j, i: (0, j))],
        out_specs=[pl.BlockSpec((tm, tc), lambda j, i: (i, j)), pl.BlockSpec((1, tc), lambda j, i: (0, j))],
        out_shape=[SDS((t, c), BF16), SDS((1, c), F32)],
        compiler_params=_cp(dimension_semantics=("arbitrary", "arbitrary")),
    )(d_out, proj, proj, w, b)


def _conv_bwd2(name, d_pre, src, src_col0, w, *, tm, tc, out_cols, out_col0, fill=None):
    t, c = d_pre.shape
    kw = w.shape[0]
    tm = min(tm, t)
    ni = t // tm
    col0 = src_col0 // tc
    ocol0 = out_col0 // tc

    def body(g_ref, gn_ref, x_ref, xh_ref, w_ref, *rest):
        o_ref, dw_ref = rest[-2:]
        i = pl.program_id(1)
        g = g_ref[...].astype(F32)
        g_next = jnp.where(i < ni - 1, gn_ref[...].astype(F32)[0:8], 0.0)
        x_prev = jnp.where(i > 0, xh_ref[...], 0.0)
        xv = x_ref[...]
        wv = w_ref[...]
        d_in = wv[kw - 1:kw, :] * g
        for k in range(kw - 1):
            d_in = d_in + wv[k:k + 1, :] * _shift_up(g, g_next, kw - 1 - k)
        o_ref[...] = d_in.astype(o_ref.dtype)
        rows = [jnp.sum(g * _shift_down(xv, x_prev, kw - 1 - k), axis=0, keepdims=True) for k in range(kw)]

        @pl.when(i == 0)
        def _():
            for k in range(kw):
                dw_ref[k:k + 1, :] = rows[k]

        @pl.when(i > 0)
        def _():
            for k in range(kw):
                dw_ref[k:k + 1, :] += rows[k]

    in_specs = [pl.BlockSpec((tm, tc), lambda j, i: (i, j)),
                pl.BlockSpec((16, tc), lambda j, i: (jnp.minimum((i + 1) * (tm // 16), t // 16 - 1), j)),
                pl.BlockSpec((tm, tc), lambda j, i: (i, col0 + j)),
                pl.BlockSpec((8, tc), lambda j, i: (jnp.maximum(i * (tm // 8) - 1, 0), col0 + j)),
                pl.BlockSpec((kw, tc), lambda j, i: (0, j))]
    operands = [d_pre, d_pre, src, src, w]
    if fill is not None:
        in_specs.append(pl.BlockSpec(memory_space=pl.ANY))
        operands.append(fill)
    return pl.pallas_call(
        body, name=name, grid=(c // tc, ni), in_specs=in_specs,
        out_specs=[pl.BlockSpec((tm, tc), lambda j, i: (i, ocol0 + j)), pl.BlockSpec((kw, tc), lambda j, i: (0, j))],
        out_shape=[SDS((t, out_cols), BF16), SDS((kw, c), F32)],
        input_output_aliases={} if fill is None else {5: 0},
        compiler_params=_cp(dimension_semantics=("arbitrary", "arbitrary")),
    )(*operands)


GELU_C = 0.7978845608028654


def _gelu_parts(v):
    inner = GELU_C * (v + 0.044715 * v * v * v)
    th = jnp.tanh(inner)
    val = 0.5 * v * (1.0 + th)
    grad = 0.5 * (1.0 + th) + 0.5 * v * (1.0 - th * th) * GELU_C * (1.0 + 3.0 * 0.044715 * v * v)
    return val, grad


def _ffn_act_specs(tm, tc, nj, order):
    def im(f):
        return (lambda i, j: f(i, j)) if order == "ij" else (lambda j, i: f(i, j))
    halo = lambda i: jnp.maximum(i * (tm // 8) - 1, 0)
    return [pl.BlockSpec((tm, tc), im(lambda i, j: (i, j))), pl.BlockSpec((8, tc), im(lambda i, j: (halo(i), j))),
            pl.BlockSpec((tm, tc), im(lambda i, j: (i, nj + j))), pl.BlockSpec((8, tc), im(lambda i, j: (halo(i), nj + j))),
            pl.BlockSpec((3, tc), im(lambda i, j: (0, j))), pl.BlockSpec((3, tc), im(lambda i, j: (0, nj + j))),
            pl.BlockSpec((1, tc), im(lambda i, j: (0, j))), pl.BlockSpec((1, tc), im(lambda i, j: (0, nj + j)))]


def _ffn_act_fwd(up_raw, w, b, *, tm, tc=1408):
    t = up_raw.shape[0]
    tm = min(tm, t)
    nj = FFN_D_FF // tc

    def body(g_ref, gh_ref, v_ref, vh_ref, wg_ref, wv_ref, bg_ref, bv_ref, o_ref):
        first = pl.program_id(0) > 0
        gate = _conv_apply(g_ref[...], jnp.where(first, gh_ref[...], 0.0), wg_ref[...], bg_ref[...], 3)
        val = _conv_apply(v_ref[...], jnp.where(first, vh_ref[...], 0.0), wv_ref[...], bv_ref[...], 3)
        o_ref[...] = (_gelu_parts(gate)[0] * val).astype(BF16)

    return pl.pallas_call(
        body, name="ffn_act_fwd", grid=(t // tm, nj), in_specs=_ffn_act_specs(tm, tc, nj, "ij"),
        out_specs=pl.BlockSpec((tm, tc), lambda i, j: (i, j)), out_shape=SDS((t, FFN_D_FF), BF16),
        compiler_params=_cp(dimension_semantics=("arbitrary", "arbitrary")),
    )(up_raw, up_raw, up_raw, up_raw, w, w, b, b)


def _ffn_act_bwd(up_raw, d_act, w, b, *, tm, tc=1408):
    t = up_raw.shape[0]
    tm = min(tm, t)
    nj = FFN_D_FF // tc

    def body(g_ref, gh_ref, v_ref, vh_ref, wg_ref, wv_ref, bg_ref, bv_ref, da_ref, dg_ref, dv_ref, dbg_ref, dbv_ref):
        i = pl.program_id(1)
        gate = _conv_apply(g_ref[...], jnp.where(i > 0, gh_ref[...], 0.0), wg_ref[...], bg_ref[...], 3)
        val = _conv_apply(v_ref[...], jnp.where(i > 0, vh_ref[...], 0.0), wv_ref[...], bv_ref[...], 3)
        ge, dge = _gelu_parts(gate)
        da = da_ref[...].astype(F32)
        d_gate = da * val * dge
        d_val = da * ge
        dg_ref[...] = d_gate.astype(BF16)
        dv_ref[...] = d_val.astype(BF16)
        _accumulate(dbg_ref, i == 0, jnp.sum(d_gate, axis=0, keepdims=True))
        _accumulate(dbv_ref, i == 0, jnp.sum(d_val, axis=0, keepdims=True))

    tile = pl.BlockSpec((tm, tc), lambda j, i: (i, j))
    row = pl.BlockSpec((1, tc), lambda j, i: (0, j))
    return pl.pallas_call(
        body, name="ffn_act_bwd", grid=(nj, t // tm), in_specs=_ffn_act_specs(tm, tc, nj, "ji") + [tile],
        out_specs=[tile, tile, row, row],
        out_shape=[SDS((t, FFN_D_FF), BF16), SDS((t, FFN_D_FF), BF16), SDS((1, FFN_D_FF), F32), SDS((1, FFN_D_FF), F32)],
        compiler_params=_cp(dimension_semantics=("arbitrary", "arbitrary")),
    )(up_raw, up_raw, up_raw, up_raw, w, w, b, b, d_act)


def _softplus(v):
    e = jnp.exp(-jnp.abs(v))
    small = e * (1.0 - 0.5 * e)
    return jnp.maximum(v, 0.0) + jnp.where(e < 1e-4, small, jnp.log(1.0 + e))


def _dt_fwd(proj, bias_pad, *, tm):
    t = proj.shape[0]
    tm = min(tm, t)

    def body(x_ref, b_ref, o_ref):
        o_ref[...] = _softplus(x_ref[...] + b_ref[...])

    return pl.pallas_call(
        body, name="dt_fwd", grid=(t // tm,),
        in_specs=[pl.BlockSpec((tm, 128), lambda i: (i, OFF_DT // 128)), pl.BlockSpec((1, 128), lambda i: (0, 0))],
        out_specs=pl.BlockSpec((tm, 128), lambda i: (i, 0)), out_shape=SDS((t, 128), F32),
        compiler_params=_cp(dimension_semantics=("arbitrary",)),
    )(proj, bias_pad)


def _dt_bwd(d_dt, proj, bias_pad, d_proj, *, tm):
    t = proj.shape[0]
    tm = min(tm, t)

    def body(g_ref, x_ref, b_ref, _, o_ref, db_ref):
        d_raw = g_ref[...] * _sigmoid(x_ref[...] + b_ref[...])
        o_ref[:, 0:128] = d_raw.astype(BF16)
        o_ref[:, 128:512] = jnp.zeros((tm, 384), BF16)
        _accumulate(db_ref, pl.program_id(0) == 0, jnp.sum(d_raw, axis=0, keepdims=True))

    return pl.pallas_call(
        body, name="dt_bwd", grid=(t // tm,),
        in_specs=[pl.BlockSpec((tm, 128), lambda i: (i, 0)), pl.BlockSpec((tm, 128), lambda i: (i, OFF_DT // 128)),
                  pl.BlockSpec((1, 128), lambda i: (0, 0)), pl.BlockSpec(memory_space=pl.ANY)],
        out_specs=[pl.BlockSpec((tm, 512), lambda i: (i, OFF_DT // 512)), pl.BlockSpec((1, 128), lambda i: (0, 0))],
        out_shape=[SDS((t, PROJ_W), BF16), SDS((1, 128), F32)],
        input_output_aliases={3: 0},
        compiler_params=_cp(dimension_semantics=("arbitrary",)),
    )(d_dt, proj, bias_pad, d_proj)


def _split3(v):
    hi = v.astype(BF16)
    r1 = v - hi.astype(F32)
    mid = r1.astype(BF16)
    return hi, mid, (r1 - mid.astype(F32)).astype(BF16)


def _times01(v, m3):
    return jnp.dot(jnp.concatenate(_split3(v), axis=1), m3, preferred_element_type=F32)


def _01times(m3, v):
    return jnp.dot(m3, jnp.concatenate(_split3(v), axis=0), preferred_element_type=F32)


def _ssd_decay(dt_ref, dtT_ref, al_ref, alT_ref, k):
    dt = dt_ref[0]
    a_row = -jnp.exp(al_ref[0])
    adt_t = dtT_ref[0] * (-jnp.exp(alT_ref[0]))
    return dt, a_row, _01times(k['low3'][...], dt * a_row), _times01(adt_t, k['up3v'][...])


def _ssd_specs(nc, rev):
    ci = (lambda c: nc - 1 - c) if rev else (lambda c: c)
    return [pl.BlockSpec((CHUNK, GROUP_W), lambda c, g: (ci(c), g)),
            pl.BlockSpec((1, CHUNK, 128), lambda c, g: (g, ci(c), 0)),
            pl.BlockSpec((1, 8, CHUNK), lambda c, g: (g, 0, ci(c))),
            pl.BlockSpec((1, 1, 128), lambda c, g: (g, 0, 0)),
            pl.BlockSpec((1, 8, 1), lambda c, g: (g, 0, 0)),
            pl.BlockSpec((1, 512), lambda c, g: (0, g))]


NT = (((1,), (1,)), ((), ()))
WIDE = 8 * CHUNK
SSD_CONST_NAMES = ('e128', 'e64', 's64', 'mlo', 'mup', 'low3', 'up3', 'up3v')
SSD_CONST_SHAPES = [pltpu.VMEM((3 * CHUNK, WIDE), BF16), pltpu.VMEM((3 * CHUNK, 512), BF16), pltpu.VMEM((512, CHUNK), BF16),
                    pltpu.VMEM((CHUNK, WIDE), F32), pltpu.VMEM((CHUNK, WIDE), F32), pltpu.VMEM((CHUNK, 3 * CHUNK), BF16),
                    pltpu.VMEM((CHUNK, 3 * CHUNK), BF16), pltpu.VMEM((3 * CHUNK, CHUNK), BF16)]


def _ssd_init_consts(k):
    row, col = _iota((3 * CHUNK, WIDE), 0), _iota((3 * CHUNK, WIDE), 1)
    k['e128'][...] = ((col >> 7) == (row & 127)).astype(BF16)
    k['e64'][...] = ((_iota((3 * CHUNK, 512), 1) >> 6) == (_iota((3 * CHUNK, 512), 0) & 127)).astype(BF16)
    k['s64'][...] = ((_iota((512, CHUNK), 0) >> 6) == _iota((512, CHUNK), 1)).astype(BF16)
    row, col = _iota((CHUNK, WIDE), 0), _iota((CHUNK, WIDE), 1)
    k['mlo'][...] = (row >= (col & 127)).astype(F32)
    k['mup'][...] = (row <= (col & 127)).astype(F32)
    row, col = _iota((CHUNK, 3 * CHUNK), 0), _iota((CHUNK, 3 * CHUNK), 1) & 127
    k['low3'][...] = (row >= col).astype(BF16)
    k['up3'][...] = (row <= col).astype(BF16)
    row, col = _iota((3 * CHUNK, CHUNK), 0) & 127, _iota((3 * CHUNK, CHUNK), 1)
    k['up3v'][...] = (row <= col).astype(BF16)


def _ssd_common(x_ref, dt_ref, dtT_ref, al_ref, alT_ref, k):
    dt, a_row, acs, acs_t = _ssd_decay(dt_ref, dtT_ref, al_ref, alT_ref, k)
    ecol = _times01(acs, k['e128'][...])
    rrow = jnp.concatenate([jnp.broadcast_to(acs_t[j:j + 1, :], (CHUNK, CHUNK)) for j in range(8)], axis=1)
    a64 = _times01(acs, k['e64'][...])
    dt64 = _times01(dt, k['e64'][...])
    a_end64 = a64[CHUNK - 1:CHUNK, :]
    xs = x_ref[:, 0:512]
    return dict(dt=dt, a_row=a_row, acs=acs, seg=ecol - rrow, dt64=dt64, e_a=jnp.exp(a64), decay=jnp.exp(a_end64 - a64),
                e_end64=jnp.exp(a_end64), xs=xs, xdt=xs * dt64, bm=x_ref[:, 512:640], cm=x_ref[:, 640:768])


def _pair_blocks(v):
    lo = _iota((CHUNK, 128), 1) < 64
    out = []
    for i in range(4):
        ch = v[:, i * 128:(i + 1) * 128]
        out.append(jnp.concatenate([jnp.where(lo, ch, 0.0), jnp.where(lo, 0.0, ch)], axis=0).astype(BF16))
    return out


def _tile8(m):
    return jnp.concatenate([m] * 8, axis=1)


def _ssd_fwd(xc, dtg, dtg_t, alog, alog_t, d_exp):
    t = xc.shape[0]
    nc = t // CHUNK

    def body(x_ref, dt_ref, dtT_ref, al_ref, alT_ref, d_ref, y_ref, hs_ref, h_scr, *consts):
        c, g = pl.program_id(0), pl.program_id(1)
        k = dict(zip(SSD_CONST_NAMES, consts))

        @pl.when(jnp.logical_and(c == 0, g == 0))
        def _():
            _ssd_init_consts(k)

        @pl.when(c == 0)
        def _():
            h_scr[g] = jnp.zeros((SSD_D_STATE, 512), F32)

        v = _ssd_common(x_ref, dt_ref, dtT_ref, al_ref, alT_ref, k)
        b16, c16 = v['bm'].astype(BF16), v['cm'].astype(BF16)
        cb = lax.dot_general(c16, b16, NT, preferred_element_type=F32)
        m16 = (jnp.exp(jnp.minimum(v['seg'], 0.0)) * k['mlo'][...] * _tile8(cb)).astype(BF16)
        xbd = _pair_blocks(v['xdt'])
        y_diag = jnp.concatenate([jnp.dot(m16[:, i * 256:(i + 1) * 256], xbd[i], preferred_element_type=F32)
                                  for i in range(4)], axis=1)
        ht = h_scr[g]
        y_off = jnp.dot(c16, ht.astype(BF16), preferred_element_type=F32)
        y_ref[...] = y_diag + v['e_a'] * y_off + d_ref[...] * v['xs']
        st = jnp.dot(v['bm'].T.astype(BF16), (v['xdt'] * v['decay']).astype(BF16), preferred_element_type=F32)
        hs_ref[0, 0] = ht
        h_scr[g] = ht * v['e_end64'] + st

    return pl.pallas_call(
        body, name="ssd_fwd", grid=(nc, SSD_N_GROUPS), in_specs=_ssd_specs(nc, False),
        out_specs=[pl.BlockSpec((CHUNK, 512), lambda c, g: (c, g)),
                   pl.BlockSpec((1, 1, SSD_D_STATE, 512), lambda c, g: (c, g, 0, 0))],
        out_shape=[SDS((t, SSD_D_INNER), F32), SDS((nc, SSD_N_GROUPS, SSD_D_STATE, 512), F32)],
        scratch_shapes=[pltpu.VMEM((SSD_N_GROUPS, SSD_D_STATE, 512), F32)] + SSD_CONST_SHAPES,
        compiler_params=_cp(dimension_semantics=("arbitrary", "arbitrary")),
    )(xc, dtg, dtg_t, alog, alog_t, d_exp)


def _ssd_bwd(xc, dtg, dtg_t, alog, alog_t, d_exp, d_y, hs, ride=None):
    t = xc.shape[0]
    nc = t // CHUNK

    r_ops, r_in_specs, r_outs, r_out_specs, r_scratch = _ride_parts(ride)

    def body(x_ref, dt_ref, dtT_ref, al_ref, alT_ref, d_ref, dy_ref, hs_ref, *rest):
        r_in, rest = rest[:len(r_ops)], rest[len(r_ops):]
        dx_ref, ddt_ref, dal_ref, dd_ref = rest[:4]
        r_out, rest = rest[4:4 + len(r_outs)], rest[4 + len(r_outs):]
        g_scr, consts, sems = rest[0], rest[1:1 + len(SSD_CONST_NAMES)], rest[1 + len(SSD_CONST_NAMES):]
        c, g = pl.program_id(0), pl.program_id(1)
        _ride_run(ride, (c == 0) & (g == 0), (c == nc - 1) & (g == SSD_N_GROUPS - 1), r_in, r_out, sems)
        k = dict(zip(SSD_CONST_NAMES, consts))
        s64, mlo, mup = k['s64'], k['mlo'], k['mup']

        @pl.when(jnp.logical_and(c == 0, g == 0))
        def _():
            _ssd_init_consts(k)

        @pl.when(c == 0)
        def _():
            g_scr[g] = jnp.zeros((SSD_D_STATE, 512), F32)

        v = _ssd_common(x_ref, dt_ref, dtT_ref, al_ref, alT_ref, k)
        dt, a_row, xs, xdt, e_a, decay = v['dt'], v['a_row'], v['xs'], v['xdt'], v['e_a'], v['decay']
        row, col = _iota((CHUNK, CHUNK), 0), _iota((CHUNK, CHUNK), 1)
        b16, c16 = v['bm'].astype(BF16), v['cm'].astype(BF16)
        ct16 = v['cm'].T.astype(BF16)
        cb = lax.dot_general(c16, b16, NT, preferred_element_type=F32)
        cbt = lax.dot_general(b16, c16, NT, preferred_element_type=F32)
        lmat = jnp.exp(jnp.minimum(v['seg'], 0.0)) * mlo[...]
        lmat_t = jnp.exp(jnp.minimum(-v['seg'], 0.0)) * mup[...]
        mmat, mmat_t = lmat * _tile8(cb), lmat_t * _tile8(cbt)
        mt16 = mmat_t.astype(BF16)
        dy = dy_ref[...]
        dye, xdec = dy * e_a, xdt * decay
        dy16, dye16, xdec16 = dy.astype(BF16), dye.astype(BF16), xdec.astype(BF16)
        xdt16 = xdt.astype(BF16)
        ht, gt = hs_ref[0, 0], g_scr[g]
        ht16, gt16 = ht.astype(BF16), gt.astype(BF16)
        xbd, dybd = _pair_blocks(xdt), _pair_blocks(dy)
        d_m, d_mt, d_x = [], [], []
        for i in range(4):
            csl = slice(i * 128, (i + 1) * 128)
            d_m.append(lax.dot_general(dy16[:, csl], xbd[i], NT, preferred_element_type=F32))
            d_mt.append(lax.dot_general(xdt16[:, csl], dybd[i], NT, preferred_element_type=F32))
            d_x.append(jnp.dot(mt16[:, i * 256:(i + 1) * 256], dybd[i], preferred_element_type=F32))
        d_m, d_mt, d_x = jnp.concatenate(d_m, axis=1), jnp.concatenate(d_mt, axis=1), jnp.concatenate(d_x, axis=1)

        def head_sum(m):
            acc = m[:, 0:CHUNK]
            for j in range(1, 8):
                acc = acc + m[:, j * CHUNK:(j + 1) * CHUNK]
            return acc

        def seg64(p):
            return jnp.dot(p.astype(BF16), s64[...], preferred_element_type=F32)

        d_cb16 = head_sum(d_m * lmat).astype(BF16)
        d_cbt16 = head_sum(d_mt * lmat_t).astype(BF16)
        dseg = d_m * mmat - d_mt * mmat_t
        da_seg = jnp.zeros((CHUNK, CHUNK), F32)
        for j in range(8):
            da_seg = jnp.where(col == j, jnp.sum(dseg[:, j * CHUNK:(j + 1) * CHUNK], axis=1, keepdims=True), da_seg)
        ch = jnp.dot(c16, ht16, preferred_element_type=F32)
        bg = jnp.dot(b16, gt16, preferred_element_type=F32)
        d_x = d_x + decay * bg
        d_decay = seg64(xdec * bg)
        e_end = jnp.exp(v['acs'][CHUNK - 1:CHUNK, :])
        d_end = e_end * jnp.sum(seg64(gt * ht), axis=0, keepdims=True) + jnp.sum(d_decay, axis=0, keepdims=True)
        d_a = seg64(dye * ch) - d_decay + da_seg + jnp.where(row == CHUNK - 1, d_end, 0.0)
        dx_ref[:, 0:512] = d_x * v['dt64'] + d_ref[...] * dy
        dx_ref[:, 640:768] = (lax.dot_general(dye16, ht16, NT, preferred_element_type=F32)
                              + jnp.dot(d_cb16, b16, preferred_element_type=F32))
        dx_ref[:, 512:640] = (lax.dot_general(xdec16, gt16, NT, preferred_element_type=F32)
                              + jnp.dot(d_cbt16, c16, preferred_element_type=F32))
        g_scr[g] = gt * v['e_end64'] + jnp.dot(ct16, dye16, preferred_element_type=F32)
        d_adt = _01times(k['up3'][...], d_a)
        ddt_ref[0] = d_adt * a_row + seg64(d_x * xs)
        d_alog = jnp.sum(d_adt * dt, axis=0, keepdims=True) * a_row
        dd_row = jnp.sum(seg64(dy * xs), axis=0, keepdims=True)
        first = c == 0

        @pl.when(first)
        def _():
            dal_ref[g] = d_alog
            dd_ref[g] = dd_row

        @pl.when(jnp.logical_not(first))
        def _():
            dal_ref[g] += d_alog
            dd_ref[g] += dd_row

    rc = lambda c: nc - 1 - c
    whole = pl.BlockSpec((SSD_N_GROUPS, 1, 128), lambda c, g: (0, 0, 0))
    return pl.pallas_call(
        body, name="ssd_bwd", grid=(nc, SSD_N_GROUPS),
        in_specs=_ssd_specs(nc, True) + [pl.BlockSpec((CHUNK, 512), lambda c, g: (rc(c), g)),
                                        pl.BlockSpec((1, 1, SSD_D_STATE, 512), lambda c, g: (rc(c), g, 0, 0))] + r_in_specs,
        out_specs=[pl.BlockSpec((CHUNK, GROUP_W), lambda c, g: (rc(c), g)),
                   pl.BlockSpec((1, CHUNK, 128), lambda c, g: (g, rc(c), 0)), whole, whole] + r_out_specs,
        out_shape=[SDS((t, SSD_CONV_DIM), F32), SDS((SSD_N_GROUPS, t, 128), F32),
                   SDS((SSD_N_GROUPS, 1, 128), F32), SDS((SSD_N_GROUPS, 1, 128), F32)] + r_outs,
        scratch_shapes=[pltpu.VMEM((SSD_N_GROUPS, SSD_D_STATE, 512), F32)] + SSD_CONST_SHAPES + r_scratch,
        compiler_params=_cp(dimension_semantics=("arbitrary", "arbitrary")),
    )(xc, dtg, dtg_t, alog, alog_t, d_exp, d_y, hs, *r_ops)


def _gated_norm_fwd(y, proj, w, *, tm):
    t = y.shape[0]
    tm = min(tm, t)

    def body(y_ref, z_ref, w_ref, o_ref):
        gv = y_ref[...] * _silu_parts(z_ref[...])[0]
        r = lax.rsqrt(jnp.mean(gv * gv, axis=-1, keepdims=True) + NORM_EPS)
        o_ref[...] = (gv * r * w_ref[...]).astype(BF16)

    tile = pl.BlockSpec((tm, 512), lambda i, g: (i, g))
    return pl.pallas_call(
        body, name="gated_norm_fwd", grid=(t // tm, SSD_N_GROUPS),
        in_specs=[tile, pl.BlockSpec((tm, 512), lambda i, g: (i, OFF_Z // 512 + g)),
                  pl.BlockSpec((1, 512), lambda i, g: (0, g))], out_specs=tile,
        out_shape=SDS((t, SSD_D_INNER), BF16),
        compiler_params=_cp(dimension_semantics=("arbitrary", "arbitrary")),
    )(y, proj, w)


def _rope(ch, cos_t, sin_t):
    first = (_iota(ch.shape, 1) & 32) == 0
    partner = jnp.where(first, pltpu.roll(ch, 96, axis=1), pltpu.roll(ch, 32, axis=1))
    return ch * cos_t + partner * sin_t


def _rope_qkv(proj, cos_t, sin_t, *, tm):
    t = proj.shape[0]
    tm = min(tm, t)

    def body(q_ref, k_ref, v_ref, c_ref, s_ref, qr_ref, kp_ref, vp_ref):
        cv, sv = c_ref[...], s_ref[...]
        lo = _iota((tm, 128), 1) < 64
        for m in range(8):
            sl = slice(m * 128, (m + 1) * 128)
            qr_ref[:, sl] = (_rope(q_ref[:, sl], cv, sv) * 0.125).astype(BF16)
        for m2 in range(2):
            sl = slice(m2 * 128, (m2 + 1) * 128)
            for src, dst in ((_rope(k_ref[:, sl], cv, sv), kp_ref), (v_ref[:, sl], vp_ref)):
                sw = pltpu.roll(src, 64, axis=1)
                base = 4 * m2 * 128
                dst[:, base:base + 128] = jnp.where(lo, src, 0.0).astype(BF16)
                dst[:, base + 128:base + 256] = jnp.where(lo, 0.0, sw).astype(BF16)
                dst[:, base + 256:base + 384] = jnp.where(lo, sw, 0.0).astype(BF16)
                dst[:, base + 384:base + 512] = jnp.where(lo, 0.0, src).astype(BF16)

    return pl.pallas_call(
        body, name="rope_qkv", grid=(t // tm,),
        in_specs=[pl.BlockSpec((tm, 1024), lambda i: (i, OFF_Q // 1024)), pl.BlockSpec((tm, 256), lambda i: (i, OFF_K // 256)),
                  pl.BlockSpec((tm, 256), lambda i: (i, OFF_V // 256)), pl.BlockSpec((tm, 128), lambda i: (i, 0)),
                  pl.BlockSpec((tm, 128), lambda i: (i, 0))],
        out_specs=[pl.BlockSpec((tm, 1024), lambda i: (i, 0))] * 3,
        out_shape=[SDS((t, 1024), BF16)] * 3,
        compiler_params=_cp(dimension_semantics=("arbitrary",)),
    )(proj, proj, proj, cos_t, sin_t)


def _attn_valid(n):
    qi, kj = _iota((CHUNK, 2 * CHUNK), 0), _iota((CHUNK, 2 * CHUNK), 1)
    return (kj > qi) & (kj <= qi + CHUNK) & ((n > 0) | (kj >= CHUNK))


def _attn_fwd(qr, kp, vp, sinks):
    t = qr.shape[0]
    nb = t // CHUNK

    def body(q_ref, kc_ref, kprev_ref, vc_ref, vprev_ref, sk_ref, o_ref, lse_ref):
        n = pl.program_id(0)
        valid = _attn_valid(n)
        lane = _iota((CHUNK, 128), 1)
        lse_all = jnp.zeros((CHUNK, 128), F32)
        for m in range(8):
            g = m // 2
            qch = q_ref[:, m * 128:(m + 1) * 128]
            o_pair = jnp.zeros((CHUNK, 128), F32)
            for e in range(2):
                h = 2 * m + e
                sl = slice((2 * g + e) * 128, (2 * g + e + 1) * 128)
                kk = jnp.concatenate([kprev_ref[:, sl], kc_ref[:, sl]], axis=0)
                vv = jnp.concatenate([vprev_ref[:, sl], vc_ref[:, sl]], axis=0)
                s = lax.dot_general(qch, kk, (((1,), (1,)), ((), ())), preferred_element_type=F32)
                s = jnp.where(valid, s, NEG)
                sink = sk_ref[0:1, h:h + 1]
                mx = jnp.maximum(jnp.max(s, axis=1, keepdims=True), sink)
                p = jnp.exp(s - mx)
                den = jnp.sum(p, axis=1, keepdims=True) + jnp.exp(sink - mx)
                o_pair = o_pair + jnp.dot((p / den).astype(BF16), vv, preferred_element_type=F32)
                lse_all = jnp.where(lane == h, mx + jnp.log(den), lse_all)
            o_ref[:, m * 128:(m + 1) * 128] = o_pair.astype(BF16)
        lse_ref[...] = lse_all

    cur = pl.BlockSpec((CHUNK, 1024), lambda n: (n, 0))
    prev = pl.BlockSpec((CHUNK, 1024), lambda n: (jnp.maximum(n - 1, 0), 0))
    return pl.pallas_call(
        body, name="attn_fwd", grid=(nb,),
        in_specs=[cur, cur, prev, cur, prev, pl.BlockSpec((1, 128), lambda n: (0, 0))],
        out_specs=[cur, pl.BlockSpec((CHUNK, 128), lambda n: (n, 0))],
        out_shape=[SDS((t, 1024), BF16), SDS((t, 128), F32)],
        compiler_params=_cp(dimension_semantics=("arbitrary",)),
    )(qr, kp, kp, vp, vp, sinks)


def _attn_bwd(qr, kp, vp, d_o, o, lse, sinks, cos_t, sin_t, d_proj, ride=None):
    t = qr.shape[0]
    nb = t // CHUNK

    r_ops, r_in_specs, r_outs, r_out_specs, r_scratch = _ride_parts(ride)

    def body(q_ref, kc_ref, kprev_ref, vc_ref, vprev_ref, do_ref, o_ref, lse_ref, sk_ref, c_ref, s_ref, cp_ref, sp_ref,
             _, *rest):
        r_in, rest = rest[:len(r_ops)], rest[len(r_ops):]
        dqkv_ref, dsk_ref = rest[:2]
        r_out, rest = rest[2:2 + len(r_outs)], rest[2 + len(r_outs):]
        acc_k, acc_v, dq_scr = rest[:3]
        n = pl.program_id(0)
        _ride_run(ride, n == 0, n == nb, r_in, r_out, rest[3:])
        lane = _iota((CHUNK, 128), 1)
        lo = lane < 64
        lane1 = _iota((1, 128), 1)

        @pl.when(n == 0)
        def _():
            acc_k[...] = jnp.zeros_like(acc_k)
            acc_v[...] = jnp.zeros_like(acc_v)
            dsk_ref[...] = jnp.zeros((1, 128), F32)

        @pl.when(n > 0)
        def _():
            dqkv_ref[:, 0:1024] = dq_scr[...]
            for r in range(8):
                acc_k[r, 0:CHUNK] = acc_k[r, CHUNK:2 * CHUNK]
                acc_v[r, 0:CHUNK] = acc_v[r, CHUNK:2 * CHUNK]
                acc_k[r, CHUNK:2 * CHUNK] = jnp.zeros((CHUNK, 128), F32)
                acc_v[r, CHUNK:2 * CHUNK] = jnp.zeros((CHUNK, 128), F32)

        @pl.when(n < nb)
        def _():
            valid = _attn_valid(n)
            lse_all = lse_ref[...]
            dsk = jnp.zeros((1, 128), F32)
            for m in range(8):
                g = m // 2
                csl = slice(m * 128, (m + 1) * 128)
                qch = q_ref[:, csl]
                doch = do_ref[:, csl]
                prod = doch.astype(F32) * o_ref[:, csl].astype(F32)
                dq_pair = jnp.zeros((CHUNK, 128), F32)
                for e in range(2):
                    h = 2 * m + e
                    sl = slice((2 * g + e) * 128, (2 * g + e + 1) * 128)
                    kk = jnp.concatenate([kprev_ref[:, sl], kc_ref[:, sl]], axis=0)
                    vv = jnp.concatenate([vprev_ref[:, sl], vc_ref[:, sl]], axis=0)
                    lse_h = lse_all[:, h:h + 1]
                    s = lax.dot_general(qch, kk, (((1,), (1,)), ((), ())), preferred_element_type=F32)
                    p = jnp.exp(jnp.where(valid, s, NEG) - lse_h)
                    delta = jnp.sum(jnp.where(lo if e == 0 else jnp.logical_not(lo), prod, 0.0), axis=1, keepdims=True)
                    d_p = lax.dot_general(doch, vv, (((1,), (1,)), ((), ())), preferred_element_type=F32)
                    d_s16 = (p * (d_p - delta)).astype(BF16)
                    dq_pair = dq_pair + jnp.dot(d_s16, kk, preferred_element_type=F32)
                    acc_k[2 * g + e] += lax.dot_general(d_s16, qch, (((0,), (0,)), ((), ())), preferred_element_type=F32)
                    acc_v[2 * g + e] += lax.dot_general(p.astype(BF16), doch, (((0,), (0,)), ((), ())), preferred_element_type=F32)
                    p_sink = jnp.exp(sk_ref[0:1, h:h + 1] - lse_h)
                    dsk = jnp.where(lane1 == h, -jnp.sum(p_sink * delta), dsk)
                dq_scr[:, csl] = (_rope(dq_pair, c_ref[...], -s_ref[...]) * 0.125).astype(BF16)
            dsk_ref[...] += dsk

        @pl.when(n > 0)
        def _():
            for m2 in range(2):
                halves = []
                for g in (2 * m2, 2 * m2 + 1):
                    for acc in (acc_k, acc_v):
                        comb = jnp.where(lo, acc[2 * g, 0:CHUNK], acc[2 * g + 1, 0:CHUNK])
                        halves.append(comb + pltpu.roll(comb, 64, axis=1))
                d_kr = jnp.where(lo, halves[0], halves[2])
                d_v = jnp.where(lo, halves[1], halves[3])
                dqkv_ref[:, OFF_K + m2 * 128:OFF_K + (m2 + 1) * 128] = _rope(d_kr, cp_ref[...], -sp_ref[...]).astype(BF16)
                dqkv_ref[:, OFF_V + m2 * 128:OFF_V + (m2 + 1) * 128] = d_v.astype(BF16)

    qn = lambda n: jnp.minimum(n, nb - 1)
    pn = lambda n: jnp.maximum(jnp.minimum(n, nb) - 1, 0)
    cur = pl.BlockSpec((CHUNK, 1024), lambda n: (qn(n), 0))
    prev = pl.BlockSpec((CHUNK, 1024), lambda n: (pn(n), 0))
    cur128 = pl.BlockSpec((CHUNK, 128), lambda n: (qn(n), 0))
    prev128 = pl.BlockSpec((CHUNK, 128), lambda n: (pn(n), 0))
    one = pl.BlockSpec((1, 128), lambda n: (0, 0))
    return pl.pallas_call(
        body, name="attn_bwd", grid=(nb + 1,),
        in_specs=[cur, cur, prev, cur, prev, cur, cur, cur128, one, cur128, cur128, prev128, prev128,
                  pl.BlockSpec(memory_space=pl.ANY)] + r_in_specs,
        out_specs=[pl.BlockSpec((CHUNK, 1536), lambda n: (pn(n), 0)), one] + r_out_specs,
        out_shape=[SDS((t, PROJ_W), BF16), SDS((1, 128), F32)] + r_outs,
        scratch_shapes=[pltpu.VMEM((8, 2 * CHUNK, 128), F32), pltpu.VMEM((8, 2 * CHUNK, 128), F32),
                        pltpu.VMEM((CHUNK, 1024), BF16)] + r_scratch,
        input_output_aliases={13: 0},
        compiler_params=_cp(dimension_semantics=("arbitrary",)),
    )(qr, kp, kp, vp, vp, d_o, o, lse, sinks, cos_t, sin_t, cos_t, sin_t, d_proj, *r_ops)


def _adamw(name, w, g, m, v, *, tr):
    rows, cols = w.shape
    tr = min(tr, rows)
    assert rows % tr == 0

    def body(w_ref, g_ref, m_ref, v_ref, d_ref, nm_ref, nv_ref):
        gv = g_ref[...]
        nm = ADAM_B1 * m_ref[...] + (1.0 - ADAM_B1) * gv
        nv = ADAM_B2 * v_ref[...] + (1.0 - ADAM_B2) * (gv * gv)
        m_hat = nm / (1.0 - ADAM_B1 ** ADAM_STEP)
        v_hat = nv / (1.0 - ADAM_B2 ** ADAM_STEP)
        d_ref[...] = -ADAM_LR * (m_hat / (jnp.sqrt(v_hat) + ADAM_EPS) + ADAM_WD * w_ref[...])
        nm_ref[...] = nm
        nv_ref[...] = nv

    tile = pl.BlockSpec((tr, cols), lambda i: (i, 0))
    return pl.pallas_call(
        body, name=name, grid=(rows // tr,), in_specs=[tile] * 4, out_specs=[tile] * 3,
        out_shape=[SDS((rows, cols), F32)] * 3, compiler_params=_cp(dimension_semantics=("arbitrary",)),
    )(w, g, m, v)


def _local_step(x, cos_t, sin_t, tgt, wb, ps, late=None, rides=None):
    t = x.shape[0]
    tm = min(512, t)
    tmw = min(1024, t)
    ij = lambda i, j, k: (i, j)
    i0 = lambda i, j, k: (i, 0)
    c0 = lambda i, j, k: (0, 0)
    cj = lambda i, j, k: (0, j)
    rides = rides or (lambda group, grads: None)
    rode = {}

    tkt = min(2048, t)
    proj, u, *arrived = _norm_mm("in_proj", x, ps['norm_mix_pre_w'], wb['cat'], tm=tmw, tn=1024,
                                 ride=late[0] if late else None)
    if late:
        more_wb, more_ps = late[1](arrived)
        wb, ps = {**wb, **more_wb}, {**ps, **more_ps}
    xc = _conv_silu_fwd(proj, ps['ssd_conv_w'], ps['ssd_conv_b'], tm=tm)
    bias_pad = jnp.pad(ps['ssd_dt_bias'], ((0, 0), (0, 96)))
    dt = _dt_fwd(proj, bias_pad, tm=tmw)
    dt32 = dt[:, :SSD_N_HEADS].reshape(t, SSD_N_GROUPS, 8)
    dtg = jnp.pad(dt32.transpose(1, 0, 2), ((0, 0), (0, 0), (0, 120)))
    dtg_t = dt32.transpose(1, 2, 0)
    alog = jnp.pad(ps['ssd_a_log'].reshape(SSD_N_GROUPS, 1, 8), ((0, 0), (0, 0), (0, 120)))
    alog_t = ps['ssd_a_log'].reshape(SSD_N_GROUPS, 8, 1)
    d_exp = jnp.repeat(ps['ssd_d'], SSD_HEAD_DIM, axis=1)
    y, hs = _ssd_fwd(xc, dtg, dtg_t, alog, alog_t, d_exp)
    gn = _gated_norm_fwd(y, proj, ps['ssd_norm_w'], tm=tm)
    qr, kp, vp = _rope_qkv(proj, cos_t, sin_t, tm=tm)
    sinks = jnp.pad(ps['attn_sinks'], ((0, 0), (0, 112)))
    ao, lse = _attn_fwd(qr, kp, vp, sinks)
    y_attn = _mm_plain("attn_out", ao, wb['ao'], tm=tmw, tn=512, tk=1024)

    def merge_ep(acc, i, j, ins, outs):
        gs, ga, ya = ins
        outs[0][...] = (_sigmoid(gs[...]) * acc + _sigmoid(ga[...]) * ya[...]).astype(BF16)
        outs[1][...] = acc

    merged, y_ssd = _mm_call(
        "ssd_out_merge", gn, wb['so'], tm=tmw, tn=512, tk=2048, epilogue=merge_ep,
        extra_in=[(proj, (tmw, 512), lambda i, j, k: (i, OFF_GS // 512 + j)),
                  (proj, (tmw, 512), lambda i, j, k: (i, OFF_GA // 512 + j)), (y_attn, (tmw, 512), ij)],
        outs=[((t, D_MODEL), BF16, (tmw, 512), ij), ((t, D_MODEL), F32, (tmw, 512), ij)])

    def mix_ep(acc, i, j, ins, outs):
        xv, wn = ins
        r = lax.rsqrt(jnp.mean(acc * acc, axis=-1, keepdims=True) + NORM_EPS)
        outs[0][...] = xv[...] + acc * r * wn[...]
        outs[1][...] = acc

    x1, mmix = _mm_call(
        "mix_out", merged, wb['mix'], tm=tm, tn=D_MODEL, tk=1024, epilogue=mix_ep,
        extra_in=[(x, (tm, D_MODEL), i0), (ps['norm_mix_post_w'], (1, D_MODEL), c0)],
        outs=[((t, D_MODEL), F32, (tm, D_MODEL), i0), ((t, D_MODEL), F32, (tm, D_MODEL), i0)])

    up_raw, h = _norm_mm("ffn_up", x1, ps['norm_ffn_pre_w'], wb['up'], tm=tmw, tn=1408)
    act = _ffn_act_fwd(up_raw, ps['ffn_conv_w'], ps['ffn_conv_b'], tm=min(256, t))

    def loss_ep(acc, i, j, ins, outs):
        x1v, tg, wn = ins
        d_ff_ref, dout_ref, loss_ref, dw_ref = outs
        wv = wn[...]
        r = lax.rsqrt(jnp.mean(acc * acc, axis=-1, keepdims=True) + NORM_EPS)
        err = x1v[...] + acc * r * wv - tg[...]
        dout = err * (1.0 / D_MODEL)
        dout_ref[...] = dout
        d_ff, dw = _rms_bwd(acc, wv, dout)
        d_ff_ref[...] = d_ff.astype(BF16)
        _accumulate(dw_ref, i == 0, dw)
        _accumulate(loss_ref, i == 0, jnp.sum(err * err, keepdims=True) * (0.5 / D_MODEL))

    d_ff, dout, loss, g_norm_ffn_post = _mm_call(
        "ffn_down_loss", act, wb['dn'], tm=tm, tn=D_MODEL, tk=FFN_D_FF, epilogue=loss_ep,
        extra_in=[(x1, (tm, D_MODEL), i0), (tgt, (tm, D_MODEL), i0), (ps['norm_ffn_post_w'], (1, D_MODEL), c0)],
        outs=[((t, D_MODEL), BF16, (tm, D_MODEL), i0), ((t, D_MODEL), F32, (tm, D_MODEL), i0),
              ((1, 1), F32, (1, 1), c0), ((1, D_MODEL), F32, (1, D_MODEL), c0)])

    d_act = _mm_plain("d_act", d_ff, wb['dn_t'], tm=tmw, tn=1408, tk=1024, out_dtype=BF16)
    g_w_down = _mm_plain("g_w_down", act, d_ff, tm=1408, tn=1024, tk=tkt, trans_a=True, out_dtype=BF16)
    d_gate, d_val, db_g, db_v = _ffn_act_bwd(up_raw, d_act, ps['ffn_conv_w'], ps['ffn_conv_b'], tm=min(256, t))
    d_up_raw, gcw_g = _conv_bwd2("ffn_conv_bwd2_gate", d_gate, up_raw, 0, ps['ffn_conv_w'][:, :FFN_D_FF], tm=min(256, t),
                                 tc=1408, out_cols=2 * FFN_D_FF, out_col0=0)
    d_up_raw, gcw_v = _conv_bwd2("ffn_conv_bwd2_val", d_val, up_raw, FFN_D_FF, ps['ffn_conv_w'][:, FFN_D_FF:], tm=min(256, t),
                                 tc=1408, out_cols=2 * FFN_D_FF, out_col0=FFN_D_FF, fill=d_up_raw)
    g_ffn_conv_w = jnp.concatenate([gcw_g, gcw_v], axis=1)

    def dx1_ep(acc, i, j, ins, outs):
        x1v, wpre, dout_v, mmv, wpost = ins
        d_x1_ref, d_mm_ref, dwpre_ref, dwpost_ref = outs
        d_n, dw_pre = _rms_bwd(x1v[...], wpre[...], acc)
        d_x1 = dout_v[...] + d_n
        d_x1_ref[...] = d_x1
        d_mm, dw_post = _rms_bwd(mmv[...], wpost[...], d_x1)
        d_mm_ref[...] = d_mm.astype(BF16)
        _accumulate(dwpre_ref, i == 0, dw_pre)
        _accumulate(dwpost_ref, i == 0, dw_post)

    d_x1, d_mm, g_norm_ffn_pre, g_norm_mix_post = _mm_call(
        "d_h", d_up_raw, wb['up_t'], tm=tm, tn=D_MODEL, tk=FFN_D_FF, epilogue=dx1_ep,
        extra_in=[(x1, (tm, D_MODEL), i0), (ps['norm_ffn_pre_w'], (1, D_MODEL), c0), (dout, (tm, D_MODEL), i0),
                  (mmix, (tm, D_MODEL), i0), (ps['norm_mix_post_w'], (1, D_MODEL), c0)],
        outs=[((t, D_MODEL), F32, (tm, D_MODEL), i0), ((t, D_MODEL), BF16, (tm, D_MODEL), i0),
              ((1, D_MODEL), F32, (1, D_MODEL), c0), ((1, D_MODEL), F32, (1, D_MODEL), c0)])
    g_w_up_t = _mm_plain("g_w_up", d_up_raw, h, tm=1408, tn=1024, tk=tkt, trans_a=True, out_dtype=BF16)
    ride_ffn = rides('ffn', {'ffn_w_up': g_w_up_t, 'ffn_w_down': g_w_down})

    def dmerge_ep(acc, i, j, ins, outs):
        gs, ga, ys, ya = ins
        sg_s, sg_a = _sigmoid(gs[...]), _sigmoid(ga[...])
        outs[0][...] = (acc * sg_s).astype(BF16)
        outs[1][...] = (acc * sg_a).astype(BF16)
        outs[2][:, 0:D_MODEL] = (acc * ys[...] * sg_s * (1.0 - sg_s)).astype(BF16)
        outs[2][:, D_MODEL:2 * D_MODEL] = (acc * ya[...] * sg_a * (1.0 - sg_a)).astype(BF16)

    d_yssd, d_yattn, d_proj = _mm_call(
        "d_merged", d_mm, wb['mix_t'], tm=tm, tn=D_MODEL, tk=1024, epilogue=dmerge_ep,
        extra_in=[(proj, (tm, D_MODEL), lambda i, j, k: (i, OFF_GS // D_MODEL)),
                  (proj, (tm, D_MODEL), lambda i, j, k: (i, OFF_GA // D_MODEL)), (y_ssd, (tm, D_MODEL), i0), (y_attn, (tm, D_MODEL), i0)],
        outs=[((t, D_MODEL), BF16, (tm, D_MODEL), i0), ((t, D_MODEL), BF16, (tm, D_MODEL), i0),
              ((t, PROJ_W), BF16, (tm, 2 * D_MODEL), lambda i, j, k: (i, OFF_GS // (2 * D_MODEL)))])
    g_w_mix = _mm_plain("g_w_mix", merged, d_mm, tm=1024, tn=1024, tk=tkt, trans_a=True, out_dtype=BF16)

    def dgn_ep(acc, i, j, ins, outs):
        yv, zv, wn = ins
        d_y_ref, d_z_ref, dw_ref = outs
        zz = zv[...]
        sz = _sigmoid(zz)
        silu = zz * sz
        gv = yv[...] * silu
        r = lax.rsqrt(jnp.mean(gv * gv, axis=-1, keepdims=True) + NORM_EPS)
        gh = gv * r
        dgh = acc * wn[...]
        dg = r * (dgh - gh * jnp.mean(dgh * gh, axis=-1, keepdims=True))
        d_y_ref[...] = dg * silu
        d_z_ref[...] = (dg * yv[...] * (sz * (1.0 + zz * (1.0 - sz)))).astype(BF16)
        dw = jnp.sum(acc * gh, axis=0, keepdims=True)

        @pl.when(i == 0)
        def _():
            dw_ref[j] = dw

        @pl.when(i > 0)
        def _():
            dw_ref[j] += dw

    d_y, d_proj, g_ssd_norm = _mm_call(
        "d_gn", d_yssd, wb['so_t'], tm=tm, tn=512, tk=1024, epilogue=dgn_ep, fill=(d_proj, 1),
        extra_in=[(y, (tm, 512), ij), (proj, (tm, 512), lambda i, j, k: (i, OFF_Z // 512 + j)), (ps['ssd_norm_w'], (1, 512), cj)],
        outs=[((t, SSD_D_INNER), F32, (tm, 512), ij), ((t, PROJ_W), BF16, (tm, 512), lambda i, j, k: (i, OFF_Z // 512 + j)),
              ((SSD_N_GROUPS, 1, 512), F32, (SSD_N_GROUPS, 1, 512), lambda i, j, k: (0, 0, 0))])
    g_ssd_norm = g_ssd_norm.reshape(1, SSD_D_INNER)
    g_w_so = _mm_plain("g_w_so", gn, d_yssd, tm=1024, tn=1024, tk=tkt, trans_a=True, out_dtype=BF16)
    d_xc, d_dtg, d_alog, d_dd, *rode['ffn'] = _ssd_bwd(xc, dtg, dtg_t, alog, alog_t, d_exp, d_y, hs, ride=ride_ffn)
    d_pre, g_ssd_conv_b = _conv_silu_bwd1(d_xc, proj, ps['ssd_conv_w'], ps['ssd_conv_b'], tm=tm)
    d_proj, g_ssd_conv_w = _conv_bwd2("ssd_conv_bwd2", d_pre, proj, OFF_XBC, ps['ssd_conv_w'], tm=tm, tc=512,
                                      out_cols=PROJ_W, out_col0=OFF_XBC, fill=d_proj)
    d_dt = jnp.pad(d_dtg[:, :, :8].transpose(1, 0, 2).reshape(t, SSD_N_HEADS), ((0, 0), (0, 96)))
    d_proj, g_dt_bias = _dt_bwd(d_dt, proj, bias_pad, d_proj, tm=tmw)

    d_ao = _mm_plain("d_ao", d_yattn, wb['ao_t'], tm=tmw, tn=512, tk=1024, out_dtype=BF16)
    g_w_ao = _mm_plain("g_w_ao", ao, d_yattn, tm=1024, tn=1024, tk=tkt, trans_a=True, out_dtype=BF16)
    ride_mix = rides('mix', {'ssd_w_out': g_w_so, 'attn_w_out': g_w_ao, 'w_mix_out': g_w_mix})
    d_proj, g_sinks, *rode['mix'] = _attn_bwd(qr, kp, vp, d_ao, ao, lse, sinks, cos_t, sin_t, d_proj, ride=ride_mix)

    def dx_ep(acc, i, j, ins, outs):
        xv, wn, dx1v = ins
        d_n, dw = _rms_bwd(xv[...], wn[...], acc)
        outs[0][...] = dx1v[...] + d_n
        _accumulate(outs[1], i == 0, dw)

    g_cat_t = _mm_plain("g_w_in", d_proj, u, tm=1024, tn=1024, tk=tkt, trans_a=True)
    grad_x, g_norm_mix_pre, *rode['w_in'] = _mm_call(
        "d_u", d_proj, wb['cat_t'], tm=tm, tn=D_MODEL, tk=2304, epilogue=dx_ep, ride=rides('w_in', {'w_in': g_cat_t}),
        extra_in=[(x, (tm, D_MODEL), i0), (ps['norm_mix_pre_w'], (1, D_MODEL), c0), (d_x1, (tm, D_MODEL), i0)],
        outs=[((t, D_MODEL), F32, (tm, D_MODEL), i0), ((1, D_MODEL), F32, (1, D_MODEL), c0)])

    grads = {
        'norm_mix_pre_w': g_norm_mix_pre, 'w_in': g_cat_t, 'ssd_conv_w': g_ssd_conv_w, 'ssd_conv_b': g_ssd_conv_b,
        'ssd_dt_bias': g_dt_bias[:, :SSD_N_HEADS], 'ssd_a_log': d_alog[:, 0, :8].reshape(1, SSD_N_HEADS),
        'ssd_d': d_dd[:, 0, :8].reshape(1, SSD_N_HEADS), 'ssd_norm_w': g_ssd_norm, 'ssd_w_out': g_w_so,
        'attn_sinks': g_sinks[:, :ATTN_N_HEADS], 'attn_w_out': g_w_ao, 'w_mix_out': g_w_mix,
        'norm_mix_post_w': g_norm_mix_post, 'norm_ffn_pre_w': g_norm_ffn_pre, 'ffn_w_up': g_w_up_t,
        'ffn_conv_w': g_ffn_conv_w, 'ffn_conv_b': jnp.concatenate([db_g, db_v], axis=1), 'ffn_w_down': g_w_down,
        'norm_ffn_post_w': g_norm_ffn_post,
    }
    return loss, grad_x, grads, rode


def _group_channels(a):
    parts = []
    for g in range(SSD_N_GROUPS):
        parts += [a[..., 512 * g:512 * (g + 1)], a[..., 2048 + 128 * g:2048 + 128 * (g + 1)],
                  a[..., 2560 + 128 * g:2560 + 128 * (g + 1)]]
    return jnp.concatenate(parts, axis=-1)


def _ungroup_channels(a):
    xs = [a[..., GROUP_W * g:GROUP_W * g + 512] for g in range(SSD_N_GROUPS)]
    bs = [a[..., GROUP_W * g + 512:GROUP_W * g + 640] for g in range(SSD_N_GROUPS)]
    cs = [a[..., GROUP_W * g + 640:GROUP_W * (g + 1)] for g in range(SSD_N_GROUPS)]
    return jnp.concatenate(xs + bs + cs, axis=-1)


def _proj_rows(a_t, lo, hi):
    out = []
    for start, length, dst in sorted(PROJ_SEGS):
        s, e = max(lo, start), min(hi, start + length)
        if s < e:
            out.append(a_t[dst + s - start:dst + e - start])
    return out


def _to_proj_layout(w_in_t):
    pieces, pos = [], 0
    for start, length, dst in sorted(PROJ_SEGS, key=lambda s: s[2]):
        if dst > pos:
            pieces.append(jnp.zeros((dst - pos, w_in_t.shape[1]), w_in_t.dtype))
        pieces.append(w_in_t[start:start + length])
        pos = dst + length
    if pos < PROJ_W:
        pieces.append(jnp.zeros((PROJ_W - pos, w_in_t.shape[1]), w_in_t.dtype))
    return jnp.concatenate(pieces, axis=0)


def _rope_tables(positions):
    half = 32
    inv_freq = ROPE_THETA ** (-jnp.arange(half, dtype=F32) * 2.0 / 64)
    ang = positions.astype(F32)[:, None] * inv_freq
    cos, sin = jnp.cos(ang), jnp.sin(ang)
    return jnp.concatenate([cos, cos, cos, cos], axis=1), jnp.concatenate([-sin, sin, -sin, sin], axis=1)


def _matmul_weights(w_in_t):
    cat_t = _to_proj_layout(w_in_t)
    return {'cat': cat_t.T, 'cat_t': cat_t}


def _late_weights(so, ao, mix, up_t, dn):
    return {'so': so, 'so_t': so.T, 'ao': ao, 'ao_t': ao.T, 'mix': mix, 'mix_t': mix.T,
            'up': up_t.T, 'up_t': up_t, 'dn': dn, 'dn_t': dn.T}


ANY = pl.BlockSpec(memory_space=pl.ANY)
MESH = pl.DeviceIdType.MESH
ROW_ALIGN = 32


def _mesh_pos():
    return lax.axis_index("x"), lax.axis_index("y"), lax.axis_index("c")


def _other_chips(x, y):
    return [(1 - x, y), (x, 1 - y), (1 - x, 1 - y)]


def _remote(src, dst, send_sems, recv_sems, k, to):
    return pltpu.make_async_remote_copy(src_ref=src, dst_ref=dst, send_sem=send_sems.at[k], recv_sem=recv_sems.at[k],
                                        device_id=to, device_id_type=MESH)


def _half(c, rh):
    return pl.ds(pl.multiple_of(c * rh, 16), rh)


def _ag_ride(shard):
    r = shard.shape[0]
    rh = r // 2

    def first_copies(w_ref, out_ref, send_sems, recv_sems):
        x, y, c = _mesh_pos()
        p = 2 * x + y
        mine = _half(c, rh)
        cps = [_remote(w_ref, out_ref.at[p], send_sems, recv_sems, 6, (x, y, 1 - c))]
        return cps + [_remote(w_ref.at[mine], out_ref.at[p, mine], send_sems, recv_sems, j, (cx, cy, c))
                      for j, (cx, cy) in enumerate(_other_chips(x, y))]

    def start(ins, outs, send_sems, recv_sems):
        for cp in first_copies(ins[0], outs[0], send_sems, recv_sems):
            cp.start()

    def finish(ins, outs, send_sems, recv_sems):
        w_ref, out_ref = ins[0], outs[0]
        x, y, c = _mesh_pos()
        sib = (x, y, 1 - c)
        mine, other = _half(c, rh), _half(1 - c, rh)
        chips = _other_chips(x, y)
        passed = []
        for j, (cx, cy) in enumerate(chips):
            slab = out_ref.at[2 * cx + cy, mine]
            _remote(slab, slab, send_sems, recv_sems, j, sib).wait_recv()
            fwd = _remote(slab, slab, send_sems, recv_sems, 3 + j, sib)
            fwd.start()
            passed.append(fwd)
        for j, (cx, cy) in enumerate(chips):
            slab = out_ref.at[2 * cx + cy, other]
            _remote(slab, slab, send_sems, recv_sems, 3 + j, sib).wait_recv()
        _remote(w_ref, out_ref.at[2 * x + y], send_sems, recv_sems, 6, sib).wait_recv()
        for cp in first_copies(w_ref, out_ref, send_sems, recv_sems) + passed:
            cp.wait_send()

    return _Ride((shard,), (SDS((N_CHIPS, r, COMM_LANES), shard.dtype),), 7, start, finish)


def _rs_ride(gbuf):
    rh = gbuf.shape[1] // 2

    def copies(g_ref, r_ref, send_sems, recv_sems, landing):
        x, y, c = _mesh_pos()
        cps = []
        for k, (cx, cy) in enumerate(_other_chips(x, y)):
            for h in range(2):
                slot = 2 * k + c if landing else 2 * k + h
                cps.append(pltpu.make_async_remote_copy(
                    src_ref=g_ref.at[2 * cx + cy, pl.ds(h * rh, rh)], dst_ref=r_ref.at[slot],
                    send_sem=send_sems.at[2 * k + h], recv_sem=recv_sems.at[slot],
                    device_id=(cx, cy, h), device_id_type=MESH))
        cps.append(_remote(g_ref.at[2 * x + y, _half(1 - c, rh)], r_ref.at[6], send_sems, recv_sems, 6, (x, y, 1 - c)))
        return cps

    def start(ins, outs, send_sems, recv_sems):
        for cp in copies(ins[0], outs[0], send_sems, recv_sems, True):
            cp.start()

    def finish(ins, outs, send_sems, recv_sems):
        for cp in copies(ins[0], outs[0], send_sems, recv_sems, False):
            cp.wait()

    return _Ride((gbuf,), (SDS((7, rh, COMM_LANES), gbuf.dtype),), 7, start, finish)


def _rs_sum(name, gbuf, got, pc_idx):
    rh = got.shape[1]
    tr = max(d for d in range(16, 513, 16) if rh % d == 0)
    nb = rh // tr

    def body(pc_ref, own_ref, *refs):
        o_ref = refs[7]
        p, c = pc_ref[0], pc_ref[1]
        own = own_ref[0].astype(F32)
        slots = [r[0].astype(F32) for r in refs[:7]]

        def term(q, h):
            code = p ^ q
            far = jnp.where(code == 2, slots[h], jnp.where(code == 1, slots[2 + h], slots[4 + h]))
            return jnp.where(code == 0, jnp.where(c == h, own, slots[6]), far)

        acc = term(0, 0)
        for q, h in [(0, 1), (1, 0), (1, 1), (2, 0), (2, 1), (3, 0), (3, 1)]:
            acc = acc + term(q, h)
        o_ref[0] = acc

    slot = lambda s: pl.BlockSpec((1, tr, COMM_LANES), lambda i, pc: (s, i, 0))
    return pl.pallas_call(
        body, name=name,
        grid_spec=pltpu.PrefetchScalarGridSpec(
            num_scalar_prefetch=1, grid=(nb,),
            in_specs=[pl.BlockSpec((1, tr, COMM_LANES), lambda i, pc: (pc[0], pc[1] * nb + i, 0))] + [slot(s) for s in range(7)],
            out_specs=pl.BlockSpec((1, tr, COMM_LANES), lambda i, pc: (pc[1], i, 0))),
        out_shape=SDS((2, rh, COMM_LANES), F32), compiler_params=_cp(dimension_semantics=("arbitrary",)),
    )(pc_idx, gbuf, *([got] * 7))


def _pair_gather_all(bufs):
    n = len(bufs)

    def body(*refs):
        outs, send_sems, recv_sems = refs[n:2 * n], refs[2 * n], refs[2 * n + 1]
        x, y, c = _mesh_pos()
        cps = [_remote(o.at[c], o.at[c], send_sems, recv_sems, k, (x, y, 1 - c)) for k, o in enumerate(outs)]
        for cp in cps:
            cp.start()
        for k, o in enumerate(outs):
            _remote(o.at[1 - c], o.at[1 - c], send_sems, recv_sems, k, (x, y, 1 - c)).wait_recv()
        for cp in cps:
            cp.wait_send()

    return pl.pallas_call(
        body, name="grad_pair_gather", in_specs=[ANY] * n, out_specs=[ANY] * n,
        out_shape=[SDS(b.shape, b.dtype) for b in bufs],
        scratch_shapes=[pltpu.SemaphoreType.DMA((n,)), pltpu.SemaphoreType.DMA((n,))],
        input_output_aliases={k: k for k in range(n)},
    )(*bufs)


def _pack_rows(big, small=()):
    parts = list(big)
    if small:
        flat = jnp.concatenate([p.reshape(-1) for p in small])
        k = -(-flat.shape[0] // (16 * COMM_LANES)) * 16
        parts.append(jnp.pad(flat, (0, k * COMM_LANES - flat.shape[0])).reshape(k, COMM_LANES))
    pad = -sum(p.shape[0] for p in parts) % ROW_ALIGN
    if pad:
        parts.append(jnp.zeros((pad, COMM_LANES), parts[0].dtype))
    return jnp.concatenate(parts, axis=0) if len(parts) > 1 else parts[0]


def _take(flat, off, shape):
    n = 1
    for d in shape:
        n *= d
    return flat[off:off + n].reshape(shape), off + n


BIG_ROWS = {'w_in': 2184, 'ssd_w_out': 512, 'attn_w_out': 256, 'w_mix_out': 256, 'ffn_w_up': 1408, 'ffn_w_down': 704}
TRANSPOSED = ('w_in', 'ffn_w_up')
LATE = ('ssd_w_out', 'attn_w_out', 'w_mix_out', 'ffn_w_up', 'ffn_w_down')
CONV_TAPS = ('ssd_conv_w', 'ffn_conv_w')
RS_GROUPS = {'ffn': ('ffn_w_up', 'ffn_w_down'), 'mix': ('ssd_w_out', 'attn_w_out', 'w_mix_out'), 'w_in': ('w_in',)}


def _exchange(name, ride):
    n_in, n_out = len(ride.ins), len(ride.outs)

    def body(*refs):
        ins, outs, sems = refs[:n_in], refs[n_in:n_in + n_out], refs[n_in + n_out:]
        ride.start(ins, outs, *sems)
        ride.finish(ins, outs, *sems)

    return pl.pallas_call(
        body, name=name, in_specs=[ANY] * n_in, out_specs=[ANY] * n_out, out_shape=list(ride.outs),
        scratch_shapes=[pltpu.SemaphoreType.DMA((ride.n_sems,)), pltpu.SemaphoreType.DMA((ride.n_sems,))],
    )(*ride.ins)


def kernel(x, positions, norm_mix_pre_w, w_in, ssd_conv_w, ssd_conv_b, ssd_dt_bias, ssd_a_log, ssd_d, ssd_norm_w, ssd_w_out, attn_sinks, attn_w_out, w_mix_out, norm_mix_post_w, norm_ffn_pre_w, ffn_w_up, ffn_conv_w, ffn_conv_b, ffn_w_down, norm_ffn_post_w, loss_target, m_norm_mix_pre_w, m_w_in, m_ssd_conv_w, m_ssd_conv_b, m_ssd_dt_bias, m_ssd_a_log, m_ssd_d, m_ssd_norm_w, m_ssd_w_out, m_attn_sinks, m_attn_w_out, m_w_mix_out, m_norm_mix_post_w, m_norm_ffn_pre_w, m_ffn_w_up, m_ffn_conv_w, m_ffn_conv_b, m_ffn_w_down, m_norm_ffn_post_w, v_norm_mix_pre_w, v_w_in, v_ssd_conv_w, v_ssd_conv_b, v_ssd_dt_bias, v_ssd_a_log, v_ssd_d, v_ssd_norm_w, v_ssd_w_out, v_attn_sinks, v_attn_w_out, v_w_mix_out, v_norm_mix_post_w, v_norm_ffn_pre_w, v_ffn_w_up, v_ffn_conv_w, v_ffn_conv_b, v_ffn_w_down, v_norm_ffn_post_w):
    given = dict(locals())
    w = {n: given[n][0] for n in WEIGHTS}
    w = {n: (a if a.ndim == 2 else a[None]) for n, a in w.items()}
    mom_m = {n: given['m_' + n].reshape(w[n].shape) for n in WEIGHTS}
    mom_v = {n: given['v_' + n].reshape(w[n].shape) for n in WEIGHTS}
    cx, cy, cc = _mesh_pos()
    pc_idx = jnp.stack([2 * cx + cy, cc]).astype(jnp.int32)

    rows_of = lambda n: (w[n].T if n in TRANSPOSED else w[n]).astype(BF16)
    gathered = _exchange("w_in_all_gather", _ag_ride(_pack_rows([rows_of('w_in')])))[0]
    wb = _matmul_weights(jnp.concatenate([gathered[s, :BIG_ROWS['w_in']] for s in range(N_CHIPS)], axis=0))
    taps = [lax.bitcast_convert_type(w[n], BF16) for n in CONV_TAPS]

    def unpack_late(arrived):
        rows, conv = {n: [] for n in LATE}, {n: [] for n in CONV_TAPS}
        for s in range(N_CHIPS):
            r0 = 0
            for n in LATE:
                rows[n].append(arrived[0][s, r0:r0 + BIG_ROWS[n]])
                r0 += BIG_ROWS[n]
            flat, off = arrived[0][s, r0:r0 + 16].reshape(-1), 0
            for n in CONV_TAPS:
                a, off = _take(flat, off, w[n].shape + (2,))
                conv[n].append(lax.bitcast_convert_type(a, F32))
        full = {n: jnp.concatenate(rows[n], axis=0) for n in LATE}
        return (_late_weights(*[full[n] for n in LATE]),
                {'ssd_conv_w': _group_channels(jnp.concatenate(conv['ssd_conv_w'], axis=1)),
                 'ffn_conv_w': jnp.concatenate(conv['ffn_conv_w'], axis=1)})

    late = (_ag_ride(_pack_rows([rows_of(n) for n in LATE], taps)), unpack_late)

    sent = {}

    def rides(group, g):
        parts = []
        for s in range(N_CHIPS):
            slab = []
            for n in RS_GROUPS[group]:
                lo, hi = BIG_ROWS[n] * s, BIG_ROWS[n] * (s + 1)
                slab += _proj_rows(g[n], lo, hi) if n == 'w_in' else [g[n][lo:hi]]
            slab = [a.astype(BF16) for a in slab]
            pad = -sum(a.shape[0] for a in slab) % ROW_ALIGN
            parts += slab + ([jnp.zeros((pad, COMM_LANES), BF16)] if pad else [])
        sent[group] = jnp.concatenate(parts, axis=0).reshape(N_CHIPS, -1, COMM_LANES)
        return _rs_ride(sent[group])

    ps = {n: w[n] for n in REPLICATED}
    ps['ssd_conv_b'] = _group_channels(w['ssd_conv_b'])
    cos_t, sin_t = _rope_tables(positions[0])
    loss, grad_x, grads, rode = _local_step(x[0], cos_t, sin_t, loss_target[0], wb, ps, late, rides)
    grads['ssd_conv_w'] = _ungroup_channels(grads['ssd_conv_w'])
    grads['ssd_conv_b'] = _ungroup_channels(grads['ssd_conv_b'])

    shard_cols = {n: sh[1] for n, _, sh in SHARDED}
    parts = []
    for s in range(N_CHIPS):
        small = [grads[n][:, shard_cols[n] * s:shard_cols[n] * (s + 1)] for n in CONV_TAPS] + [grads[n] for n in REPLICATED]
        flat = _pack_rows([], small)
        high = flat.astype(BF16)
        parts += [high, (flat - high.astype(F32)).astype(BF16)]
    sent['small'] = jnp.concatenate(parts, axis=0).reshape(N_CHIPS, -1, COMM_LANES)
    rode['small'] = _exchange("grad_small_exchange", _rs_ride(sent['small']))

    groups = ('ffn', 'mix', 'w_in', 'small')
    red = _pair_gather_all([_rs_sum("grad_sum_" + g, sent[g], rode[g][0], pc_idx) for g in groups])
    red = {g: r.reshape(-1, COMM_LANES) for g, r in zip(groups, red)}
    g_red = {}
    for g in groups[:3]:
        r0 = 0
        for n in RS_GROUPS[g]:
            g_red[n] = red[g][r0:r0 + BIG_ROWS[n]].T if n in TRANSPOSED else red[g][r0:r0 + BIG_ROWS[n]]
            r0 += BIG_ROWS[n]
    half = red['small'].shape[0] // 2
    flat, off = (red['small'][:half] + red['small'][half:]).reshape(-1), 0
    for n in CONV_TAPS + REPLICATED:
        g_red[n], off = _take(flat, off, w[n].shape)

    small_names = [n for n in WEIGHTS if n not in MATMUL_WEIGHTS]
    delta, new_m, new_v = {}, {}, {}
    for n in MATMUL_WEIGHTS:
        delta[n], new_m[n], new_v[n] = _adamw("adamw_" + n, w[n], g_red[n], mom_m[n], mom_v[n], tr=64)
    packed = [_pack_small([d[n] for n in small_names]) for d in (w, g_red, mom_m, mom_v)]
    outs = _adamw("adamw_small", *packed, tr=packed[0].shape[0])
    for res, o in zip((delta, new_m, new_v), outs):
        fl, off = o.reshape(-1), 0
        for n in small_names:
            res[n], off = _take(fl, off, w[n].shape)

    loss_all = lax.psum(loss[0, 0], ("x", "y", "c"))
    shaped = lambda d: [d[n].reshape(given[n].shape) for n in WEIGHTS]
    return (loss_all, grad_x[None], *shaped(g_red), *shaped(delta), *shaped(new_m), *shaped(new_v))


def _pack_small(pieces):
    flat = jnp.concatenate([p.reshape(-1) for p in pieces])
    rows = -(-flat.shape[0] // (128 * 8)) * 8
    return jnp.pad(flat, (0, rows * 128 - flat.shape[0])).reshape(rows, 128)
```

```python
from typing import Callable, NamedTuple

import jax
import jax.numpy as jnp
from jax import lax
from jax.experimental import pallas as pl
from jax.experimental.pallas import tpu as pltpu

F32 = jnp.float32
BF16 = jnp.bfloat16
SDS = jax.ShapeDtypeStruct
HIGHEST = lax.Precision.HIGHEST

D_MODEL = 1024
SSD_D_INNER = 2048
SSD_N_HEADS = 32
SSD_HEAD_DIM = 64
SSD_N_GROUPS = 4
SSD_HEADS_PER_GROUP = 8
SSD_D_STATE = 128
SSD_CONV_DIM = 3072
CHUNK = 128
ATTN_N_HEADS = 16
KV_WIDTH = 256
FFN_D_FF = 2816
IN_PROJ_DIM = 8736
ROPE_THETA = 10000.0
NORM_EPS = 1e-6
ADAM_LR, ADAM_B1, ADAM_B2, ADAM_EPS, ADAM_WD, ADAM_STEP = 0.001, 0.9, 0.999, 1e-08, 0.01, 10

PROJ_W = 9216
OFF_Q, OFF_K, OFF_V, OFF_Z, OFF_DT, OFF_GS, OFF_GA, OFF_XBC = 0, 1024, 1280, 1536, 3584, 4096, 5120, 6144
GROUP_W = 768
PROJ_SEGS = ([(0, 2048, OFF_Z)]
             + [(2048 + 512 * g, 512, OFF_XBC + GROUP_W * g) for g in range(4)]
             + [(4096 + 128 * g, 128, OFF_XBC + GROUP_W * g + 512) for g in range(4)]
             + [(4608 + 128 * g, 128, OFF_XBC + GROUP_W * g + 640) for g in range(4)]
             + [(5120, 32, OFF_DT), (5152, 1024, OFF_Q), (6176, 256, OFF_K), (6432, 256, OFF_V),
                (6688, 1024, OFF_GS), (7712, 1024, OFF_GA)])
VMEM_LIMIT_MB = 48
NEG = -1e30

WEIGHTS = ('norm_mix_pre_w', 'w_in', 'ssd_conv_w', 'ssd_conv_b', 'ssd_dt_bias', 'ssd_a_log', 'ssd_d', 'ssd_norm_w',
           'ssd_w_out', 'attn_sinks', 'attn_w_out', 'w_mix_out', 'norm_mix_post_w', 'norm_ffn_pre_w', 'ffn_w_up',
           'ffn_conv_w', 'ffn_conv_b', 'ffn_w_down', 'norm_ffn_post_w')
SHARDED = (('w_in', 1, (1024, 2184)), ('ssd_conv_w', 1, (4, 768)), ('ssd_w_out', 0, (512, 1024)),
           ('attn_w_out', 0, (256, 1024)), ('w_mix_out', 0, (256, 1024)), ('ffn_w_up', 1, (1024, 1408)),
           ('ffn_conv_w', 1, (3, 1408)), ('ffn_w_down', 0, (704, 1024)))
MATMUL_WEIGHTS = ('w_in', 'ssd_w_out', 'attn_w_out', 'w_mix_out', 'ffn_w_up', 'ffn_w_down')
REPLICATED = tuple(n for n in WEIGHTS if n not in {s[0] for s in SHARDED})
N_CHIPS = 4
COMM_LANES = 1024


def _cp(vmem_mb=VMEM_LIMIT_MB, **kw):
    return pltpu.CompilerParams(vmem_limit_bytes=vmem_mb << 20, **kw)


class _Ride(NamedTuple):
    ins: tuple
    outs: tuple
    n_sems: int
    start: Callable
    finish: Callable


def _ride_parts(ride):
    if ride is None:
        return [], [], [], [], []
    hbm = pl.BlockSpec(memory_space=pl.ANY)
    return (list(ride.ins), [hbm] * len(ride.ins), list(ride.outs), [hbm] * len(ride.outs),
            [pltpu.SemaphoreType.DMA((ride.n_sems,)), pltpu.SemaphoreType.DMA((ride.n_sems,))])


def _ride_run(ride, first, last, in_refs, out_refs, sems):
    if ride is None:
        return

    @pl.when(first)
    def _():
        ride.start(in_refs, out_refs, *sems)

    @pl.when(last)
    def _():
        ride.finish(in_refs, out_refs, *sems)


def _iota(shape, axis):
    return lax.broadcasted_iota(jnp.int32, shape, axis)


def _sigmoid(v):
    return 1.0 / (1.0 + jnp.exp(-v))


def _mm_call(name, a, b, *, tm, tn, tk, epilogue, outs, extra_in=(), trans_a=False, fill=None, ride=None):
    if trans_a:
        kdim, m = a.shape
    else:
        m, kdim = a.shape
    n = b.shape[1]
    assert b.shape[0] == kdim and m % tm == 0 and n % tn == 0 and kdim % tk == 0, (name, a.shape, b.shape, tm, tn, tk)
    gi, gj, gk = m // tm, n // tn, kdim // tk
    n_in, n_out = len(extra_in), len(outs)

    n_fill = 0 if fill is None else 1
    r_ops, r_in_specs, r_outs, r_out_specs, r_scratch = _ride_parts(ride)

    def body(a_ref, b_ref, *rest):
        ins = rest[:n_in]
        rest = rest[n_in + n_fill:]
        r_in, rest = rest[:len(r_ops)], rest[len(r_ops):]
        out_refs, rest = rest[:n_out], rest[n_out:]
        r_out, scratch = rest[:len(r_outs)], rest[len(r_outs):]
        i, j, k = pl.program_id(0), pl.program_id(1), pl.program_id(2)
        _ride_run(ride, (i == 0) & (j == 0) & (k == 0), (i == gi - 1) & (j == gj - 1) & (k == gk - 1),
                  r_in, r_out, scratch[-2:])
        av = a_ref[...].astype(BF16)
        bv = b_ref[...].astype(BF16)
        if trans_a:
            part = lax.dot_general(av, bv, (((0,), (0,)), ((), ())), preferred_element_type=F32)
        else:
            part = jnp.dot(av, bv, preferred_element_type=F32)
        if gk == 1:
            epilogue(part, i, j, ins, out_refs)
        else:
            acc = scratch[0]

            @pl.when(k == 0)
            def _():
                acc[...] = part

            @pl.when(k > 0)
            def _():
                acc[...] += part

            @pl.when(k == gk - 1)
            def _():
                epilogue(acc[...], i, j, ins, out_refs)

    a_spec = pl.BlockSpec((tk, tm), lambda i, j, k: (k, i)) if trans_a else pl.BlockSpec((tm, tk), lambda i, j, k: (i, k))
    in_specs = [a_spec, pl.BlockSpec((tk, tn), lambda i, j, k: (k, j))]
    in_specs += [pl.BlockSpec(bs, im) for _, bs, im in extra_in]
    operands = [a, b] + [e[0] for e in extra_in]
    aliases = {}
    if fill is not None:
        in_specs.append(pl.BlockSpec(memory_space=pl.ANY))
        aliases = {len(operands): fill[1]}
        operands.append(fill[0])
    return pl.pallas_call(
        body, name=name, grid=(gi, gj, gk), in_specs=in_specs + r_in_specs,
        out_specs=[pl.BlockSpec(bs, im) for _, _, bs, im in outs] + r_out_specs,
        out_shape=[SDS(s, d) for s, d, _, _ in outs] + r_outs,
        scratch_shapes=([pltpu.VMEM((tm, tn), F32)] if gk > 1 else []) + r_scratch,
        input_output_aliases=aliases,
        compiler_params=_cp(dimension_semantics=("arbitrary", "arbitrary", "arbitrary")),
    )(*operands, *r_ops)


def _mm_plain(name, a, b, *, tm, tn, tk, out_dtype=F32, trans_a=False):
    m = a.shape[1] if trans_a else a.shape[0]

    def epilogue(acc, i, j, ins, outs):
        outs[0][...] = acc.astype(out_dtype)

    return _mm_call(name, a, b, tm=tm, tn=tn, tk=tk, epilogue=epilogue, trans_a=trans_a,
                    outs=[((m, b.shape[1]), out_dtype, (tm, tn), lambda i, j, k: (i, j))])[0]


def _accumulate(ref, first, value):
    @pl.when(first)
    def _():
        ref[...] = value

    @pl.when(jnp.logical_not(first))
    def _():
        ref[...] += value


def _rms_bwd(xv, w, dy):
    r = lax.rsqrt(jnp.mean(xv * xv, axis=-1, keepdims=True) + NORM_EPS)
    xn = xv * r
    dxh = dy * w
    dx = r * (dxh - xn * jnp.mean(dxh * xn, axis=-1, keepdims=True))
    return dx, jnp.sum(dy * xn, axis=0, keepdims=True)


def _norm_mm(name, x, wn, w, *, tm, tn, ride=None):
    t, dm = x.shape
    n = w.shape[1]
    tm = min(tm, t)
    gi, gj = t // tm, n // tn
    r_ops, r_in_specs, r_outs, r_out_specs, r_scratch = _ride_parts(ride)

    def body(x_ref, wn_ref, w_ref, *rest):
        r_in, rest = rest[:len(r_ops)], rest[len(r_ops):]
        o_ref, u_ref = rest[:2]
        r_out, sems = rest[2:2 + len(r_outs)], rest[2 + len(r_outs):]
        i, j = pl.program_id(0), pl.program_id(1)
        _ride_run(ride, (i == 0) & (j == 0), (i == gi - 1) & (j == gj - 1), r_in, r_out, sems)

        @pl.when(j == 0)
        def _():
            xv = x_ref[...]
            r = lax.rsqrt(jnp.mean(xv * xv, axis=-1, keepdims=True) + NORM_EPS)
            u_ref[...] = (xv * r * wn_ref[...]).astype(BF16)

        o_ref[...] = jnp.dot(u_ref[...], w_ref[...], preferred_element_type=F32)

    return pl.pallas_call(
        body, name=name, grid=(gi, gj),
        in_specs=[pl.BlockSpec((tm, dm), lambda i, j: (i, 0)), pl.BlockSpec((1, dm), lambda i, j: (0, 0)),
                  pl.BlockSpec((dm, tn), lambda i, j: (0, j))] + r_in_specs,
        out_specs=[pl.BlockSpec((tm, tn), lambda i, j: (i, j)), pl.BlockSpec((tm, dm), lambda i, j: (i, 0))] + r_out_specs,
        out_shape=[SDS((t, n), F32), SDS((t, dm), BF16)] + r_outs, scratch_shapes=r_scratch,
        compiler_params=_cp(dimension_semantics=("arbitrary", "arbitrary")),
    )(x, wn, w, *r_ops)


def _shift_down(tile, halo, s):
    if s == 0:
        return tile
    r = pltpu.roll(tile, s, axis=0)
    h = pltpu.roll(halo, s, axis=0)
    head = jnp.where(_iota(h.shape, 0) < s, h, r[0:8])
    return jnp.concatenate([head, r[8:]], axis=0)


def _shift_up(tile, halo, s):
    if s == 0:
        return tile
    n = tile.shape[0]
    r = pltpu.roll(tile, n - s, axis=0)
    h = pltpu.roll(halo, 8 - s, axis=0)
    tail = jnp.where(_iota(h.shape, 0) >= 8 - s, h, r[n - 8:])
    return jnp.concatenate([r[:n - 8], tail], axis=0)


def _conv_apply(tile, halo, wv, bv, kw):
    acc = bv + wv[kw - 1:kw, :] * tile
    for k in range(kw - 1):
        acc = acc + wv[k:k + 1, :] * _shift_down(tile, halo, kw - 1 - k)
    return acc


def _prev_halo_spec(tm, tc, col0):
    return pl.BlockSpec((8, tc), lambda i, j: (jnp.maximum(i * (tm // 8) - 1, 0), col0 + j))


def _silu_parts(pre):
    sg = _sigmoid(pre)
    return pre * sg, sg * (1.0 + pre * (1.0 - sg))


def _conv_silu_fwd(proj, w, b, *, tm, tc=512):
    t = proj.shape[0]
    c = w.shape[1]
    tm = min(tm, t)
    col0 = OFF_XBC // tc

    def body(x_ref, h_ref, w_ref, b_ref, o_ref):
        halo = jnp.where(pl.program_id(0) > 0, h_ref[...], 0.0)
        o_ref[...] = _silu_parts(_conv_apply(x_ref[...], halo, w_ref[...], b_ref[...], 4))[0]

    return pl.pallas_call(
        body, name="ssd_conv_fwd", grid=(t // tm, c // tc),
        in_specs=[pl.BlockSpec((tm, tc), lambda i, j: (i, col0 + j)), _prev_halo_spec(tm, tc, col0),
                  pl.BlockSpec((4, tc), lambda i, j: (0, j)), pl.BlockSpec((1, tc), lambda i, j: (0, j))],
        out_specs=pl.BlockSpec((tm, tc), lambda i, j: (i, j)),
        out_shape=SDS((t, c), F32),
        compiler_params=_cp(dimension_semantics=("arbitrary", "arbitrary")),
    )(proj, proj, w, b)


def _conv_silu_bwd1(d_out, proj, w, b, *, tm, tc=512):
    t = proj.shape[0]
    c = w.shape[1]
    tm = min(tm, t)
    col0 = OFF_XBC // tc

    def body(g_ref, x_ref, h_ref, w_ref, b_ref, o_ref, db_ref):
        i = pl.program_id(1)
        halo = jnp.where(i > 0, h_ref[...], 0.0)
        d_pre = g_ref[...] * _silu_parts(_conv_apply(x_ref[...], halo, w_ref[...], b_ref[...], 4))[1]
        o_ref[...] = d_pre.astype(BF16)
        _accumulate(db_ref, i == 0, jnp.sum(d_pre, axis=0, keepdims=True))

    return pl.pallas_call(
        body, name="ssd_conv_bwd1", grid=(c // tc, t // tm),
        in_specs=[pl.BlockSpec((tm, tc), lambda j, i: (i, j)), pl.BlockSpec((tm, tc), lambda j, i: (i, col0 + j)),
                  pl.BlockSpec((8, tc), lambda j, i: (jnp.maximum(i * (tm // 8) - 1, 0), col0 + j)),
                  pl.BlockSpec((4, tc), lambda j, i: (0, j)), pl.BlockSpec((1, tc), lambda j, i: (0, j))],
        out_specs=[pl.BlockSpec((tm, tc), lambda j, i: (i, j)), pl.BlockSpec((1, tc), lambda j, i: (0, j))],
        out_shape=[SDS((t, c), BF16), SDS((1, c), F32)],
        compiler_params=_cp(dimension_semantics=("arbitrary", "arbitrary")),
    )(d_out, proj, proj, w, b)


def _conv_bwd2(name, d_pre, src, src_col0, w, *, tm, tc, out_cols, out_col0, fill=None):
    t, c = d_pre.shape
    kw = w.shape[0]
    tm = min(tm, t)
    ni = t // tm
    col0 = src_col0 // tc
    ocol0 = out_col0 // tc

    def body(g_ref, gn_ref, x_ref, xh_ref, w_ref, *rest):
        o_ref, dw_ref = rest[-2:]
        i = pl.program_id(1)
        g = g_ref[...].astype(F32)
        g_next = jnp.where(i < ni - 1, gn_ref[...].astype(F32)[0:8], 0.0)
        x_prev = jnp.where(i > 0, xh_ref[...], 0.0)
        xv = x_ref[...]
        wv = w_ref[...]
        d_in = wv[kw - 1:kw, :] * g
        for k in range(kw - 1):
            d_in = d_in + wv[k:k + 1, :] * _shift_up(g, g_next, kw - 1 - k)
        o_ref[...] = d_in.astype(o_ref.dtype)
        rows = [jnp.sum(g * _shift_down(xv, x_prev, kw - 1 - k), axis=0, keepdims=True) for k in range(kw)]

        @pl.when(i == 0)
        def _():
            for k in range(kw):
                dw_ref[k:k + 1, :] = rows[k]

        @pl.when(i > 0)
        def _():
            for k in range(kw):
                dw_ref[k:k + 1, :] += rows[k]

    in_specs = [pl.BlockSpec((tm, tc), lambda j, i: (i, j)),
                pl.BlockSpec((16, tc), lambda j, i: (jnp.minimum((i + 1) * (tm // 16), t // 16 - 1), j)),
                pl.BlockSpec((tm, tc), lambda j, i: (i, col0 + j)),
                pl.BlockSpec((8, tc), lambda j, i: (jnp.maximum(i * (tm // 8) - 1, 0), col0 + j)),
                pl.BlockSpec((kw, tc), lambda j, i: (0, j))]
    operands = [d_pre, d_pre, src, src, w]
    if fill is not None:
        in_specs.append(pl.BlockSpec(memory_space=pl.ANY))
        operands.append(fill)
    return pl.pallas_call(
        body, name=name, grid=(c // tc, ni), in_specs=in_specs,
        out_specs=[pl.BlockSpec((tm, tc), lambda j, i: (i, ocol0 + j)), pl.BlockSpec((kw, tc), lambda j, i: (0, j))],
        out_shape=[SDS((t, out_cols), BF16), SDS((kw, c), F32)],
        input_output_aliases={} if fill is None else {5: 0},
        compiler_params=_cp(dimension_semantics=("arbitrary", "arbitrary")),
    )(*operands)


GELU_C = 0.7978845608028654


def _gelu_parts(v):
    inner = GELU_C * (v + 0.044715 * v * v * v)
    th = jnp.tanh(inner)
    val = 0.5 * v * (1.0 + th)
    grad = 0.5 * (1.0 + th) + 0.5 * v * (1.0 - th * th) * GELU_C * (1.0 + 3.0 * 0.044715 * v * v)
    return val, grad


def _ffn_act_specs(tm, tc, nj, order):
    def im(f):
        return (lambda i, j: f(i, j)) if order == "ij" else (lambda j, i: f(i, j))
    halo = lambda i: jnp.maximum(i * (tm // 8) - 1, 0)
    return [pl.BlockSpec((tm, tc), im(lambda i, j: (i, j))), pl.BlockSpec((8, tc), im(lambda i, j: (halo(i), j))),
            pl.BlockSpec((tm, tc), im(lambda i, j: (i, nj + j))), pl.BlockSpec((8, tc), im(lambda i, j: (halo(i), nj + j))),
            pl.BlockSpec((3, tc), im(lambda i, j: (0, j))), pl.BlockSpec((3, tc), im(lambda i, j: (0, nj + j))),
            pl.BlockSpec((1, tc), im(lambda i, j: (0, j))), pl.BlockSpec((1, tc), im(lambda i, j: (0, nj + j)))]


def _ffn_act_fwd(up_raw, w, b, *, tm, tc=1408):
    t = up_raw.shape[0]
    tm = min(tm, t)
    nj = FFN_D_FF // tc

    def body(g_ref, gh_ref, v_ref, vh_ref, wg_ref, wv_ref, bg_ref, bv_ref, o_ref):
        first = pl.program_id(0) > 0
        gate = _conv_apply(g_ref[...], jnp.where(first, gh_ref[...], 0.0), wg_ref[...], bg_ref[...], 3)
        val = _conv_apply(v_ref[...], jnp.where(first, vh_ref[...], 0.0), wv_ref[...], bv_ref[...], 3)
        o_ref[...] = (_gelu_parts(gate)[0] * val).astype(BF16)

    return pl.pallas_call(
        body, name="ffn_act_fwd", grid=(t // tm, nj), in_specs=_ffn_act_specs(tm, tc, nj, "ij"),
        out_specs=pl.BlockSpec((tm, tc), lambda i, j: (i, j)), out_shape=SDS((t, FFN_D_FF), BF16),
        compiler_params=_cp(dimension_semantics=("arbitrary", "arbitrary")),
    )(up_raw, up_raw, up_raw, up_raw, w, w, b, b)


def _ffn_act_bwd(up_raw, d_act, w, b, *, tm, tc=1408):
    t = up_raw.shape[0]
    tm = min(tm, t)
    nj = FFN_D_FF // tc

    def body(g_ref, gh_ref, v_ref, vh_ref, wg_ref, wv_ref, bg_ref, bv_ref, da_ref, dg_ref, dv_ref, dbg_ref, dbv_ref):
        i = pl.program_id(1)
        gate = _conv_apply(g_ref[...], jnp.where(i > 0, gh_ref[...], 0.0), wg_ref[...], bg_ref[...], 3)
        val = _conv_apply(v_ref[...], jnp.where(i > 0, vh_ref[...], 0.0), wv_ref[...], bv_ref[...], 3)
        ge, dge = _gelu_parts(gate)
        da = da_ref[...].astype(F32)
        d_gate = da * val * dge
        d_val = da * ge
        dg_ref[...] = d_gate.astype(BF16)
        dv_ref[...] = d_val.astype(BF16)
        _accumulate(dbg_ref, i == 0, jnp.sum(d_gate, axis=0, keepdims=True))
        _accumulate(dbv_ref, i == 0, jnp.sum(d_val, axis=0, keepdims=True))

    tile = pl.BlockSpec((tm, tc), lambda j, i: (i, j))
    row = pl.BlockSpec((1, tc), lambda j, i: (0, j))
    return pl.pallas_call(
        body, name="ffn_act_bwd", grid=(nj, t // tm), in_specs=_ffn_act_specs(tm, tc, nj, "ji") + [tile],
        out_specs=[tile, tile, row, row],
        out_shape=[SDS((t, FFN_D_FF), BF16), SDS((t, FFN_D_FF), BF16), SDS((1, FFN_D_FF), F32), SDS((1, FFN_D_FF), F32)],
        compiler_params=_cp(dimension_semantics=("arbitrary", "arbitrary")),
    )(up_raw, up_raw, up_raw, up_raw, w, w, b, b, d_act)


def _softplus(v):
    e = jnp.exp(-jnp.abs(v))
    small = e * (1.0 - 0.5 * e)
    return jnp.maximum(v, 0.0) + jnp.where(e < 1e-4, small, jnp.log(1.0 + e))


def _dt_fwd(proj, bias_pad, *, tm):
    t = proj.shape[0]
    tm = min(tm, t)

    def body(x_ref, b_ref, o_ref):
        o_ref[...] = _softplus(x_ref[...] + b_ref[...])

    return pl.pallas_call(
        body, name="dt_fwd", grid=(t // tm,),
        in_specs=[pl.BlockSpec((tm, 128), lambda i: (i, OFF_DT // 128)), pl.BlockSpec((1, 128), lambda i: (0, 0))],
        out_specs=pl.BlockSpec((tm, 128), lambda i: (i, 0)), out_shape=SDS((t, 128), F32),
        compiler_params=_cp(dimension_semantics=("arbitrary",)),
    )(proj, bias_pad)


def _dt_bwd(d_dt, proj, bias_pad, d_proj, *, tm):
    t = proj.shape[0]
    tm = min(tm, t)

    def body(g_ref, x_ref, b_ref, _, o_ref, db_ref):
        d_raw = g_ref[...] * _sigmoid(x_ref[...] + b_ref[...])
        o_ref[:, 0:128] = d_raw.astype(BF16)
        o_ref[:, 128:512] = jnp.zeros((tm, 384), BF16)
        _accumulate(db_ref, pl.program_id(0) == 0, jnp.sum(d_raw, axis=0, keepdims=True))

    return pl.pallas_call(
        body, name="dt_bwd", grid=(t // tm,),
        in_specs=[pl.BlockSpec((tm, 128), lambda i: (i, 0)), pl.BlockSpec((tm, 128), lambda i: (i, OFF_DT // 128)),
                  pl.BlockSpec((1, 128), lambda i: (0, 0)), pl.BlockSpec(memory_space=pl.ANY)],
        out_specs=[pl.BlockSpec((tm, 512), lambda i: (i, OFF_DT // 512)), pl.BlockSpec((1, 128), lambda i: (0, 0))],
        out_shape=[SDS((t, PROJ_W), BF16), SDS((1, 128), F32)],
        input_output_aliases={3: 0},
        compiler_params=_cp(dimension_semantics=("arbitrary",)),
    )(d_dt, proj, bias_pad, d_proj)


def _split3(v):
    hi = v.astype(BF16)
    r1 = v - hi.astype(F32)
    mid = r1.astype(BF16)
    return hi, mid, (r1 - mid.astype(F32)).astype(BF16)


def _times01(v, m3):
    return jnp.dot(jnp.concatenate(_split3(v), axis=1), m3, preferred_element_type=F32)


def _01times(m3, v):
    return jnp.dot(m3, jnp.concatenate(_split3(v), axis=0), preferred_element_type=F32)


def _ssd_decay(dt_ref, dtT_ref, al_ref, alT_ref, k):
    dt = dt_ref[0]
    a_row = -jnp.exp(al_ref[0])
    adt_t = dtT_ref[0] * (-jnp.exp(alT_ref[0]))
    return dt, a_row, _01times(k['low3'][...], dt * a_row), _times01(adt_t, k['up3v'][...])


def _ssd_specs(nc, rev):
    ci = (lambda c: nc - 1 - c) if rev else (lambda c: c)
    return [pl.BlockSpec((CHUNK, GROUP_W), lambda c, g: (ci(c), g)),
            pl.BlockSpec((1, CHUNK, 128), lambda c, g: (g, ci(c), 0)),
            pl.BlockSpec((1, 8, CHUNK), lambda c, g: (g, 0, ci(c))),
            pl.BlockSpec((1, 1, 128), lambda c, g: (g, 0, 0)),
            pl.BlockSpec((1, 8, 1), lambda c, g: (g, 0, 0)),
            pl.BlockSpec((1, 512), lambda c, g: (0, g))]


NT = (((1,), (1,)), ((), ()))
WIDE = 8 * CHUNK
SSD_CONST_NAMES = ('e128', 'e64', 's64', 'mlo', 'mup', 'low3', 'up3', 'up3v')
SSD_CONST_SHAPES = [pltpu.VMEM((3 * CHUNK, WIDE), BF16), pltpu.VMEM((3 * CHUNK, 512), BF16), pltpu.VMEM((512, CHUNK), BF16),
                    pltpu.VMEM((CHUNK, WIDE), F32), pltpu.VMEM((CHUNK, WIDE), F32), pltpu.VMEM((CHUNK, 3 * CHUNK), BF16),
                    pltpu.VMEM((CHUNK, 3 * CHUNK), BF16), pltpu.VMEM((3 * CHUNK, CHUNK), BF16)]


def _ssd_init_consts(k):
    row, col = _iota((3 * CHUNK, WIDE), 0), _iota((3 * CHUNK, WIDE), 1)
    k['e128'][...] = ((col >> 7) == (row & 127)).astype(BF16)
    k['e64'][...] = ((_iota((3 * CHUNK, 512), 1) >> 6) == (_iota((3 * CHUNK, 512), 0) & 127)).astype(BF16)
    k['s64'][...] = ((_iota((512, CHUNK), 0) >> 6) == _iota((512, CHUNK), 1)).astype(BF16)
    row, col = _iota((CHUNK, WIDE), 0), _iota((CHUNK, WIDE), 1)
    k['mlo'][...] = (row >= (col & 127)).astype(F32)
    k['mup'][...] = (row <= (col & 127)).astype(F32)
    row, col = _iota((CHUNK, 3 * CHUNK), 0), _iota((CHUNK, 3 * CHUNK), 1) & 127
    k['low3'][...] = (row >= col).astype(BF16)
    k['up3'][...] = (row <= col).astype(BF16)
    row, col = _iota((3 * CHUNK, CHUNK), 0) & 127, _iota((3 * CHUNK, CHUNK), 1)
    k['up3v'][...] = (row <= col).astype(BF16)


def _ssd_common(x_ref, dt_ref, dtT_ref, al_ref, alT_ref, k):
    dt, a_row, acs, acs_t = _ssd_decay(dt_ref, dtT_ref, al_ref, alT_ref, k)
    ecol = _times01(acs, k['e128'][...])
    rrow = jnp.concatenate([jnp.broadcast_to(acs_t[j:j + 1, :], (CHUNK, CHUNK)) for j in range(8)], axis=1)
    a64 = _times01(acs, k['e64'][...])
    dt64 = _times01(dt, k['e64'][...])
    a_end64 = a64[CHUNK - 1:CHUNK, :]
    xs = x_ref[:, 0:512]
    return dict(dt=dt, a_row=a_row, acs=acs, seg=ecol - rrow, dt64=dt64, e_a=jnp.exp(a64), decay=jnp.exp(a_end64 - a64),
                e_end64=jnp.exp(a_end64), xs=xs, xdt=xs * dt64, bm=x_ref[:, 512:640], cm=x_ref[:, 640:768])


def _pair_blocks(v):
    lo = _iota((CHUNK, 128), 1) < 64
    out = []
    for i in range(4):
        ch = v[:, i * 128:(i + 1) * 128]
        out.append(jnp.concatenate([jnp.where(lo, ch, 0.0), jnp.where(lo, 0.0, ch)], axis=0).astype(BF16))
    return out


def _tile8(m):
    return jnp.concatenate([m] * 8, axis=1)


def _ssd_fwd(xc, dtg, dtg_t, alog, alog_t, d_exp):
    t = xc.shape[0]
    nc = t // CHUNK

    def body(x_ref, dt_ref, dtT_ref, al_ref, alT_ref, d_ref, y_ref, hs_ref, h_scr, *consts):
        c, g = pl.program_id(0), pl.program_id(1)
        k = dict(zip(SSD_CONST_NAMES, consts))

        @pl.when(jnp.logical_and(c == 0, g == 0))
        def _():
            _ssd_init_consts(k)

        @pl.when(c == 0)
        def _():
            h_scr[g] = jnp.zeros((SSD_D_STATE, 512), F32)

        v = _ssd_common(x_ref, dt_ref, dtT_ref, al_ref, alT_ref, k)
        b16, c16 = v['bm'].astype(BF16), v['cm'].astype(BF16)
        cb = lax.dot_general(c16, b16, NT, preferred_element_type=F32)
        m16 = (jnp.exp(jnp.minimum(v['seg'], 0.0)) * k['mlo'][...] * _tile8(cb)).astype(BF16)
        xbd = _pair_blocks(v['xdt'])
        y_diag = jnp.concatenate([jnp.dot(m16[:, i * 256:(i + 1) * 256], xbd[i], preferred_element_type=F32)
                                  for i in range(4)], axis=1)
        ht = h_scr[g]
        y_off = jnp.dot(c16, ht.astype(BF16), preferred_element_type=F32)
        y_ref[...] = y_diag + v['e_a'] * y_off + d_ref[...] * v['xs']
        st = jnp.dot(v['bm'].T.astype(BF16), (v['xdt'] * v['decay']).astype(BF16), preferred_element_type=F32)
        hs_ref[0, 0] = ht
        h_scr[g] = ht * v['e_end64'] + st

    return pl.pallas_call(
        body, name="ssd_fwd", grid=(nc, SSD_N_GROUPS), in_specs=_ssd_specs(nc, False),
        out_specs=[pl.BlockSpec((CHUNK, 512), lambda c, g: (c, g)),
                   pl.BlockSpec((1, 1, SSD_D_STATE, 512), lambda c, g: (c, g, 0, 0))],
        out_shape=[SDS((t, SSD_D_INNER), F32), SDS((nc, SSD_N_GROUPS, SSD_D_STATE, 512), F32)],
        scratch_shapes=[pltpu.VMEM((SSD_N_GROUPS, SSD_D_STATE, 512), F32)] + SSD_CONST_SHAPES,
        compiler_params=_cp(dimension_semantics=("arbitrary", "arbitrary")),
    )(xc, dtg, dtg_t, alog, alog_t, d_exp)


def _ssd_bwd(xc, dtg, dtg_t, alog, alog_t, d_exp, d_y, hs, ride=None):
    t = xc.shape[0]
    nc = t // CHUNK

    r_ops, r_in_specs, r_outs, r_out_specs, r_scratch = _ride_parts(ride)

    def body(x_ref, dt_ref, dtT_ref, al_ref, alT_ref, d_ref, dy_ref, hs_ref, *rest):
        r_in, rest = rest[:len(r_ops)], rest[len(r_ops):]
        dx_ref, ddt_ref, dal_ref, dd_ref = rest[:4]
        r_out, rest = rest[4:4 + len(r_outs)], rest[4 + len(r_outs):]
        g_scr, consts, sems = rest[0], rest[1:1 + len(SSD_CONST_NAMES)], rest[1 + len(SSD_CONST_NAMES):]
        c, g = pl.program_id(0), pl.program_id(1)
        _ride_run(ride, (c == 0) & (g == 0), (c == nc - 1) & (g == SSD_N_GROUPS - 1), r_in, r_out, sems)
        k = dict(zip(SSD_CONST_NAMES, consts))
        s64, mlo, mup = k['s64'], k['mlo'], k['mup']

        @pl.when(jnp.logical_and(c == 0, g == 0))
        def _():
            _ssd_init_consts(k)

        @pl.when(c == 0)
        def _():
            g_scr[g] = jnp.zeros((SSD_D_STATE, 512), F32)

        v = _ssd_common(x_ref, dt_ref, dtT_ref, al_ref, alT_ref, k)
        dt, a_row, xs, xdt, e_a, decay = v['dt'], v['a_row'], v['xs'], v['xdt'], v['e_a'], v['decay']
        row, col = _iota((CHUNK, CHUNK), 0), _iota((CHUNK, CHUNK), 1)
        b16, c16 = v['bm'].astype(BF16), v['cm'].astype(BF16)
        ct16 = v['cm'].T.astype(BF16)
        cb = lax.dot_general(c16, b16, NT, preferred_element_type=F32)
        cbt = lax.dot_general(b16, c16, NT, preferred_element_type=F32)
        lmat = jnp.exp(jnp.minimum(v['seg'], 0.0)) * mlo[...]
        lmat_t = jnp.exp(jnp.minimum(-v['seg'], 0.0)) * mup[...]
        mmat, mmat_t = lmat * _tile8(cb), lmat_t * _tile8(cbt)
        mt16 = mmat_t.astype(BF16)
        dy = dy_ref[...]
        dye, xdec = dy * e_a, xdt * decay
        dy16, dye16, xdec16 = dy.astype(BF16), dye.astype(BF16), xdec.astype(BF16)
        xdt16 = xdt.astype(BF16)
        ht, gt = hs_ref[0, 0], g_scr[g]
        ht16, gt16 = ht.astype(BF16), gt.astype(BF16)
        xbd, dybd = _pair_blocks(xdt), _pair_blocks(dy)
        d_m, d_mt, d_x = [], [], []
        for i in range(4):
            csl = slice(i * 128, (i + 1) * 128)
            d_m.append(lax.dot_general(dy16[:, csl], xbd[i], NT, preferred_element_type=F32))
            d_mt.append(lax.dot_general(xdt16[:, csl], dybd[i], NT, preferred_element_type=F32))
            d_x.append(jnp.dot(mt16[:, i * 256:(i + 1) * 256], dybd[i], preferred_element_type=F32))
        d_m, d_mt, d_x = jnp.concatenate(d_m, axis=1), jnp.concatenate(d_mt, axis=1), jnp.concatenate(d_x, axis=1)

        def head_sum(m):
            acc = m[:, 0:CHUNK]
            for j in range(1, 8):
                acc = acc + m[:, j * CHUNK:(j + 1) * CHUNK]
            return acc

        def seg64(p):
            return jnp.dot(p.astype(BF16), s64[...], preferred_element_type=F32)

        d_cb16 = head_sum(d_m * lmat).astype(BF16)
        d_cbt16 = head_sum(d_mt * lmat_t).astype(BF16)
        dseg = d_m * mmat - d_mt * mmat_t
        da_seg = jnp.zeros((CHUNK, CHUNK), F32)
        for j in range(8):
            da_seg = jnp.where(col == j, jnp.sum(dseg[:, j * CHUNK:(j + 1) * CHUNK], axis=1, keepdims=True), da_seg)
        ch = jnp.dot(c16, ht16, preferred_element_type=F32)
        bg = jnp.dot(b16, gt16, preferred_element_type=F32)
        d_x = d_x + decay * bg
        d_decay = seg64(xdec * bg)
        e_end = jnp.exp(v['acs'][CHUNK - 1:CHUNK, :])
        d_end = e_end * jnp.sum(seg64(gt * ht), axis=0, keepdims=True) + jnp.sum(d_decay, axis=0, keepdims=True)
        d_a = seg64(dye * ch) - d_decay + da_seg + jnp.where(row == CHUNK - 1, d_end, 0.0)
        dx_ref[:, 0:512] = d_x * v['dt64'] + d_ref[...] * dy
        dx_ref[:, 640:768] = (lax.dot_general(dye16, ht16, NT, preferred_element_type=F32)
                              + jnp.dot(d_cb16, b16, preferred_element_type=F32))
        dx_ref[:, 512:640] = (lax.dot_general(xdec16, gt16, NT, preferred_element_type=F32)
                              + jnp.dot(d_cbt16, c16, preferred_element_type=F32))
        g_scr[g] = gt * v['e_end64'] + jnp.dot(ct16, dye16, preferred_element_type=F32)
        d_adt = _01times(k['up3'][...], d_a)
        ddt_ref[0] = d_adt * a_row + seg64(d_x * xs)
        d_alog = jnp.sum(d_adt * dt, axis=0, keepdims=True) * a_row
        dd_row = jnp.sum(seg64(dy * xs), axis=0, keepdims=True)
        first = c == 0

        @pl.when(first)
        def _():
            dal_ref[g] = d_alog
            dd_ref[g] = dd_row

        @pl.when(jnp.logical_not(first))
        def _():
            dal_ref[g] += d_alog
            dd_ref[g] += dd_row

    rc = lambda c: nc - 1 - c
    whole = pl.BlockSpec((SSD_N_GROUPS, 1, 128), lambda c, g: (0, 0, 0))
    return pl.pallas_call(
        body, name="ssd_bwd", grid=(nc, SSD_N_GROUPS),
        in_specs=_ssd_specs(nc, True) + [pl.BlockSpec((CHUNK, 512), lambda c, g: (rc(c), g)),
                                        pl.BlockSpec((1, 1, SSD_D_STATE, 512), lambda c, g: (rc(c), g, 0, 0))] + r_in_specs,
        out_specs=[pl.BlockSpec((CHUNK, GROUP_W), lambda c, g: (rc(c), g)),
                   pl.BlockSpec((1, CHUNK, 128), lambda c, g: (g, rc(c), 0)), whole, whole] + r_out_specs,
        out_shape=[SDS((t, SSD_CONV_DIM), F32), SDS((SSD_N_GROUPS, t, 128), F32),
                   SDS((SSD_N_GROUPS, 1, 128), F32), SDS((SSD_N_GROUPS, 1, 128), F32)] + r_outs,
        scratch_shapes=[pltpu.VMEM((SSD_N_GROUPS, SSD_D_STATE, 512), F32)] + SSD_CONST_SHAPES + r_scratch,
        compiler_params=_cp(dimension_semantics=("arbitrary", "arbitrary")),
    )(xc, dtg, dtg_t, alog, alog_t, d_exp, d_y, hs, *r_ops)


def _gated_norm_fwd(y, proj, w, *, tm):
    t = y.shape[0]
    tm = min(tm, t)

    def body(y_ref, z_ref, w_ref, o_ref):
        gv = y_ref[...] * _silu_parts(z_ref[...])[0]
        r = lax.rsqrt(jnp.mean(gv * gv, axis=-1, keepdims=True) + NORM_EPS)
        o_ref[...] = (gv * r * w_ref[...]).astype(BF16)

    tile = pl.BlockSpec((tm, 512), lambda i, g: (i, g))
    return pl.pallas_call(
        body, name="gated_norm_fwd", grid=(t // tm, SSD_N_GROUPS),
        in_specs=[tile, pl.BlockSpec((tm, 512), lambda i, g: (i, OFF_Z // 512 + g)),
                  pl.BlockSpec((1, 512), lambda i, g: (0, g))], out_specs=tile,
        out_shape=SDS((t, SSD_D_INNER), BF16),
        compiler_params=_cp(dimension_semantics=("arbitrary", "arbitrary")),
    )(y, proj, w)


def _rope(ch, cos_t, sin_t):
    first = (_iota(ch.shape, 1) & 32) == 0
    partner = jnp.where(first, pltpu.roll(ch, 96, axis=1), pltpu.roll(ch, 32, axis=1))
    return ch * cos_t + partner * sin_t


def _rope_qkv(proj, cos_t, sin_t, *, tm):
    t = proj.shape[0]
    tm = min(tm, t)

    def body(q_ref, k_ref, v_ref, c_ref, s_ref, qr_ref, kp_ref, vp_ref, kt_ref, vt_ref):
        cv, sv = c_ref[...], s_ref[...]
        lo = _iota((tm, 128), 1) < 64
        for m in range(8):
            sl = slice(m * 128, (m + 1) * 128)
            qr_ref[:, sl] = (_rope(q_ref[:, sl], cv, sv) * 0.125).astype(BF16)
        for m2 in range(2):
            sl = slice(m2 * 128, (m2 + 1) * 128)
            for src, dst, dst_t in ((_rope(k_ref[:, sl], cv, sv), kp_ref, kt_ref), (v_ref[:, sl], vp_ref, vt_ref)):
                sw = pltpu.roll(src, 64, axis=1)
                padded = (jnp.where(lo, src, 0.0), jnp.where(lo, 0.0, sw), jnp.where(lo, sw, 0.0), jnp.where(lo, 0.0, src))
                for i, pad in enumerate(padded):
                    rows = slice((4 * m2 + i) * 128, (4 * m2 + i + 1) * 128)
                    dst[:, rows] = pad.astype(BF16)
                    dst_t[rows, :] = pad.T.astype(BF16)

    return pl.pallas_call(
        body, name="rope_qkv", grid=(t // tm,),
        in_specs=[pl.BlockSpec((tm, 1024), lambda i: (i, OFF_Q // 1024)), pl.BlockSpec((tm, 256), lambda i: (i, OFF_K // 256)),
                  pl.BlockSpec((tm, 256), lambda i: (i, OFF_V // 256)), pl.BlockSpec((tm, 128), lambda i: (i, 0)),
                  pl.BlockSpec((tm, 128), lambda i: (i, 0))],
        out_specs=[pl.BlockSpec((tm, 1024), lambda i: (i, 0))] * 3 + [pl.BlockSpec((1024, tm), lambda i: (0, i))] * 2,
        out_shape=[SDS((t, 1024), BF16)] * 3 + [SDS((1024, t), BF16)] * 2,
        compiler_params=_cp(dimension_semantics=("arbitrary",)),
    )(proj, proj, proj, cos_t, sin_t)


def _attn_valid(n):
    kj, qi = _iota((2 * CHUNK, CHUNK), 0), _iota((2 * CHUNK, CHUNK), 1)
    return (kj > qi) & (kj <= qi + CHUNK) & ((n > 0) | (kj >= CHUNK))


def _attn_fwd(qr, kp, vt, sinks):
    t = qr.shape[0]
    nb = t // CHUNK

    def body(q_ref, kc_ref, kprev_ref, vc_ref, vprev_ref, sk_ref, o_ref, lse_ref):
        n = pl.program_id(0)
        valid = _attn_valid(n)
        head_row = _iota((16, CHUNK), 0)
        lse_all = jnp.zeros((16, CHUNK), F32)
        for m in range(8):
            g = m // 2
            qch = q_ref[:, m * 128:(m + 1) * 128]
            o_t = jnp.zeros((128, CHUNK), F32)
            for e in range(2):
                h = 2 * m + e
                sl = slice((2 * g + e) * 128, (2 * g + e + 1) * 128)
                kk = jnp.concatenate([kprev_ref[:, sl], kc_ref[:, sl]], axis=0)
                vv_t = jnp.concatenate([vprev_ref[sl, :], vc_ref[sl, :]], axis=1)
                s = jnp.where(valid, lax.dot_general(kk, qch, NT, preferred_element_type=F32), NEG)
                sink = sk_ref[0:1, h:h + 1]
                mx = jnp.maximum(jnp.max(s, axis=0, keepdims=True), sink)
                p = jnp.exp(s - mx)
                den = jnp.sum(p, axis=0, keepdims=True) + jnp.exp(sink - mx)
                o_t = o_t + jnp.dot(vv_t, p.astype(BF16), preferred_element_type=F32) * (1.0 / den)
                lse_all = jnp.where(head_row == h, mx + jnp.log(den), lse_all)
            o_ref[:, m * 128:(m + 1) * 128] = o_t.T.astype(BF16)
        lse_ref[0] = lse_all

    cur = pl.BlockSpec((CHUNK, 1024), lambda n: (n, 0))
    prev = pl.BlockSpec((CHUNK, 1024), lambda n: (jnp.maximum(n - 1, 0), 0))
    cur_t = pl.BlockSpec((1024, CHUNK), lambda n: (0, n))
    prev_t = pl.BlockSpec((1024, CHUNK), lambda n: (0, jnp.maximum(n - 1, 0)))
    return pl.pallas_call(
        body, name="attn_fwd", grid=(nb,),
        in_specs=[cur, cur, prev, cur_t, prev_t, pl.BlockSpec((1, 128), lambda n: (0, 0))],
        out_specs=[cur, pl.BlockSpec((1, 16, CHUNK), lambda n: (n, 0, 0))],
        out_shape=[SDS((t, 1024), BF16), SDS((nb, 16, CHUNK), F32)],
        compiler_params=_cp(dimension_semantics=("arbitrary",)),
    )(qr, kp, kp, vt, vt, sinks)


def _attn_bwd(qr, kp, vp, kt, d_o, o, lse, sinks, cos_t, sin_t, d_proj, ride=None):
    t = qr.shape[0]
    nb = t // CHUNK

    r_ops, r_in_specs, r_outs, r_out_specs, r_scratch = _ride_parts(ride)

    def body(q_ref, kc_ref, kprev_ref, vc_ref, vprev_ref, ktc_ref, ktprev_ref, do_ref, o_ref, lse_ref, sk_ref,
             c_ref, s_ref, cp_ref, sp_ref, _, *rest):
        r_in, rest = rest[:len(r_ops)], rest[len(r_ops):]
        dqkv_ref, dsk_ref = rest[:2]
        r_out, rest = rest[2:2 + len(r_outs)], rest[2 + len(r_outs):]
        acc_k, acc_v, dq_scr = rest[:3]
        n = pl.program_id(0)
        _ride_run(ride, n == 0, n == nb, r_in, r_out, rest[3:])
        lane = _iota((CHUNK, 128), 1)
        lo = lane < 64
        lane1 = _iota((1, 128), 1)

        @pl.when(n == 0)
        def _():
            acc_k[...] = jnp.zeros_like(acc_k)
            acc_v[...] = jnp.zeros_like(acc_v)
            dsk_ref[...] = jnp.zeros((1, 128), F32)

        @pl.when(n > 0)
        def _():
            dqkv_ref[:, 0:1024] = dq_scr[...]
            for r in range(8):
                acc_k[r, 0:CHUNK] = acc_k[r, CHUNK:2 * CHUNK]
                acc_v[r, 0:CHUNK] = acc_v[r, CHUNK:2 * CHUNK]
                acc_k[r, CHUNK:2 * CHUNK] = jnp.zeros((CHUNK, 128), F32)
                acc_v[r, CHUNK:2 * CHUNK] = jnp.zeros((CHUNK, 128), F32)

        @pl.when(n < nb)
        def _():
            valid = _attn_valid(n)
            lse_all = lse_ref[0]
            dsk = jnp.zeros((1, 128), F32)
            for m in range(8):
                g = m // 2
                csl = slice(m * 128, (m + 1) * 128)
                qch = q_ref[:, csl]
                doch = do_ref[:, csl]
                prod_t = (doch.astype(F32) * o_ref[:, csl].astype(F32)).T
                dq_t = jnp.zeros((128, CHUNK), F32)
                for e in range(2):
                    h = 2 * m + e
                    sl = slice((2 * g + e) * 128, (2 * g + e + 1) * 128)
                    kk = jnp.concatenate([kprev_ref[:, sl], kc_ref[:, sl]], axis=0)
                    vv = jnp.concatenate([vprev_ref[:, sl], vc_ref[:, sl]], axis=0)
                    kk_t = jnp.concatenate([ktprev_ref[sl, :], ktc_ref[sl, :]], axis=1)
                    lse_h = lse_all[h:h + 1, :]
                    s = lax.dot_general(kk, qch, NT, preferred_element_type=F32)
                    p = jnp.exp(jnp.where(valid, s, NEG) - lse_h)
                    delta = jnp.sum(prod_t[64 * e:64 * (e + 1)], axis=0, keepdims=True)
                    d_p = lax.dot_general(vv, doch, NT, preferred_element_type=F32)
                    d_s16 = (p * (d_p - delta)).astype(BF16)
                    dq_t = dq_t + jnp.dot(kk_t, d_s16, preferred_element_type=F32)
                    acc_k[2 * g + e] += jnp.dot(d_s16, qch, preferred_element_type=F32)
                    acc_v[2 * g + e] += jnp.dot(p.astype(BF16), doch, preferred_element_type=F32)
                    p_sink = jnp.exp(sk_ref[0:1, h:h + 1] - lse_h)
                    dsk = jnp.where(lane1 == h, -jnp.sum(p_sink * delta), dsk)
                dq_scr[:, csl] = (_rope(dq_t.T, c_ref[...], -s_ref[...]) * 0.125).astype(BF16)
            dsk_ref[...] += dsk

        @pl.when(n > 0)
        def _():
            for m2 in range(2):
                halves = []
                for g in (2 * m2, 2 * m2 + 1):
                    for acc in (acc_k, acc_v):
                        comb = jnp.where(lo, acc[2 * g, 0:CHUNK], acc[2 * g + 1, 0:CHUNK])
                        halves.append(comb + pltpu.roll(comb, 64, axis=1))
                d_kr = jnp.where(lo, halves[0], halves[2])
                d_v = jnp.where(lo, halves[1], halves[3])
                dqkv_ref[:, OFF_K + m2 * 128:OFF_K + (m2 + 1) * 128] = _rope(d_kr, cp_ref[...], -sp_ref[...]).astype(BF16)
                dqkv_ref[:, OFF_V + m2 * 128:OFF_V + (m2 + 1) * 128] = d_v.astype(BF16)

    qn = lambda n: jnp.minimum(n, nb - 1)
    pn = lambda n: jnp.maximum(jnp.minimum(n, nb) - 1, 0)
    cur = pl.BlockSpec((CHUNK, 1024), lambda n: (qn(n), 0))
    prev = pl.BlockSpec((CHUNK, 1024), lambda n: (pn(n), 0))
    cur128 = pl.BlockSpec((CHUNK, 128), lambda n: (qn(n), 0))
    prev128 = pl.BlockSpec((CHUNK, 128), lambda n: (pn(n), 0))
    cur_t = pl.BlockSpec((1024, CHUNK), lambda n: (0, qn(n)))
    prev_t = pl.BlockSpec((1024, CHUNK), lambda n: (0, pn(n)))
    one = pl.BlockSpec((1, 128), lambda n: (0, 0))
    return pl.pallas_call(
        body, name="attn_bwd", grid=(nb + 1,),
        in_specs=[cur, cur, prev, cur, prev, cur_t, prev_t, cur, cur, pl.BlockSpec((1, 16, CHUNK), lambda n: (qn(n), 0, 0)),
                  one, cur128, cur128, prev128, prev128, pl.BlockSpec(memory_space=pl.ANY)] + r_in_specs,
        out_specs=[pl.BlockSpec((CHUNK, 1536), lambda n: (pn(n), 0)), one] + r_out_specs,
        out_shape=[SDS((t, PROJ_W), BF16), SDS((1, 128), F32)] + r_outs,
        scratch_shapes=[pltpu.VMEM((8, 2 * CHUNK, 128), F32), pltpu.VMEM((8, 2 * CHUNK, 128), F32),
                        pltpu.VMEM((CHUNK, 1024), BF16)] + r_scratch,
        input_output_aliases={15: 0},
        compiler_params=_cp(dimension_semantics=("arbitrary",)),
    )(qr, kp, kp, vp, vp, kt, kt, d_o, o, lse, sinks, cos_t, sin_t, cos_t, sin_t, d_proj, *r_ops)


def _adamw(name, w, g, m, v, *, tr):
    rows, cols = w.shape
    tr = min(tr, rows)
    assert rows % tr == 0

    def body(w_ref, g_ref, m_ref, v_ref, d_ref, nm_ref, nv_ref):
        gv = g_ref[...]
        nm = ADAM_B1 * m_ref[...] + (1.0 - ADAM_B1) * gv
        nv = ADAM_B2 * v_ref[...] + (1.0 - ADAM_B2) * (gv * gv)
        m_hat = nm / (1.0 - ADAM_B1 ** ADAM_STEP)
        v_hat = nv / (1.0 - ADAM_B2 ** ADAM_STEP)
        d_ref[...] = -ADAM_LR * (m_hat / (jnp.sqrt(v_hat) + ADAM_EPS) + ADAM_WD * w_ref[...])
        nm_ref[...] = nm
        nv_ref[...] = nv

    tile = pl.BlockSpec((tr, cols), lambda i: (i, 0))
    return pl.pallas_call(
        body, name=name, grid=(rows // tr,), in_specs=[tile] * 4, out_specs=[tile] * 3,
        out_shape=[SDS((rows, cols), F32)] * 3, compiler_params=_cp(dimension_semantics=("arbitrary",)),
    )(w, g, m, v)


def _local_step(x, cos_t, sin_t, tgt, wb, ps, late=None, rides=None):
    t = x.shape[0]
    tm = min(512, t)
    tmw = min(1024, t)
    ij = lambda i, j, k: (i, j)
    i0 = lambda i, j, k: (i, 0)
    c0 = lambda i, j, k: (0, 0)
    cj = lambda i, j, k: (0, j)
    rides = rides or (lambda group, grads: None)
    rode = {}

    tkt = min(2048, t)
    proj, u, *arrived = _norm_mm("in_proj", x, ps['norm_mix_pre_w'], wb['cat'], tm=tmw, tn=1024,
                                 ride=late[0] if late else None)
    if late:
        more_wb, more_ps = late[1](arrived)
        wb, ps = {**wb, **more_wb}, {**ps, **more_ps}
    xc = _conv_silu_fwd(proj, ps['ssd_conv_w'], ps['ssd_conv_b'], tm=tm)
    bias_pad = jnp.pad(ps['ssd_dt_bias'], ((0, 0), (0, 96)))
    dt = _dt_fwd(proj, bias_pad, tm=tmw)
    dt32 = dt[:, :SSD_N_HEADS].reshape(t, SSD_N_GROUPS, 8)
    dtg = jnp.pad(dt32.transpose(1, 0, 2), ((0, 0), (0, 0), (0, 120)))
    dtg_t = dt32.transpose(1, 2, 0)
    alog = jnp.pad(ps['ssd_a_log'].reshape(SSD_N_GROUPS, 1, 8), ((0, 0), (0, 0), (0, 120)))
    alog_t = ps['ssd_a_log'].reshape(SSD_N_GROUPS, 8, 1)
    d_exp = jnp.repeat(ps['ssd_d'], SSD_HEAD_DIM, axis=1)
    y, hs = _ssd_fwd(xc, dtg, dtg_t, alog, alog_t, d_exp)
    gn = _gated_norm_fwd(y, proj, ps['ssd_norm_w'], tm=tm)
    qr, kp, vp, kt, vt = _rope_qkv(proj, cos_t, sin_t, tm=tm)
    sinks = jnp.pad(ps['attn_sinks'], ((0, 0), (0, 112)))
    ao, lse = _attn_fwd(qr, kp, vt, sinks)
    y_attn = _mm_plain("attn_out", ao, wb['ao'], tm=tmw, tn=512, tk=1024)

    def merge_ep(acc, i, j, ins, outs):
        gs, ga, ya = ins
        outs[0][...] = (_sigmoid(gs[...]) * acc + _sigmoid(ga[...]) * ya[...]).astype(BF16)
        outs[1][...] = acc

    merged, y_ssd = _mm_call(
        "ssd_out_merge", gn, wb['so'], tm=tmw, tn=512, tk=2048, epilogue=merge_ep,
        extra_in=[(proj, (tmw, 512), lambda i, j, k: (i, OFF_GS // 512 + j)),
                  (proj, (tmw, 512), lambda i, j, k: (i, OFF_GA // 512 + j)), (y_attn, (tmw, 512), ij)],
        outs=[((t, D_MODEL), BF16, (tmw, 512), ij), ((t, D_MODEL), F32, (tmw, 512), ij)])

    def mix_ep(acc, i, j, ins, outs):
        xv, wn = ins
        r = lax.rsqrt(jnp.mean(acc * acc, axis=-1, keepdims=True) + NORM_EPS)
        outs[0][...] = xv[...] + acc * r * wn[...]
        outs[1][...] = acc

    x1, mmix = _mm_call(
        "mix_out", merged, wb['mix'], tm=tm, tn=D_MODEL, tk=1024, epilogue=mix_ep,
        extra_in=[(x, (tm, D_MODEL), i0), (ps['norm_mix_post_w'], (1, D_MODEL), c0)],
        outs=[((t, D_MODEL), F32, (tm, D_MODEL), i0), ((t, D_MODEL), F32, (tm, D_MODEL), i0)])

    up_raw, h = _norm_mm("ffn_up", x1, ps['norm_ffn_pre_w'], wb['up'], tm=tmw, tn=1408)
    act = _ffn_act_fwd(up_raw, ps['ffn_conv_w'], ps['ffn_conv_b'], tm=min(256, t))

    def loss_ep(acc, i, j, ins, outs):
        x1v, tg, wn = ins
        d_ff_ref, dout_ref, loss_ref, dw_ref = outs
        wv = wn[...]
        r = lax.rsqrt(jnp.mean(acc * acc, axis=-1, keepdims=True) + NORM_EPS)
        err = x1v[...] + acc * r * wv - tg[...]
        dout = err * (1.0 / D_MODEL)
        dout_ref[...] = dout
        d_ff, dw = _rms_bwd(acc, wv, dout)
        d_ff_ref[...] = d_ff.astype(BF16)
        _accumulate(dw_ref, i == 0, dw)
        _accumulate(loss_ref, i == 0, jnp.sum(err * err, keepdims=True) * (0.5 / D_MODEL))

    d_ff, dout, loss, g_norm_ffn_post = _mm_call(
        "ffn_down_loss", act, wb['dn'], tm=tm, tn=D_MODEL, tk=FFN_D_FF, epilogue=loss_ep,
        extra_in=[(x1, (tm, D_MODEL), i0), (tgt, (tm, D_MODEL), i0), (ps['norm_ffn_post_w'], (1, D_MODEL), c0)],
        outs=[((t, D_MODEL), BF16, (tm, D_MODEL), i0), ((t, D_MODEL), F32, (tm, D_MODEL), i0),
              ((1, 1), F32, (1, 1), c0), ((1, D_MODEL), F32, (1, D_MODEL), c0)])

    d_act = _mm_plain("d_act", d_ff, wb['dn_t'], tm=tmw, tn=1408, tk=1024, out_dtype=BF16)
    g_w_down = _mm_plain("g_w_down", act, d_ff, tm=1408, tn=1024, tk=tkt, trans_a=True, out_dtype=BF16)
    d_gate, d_val, db_g, db_v = _ffn_act_bwd(up_raw, d_act, ps['ffn_conv_w'], ps['ffn_conv_b'], tm=min(256, t))
    d_up_raw, gcw_g = _conv_bwd2("ffn_conv_bwd2_gate", d_gate, up_raw, 0, ps['ffn_conv_w'][:, :FFN_D_FF], tm=min(256, t),
                                 tc=1408, out_cols=2 * FFN_D_FF, out_col0=0)
    d_up_raw, gcw_v = _conv_bwd2("ffn_conv_bwd2_val", d_val, up_raw, FFN_D_FF, ps['ffn_conv_w'][:, FFN_D_FF:], tm=min(256, t),
                                 tc=1408, out_cols=2 * FFN_D_FF, out_col0=FFN_D_FF, fill=d_up_raw)
    g_ffn_conv_w = jnp.concatenate([gcw_g, gcw_v], axis=1)

    def dx1_ep(acc, i, j, ins, outs):
        x1v, wpre, dout_v, mmv, wpost = ins
        d_x1_ref, d_mm_ref, dwpre_ref, dwpost_ref = outs
        d_n, dw_pre = _rms_bwd(x1v[...], wpre[...], acc)
        d_x1 = dout_v[...] + d_n
        d_x1_ref[...] = d_x1
        d_mm, dw_post = _rms_bwd(mmv[...], wpost[...], d_x1)
        d_mm_ref[...] = d_mm.astype(BF16)
        _accumulate(dwpre_ref, i == 0, dw_pre)
        _accumulate(dwpost_ref, i == 0, dw_post)

    d_x1, d_mm, g_norm_ffn_pre, g_norm_mix_post = _mm_call(
        "d_h", d_up_raw, wb['up_t'], tm=tm, tn=D_MODEL, tk=FFN_D_FF, epilogue=dx1_ep,
        extra_in=[(x1, (tm, D_MODEL), i0), (ps['norm_ffn_pre_w'], (1, D_MODEL), c0), (dout, (tm, D_MODEL), i0),
                  (mmix, (tm, D_MODEL), i0), (ps['norm_mix_post_w'], (1, D_MODEL), c0)],
        outs=[((t, D_MODEL), F32, (tm, D_MODEL), i0), ((t, D_MODEL), BF16, (tm, D_MODEL), i0),
              ((1, D_MODEL), F32, (1, D_MODEL), c0), ((1, D_MODEL), F32, (1, D_MODEL), c0)])
    g_w_up_t = _mm_plain("g_w_up", d_up_raw, h, tm=1408, tn=1024, tk=tkt, trans_a=True, out_dtype=BF16)
    ride_ffn = rides('ffn', {'ffn_w_up': g_w_up_t, 'ffn_w_down': g_w_down})

    def dmerge_ep(acc, i, j, ins, outs):
        gs, ga, ys, ya = ins
        sg_s, sg_a = _sigmoid(gs[...]), _sigmoid(ga[...])
        outs[0][...] = (acc * sg_s).astype(BF16)
        outs[1][...] = (acc * sg_a).astype(BF16)
        outs[2][:, 0:D_MODEL] = (acc * ys[...] * sg_s * (1.0 - sg_s)).astype(BF16)
        outs[2][:, D_MODEL:2 * D_MODEL] = (acc * ya[...] * sg_a * (1.0 - sg_a)).astype(BF16)

    d_yssd, d_yattn, d_proj = _mm_call(
        "d_merged", d_mm, wb['mix_t'], tm=tm, tn=D_MODEL, tk=1024, epilogue=dmerge_ep,
        extra_in=[(proj, (tm, D_MODEL), lambda i, j, k: (i, OFF_GS // D_MODEL)),
                  (proj, (tm, D_MODEL), lambda i, j, k: (i, OFF_GA // D_MODEL)), (y_ssd, (tm, D_MODEL), i0), (y_attn, (tm, D_MODEL), i0)],
        outs=[((t, D_MODEL), BF16, (tm, D_MODEL), i0), ((t, D_MODEL), BF16, (tm, D_MODEL), i0),
              ((t, PROJ_W), BF16, (tm, 2 * D_MODEL), lambda i, j, k: (i, OFF_GS // (2 * D_MODEL)))])
    g_w_mix = _mm_plain("g_w_mix", merged, d_mm, tm=1024, tn=1024, tk=tkt, trans_a=True, out_dtype=BF16)

    def dgn_ep(acc, i, j, ins, outs):
        yv, zv, wn = ins
        d_y_ref, d_z_ref, dw_ref = outs
        zz = zv[...]
        sz = _sigmoid(zz)
        silu = zz * sz
        gv = yv[...] * silu
        r = lax.rsqrt(jnp.mean(gv * gv, axis=-1, keepdims=True) + NORM_EPS)
        gh = gv * r
        dgh = acc * wn[...]
        dg = r * (dgh - gh * jnp.mean(dgh * gh, axis=-1, keepdims=True))
        d_y_ref[...] = dg * silu
        d_z_ref[...] = (dg * yv[...] * (sz * (1.0 + zz * (1.0 - sz)))).astype(BF16)
        dw = jnp.sum(acc * gh, axis=0, keepdims=True)

        @pl.when(i == 0)
        def _():
            dw_ref[j] = dw

        @pl.when(i > 0)
        def _():
            dw_ref[j] += dw

    d_y, d_proj, g_ssd_norm = _mm_call(
        "d_gn", d_yssd, wb['so_t'], tm=tm, tn=512, tk=1024, epilogue=dgn_ep, fill=(d_proj, 1),
        extra_in=[(y, (tm, 512), ij), (proj, (tm, 512), lambda i, j, k: (i, OFF_Z // 512 + j)), (ps['ssd_norm_w'], (1, 512), cj)],
        outs=[((t, SSD_D_INNER), F32, (tm, 512), ij), ((t, PROJ_W), BF16, (tm, 512), lambda i, j, k: (i, OFF_Z // 512 + j)),
              ((SSD_N_GROUPS, 1, 512), F32, (SSD_N_GROUPS, 1, 512), lambda i, j, k: (0, 0, 0))])
    g_ssd_norm = g_ssd_norm.reshape(1, SSD_D_INNER)
    g_w_so = _mm_plain("g_w_so", gn, d_yssd, tm=1024, tn=1024, tk=tkt, trans_a=True, out_dtype=BF16)
    d_xc, d_dtg, d_alog, d_dd, *rode['ffn'] = _ssd_bwd(xc, dtg, dtg_t, alog, alog_t, d_exp, d_y, hs, ride=ride_ffn)
    d_pre, g_ssd_conv_b = _conv_silu_bwd1(d_xc, proj, ps['ssd_conv_w'], ps['ssd_conv_b'], tm=tm)
    d_proj, g_ssd_conv_w = _conv_bwd2("ssd_conv_bwd2", d_pre, proj, OFF_XBC, ps['ssd_conv_w'], tm=tm, tc=512,
                                      out_cols=PROJ_W, out_col0=OFF_XBC, fill=d_proj)
    d_dt = jnp.pad(d_dtg[:, :, :8].transpose(1, 0, 2).reshape(t, SSD_N_HEADS), ((0, 0), (0, 96)))
    d_proj, g_dt_bias = _dt_bwd(d_dt, proj, bias_pad, d_proj, tm=tmw)

    d_ao = _mm_plain("d_ao", d_yattn, wb['ao_t'], tm=tmw, tn=512, tk=1024, out_dtype=BF16)
    g_w_ao = _mm_plain("g_w_ao", ao, d_yattn, tm=1024, tn=1024, tk=tkt, trans_a=True, out_dtype=BF16)
    ride_mix = rides('mix', {'ssd_w_out': g_w_so, 'attn_w_out': g_w_ao, 'w_mix_out': g_w_mix})
    d_proj, g_sinks, *rode['mix'] = _attn_bwd(qr, kp, vp, kt, d_ao, ao, lse, sinks, cos_t, sin_t, d_proj, ride=ride_mix)

    def dx_ep(acc, i, j, ins, outs):
        xv, wn, dx1v = ins
        d_n, dw = _rms_bwd(xv[...], wn[...], acc)
        outs[0][...] = dx1v[...] + d_n
        _accumulate(outs[1], i == 0, dw)

    g_cat_t = _mm_plain("g_w_in", d_proj, u, tm=1024, tn=1024, tk=tkt, trans_a=True)
    grad_x, g_norm_mix_pre, *rode['w_in'] = _mm_call(
        "d_u", d_proj, wb['cat_t'], tm=tm, tn=D_MODEL, tk=2304, epilogue=dx_ep, ride=rides('w_in', {'w_in': g_cat_t}),
        extra_in=[(x, (tm, D_MODEL), i0), (ps['norm_mix_pre_w'], (1, D_MODEL), c0), (d_x1, (tm, D_MODEL), i0)],
        outs=[((t, D_MODEL), F32, (tm, D_MODEL), i0), ((1, D_MODEL), F32, (1, D_MODEL), c0)])

    grads = {
        'norm_mix_pre_w': g_norm_mix_pre, 'w_in': g_cat_t, 'ssd_conv_w': g_ssd_conv_w, 'ssd_conv_b': g_ssd_conv_b,
        'ssd_dt_bias': g_dt_bias[:, :SSD_N_HEADS], 'ssd_a_log': d_alog[:, 0, :8].reshape(1, SSD_N_HEADS),
        'ssd_d': d_dd[:, 0, :8].reshape(1, SSD_N_HEADS), 'ssd_norm_w': g_ssd_norm, 'ssd_w_out': g_w_so,
        'attn_sinks': g_sinks[:, :ATTN_N_HEADS], 'attn_w_out': g_w_ao, 'w_mix_out': g_w_mix,
        'norm_mix_post_w': g_norm_mix_post, 'norm_ffn_pre_w': g_norm_ffn_pre, 'ffn_w_up': g_w_up_t,
        'ffn_conv_w': g_ffn_conv_w, 'ffn_conv_b': jnp.concatenate([db_g, db_v], axis=1), 'ffn_w_down': g_w_down,
        'norm_ffn_post_w': g_norm_ffn_post,
    }
    return loss, grad_x, grads, rode


def _group_channels(a):
    parts = []
    for g in range(SSD_N_GROUPS):
        parts += [a[..., 512 * g:512 * (g + 1)], a[..., 2048 + 128 * g:2048 + 128 * (g + 1)],
                  a[..., 2560 + 128 * g:2560 + 128 * (g + 1)]]
    return jnp.concatenate(parts, axis=-1)


def _ungroup_channels(a):
    xs = [a[..., GROUP_W * g:GROUP_W * g + 512] for g in range(SSD_N_GROUPS)]
    bs = [a[..., GROUP_W * g + 512:GROUP_W * g + 640] for g in range(SSD_N_GROUPS)]
    cs = [a[..., GROUP_W * g + 640:GROUP_W * (g + 1)] for g in range(SSD_N_GROUPS)]
    return jnp.concatenate(xs + bs + cs, axis=-1)


def _proj_rows(a_t, lo, hi):
    out = []
    for start, length, dst in sorted(PROJ_SEGS):
        s, e = max(lo, start), min(hi, start + length)
        if s < e:
            out.append(a_t[dst + s - start:dst + e - start])
    return out


def _to_proj_layout(w_in_t):
    pieces, pos = [], 0
    for start, length, dst in sorted(PROJ_SEGS, key=lambda s: s[2]):
        if dst > pos:
            pieces.append(jnp.zeros((dst - pos, w_in_t.shape[1]), w_in_t.dtype))
        pieces.append(w_in_t[start:start + length])
        pos = dst + length
    if pos < PROJ_W:
        pieces.append(jnp.zeros((PROJ_W - pos, w_in_t.shape[1]), w_in_t.dtype))
    return jnp.concatenate(pieces, axis=0)


def _rope_tables(positions):
    half = 32
    inv_freq = ROPE_THETA ** (-jnp.arange(half, dtype=F32) * 2.0 / 64)
    ang = positions.astype(F32)[:, None] * inv_freq
    cos, sin = jnp.cos(ang), jnp.sin(ang)
    return jnp.concatenate([cos, cos, cos, cos], axis=1), jnp.concatenate([-sin, sin, -sin, sin], axis=1)


def _matmul_weights(w_in_t):
    cat_t = _to_proj_layout(w_in_t)
    return {'cat': cat_t.T, 'cat_t': cat_t}


def _late_weights(so, ao, mix, up_t, dn):
    return {'so': so, 'so_t': so.T, 'ao': ao, 'ao_t': ao.T, 'mix': mix, 'mix_t': mix.T,
            'up': up_t.T, 'up_t': up_t, 'dn': dn, 'dn_t': dn.T}


ANY = pl.BlockSpec(memory_space=pl.ANY)
MESH = pl.DeviceIdType.MESH
ROW_ALIGN = 32


def _mesh_pos():
    return lax.axis_index("x"), lax.axis_index("y"), lax.axis_index("c")


def _other_chips(x, y):
    return [(1 - x, y), (x, 1 - y), (1 - x, 1 - y)]


def _remote(src, dst, send_sems, recv_sems, k, to):
    return pltpu.make_async_remote_copy(src_ref=src, dst_ref=dst, send_sem=send_sems.at[k], recv_sem=recv_sems.at[k],
                                        device_id=to, device_id_type=MESH)


def _half(c, rh):
    return pl.ds(pl.multiple_of(c * rh, 16), rh)


def _ag_ride(shard):
    r = shard.shape[0]
    rh = r // 2

    def first_copies(w_ref, out_ref, send_sems, recv_sems):
        x, y, c = _mesh_pos()
        p = 2 * x + y
        mine = _half(c, rh)
        cps = [_remote(w_ref, out_ref.at[p], send_sems, recv_sems, 6, (x, y, 1 - c))]
        return cps + [_remote(w_ref.at[mine], out_ref.at[p, mine], send_sems, recv_sems, j, (cx, cy, c))
                      for j, (cx, cy) in enumerate(_other_chips(x, y))]

    def start(ins, outs, send_sems, recv_sems):
        for cp in first_copies(ins[0], outs[0], send_sems, recv_sems):
            cp.start()

    def finish(ins, outs, send_sems, recv_sems):
        w_ref, out_ref = ins[0], outs[0]
        x, y, c = _mesh_pos()
        sib = (x, y, 1 - c)
        mine, other = _half(c, rh), _half(1 - c, rh)
        chips = _other_chips(x, y)
        passed = []
        for j, (cx, cy) in enumerate(chips):
            slab = out_ref.at[2 * cx + cy, mine]
            _remote(slab, slab, send_sems, recv_sems, j, sib).wait_recv()
            fwd = _remote(slab, slab, send_sems, recv_sems, 3 + j, sib)
            fwd.start()
            passed.append(fwd)
        for j, (cx, cy) in enumerate(chips):
            slab = out_ref.at[2 * cx + cy, other]
            _remote(slab, slab, send_sems, recv_sems, 3 + j, sib).wait_recv()
        _remote(w_ref, out_ref.at[2 * x + y], send_sems, recv_sems, 6, sib).wait_recv()
        for cp in first_copies(w_ref, out_ref, send_sems, recv_sems) + passed:
            cp.wait_send()

    return _Ride((shard,), (SDS((N_CHIPS, r, COMM_LANES), shard.dtype),), 7, start, finish)


def _rs_ride(gbuf):
    rh = gbuf.shape[1] // 2

    def copies(g_ref, r_ref, send_sems, recv_sems, landing):
        x, y, c = _mesh_pos()
        cps = []
        for k, (cx, cy) in enumerate(_other_chips(x, y)):
            for h in range(2):
                slot = 2 * k + c if landing else 2 * k + h
                cps.append(pltpu.make_async_remote_copy(
                    src_ref=g_ref.at[2 * cx + cy, pl.ds(h * rh, rh)], dst_ref=r_ref.at[slot],
                    send_sem=send_sems.at[2 * k + h], recv_sem=recv_sems.at[slot],
                    device_id=(cx, cy, h), device_id_type=MESH))
        cps.append(_remote(g_ref.at[2 * x + y, _half(1 - c, rh)], r_ref.at[6], send_sems, recv_sems, 6, (x, y, 1 - c)))
        return cps

    def start(ins, outs, send_sems, recv_sems):
        for cp in copies(ins[0], outs[0], send_sems, recv_sems, True):
            cp.start()

    def finish(ins, outs, send_sems, recv_sems):
        for cp in copies(ins[0], outs[0], send_sems, recv_sems, False):
            cp.wait()

    return _Ride((gbuf,), (SDS((7, rh, COMM_LANES), gbuf.dtype),), 7, start, finish)


def _rs_sum(name, gbuf, got, pc_idx):
    rh = got.shape[1]
    tr = max(d for d in range(16, 513, 16) if rh % d == 0)
    nb = rh // tr

    def body(pc_ref, own_ref, *refs):
        o_ref = refs[7]
        p, c = pc_ref[0], pc_ref[1]
        own = own_ref[0].astype(F32)
        slots = [r[0].astype(F32) for r in refs[:7]]

        def term(q, h):
            code = p ^ q
            far = jnp.where(code == 2, slots[h], jnp.where(code == 1, slots[2 + h], slots[4 + h]))
            return jnp.where(code == 0, jnp.where(c == h, own, slots[6]), far)

        acc = term(0, 0)
        for q, h in [(0, 1), (1, 0), (1, 1), (2, 0), (2, 1), (3, 0), (3, 1)]:
            acc = acc + term(q, h)
        o_ref[0] = acc

    slot = lambda s: pl.BlockSpec((1, tr, COMM_LANES), lambda i, pc: (s, i, 0))
    return pl.pallas_call(
        body, name=name,
        grid_spec=pltpu.PrefetchScalarGridSpec(
            num_scalar_prefetch=1, grid=(nb,),
            in_specs=[pl.BlockSpec((1, tr, COMM_LANES), lambda i, pc: (pc[0], pc[1] * nb + i, 0))] + [slot(s) for s in range(7)],
            out_specs=pl.BlockSpec((1, tr, COMM_LANES), lambda i, pc: (pc[1], i, 0))),
        out_shape=SDS((2, rh, COMM_LANES), F32), compiler_params=_cp(dimension_semantics=("arbitrary",)),
    )(pc_idx, gbuf, *([got] * 7))


def _pair_gather_all(bufs):
    n = len(bufs)

    def body(*refs):
        outs, send_sems, recv_sems = refs[n:2 * n], refs[2 * n], refs[2 * n + 1]
        x, y, c = _mesh_pos()
        cps = [_remote(o.at[c], o.at[c], send_sems, recv_sems, k, (x, y, 1 - c)) for k, o in enumerate(outs)]
        for cp in cps:
            cp.start()
        for k, o in enumerate(outs):
            _remote(o.at[1 - c], o.at[1 - c], send_sems, recv_sems, k, (x, y, 1 - c)).wait_recv()
        for cp in cps:
            cp.wait_send()

    return pl.pallas_call(
        body, name="grad_pair_gather", in_specs=[ANY] * n, out_specs=[ANY] * n,
        out_shape=[SDS(b.shape, b.dtype) for b in bufs],
        scratch_shapes=[pltpu.SemaphoreType.DMA((n,)), pltpu.SemaphoreType.DMA((n,))],
        input_output_aliases={k: k for k in range(n)},
    )(*bufs)


def _pack_rows(big, small=()):
    parts = list(big)
    if small:
        flat = jnp.concatenate([p.reshape(-1) for p in small])
        k = -(-flat.shape[0] // (16 * COMM_LANES)) * 16
        parts.append(jnp.pad(flat, (0, k * COMM_LANES - flat.shape[0])).reshape(k, COMM_LANES))
    pad = -sum(p.shape[0] for p in parts) % ROW_ALIGN
    if pad:
        parts.append(jnp.zeros((pad, COMM_LANES), parts[0].dtype))
    return jnp.concatenate(parts, axis=0) if len(parts) > 1 else parts[0]


def _take(flat, off, shape):
    n = 1
    for d in shape:
        n *= d
    return flat[off:off + n].reshape(shape), off + n


BIG_ROWS = {'w_in': 2184, 'ssd_w_out': 512, 'attn_w_out': 256, 'w_mix_out': 256, 'ffn_w_up': 1408, 'ffn_w_down': 704}
TRANSPOSED = ('w_in', 'ffn_w_up')
LATE = ('ssd_w_out', 'attn_w_out', 'w_mix_out', 'ffn_w_up', 'ffn_w_down')
CONV_TAPS = ('ssd_conv_w', 'ffn_conv_w')
RS_GROUPS = {'ffn': ('ffn_w_up', 'ffn_w_down'), 'mix': ('ssd_w_out', 'attn_w_out', 'w_mix_out'), 'w_in': ('w_in',)}


def _exchange(name, ride):
    n_in, n_out = len(ride.ins), len(ride.outs)

    def body(*refs):
        ins, outs, sems = refs[:n_in], refs[n_in:n_in + n_out], refs[n_in + n_out:]
        ride.start(ins, outs, *sems)
        ride.finish(ins, outs, *sems)

    return pl.pallas_call(
        body, name=name, in_specs=[ANY] * n_in, out_specs=[ANY] * n_out, out_shape=list(ride.outs),
        scratch_shapes=[pltpu.SemaphoreType.DMA((ride.n_sems,)), pltpu.SemaphoreType.DMA((ride.n_sems,))],
    )(*ride.ins)


def kernel(x, positions, norm_mix_pre_w, w_in, ssd_conv_w, ssd_conv_b, ssd_dt_bias, ssd_a_log, ssd_d, ssd_norm_w, ssd_w_out, attn_sinks, attn_w_out, w_mix_out, norm_mix_post_w, norm_ffn_pre_w, ffn_w_up, ffn_conv_w, ffn_conv_b, ffn_w_down, norm_ffn_post_w, loss_target, m_norm_mix_pre_w, m_w_in, m_ssd_conv_w, m_ssd_conv_b, m_ssd_dt_bias, m_ssd_a_log, m_ssd_d, m_ssd_norm_w, m_ssd_w_out, m_attn_sinks, m_attn_w_out, m_w_mix_out, m_norm_mix_post_w, m_norm_ffn_pre_w, m_ffn_w_up, m_ffn_conv_w, m_ffn_conv_b, m_ffn_w_down, m_norm_ffn_post_w, v_norm_mix_pre_w, v_w_in, v_ssd_conv_w, v_ssd_conv_b, v_ssd_dt_bias, v_ssd_a_log, v_ssd_d, v_ssd_norm_w, v_ssd_w_out, v_attn_sinks, v_attn_w_out, v_w_mix_out, v_norm_mix_post_w, v_norm_ffn_pre_w, v_ffn_w_up, v_ffn_conv_w, v_ffn_conv_b, v_ffn_w_down, v_norm_ffn_post_w):
    given = dict(locals())
    w = {n: given[n][0] for n in WEIGHTS}
    w = {n: (a if a.ndim == 2 else a[None]) for n, a in w.items()}
    mom_m = {n: given['m_' + n].reshape(w[n].shape) for n in WEIGHTS}
    mom_v = {n: given['v_' + n].reshape(w[n].shape) for n in WEIGHTS}
    cx, cy, cc = _mesh_pos()
    pc_idx = jnp.stack([2 * cx + cy, cc]).astype(jnp.int32)

    rows_of = lambda n: (w[n].T if n in TRANSPOSED else w[n]).astype(BF16)
    gathered = _exchange("w_in_all_gather", _ag_ride(_pack_rows([rows_of('w_in')])))[0]
    wb = _matmul_weights(jnp.concatenate([gathered[s, :BIG_ROWS['w_in']] for s in range(N_CHIPS)], axis=0))
    taps = [lax.bitcast_convert_type(w[n], BF16) for n in CONV_TAPS]

    def unpack_late(arrived):
        rows, conv = {n: [] for n in LATE}, {n: [] for n in CONV_TAPS}
        for s in range(N_CHIPS):
            r0 = 0
            for n in LATE:
                rows[n].append(arrived[0][s, r0:r0 + BIG_ROWS[n]])
                r0 += BIG_ROWS[n]
            flat, off = arrived[0][s, r0:r0 + 16].reshape(-1), 0
            for n in CONV_TAPS:
                a, off = _take(flat, off, w[n].shape + (2,))
                conv[n].append(lax.bitcast_convert_type(a, F32))
        full = {n: jnp.concatenate(rows[n], axis=0) for n in LATE}
        return (_late_weights(*[full[n] for n in LATE]),
                {'ssd_conv_w': _group_channels(jnp.concatenate(conv['ssd_conv_w'], axis=1)),
                 'ffn_conv_w': jnp.concatenate(conv['ffn_conv_w'], axis=1)})

    late = (_ag_ride(_pack_rows([rows_of(n) for n in LATE], taps)), unpack_late)

    sent = {}

    def rides(group, g):
        parts = []
        for s in range(N_CHIPS):
            slab = []
            for n in RS_GROUPS[group]:
                lo, hi = BIG_ROWS[n] * s, BIG_ROWS[n] * (s + 1)
                slab += _proj_rows(g[n], lo, hi) if n == 'w_in' else [g[n][lo:hi]]
            slab = [a.astype(BF16) for a in slab]
            pad = -sum(a.shape[0] for a in slab) % ROW_ALIGN
            parts += slab + ([jnp.zeros((pad, COMM_LANES), BF16)] if pad else [])
        sent[group] = jnp.concatenate(parts, axis=0).reshape(N_CHIPS, -1, COMM_LANES)
        return _rs_ride(sent[group])

    ps = {n: w[n] for n in REPLICATED}
    ps['ssd_conv_b'] = _group_channels(w['ssd_conv_b'])
    cos_t, sin_t = _rope_tables(positions[0])
    loss, grad_x, grads, rode = _local_step(x[0], cos_t, sin_t, loss_target[0], wb, ps, late, rides)
    grads['ssd_conv_w'] = _ungroup_channels(grads['ssd_conv_w'])
    grads['ssd_conv_b'] = _ungroup_channels(grads['ssd_conv_b'])

    shard_cols = {n: sh[1] for n, _, sh in SHARDED}
    parts = []
    for s in range(N_CHIPS):
        small = [grads[n][:, shard_cols[n] * s:shard_cols[n] * (s + 1)] for n in CONV_TAPS] + [grads[n] for n in REPLICATED]
        flat = _pack_rows([], small)
        high = flat.astype(BF16)
        parts += [high, (flat - high.astype(F32)).astype(BF16)]
    sent['small'] = jnp.concatenate(parts, axis=0).reshape(N_CHIPS, -1, COMM_LANES)
    rode['small'] = _exchange("grad_small_exchange", _rs_ride(sent['small']))

    groups = ('ffn', 'mix', 'w_in', 'small')
    red = _pair_gather_all([_rs_sum("grad_sum_" + g, sent[g], rode[g][0], pc_idx) for g in groups])
    red = {g: r.reshape(-1, COMM_LANES) for g, r in zip(groups, red)}
    g_red = {}
    for g in groups[:3]:
        r0 = 0
        for n in RS_GROUPS[g]:
            g_red[n] = red[g][r0:r0 + BIG_ROWS[n]].T if n in TRANSPOSED else red[g][r0:r0 + BIG_ROWS[n]]
            r0 += BIG_ROWS[n]
    half = red['small'].shape[0] // 2
    flat, off = (red['small'][:half] + red['small'][half:]).reshape(-1), 0
    for n in CONV_TAPS + REPLICATED:
        g_red[n], off = _take(flat, off, w[n].shape)

    small_names = [n for n in WEIGHTS if n not in MATMUL_WEIGHTS]
    delta, new_m, new_v = {}, {}, {}
    for n in MATMUL_WEIGHTS:
        delta[n], new_m[n], new_v[n] = _adamw("adamw_" + n, w[n], g_red[n], mom_m[n], mom_v[n], tr=64)
    packed = [_pack_small([d[n] for n in small_names]) for d in (w, g_red, mom_m, mom_v)]
    outs = _adamw("adamw_small", *packed, tr=packed[0].shape[0])
    for res, o in zip((delta, new_m, new_v), outs):
        fl, off = o.reshape(-1), 0
        for n in small_names:
            res[n], off = _take(fl, off, w[n].shape)

    loss_all = lax.psum(loss[0, 0], ("x", "y", "c"))
    shaped = lambda d: [d[n].reshape(given[n].shape) for n in WEIGHTS]
    return (loss_all, grad_x[None], *shaped(g_red), *shaped(delta), *shaped(new_m), *shaped(new_v))


def _pack_small(pieces):
    flat = jnp.concatenate([p.reshape(-1) for p in pieces])
    rows = -(-flat.shape[0] // (128 * 8)) * 8
    return jnp.pad(flat, (0, rows * 128 - flat.shape[0])).reshape(rows, 128)
```

```python
from typing import Callable, NamedTuple

import jax
import jax.numpy as jnp
from jax import lax
from jax.experimental import pallas as pl
from jax.experimental.pallas import tpu as pltpu

F32 = jnp.float32
BF16 = jnp.bfloat16
SDS = jax.ShapeDtypeStruct
HIGHEST = lax.Precision.HIGHEST

D_MODEL = 1024
SSD_D_INNER = 2048
SSD_N_HEADS = 32
SSD_HEAD_DIM = 64
SSD_N_GROUPS = 4
SSD_HEADS_PER_GROUP = 8
SSD_D_STATE = 128
SSD_CONV_DIM = 3072
CHUNK = 128
ATTN_N_HEADS = 16
KV_WIDTH = 256
FFN_D_FF = 2816
IN_PROJ_DIM = 8736
ROPE_THETA = 10000.0
NORM_EPS = 1e-6
ADAM_LR, ADAM_B1, ADAM_B2, ADAM_EPS, ADAM_WD, ADAM_STEP = 0.001, 0.9, 0.999, 1e-08, 0.01, 10

PROJ_W = 9216
OFF_Q, OFF_K, OFF_V, OFF_Z, OFF_DT, OFF_GS, OFF_GA, OFF_XBC = 0, 1024, 1280, 1536, 3584, 4096, 5120, 6144
GROUP_W = 768
PROJ_SEGS = ([(0, 2048, OFF_Z)]
             + [(2048 + 512 * g, 512, OFF_XBC + GROUP_W * g) for g in range(4)]
             + [(4096 + 128 * g, 128, OFF_XBC + GROUP_W * g + 512) for g in range(4)]
             + [(4608 + 128 * g, 128, OFF_XBC + GROUP_W * g + 640) for g in range(4)]
             + [(5120, 32, OFF_DT), (5152, 1024, OFF_Q), (6176, 256, OFF_K), (6432, 256, OFF_V),
                (6688, 1024, OFF_GS), (7712, 1024, OFF_GA)])
VMEM_LIMIT_MB = 48
NEG = -1e30

WEIGHTS = ('norm_mix_pre_w', 'w_in', 'ssd_conv_w', 'ssd_conv_b', 'ssd_dt_bias', 'ssd_a_log', 'ssd_d', 'ssd_norm_w',
           'ssd_w_out', 'attn_sinks', 'attn_w_out', 'w_mix_out', 'norm_mix_post_w', 'norm_ffn_pre_w', 'ffn_w_up',
           'ffn_conv_w', 'ffn_conv_b', 'ffn_w_down', 'norm_ffn_post_w')
SHARDED = (('w_in', 1, (1024, 2184)), ('ssd_conv_w', 1, (4, 768)), ('ssd_w_out', 0, (512, 1024)),
           ('attn_w_out', 0, (256, 1024)), ('w_mix_out', 0, (256, 1024)), ('ffn_w_up', 1, (1024, 1408)),
           ('ffn_conv_w', 1, (3, 1408)), ('ffn_w_down', 0, (704, 1024)))
MATMUL_WEIGHTS = ('w_in', 'ssd_w_out', 'attn_w_out', 'w_mix_out', 'ffn_w_up', 'ffn_w_down')
REPLICATED = tuple(n for n in WEIGHTS if n not in {s[0] for s in SHARDED})
N_CHIPS = 4
COMM_LANES = 1024


def _cp(vmem_mb=VMEM_LIMIT_MB, **kw):
    return pltpu.CompilerParams(vmem_limit_bytes=vmem_mb << 20, **kw)


class _Ride(NamedTuple):
    ins: tuple
    outs: tuple
    n_sems: int
    start: Callable
    finish: Callable
    middle: Callable = None


def _ride_parts(ride):
    if ride is None:
        return [], [], [], [], []
    hbm = pl.BlockSpec(memory_space=pl.ANY)
    return (list(ride.ins), [hbm] * len(ride.ins), list(ride.outs), [hbm] * len(ride.outs),
            [pltpu.SemaphoreType.DMA((ride.n_sems,)), pltpu.SemaphoreType.DMA((ride.n_sems,))])


def _ride_run(ride, first, last, in_refs, out_refs, sems, middle=None):
    if ride is None:
        return

    @pl.when(first)
    def _():
        ride.start(in_refs, out_refs, *sems)

    if ride.middle is not None:
        @pl.when(last if middle is None else middle)
        def _():
            ride.middle(in_refs, out_refs, *sems)

    @pl.when(last)
    def _():
        ride.finish(in_refs, out_refs, *sems)


def _iota(shape, axis):
    return lax.broadcasted_iota(jnp.int32, shape, axis)


def _sigmoid(v):
    return 1.0 / (1.0 + jnp.exp(-v))


def _mm_call(name, a, b, *, tm, tn, tk, epilogue, outs, extra_in=(), trans_a=False, fill=None, ride=None):
    if trans_a:
        kdim, m = a.shape
    else:
        m, kdim = a.shape
    n = b.shape[1]
    assert b.shape[0] == kdim and m % tm == 0 and n % tn == 0 and kdim % tk == 0, (name, a.shape, b.shape, tm, tn, tk)
    gi, gj, gk = m // tm, n // tn, kdim // tk
    n_in, n_out = len(extra_in), len(outs)

    n_fill = 0 if fill is None else 1
    r_ops, r_in_specs, r_outs, r_out_specs, r_scratch = _ride_parts(ride)

    def body(a_ref, b_ref, *rest):
        ins = rest[:n_in]
        rest = rest[n_in + n_fill:]
        r_in, rest = rest[:len(r_ops)], rest[len(r_ops):]
        out_refs, rest = rest[:n_out], rest[n_out:]
        r_out, scratch = rest[:len(r_outs)], rest[len(r_outs):]
        i, j, k = pl.program_id(0), pl.program_id(1), pl.program_id(2)
        _ride_run(ride, (i == 0) & (j == 0) & (k == 0), (i == gi - 1) & (j == gj - 1) & (k == gk - 1),
                  r_in, r_out, scratch[-2:])
        av = a_ref[...].astype(BF16)
        bv = b_ref[...].astype(BF16)
        if trans_a:
            part = lax.dot_general(av, bv, (((0,), (0,)), ((), ())), preferred_element_type=F32)
        else:
            part = jnp.dot(av, bv, preferred_element_type=F32)
        if gk == 1:
            epilogue(part, i, j, ins, out_refs)
        else:
            acc = scratch[0]

            @pl.when(k == 0)
            def _():
                acc[...] = part

            @pl.when(k > 0)
            def _():
                acc[...] += part

            @pl.when(k == gk - 1)
            def _():
                epilogue(acc[...], i, j, ins, out_refs)

    a_spec = pl.BlockSpec((tk, tm), lambda i, j, k: (k, i)) if trans_a else pl.BlockSpec((tm, tk), lambda i, j, k: (i, k))
    in_specs = [a_spec, pl.BlockSpec((tk, tn), lambda i, j, k: (k, j))]
    in_specs += [pl.BlockSpec(bs, im) for _, bs, im in extra_in]
    operands = [a, b] + [e[0] for e in extra_in]
    aliases = {}
    if fill is not None:
        in_specs.append(pl.BlockSpec(memory_space=pl.ANY))
        aliases = {len(operands): fill[1]}
        operands.append(fill[0])
    return pl.pallas_call(
        body, name=name, grid=(gi, gj, gk), in_specs=in_specs + r_in_specs,
        out_specs=[pl.BlockSpec(bs, im) for _, _, bs, im in outs] + r_out_specs,
        out_shape=[SDS(s, d) for s, d, _, _ in outs] + r_outs,
        scratch_shapes=([pltpu.VMEM((tm, tn), F32)] if gk > 1 else []) + r_scratch,
        input_output_aliases=aliases,
        compiler_params=_cp(dimension_semantics=("arbitrary", "arbitrary", "arbitrary")),
    )(*operands, *r_ops)


def _mm_plain(name, a, b, *, tm, tn, tk, out_dtype=F32, trans_a=False):
    m = a.shape[1] if trans_a else a.shape[0]

    def epilogue(acc, i, j, ins, outs):
        outs[0][...] = acc.astype(out_dtype)

    return _mm_call(name, a, b, tm=tm, tn=tn, tk=tk, epilogue=epilogue, trans_a=trans_a,
                    outs=[((m, b.shape[1]), out_dtype, (tm, tn), lambda i, j, k: (i, j))])[0]


def _accumulate(ref, first, value):
    @pl.when(first)
    def _():
        ref[...] = value

    @pl.when(jnp.logical_not(first))
    def _():
        ref[...] += value


def _rms_bwd(xv, w, dy):
    r = lax.rsqrt(jnp.mean(xv * xv, axis=-1, keepdims=True) + NORM_EPS)
    xn = xv * r
    dxh = dy * w
    dx = r * (dxh - xn * jnp.mean(dxh * xn, axis=-1, keepdims=True))
    return dx, jnp.sum(dy * xn, axis=0, keepdims=True)


def _norm_mm(name, x, wn, w, *, tm, tn, ride=None):
    t, dm = x.shape
    n = w.shape[1]
    tm = min(tm, t)
    gi, gj = t // tm, n // tn
    r_ops, r_in_specs, r_outs, r_out_specs, r_scratch = _ride_parts(ride)

    def body(x_ref, wn_ref, w_ref, *rest):
        r_in, rest = rest[:len(r_ops)], rest[len(r_ops):]
        o_ref, u_ref = rest[:2]
        r_out, sems = rest[2:2 + len(r_outs)], rest[2 + len(r_outs):]
        i, j = pl.program_id(0), pl.program_id(1)
        _ride_run(ride, (i == 0) & (j == 0), (i == gi - 1) & (j == gj - 1), r_in, r_out, sems,
                  middle=(i == (3 * gi) // 4) & (j == 0) if gi > 1 else None)

        @pl.when(j == 0)
        def _():
            xv = x_ref[...]
            r = lax.rsqrt(jnp.mean(xv * xv, axis=-1, keepdims=True) + NORM_EPS)
            u_ref[...] = (xv * r * wn_ref[...]).astype(BF16)

        o_ref[...] = jnp.dot(u_ref[...], w_ref[...], preferred_element_type=F32)

    return pl.pallas_call(
        body, name=name, grid=(gi, gj),
        in_specs=[pl.BlockSpec((tm, dm), lambda i, j: (i, 0)), pl.BlockSpec((1, dm), lambda i, j: (0, 0)),
                  pl.BlockSpec((dm, tn), lambda i, j: (0, j))] + r_in_specs,
        out_specs=[pl.BlockSpec((tm, tn), lambda i, j: (i, j)), pl.BlockSpec((tm, dm), lambda i, j: (i, 0))] + r_out_specs,
        out_shape=[SDS((t, n), F32), SDS((t, dm), BF16)] + r_outs, scratch_shapes=r_scratch,
        compiler_params=_cp(dimension_semantics=("arbitrary", "arbitrary")),
    )(x, wn, w, *r_ops)


def _shift_down(tile, halo, s):
    if s == 0:
        return tile
    r = pltpu.roll(tile, s, axis=0)
    h = pltpu.roll(halo, s, axis=0)
    head = jnp.where(_iota(h.shape, 0) < s, h, r[0:8])
    return jnp.concatenate([head, r[8:]], axis=0)


def _shift_up(tile, halo, s):
    if s == 0:
        return tile
    n = tile.shape[0]
    r = pltpu.roll(tile, n - s, axis=0)
    h = pltpu.roll(halo, 8 - s, axis=0)
    tail = jnp.where(_iota(h.shape, 0) >= 8 - s, h, r[n - 8:])
    return jnp.concatenate([r[:n - 8], tail], axis=0)


def _conv_apply(tile, halo, wv, bv, kw):
    acc = bv + wv[kw - 1:kw, :] * tile
    for k in range(kw - 1):
        acc = acc + wv[k:k + 1, :] * _shift_down(tile, halo, kw - 1 - k)
    return acc


def _prev_halo_spec(tm, tc, col0):
    return pl.BlockSpec((8, tc), lambda i, j: (jnp.maximum(i * (tm // 8) - 1, 0), col0 + j))


def _silu_parts(pre):
    sg = _sigmoid(pre)
    return pre * sg, sg * (1.0 + pre * (1.0 - sg))


def _conv_silu_fwd(proj, w, b, *, tm, tc=512):
    t = proj.shape[0]
    c = w.shape[1]
    tm = min(tm, t)
    col0 = OFF_XBC // tc

    def body(x_ref, h_ref, w_ref, b_ref, o_ref, pre_ref):
        halo = jnp.where(pl.program_id(0) > 0, h_ref[...], 0.0)
        pre = _conv_apply(x_ref[...], halo, w_ref[...], b_ref[...], 4)
        o_ref[...] = _silu_parts(pre)[0]
        pre_ref[...] = pre.astype(BF16)

    tile = pl.BlockSpec((tm, tc), lambda i, j: (i, j))
    return pl.pallas_call(
        body, name="ssd_conv_fwd", grid=(t // tm, c // tc),
        in_specs=[pl.BlockSpec((tm, tc), lambda i, j: (i, col0 + j)), _prev_halo_spec(tm, tc, col0),
                  pl.BlockSpec((4, tc), lambda i, j: (0, j)), pl.BlockSpec((1, tc), lambda i, j: (0, j))],
        out_specs=[tile, tile], out_shape=[SDS((t, c), F32), SDS((t, c), BF16)],
        compiler_params=_cp(dimension_semantics=("arbitrary", "arbitrary")),
    )(proj, proj, w, b)


def _conv_silu_bwd1(d_out, pre, *, tm, tc=512):
    t, c = pre.shape
    tm = min(tm, t)

    def body(g_ref, p_ref, o_ref, db_ref):
        i = pl.program_id(1)
        d_pre = g_ref[...] * _silu_parts(p_ref[...].astype(F32))[1]
        o_ref[...] = d_pre.astype(BF16)
        _accumulate(db_ref, i == 0, jnp.sum(d_pre, axis=0, keepdims=True))

    tile = pl.BlockSpec((tm, tc), lambda j, i: (i, j))
    return pl.pallas_call(
        body, name="ssd_conv_bwd1", grid=(c // tc, t // tm), in_specs=[tile, tile],
        out_specs=[tile, pl.BlockSpec((1, tc), lambda j, i: (0, j))],
        out_shape=[SDS((t, c), BF16), SDS((1, c), F32)],
        compiler_params=_cp(dimension_semantics=("arbitrary", "arbitrary")),
    )(d_out, pre)


def _conv_bwd2(name, d_pre, src, src_col0, w, *, tm, tc, out_cols, out_col0, fill=None):
    t, c = d_pre.shape
    kw = w.shape[0]
    tm = min(tm, t)
    ni = t // tm
    col0 = src_col0 // tc
    ocol0 = out_col0 // tc

    def body(g_ref, gn_ref, x_ref, w_ref, *rest):
        o_ref, dw_ref = rest[-2:]
        i = pl.program_id(1)
        g = g_ref[...].astype(F32)
        g_next = jnp.where(i < ni - 1, gn_ref[...].astype(F32)[0:8], 0.0)
        xv = x_ref[...]
        wv = w_ref[...]
        shifted = [_shift_up(g, g_next, kw - 1 - k) for k in range(kw)]
        d_in = wv[0:1, :] * shifted[0]
        for k in range(1, kw):
            d_in = d_in + wv[k:k + 1, :] * shifted[k]
        o_ref[...] = d_in.astype(o_ref.dtype)
        rows = [jnp.sum(shifted[k] * xv, axis=0, keepdims=True) for k in range(kw)]

        @pl.when(i == 0)
        def _():
            for k in range(kw):
                dw_ref[k:k + 1, :] = rows[k]

        @pl.when(i > 0)
        def _():
            for k in range(kw):
                dw_ref[k:k + 1, :] += rows[k]

    in_specs = [pl.BlockSpec((tm, tc), lambda j, i: (i, j)),
                pl.BlockSpec((16, tc), lambda j, i: (jnp.minimum((i + 1) * (tm // 16), t // 16 - 1), j)),
                pl.BlockSpec((tm, tc), lambda j, i: (i, col0 + j)),
                pl.BlockSpec((kw, tc), lambda j, i: (0, j))]
    operands = [d_pre, d_pre, src, w]
    if fill is not None:
        in_specs.append(pl.BlockSpec(memory_space=pl.ANY))
        operands.append(fill)
    return pl.pallas_call(
        body, name=name, grid=(c // tc, ni), in_specs=in_specs,
        out_specs=[pl.BlockSpec((tm, tc), lambda j, i: (i, ocol0 + j)), pl.BlockSpec((kw, tc), lambda j, i: (0, j))],
        out_shape=[SDS((t, out_cols), BF16), SDS((kw, c), F32)],
        input_output_aliases={} if fill is None else {4: 0},
        compiler_params=_cp(dimension_semantics=("arbitrary", "arbitrary")),
    )(*operands)


GELU_C = 0.7978845608028654


def _gelu_parts(v):
    inner = GELU_C * (v + 0.044715 * v * v * v)
    th = jnp.tanh(inner)
    val = 0.5 * v * (1.0 + th)
    grad = 0.5 * (1.0 + th) + 0.5 * v * (1.0 - th * th) * GELU_C * (1.0 + 3.0 * 0.044715 * v * v)
    return val, grad


def _ffn_act_fwd(up_raw, w, b, *, tm, tc=1408):
    t = up_raw.shape[0]
    tm = min(tm, t)
    nj = FFN_D_FF // tc
    halo = lambda i: jnp.maximum(i * (tm // 8) - 1, 0)

    def body(g_ref, gh_ref, v_ref, vh_ref, wg_ref, wv_ref, bg_ref, bv_ref, o_ref, gate_ref, val_ref):
        first = pl.program_id(0) > 0
        gate = _conv_apply(g_ref[...], jnp.where(first, gh_ref[...], 0.0), wg_ref[...], bg_ref[...], 3)
        val = _conv_apply(v_ref[...], jnp.where(first, vh_ref[...], 0.0), wv_ref[...], bv_ref[...], 3)
        o_ref[...] = (_gelu_parts(gate)[0] * val).astype(BF16)
        gate_ref[...] = gate.astype(BF16)
        val_ref[...] = val.astype(BF16)

    tile = pl.BlockSpec((tm, tc), lambda i, j: (i, j))
    return pl.pallas_call(
        body, name="ffn_act_fwd", grid=(t // tm, nj),
        in_specs=[tile, pl.BlockSpec((8, tc), lambda i, j: (halo(i), j)),
                  pl.BlockSpec((tm, tc), lambda i, j: (i, nj + j)), pl.BlockSpec((8, tc), lambda i, j: (halo(i), nj + j)),
                  pl.BlockSpec((3, tc), lambda i, j: (0, j)), pl.BlockSpec((3, tc), lambda i, j: (0, nj + j)),
                  pl.BlockSpec((1, tc), lambda i, j: (0, j)), pl.BlockSpec((1, tc), lambda i, j: (0, nj + j))],
        out_specs=[tile] * 3, out_shape=[SDS((t, FFN_D_FF), BF16)] * 3,
        compiler_params=_cp(dimension_semantics=("arbitrary", "arbitrary")),
    )(up_raw, up_raw, up_raw, up_raw, w, w, b, b)


def _ffn_act_bwd(gate, val, d_act, *, tm, tc=1408):
    t = gate.shape[0]
    tm = min(tm, t)
    nj = FFN_D_FF // tc

    def body(g_ref, v_ref, da_ref, dg_ref, dv_ref, dbg_ref, dbv_ref):
        i = pl.program_id(1)
        val = v_ref[...].astype(F32)
        ge, dge = _gelu_parts(g_ref[...].astype(F32))
        da = da_ref[...].astype(F32)
        d_gate = da * val * dge
        d_val = da * ge
        dg_ref[...] = d_gate.astype(BF16)
        dv_ref[...] = d_val.astype(BF16)
        _accumulate(dbg_ref, i == 0, jnp.sum(d_gate, axis=0, keepdims=True))
        _accumulate(dbv_ref, i == 0, jnp.sum(d_val, axis=0, keepdims=True))

    tile = pl.BlockSpec((tm, tc), lambda j, i: (i, j))
    row = pl.BlockSpec((1, tc), lambda j, i: (0, j))
    return pl.pallas_call(
        body, name="ffn_act_bwd", grid=(nj, t // tm), in_specs=[tile] * 3, out_specs=[tile, tile, row, row],
        out_shape=[SDS((t, FFN_D_FF), BF16), SDS((t, FFN_D_FF), BF16), SDS((1, FFN_D_FF), F32), SDS((1, FFN_D_FF), F32)],
        compiler_params=_cp(dimension_semantics=("arbitrary", "arbitrary")),
    )(gate, val, d_act)


def _softplus(v):
    e = jnp.exp(-jnp.abs(v))
    small = e * (1.0 - 0.5 * e)
    return jnp.maximum(v, 0.0) + jnp.where(e < 1e-4, small, jnp.log(1.0 + e))


def _dt_fwd(proj, bias_pad, *, tm):
    t = proj.shape[0]
    tm = min(tm, t)

    def body(x_ref, b_ref, o_ref):
        o_ref[...] = _softplus(x_ref[...] + b_ref[...])

    return pl.pallas_call(
        body, name="dt_fwd", grid=(t // tm,),
        in_specs=[pl.BlockSpec((tm, 128), lambda i: (i, OFF_DT // 128)), pl.BlockSpec((1, 128), lambda i: (0, 0))],
        out_specs=pl.BlockSpec((tm, 128), lambda i: (i, 0)), out_shape=SDS((t, 128), F32),
        compiler_params=_cp(dimension_semantics=("arbitrary",)),
    )(proj, bias_pad)


def _dt_bwd(d_dt, proj, bias_pad, d_proj, *, tm):
    t = proj.shape[0]
    tm = min(tm, t)

    def body(g_ref, x_ref, b_ref, _, o_ref, db_ref):
        d_raw = g_ref[...] * _sigmoid(x_ref[...] + b_ref[...])
        o_ref[:, 0:128] = d_raw.astype(BF16)
        o_ref[:, 128:512] = jnp.zeros((tm, 384), BF16)
        _accumulate(db_ref, pl.program_id(0) == 0, jnp.sum(d_raw, axis=0, keepdims=True))

    return pl.pallas_call(
        body, name="dt_bwd", grid=(t // tm,),
        in_specs=[pl.BlockSpec((tm, 128), lambda i: (i, 0)), pl.BlockSpec((tm, 128), lambda i: (i, OFF_DT // 128)),
                  pl.BlockSpec((1, 128), lambda i: (0, 0)), pl.BlockSpec(memory_space=pl.ANY)],
        out_specs=[pl.BlockSpec((tm, 512), lambda i: (i, OFF_DT // 512)), pl.BlockSpec((1, 128), lambda i: (0, 0))],
        out_shape=[SDS((t, PROJ_W), BF16), SDS((1, 128), F32)],
        input_output_aliases={3: 0},
        compiler_params=_cp(dimension_semantics=("arbitrary",)),
    )(d_dt, proj, bias_pad, d_proj)


def _split3(v):
    hi = v.astype(BF16)
    r1 = v - hi.astype(F32)
    mid = r1.astype(BF16)
    return hi, mid, (r1 - mid.astype(F32)).astype(BF16)


def _times01(v, m3):
    return jnp.dot(jnp.concatenate(_split3(v), axis=1), m3, preferred_element_type=F32)


def _01times(m3, v):
    return jnp.dot(m3, jnp.concatenate(_split3(v), axis=0), preferred_element_type=F32)


def _ssd_decay(dt_ref, dtT_ref, al_ref, alT_ref, k):
    dt = dt_ref[0]
    a_row = -jnp.exp(al_ref[0])
    adt_t = dtT_ref[0] * (-jnp.exp(alT_ref[0]))
    return dt, a_row, _01times(k['low3'][...], dt * a_row), _times01(adt_t, k['up3v'][...])


def _ssd_specs(nc, rev):
    ci = (lambda c: nc - 1 - c) if rev else (lambda c: c)
    return [pl.BlockSpec((CHUNK, GROUP_W), lambda c, g: (ci(c), g)),
            pl.BlockSpec((1, CHUNK, 128), lambda c, g: (g, ci(c), 0)),
            pl.BlockSpec((1, 8, CHUNK), lambda c, g: (g, 0, ci(c))),
            pl.BlockSpec((1, 1, 128), lambda c, g: (g, 0, 0)),
            pl.BlockSpec((1, 8, 1), lambda c, g: (g, 0, 0)),
            pl.BlockSpec((1, 512), lambda c, g: (0, g))]


NT = (((1,), (1,)), ((), ()))
WIDE = 8 * CHUNK
SSD_CONST_NAMES = ('e128', 'e64', 's64', 'mlo', 'mup', 'low3', 'up3', 'up3v')
SSD_CONST_SHAPES = [pltpu.VMEM((3 * CHUNK, WIDE), BF16), pltpu.VMEM((3 * CHUNK, 512), BF16), pltpu.VMEM((512, CHUNK), BF16),
                    pltpu.VMEM((CHUNK, WIDE), F32), pltpu.VMEM((CHUNK, WIDE), F32), pltpu.VMEM((CHUNK, 3 * CHUNK), BF16),
                    pltpu.VMEM((CHUNK, 3 * CHUNK), BF16), pltpu.VMEM((3 * CHUNK, CHUNK), BF16)]


def _ssd_init_consts(k):
    row, col = _iota((3 * CHUNK, WIDE), 0), _iota((3 * CHUNK, WIDE), 1)
    k['e128'][...] = ((col >> 7) == (row & 127)).astype(BF16)
    k['e64'][...] = ((_iota((3 * CHUNK, 512), 1) >> 6) == (_iota((3 * CHUNK, 512), 0) & 127)).astype(BF16)
    k['s64'][...] = ((_iota((512, CHUNK), 0) >> 6) == _iota((512, CHUNK), 1)).astype(BF16)
    row, col = _iota((CHUNK, WIDE), 0), _iota((CHUNK, WIDE), 1)
    k['mlo'][...] = (row >= (col & 127)).astype(F32)
    k['mup'][...] = (row <= (col & 127)).astype(F32)
    row, col = _iota((CHUNK, 3 * CHUNK), 0), _iota((CHUNK, 3 * CHUNK), 1) & 127
    k['low3'][...] = (row >= col).astype(BF16)
    k['up3'][...] = (row <= col).astype(BF16)
    row, col = _iota((3 * CHUNK, CHUNK), 0) & 127, _iota((3 * CHUNK, CHUNK), 1)
    k['up3v'][...] = (row <= col).astype(BF16)


def _ssd_common(x_ref, dt_ref, dtT_ref, al_ref, alT_ref, k):
    dt, a_row, acs, acs_t = _ssd_decay(dt_ref, dtT_ref, al_ref, alT_ref, k)
    ecol = _times01(acs, k['e128'][...])
    rrow = jnp.concatenate([jnp.broadcast_to(acs_t[j:j + 1, :], (CHUNK, CHUNK)) for j in range(8)], axis=1)
    a64 = _times01(acs, k['e64'][...])
    dt64 = _times01(dt, k['e64'][...])
    a_end64 = a64[CHUNK - 1:CHUNK, :]
    xs = x_ref[:, 0:512]
    return dict(dt=dt, a_row=a_row, acs=acs, seg=ecol - rrow, dt64=dt64, e_a=jnp.exp(a64), decay=jnp.exp(a_end64 - a64),
                e_end64=jnp.exp(a_end64), xs=xs, xdt=xs * dt64, bm=x_ref[:, 512:640], cm=x_ref[:, 640:768])


def _pair_blocks(v):
    lo = _iota((CHUNK, 128), 1) < 64
    out = []
    for i in range(4):
        ch = v[:, i * 128:(i + 1) * 128]
        out.append(jnp.concatenate([jnp.where(lo, ch, 0.0), jnp.where(lo, 0.0, ch)], axis=0).astype(BF16))
    return out


def _tile8(m):
    return jnp.concatenate([m] * 8, axis=1)


def _ssd_fwd(xc, dtg, dtg_t, alog, alog_t, d_exp):
    t = xc.shape[0]
    nc = t // CHUNK

    def body(x_ref, dt_ref, dtT_ref, al_ref, alT_ref, d_ref, y_ref, hs_ref, h_scr, *consts):
        c, g = pl.program_id(0), pl.program_id(1)
        k = dict(zip(SSD_CONST_NAMES, consts))

        @pl.when(jnp.logical_and(c == 0, g == 0))
        def _():
            _ssd_init_consts(k)

        @pl.when(c == 0)
        def _():
            h_scr[g] = jnp.zeros((SSD_D_STATE, 512), F32)

        v = _ssd_common(x_ref, dt_ref, dtT_ref, al_ref, alT_ref, k)
        b16, c16 = v['bm'].astype(BF16), v['cm'].astype(BF16)
        cb = lax.dot_general(c16, b16, NT, preferred_element_type=F32)
        m16 = (jnp.exp(jnp.minimum(v['seg'], 0.0)) * k['mlo'][...] * _tile8(cb)).astype(BF16)
        xbd = _pair_blocks(v['xdt'])
        y_diag = jnp.concatenate([jnp.dot(m16[:, i * 256:(i + 1) * 256], xbd[i], preferred_element_type=F32)
                                  for i in range(4)], axis=1)
        ht = h_scr[g]
        y_off = jnp.dot(c16, ht.astype(BF16), preferred_element_type=F32)
        y_ref[...] = y_diag + v['e_a'] * y_off + d_ref[...] * v['xs']
        st = jnp.dot(v['bm'].T.astype(BF16), (v['xdt'] * v['decay']).astype(BF16), preferred_element_type=F32)
        hs_ref[0, 0] = ht
        h_scr[g] = ht * v['e_end64'] + st

    return pl.pallas_call(
        body, name="ssd_fwd", grid=(nc, SSD_N_GROUPS), in_specs=_ssd_specs(nc, False),
        out_specs=[pl.BlockSpec((CHUNK, 512), lambda c, g: (c, g)),
                   pl.BlockSpec((1, 1, SSD_D_STATE, 512), lambda c, g: (c, g, 0, 0))],
        out_shape=[SDS((t, SSD_D_INNER), F32), SDS((nc, SSD_N_GROUPS, SSD_D_STATE, 512), F32)],
        scratch_shapes=[pltpu.VMEM((SSD_N_GROUPS, SSD_D_STATE, 512), F32)] + SSD_CONST_SHAPES,
        compiler_params=_cp(dimension_semantics=("arbitrary", "arbitrary")),
    )(xc, dtg, dtg_t, alog, alog_t, d_exp)


def _ssd_bwd(xc, dtg, dtg_t, alog, alog_t, d_exp, d_y, hs, ride=None):
    t = xc.shape[0]
    nc = t // CHUNK

    r_ops, r_in_specs, r_outs, r_out_specs, r_scratch = _ride_parts(ride)

    def body(x_ref, dt_ref, dtT_ref, al_ref, alT_ref, d_ref, dy_ref, hs_ref, *rest):
        r_in, rest = rest[:len(r_ops)], rest[len(r_ops):]
        dx_ref, ddt_ref, dal_ref, dd_ref = rest[:4]
        r_out, rest = rest[4:4 + len(r_outs)], rest[4 + len(r_outs):]
        g_scr, consts, sems = rest[0], rest[1:1 + len(SSD_CONST_NAMES)], rest[1 + len(SSD_CONST_NAMES):]
        c, g = pl.program_id(0), pl.program_id(1)
        _ride_run(ride, (c == 0) & (g == 0), (c == nc - 1) & (g == SSD_N_GROUPS - 1), r_in, r_out, sems)
        k = dict(zip(SSD_CONST_NAMES, consts))
        s64, mlo, mup = k['s64'], k['mlo'], k['mup']

        @pl.when(jnp.logical_and(c == 0, g == 0))
        def _():
            _ssd_init_consts(k)

        @pl.when(c == 0)
        def _():
            g_scr[g] = jnp.zeros((SSD_D_STATE, 512), F32)

        v = _ssd_common(x_ref, dt_ref, dtT_ref, al_ref, alT_ref, k)
        dt, a_row, xs, xdt, e_a, decay = v['dt'], v['a_row'], v['xs'], v['xdt'], v['e_a'], v['decay']
        row, col = _iota((CHUNK, CHUNK), 0), _iota((CHUNK, CHUNK), 1)
        b16, c16 = v['bm'].astype(BF16), v['cm'].astype(BF16)
        ct16 = v['cm'].T.astype(BF16)
        cb = lax.dot_general(c16, b16, NT, preferred_element_type=F32)
        cbt = lax.dot_general(b16, c16, NT, preferred_element_type=F32)
        lmat = jnp.exp(jnp.minimum(v['seg'], 0.0)) * mlo[...]
        lmat_t = jnp.exp(jnp.minimum(-v['seg'], 0.0)) * mup[...]
        mmat, mmat_t = lmat * _tile8(cb), lmat_t * _tile8(cbt)
        mt16 = mmat_t.astype(BF16)
        dy = dy_ref[...]
        dye, xdec = dy * e_a, xdt * decay
        dy16, dye16, xdec16 = dy.astype(BF16), dye.astype(BF16), xdec.astype(BF16)
        xdt16 = xdt.astype(BF16)
        ht, gt = hs_ref[0, 0], g_scr[g]
        ht16, gt16 = ht.astype(BF16), gt.astype(BF16)
        xbd, dybd = _pair_blocks(xdt), _pair_blocks(dy)
        d_m, d_mt, d_x = [], [], []
        for i in range(4):
            csl = slice(i * 128, (i + 1) * 128)
            d_m.append(lax.dot_general(dy16[:, csl], xbd[i], NT, preferred_element_type=F32))
            d_mt.append(lax.dot_general(xdt16[:, csl], dybd[i], NT, preferred_element_type=F32))
            d_x.append(jnp.dot(mt16[:, i * 256:(i + 1) * 256], dybd[i], preferred_element_type=F32))
        d_m, d_mt, d_x = jnp.concatenate(d_m, axis=1), jnp.concatenate(d_mt, axis=1), jnp.concatenate(d_x, axis=1)

        def head_sum(m):
            acc = m[:, 0:CHUNK]
            for j in range(1, 8):
                acc = acc + m[:, j * CHUNK:(j + 1) * CHUNK]
            return acc

        def seg64(p):
            return jnp.dot(p.astype(BF16), s64[...], preferred_element_type=F32)

        d_cb16 = head_sum(d_m * lmat).astype(BF16)
        d_cbt16 = head_sum(d_mt * lmat_t).astype(BF16)
        dseg = d_m * mmat - d_mt * mmat_t
        da_seg = jnp.zeros((CHUNK, CHUNK), F32)
        for j in range(8):
            da_seg = jnp.where(col == j, jnp.sum(dseg[:, j * CHUNK:(j + 1) * CHUNK], axis=1, keepdims=True), da_seg)
        ch = jnp.dot(c16, ht16, preferred_element_type=F32)
        bg = jnp.dot(b16, gt16, preferred_element_type=F32)
        d_x = d_x + decay * bg
        d_decay = seg64(xdec * bg)
        e_end = jnp.exp(v['acs'][CHUNK - 1:CHUNK, :])
        d_end = e_end * jnp.sum(seg64(gt * ht), axis=0, keepdims=True) + jnp.sum(d_decay, axis=0, keepdims=True)
        d_a = seg64(dye * ch) - d_decay + da_seg + jnp.where(row == CHUNK - 1, d_end, 0.0)
        dx_ref[:, 0:512] = d_x * v['dt64'] + d_ref[...] * dy
        dx_ref[:, 640:768] = (lax.dot_general(dye16, ht16, NT, preferred_element_type=F32)
                              + jnp.dot(d_cb16, b16, preferred_element_type=F32))
        dx_ref[:, 512:640] = (lax.dot_general(xdec16, gt16, NT, preferred_element_type=F32)
                              + jnp.dot(d_cbt16, c16, preferred_element_type=F32))
        g_scr[g] = gt * v['e_end64'] + jnp.dot(ct16, dye16, preferred_element_type=F32)
        d_adt = _01times(k['up3'][...], d_a)
        ddt_ref[0] = d_adt * a_row + seg64(d_x * xs)
        d_alog = jnp.sum(d_adt * dt, axis=0, keepdims=True) * a_row
        dd_row = jnp.sum(seg64(dy * xs), axis=0, keepdims=True)
        first = c == 0

        @pl.when(first)
        def _():
            dal_ref[g] = d_alog
            dd_ref[g] = dd_row

        @pl.when(jnp.logical_not(first))
        def _():
            dal_ref[g] += d_alog
            dd_ref[g] += dd_row

    rc = lambda c: nc - 1 - c
    whole = pl.BlockSpec((SSD_N_GROUPS, 1, 128), lambda c, g: (0, 0, 0))
    return pl.pallas_call(
        body, name="ssd_bwd", grid=(nc, SSD_N_GROUPS),
        in_specs=_ssd_specs(nc, True) + [pl.BlockSpec((CHUNK, 512), lambda c, g: (rc(c), g)),
                                        pl.BlockSpec((1, 1, SSD_D_STATE, 512), lambda c, g: (rc(c), g, 0, 0))] + r_in_specs,
        out_specs=[pl.BlockSpec((CHUNK, GROUP_W), lambda c, g: (rc(c), g)),
                   pl.BlockSpec((1, CHUNK, 128), lambda c, g: (g, rc(c), 0)), whole, whole] + r_out_specs,
        out_shape=[SDS((t, SSD_CONV_DIM), F32), SDS((SSD_N_GROUPS, t, 128), F32),
                   SDS((SSD_N_GROUPS, 1, 128), F32), SDS((SSD_N_GROUPS, 1, 128), F32)] + r_outs,
        scratch_shapes=[pltpu.VMEM((SSD_N_GROUPS, SSD_D_STATE, 512), F32)] + SSD_CONST_SHAPES + r_scratch,
        compiler_params=_cp(dimension_semantics=("arbitrary", "arbitrary")),
    )(xc, dtg, dtg_t, alog, alog_t, d_exp, d_y, hs, *r_ops)


def _gated_norm_fwd(y, proj, w, *, tm):
    t = y.shape[0]
    tm = min(tm, t)

    def body(y_ref, z_ref, w_ref, o_ref):
        gv = y_ref[...] * _silu_parts(z_ref[...])[0]
        r = lax.rsqrt(jnp.mean(gv * gv, axis=-1, keepdims=True) + NORM_EPS)
        o_ref[...] = (gv * r * w_ref[...]).astype(BF16)

    tile = pl.BlockSpec((tm, 512), lambda i, g: (i, g))
    return pl.pallas_call(
        body, name="gated_norm_fwd", grid=(t // tm, SSD_N_GROUPS),
        in_specs=[tile, pl.BlockSpec((tm, 512), lambda i, g: (i, OFF_Z // 512 + g)),
                  pl.BlockSpec((1, 512), lambda i, g: (0, g))], out_specs=tile,
        out_shape=SDS((t, SSD_D_INNER), BF16),
        compiler_params=_cp(dimension_semantics=("arbitrary", "arbitrary")),
    )(y, proj, w)


def _rope(ch, cos_t, sin_t):
    first = (_iota(ch.shape, 1) & 32) == 0
    partner = jnp.where(first, pltpu.roll(ch, 96, axis=1), pltpu.roll(ch, 32, axis=1))
    return ch * cos_t + partner * sin_t


def _rope_qkv(proj, cos_t, sin_t, *, tm):
    t = proj.shape[0]
    tm = min(tm, t)

    def body(q_ref, k_ref, v_ref, c_ref, s_ref, qr_ref, kp_ref, vp_ref, kt_ref, vt_ref):
        cv, sv = c_ref[...], s_ref[...]
        lo = _iota((tm, 128), 1) < 64
        for m in range(8):
            sl = slice(m * 128, (m + 1) * 128)
            qr_ref[:, sl] = (_rope(q_ref[:, sl], cv, sv) * 0.125).astype(BF16)
        for m2 in range(2):
            sl = slice(m2 * 128, (m2 + 1) * 128)
            for src, dst, dst_t in ((_rope(k_ref[:, sl], cv, sv), kp_ref, kt_ref), (v_ref[:, sl], vp_ref, vt_ref)):
                sw = pltpu.roll(src, 64, axis=1)
                padded = (jnp.where(lo, src, 0.0), jnp.where(lo, 0.0, sw), jnp.where(lo, sw, 0.0), jnp.where(lo, 0.0, src))
                for i, pad in enumerate(padded):
                    rows = slice((4 * m2 + i) * 128, (4 * m2 + i + 1) * 128)
                    dst[:, rows] = pad.astype(BF16)
                    dst_t[rows, :] = pad.T.astype(BF16)

    return pl.pallas_call(
        body, name="rope_qkv", grid=(t // tm,),
        in_specs=[pl.BlockSpec((tm, 1024), lambda i: (i, OFF_Q // 1024)), pl.BlockSpec((tm, 256), lambda i: (i, OFF_K // 256)),
                  pl.BlockSpec((tm, 256), lambda i: (i, OFF_V // 256)), pl.BlockSpec((tm, 128), lambda i: (i, 0)),
                  pl.BlockSpec((tm, 128), lambda i: (i, 0))],
        out_specs=[pl.BlockSpec((tm, 1024), lambda i: (i, 0))] * 3 + [pl.BlockSpec((1024, tm), lambda i: (0, i))] * 2,
        out_shape=[SDS((t, 1024), BF16)] * 3 + [SDS((1024, t), BF16)] * 2,
        compiler_params=_cp(dimension_semantics=("arbitrary",)),
    )(proj, proj, proj, cos_t, sin_t)


def _attn_valid(n):
    kj, qi = _iota((2 * CHUNK, CHUNK), 0), _iota((2 * CHUNK, CHUNK), 1)
    return (kj > qi) & (kj <= qi + CHUNK) & ((n > 0) | (kj >= CHUNK))


def _attn_fwd(qr, kp, vt, sinks):
    t = qr.shape[0]
    nb = t // CHUNK

    def body(q_ref, kc_ref, kprev_ref, vc_ref, vprev_ref, sk_ref, o_ref, lse_ref):
        n = pl.program_id(0)
        valid = _attn_valid(n)
        head_row = _iota((16, CHUNK), 0)
        lse_all = jnp.zeros((16, CHUNK), F32)
        for m in range(8):
            g = m // 2
            qch = q_ref[:, m * 128:(m + 1) * 128]
            o_t = jnp.zeros((128, CHUNK), F32)
            for e in range(2):
                h = 2 * m + e
                sl = slice((2 * g + e) * 128, (2 * g + e + 1) * 128)
                kk = jnp.concatenate([kprev_ref[:, sl], kc_ref[:, sl]], axis=0)
                vv_t = jnp.concatenate([vprev_ref[sl, :], vc_ref[sl, :]], axis=1)
                s = jnp.where(valid, lax.dot_general(kk, qch, NT, preferred_element_type=F32), NEG)
                sink = sk_ref[0:1, h:h + 1]
                mx = jnp.maximum(jnp.max(s, axis=0, keepdims=True), sink)
                p = jnp.exp(s - mx)
                den = jnp.sum(p, axis=0, keepdims=True) + jnp.exp(sink - mx)
                o_t = o_t + jnp.dot(vv_t, p.astype(BF16), preferred_element_type=F32) * (1.0 / den)
                lse_all = jnp.where(head_row == h, mx + jnp.log(den), lse_all)
            o_ref[:, m * 128:(m + 1) * 128] = o_t.T.astype(BF16)
        lse_ref[0] = lse_all

    cur = pl.BlockSpec((CHUNK, 1024), lambda n: (n, 0))
    prev = pl.BlockSpec((CHUNK, 1024), lambda n: (jnp.maximum(n - 1, 0), 0))
    cur_t = pl.BlockSpec((1024, CHUNK), lambda n: (0, n))
    prev_t = pl.BlockSpec((1024, CHUNK), lambda n: (0, jnp.maximum(n - 1, 0)))
    return pl.pallas_call(
        body, name="attn_fwd", grid=(nb,),
        in_specs=[cur, cur, prev, cur_t, prev_t, pl.BlockSpec((1, 128), lambda n: (0, 0))],
        out_specs=[cur, pl.BlockSpec((1, 16, CHUNK), lambda n: (n, 0, 0))],
        out_shape=[SDS((t, 1024), BF16), SDS((nb, 16, CHUNK), F32)],
        compiler_params=_cp(dimension_semantics=("arbitrary",)),
    )(qr, kp, kp, vt, vt, sinks)


def _attn_bwd(qr, kp, vp, kt, d_o, o, lse, sinks, cos_t, sin_t, d_proj, ride=None):
    t = qr.shape[0]
    nb = t // CHUNK

    r_ops, r_in_specs, r_outs, r_out_specs, r_scratch = _ride_parts(ride)

    def body(q_ref, kc_ref, kprev_ref, vc_ref, vprev_ref, ktc_ref, ktprev_ref, do_ref, o_ref, lse_ref, sk_ref,
             c_ref, s_ref, cp_ref, sp_ref, _, *rest):
        r_in, rest = rest[:len(r_ops)], rest[len(r_ops):]
        dqkv_ref, dsk_ref = rest[:2]
        r_out, rest = rest[2:2 + len(r_outs)], rest[2 + len(r_outs):]
        acc_k, acc_v, dq_scr = rest[:3]
        n = pl.program_id(0)
        _ride_run(ride, n == 0, n == nb, r_in, r_out, rest[3:])
        lane = _iota((CHUNK, 128), 1)
        lo = lane < 64
        lane1 = _iota((1, 128), 1)

        @pl.when(n == 0)
        def _():
            acc_k[...] = jnp.zeros_like(acc_k)
            acc_v[...] = jnp.zeros_like(acc_v)
            dsk_ref[...] = jnp.zeros((1, 128), F32)

        @pl.when(n > 0)
        def _():
            dqkv_ref[:, 0:1024] = dq_scr[...]
            for r in range(8):
                acc_k[r, 0:CHUNK] = acc_k[r, CHUNK:2 * CHUNK]
                acc_v[r, 0:CHUNK] = acc_v[r, CHUNK:2 * CHUNK]
                acc_k[r, CHUNK:2 * CHUNK] = jnp.zeros((CHUNK, 128), F32)
                acc_v[r, CHUNK:2 * CHUNK] = jnp.zeros((CHUNK, 128), F32)

        @pl.when(n < nb)
        def _():
            valid = _attn_valid(n)
            lse_all = lse_ref[0]
            dsk = jnp.zeros((1, 128), F32)
            for m in range(8):
                g = m // 2
                csl = slice(m * 128, (m + 1) * 128)
                qch = q_ref[:, csl]
                doch = do_ref[:, csl]
                prod_t = (doch.astype(F32) * o_ref[:, csl].astype(F32)).T
                dq_t = jnp.zeros((128, CHUNK), F32)
                for e in range(2):
                    h = 2 * m + e
                    sl = slice((2 * g + e) * 128, (2 * g + e + 1) * 128)
                    kk = jnp.concatenate([kprev_ref[:, sl], kc_ref[:, sl]], axis=0)
                    vv = jnp.concatenate([vprev_ref[:, sl], vc_ref[:, sl]], axis=0)
                    kk_t = jnp.concatenate([ktprev_ref[sl, :], ktc_ref[sl, :]], axis=1)
                    lse_h = lse_all[h:h + 1, :]
                    s = lax.dot_general(kk, qch, NT, preferred_element_type=F32)
                    p = jnp.exp(jnp.where(valid, s, NEG) - lse_h)
                    delta = jnp.sum(prod_t[64 * e:64 * (e + 1)], axis=0, keepdims=True)
                    d_p = lax.dot_general(vv, doch, NT, preferred_element_type=F32)
                    d_s16 = (p * (d_p - delta)).astype(BF16)
                    dq_t = dq_t + jnp.dot(kk_t, d_s16, preferred_element_type=F32)
                    acc_k[2 * g + e] += jnp.dot(d_s16, qch, preferred_element_type=F32)
                    acc_v[2 * g + e] += jnp.dot(p.astype(BF16), doch, preferred_element_type=F32)
                    p_sink = jnp.exp(sk_ref[0:1, h:h + 1] - lse_h)
                    dsk = jnp.where(lane1 == h, -jnp.sum(p_sink * delta), dsk)
                dq_scr[:, csl] = (_rope(dq_t.T, c_ref[...], -s_ref[...]) * 0.125).astype(BF16)
            dsk_ref[...] += dsk

        @pl.when(n > 0)
        def _():
            for m2 in range(2):
                halves = []
                for g in (2 * m2, 2 * m2 + 1):
                    for acc in (acc_k, acc_v):
                        comb = jnp.where(lo, acc[2 * g, 0:CHUNK], acc[2 * g + 1, 0:CHUNK])
                        halves.append(comb + pltpu.roll(comb, 64, axis=1))
                d_kr = jnp.where(lo, halves[0], halves[2])
                d_v = jnp.where(lo, halves[1], halves[3])
                dqkv_ref[:, OFF_K + m2 * 128:OFF_K + (m2 + 1) * 128] = _rope(d_kr, cp_ref[...], -sp_ref[...]).astype(BF16)
                dqkv_ref[:, OFF_V + m2 * 128:OFF_V + (m2 + 1) * 128] = d_v.astype(BF16)

    qn = lambda n: jnp.minimum(n, nb - 1)
    pn = lambda n: jnp.maximum(jnp.minimum(n, nb) - 1, 0)
    cur = pl.BlockSpec((CHUNK, 1024), lambda n: (qn(n), 0))
    prev = pl.BlockSpec((CHUNK, 1024), lambda n: (pn(n), 0))
    cur128 = pl.BlockSpec((CHUNK, 128), lambda n: (qn(n), 0))
    prev128 = pl.BlockSpec((CHUNK, 128), lambda n: (pn(n), 0))
    cur_t = pl.BlockSpec((1024, CHUNK), lambda n: (0, qn(n)))
    prev_t = pl.BlockSpec((1024, CHUNK), lambda n: (0, pn(n)))
    one = pl.BlockSpec((1, 128), lambda n: (0, 0))
    return pl.pallas_call(
        body, name="attn_bwd", grid=(nb + 1,),
        in_specs=[cur, cur, prev, cur, prev, cur_t, prev_t, cur, cur, pl.BlockSpec((1, 16, CHUNK), lambda n: (qn(n), 0, 0)),
                  one, cur128, cur128, prev128, prev128, pl.BlockSpec(memory_space=pl.ANY)] + r_in_specs,
        out_specs=[pl.BlockSpec((CHUNK, 1536), lambda n: (pn(n), 0)), one] + r_out_specs,
        out_shape=[SDS((t, PROJ_W), BF16), SDS((1, 128), F32)] + r_outs,
        scratch_shapes=[pltpu.VMEM((8, 2 * CHUNK, 128), F32), pltpu.VMEM((8, 2 * CHUNK, 128), F32),
                        pltpu.VMEM((CHUNK, 1024), BF16)] + r_scratch,
        input_output_aliases={15: 0},
        compiler_params=_cp(dimension_semantics=("arbitrary",)),
    )(qr, kp, kp, vp, vp, kt, kt, d_o, o, lse, sinks, cos_t, sin_t, cos_t, sin_t, d_proj, *r_ops)


def _adamw(name, w, g, m, v, *, tr):
    rows, cols = w.shape
    tr = min(tr, rows)
    assert rows % tr == 0

    def body(w_ref, g_ref, m_ref, v_ref, d_ref, nm_ref, nv_ref):
        gv = g_ref[...]
        nm = ADAM_B1 * m_ref[...] + (1.0 - ADAM_B1) * gv
        nv = ADAM_B2 * v_ref[...] + (1.0 - ADAM_B2) * (gv * gv)
        m_hat = nm / (1.0 - ADAM_B1 ** ADAM_STEP)
        v_hat = nv / (1.0 - ADAM_B2 ** ADAM_STEP)
        d_ref[...] = -ADAM_LR * (m_hat / (jnp.sqrt(v_hat) + ADAM_EPS) + ADAM_WD * w_ref[...])
        nm_ref[...] = nm
        nv_ref[...] = nv

    tile = pl.BlockSpec((tr, cols), lambda i: (i, 0))
    return pl.pallas_call(
        body, name=name, grid=(rows // tr,), in_specs=[tile] * 4, out_specs=[tile] * 3,
        out_shape=[SDS((rows, cols), F32)] * 3, compiler_params=_cp(dimension_semantics=("arbitrary",)),
    )(w, g, m, v)


def _local_step(x, cos_t, sin_t, tgt, wb, ps, late=None, rides=None):
    t = x.shape[0]
    tm = min(512, t)
    tmw = min(1024, t)
    ij = lambda i, j, k: (i, j)
    i0 = lambda i, j, k: (i, 0)
    c0 = lambda i, j, k: (0, 0)
    cj = lambda i, j, k: (0, j)
    rides = rides or (lambda group, grads: None)
    rode = {}

    tkt = min(2048, t)
    proj, u, *arrived = _norm_mm("in_proj", x, ps['norm_mix_pre_w'], wb['cat'], tm=tmw, tn=1024,
                                 ride=late[0] if late else None)
    if late:
        more_wb, more_ps = late[1](arrived)
        wb, ps = {**wb, **more_wb}, {**ps, **more_ps}
    xc, xc_pre = _conv_silu_fwd(proj, ps['ssd_conv_w'], ps['ssd_conv_b'], tm=tm)
    bias_pad = jnp.pad(ps['ssd_dt_bias'], ((0, 0), (0, 96)))
    dt = _dt_fwd(proj, bias_pad, tm=tmw)
    dt32 = dt[:, :SSD_N_HEADS].reshape(t, SSD_N_GROUPS, 8)
    dtg = jnp.pad(dt32.transpose(1, 0, 2), ((0, 0), (0, 0), (0, 120)))
    dtg_t = dt32.transpose(1, 2, 0)
    alog = jnp.pad(ps['ssd_a_log'].reshape(SSD_N_GROUPS, 1, 8), ((0, 0), (0, 0), (0, 120)))
    alog_t = ps['ssd_a_log'].reshape(SSD_N_GROUPS, 8, 1)
    d_exp = jnp.repeat(ps['ssd_d'], SSD_HEAD_DIM, axis=1)
    y, hs = _ssd_fwd(xc, dtg, dtg_t, alog, alog_t, d_exp)
    gn = _gated_norm_fwd(y, proj, ps['ssd_norm_w'], tm=tm)
    qr, kp, vp, kt, vt = _rope_qkv(proj, cos_t, sin_t, tm=tm)
    sinks = jnp.pad(ps['attn_sinks'], ((0, 0), (0, 112)))
    ao, lse = _attn_fwd(qr, kp, vt, sinks)
    y_attn = _mm_plain("attn_out", ao, wb['ao'], tm=tmw, tn=512, tk=1024)

    def merge_ep(acc, i, j, ins, outs):
        gs, ga, ya = ins
        outs[0][...] = (_sigmoid(gs[...]) * acc + _sigmoid(ga[...]) * ya[...]).astype(BF16)
        outs[1][...] = acc

    merged, y_ssd = _mm_call(
        "ssd_out_merge", gn, wb['so'], tm=tmw, tn=512, tk=2048, epilogue=merge_ep,
        extra_in=[(proj, (tmw, 512), lambda i, j, k: (i, OFF_GS // 512 + j)),
                  (proj, (tmw, 512), lambda i, j, k: (i, OFF_GA // 512 + j)), (y_attn, (tmw, 512), ij)],
        outs=[((t, D_MODEL), BF16, (tmw, 512), ij), ((t, D_MODEL), F32, (tmw, 512), ij)])

    def mix_ep(acc, i, j, ins, outs):
        xv, wn = ins
        r = lax.rsqrt(jnp.mean(acc * acc, axis=-1, keepdims=True) + NORM_EPS)
        outs[0][...] = xv[...] + acc * r * wn[...]
        outs[1][...] = acc

    x1, mmix = _mm_call(
        "mix_out", merged, wb['mix'], tm=tm, tn=D_MODEL, tk=1024, epilogue=mix_ep,
        extra_in=[(x, (tm, D_MODEL), i0), (ps['norm_mix_post_w'], (1, D_MODEL), c0)],
        outs=[((t, D_MODEL), F32, (tm, D_MODEL), i0), ((t, D_MODEL), F32, (tm, D_MODEL), i0)])

    up_raw, h = _norm_mm("ffn_up", x1, ps['norm_ffn_pre_w'], wb['up'], tm=tmw, tn=1408)
    act, ffn_gate, ffn_val = _ffn_act_fwd(up_raw, ps['ffn_conv_w'], ps['ffn_conv_b'], tm=min(256, t))

    def loss_ep(acc, i, j, ins, outs):
        x1v, tg, wn = ins
        d_ff_ref, dout_ref, loss_ref, dw_ref = outs
        wv = wn[...]
        r = lax.rsqrt(jnp.mean(acc * acc, axis=-1, keepdims=True) + NORM_EPS)
        err = x1v[...] + acc * r * wv - tg[...]
        dout = err * (1.0 / D_MODEL)
        dout_ref[...] = dout
        d_ff, dw = _rms_bwd(acc, wv, dout)
        d_ff_ref[...] = d_ff.astype(BF16)
        _accumulate(dw_ref, i == 0, dw)
        _accumulate(loss_ref, i == 0, jnp.sum(err * err, keepdims=True) * (0.5 / D_MODEL))

    d_ff, dout, loss, g_norm_ffn_post = _mm_call(
        "ffn_down_loss", act, wb['dn'], tm=tm, tn=D_MODEL, tk=FFN_D_FF, epilogue=loss_ep,
        extra_in=[(x1, (tm, D_MODEL), i0), (tgt, (tm, D_MODEL), i0), (ps['norm_ffn_post_w'], (1, D_MODEL), c0)],
        outs=[((t, D_MODEL), BF16, (tm, D_MODEL), i0), ((t, D_MODEL), F32, (tm, D_MODEL), i0),
              ((1, 1), F32, (1, 1), c0), ((1, D_MODEL), F32, (1, D_MODEL), c0)])

    d_act = _mm_plain("d_act", d_ff, wb['dn_t'], tm=tmw, tn=1408, tk=1024, out_dtype=BF16)
    g_w_down = _mm_plain("g_w_down", act, d_ff, tm=1408, tn=1024, tk=tkt, trans_a=True, out_dtype=BF16)
    d_gate, d_val, db_g, db_v = _ffn_act_bwd(ffn_gate, ffn_val, d_act, tm=min(256, t))
    d_up_raw, gcw_g = _conv_bwd2("ffn_conv_bwd2_gate", d_gate, up_raw, 0, ps['ffn_conv_w'][:, :FFN_D_FF], tm=min(256, t),
                                 tc=1408, out_cols=2 * FFN_D_FF, out_col0=0)
    d_up_raw, gcw_v = _conv_bwd2("ffn_conv_bwd2_val", d_val, up_raw, FFN_D_FF, ps['ffn_conv_w'][:, FFN_D_FF:], tm=min(256, t),
                                 tc=1408, out_cols=2 * FFN_D_FF, out_col0=FFN_D_FF, fill=d_up_raw)
    g_ffn_conv_w = jnp.concatenate([gcw_g, gcw_v], axis=1)

    def dx1_ep(acc, i, j, ins, outs):
        x1v, wpre, dout_v, mmv, wpost = ins
        d_x1_ref, d_mm_ref, dwpre_ref, dwpost_ref = outs
        d_n, dw_pre = _rms_bwd(x1v[...], wpre[...], acc)
        d_x1 = dout_v[...] + d_n
        d_x1_ref[...] = d_x1
        d_mm, dw_post = _rms_bwd(mmv[...], wpost[...], d_x1)
        d_mm_ref[...] = d_mm.astype(BF16)
        _accumulate(dwpre_ref, i == 0, dw_pre)
        _accumulate(dwpost_ref, i == 0, dw_post)

    d_x1, d_mm, g_norm_ffn_pre, g_norm_mix_post = _mm_call(
        "d_h", d_up_raw, wb['up_t'], tm=tm, tn=D_MODEL, tk=FFN_D_FF, epilogue=dx1_ep,
        extra_in=[(x1, (tm, D_MODEL), i0), (ps['norm_ffn_pre_w'], (1, D_MODEL), c0), (dout, (tm, D_MODEL), i0),
                  (mmix, (tm, D_MODEL), i0), (ps['norm_mix_post_w'], (1, D_MODEL), c0)],
        outs=[((t, D_MODEL), F32, (tm, D_MODEL), i0), ((t, D_MODEL), BF16, (tm, D_MODEL), i0),
              ((1, D_MODEL), F32, (1, D_MODEL), c0), ((1, D_MODEL), F32, (1, D_MODEL), c0)])
    g_w_up_t = _mm_plain("g_w_up", d_up_raw, h, tm=1408, tn=1024, tk=tkt, trans_a=True, out_dtype=BF16)
    ride_ffn = rides('ffn', {'ffn_w_up': g_w_up_t, 'ffn_w_down': g_w_down})

    def dmerge_ep(acc, i, j, ins, outs):
        gs, ga, ys, ya = ins
        sg_s, sg_a = _sigmoid(gs[...]), _sigmoid(ga[...])
        outs[0][...] = (acc * sg_s).astype(BF16)
        outs[1][...] = (acc * sg_a).astype(BF16)
        outs[2][:, 0:D_MODEL] = (acc * ys[...] * sg_s * (1.0 - sg_s)).astype(BF16)
        outs[2][:, D_MODEL:2 * D_MODEL] = (acc * ya[...] * sg_a * (1.0 - sg_a)).astype(BF16)

    d_yssd, d_yattn, d_proj = _mm_call(
        "d_merged", d_mm, wb['mix_t'], tm=tm, tn=D_MODEL, tk=1024, epilogue=dmerge_ep,
        extra_in=[(proj, (tm, D_MODEL), lambda i, j, k: (i, OFF_GS // D_MODEL)),
                  (proj, (tm, D_MODEL), lambda i, j, k: (i, OFF_GA // D_MODEL)), (y_ssd, (tm, D_MODEL), i0), (y_attn, (tm, D_MODEL), i0)],
        outs=[((t, D_MODEL), BF16, (tm, D_MODEL), i0), ((t, D_MODEL), BF16, (tm, D_MODEL), i0),
              ((t, PROJ_W), BF16, (tm, 2 * D_MODEL), lambda i, j, k: (i, OFF_GS // (2 * D_MODEL)))])
    g_w_mix = _mm_plain("g_w_mix", merged, d_mm, tm=1024, tn=1024, tk=tkt, trans_a=True, out_dtype=BF16)

    def dgn_ep(acc, i, j, ins, outs):
        yv, zv, wn = ins
        d_y_ref, d_z_ref, dw_ref = outs
        zz = zv[...]
        sz = _sigmoid(zz)
        silu = zz * sz
        gv = yv[...] * silu
        r = lax.rsqrt(jnp.mean(gv * gv, axis=-1, keepdims=True) + NORM_EPS)
        gh = gv * r
        dgh = acc * wn[...]
        dg = r * (dgh - gh * jnp.mean(dgh * gh, axis=-1, keepdims=True))
        d_y_ref[...] = dg * silu
        d_z_ref[...] = (dg * yv[...] * (sz * (1.0 + zz * (1.0 - sz)))).astype(BF16)
        dw = jnp.sum(acc * gh, axis=0, keepdims=True)

        @pl.when(i == 0)
        def _():
            dw_ref[j] = dw

        @pl.when(i > 0)
        def _():
            dw_ref[j] += dw

    d_y, d_proj, g_ssd_norm = _mm_call(
        "d_gn", d_yssd, wb['so_t'], tm=tm, tn=512, tk=1024, epilogue=dgn_ep, fill=(d_proj, 1),
        extra_in=[(y, (tm, 512), ij), (proj, (tm, 512), lambda i, j, k: (i, OFF_Z // 512 + j)), (ps['ssd_norm_w'], (1, 512), cj)],
        outs=[((t, SSD_D_INNER), F32, (tm, 512), ij), ((t, PROJ_W), BF16, (tm, 512), lambda i, j, k: (i, OFF_Z // 512 + j)),
              ((SSD_N_GROUPS, 1, 512), F32, (SSD_N_GROUPS, 1, 512), lambda i, j, k: (0, 0, 0))])
    g_ssd_norm = g_ssd_norm.reshape(1, SSD_D_INNER)
    g_w_so = _mm_plain("g_w_so", gn, d_yssd, tm=1024, tn=1024, tk=tkt, trans_a=True, out_dtype=BF16)
    d_xc, d_dtg, d_alog, d_dd, *rode['ffn'] = _ssd_bwd(xc, dtg, dtg_t, alog, alog_t, d_exp, d_y, hs, ride=ride_ffn)
    d_pre, g_ssd_conv_b = _conv_silu_bwd1(d_xc, xc_pre, tm=tm)
    d_proj, g_ssd_conv_w = _conv_bwd2("ssd_conv_bwd2", d_pre, proj, OFF_XBC, ps['ssd_conv_w'], tm=tm, tc=512,
                                      out_cols=PROJ_W, out_col0=OFF_XBC, fill=d_proj)
    d_dt = jnp.pad(d_dtg[:, :, :8].transpose(1, 0, 2).reshape(t, SSD_N_HEADS), ((0, 0), (0, 96)))
    d_proj, g_dt_bias = _dt_bwd(d_dt, proj, bias_pad, d_proj, tm=tmw)

    d_ao = _mm_plain("d_ao", d_yattn, wb['ao_t'], tm=tmw, tn=512, tk=1024, out_dtype=BF16)
    g_w_ao = _mm_plain("g_w_ao", ao, d_yattn, tm=1024, tn=1024, tk=tkt, trans_a=True, out_dtype=BF16)
    ride_mix = rides('mix', {'ssd_w_out': g_w_so, 'attn_w_out': g_w_ao, 'w_mix_out': g_w_mix})
    d_proj, g_sinks, *rode['mix'] = _attn_bwd(qr, kp, vp, kt, d_ao, ao, lse, sinks, cos_t, sin_t, d_proj, ride=ride_mix)

    def dx_ep(acc, i, j, ins, outs):
        xv, wn, dx1v = ins
        d_n, dw = _rms_bwd(xv[...], wn[...], acc)
        outs[0][...] = dx1v[...] + d_n
        _accumulate(outs[1], i == 0, dw)

    g_cat_t = _mm_plain("g_w_in", d_proj, u, tm=1024, tn=1024, tk=tkt, trans_a=True)
    grad_x, g_norm_mix_pre, *rode['w_in'] = _mm_call(
        "d_u", d_proj, wb['cat_t'], tm=tm, tn=D_MODEL, tk=2304, epilogue=dx_ep, ride=rides('w_in', {'w_in': g_cat_t}),
        extra_in=[(x, (tm, D_MODEL), i0), (ps['norm_mix_pre_w'], (1, D_MODEL), c0), (d_x1, (tm, D_MODEL), i0)],
        outs=[((t, D_MODEL), F32, (tm, D_MODEL), i0), ((1, D_MODEL), F32, (1, D_MODEL), c0)])

    grads = {
        'norm_mix_pre_w': g_norm_mix_pre, 'w_in': g_cat_t, 'ssd_conv_w': g_ssd_conv_w, 'ssd_conv_b': g_ssd_conv_b,
        'ssd_dt_bias': g_dt_bias[:, :SSD_N_HEADS], 'ssd_a_log': d_alog[:, 0, :8].reshape(1, SSD_N_HEADS),
        'ssd_d': d_dd[:, 0, :8].reshape(1, SSD_N_HEADS), 'ssd_norm_w': g_ssd_norm, 'ssd_w_out': g_w_so,
        'attn_sinks': g_sinks[:, :ATTN_N_HEADS], 'attn_w_out': g_w_ao, 'w_mix_out': g_w_mix,
        'norm_mix_post_w': g_norm_mix_post, 'norm_ffn_pre_w': g_norm_ffn_pre, 'ffn_w_up': g_w_up_t,
        'ffn_conv_w': g_ffn_conv_w, 'ffn_conv_b': jnp.concatenate([db_g, db_v], axis=1), 'ffn_w_down': g_w_down,
        'norm_ffn_post_w': g_norm_ffn_post,
    }
    return loss, grad_x, grads, rode


def _group_channels(a):
    parts = []
    for g in range(SSD_N_GROUPS):
        parts += [a[..., 512 * g:512 * (g + 1)], a[..., 2048 + 128 * g:2048 + 128 * (g + 1)],
                  a[..., 2560 + 128 * g:2560 + 128 * (g + 1)]]
    return jnp.concatenate(parts, axis=-1)


def _ungroup_channels(a):
    xs = [a[..., GROUP_W * g:GROUP_W * g + 512] for g in range(SSD_N_GROUPS)]
    bs = [a[..., GROUP_W * g + 512:GROUP_W * g + 640] for g in range(SSD_N_GROUPS)]
    cs = [a[..., GROUP_W * g + 640:GROUP_W * (g + 1)] for g in range(SSD_N_GROUPS)]
    return jnp.concatenate(xs + bs + cs, axis=-1)


def _proj_rows(a_t, lo, hi):
    out = []
    for start, length, dst in sorted(PROJ_SEGS):
        s, e = max(lo, start), min(hi, start + length)
        if s < e:
            out.append(a_t[dst + s - start:dst + e - start])
    return out


def _to_proj_layout(w_in_t):
    pieces, pos = [], 0
    for start, length, dst in sorted(PROJ_SEGS, key=lambda s: s[2]):
        if dst > pos:
            pieces.append(jnp.zeros((dst - pos, w_in_t.shape[1]), w_in_t.dtype))
        pieces.append(w_in_t[start:start + length])
        pos = dst + length
    if pos < PROJ_W:
        pieces.append(jnp.zeros((PROJ_W - pos, w_in_t.shape[1]), w_in_t.dtype))
    return jnp.concatenate(pieces, axis=0)


def _rope_tables(positions):
    half = 32
    inv_freq = ROPE_THETA ** (-jnp.arange(half, dtype=F32) * 2.0 / 64)
    ang = positions.astype(F32)[:, None] * inv_freq
    cos, sin = jnp.cos(ang), jnp.sin(ang)
    return jnp.concatenate([cos, cos, cos, cos], axis=1), jnp.concatenate([-sin, sin, -sin, sin], axis=1)


def _matmul_weights(w_in_t):
    cat_t = _to_proj_layout(w_in_t)
    return {'cat': cat_t.T, 'cat_t': cat_t}


def _late_weights(so, ao, mix, up_t, dn):
    return {'so': so, 'so_t': so.T, 'ao': ao, 'ao_t': ao.T, 'mix': mix, 'mix_t': mix.T,
            'up': up_t.T, 'up_t': up_t, 'dn': dn, 'dn_t': dn.T}


ANY = pl.BlockSpec(memory_space=pl.ANY)
MESH = pl.DeviceIdType.MESH
ROW_ALIGN = 32


def _mesh_pos():
    return lax.axis_index("x"), lax.axis_index("y"), lax.axis_index("c")


def _other_chips(x, y):
    return [(1 - x, y), (x, 1 - y), (1 - x, 1 - y)]


def _remote(src, dst, send_sems, recv_sems, k, to):
    return pltpu.make_async_remote_copy(src_ref=src, dst_ref=dst, send_sem=send_sems.at[k], recv_sem=recv_sems.at[k],
                                        device_id=to, device_id_type=MESH)


def _half(c, rh):
    return pl.ds(pl.multiple_of(c * rh, 16), rh)


def _ag_ride(shard):
    r = shard.shape[0]
    rh = r // 2

    def first_copies(w_ref, out_ref, send_sems, recv_sems):
        x, y, c = _mesh_pos()
        p = 2 * x + y
        mine = _half(c, rh)
        cps = [_remote(w_ref, out_ref.at[p], send_sems, recv_sems, 6, (x, y, 1 - c))]
        return cps + [_remote(w_ref.at[mine], out_ref.at[p, mine], send_sems, recv_sems, j, (cx, cy, c))
                      for j, (cx, cy) in enumerate(_other_chips(x, y))]

    def start(ins, outs, send_sems, recv_sems):
        for cp in first_copies(ins[0], outs[0], send_sems, recv_sems):
            cp.start()

    def forwards(out_ref, send_sems, recv_sems, half):
        x, y, c = _mesh_pos()
        return [_remote(out_ref.at[2 * cx + cy, half], out_ref.at[2 * cx + cy, half], send_sems, recv_sems, 3 + j, (x, y, 1 - c))
                for j, (cx, cy) in enumerate(_other_chips(x, y))]

    def middle(ins, outs, send_sems, recv_sems):
        x, y, c = _mesh_pos()
        mine = _half(c, rh)
        for j, (fwd, (cx, cy)) in enumerate(zip(forwards(outs[0], send_sems, recv_sems, mine), _other_chips(x, y))):
            slab = outs[0].at[2 * cx + cy, mine]
            _remote(slab, slab, send_sems, recv_sems, j, (x, y, 1 - c)).wait_recv()
            fwd.start()

    def finish(ins, outs, send_sems, recv_sems):
        w_ref, out_ref = ins[0], outs[0]
        x, y, c = _mesh_pos()
        for cp in forwards(out_ref, send_sems, recv_sems, _half(1 - c, rh)):
            cp.wait_recv()
        _remote(w_ref, out_ref.at[2 * x + y], send_sems, recv_sems, 6, (x, y, 1 - c)).wait_recv()
        for cp in first_copies(w_ref, out_ref, send_sems, recv_sems) + forwards(out_ref, send_sems, recv_sems, _half(c, rh)):
            cp.wait_send()

    return _Ride((shard,), (SDS((N_CHIPS, r, COMM_LANES), shard.dtype),), 7, start, finish, middle)


def _rs_ride(gbuf):
    rh = gbuf.shape[1] // 2

    def copies(g_ref, r_ref, send_sems, recv_sems, landing):
        x, y, c = _mesh_pos()
        cps = []
        for k, (cx, cy) in enumerate(_other_chips(x, y)):
            for h in range(2):
                slot = 2 * k + c if landing else 2 * k + h
                cps.append(pltpu.make_async_remote_copy(
                    src_ref=g_ref.at[2 * cx + cy, pl.ds(h * rh, rh)], dst_ref=r_ref.at[slot],
                    send_sem=send_sems.at[2 * k + h], recv_sem=recv_sems.at[slot],
                    device_id=(cx, cy, h), device_id_type=MESH))
        cps.append(_remote(g_ref.at[2 * x + y, _half(1 - c, rh)], r_ref.at[6], send_sems, recv_sems, 6, (x, y, 1 - c)))
        return cps

    def start(ins, outs, send_sems, recv_sems):
        for cp in copies(ins[0], outs[0], send_sems, recv_sems, True):
            cp.start()

    def finish(ins, outs, send_sems, recv_sems):
        for cp in copies(ins[0], outs[0], send_sems, recv_sems, False):
            cp.wait()

    return _Ride((gbuf,), (SDS((7, rh, COMM_LANES), gbuf.dtype),), 7, start, finish)


def _rs_sum(name, gbuf, got, pc_idx):
    rh = got.shape[1]
    tr = max(d for d in range(16, 513, 16) if rh % d == 0)
    nb = rh // tr

    def body(pc_ref, own_ref, *refs):
        o_ref = refs[7]
        p, c = pc_ref[0], pc_ref[1]
        own = own_ref[0].astype(F32)
        slots = [r[0].astype(F32) for r in refs[:7]]

        def term(q, h):
            code = p ^ q
            far = jnp.where(code == 2, slots[h], jnp.where(code == 1, slots[2 + h], slots[4 + h]))
            return jnp.where(code == 0, jnp.where(c == h, own, slots[6]), far)

        acc = term(0, 0)
        for q, h in [(0, 1), (1, 0), (1, 1), (2, 0), (2, 1), (3, 0), (3, 1)]:
            acc = acc + term(q, h)
        o_ref[0] = acc

    slot = lambda s: pl.BlockSpec((1, tr, COMM_LANES), lambda i, pc: (s, i, 0))
    return pl.pallas_call(
        body, name=name,
        grid_spec=pltpu.PrefetchScalarGridSpec(
            num_scalar_prefetch=1, grid=(nb,),
            in_specs=[pl.BlockSpec((1, tr, COMM_LANES), lambda i, pc: (pc[0], pc[1] * nb + i, 0))] + [slot(s) for s in range(7)],
            out_specs=pl.BlockSpec((1, tr, COMM_LANES), lambda i, pc: (pc[1], i, 0))),
        out_shape=SDS((2, rh, COMM_LANES), F32), compiler_params=_cp(dimension_semantics=("arbitrary",)),
    )(pc_idx, gbuf, *([got] * 7))


def _pair_gather_all(bufs):
    n = len(bufs)

    def body(*refs):
        outs, send_sems, recv_sems = refs[n:2 * n], refs[2 * n], refs[2 * n + 1]
        x, y, c = _mesh_pos()
        cps = [_remote(o.at[c], o.at[c], send_sems, recv_sems, k, (x, y, 1 - c)) for k, o in enumerate(outs)]
        for cp in cps:
            cp.start()
        for k, o in enumerate(outs):
            _remote(o.at[1 - c], o.at[1 - c], send_sems, recv_sems, k, (x, y, 1 - c)).wait_recv()
        for cp in cps:
            cp.wait_send()

    return pl.pallas_call(
        body, name="grad_pair_gather", in_specs=[ANY] * n, out_specs=[ANY] * n,
        out_shape=[SDS(b.shape, b.dtype) for b in bufs],
        scratch_shapes=[pltpu.SemaphoreType.DMA((n,)), pltpu.SemaphoreType.DMA((n,))],
        input_output_aliases={k: k for k in range(n)},
    )(*bufs)


def _pack_rows(big, small=()):
    parts = list(big)
    if small:
        flat = jnp.concatenate([p.reshape(-1) for p in small])
        k = -(-flat.shape[0] // (16 * COMM_LANES)) * 16
        parts.append(jnp.pad(flat, (0, k * COMM_LANES - flat.shape[0])).reshape(k, COMM_LANES))
    pad = -sum(p.shape[0] for p in parts) % ROW_ALIGN
    if pad:
        parts.append(jnp.zeros((pad, COMM_LANES), parts[0].dtype))
    return jnp.concatenate(parts, axis=0) if len(parts) > 1 else parts[0]


def _take(flat, off, shape):
    n = 1
    for d in shape:
        n *= d
    return flat[off:off + n].reshape(shape), off + n


BIG_ROWS = {'w_in': 2184, 'ssd_w_out': 512, 'attn_w_out': 256, 'w_mix_out': 256, 'ffn_w_up': 1408, 'ffn_w_down': 704}
TRANSPOSED = ('w_in', 'ffn_w_up')
LATE = ('ssd_w_out', 'attn_w_out', 'w_mix_out', 'ffn_w_up', 'ffn_w_down')
CONV_TAPS = ('ssd_conv_w', 'ffn_conv_w')
RS_GROUPS = {'ffn': ('ffn_w_up', 'ffn_w_down'), 'mix': ('ssd_w_out', 'attn_w_out', 'w_mix_out'), 'w_in': ('w_in',)}


def _exchange(name, ride):
    n_in, n_out = len(ride.ins), len(ride.outs)

    def body(*refs):
        ins, outs, sems = refs[:n_in], refs[n_in:n_in + n_out], refs[n_in + n_out:]
        ride.start(ins, outs, *sems)
        if ride.middle is not None:
            ride.middle(ins, outs, *sems)
        ride.finish(ins, outs, *sems)

    return pl.pallas_call(
        body, name=name, in_specs=[ANY] * n_in, out_specs=[ANY] * n_out, out_shape=list(ride.outs),
        scratch_shapes=[pltpu.SemaphoreType.DMA((ride.n_sems,)), pltpu.SemaphoreType.DMA((ride.n_sems,))],
    )(*ride.ins)


def kernel(x, positions, norm_mix_pre_w, w_in, ssd_conv_w, ssd_conv_b, ssd_dt_bias, ssd_a_log, ssd_d, ssd_norm_w, ssd_w_out, attn_sinks, attn_w_out, w_mix_out, norm_mix_post_w, norm_ffn_pre_w, ffn_w_up, ffn_conv_w, ffn_conv_b, ffn_w_down, norm_ffn_post_w, loss_target, m_norm_mix_pre_w, m_w_in, m_ssd_conv_w, m_ssd_conv_b, m_ssd_dt_bias, m_ssd_a_log, m_ssd_d, m_ssd_norm_w, m_ssd_w_out, m_attn_sinks, m_attn_w_out, m_w_mix_out, m_norm_mix_post_w, m_norm_ffn_pre_w, m_ffn_w_up, m_ffn_conv_w, m_ffn_conv_b, m_ffn_w_down, m_norm_ffn_post_w, v_norm_mix_pre_w, v_w_in, v_ssd_conv_w, v_ssd_conv_b, v_ssd_dt_bias, v_ssd_a_log, v_ssd_d, v_ssd_norm_w, v_ssd_w_out, v_attn_sinks, v_attn_w_out, v_w_mix_out, v_norm_mix_post_w, v_norm_ffn_pre_w, v_ffn_w_up, v_ffn_conv_w, v_ffn_conv_b, v_ffn_w_down, v_norm_ffn_post_w):
    given = dict(locals())
    w = {n: given[n][0] for n in WEIGHTS}
    w = {n: (a if a.ndim == 2 else a[None]) for n, a in w.items()}
    mom_m = {n: given['m_' + n].reshape(w[n].shape) for n in WEIGHTS}
    mom_v = {n: given['v_' + n].reshape(w[n].shape) for n in WEIGHTS}
    cx, cy, cc = _mesh_pos()
    pc_idx = jnp.stack([2 * cx + cy, cc]).astype(jnp.int32)

    rows_of = lambda n: (w[n].T if n in TRANSPOSED else w[n]).astype(BF16)
    gathered = _exchange("w_in_all_gather", _ag_ride(_pack_rows([rows_of('w_in')])))[0]
    wb = _matmul_weights(jnp.concatenate([gathered[s, :BIG_ROWS['w_in']] for s in range(N_CHIPS)], axis=0))
    taps = [lax.bitcast_convert_type(w[n], BF16) for n in CONV_TAPS]

    def unpack_late(arrived):
        rows, conv = {n: [] for n in LATE}, {n: [] for n in CONV_TAPS}
        for s in range(N_CHIPS):
            r0 = 0
            for n in LATE:
                rows[n].append(arrived[0][s, r0:r0 + BIG_ROWS[n]])
                r0 += BIG_ROWS[n]
            flat, off = arrived[0][s, r0:r0 + 16].reshape(-1), 0
            for n in CONV_TAPS:
                a, off = _take(flat, off, w[n].shape + (2,))
                conv[n].append(lax.bitcast_convert_type(a, F32))
        full = {n: jnp.concatenate(rows[n], axis=0) for n in LATE}
        return (_late_weights(*[full[n] for n in LATE]),
                {'ssd_conv_w': _group_channels(jnp.concatenate(conv['ssd_conv_w'], axis=1)),
                 'ffn_conv_w': jnp.concatenate(conv['ffn_conv_w'], axis=1)})

    late = (_ag_ride(_pack_rows([rows_of(n) for n in LATE], taps)), unpack_late)

    sent = {}

    def rides(group, g):
        parts = []
        for s in range(N_CHIPS):
            slab = []
            for n in RS_GROUPS[group]:
                lo, hi = BIG_ROWS[n] * s, BIG_ROWS[n] * (s + 1)
                slab += _proj_rows(g[n], lo, hi) if n == 'w_in' else [g[n][lo:hi]]
            slab = [a.astype(BF16) for a in slab]
            pad = -sum(a.shape[0] for a in slab) % ROW_ALIGN
            parts += slab + ([jnp.zeros((pad, COMM_LANES), BF16)] if pad else [])
        sent[group] = jnp.concatenate(parts, axis=0).reshape(N_CHIPS, -1, COMM_LANES)
        return _rs_ride(sent[group])

    ps = {n: w[n] for n in REPLICATED}
    ps['ssd_conv_b'] = _group_channels(w['ssd_conv_b'])
    cos_t, sin_t = _rope_tables(positions[0])
    loss, grad_x, grads, rode = _local_step(x[0], cos_t, sin_t, loss_target[0], wb, ps, late, rides)
    grads['ssd_conv_w'] = _ungroup_channels(grads['ssd_conv_w'])
    grads['ssd_conv_b'] = _ungroup_channels(grads['ssd_conv_b'])

    shard_cols = {n: sh[1] for n, _, sh in SHARDED}
    parts = []
    for s in range(N_CHIPS):
        small = [grads[n][:, shard_cols[n] * s:shard_cols[n] * (s + 1)] for n in CONV_TAPS] + [grads[n] for n in REPLICATED]
        flat = _pack_rows([], small)
        high = flat.astype(BF16)
        parts += [high, (flat - high.astype(F32)).astype(BF16)]
    sent['small'] = jnp.concatenate(parts, axis=0).reshape(N_CHIPS, -1, COMM_LANES)
    rode['small'] = _exchange("grad_small_exchange", _rs_ride(sent['small']))

    groups = ('ffn', 'mix', 'w_in', 'small')
    red = _pair_gather_all([_rs_sum("grad_sum_" + g, sent[g], rode[g][0], pc_idx) for g in groups])
    red = {g: r.reshape(-1, COMM_LANES) for g, r in zip(groups, red)}
    g_red = {}
    for g in groups[:3]:
        r0 = 0
        for n in RS_GROUPS[g]:
            g_red[n] = red[g][r0:r0 + BIG_ROWS[n]].T if n in TRANSPOSED else red[g][r0:r0 + BIG_ROWS[n]]
            r0 += BIG_ROWS[n]
    half = red['small'].shape[0] // 2
    flat, off = (red['small'][:half] + red['small'][half:]).reshape(-1), 0
    for n in CONV_TAPS + REPLICATED:
        g_red[n], off = _take(flat, off, w[n].shape)

    small_names = [n for n in WEIGHTS if n not in MATMUL_WEIGHTS]
    delta, new_m, new_v = {}, {}, {}
    for n in MATMUL_WEIGHTS:
        delta[n], new_m[n], new_v[n] = _adamw("adamw_" + n, w[n], g_red[n], mom_m[n], mom_v[n], tr=64)
    packed = [_pack_small([d[n] for n in small_names]) for d in (w, g_red, mom_m, mom_v)]
    outs = _adamw("adamw_small", *packed, tr=packed[0].shape[0])
    for res, o in zip((delta, new_m, new_v), outs):
        fl, off = o.reshape(-1), 0
        for n in small_names:
            res[n], off = _take(fl, off, w[n].shape)

    loss_all = lax.psum(loss[0, 0], ("x", "y", "c"))
    shaped = lambda d: [d[n].reshape(given[n].shape) for n in WEIGHTS]
    return (loss_all, grad_x[None], *shaped(g_red), *shaped(delta), *shaped(new_m), *shaped(new_v))


def _pack_small(pieces):
    flat = jnp.concatenate([p.reshape(-1) for p in pieces])
    rows = -(-flat.shape[0] // (128 * 8)) * 8
    return jnp.pad(flat, (0, rows * 128 - flat.shape[0])).reshape(rows, 128)
```

```python
from typing import Callable, NamedTuple

import jax
import jax.numpy as jnp
from jax import lax
from jax.experimental import pallas as pl
from jax.experimental.pallas import tpu as pltpu

F32 = jnp.float32
BF16 = jnp.bfloat16
SDS = jax.ShapeDtypeStruct
HIGHEST = lax.Precision.HIGHEST

D_MODEL = 1024
SSD_D_INNER = 2048
SSD_N_HEADS = 32
SSD_HEAD_DIM = 64
SSD_N_GROUPS = 4
SSD_HEADS_PER_GROUP = 8
SSD_D_STATE = 128
SSD_CONV_DIM = 3072
CHUNK = 128
ATTN_N_HEADS = 16
KV_WIDTH = 256
FFN_D_FF = 2816
IN_PROJ_DIM = 8736
ROPE_THETA = 10000.0
NORM_EPS = 1e-6
ADAM_LR, ADAM_B1, ADAM_B2, ADAM_EPS, ADAM_WD, ADAM_STEP = 0.001, 0.9, 0.999, 1e-08, 0.01, 10

PROJ_W = 9216
OFF_Q, OFF_K, OFF_V, OFF_Z, OFF_DT, OFF_GS, OFF_GA, OFF_XBC = 0, 1024, 1280, 1536, 3584, 4096, 5120, 6144
GROUP_W = 768
PROJ_SEGS = ([(0, 2048, OFF_Z)]
             + [(2048 + 512 * g, 512, OFF_XBC + GROUP_W * g) for g in range(4)]
             + [(4096 + 128 * g, 128, OFF_XBC + GROUP_W * g + 512) for g in range(4)]
             + [(4608 + 128 * g, 128, OFF_XBC + GROUP_W * g + 640) for g in range(4)]
             + [(5120, 32, OFF_DT), (5152, 1024, OFF_Q), (6176, 256, OFF_K), (6432, 256, OFF_V),
                (6688, 1024, OFF_GS), (7712, 1024, OFF_GA)])
VMEM_LIMIT_MB = 48
NEG = -1e30

WEIGHTS = ('norm_mix_pre_w', 'w_in', 'ssd_conv_w', 'ssd_conv_b', 'ssd_dt_bias', 'ssd_a_log', 'ssd_d', 'ssd_norm_w',
           'ssd_w_out', 'attn_sinks', 'attn_w_out', 'w_mix_out', 'norm_mix_post_w', 'norm_ffn_pre_w', 'ffn_w_up',
           'ffn_conv_w', 'ffn_conv_b', 'ffn_w_down', 'norm_ffn_post_w')
SHARDED = (('w_in', 1, (1024, 2184)), ('ssd_conv_w', 1, (4, 768)), ('ssd_w_out', 0, (512, 1024)),
           ('attn_w_out', 0, (256, 1024)), ('w_mix_out', 0, (256, 1024)), ('ffn_w_up', 1, (1024, 1408)),
           ('ffn_conv_w', 1, (3, 1408)), ('ffn_w_down', 0, (704, 1024)))
MATMUL_WEIGHTS = ('w_in', 'ssd_w_out', 'attn_w_out', 'w_mix_out', 'ffn_w_up', 'ffn_w_down')
REPLICATED = tuple(n for n in WEIGHTS if n not in {s[0] for s in SHARDED})
N_CHIPS = 4
COMM_LANES = 1024


def _cp(vmem_mb=VMEM_LIMIT_MB, **kw):
    return pltpu.CompilerParams(vmem_limit_bytes=vmem_mb << 20, **kw)


class _Ride(NamedTuple):
    ins: tuple
    outs: tuple
    n_sems: int
    start: Callable
    finish: Callable
    middle: Callable = None


def _ride_parts(ride):
    if ride is None:
        return [], [], [], [], []
    hbm = pl.BlockSpec(memory_space=pl.ANY)
    return (list(ride.ins), [hbm] * len(ride.ins), list(ride.outs), [hbm] * len(ride.outs),
            [pltpu.SemaphoreType.DMA((ride.n_sems,)), pltpu.SemaphoreType.DMA((ride.n_sems,))])


def _ride_run(ride, first, last, in_refs, out_refs, sems, middle=None):
    if ride is None:
        return

    @pl.when(first)
    def _():
        ride.start(in_refs, out_refs, *sems)

    if ride.middle is not None:
        @pl.when(last if middle is None else middle)
        def _():
            ride.middle(in_refs, out_refs, *sems)

    @pl.when(last)
    def _():
        ride.finish(in_refs, out_refs, *sems)


def _iota(shape, axis):
    return lax.broadcasted_iota(jnp.int32, shape, axis)


def _sigmoid(v):
    return 1.0 / (1.0 + jnp.exp(-v))


def _mm_call(name, a, b, *, tm, tn, tk, epilogue, outs, extra_in=(), trans_a=False, fill=None, ride=None):
    if trans_a:
        kdim, m = a.shape
    else:
        m, kdim = a.shape
    n = b.shape[1]
    assert b.shape[0] == kdim and m % tm == 0 and n % tn == 0 and kdim % tk == 0, (name, a.shape, b.shape, tm, tn, tk)
    gi, gj, gk = m // tm, n // tn, kdim // tk
    n_in, n_out = len(extra_in), len(outs)

    n_fill = 0 if fill is None else 1
    r_ops, r_in_specs, r_outs, r_out_specs, r_scratch = _ride_parts(ride)

    def body(a_ref, b_ref, *rest):
        ins = rest[:n_in]
        rest = rest[n_in + n_fill:]
        r_in, rest = rest[:len(r_ops)], rest[len(r_ops):]
        out_refs, rest = rest[:n_out], rest[n_out:]
        r_out, scratch = rest[:len(r_outs)], rest[len(r_outs):]
        i, j, k = pl.program_id(0), pl.program_id(1), pl.program_id(2)
        _ride_run(ride, (i == 0) & (j == 0) & (k == 0), (i == gi - 1) & (j == gj - 1) & (k == gk - 1),
                  r_in, r_out, scratch[-2:])
        av = a_ref[...].astype(BF16)
        bv = b_ref[...].astype(BF16)
        if trans_a:
            part = lax.dot_general(av, bv, (((0,), (0,)), ((), ())), preferred_element_type=F32)
        else:
            part = jnp.dot(av, bv, preferred_element_type=F32)
        if gk == 1:
            epilogue(part, i, j, ins, out_refs)
        else:
            acc = scratch[0]

            @pl.when(k == 0)
            def _():
                acc[...] = part

            @pl.when(k > 0)
            def _():
                acc[...] += part

            @pl.when(k == gk - 1)
            def _():
                epilogue(acc[...], i, j, ins, out_refs)

    a_spec = pl.BlockSpec((tk, tm), lambda i, j, k: (k, i)) if trans_a else pl.BlockSpec((tm, tk), lambda i, j, k: (i, k))
    in_specs = [a_spec, pl.BlockSpec((tk, tn), lambda i, j, k: (k, j))]
    in_specs += [pl.BlockSpec(bs, im) for _, bs, im in extra_in]
    operands = [a, b] + [e[0] for e in extra_in]
    aliases = {}
    if fill is not None:
        in_specs.append(pl.BlockSpec(memory_space=pl.ANY))
        aliases = {len(operands): fill[1]}
        operands.append(fill[0])
    return pl.pallas_call(
        body, name=name, grid=(gi, gj, gk), in_specs=in_specs + r_in_specs,
        out_specs=[pl.BlockSpec(bs, im) for _, _, bs, im in outs] + r_out_specs,
        out_shape=[SDS(s, d) for s, d, _, _ in outs] + r_outs,
        scratch_shapes=([pltpu.VMEM((tm, tn), F32)] if gk > 1 else []) + r_scratch,
        input_output_aliases=aliases,
        compiler_params=_cp(dimension_semantics=("arbitrary", "arbitrary", "arbitrary")),
    )(*operands, *r_ops)


def _mm_plain(name, a, b, *, tm, tn, tk, out_dtype=F32, trans_a=False):
    m = a.shape[1] if trans_a else a.shape[0]

    def epilogue(acc, i, j, ins, outs):
        outs[0][...] = acc.astype(out_dtype)

    return _mm_call(name, a, b, tm=tm, tn=tn, tk=tk, epilogue=epilogue, trans_a=trans_a,
                    outs=[((m, b.shape[1]), out_dtype, (tm, tn), lambda i, j, k: (i, j))])[0]


def _accumulate(ref, first, value):
    @pl.when(first)
    def _():
        ref[...] = value

    @pl.when(jnp.logical_not(first))
    def _():
        ref[...] += value


def _rms_bwd(xv, w, dy):
    r = lax.rsqrt(jnp.mean(xv * xv, axis=-1, keepdims=True) + NORM_EPS)
    xn = xv * r
    dxh = dy * w
    dx = r * (dxh - xn * jnp.mean(dxh * xn, axis=-1, keepdims=True))
    return dx, jnp.sum(dy * xn, axis=0, keepdims=True)


def _norm_mm(name, x, wn, w, *, tm, tn, ride=None):
    t, dm = x.shape
    n = w.shape[1]
    tm = min(tm, t)
    gi, gj = t // tm, n // tn
    r_ops, r_in_specs, r_outs, r_out_specs, r_scratch = _ride_parts(ride)

    def body(x_ref, wn_ref, w_ref, *rest):
        r_in, rest = rest[:len(r_ops)], rest[len(r_ops):]
        o_ref, u_ref = rest[:2]
        r_out, sems = rest[2:2 + len(r_outs)], rest[2 + len(r_outs):]
        i, j = pl.program_id(0), pl.program_id(1)
        _ride_run(ride, (i == 0) & (j == 0), (i == gi - 1) & (j == gj - 1), r_in, r_out, sems,
                  middle=(i == (3 * gi) // 4) & (j == 0) if gi > 1 else None)

        @pl.when(j == 0)
        def _():
            xv = x_ref[...]
            r = lax.rsqrt(jnp.mean(xv * xv, axis=-1, keepdims=True) + NORM_EPS)
            u_ref[...] = (xv * r * wn_ref[...]).astype(BF16)

        o_ref[...] = jnp.dot(u_ref[...], w_ref[...], preferred_element_type=F32)

    return pl.pallas_call(
        body, name=name, grid=(gi, gj),
        in_specs=[pl.BlockSpec((tm, dm), lambda i, j: (i, 0)), pl.BlockSpec((1, dm), lambda i, j: (0, 0)),
                  pl.BlockSpec((dm, tn), lambda i, j: (0, j))] + r_in_specs,
        out_specs=[pl.BlockSpec((tm, tn), lambda i, j: (i, j)), pl.BlockSpec((tm, dm), lambda i, j: (i, 0))] + r_out_specs,
        out_shape=[SDS((t, n), F32), SDS((t, dm), BF16)] + r_outs, scratch_shapes=r_scratch,
        compiler_params=_cp(dimension_semantics=("arbitrary", "arbitrary")),
    )(x, wn, w, *r_ops)


def _shift_down(tile, halo, s):
    if s == 0:
        return tile
    r = pltpu.roll(tile, s, axis=0)
    h = pltpu.roll(halo, s, axis=0)
    head = jnp.where(_iota(h.shape, 0) < s, h, r[0:8])
    return jnp.concatenate([head, r[8:]], axis=0)


def _shift_up(tile, halo, s):
    if s == 0:
        return tile
    n = tile.shape[0]
    r = pltpu.roll(tile, n - s, axis=0)
    h = pltpu.roll(halo, 8 - s, axis=0)
    tail = jnp.where(_iota(h.shape, 0) >= 8 - s, h, r[n - 8:])
    return jnp.concatenate([r[:n - 8], tail], axis=0)


def _conv_apply(tile, halo, wv, bv, kw):
    acc = bv + wv[kw - 1:kw, :] * tile
    for k in range(kw - 1):
        acc = acc + wv[k:k + 1, :] * _shift_down(tile, halo, kw - 1 - k)
    return acc


def _prev_halo_spec(tm, tc, col0):
    return pl.BlockSpec((8, tc), lambda i, j: (jnp.maximum(i * (tm // 8) - 1, 0), col0 + j))


def _silu_parts(pre):
    sg = _sigmoid(pre)
    return pre * sg, sg * (1.0 + pre * (1.0 - sg))


def _conv_silu_fwd(proj, w, b, *, tm, tc=1536):
    t = proj.shape[0]
    c = w.shape[1]
    tm = min(tm, t)
    col0 = OFF_XBC // tc

    def body(x_ref, h_ref, w_ref, b_ref, o_ref, pre_ref):
        halo = jnp.where(pl.program_id(0) > 0, h_ref[...], 0.0)
        pre = _conv_apply(x_ref[...], halo, w_ref[...], b_ref[...], 4)
        o_ref[...] = _silu_parts(pre)[0]
        pre_ref[...] = pre.astype(BF16)

    tile = pl.BlockSpec((tm, tc), lambda i, j: (i, j))
    return pl.pallas_call(
        body, name="ssd_conv_fwd", grid=(t // tm, c // tc),
        in_specs=[pl.BlockSpec((tm, tc), lambda i, j: (i, col0 + j)), _prev_halo_spec(tm, tc, col0),
                  pl.BlockSpec((4, tc), lambda i, j: (0, j)), pl.BlockSpec((1, tc), lambda i, j: (0, j))],
        out_specs=[tile, tile], out_shape=[SDS((t, c), F32), SDS((t, c), BF16)],
        compiler_params=_cp(dimension_semantics=("arbitrary", "arbitrary")),
    )(proj, proj, w, b)


def _conv_silu_bwd1(d_out, pre, *, tm, tc=1536):
    t, c = pre.shape
    tm = min(tm, t)

    def body(g_ref, p_ref, o_ref, db_ref):
        i = pl.program_id(1)
        d_pre = g_ref[...] * _silu_parts(p_ref[...].astype(F32))[1]
        o_ref[...] = d_pre.astype(BF16)
        _accumulate(db_ref, i == 0, jnp.sum(d_pre, axis=0, keepdims=True))

    tile = pl.BlockSpec((tm, tc), lambda j, i: (i, j))
    return pl.pallas_call(
        body, name="ssd_conv_bwd1", grid=(c // tc, t // tm), in_specs=[tile, tile],
        out_specs=[tile, pl.BlockSpec((1, tc), lambda j, i: (0, j))],
        out_shape=[SDS((t, c), BF16), SDS((1, c), F32)],
        compiler_params=_cp(dimension_semantics=("arbitrary", "arbitrary")),
    )(d_out, pre)


def _conv_bwd2(name, d_pre, src, src_col0, w, *, tm, tc, out_cols, out_col0, fill=None):
    t, c = d_pre.shape
    kw = w.shape[0]
    tm = min(tm, t)
    ni = t // tm
    col0 = src_col0 // tc
    ocol0 = out_col0 // tc

    def body(g_ref, gn_ref, x_ref, w_ref, *rest):
        o_ref, dw_ref = rest[-2:]
        i = pl.program_id(1)
        g = g_ref[...].astype(F32)
        g_next = jnp.where(i < ni - 1, gn_ref[...].astype(F32)[0:8], 0.0)
        xv = x_ref[...]
        wv = w_ref[...]
        shifted = [_shift_up(g, g_next, kw - 1 - k) for k in range(kw)]
        d_in = wv[0:1, :] * shifted[0]
        for k in range(1, kw):
            d_in = d_in + wv[k:k + 1, :] * shifted[k]
        o_ref[...] = d_in.astype(o_ref.dtype)
        rows = [jnp.sum(shifted[k] * xv, axis=0, keepdims=True) for k in range(kw)]

        @pl.when(i == 0)
        def _():
            for k in range(kw):
                dw_ref[k:k + 1, :] = rows[k]

        @pl.when(i > 0)
        def _():
            for k in range(kw):
                dw_ref[k:k + 1, :] += rows[k]

    in_specs = [pl.BlockSpec((tm, tc), lambda j, i: (i, j)),
                pl.BlockSpec((16, tc), lambda j, i: (jnp.minimum((i + 1) * (tm // 16), t // 16 - 1), j)),
                pl.BlockSpec((tm, tc), lambda j, i: (i, col0 + j)),
                pl.BlockSpec((kw, tc), lambda j, i: (0, j))]
    operands = [d_pre, d_pre, src, w]
    if fill is not None:
        in_specs.append(pl.BlockSpec(memory_space=pl.ANY))
        operands.append(fill)
    return pl.pallas_call(
        body, name=name, grid=(c // tc, ni), in_specs=in_specs,
        out_specs=[pl.BlockSpec((tm, tc), lambda j, i: (i, ocol0 + j)), pl.BlockSpec((kw, tc), lambda j, i: (0, j))],
        out_shape=[SDS((t, out_cols), BF16), SDS((kw, c), F32)],
        input_output_aliases={} if fill is None else {4: 0},
        compiler_params=_cp(dimension_semantics=("arbitrary", "arbitrary")),
    )(*operands)


GELU_C = 0.7978845608028654


def _gelu_parts(v):
    inner = GELU_C * (v + 0.044715 * v * v * v)
    th = jnp.tanh(inner)
    val = 0.5 * v * (1.0 + th)
    grad = 0.5 * (1.0 + th) + 0.5 * v * (1.0 - th * th) * GELU_C * (1.0 + 3.0 * 0.044715 * v * v)
    return val, grad


def _ffn_act_fwd(up_raw, w, b, *, tm, tc=1408):
    t = up_raw.shape[0]
    tm = min(tm, t)
    nj = FFN_D_FF // tc
    halo = lambda i: jnp.maximum(i * (tm // 8) - 1, 0)

    def body(g_ref, gh_ref, v_ref, vh_ref, wg_ref, wv_ref, bg_ref, bv_ref, o_ref, gate_ref, val_ref):
        first = pl.program_id(0) > 0
        gate = _conv_apply(g_ref[...], jnp.where(first, gh_ref[...], 0.0), wg_ref[...], bg_ref[...], 3)
        val = _conv_apply(v_ref[...], jnp.where(first, vh_ref[...], 0.0), wv_ref[...], bv_ref[...], 3)
        o_ref[...] = (_gelu_parts(gate)[0] * val).astype(BF16)
        gate_ref[...] = gate.astype(BF16)
        val_ref[...] = val.astype(BF16)

    tile = pl.BlockSpec((tm, tc), lambda i, j: (i, j))
    return pl.pallas_call(
        body, name="ffn_act_fwd", grid=(t // tm, nj),
        in_specs=[tile, pl.BlockSpec((8, tc), lambda i, j: (halo(i), j)),
                  pl.BlockSpec((tm, tc), lambda i, j: (i, nj + j)), pl.BlockSpec((8, tc), lambda i, j: (halo(i), nj + j)),
                  pl.BlockSpec((3, tc), lambda i, j: (0, j)), pl.BlockSpec((3, tc), lambda i, j: (0, nj + j)),
                  pl.BlockSpec((1, tc), lambda i, j: (0, j)), pl.BlockSpec((1, tc), lambda i, j: (0, nj + j))],
        out_specs=[tile] * 3, out_shape=[SDS((t, FFN_D_FF), BF16)] * 3,
        compiler_params=_cp(dimension_semantics=("arbitrary", "arbitrary")),
    )(up_raw, up_raw, up_raw, up_raw, w, w, b, b)


def _ffn_act_bwd(gate, val, d_act, *, tm, tc=1408):
    t = gate.shape[0]
    tm = min(tm, t)
    nj = FFN_D_FF // tc

    def body(g_ref, v_ref, da_ref, dg_ref, dv_ref, dbg_ref, dbv_ref):
        i = pl.program_id(1)
        val = v_ref[...].astype(F32)
        ge, dge = _gelu_parts(g_ref[...].astype(F32))
        da = da_ref[...].astype(F32)
        d_gate = da * val * dge
        d_val = da * ge
        dg_ref[...] = d_gate.astype(BF16)
        dv_ref[...] = d_val.astype(BF16)
        _accumulate(dbg_ref, i == 0, jnp.sum(d_gate, axis=0, keepdims=True))
        _accumulate(dbv_ref, i == 0, jnp.sum(d_val, axis=0, keepdims=True))

    tile = pl.BlockSpec((tm, tc), lambda j, i: (i, j))
    row = pl.BlockSpec((1, tc), lambda j, i: (0, j))
    return pl.pallas_call(
        body, name="ffn_act_bwd", grid=(nj, t // tm), in_specs=[tile] * 3, out_specs=[tile, tile, row, row],
        out_shape=[SDS((t, FFN_D_FF), BF16), SDS((t, FFN_D_FF), BF16), SDS((1, FFN_D_FF), F32), SDS((1, FFN_D_FF), F32)],
        compiler_params=_cp(dimension_semantics=("arbitrary", "arbitrary")),
    )(gate, val, d_act)


def _softplus(v):
    e = jnp.exp(-jnp.abs(v))
    small = e * (1.0 - 0.5 * e)
    return jnp.maximum(v, 0.0) + jnp.where(e < 1e-4, small, jnp.log(1.0 + e))


def _dt_fwd(proj, bias_pad, *, tm):
    t = proj.shape[0]
    tm = min(tm, t)

    def body(x_ref, b_ref, o_ref):
        o_ref[...] = _softplus(x_ref[...] + b_ref[...])

    return pl.pallas_call(
        body, name="dt_fwd", grid=(t // tm,),
        in_specs=[pl.BlockSpec((tm, 128), lambda i: (i, OFF_DT // 128)), pl.BlockSpec((1, 128), lambda i: (0, 0))],
        out_specs=pl.BlockSpec((tm, 128), lambda i: (i, 0)), out_shape=SDS((t, 128), F32),
        compiler_params=_cp(dimension_semantics=("arbitrary",)),
    )(proj, bias_pad)


def _dt_bwd(d_dt, proj, bias_pad, d_proj, *, tm):
    t = proj.shape[0]
    tm = min(tm, t)

    def body(g_ref, x_ref, b_ref, _, o_ref, db_ref):
        d_raw = g_ref[...] * _sigmoid(x_ref[...] + b_ref[...])
        o_ref[:, 0:128] = d_raw.astype(BF16)
        o_ref[:, 128:512] = jnp.zeros((tm, 384), BF16)
        _accumulate(db_ref, pl.program_id(0) == 0, jnp.sum(d_raw, axis=0, keepdims=True))

    return pl.pallas_call(
        body, name="dt_bwd", grid=(t // tm,),
        in_specs=[pl.BlockSpec((tm, 128), lambda i: (i, 0)), pl.BlockSpec((tm, 128), lambda i: (i, OFF_DT // 128)),
                  pl.BlockSpec((1, 128), lambda i: (0, 0)), pl.BlockSpec(memory_space=pl.ANY)],
        out_specs=[pl.BlockSpec((tm, 512), lambda i: (i, OFF_DT // 512)), pl.BlockSpec((1, 128), lambda i: (0, 0))],
        out_shape=[SDS((t, PROJ_W), BF16), SDS((1, 128), F32)],
        input_output_aliases={3: 0},
        compiler_params=_cp(dimension_semantics=("arbitrary",)),
    )(d_dt, proj, bias_pad, d_proj)


def _split3(v):
    hi = v.astype(BF16)
    r1 = v - hi.astype(F32)
    mid = r1.astype(BF16)
    return hi, mid, (r1 - mid.astype(F32)).astype(BF16)


def _times01(v, m3):
    return jnp.dot(jnp.concatenate(_split3(v), axis=1), m3, preferred_element_type=F32)


def _01times(m3, v):
    return jnp.dot(m3, jnp.concatenate(_split3(v), axis=0), preferred_element_type=F32)


def _ssd_decay(dt_ref, dtT_ref, al_ref, alT_ref, k):
    dt = dt_ref[0]
    a_row = -jnp.exp(al_ref[0])
    adt_t = dtT_ref[0] * (-jnp.exp(alT_ref[0]))
    return dt, a_row, _01times(k['low3'][...], dt * a_row), _times01(adt_t, k['up3v'][...])


def _ssd_specs(nc, rev):
    ci = (lambda c: nc - 1 - c) if rev else (lambda c: c)
    return [pl.BlockSpec((CHUNK, SSD_CONV_DIM), lambda c: (ci(c), 0)),
            pl.BlockSpec((SSD_N_GROUPS, CHUNK, 128), lambda c: (0, ci(c), 0)),
            pl.BlockSpec((SSD_N_GROUPS, 8, CHUNK), lambda c: (0, 0, ci(c))),
            pl.BlockSpec((SSD_N_GROUPS, 1, 128), lambda c: (0, 0, 0)),
            pl.BlockSpec((SSD_N_GROUPS, 8, 1), lambda c: (0, 0, 0)),
            pl.BlockSpec((1, SSD_D_INNER), lambda c: (0, 0))]


def _ssd_group_views(g, x_ref, dt_ref, dtT_ref, al_ref, alT_ref, d_ref):
    return (x_ref.at[:, g * GROUP_W:(g + 1) * GROUP_W], dt_ref.at[g:g + 1], dtT_ref.at[g:g + 1], al_ref.at[g:g + 1],
            alT_ref.at[g:g + 1], d_ref.at[:, g * 512:(g + 1) * 512])


NT = (((1,), (1,)), ((), ()))
WIDE = 8 * CHUNK
SSD_CONST_NAMES = ('e128', 'e64', 's64', 'mlo', 'mup', 'low3', 'up3', 'up3v')
SSD_CONST_SHAPES = [pltpu.VMEM((3 * CHUNK, WIDE), BF16), pltpu.VMEM((3 * CHUNK, 512), BF16), pltpu.VMEM((512, CHUNK), BF16),
                    pltpu.VMEM((CHUNK, WIDE), F32), pltpu.VMEM((CHUNK, WIDE), F32), pltpu.VMEM((CHUNK, 3 * CHUNK), BF16),
                    pltpu.VMEM((CHUNK, 3 * CHUNK), BF16), pltpu.VMEM((3 * CHUNK, CHUNK), BF16)]


def _ssd_init_consts(k):
    row, col = _iota((3 * CHUNK, WIDE), 0), _iota((3 * CHUNK, WIDE), 1)
    k['e128'][...] = ((col >> 7) == (row & 127)).astype(BF16)
    k['e64'][...] = ((_iota((3 * CHUNK, 512), 1) >> 6) == (_iota((3 * CHUNK, 512), 0) & 127)).astype(BF16)
    k['s64'][...] = ((_iota((512, CHUNK), 0) >> 6) == _iota((512, CHUNK), 1)).astype(BF16)
    row, col = _iota((CHUNK, WIDE), 0), _iota((CHUNK, WIDE), 1)
    k['mlo'][...] = (row >= (col & 127)).astype(F32)
    k['mup'][...] = (row <= (col & 127)).astype(F32)
    row, col = _iota((CHUNK, 3 * CHUNK), 0), _iota((CHUNK, 3 * CHUNK), 1) & 127
    k['low3'][...] = (row >= col).astype(BF16)
    k['up3'][...] = (row <= col).astype(BF16)
    row, col = _iota((3 * CHUNK, CHUNK), 0) & 127, _iota((3 * CHUNK, CHUNK), 1)
    k['up3v'][...] = (row <= col).astype(BF16)


def _ssd_common(x_ref, dt_ref, dtT_ref, al_ref, alT_ref, k):
    dt, a_row, acs, acs_t = _ssd_decay(dt_ref, dtT_ref, al_ref, alT_ref, k)
    ecol = _times01(acs, k['e128'][...])
    rrow = jnp.concatenate([jnp.broadcast_to(acs_t[j:j + 1, :], (CHUNK, CHUNK)) for j in range(8)], axis=1)
    a64 = _times01(acs, k['e64'][...])
    dt64 = _times01(dt, k['e64'][...])
    a_end64 = a64[CHUNK - 1:CHUNK, :]
    xs = x_ref[:, 0:512]
    return dict(dt=dt, a_row=a_row, acs=acs, seg=ecol - rrow, dt64=dt64, e_a=jnp.exp(a64), decay=jnp.exp(a_end64 - a64),
                e_end64=jnp.exp(a_end64), xs=xs, xdt=xs * dt64, bm=x_ref[:, 512:640], cm=x_ref[:, 640:768])


def _pair_blocks(v):
    lo = _iota((CHUNK, 128), 1) < 64
    out = []
    for i in range(4):
        ch = v[:, i * 128:(i + 1) * 128]
        out.append(jnp.concatenate([jnp.where(lo, ch, 0.0), jnp.where(lo, 0.0, ch)], axis=0).astype(BF16))
    return out


def _tile8(m):
    return jnp.concatenate([m] * 8, axis=1)


def _ssd_fwd(xc, dtg, dtg_t, alog, alog_t, d_exp):
    t = xc.shape[0]
    nc = t // CHUNK

    def body(xa_ref, dta_ref, dtTa_ref, ala_ref, alTa_ref, da_ref, ya_ref, hs_ref, h_scr, *consts):
        c = pl.program_id(0)
        k = dict(zip(SSD_CONST_NAMES, consts))

        @pl.when(c == 0)
        def _():
            _ssd_init_consts(k)
            h_scr[...] = jnp.zeros_like(h_scr)

        for g in range(SSD_N_GROUPS):
            x_ref, dt_ref, dtT_ref, al_ref, alT_ref, d_ref = _ssd_group_views(g, xa_ref, dta_ref, dtTa_ref, ala_ref, alTa_ref, da_ref)
            v = _ssd_common(x_ref, dt_ref, dtT_ref, al_ref, alT_ref, k)
            b16, c16 = v['bm'].astype(BF16), v['cm'].astype(BF16)
            cb = lax.dot_general(c16, b16, NT, preferred_element_type=F32)
            m16 = (jnp.exp(jnp.minimum(v['seg'], 0.0)) * k['mlo'][...] * _tile8(cb)).astype(BF16)
            xbd = _pair_blocks(v['xdt'])
            y_diag = jnp.concatenate([jnp.dot(m16[:, i * 256:(i + 1) * 256], xbd[i], preferred_element_type=F32)
                                      for i in range(4)], axis=1)
            ht = h_scr[g]
            y_off = jnp.dot(c16, ht.astype(BF16), preferred_element_type=F32)
            ya_ref[:, g * 512:(g + 1) * 512] = y_diag + v['e_a'] * y_off + d_ref[...] * v['xs']
            st = jnp.dot(v['bm'].T.astype(BF16), (v['xdt'] * v['decay']).astype(BF16), preferred_element_type=F32)
            hs_ref[0, g] = ht
            h_scr[g] = ht * v['e_end64'] + st

    return pl.pallas_call(
        body, name="ssd_fwd", grid=(nc,), in_specs=_ssd_specs(nc, False),
        out_specs=[pl.BlockSpec((CHUNK, SSD_D_INNER), lambda c: (c, 0)),
                   pl.BlockSpec((1, SSD_N_GROUPS, SSD_D_STATE, 512), lambda c: (c, 0, 0, 0))],
        out_shape=[SDS((t, SSD_D_INNER), F32), SDS((nc, SSD_N_GROUPS, SSD_D_STATE, 512), F32)],
        scratch_shapes=[pltpu.VMEM((SSD_N_GROUPS, SSD_D_STATE, 512), F32)] + SSD_CONST_SHAPES,
        compiler_params=_cp(dimension_semantics=("arbitrary",)),
    )(xc, dtg, dtg_t, alog, alog_t, d_exp)


def _ssd_bwd(xc, dtg, dtg_t, alog, alog_t, d_exp, d_y, hs, ride=None):
    t = xc.shape[0]
    nc = t // CHUNK

    r_ops, r_in_specs, r_outs, r_out_specs, r_scratch = _ride_parts(ride)

    def body(xa_ref, dta_ref, dtTa_ref, ala_ref, alTa_ref, da_ref, dya_ref, hs_ref, *rest):
        r_in, rest = rest[:len(r_ops)], rest[len(r_ops):]
        dxa_ref, ddta_ref, dal_ref, dd_ref = rest[:4]
        r_out, rest = rest[4:4 + len(r_outs)], rest[4 + len(r_outs):]
        g_scr, consts, sems = rest[0], rest[1:1 + len(SSD_CONST_NAMES)], rest[1 + len(SSD_CONST_NAMES):]
        c = pl.program_id(0)
        _ride_run(ride, c == 0, c == nc - 1, r_in, r_out, sems)
        k = dict(zip(SSD_CONST_NAMES, consts))

        @pl.when(c == 0)
        def _():
            _ssd_init_consts(k)
            g_scr[...] = jnp.zeros_like(g_scr)

        for g in range(SSD_N_GROUPS):
            views = _ssd_group_views(g, xa_ref, dta_ref, dtTa_ref, ala_ref, alTa_ref, da_ref)
            one_group(c, g, k, *views, dya_ref.at[:, g * 512:(g + 1) * 512], hs_ref, g_scr,
                      dxa_ref.at[:, g * GROUP_W:(g + 1) * GROUP_W], ddta_ref.at[g:g + 1], dal_ref, dd_ref)

    def one_group(c, g, k, x_ref, dt_ref, dtT_ref, al_ref, alT_ref, d_ref, dy_ref, hs_ref, g_scr, dx_ref, ddt_ref,
                  dal_ref, dd_ref):
        s64, mlo, mup = k['s64'], k['mlo'], k['mup']
        v = _ssd_common(x_ref, dt_ref, dtT_ref, al_ref, alT_ref, k)
        dt, a_row, xs, xdt, e_a, decay = v['dt'], v['a_row'], v['xs'], v['xdt'], v['e_a'], v['decay']
        row, col = _iota((CHUNK, CHUNK), 0), _iota((CHUNK, CHUNK), 1)
        b16, c16 = v['bm'].astype(BF16), v['cm'].astype(BF16)
        ct16 = v['cm'].T.astype(BF16)
        cb = lax.dot_general(c16, b16, NT, preferred_element_type=F32)
        cbt = lax.dot_general(b16, c16, NT, preferred_element_type=F32)
        lmat = jnp.exp(jnp.minimum(v['seg'], 0.0)) * mlo[...]
        lmat_t = jnp.exp(jnp.minimum(-v['seg'], 0.0)) * mup[...]
        mmat, mmat_t = lmat * _tile8(cb), lmat_t * _tile8(cbt)
        mt16 = mmat_t.astype(BF16)
        dy = dy_ref[...]
        dye, xdec = dy * e_a, xdt * decay
        dy16, dye16, xdec16 = dy.astype(BF16), dye.astype(BF16), xdec.astype(BF16)
        xdt16 = xdt.astype(BF16)
        ht, gt = hs_ref[0, g], g_scr[g]
        ht16, gt16 = ht.astype(BF16), gt.astype(BF16)
        xbd, dybd = _pair_blocks(xdt), _pair_blocks(dy)
        d_m, d_mt, d_x = [], [], []
        for i in range(4):
            csl = slice(i * 128, (i + 1) * 128)
            d_m.append(lax.dot_general(dy16[:, csl], xbd[i], NT, preferred_element_type=F32))
            d_mt.append(lax.dot_general(xdt16[:, csl], dybd[i], NT, preferred_element_type=F32))
            d_x.append(jnp.dot(mt16[:, i * 256:(i + 1) * 256], dybd[i], preferred_element_type=F32))
        d_m, d_mt, d_x = jnp.concatenate(d_m, axis=1), jnp.concatenate(d_mt, axis=1), jnp.concatenate(d_x, axis=1)

        def head_sum(m):
            acc = m[:, 0:CHUNK]
            for j in range(1, 8):
                acc = acc + m[:, j * CHUNK:(j + 1) * CHUNK]
            return acc

        def seg64(p):
            return jnp.dot(p.astype(BF16), s64[...], preferred_element_type=F32)

        d_cb16 = head_sum(d_m * lmat).astype(BF16)
        d_cbt16 = head_sum(d_mt * lmat_t).astype(BF16)
        dseg = d_m * mmat - d_mt * mmat_t
        da_seg = jnp.zeros((CHUNK, CHUNK), F32)
        for j in range(8):
            da_seg = jnp.where(col == j, jnp.sum(dseg[:, j * CHUNK:(j + 1) * CHUNK], axis=1, keepdims=True), da_seg)
        ch = jnp.dot(c16, ht16, preferred_element_type=F32)
        bg = jnp.dot(b16, gt16, preferred_element_type=F32)
        d_x = d_x + decay * bg
        d_decay = seg64(xdec * bg)
        e_end = jnp.exp(v['acs'][CHUNK - 1:CHUNK, :])
        d_end = e_end * jnp.sum(seg64(gt * ht), axis=0, keepdims=True) + jnp.sum(d_decay, axis=0, keepdims=True)
        d_a = seg64(dye * ch) - d_decay + da_seg + jnp.where(row == CHUNK - 1, d_end, 0.0)
        dx_ref[:, 0:512] = d_x * v['dt64'] + d_ref[...] * dy
        dx_ref[:, 640:768] = (lax.dot_general(dye16, ht16, NT, preferred_element_type=F32)
                              + jnp.dot(d_cb16, b16, preferred_element_type=F32))
        dx_ref[:, 512:640] = (lax.dot_general(xdec16, gt16, NT, preferred_element_type=F32)
                              + jnp.dot(d_cbt16, c16, preferred_element_type=F32))
        g_scr[g] = gt * v['e_end64'] + jnp.dot(ct16, dye16, preferred_element_type=F32)
        d_adt = _01times(k['up3'][...], d_a)
        ddt_ref[0] = d_adt * a_row + seg64(d_x * xs)
        d_alog = jnp.sum(d_adt * dt, axis=0, keepdims=True) * a_row
        dd_row = jnp.sum(seg64(dy * xs), axis=0, keepdims=True)
        first = c == 0

        @pl.when(first)
        def _():
            dal_ref[g] = d_alog
            dd_ref[g] = dd_row

        @pl.when(jnp.logical_not(first))
        def _():
            dal_ref[g] += d_alog
            dd_ref[g] += dd_row

    rc = lambda c: nc - 1 - c
    whole = pl.BlockSpec((SSD_N_GROUPS, 1, 128), lambda c: (0, 0, 0))
    return pl.pallas_call(
        body, name="ssd_bwd", grid=(nc,),
        in_specs=_ssd_specs(nc, True) + [pl.BlockSpec((CHUNK, SSD_D_INNER), lambda c: (rc(c), 0)),
                                        pl.BlockSpec((1, SSD_N_GROUPS, SSD_D_STATE, 512), lambda c: (rc(c), 0, 0, 0))] + r_in_specs,
        out_specs=[pl.BlockSpec((CHUNK, SSD_CONV_DIM), lambda c: (rc(c), 0)),
                   pl.BlockSpec((SSD_N_GROUPS, CHUNK, 128), lambda c: (0, rc(c), 0)), whole, whole] + r_out_specs,
        out_shape=[SDS((t, SSD_CONV_DIM), F32), SDS((SSD_N_GROUPS, t, 128), F32),
                   SDS((SSD_N_GROUPS, 1, 128), F32), SDS((SSD_N_GROUPS, 1, 128), F32)] + r_outs,
        scratch_shapes=[pltpu.VMEM((SSD_N_GROUPS, SSD_D_STATE, 512), F32)] + SSD_CONST_SHAPES + r_scratch,
        compiler_params=_cp(dimension_semantics=("arbitrary",)),
    )(xc, dtg, dtg_t, alog, alog_t, d_exp, d_y, hs, *r_ops)


def _gated_norm_fwd(y, proj, w, *, tm):
    t = y.shape[0]
    tm = min(tm, t)

    def body(y_ref, z_ref, w_ref, o_ref):
        gv = y_ref[...] * _silu_parts(z_ref[...])[0]
        r = lax.rsqrt(jnp.mean(gv * gv, axis=-1, keepdims=True) + NORM_EPS)
        o_ref[...] = (gv * r * w_ref[...]).astype(BF16)

    tile = pl.BlockSpec((tm, 512), lambda i, g: (i, g))
    return pl.pallas_call(
        body, name="gated_norm_fwd", grid=(t // tm, SSD_N_GROUPS),
        in_specs=[tile, pl.BlockSpec((tm, 512), lambda i, g: (i, OFF_Z // 512 + g)),
                  pl.BlockSpec((1, 512), lambda i, g: (0, g))], out_specs=tile,
        out_shape=SDS((t, SSD_D_INNER), BF16),
        compiler_params=_cp(dimension_semantics=("arbitrary", "arbitrary")),
    )(y, proj, w)


def _rope(ch, cos_t, sin_t):
    first = (_iota(ch.shape, 1) & 32) == 0
    partner = jnp.where(first, pltpu.roll(ch, 96, axis=1), pltpu.roll(ch, 32, axis=1))
    return ch * cos_t + partner * sin_t


def _rope_qkv(proj, cos_t, sin_t, *, tm):
    t = proj.shape[0]
    tm = min(tm, t)

    def body(q_ref, k_ref, v_ref, c_ref, s_ref, qr_ref, kp_ref, vp_ref, kt_ref, vt_ref):
        cv, sv = c_ref[...], s_ref[...]
        lo = _iota((tm, 128), 1) < 64
        for m in range(8):
            sl = slice(m * 128, (m + 1) * 128)
            qr_ref[:, sl] = (_rope(q_ref[:, sl], cv, sv) * 0.125).astype(BF16)
        for m2 in range(2):
            sl = slice(m2 * 128, (m2 + 1) * 128)
            for src, dst, dst_t in ((_rope(k_ref[:, sl], cv, sv), kp_ref, kt_ref), (v_ref[:, sl], vp_ref, vt_ref)):
                sw = pltpu.roll(src, 64, axis=1)
                padded = (jnp.where(lo, src, 0.0), jnp.where(lo, 0.0, sw), jnp.where(lo, sw, 0.0), jnp.where(lo, 0.0, src))
                for i, pad in enumerate(padded):
                    rows = slice((4 * m2 + i) * 128, (4 * m2 + i + 1) * 128)
                    dst[:, rows] = pad.astype(BF16)
                    dst_t[rows, :] = pad.T.astype(BF16)

    return pl.pallas_call(
        body, name="rope_qkv", grid=(t // tm,),
        in_specs=[pl.BlockSpec((tm, 1024), lambda i: (i, OFF_Q // 1024)), pl.BlockSpec((tm, 256), lambda i: (i, OFF_K // 256)),
                  pl.BlockSpec((tm, 256), lambda i: (i, OFF_V // 256)), pl.BlockSpec((tm, 128), lambda i: (i, 0)),
                  pl.BlockSpec((tm, 128), lambda i: (i, 0))],
        out_specs=[pl.BlockSpec((tm, 1024), lambda i: (i, 0))] * 3 + [pl.BlockSpec((1024, tm), lambda i: (0, i))] * 2,
        out_shape=[SDS((t, 1024), BF16)] * 3 + [SDS((1024, t), BF16)] * 2,
        compiler_params=_cp(dimension_semantics=("arbitrary",)),
    )(proj, proj, proj, cos_t, sin_t)


def _attn_valid(n):
    kj, qi = _iota((2 * CHUNK, CHUNK), 0), _iota((2 * CHUNK, CHUNK), 1)
    return (kj > qi) & (kj <= qi + CHUNK) & ((n > 0) | (kj >= CHUNK))


def _attn_fwd(qr, kp, vt, sinks):
    t = qr.shape[0]
    nb = t // CHUNK

    def body(q_ref, kc_ref, kprev_ref, vc_ref, vprev_ref, sk_ref, o_ref, lse_ref):
        n = pl.program_id(0)
        valid = _attn_valid(n)
        head_row = _iota((16, CHUNK), 0)
        lse_all = jnp.zeros((16, CHUNK), F32)
        for m in range(8):
            g = m // 2
            qch = q_ref[:, m * 128:(m + 1) * 128]
            o_t = jnp.zeros((128, CHUNK), F32)
            for e in range(2):
                h = 2 * m + e
                sl = slice((2 * g + e) * 128, (2 * g + e + 1) * 128)
                kk = jnp.concatenate([kprev_ref[:, sl], kc_ref[:, sl]], axis=0)
                vv_t = jnp.concatenate([vprev_ref[sl, :], vc_ref[sl, :]], axis=1)
                s = jnp.where(valid, lax.dot_general(kk, qch, NT, preferred_element_type=F32), NEG)
                sink = sk_ref[0:1, h:h + 1]
                mx = jnp.maximum(jnp.max(s, axis=0, keepdims=True), sink)
                p = jnp.exp(s - mx)
                den = jnp.sum(p, axis=0, keepdims=True) + jnp.exp(sink - mx)
                o_t = o_t + jnp.dot(vv_t, p.astype(BF16), preferred_element_type=F32) * (1.0 / den)
                lse_all = jnp.where(head_row == h, mx + jnp.log(den), lse_all)
            o_ref[:, m * 128:(m + 1) * 128] = o_t.T.astype(BF16)
        lse_ref[0] = lse_all

    cur = pl.BlockSpec((CHUNK, 1024), lambda n: (n, 0))
    prev = pl.BlockSpec((CHUNK, 1024), lambda n: (jnp.maximum(n - 1, 0), 0))
    cur_t = pl.BlockSpec((1024, CHUNK), lambda n: (0, n))
    prev_t = pl.BlockSpec((1024, CHUNK), lambda n: (0, jnp.maximum(n - 1, 0)))
    return pl.pallas_call(
        body, name="attn_fwd", grid=(nb,),
        in_specs=[cur, cur, prev, cur_t, prev_t, pl.BlockSpec((1, 128), lambda n: (0, 0))],
        out_specs=[cur, pl.BlockSpec((1, 16, CHUNK), lambda n: (n, 0, 0))],
        out_shape=[SDS((t, 1024), BF16), SDS((nb, 16, CHUNK), F32)],
        compiler_params=_cp(dimension_semantics=("arbitrary",)),
    )(qr, kp, kp, vt, vt, sinks)


def _attn_bwd(qr, kp, vp, kt, d_o, o, lse, sinks, cos_t, sin_t, d_proj, ride=None):
    t = qr.shape[0]
    nb = t // CHUNK

    r_ops, r_in_specs, r_outs, r_out_specs, r_scratch = _ride_parts(ride)

    def body(q_ref, kc_ref, kprev_ref, vc_ref, vprev_ref, ktc_ref, ktprev_ref, do_ref, o_ref, lse_ref, sk_ref,
             c_ref, s_ref, cp_ref, sp_ref, _, *rest):
        r_in, rest = rest[:len(r_ops)], rest[len(r_ops):]
        dqkv_ref, dsk_ref = rest[:2]
        r_out, rest = rest[2:2 + len(r_outs)], rest[2 + len(r_outs):]
        acc_k, acc_v, dq_scr = rest[:3]
        n = pl.program_id(0)
        _ride_run(ride, n == 0, n == nb, r_in, r_out, rest[3:])
        lane = _iota((CHUNK, 128), 1)
        lo = lane < 64
        lane1 = _iota((1, 128), 1)

        @pl.when(n == 0)
        def _():
            acc_k[...] = jnp.zeros_like(acc_k)
            acc_v[...] = jnp.zeros_like(acc_v)
            dsk_ref[...] = jnp.zeros((1, 128), F32)

        @pl.when(n > 0)
        def _():
            dqkv_ref[:, 0:1024] = dq_scr[...]
            for r in range(8):
                acc_k[r, 0:CHUNK] = acc_k[r, CHUNK:2 * CHUNK]
                acc_v[r, 0:CHUNK] = acc_v[r, CHUNK:2 * CHUNK]
                acc_k[r, CHUNK:2 * CHUNK] = jnp.zeros((CHUNK, 128), F32)
                acc_v[r, CHUNK:2 * CHUNK] = jnp.zeros((CHUNK, 128), F32)

        @pl.when(n < nb)
        def _():
            valid = _attn_valid(n)
            lse_all = lse_ref[0]
            dsk = jnp.zeros((1, 128), F32)
            for m in range(8):
                g = m // 2
                csl = slice(m * 128, (m + 1) * 128)
                qch = q_ref[:, csl]
                doch = do_ref[:, csl]
                prod_t = (doch.astype(F32) * o_ref[:, csl].astype(F32)).T
                dq_t = jnp.zeros((128, CHUNK), F32)
                for e in range(2):
                    h = 2 * m + e
                    sl = slice((2 * g + e) * 128, (2 * g + e + 1) * 128)
                    kk = jnp.concatenate([kprev_ref[:, sl], kc_ref[:, sl]], axis=0)
                    vv = jnp.concatenate([vprev_ref[:, sl], vc_ref[:, sl]], axis=0)
                    kk_t = jnp.concatenate([ktprev_ref[sl, :], ktc_ref[sl, :]], axis=1)
                    lse_h = lse_all[h:h + 1, :]
                    s = lax.dot_general(kk, qch, NT, preferred_element_type=F32)
                    p = jnp.exp(jnp.where(valid, s, NEG) - lse_h)
                    delta = jnp.sum(prod_t[64 * e:64 * (e + 1)], axis=0, keepdims=True)
                    d_p = lax.dot_general(vv, doch, NT, preferred_element_type=F32)
                    d_s16 = (p * (d_p - delta)).astype(BF16)
                    dq_t = dq_t + jnp.dot(kk_t, d_s16, preferred_element_type=F32)
                    acc_k[2 * g + e] += jnp.dot(d_s16, qch, preferred_element_type=F32)
                    acc_v[2 * g + e] += jnp.dot(p.astype(BF16), doch, preferred_element_type=F32)
                    p_sink = jnp.exp(sk_ref[0:1, h:h + 1] - lse_h)
                    dsk = jnp.where(lane1 == h, -jnp.sum(p_sink * delta), dsk)
                dq_scr[:, csl] = (_rope(dq_t.T, c_ref[...], -s_ref[...]) * 0.125).astype(BF16)
            dsk_ref[...] += dsk

        @pl.when(n > 0)
        def _():
            for m2 in range(2):
                halves = []
                for g in (2 * m2, 2 * m2 + 1):
                    for acc in (acc_k, acc_v):
                        comb = jnp.where(lo, acc[2 * g, 0:CHUNK], acc[2 * g + 1, 0:CHUNK])
                        halves.append(comb + pltpu.roll(comb, 64, axis=1))
                d_kr = jnp.where(lo, halves[0], halves[2])
                d_v = jnp.where(lo, halves[1], halves[3])
                dqkv_ref[:, OFF_K + m2 * 128:OFF_K + (m2 + 1) * 128] = _rope(d_kr, cp_ref[...], -sp_ref[...]).astype(BF16)
                dqkv_ref[:, OFF_V + m2 * 128:OFF_V + (m2 + 1) * 128] = d_v.astype(BF16)

    qn = lambda n: jnp.minimum(n, nb - 1)
    pn = lambda n: jnp.maximum(jnp.minimum(n, nb) - 1, 0)
    cur = pl.BlockSpec((CHUNK, 1024), lambda n: (qn(n), 0))
    prev = pl.BlockSpec((CHUNK, 1024), lambda n: (pn(n), 0))
    cur128 = pl.BlockSpec((CHUNK, 128), lambda n: (qn(n), 0))
    prev128 = pl.BlockSpec((CHUNK, 128), lambda n: (pn(n), 0))
    cur_t = pl.BlockSpec((1024, CHUNK), lambda n: (0, qn(n)))
    prev_t = pl.BlockSpec((1024, CHUNK), lambda n: (0, pn(n)))
    one = pl.BlockSpec((1, 128), lambda n: (0, 0))
    return pl.pallas_call(
        body, name="attn_bwd", grid=(nb + 1,),
        in_specs=[cur, cur, prev, cur, prev, cur_t, prev_t, cur, cur, pl.BlockSpec((1, 16, CHUNK), lambda n: (qn(n), 0, 0)),
                  one, cur128, cur128, prev128, prev128, pl.BlockSpec(memory_space=pl.ANY)] + r_in_specs,
        out_specs=[pl.BlockSpec((CHUNK, 1536), lambda n: (pn(n), 0)), one] + r_out_specs,
        out_shape=[SDS((t, PROJ_W), BF16), SDS((1, 128), F32)] + r_outs,
        scratch_shapes=[pltpu.VMEM((8, 2 * CHUNK, 128), F32), pltpu.VMEM((8, 2 * CHUNK, 128), F32),
                        pltpu.VMEM((CHUNK, 1024), BF16)] + r_scratch,
        input_output_aliases={15: 0},
        compiler_params=_cp(dimension_semantics=("arbitrary",)),
    )(qr, kp, kp, vp, vp, kt, kt, d_o, o, lse, sinks, cos_t, sin_t, cos_t, sin_t, d_proj, *r_ops)


def _adamw(name, w, g, m, v, *, tr):
    rows, cols = w.shape
    tr = min(tr, rows)
    assert rows % tr == 0

    def body(w_ref, g_ref, m_ref, v_ref, d_ref, nm_ref, nv_ref):
        gv = g_ref[...]
        nm = ADAM_B1 * m_ref[...] + (1.0 - ADAM_B1) * gv
        nv = ADAM_B2 * v_ref[...] + (1.0 - ADAM_B2) * (gv * gv)
        m_hat = nm / (1.0 - ADAM_B1 ** ADAM_STEP)
        v_hat = nv / (1.0 - ADAM_B2 ** ADAM_STEP)
        d_ref[...] = -ADAM_LR * (m_hat / (jnp.sqrt(v_hat) + ADAM_EPS) + ADAM_WD * w_ref[...])
        nm_ref[...] = nm
        nv_ref[...] = nv

    tile = pl.BlockSpec((tr, cols), lambda i: (i, 0))
    return pl.pallas_call(
        body, name=name, grid=(rows // tr,), in_specs=[tile] * 4, out_specs=[tile] * 3,
        out_shape=[SDS((rows, cols), F32)] * 3, compiler_params=_cp(dimension_semantics=("arbitrary",)),
    )(w, g, m, v)


def _local_step(x, cos_t, sin_t, tgt, wb, ps, late=None, rides=None):
    t = x.shape[0]
    tm = min(512, t)
    tmw = min(1024, t)
    ij = lambda i, j, k: (i, j)
    i0 = lambda i, j, k: (i, 0)
    c0 = lambda i, j, k: (0, 0)
    cj = lambda i, j, k: (0, j)
    rides = rides or (lambda group, grads: None)
    rode = {}

    tkt = min(2048, t)
    proj, u, *arrived = _norm_mm("in_proj", x, ps['norm_mix_pre_w'], wb['cat'], tm=tmw, tn=1024,
                                 ride=late[0] if late else None)
    if late:
        more_wb, more_ps = late[1](arrived)
        wb, ps = {**wb, **more_wb}, {**ps, **more_ps}
    xc, xc_pre = _conv_silu_fwd(proj, ps['ssd_conv_w'], ps['ssd_conv_b'], tm=tm)
    bias_pad = jnp.pad(ps['ssd_dt_bias'], ((0, 0), (0, 96)))
    dt = _dt_fwd(proj, bias_pad, tm=tmw)
    dt32 = dt[:, :SSD_N_HEADS].reshape(t, SSD_N_GROUPS, 8)
    dtg = jnp.pad(dt32.transpose(1, 0, 2), ((0, 0), (0, 0), (0, 120)))
    dtg_t = dt32.transpose(1, 2, 0)
    alog = jnp.pad(ps['ssd_a_log'].reshape(SSD_N_GROUPS, 1, 8), ((0, 0), (0, 0), (0, 120)))
    alog_t = ps['ssd_a_log'].reshape(SSD_N_GROUPS, 8, 1)
    d_exp = jnp.repeat(ps['ssd_d'], SSD_HEAD_DIM, axis=1)
    y, hs = _ssd_fwd(xc, dtg, dtg_t, alog, alog_t, d_exp)
    gn = _gated_norm_fwd(y, proj, ps['ssd_norm_w'], tm=tmw)
    qr, kp, vp, kt, vt = _rope_qkv(proj, cos_t, sin_t, tm=tm)
    sinks = jnp.pad(ps['attn_sinks'], ((0, 0), (0, 112)))
    ao, lse = _attn_fwd(qr, kp, vt, sinks)
    y_attn = _mm_plain("attn_out", ao, wb['ao'], tm=tmw, tn=512, tk=1024)

    def merge_ep(acc, i, j, ins, outs):
        gs, ga, ya = ins
        outs[0][...] = (_sigmoid(gs[...]) * acc + _sigmoid(ga[...]) * ya[...]).astype(BF16)
        outs[1][...] = acc

    merged, y_ssd = _mm_call(
        "ssd_out_merge", gn, wb['so'], tm=tmw, tn=512, tk=2048, epilogue=merge_ep,
        extra_in=[(proj, (tmw, 512), lambda i, j, k: (i, OFF_GS // 512 + j)),
                  (proj, (tmw, 512), lambda i, j, k: (i, OFF_GA // 512 + j)), (y_attn, (tmw, 512), ij)],
        outs=[((t, D_MODEL), BF16, (tmw, 512), ij), ((t, D_MODEL), F32, (tmw, 512), ij)])

    def mix_ep(acc, i, j, ins, outs):
        xv, wn = ins
        r = lax.rsqrt(jnp.mean(acc * acc, axis=-1, keepdims=True) + NORM_EPS)
        outs[0][...] = xv[...] + acc * r * wn[...]
        outs[1][...] = acc

    x1, mmix = _mm_call(
        "mix_out", merged, wb['mix'], tm=tm, tn=D_MODEL, tk=1024, epilogue=mix_ep,
        extra_in=[(x, (tm, D_MODEL), i0), (ps['norm_mix_post_w'], (1, D_MODEL), c0)],
        outs=[((t, D_MODEL), F32, (tm, D_MODEL), i0), ((t, D_MODEL), F32, (tm, D_MODEL), i0)])

    up_raw, h = _norm_mm("ffn_up", x1, ps['norm_ffn_pre_w'], wb['up'], tm=tmw, tn=1408)
    act, ffn_gate, ffn_val = _ffn_act_fwd(up_raw, ps['ffn_conv_w'], ps['ffn_conv_b'], tm=tm)

    def loss_ep(acc, i, j, ins, outs):
        x1v, tg, wn = ins
        d_ff_ref, dout_ref, loss_ref, dw_ref = outs
        wv = wn[...]
        r = lax.rsqrt(jnp.mean(acc * acc, axis=-1, keepdims=True) + NORM_EPS)
        err = x1v[...] + acc * r * wv - tg[...]
        dout = err * (1.0 / D_MODEL)
        dout_ref[...] = dout
        d_ff, dw = _rms_bwd(acc, wv, dout)
        d_ff_ref[...] = d_ff.astype(BF16)
        _accumulate(dw_ref, i == 0, dw)
        _accumulate(loss_ref, i == 0, jnp.sum(err * err, keepdims=True) * (0.5 / D_MODEL))

    d_ff, dout, loss, g_norm_ffn_post = _mm_call(
        "ffn_down_loss", act, wb['dn'], tm=tm, tn=D_MODEL, tk=FFN_D_FF, epilogue=loss_ep,
        extra_in=[(x1, (tm, D_MODEL), i0), (tgt, (tm, D_MODEL), i0), (ps['norm_ffn_post_w'], (1, D_MODEL), c0)],
        outs=[((t, D_MODEL), BF16, (tm, D_MODEL), i0), ((t, D_MODEL), F32, (tm, D_MODEL), i0),
              ((1, 1), F32, (1, 1), c0), ((1, D_MODEL), F32, (1, D_MODEL), c0)])

    d_act = _mm_plain("d_act", d_ff, wb['dn_t'], tm=tmw, tn=1408, tk=1024, out_dtype=BF16)
    g_w_down = _mm_plain("g_w_down", act, d_ff, tm=1408, tn=1024, tk=tkt, trans_a=True, out_dtype=BF16)
    d_gate, d_val, db_g, db_v = _ffn_act_bwd(ffn_gate, ffn_val, d_act, tm=tm)
    d_up_raw, gcw_g = _conv_bwd2("ffn_conv_bwd2_gate", d_gate, up_raw, 0, ps['ffn_conv_w'][:, :FFN_D_FF], tm=tm,
                                 tc=1408, out_cols=2 * FFN_D_FF, out_col0=0)
    d_up_raw, gcw_v = _conv_bwd2("ffn_conv_bwd2_val", d_val, up_raw, FFN_D_FF, ps['ffn_conv_w'][:, FFN_D_FF:], tm=tm,
                                 tc=1408, out_cols=2 * FFN_D_FF, out_col0=FFN_D_FF, fill=d_up_raw)
    g_ffn_conv_w = jnp.concatenate([gcw_g, gcw_v], axis=1)

    def dx1_ep(acc, i, j, ins, outs):
        x1v, wpre, dout_v, mmv, wpost = ins
        d_x1_ref, d_mm_ref, dwpre_ref, dwpost_ref = outs
        d_n, dw_pre = _rms_bwd(x1v[...], wpre[...], acc)
        d_x1 = dout_v[...] + d_n
        d_x1_ref[...] = d_x1
        d_mm, dw_post = _rms_bwd(mmv[...], wpost[...], d_x1)
        d_mm_ref[...] = d_mm.astype(BF16)
        _accumulate(dwpre_ref, i == 0, dw_pre)
        _accumulate(dwpost_ref, i == 0, dw_post)

    d_x1, d_mm, g_norm_ffn_pre, g_norm_mix_post = _mm_call(
        "d_h", d_up_raw, wb['up_t'], tm=tm, tn=D_MODEL, tk=FFN_D_FF, epilogue=dx1_ep,
        extra_in=[(x1, (tm, D_MODEL), i0), (ps['norm_ffn_pre_w'], (1, D_MODEL), c0), (dout, (tm, D_MODEL), i0),
                  (mmix, (tm, D_MODEL), i0), (ps['norm_mix_post_w'], (1, D_MODEL), c0)],
        outs=[((t, D_MODEL), F32, (tm, D_MODEL), i0), ((t, D_MODEL), BF16, (tm, D_MODEL), i0),
              ((1, D_MODEL), F32, (1, D_MODEL), c0), ((1, D_MODEL), F32, (1, D_MODEL), c0)])
    g_w_up_t = _mm_plain("g_w_up", d_up_raw, h, tm=1408, tn=1024, tk=tkt, trans_a=True, out_dtype=BF16)
    ride_ffn = rides('ffn', {'ffn_w_up': g_w_up_t, 'ffn_w_down': g_w_down})

    def dmerge_ep(acc, i, j, ins, outs):
        gs, ga, ys, ya = ins
        sg_s, sg_a = _sigmoid(gs[...]), _sigmoid(ga[...])
        outs[0][...] = (acc * sg_s).astype(BF16)
        outs[1][...] = (acc * sg_a).astype(BF16)
        outs[2][:, 0:D_MODEL] = (acc * ys[...] * sg_s * (1.0 - sg_s)).astype(BF16)
        outs[2][:, D_MODEL:2 * D_MODEL] = (acc * ya[...] * sg_a * (1.0 - sg_a)).astype(BF16)

    d_yssd, d_yattn, d_proj = _mm_call(
        "d_merged", d_mm, wb['mix_t'], tm=tm, tn=D_MODEL, tk=1024, epilogue=dmerge_ep,
        extra_in=[(proj, (tm, D_MODEL), lambda i, j, k: (i, OFF_GS // D_MODEL)),
                  (proj, (tm, D_MODEL), lambda i, j, k: (i, OFF_GA // D_MODEL)), (y_ssd, (tm, D_MODEL), i0), (y_attn, (tm, D_MODEL), i0)],
        outs=[((t, D_MODEL), BF16, (tm, D_MODEL), i0), ((t, D_MODEL), BF16, (tm, D_MODEL), i0),
              ((t, PROJ_W), BF16, (tm, 2 * D_MODEL), lambda i, j, k: (i, OFF_GS // (2 * D_MODEL)))])
    g_w_mix = _mm_plain("g_w_mix", merged, d_mm, tm=1024, tn=1024, tk=tkt, trans_a=True, out_dtype=BF16)

    def dgn_ep(acc, i, j, ins, outs):
        yv, zv, wn = ins
        d_y_ref, d_z_ref, dw_ref = outs
        zz = zv[...]
        sz = _sigmoid(zz)
        silu = zz * sz
        gv = yv[...] * silu
        r = lax.rsqrt(jnp.mean(gv * gv, axis=-1, keepdims=True) + NORM_EPS)
        gh = gv * r
        dgh = acc * wn[...]
        dg = r * (dgh - gh * jnp.mean(dgh * gh, axis=-1, keepdims=True))
        d_y_ref[...] = dg * silu
        d_z_ref[...] = (dg * yv[...] * (sz * (1.0 + zz * (1.0 - sz)))).astype(BF16)
        dw = jnp.sum(acc * gh, axis=0, keepdims=True)

        @pl.when(i == 0)
        def _():
            dw_ref[j] = dw

        @pl.when(i > 0)
        def _():
            dw_ref[j] += dw

    d_y, d_proj, g_ssd_norm = _mm_call(
        "d_gn", d_yssd, wb['so_t'], tm=tm, tn=512, tk=1024, epilogue=dgn_ep, fill=(d_proj, 1),
        extra_in=[(y, (tm, 512), ij), (proj, (tm, 512), lambda i, j, k: (i, OFF_Z // 512 + j)), (ps['ssd_norm_w'], (1, 512), cj)],
        outs=[((t, SSD_D_INNER), F32, (tm, 512), ij), ((t, PROJ_W), BF16, (tm, 512), lambda i, j, k: (i, OFF_Z // 512 + j)),
              ((SSD_N_GROUPS, 1, 512), F32, (SSD_N_GROUPS, 1, 512), lambda i, j, k: (0, 0, 0))])
    g_ssd_norm = g_ssd_norm.reshape(1, SSD_D_INNER)
    g_w_so = _mm_plain("g_w_so", gn, d_yssd, tm=1024, tn=1024, tk=tkt, trans_a=True, out_dtype=BF16)
    d_xc, d_dtg, d_alog, d_dd, *rode['ffn'] = _ssd_bwd(xc, dtg, dtg_t, alog, alog_t, d_exp, d_y, hs, ride=ride_ffn)
    d_pre, g_ssd_conv_b = _conv_silu_bwd1(d_xc, xc_pre, tm=tm)
    d_proj, g_ssd_conv_w = _conv_bwd2("ssd_conv_bwd2", d_pre, proj, OFF_XBC, ps['ssd_conv_w'], tm=tm, tc=1536,
                                      out_cols=PROJ_W, out_col0=OFF_XBC, fill=d_proj)
    d_dt = jnp.pad(d_dtg[:, :, :8].transpose(1, 0, 2).reshape(t, SSD_N_HEADS), ((0, 0), (0, 96)))
    d_proj, g_dt_bias = _dt_bwd(d_dt, proj, bias_pad, d_proj, tm=tmw)

    d_ao = _mm_plain("d_ao", d_yattn, wb['ao_t'], tm=tmw, tn=512, tk=1024, out_dtype=BF16)
    g_w_ao = _mm_plain("g_w_ao", ao, d_yattn, tm=1024, tn=1024, tk=tkt, trans_a=True, out_dtype=BF16)
    ride_mix = rides('mix', {'ssd_w_out': g_w_so, 'attn_w_out': g_w_ao, 'w_mix_out': g_w_mix})
    d_proj, g_sinks, *rode['mix'] = _attn_bwd(qr, kp, vp, kt, d_ao, ao, lse, sinks, cos_t, sin_t, d_proj, ride=ride_mix)

    def dx_ep(acc, i, j, ins, outs):
        xv, wn, dx1v = ins
        d_n, dw = _rms_bwd(xv[...], wn[...], acc)
        outs[0][...] = dx1v[...] + d_n
        _accumulate(outs[1], i == 0, dw)

    g_cat_t = _mm_plain("g_w_in", d_proj, u, tm=1024, tn=1024, tk=tkt, trans_a=True)
    grad_x, g_norm_mix_pre, *rode['w_in'] = _mm_call(
        "d_u", d_proj, wb['cat_t'], tm=tm, tn=D_MODEL, tk=2304, epilogue=dx_ep, ride=rides('w_in', {'w_in': g_cat_t}),
        extra_in=[(x, (tm, D_MODEL), i0), (ps['norm_mix_pre_w'], (1, D_MODEL), c0), (d_x1, (tm, D_MODEL), i0)],
        outs=[((t, D_MODEL), F32, (tm, D_MODEL), i0), ((1, D_MODEL), F32, (1, D_MODEL), c0)])

    grads = {
        'norm_mix_pre_w': g_norm_mix_pre, 'w_in': g_cat_t, 'ssd_conv_w': g_ssd_conv_w, 'ssd_conv_b': g_ssd_conv_b,
        'ssd_dt_bias': g_dt_bias[:, :SSD_N_HEADS], 'ssd_a_log': d_alog[:, 0, :8].reshape(1, SSD_N_HEADS),
        'ssd_d': d_dd[:, 0, :8].reshape(1, SSD_N_HEADS), 'ssd_norm_w': g_ssd_norm, 'ssd_w_out': g_w_so,
        'attn_sinks': g_sinks[:, :ATTN_N_HEADS], 'attn_w_out': g_w_ao, 'w_mix_out': g_w_mix,
        'norm_mix_post_w': g_norm_mix_post, 'norm_ffn_pre_w': g_norm_ffn_pre, 'ffn_w_up': g_w_up_t,
        'ffn_conv_w': g_ffn_conv_w, 'ffn_conv_b': jnp.concatenate([db_g, db_v], axis=1), 'ffn_w_down': g_w_down,
        'norm_ffn_post_w': g_norm_ffn_post,
    }
    return loss, grad_x, grads, rode


def _group_channels(a):
    parts = []
    for g in range(SSD_N_GROUPS):
        parts += [a[..., 512 * g:512 * (g + 1)], a[..., 2048 + 128 * g:2048 + 128 * (g + 1)],
                  a[..., 2560 + 128 * g:2560 + 128 * (g + 1)]]
    return jnp.concatenate(parts, axis=-1)


def _ungroup_channels(a):
    xs = [a[..., GROUP_W * g:GROUP_W * g + 512] for g in range(SSD_N_GROUPS)]
    bs = [a[..., GROUP_W * g + 512:GROUP_W * g + 640] for g in range(SSD_N_GROUPS)]
    cs = [a[..., GROUP_W * g + 640:GROUP_W * (g + 1)] for g in range(SSD_N_GROUPS)]
    return jnp.concatenate(xs + bs + cs, axis=-1)


def _proj_rows(a_t, lo, hi):
    out = []
    for start, length, dst in sorted(PROJ_SEGS):
        s, e = max(lo, start), min(hi, start + length)
        if s < e:
            out.append(a_t[dst + s - start:dst + e - start])
    return out


def _to_proj_layout(w_in_t):
    pieces, pos = [], 0
    for start, length, dst in sorted(PROJ_SEGS, key=lambda s: s[2]):
        if dst > pos:
            pieces.append(jnp.zeros((dst - pos, w_in_t.shape[1]), w_in_t.dtype))
        pieces.append(w_in_t[start:start + length])
        pos = dst + length
    if pos < PROJ_W:
        pieces.append(jnp.zeros((PROJ_W - pos, w_in_t.shape[1]), w_in_t.dtype))
    return jnp.concatenate(pieces, axis=0)


def _rope_tables(positions):
    half = 32
    inv_freq = ROPE_THETA ** (-jnp.arange(half, dtype=F32) * 2.0 / 64)
    ang = positions.astype(F32)[:, None] * inv_freq
    cos, sin = jnp.cos(ang), jnp.sin(ang)
    return jnp.concatenate([cos, cos, cos, cos], axis=1), jnp.concatenate([-sin, sin, -sin, sin], axis=1)


def _matmul_weights(w_in_t):
    cat_t = _to_proj_layout(w_in_t)
    return {'cat': cat_t.T, 'cat_t': cat_t}


def _late_weights(so, ao, mix, up_t, dn):
    return {'so': so, 'so_t': so.T, 'ao': ao, 'ao_t': ao.T, 'mix': mix, 'mix_t': mix.T,
            'up': up_t.T, 'up_t': up_t, 'dn': dn, 'dn_t': dn.T}


ANY = pl.BlockSpec(memory_space=pl.ANY)
MESH = pl.DeviceIdType.MESH
ROW_ALIGN = 32


def _mesh_pos():
    return lax.axis_index("x"), lax.axis_index("y"), lax.axis_index("c")


def _other_chips(x, y):
    return [(1 - x, y), (x, 1 - y), (1 - x, 1 - y)]


def _remote(src, dst, send_sems, recv_sems, k, to):
    return pltpu.make_async_remote_copy(src_ref=src, dst_ref=dst, send_sem=send_sems.at[k], recv_sem=recv_sems.at[k],
                                        device_id=to, device_id_type=MESH)


def _half(c, rh):
    return pl.ds(pl.multiple_of(c * rh, 16), rh)


def _ag_ride(shard):
    r = shard.shape[0]
    rh = r // 2

    def first_copies(w_ref, out_ref, send_sems, recv_sems):
        x, y, c = _mesh_pos()
        p = 2 * x + y
        mine = _half(c, rh)
        cps = [_remote(w_ref, out_ref.at[p], send_sems, recv_sems, 6, (x, y, 1 - c))]
        return cps + [_remote(w_ref.at[mine], out_ref.at[p, mine], send_sems, recv_sems, j, (cx, cy, c))
                      for j, (cx, cy) in enumerate(_other_chips(x, y))]

    def start(ins, outs, send_sems, recv_sems):
        for cp in first_copies(ins[0], outs[0], send_sems, recv_sems):
            cp.start()

    def forwards(out_ref, send_sems, recv_sems, half):
        x, y, c = _mesh_pos()
        return [_remote(out_ref.at[2 * cx + cy, half], out_ref.at[2 * cx + cy, half], send_sems, recv_sems, 3 + j, (x, y, 1 - c))
                for j, (cx, cy) in enumerate(_other_chips(x, y))]

    def middle(ins, outs, send_sems, recv_sems):
        x, y, c = _mesh_pos()
        mine = _half(c, rh)
        for j, (fwd, (cx, cy)) in enumerate(zip(forwards(outs[0], send_sems, recv_sems, mine), _other_chips(x, y))):
            slab = outs[0].at[2 * cx + cy, mine]
            _remote(slab, slab, send_sems, recv_sems, j, (x, y, 1 - c)).wait_recv()
            fwd.start()

    def finish(ins, outs, send_sems, recv_sems):
        w_ref, out_ref = ins[0], outs[0]
        x, y, c = _mesh_pos()
        for cp in forwards(out_ref, send_sems, recv_sems, _half(1 - c, rh)):
            cp.wait_recv()
        _remote(w_ref, out_ref.at[2 * x + y], send_sems, recv_sems, 6, (x, y, 1 - c)).wait_recv()
        for cp in first_copies(w_ref, out_ref, send_sems, recv_sems) + forwards(out_ref, send_sems, recv_sems, _half(c, rh)):
            cp.wait_send()

    return _Ride((shard,), (SDS((N_CHIPS, r, COMM_LANES), shard.dtype),), 7, start, finish, middle)


def _rs_ride(gbuf):
    rh = gbuf.shape[1] // 2

    def copies(g_ref, r_ref, send_sems, recv_sems, landing):
        x, y, c = _mesh_pos()
        cps = []
        for k, (cx, cy) in enumerate(_other_chips(x, y)):
            for h in range(2):
                slot = 2 * k + c if landing else 2 * k + h
                cps.append(pltpu.make_async_remote_copy(
                    src_ref=g_ref.at[2 * cx + cy, pl.ds(h * rh, rh)], dst_ref=r_ref.at[slot],
                    send_sem=send_sems.at[2 * k + h], recv_sem=recv_sems.at[slot],
                    device_id=(cx, cy, h), device_id_type=MESH))
        cps.append(_remote(g_ref.at[2 * x + y, _half(1 - c, rh)], r_ref.at[6], send_sems, recv_sems, 6, (x, y, 1 - c)))
        return cps

    def start(ins, outs, send_sems, recv_sems):
        for cp in copies(ins[0], outs[0], send_sems, recv_sems, True):
            cp.start()

    def finish(ins, outs, send_sems, recv_sems):
        for cp in copies(ins[0], outs[0], send_sems, recv_sems, False):
            cp.wait()

    return _Ride((gbuf,), (SDS((7, rh, COMM_LANES), gbuf.dtype),), 7, start, finish)


def _rs_sum(name, gbuf, got, pc_idx):
    rh = got.shape[1]
    tr = max(d for d in range(16, 513, 16) if rh % d == 0)
    nb = rh // tr

    def body(pc_ref, own_ref, *refs):
        o_ref = refs[7]
        p, c = pc_ref[0], pc_ref[1]
        own = own_ref[0].astype(F32)
        slots = [r[0].astype(F32) for r in refs[:7]]

        def term(q, h):
            code = p ^ q
            far = jnp.where(code == 2, slots[h], jnp.where(code == 1, slots[2 + h], slots[4 + h]))
            return jnp.where(code == 0, jnp.where(c == h, own, slots[6]), far)

        acc = term(0, 0)
        for q, h in [(0, 1), (1, 0), (1, 1), (2, 0), (2, 1), (3, 0), (3, 1)]:
            acc = acc + term(q, h)
        o_ref[0] = acc

    slot = lambda s: pl.BlockSpec((1, tr, COMM_LANES), lambda i, pc: (s, i, 0))
    return pl.pallas_call(
        body, name=name,
        grid_spec=pltpu.PrefetchScalarGridSpec(
            num_scalar_prefetch=1, grid=(nb,),
            in_specs=[pl.BlockSpec((1, tr, COMM_LANES), lambda i, pc: (pc[0], pc[1] * nb + i, 0))] + [slot(s) for s in range(7)],
            out_specs=pl.BlockSpec((1, tr, COMM_LANES), lambda i, pc: (pc[1], i, 0))),
        out_shape=SDS((2, rh, COMM_LANES), F32), compiler_params=_cp(dimension_semantics=("arbitrary",)),
    )(pc_idx, gbuf, *([got] * 7))


def _pair_gather_all(bufs):
    n = len(bufs)

    def body(*refs):
        outs, send_sems, recv_sems = refs[n:2 * n], refs[2 * n], refs[2 * n + 1]
        x, y, c = _mesh_pos()
        cps = [_remote(o.at[c], o.at[c], send_sems, recv_sems, k, (x, y, 1 - c)) for k, o in enumerate(outs)]
        for cp in cps:
            cp.start()
        for k, o in enumerate(outs):
            _remote(o.at[1 - c], o.at[1 - c], send_sems, recv_sems, k, (x, y, 1 - c)).wait_recv()
        for cp in cps:
            cp.wait_send()

    return pl.pallas_call(
        body, name="grad_pair_gather", in_specs=[ANY] * n, out_specs=[ANY] * n,
        out_shape=[SDS(b.shape, b.dtype) for b in bufs],
        scratch_shapes=[pltpu.SemaphoreType.DMA((n,)), pltpu.SemaphoreType.DMA((n,))],
        input_output_aliases={k: k for k in range(n)},
    )(*bufs)


def _pack_rows(big, small=()):
    parts = list(big)
    if small:
        flat = jnp.concatenate([p.reshape(-1) for p in small])
        k = -(-flat.shape[0] // (16 * COMM_LANES)) * 16
        parts.append(jnp.pad(flat, (0, k * COMM_LANES - flat.shape[0])).reshape(k, COMM_LANES))
    pad = -sum(p.shape[0] for p in parts) % ROW_ALIGN
    if pad:
        parts.append(jnp.zeros((pad, COMM_LANES), parts[0].dtype))
    return jnp.concatenate(parts, axis=0) if len(parts) > 1 else parts[0]


def _take(flat, off, shape):
    n = 1
    for d in shape:
        n *= d
    return flat[off:off + n].reshape(shape), off + n


BIG_ROWS = {'w_in': 2184, 'ssd_w_out': 512, 'attn_w_out': 256, 'w_mix_out': 256, 'ffn_w_up': 1408, 'ffn_w_down': 704}
TRANSPOSED = ('w_in', 'ffn_w_up')
LATE = ('ssd_w_out', 'attn_w_out', 'w_mix_out', 'ffn_w_up', 'ffn_w_down')
CONV_TAPS = ('ssd_conv_w', 'ffn_conv_w')
RS_GROUPS = {'ffn': ('ffn_w_up', 'ffn_w_down'), 'mix': ('ssd_w_out', 'attn_w_out', 'w_mix_out'), 'w_in': ('w_in',)}


def _exchange(name, ride):
    n_in, n_out = len(ride.ins), len(ride.outs)

    def body(*refs):
        ins, outs, sems = refs[:n_in], refs[n_in:n_in + n_out], refs[n_in + n_out:]
        ride.start(ins, outs, *sems)
        if ride.middle is not None:
            ride.middle(ins, outs, *sems)
        ride.finish(ins, outs, *sems)

    return pl.pallas_call(
        body, name=name, in_specs=[ANY] * n_in, out_specs=[ANY] * n_out, out_shape=list(ride.outs),
        scratch_shapes=[pltpu.SemaphoreType.DMA((ride.n_sems,)), pltpu.SemaphoreType.DMA((ride.n_sems,))],
    )(*ride.ins)


def kernel(x, positions, norm_mix_pre_w, w_in, ssd_conv_w, ssd_conv_b, ssd_dt_bias, ssd_a_log, ssd_d, ssd_norm_w, ssd_w_out, attn_sinks, attn_w_out, w_mix_out, norm_mix_post_w, norm_ffn_pre_w, ffn_w_up, ffn_conv_w, ffn_conv_b, ffn_w_down, norm_ffn_post_w, loss_target, m_norm_mix_pre_w, m_w_in, m_ssd_conv_w, m_ssd_conv_b, m_ssd_dt_bias, m_ssd_a_log, m_ssd_d, m_ssd_norm_w, m_ssd_w_out, m_attn_sinks, m_attn_w_out, m_w_mix_out, m_norm_mix_post_w, m_norm_ffn_pre_w, m_ffn_w_up, m_ffn_conv_w, m_ffn_conv_b, m_ffn_w_down, m_norm_ffn_post_w, v_norm_mix_pre_w, v_w_in, v_ssd_conv_w, v_ssd_conv_b, v_ssd_dt_bias, v_ssd_a_log, v_ssd_d, v_ssd_norm_w, v_ssd_w_out, v_attn_sinks, v_attn_w_out, v_w_mix_out, v_norm_mix_post_w, v_norm_ffn_pre_w, v_ffn_w_up, v_ffn_conv_w, v_ffn_conv_b, v_ffn_w_down, v_norm_ffn_post_w):
    given = dict(locals())
    w = {n: given[n][0] for n in WEIGHTS}
    w = {n: (a if a.ndim == 2 else a[None]) for n, a in w.items()}
    mom_m = {n: given['m_' + n].reshape(w[n].shape) for n in WEIGHTS}
    mom_v = {n: given['v_' + n].reshape(w[n].shape) for n in WEIGHTS}
    cx, cy, cc = _mesh_pos()
    pc_idx = jnp.stack([2 * cx + cy, cc]).astype(jnp.int32)

    rows_of = lambda n: (w[n].T if n in TRANSPOSED else w[n]).astype(BF16)
    gathered = _exchange("w_in_all_gather", _ag_ride(_pack_rows([rows_of('w_in')])))[0]
    wb = _matmul_weights(jnp.concatenate([gathered[s, :BIG_ROWS['w_in']] for s in range(N_CHIPS)], axis=0))
    taps = [lax.bitcast_convert_type(w[n], BF16) for n in CONV_TAPS]

    def unpack_late(arrived):
        rows, conv = {n: [] for n in LATE}, {n: [] for n in CONV_TAPS}
        for s in range(N_CHIPS):
            r0 = 0
            for n in LATE:
                rows[n].append(arrived[0][s, r0:r0 + BIG_ROWS[n]])
                r0 += BIG_ROWS[n]
            flat, off = arrived[0][s, r0:r0 + 16].reshape(-1), 0
            for n in CONV_TAPS:
                a, off = _take(flat, off, w[n].shape + (2,))
                conv[n].append(lax.bitcast_convert_type(a, F32))
        full = {n: jnp.concatenate(rows[n], axis=0) for n in LATE}
        return (_late_weights(*[full[n] for n in LATE]),
                {'ssd_conv_w': _group_channels(jnp.concatenate(conv['ssd_conv_w'], axis=1)),
                 'ffn_conv_w': jnp.concatenate(conv['ffn_conv_w'], axis=1)})

    late = (_ag_ride(_pack_rows([rows_of(n) for n in LATE], taps)), unpack_late)

    sent = {}

    def rides(group, g):
        parts = []
        for s in range(N_CHIPS):
            slab = []
            for n in RS_GROUPS[group]:
                lo, hi = BIG_ROWS[n] * s, BIG_ROWS[n] * (s + 1)
                slab += _proj_rows(g[n], lo, hi) if n == 'w_in' else [g[n][lo:hi]]
            slab = [a.astype(BF16) for a in slab]
            pad = -sum(a.shape[0] for a in slab) % ROW_ALIGN
            parts += slab + ([jnp.zeros((pad, COMM_LANES), BF16)] if pad else [])
        sent[group] = jnp.concatenate(parts, axis=0).reshape(N_CHIPS, -1, COMM_LANES)
        return _rs_ride(sent[group])

    ps = {n: w[n] for n in REPLICATED}
    ps['ssd_conv_b'] = _group_channels(w['ssd_conv_b'])
    cos_t, sin_t = _rope_tables(positions[0])
    loss, grad_x, grads, rode = _local_step(x[0], cos_t, sin_t, loss_target[0], wb, ps, late, rides)
    grads['ssd_conv_w'] = _ungroup_channels(grads['ssd_conv_w'])
    grads['ssd_conv_b'] = _ungroup_channels(grads['ssd_conv_b'])

    shard_cols = {n: sh[1] for n, _, sh in SHARDED}
    parts = []
    for s in range(N_CHIPS):
        small = [grads[n][:, shard_cols[n] * s:shard_cols[n] * (s + 1)] for n in CONV_TAPS] + [grads[n] for n in REPLICATED]
        flat = _pack_rows([], small)
        high = flat.astype(BF16)
        parts += [high, (flat - high.astype(F32)).astype(BF16)]
    sent['small'] = jnp.concatenate(parts, axis=0).reshape(N_CHIPS, -1, COMM_LANES)
    rode['small'] = _exchange("grad_small_exchange", _rs_ride(sent['small']))

    groups = ('ffn', 'mix', 'w_in', 'small')
    red = _pair_gather_all([_rs_sum("grad_sum_" + g, sent[g], rode[g][0], pc_idx) for g in groups])
    red = {g: r.reshape(-1, COMM_LANES) for g, r in zip(groups, red)}
    g_red = {}
    for g in groups[:3]:
        r0 = 0
        for n in RS_GROUPS[g]:
            g_red[n] = red[g][r0:r0 + BIG_ROWS[n]].T if n in TRANSPOSED else red[g][r0:r0 + BIG_ROWS[n]]
            r0 += BIG_ROWS[n]
    half = red['small'].shape[0] // 2
    flat, off = (red['small'][:half] + red['small'][half:]).reshape(-1), 0
    for n in CONV_TAPS + REPLICATED:
        g_red[n], off = _take(flat, off, w[n].shape)

    small_names = [n for n in WEIGHTS if n not in MATMUL_WEIGHTS]
    delta, new_m, new_v = {}, {}, {}
    for n in MATMUL_WEIGHTS:
        delta[n], new_m[n], new_v[n] = _adamw("adamw_" + n, w[n], g_red[n], mom_m[n], mom_v[n],
                                                  tr=max(d for d in range(8, 353, 8) if w[n].shape[0] % d == 0))
    packed = [_pack_small([d[n] for n in small_names]) for d in (w, g_red, mom_m, mom_v)]
    outs = _adamw("adamw_small", *packed, tr=packed[0].shape[0])
    for res, o in zip((delta, new_m, new_v), outs):
        fl, off = o.reshape(-1), 0
        for n in small_names:
            res[n], off = _take(fl, off, w[n].shape)

    loss_all = lax.psum(loss[0, 0], ("x", "y", "c"))
    shaped = lambda d: [d[n].reshape(given[n].shape) for n in WEIGHTS]
    return (loss_all, grad_x[None], *shaped(g_red), *shaped(delta), *shaped(new_m), *shaped(new_v))


def _pack_small(pieces):
    flat = jnp.concatenate([p.reshape(-1) for p in pieces])
    rows = -(-flat.shape[0] // (128 * 8)) * 8
    return jnp.pad(flat, (0, rows * 128 - flat.shape[0])).reshape(rows, 128)
```

```python
from typing import Callable, NamedTuple

import jax
import jax.numpy as jnp
from jax import lax
from jax.experimental import pallas as pl
from jax.experimental.pallas import tpu as pltpu

F32 = jnp.float32
BF16 = jnp.bfloat16
SDS = jax.ShapeDtypeStruct
HIGHEST = lax.Precision.HIGHEST

D_MODEL = 1024
SSD_D_INNER = 2048
SSD_N_HEADS = 32
SSD_HEAD_DIM = 64
SSD_N_GROUPS = 4
SSD_HEADS_PER_GROUP = 8
SSD_D_STATE = 128
SSD_CONV_DIM = 3072
CHUNK = 128
ATTN_N_HEADS = 16
KV_WIDTH = 256
FFN_D_FF = 2816
IN_PROJ_DIM = 8736
ROPE_THETA = 10000.0
NORM_EPS = 1e-6
ADAM_LR, ADAM_B1, ADAM_B2, ADAM_EPS, ADAM_WD, ADAM_STEP = 0.001, 0.9, 0.999, 1e-08, 0.01, 10

PROJ_W = 9216
OFF_Q, OFF_K, OFF_V, OFF_Z, OFF_DT, OFF_GS, OFF_GA, OFF_XBC = 0, 1024, 1280, 1536, 3584, 4096, 5120, 6144
GROUP_W = 768
PROJ_SEGS = ([(0, 2048, OFF_Z)]
             + [(2048 + 512 * g, 512, OFF_XBC + GROUP_W * g) for g in range(4)]
             + [(4096 + 128 * g, 128, OFF_XBC + GROUP_W * g + 512) for g in range(4)]
             + [(4608 + 128 * g, 128, OFF_XBC + GROUP_W * g + 640) for g in range(4)]
             + [(5120, 32, OFF_DT), (5152, 1024, OFF_Q), (6176, 256, OFF_K), (6432, 256, OFF_V),
                (6688, 1024, OFF_GS), (7712, 1024, OFF_GA)])
VMEM_LIMIT_MB = 48
NEG = -1e30

WEIGHTS = ('norm_mix_pre_w', 'w_in', 'ssd_conv_w', 'ssd_conv_b', 'ssd_dt_bias', 'ssd_a_log', 'ssd_d', 'ssd_norm_w',
           'ssd_w_out', 'attn_sinks', 'attn_w_out', 'w_mix_out', 'norm_mix_post_w', 'norm_ffn_pre_w', 'ffn_w_up',
           'ffn_conv_w', 'ffn_conv_b', 'ffn_w_down', 'norm_ffn_post_w')
SHARDED = (('w_in', 1, (1024, 2184)), ('ssd_conv_w', 1, (4, 768)), ('ssd_w_out', 0, (512, 1024)),
           ('attn_w_out', 0, (256, 1024)), ('w_mix_out', 0, (256, 1024)), ('ffn_w_up', 1, (1024, 1408)),
           ('ffn_conv_w', 1, (3, 1408)), ('ffn_w_down', 0, (704, 1024)))
MATMUL_WEIGHTS = ('w_in', 'ssd_w_out', 'attn_w_out', 'w_mix_out', 'ffn_w_up', 'ffn_w_down')
REPLICATED = tuple(n for n in WEIGHTS if n not in {s[0] for s in SHARDED})
N_CHIPS = 4
COMM_LANES = 1024


def _cp(vmem_mb=VMEM_LIMIT_MB, **kw):
    return pltpu.CompilerParams(vmem_limit_bytes=vmem_mb << 20, **kw)


class _Ride(NamedTuple):
    ins: tuple
    outs: tuple
    n_sems: int
    start: Callable
    finish: Callable
    middle: Callable = None


def _ride_parts(ride):
    if ride is None:
        return [], [], [], [], []
    hbm = pl.BlockSpec(memory_space=pl.ANY)
    return (list(ride.ins), [hbm] * len(ride.ins), list(ride.outs), [hbm] * len(ride.outs),
            [pltpu.SemaphoreType.DMA((ride.n_sems,)), pltpu.SemaphoreType.DMA((ride.n_sems,))])


def _ride_run(ride, first, last, in_refs, out_refs, sems, middle=None):
    if ride is None:
        return

    @pl.when(first)
    def _():
        ride.start(in_refs, out_refs, *sems)

    if ride.middle is not None:
        @pl.when(last if middle is None else middle)
        def _():
            ride.middle(in_refs, out_refs, *sems)

    @pl.when(last)
    def _():
        ride.finish(in_refs, out_refs, *sems)


def _iota(shape, axis):
    return lax.broadcasted_iota(jnp.int32, shape, axis)


def _sigmoid(v):
    return 1.0 / (1.0 + jnp.exp(-v))


def _mm_call(name, a, b, *, tm, tn, tk, epilogue, outs, extra_in=(), trans_a=False, fill=None, ride=None):
    if trans_a:
        kdim, m = a.shape
    else:
        m, kdim = a.shape
    n = b.shape[1]
    assert b.shape[0] == kdim and m % tm == 0 and n % tn == 0 and kdim % tk == 0, (name, a.shape, b.shape, tm, tn, tk)
    gi, gj, gk = m // tm, n // tn, kdim // tk
    n_in, n_out = len(extra_in), len(outs)

    n_fill = 0 if fill is None else 1
    r_ops, r_in_specs, r_outs, r_out_specs, r_scratch = _ride_parts(ride)

    def body(a_ref, b_ref, *rest):
        ins = rest[:n_in]
        rest = rest[n_in + n_fill:]
        r_in, rest = rest[:len(r_ops)], rest[len(r_ops):]
        out_refs, rest = rest[:n_out], rest[n_out:]
        r_out, scratch = rest[:len(r_outs)], rest[len(r_outs):]
        i, j, k = pl.program_id(0), pl.program_id(1), pl.program_id(2)
        _ride_run(ride, (i == 0) & (j == 0) & (k == 0), (i == gi - 1) & (j == gj - 1) & (k == gk - 1),
                  r_in, r_out, scratch[-2:])
        av = a_ref[...].astype(BF16)
        bv = b_ref[...].astype(BF16)
        if trans_a:
            part = lax.dot_general(av, bv, (((0,), (0,)), ((), ())), preferred_element_type=F32)
        else:
            part = jnp.dot(av, bv, preferred_element_type=F32)
        if gk == 1:
            epilogue(part, i, j, ins, out_refs)
        else:
            acc = scratch[0]

            @pl.when(k == 0)
            def _():
                acc[...] = part

            @pl.when(k > 0)
            def _():
                acc[...] += part

            @pl.when(k == gk - 1)
            def _():
                epilogue(acc[...], i, j, ins, out_refs)

    a_spec = pl.BlockSpec((tk, tm), lambda i, j, k: (k, i)) if trans_a else pl.BlockSpec((tm, tk), lambda i, j, k: (i, k))
    in_specs = [a_spec, pl.BlockSpec((tk, tn), lambda i, j, k: (k, j))]
    in_specs += [pl.BlockSpec(bs, im) for _, bs, im in extra_in]
    operands = [a, b] + [e[0] for e in extra_in]
    aliases = {}
    if fill is not None:
        in_specs.append(pl.BlockSpec(memory_space=pl.ANY))
        aliases = {len(operands): fill[1]}
        operands.append(fill[0])
    return pl.pallas_call(
        body, name=name, grid=(gi, gj, gk), in_specs=in_specs + r_in_specs,
        out_specs=[pl.BlockSpec(bs, im) for _, _, bs, im in outs] + r_out_specs,
        out_shape=[SDS(s, d) for s, d, _, _ in outs] + r_outs,
        scratch_shapes=([pltpu.VMEM((tm, tn), F32)] if gk > 1 else []) + r_scratch,
        input_output_aliases=aliases,
        compiler_params=_cp(dimension_semantics=("arbitrary", "arbitrary", "arbitrary")),
    )(*operands, *r_ops)


def _mm_plain(name, a, b, *, tm, tn, tk, out_dtype=F32, trans_a=False):
    m = a.shape[1] if trans_a else a.shape[0]

    def epilogue(acc, i, j, ins, outs):
        outs[0][...] = acc.astype(out_dtype)

    return _mm_call(name, a, b, tm=tm, tn=tn, tk=tk, epilogue=epilogue, trans_a=trans_a,
                    outs=[((m, b.shape[1]), out_dtype, (tm, tn), lambda i, j, k: (i, j))])[0]


def _accumulate(ref, first, value):
    @pl.when(first)
    def _():
        ref[...] = value

    @pl.when(jnp.logical_not(first))
    def _():
        ref[...] += value


def _rms_bwd(xv, w, dy):
    r = lax.rsqrt(jnp.mean(xv * xv, axis=-1, keepdims=True) + NORM_EPS)
    xn = xv * r
    dxh = dy * w
    dx = r * (dxh - xn * jnp.mean(dxh * xn, axis=-1, keepdims=True))
    return dx, jnp.sum(dy * xn, axis=0, keepdims=True)


def _norm_mm(name, x, wn, w, *, tm, tn, ride=None):
    t, dm = x.shape
    n = w.shape[1]
    tm = min(tm, t)
    gi, gj = t // tm, n // tn
    r_ops, r_in_specs, r_outs, r_out_specs, r_scratch = _ride_parts(ride)

    def body(x_ref, wn_ref, w_ref, *rest):
        r_in, rest = rest[:len(r_ops)], rest[len(r_ops):]
        o_ref, u_ref = rest[:2]
        r_out, sems = rest[2:2 + len(r_outs)], rest[2 + len(r_outs):]
        i, j = pl.program_id(0), pl.program_id(1)
        _ride_run(ride, (i == 0) & (j == 0), (i == gi - 1) & (j == gj - 1), r_in, r_out, sems,
                  middle=(i == (3 * gi) // 4) & (j == 0) if gi > 1 else None)

        @pl.when(j == 0)
        def _():
            xv = x_ref[...]
            r = lax.rsqrt(jnp.mean(xv * xv, axis=-1, keepdims=True) + NORM_EPS)
            u_ref[...] = (xv * r * wn_ref[...]).astype(BF16)

        o_ref[...] = jnp.dot(u_ref[...], w_ref[...], preferred_element_type=F32)

    return pl.pallas_call(
        body, name=name, grid=(gi, gj),
        in_specs=[pl.BlockSpec((tm, dm), lambda i, j: (i, 0)), pl.BlockSpec((1, dm), lambda i, j: (0, 0)),
                  pl.BlockSpec((dm, tn), lambda i, j: (0, j))] + r_in_specs,
        out_specs=[pl.BlockSpec((tm, tn), lambda i, j: (i, j)), pl.BlockSpec((tm, dm), lambda i, j: (i, 0))] + r_out_specs,
        out_shape=[SDS((t, n), F32), SDS((t, dm), BF16)] + r_outs, scratch_shapes=r_scratch,
        compiler_params=_cp(dimension_semantics=("arbitrary", "arbitrary")),
    )(x, wn, w, *r_ops)


def _shift_down(tile, halo, s):
    if s == 0:
        return tile
    r = pltpu.roll(tile, s, axis=0)
    h = pltpu.roll(halo, s, axis=0)
    head = jnp.where(_iota(h.shape, 0) < s, h, r[0:8])
    return jnp.concatenate([head, r[8:]], axis=0)


def _shift_up(tile, halo, s):
    if s == 0:
        return tile
    n = tile.shape[0]
    r = pltpu.roll(tile, n - s, axis=0)
    h = pltpu.roll(halo, 8 - s, axis=0)
    tail = jnp.where(_iota(h.shape, 0) >= 8 - s, h, r[n - 8:])
    return jnp.concatenate([r[:n - 8], tail], axis=0)


def _conv_apply(tile, halo, wv, bv, kw):
    acc = bv + wv[kw - 1:kw, :] * tile
    for k in range(kw - 1):
        acc = acc + wv[k:k + 1, :] * _shift_down(tile, halo, kw - 1 - k)
    return acc


def _prev_halo_spec(tm, tc, col0):
    return pl.BlockSpec((8, tc), lambda i, j: (jnp.maximum(i * (tm // 8) - 1, 0), col0 + j))


def _silu_parts(pre):
    sg = _sigmoid(pre)
    return pre * sg, sg * (1.0 + pre * (1.0 - sg))


def _conv_silu_fwd(proj, w, b, *, tm, tc=1536):
    t = proj.shape[0]
    c = w.shape[1]
    tm = min(tm, t)
    col0 = OFF_XBC // tc

    def body(x_ref, h_ref, w_ref, b_ref, o_ref, pre_ref):
        halo = jnp.where(pl.program_id(0) > 0, h_ref[...], 0.0)
        pre = _conv_apply(x_ref[...], halo, w_ref[...], b_ref[...], 4)
        o_ref[...] = _silu_parts(pre)[0]
        pre_ref[...] = pre.astype(BF16)

    tile = pl.BlockSpec((tm, tc), lambda i, j: (i, j))
    return pl.pallas_call(
        body, name="ssd_conv_fwd", grid=(t // tm, c // tc),
        in_specs=[pl.BlockSpec((tm, tc), lambda i, j: (i, col0 + j)), _prev_halo_spec(tm, tc, col0),
                  pl.BlockSpec((4, tc), lambda i, j: (0, j)), pl.BlockSpec((1, tc), lambda i, j: (0, j))],
        out_specs=[tile, tile], out_shape=[SDS((t, c), F32), SDS((t, c), BF16)],
        compiler_params=_cp(dimension_semantics=("arbitrary", "arbitrary")),
    )(proj, proj, w, b)


def _conv_silu_bwd1(d_out, pre, *, tm, tc=1536):
    t, c = pre.shape
    tm = min(tm, t)

    def body(g_ref, p_ref, o_ref, db_ref):
        i = pl.program_id(1)
        d_pre = g_ref[...] * _silu_parts(p_ref[...].astype(F32))[1]
        o_ref[...] = d_pre.astype(BF16)
        _accumulate(db_ref, i == 0, jnp.sum(d_pre, axis=0, keepdims=True))

    tile = pl.BlockSpec((tm, tc), lambda j, i: (i, j))
    return pl.pallas_call(
        body, name="ssd_conv_bwd1", grid=(c // tc, t // tm), in_specs=[tile, tile],
        out_specs=[tile, pl.BlockSpec((1, tc), lambda j, i: (0, j))],
        out_shape=[SDS((t, c), BF16), SDS((1, c), F32)],
        compiler_params=_cp(dimension_semantics=("arbitrary", "arbitrary")),
    )(d_out, pre)


def _conv_bwd2(name, d_pre, src, src_col0, w, *, tm, tc, out_cols, out_col0, fill=None):
    t, c = d_pre.shape
    kw = w.shape[0]
    tm = min(tm, t)
    ni = t // tm
    col0 = src_col0 // tc
    ocol0 = out_col0 // tc

    def body(g_ref, gn_ref, x_ref, w_ref, *rest):
        o_ref, dw_ref = rest[-2:]
        i = pl.program_id(1)
        g = g_ref[...].astype(F32)
        g_next = jnp.where(i < ni - 1, gn_ref[...].astype(F32)[0:8], 0.0)
        xv = x_ref[...]
        wv = w_ref[...]
        shifted = [_shift_up(g, g_next, kw - 1 - k) for k in range(kw)]
        d_in = wv[0:1, :] * shifted[0]
        for k in range(1, kw):
            d_in = d_in + wv[k:k + 1, :] * shifted[k]
        o_ref[...] = d_in.astype(o_ref.dtype)
        rows = [jnp.sum(shifted[k] * xv, axis=0, keepdims=True) for k in range(kw)]

        @pl.when(i == 0)
        def _():
            for k in range(kw):
                dw_ref[k:k + 1, :] = rows[k]

        @pl.when(i > 0)
        def _():
            for k in range(kw):
                dw_ref[k:k + 1, :] += rows[k]

    in_specs = [pl.BlockSpec((tm, tc), lambda j, i: (i, j)),
                pl.BlockSpec((16, tc), lambda j, i: (jnp.minimum((i + 1) * (tm // 16), t // 16 - 1), j)),
                pl.BlockSpec((tm, tc), lambda j, i: (i, col0 + j)),
                pl.BlockSpec((kw, tc), lambda j, i: (0, j))]
    operands = [d_pre, d_pre, src, w]
    if fill is not None:
        in_specs.append(pl.BlockSpec(memory_space=pl.ANY))
        operands.append(fill)
    return pl.pallas_call(
        body, name=name, grid=(c // tc, ni), in_specs=in_specs,
        out_specs=[pl.BlockSpec((tm, tc), lambda j, i: (i, ocol0 + j)), pl.BlockSpec((kw, tc), lambda j, i: (0, j))],
        out_shape=[SDS((t, out_cols), BF16), SDS((kw, c), F32)],
        input_output_aliases={} if fill is None else {4: 0},
        compiler_params=_cp(dimension_semantics=("arbitrary", "arbitrary")),
    )(*operands)


GELU_C = 0.7978845608028654


def _gelu_parts(v):
    inner = GELU_C * (v + 0.044715 * v * v * v)
    th = jnp.tanh(inner)
    val = 0.5 * v * (1.0 + th)
    grad = 0.5 * (1.0 + th) + 0.5 * v * (1.0 - th * th) * GELU_C * (1.0 + 3.0 * 0.044715 * v * v)
    return val, grad


def _ffn_act_fwd(up_raw, w, b, *, tm, tc=1408):
    t = up_raw.shape[0]
    tm = min(tm, t)
    nj = FFN_D_FF // tc
    halo = lambda i: jnp.maximum(i * (tm // 8) - 1, 0)

    def body(g_ref, gh_ref, v_ref, vh_ref, wg_ref, wv_ref, bg_ref, bv_ref, o_ref, gate_ref, val_ref):
        first = pl.program_id(0) > 0
        gate = _conv_apply(g_ref[...], jnp.where(first, gh_ref[...], 0.0), wg_ref[...], bg_ref[...], 3)
        val = _conv_apply(v_ref[...], jnp.where(first, vh_ref[...], 0.0), wv_ref[...], bv_ref[...], 3)
        o_ref[...] = (_gelu_parts(gate)[0] * val).astype(BF16)
        gate_ref[...] = gate.astype(BF16)
        val_ref[...] = val.astype(BF16)

    tile = pl.BlockSpec((tm, tc), lambda i, j: (i, j))
    return pl.pallas_call(
        body, name="ffn_act_fwd", grid=(t // tm, nj),
        in_specs=[tile, pl.BlockSpec((8, tc), lambda i, j: (halo(i), j)),
                  pl.BlockSpec((tm, tc), lambda i, j: (i, nj + j)), pl.BlockSpec((8, tc), lambda i, j: (halo(i), nj + j)),
                  pl.BlockSpec((3, tc), lambda i, j: (0, j)), pl.BlockSpec((3, tc), lambda i, j: (0, nj + j)),
                  pl.BlockSpec((1, tc), lambda i, j: (0, j)), pl.BlockSpec((1, tc), lambda i, j: (0, nj + j))],
        out_specs=[tile] * 3, out_shape=[SDS((t, FFN_D_FF), BF16)] * 3,
        compiler_params=_cp(dimension_semantics=("arbitrary", "arbitrary")),
    )(up_raw, up_raw, up_raw, up_raw, w, w, b, b)


def _ffn_act_bwd(gate, val, d_act, *, tm, tc=1408):
    t = gate.shape[0]
    tm = min(tm, t)
    nj = FFN_D_FF // tc

    def body(g_ref, v_ref, da_ref, dg_ref, dv_ref, dbg_ref, dbv_ref):
        i = pl.program_id(1)
        val = v_ref[...].astype(F32)
        ge, dge = _gelu_parts(g_ref[...].astype(F32))
        da = da_ref[...].astype(F32)
        d_gate = da * val * dge
        d_val = da * ge
        dg_ref[...] = d_gate.astype(BF16)
        dv_ref[...] = d_val.astype(BF16)
        _accumulate(dbg_ref, i == 0, jnp.sum(d_gate, axis=0, keepdims=True))
        _accumulate(dbv_ref, i == 0, jnp.sum(d_val, axis=0, keepdims=True))

    tile = pl.BlockSpec((tm, tc), lambda j, i: (i, j))
    row = pl.BlockSpec((1, tc), lambda j, i: (0, j))
    return pl.pallas_call(
        body, name="ffn_act_bwd", grid=(nj, t // tm), in_specs=[tile] * 3, out_specs=[tile, tile, row, row],
        out_shape=[SDS((t, FFN_D_FF), BF16), SDS((t, FFN_D_FF), BF16), SDS((1, FFN_D_FF), F32), SDS((1, FFN_D_FF), F32)],
        compiler_params=_cp(dimension_semantics=("arbitrary", "arbitrary")),
    )(gate, val, d_act)


def _softplus(v):
    e = jnp.exp(-jnp.abs(v))
    small = e * (1.0 - 0.5 * e)
    return jnp.maximum(v, 0.0) + jnp.where(e < 1e-4, small, jnp.log(1.0 + e))


def _dt_fwd(proj, bias_pad, *, tm):
    t = proj.shape[0]
    tm = min(tm, t)

    def body(x_ref, b_ref, g_ref, gt_ref):
        dt = _softplus(x_ref[...] + b_ref[...])
        first8 = _iota((tm, 128), 1) < 8
        for g in range(SSD_N_GROUPS):
            dg = jnp.where(first8, dt if g == 0 else pltpu.roll(dt, 128 - 8 * g, axis=1), 0.0)
            g_ref[g] = dg
            gt_ref[g] = dg.T[0:8, :]

    return pl.pallas_call(
        body, name="dt_fwd", grid=(t // tm,),
        in_specs=[pl.BlockSpec((tm, 128), lambda i: (i, OFF_DT // 128)), pl.BlockSpec((1, 128), lambda i: (0, 0))],
        out_specs=[pl.BlockSpec((SSD_N_GROUPS, tm, 128), lambda i: (0, i, 0)), pl.BlockSpec((SSD_N_GROUPS, 8, tm), lambda i: (0, 0, i))],
        out_shape=[SDS((SSD_N_GROUPS, t, 128), F32), SDS((SSD_N_GROUPS, 8, t), F32)],
        compiler_params=_cp(dimension_semantics=("arbitrary",)),
    )(proj, bias_pad)


def _dt_bwd(d_dtg, proj, bias_pad, d_proj, *, tm):
    t = proj.shape[0]
    tm = min(tm, t)

    def body(g_ref, x_ref, b_ref, _, o_ref, db_ref):
        first8 = _iota((tm, 128), 1) < 8
        d_dt = jnp.where(first8, g_ref[0], 0.0)
        for g in range(1, SSD_N_GROUPS):
            d_dt = d_dt + pltpu.roll(jnp.where(first8, g_ref[g], 0.0), 8 * g, axis=1)
        d_raw = d_dt * _sigmoid(x_ref[...] + b_ref[...])
        o_ref[:, 0:128] = d_raw.astype(BF16)
        o_ref[:, 128:512] = jnp.zeros((tm, 384), BF16)
        _accumulate(db_ref, pl.program_id(0) == 0, jnp.sum(d_raw, axis=0, keepdims=True))

    return pl.pallas_call(
        body, name="dt_bwd", grid=(t // tm,),
        in_specs=[pl.BlockSpec((SSD_N_GROUPS, tm, 128), lambda i: (0, i, 0)), pl.BlockSpec((tm, 128), lambda i: (i, OFF_DT // 128)),
                  pl.BlockSpec((1, 128), lambda i: (0, 0)), pl.BlockSpec(memory_space=pl.ANY)],
        out_specs=[pl.BlockSpec((tm, 512), lambda i: (i, OFF_DT // 512)), pl.BlockSpec((1, 128), lambda i: (0, 0))],
        out_shape=[SDS((t, PROJ_W), BF16), SDS((1, 128), F32)],
        input_output_aliases={3: 0},
        compiler_params=_cp(dimension_semantics=("arbitrary",)),
    )(d_dtg, proj, bias_pad, d_proj)


def _split3(v):
    hi = v.astype(BF16)
    r1 = v - hi.astype(F32)
    mid = r1.astype(BF16)
    return hi, mid, (r1 - mid.astype(F32)).astype(BF16)


def _times01(v, m3):
    return jnp.dot(jnp.concatenate(_split3(v), axis=1), m3, preferred_element_type=F32)


def _01times(m3, v):
    return jnp.dot(m3, jnp.concatenate(_split3(v), axis=0), preferred_element_type=F32)


def _ssd_decay(dt_ref, dtT_ref, al_ref, alT_ref, k):
    dt = dt_ref[0]
    a_row = -jnp.exp(al_ref[0])
    adt_t = dtT_ref[0] * (-jnp.exp(alT_ref[0]))
    return dt, a_row, _01times(k['low3'][...], dt * a_row), _times01(adt_t, k['up3v'][...])


def _ssd_specs(nc, rev):
    ci = (lambda c: nc - 1 - c) if rev else (lambda c: c)
    return [pl.BlockSpec((CHUNK, SSD_CONV_DIM), lambda c: (ci(c), 0)),
            pl.BlockSpec((SSD_N_GROUPS, CHUNK, 128), lambda c: (0, ci(c), 0)),
            pl.BlockSpec((SSD_N_GROUPS, 8, CHUNK), lambda c: (0, 0, ci(c))),
            pl.BlockSpec((SSD_N_GROUPS, 1, 128), lambda c: (0, 0, 0)),
            pl.BlockSpec((SSD_N_GROUPS, 8, 1), lambda c: (0, 0, 0)),
            pl.BlockSpec((1, SSD_D_INNER), lambda c: (0, 0))]


def _ssd_group_views(g, x_ref, dt_ref, dtT_ref, al_ref, alT_ref, d_ref):
    return (x_ref.at[:, g * GROUP_W:(g + 1) * GROUP_W], dt_ref.at[g:g + 1], dtT_ref.at[g:g + 1], al_ref.at[g:g + 1],
            alT_ref.at[g:g + 1], d_ref.at[:, g * 512:(g + 1) * 512])


NT = (((1,), (1,)), ((), ()))
WIDE = 8 * CHUNK
SSD_CONST_NAMES = ('e128', 'e64', 's64', 'mlo', 'mup', 'low3', 'up3', 'up3v')
SSD_CONST_SHAPES = [pltpu.VMEM((3 * CHUNK, WIDE), BF16), pltpu.VMEM((3 * CHUNK, 512), BF16), pltpu.VMEM((512, CHUNK), BF16),
                    pltpu.VMEM((CHUNK, WIDE), F32), pltpu.VMEM((CHUNK, WIDE), F32), pltpu.VMEM((CHUNK, 3 * CHUNK), BF16),
                    pltpu.VMEM((CHUNK, 3 * CHUNK), BF16), pltpu.VMEM((3 * CHUNK, CHUNK), BF16)]


def _ssd_init_consts(k):
    row, col = _iota((3 * CHUNK, WIDE), 0), _iota((3 * CHUNK, WIDE), 1)
    k['e128'][...] = ((col >> 7) == (row & 127)).astype(BF16)
    k['e64'][...] = ((_iota((3 * CHUNK, 512), 1) >> 6) == (_iota((3 * CHUNK, 512), 0) & 127)).astype(BF16)
    k['s64'][...] = ((_iota((512, CHUNK), 0) >> 6) == _iota((512, CHUNK), 1)).astype(BF16)
    row, col = _iota((CHUNK, WIDE), 0), _iota((CHUNK, WIDE), 1)
    k['mlo'][...] = (row >= (col & 127)).astype(F32)
    k['mup'][...] = (row <= (col & 127)).astype(F32)
    row, col = _iota((CHUNK, 3 * CHUNK), 0), _iota((CHUNK, 3 * CHUNK), 1) & 127
    k['low3'][...] = (row >= col).astype(BF16)
    k['up3'][...] = (row <= col).astype(BF16)
    row, col = _iota((3 * CHUNK, CHUNK), 0) & 127, _iota((3 * CHUNK, CHUNK), 1)
    k['up3v'][...] = (row <= col).astype(BF16)


def _ssd_common(x_ref, dt_ref, dtT_ref, al_ref, alT_ref, k):
    dt, a_row, acs, acs_t = _ssd_decay(dt_ref, dtT_ref, al_ref, alT_ref, k)
    ecol = _times01(acs, k['e128'][...])
    rrow = jnp.concatenate([jnp.broadcast_to(acs_t[j:j + 1, :], (CHUNK, CHUNK)) for j in range(8)], axis=1)
    a64 = _times01(acs, k['e64'][...])
    dt64 = _times01(dt, k['e64'][...])
    a_end64 = a64[CHUNK - 1:CHUNK, :]
    xs = x_ref[:, 0:512]
    return dict(dt=dt, a_row=a_row, acs=acs, seg=ecol - rrow, dt64=dt64, e_a=jnp.exp(a64), decay=jnp.exp(a_end64 - a64),
                e_end64=jnp.exp(a_end64), xs=xs, xdt=xs * dt64, bm=x_ref[:, 512:640], cm=x_ref[:, 640:768])


def _pair_blocks(v):
    lo = _iota((CHUNK, 128), 1) < 64
    out = []
    for i in range(4):
        ch = v[:, i * 128:(i + 1) * 128]
        out.append(jnp.concatenate([jnp.where(lo, ch, 0.0), jnp.where(lo, 0.0, ch)], axis=0).astype(BF16))
    return out


def _tile8(m):
    return jnp.concatenate([m] * 8, axis=1)


def _ssd_fwd(xc, dtg, dtg_t, alog, alog_t, d_exp):
    t = xc.shape[0]
    nc = t // CHUNK

    def body(xa_ref, dta_ref, dtTa_ref, ala_ref, alTa_ref, da_ref, ya_ref, hs_ref, h_scr, *consts):
        c = pl.program_id(0)
        k = dict(zip(SSD_CONST_NAMES, consts))

        @pl.when(c == 0)
        def _():
            _ssd_init_consts(k)
            h_scr[...] = jnp.zeros_like(h_scr)

        for g in range(SSD_N_GROUPS):
            x_ref, dt_ref, dtT_ref, al_ref, alT_ref, d_ref = _ssd_group_views(g, xa_ref, dta_ref, dtTa_ref, ala_ref, alTa_ref, da_ref)
            v = _ssd_common(x_ref, dt_ref, dtT_ref, al_ref, alT_ref, k)
            b16, c16 = v['bm'].astype(BF16), v['cm'].astype(BF16)
            cb = lax.dot_general(c16, b16, NT, preferred_element_type=F32)
            m16 = (jnp.exp(jnp.minimum(v['seg'], 0.0)) * k['mlo'][...] * _tile8(cb)).astype(BF16)
            xbd = _pair_blocks(v['xdt'])
            y_diag = jnp.concatenate([jnp.dot(m16[:, i * 256:(i + 1) * 256], xbd[i], preferred_element_type=F32)
                                      for i in range(4)], axis=1)
            ht = h_scr[g]
            y_off = jnp.dot(c16, ht.astype(BF16), preferred_element_type=F32)
            ya_ref[:, g * 512:(g + 1) * 512] = y_diag + v['e_a'] * y_off + d_ref[...] * v['xs']
            st = jnp.dot(v['bm'].T.astype(BF16), (v['xdt'] * v['decay']).astype(BF16), preferred_element_type=F32)
            hs_ref[0, g] = ht
            h_scr[g] = ht * v['e_end64'] + st

    return pl.pallas_call(
        body, name="ssd_fwd", grid=(nc,), in_specs=_ssd_specs(nc, False),
        out_specs=[pl.BlockSpec((CHUNK, SSD_D_INNER), lambda c: (c, 0)),
                   pl.BlockSpec((1, SSD_N_GROUPS, SSD_D_STATE, 512), lambda c: (c, 0, 0, 0))],
        out_shape=[SDS((t, SSD_D_INNER), F32), SDS((nc, SSD_N_GROUPS, SSD_D_STATE, 512), F32)],
        scratch_shapes=[pltpu.VMEM((SSD_N_GROUPS, SSD_D_STATE, 512), F32)] + SSD_CONST_SHAPES,
        compiler_params=_cp(dimension_semantics=("arbitrary",)),
    )(xc, dtg, dtg_t, alog, alog_t, d_exp)


def _ssd_bwd(xc, dtg, dtg_t, alog, alog_t, d_exp, d_y, hs, ride=None):
    t = xc.shape[0]
    nc = t // CHUNK

    r_ops, r_in_specs, r_outs, r_out_specs, r_scratch = _ride_parts(ride)

    def body(xa_ref, dta_ref, dtTa_ref, ala_ref, alTa_ref, da_ref, dya_ref, hs_ref, *rest):
        r_in, rest = rest[:len(r_ops)], rest[len(r_ops):]
        dxa_ref, ddta_ref, dal_ref, dd_ref = rest[:4]
        r_out, rest = rest[4:4 + len(r_outs)], rest[4 + len(r_outs):]
        g_scr, consts, sems = rest[0], rest[1:1 + len(SSD_CONST_NAMES)], rest[1 + len(SSD_CONST_NAMES):]
        c = pl.program_id(0)
        _ride_run(ride, c == 0, c == nc - 1, r_in, r_out, sems)
        k = dict(zip(SSD_CONST_NAMES, consts))

        @pl.when(c == 0)
        def _():
            _ssd_init_consts(k)
            g_scr[...] = jnp.zeros_like(g_scr)

        for g in range(SSD_N_GROUPS):
            views = _ssd_group_views(g, xa_ref, dta_ref, dtTa_ref, ala_ref, alTa_ref, da_ref)
            one_group(c, g, k, *views, dya_ref.at[:, g * 512:(g + 1) * 512], hs_ref, g_scr,
                      dxa_ref.at[:, g * GROUP_W:(g + 1) * GROUP_W], ddta_ref.at[g:g + 1], dal_ref, dd_ref)

    def one_group(c, g, k, x_ref, dt_ref, dtT_ref, al_ref, alT_ref, d_ref, dy_ref, hs_ref, g_scr, dx_ref, ddt_ref,
                  dal_ref, dd_ref):
        s64, mlo, mup = k['s64'], k['mlo'], k['mup']
        v = _ssd_common(x_ref, dt_ref, dtT_ref, al_ref, alT_ref, k)
        dt, a_row, xs, xdt, e_a, decay = v['dt'], v['a_row'], v['xs'], v['xdt'], v['e_a'], v['decay']
        row, col = _iota((CHUNK, CHUNK), 0), _iota((CHUNK, CHUNK), 1)
        b16, c16 = v['bm'].astype(BF16), v['cm'].astype(BF16)
        ct16 = v['cm'].T.astype(BF16)
        cb = lax.dot_general(c16, b16, NT, preferred_element_type=F32)
        cbt = lax.dot_general(b16, c16, NT, preferred_element_type=F32)
        lmat = jnp.exp(jnp.minimum(v['seg'], 0.0)) * mlo[...]
        lmat_t = jnp.exp(jnp.minimum(-v['seg'], 0.0)) * mup[...]
        mmat, mmat_t = lmat * _tile8(cb), lmat_t * _tile8(cbt)
        mt16 = mmat_t.astype(BF16)
        dy = dy_ref[...]
        dye, xdec = dy * e_a, xdt * decay
        dy16, dye16, xdec16 = dy.astype(BF16), dye.astype(BF16), xdec.astype(BF16)
        xdt16 = xdt.astype(BF16)
        ht, gt = hs_ref[0, g], g_scr[g]
        ht16, gt16 = ht.astype(BF16), gt.astype(BF16)
        xbd, dybd = _pair_blocks(xdt), _pair_blocks(dy)
        d_m, d_mt, d_x = [], [], []
        for i in range(4):
            csl = slice(i * 128, (i + 1) * 128)
            d_m.append(lax.dot_general(dy16[:, csl], xbd[i], NT, preferred_element_type=F32))
            d_mt.append(lax.dot_general(xdt16[:, csl], dybd[i], NT, preferred_element_type=F32))
            d_x.append(jnp.dot(mt16[:, i * 256:(i + 1) * 256], dybd[i], preferred_element_type=F32))
        d_m, d_mt, d_x = jnp.concatenate(d_m, axis=1), jnp.concatenate(d_mt, axis=1), jnp.concatenate(d_x, axis=1)

        def head_sum(m):
            acc = m[:, 0:CHUNK]
            for j in range(1, 8):
                acc = acc + m[:, j * CHUNK:(j + 1) * CHUNK]
            return acc

        def seg64(p):
            return jnp.dot(p.astype(BF16), s64[...], preferred_element_type=F32)

        d_cb16 = head_sum(d_m * lmat).astype(BF16)
        d_cbt16 = head_sum(d_mt * lmat_t).astype(BF16)
        dseg = d_m * mmat - d_mt * mmat_t
        da_seg = jnp.zeros((CHUNK, CHUNK), F32)
        for j in range(8):
            da_seg = jnp.where(col == j, jnp.sum(dseg[:, j * CHUNK:(j + 1) * CHUNK], axis=1, keepdims=True), da_seg)
        ch = jnp.dot(c16, ht16, preferred_element_type=F32)
        bg = jnp.dot(b16, gt16, preferred_element_type=F32)
        d_x = d_x + decay * bg
        d_decay = seg64(xdec * bg)
        e_end = jnp.exp(v['acs'][CHUNK - 1:CHUNK, :])
        d_end = e_end * jnp.sum(seg64(gt * ht), axis=0, keepdims=True) + jnp.sum(d_decay, axis=0, keepdims=True)
        d_a = seg64(dye * ch) - d_decay + da_seg + jnp.where(row == CHUNK - 1, d_end, 0.0)
        dx_ref[:, 0:512] = d_x * v['dt64'] + d_ref[...] * dy
        dx_ref[:, 640:768] = (lax.dot_general(dye16, ht16, NT, preferred_element_type=F32)
                              + jnp.dot(d_cb16, b16, preferred_element_type=F32))
        dx_ref[:, 512:640] = (lax.dot_general(xdec16, gt16, NT, preferred_element_type=F32)
                              + jnp.dot(d_cbt16, c16, preferred_element_type=F32))
        g_scr[g] = gt * v['e_end64'] + jnp.dot(ct16, dye16, preferred_element_type=F32)
        d_adt = _01times(k['up3'][...], d_a)
        ddt_ref[0] = d_adt * a_row + seg64(d_x * xs)
        d_alog = jnp.sum(d_adt * dt, axis=0, keepdims=True) * a_row
        dd_row = jnp.sum(seg64(dy * xs), axis=0, keepdims=True)
        first = c == 0

        @pl.when(first)
        def _():
            dal_ref[g] = d_alog
            dd_ref[g] = dd_row

        @pl.when(jnp.logical_not(first))
        def _():
            dal_ref[g] += d_alog
            dd_ref[g] += dd_row

    rc = lambda c: nc - 1 - c
    whole = pl.BlockSpec((SSD_N_GROUPS, 1, 128), lambda c: (0, 0, 0))
    return pl.pallas_call(
        body, name="ssd_bwd", grid=(nc,),
        in_specs=_ssd_specs(nc, True) + [pl.BlockSpec((CHUNK, SSD_D_INNER), lambda c: (rc(c), 0)),
                                        pl.BlockSpec((1, SSD_N_GROUPS, SSD_D_STATE, 512), lambda c: (rc(c), 0, 0, 0))] + r_in_specs,
        out_specs=[pl.BlockSpec((CHUNK, SSD_CONV_DIM), lambda c: (rc(c), 0)),
                   pl.BlockSpec((SSD_N_GROUPS, CHUNK, 128), lambda c: (0, rc(c), 0)), whole, whole] + r_out_specs,
        out_shape=[SDS((t, SSD_CONV_DIM), F32), SDS((SSD_N_GROUPS, t, 128), F32),
                   SDS((SSD_N_GROUPS, 1, 128), F32), SDS((SSD_N_GROUPS, 1, 128), F32)] + r_outs,
        scratch_shapes=[pltpu.VMEM((SSD_N_GROUPS, SSD_D_STATE, 512), F32)] + SSD_CONST_SHAPES + r_scratch,
        compiler_params=_cp(dimension_semantics=("arbitrary",)),
    )(xc, dtg, dtg_t, alog, alog_t, d_exp, d_y, hs, *r_ops)


def _gated_norm_fwd(y, proj, w, *, tm):
    t = y.shape[0]
    tm = min(tm, t)

    def body(y_ref, z_ref, w_ref, o_ref):
        gv = y_ref[...] * _silu_parts(z_ref[...])[0]
        r = lax.rsqrt(jnp.mean(gv * gv, axis=-1, keepdims=True) + NORM_EPS)
        o_ref[...] = (gv * r * w_ref[...]).astype(BF16)

    tile = pl.BlockSpec((tm, 512), lambda i, g: (i, g))
    return pl.pallas_call(
        body, name="gated_norm_fwd", grid=(t // tm, SSD_N_GROUPS),
        in_specs=[tile, pl.BlockSpec((tm, 512), lambda i, g: (i, OFF_Z // 512 + g)),
                  pl.BlockSpec((1, 512), lambda i, g: (0, g))], out_specs=tile,
        out_shape=SDS((t, SSD_D_INNER), BF16),
        compiler_params=_cp(dimension_semantics=("arbitrary", "arbitrary")),
    )(y, proj, w)


def _rope(ch, cos_t, sin_t):
    first = (_iota(ch.shape, 1) & 32) == 0
    partner = jnp.where(first, pltpu.roll(ch, 96, axis=1), pltpu.roll(ch, 32, axis=1))
    return ch * cos_t + partner * sin_t


def _rope_qkv(proj, cos_t, sin_t, *, tm):
    t = proj.shape[0]
    tm = min(tm, t)

    def body(q_ref, k_ref, v_ref, c_ref, s_ref, qr_ref, kp_ref, vp_ref, kt_ref, vt_ref):
        cv, sv = c_ref[...], s_ref[...]
        lo = _iota((tm, 128), 1) < 64
        for m in range(8):
            sl = slice(m * 128, (m + 1) * 128)
            qr_ref[:, sl] = (_rope(q_ref[:, sl], cv, sv) * 0.125).astype(BF16)
        for m2 in range(2):
            sl = slice(m2 * 128, (m2 + 1) * 128)
            for src, dst, dst_t in ((_rope(k_ref[:, sl], cv, sv), kp_ref, kt_ref), (v_ref[:, sl], vp_ref, vt_ref)):
                sw = pltpu.roll(src, 64, axis=1)
                padded = (jnp.where(lo, src, 0.0), jnp.where(lo, 0.0, sw), jnp.where(lo, sw, 0.0), jnp.where(lo, 0.0, src))
                for i, pad in enumerate(padded):
                    rows = slice((4 * m2 + i) * 128, (4 * m2 + i + 1) * 128)
                    dst[:, rows] = pad.astype(BF16)
                    dst_t[rows, :] = pad.T.astype(BF16)

    return pl.pallas_call(
        body, name="rope_qkv", grid=(t // tm,),
        in_specs=[pl.BlockSpec((tm, 1024), lambda i: (i, OFF_Q // 1024)), pl.BlockSpec((tm, 256), lambda i: (i, OFF_K // 256)),
                  pl.BlockSpec((tm, 256), lambda i: (i, OFF_V // 256)), pl.BlockSpec((tm, 128), lambda i: (i, 0)),
                  pl.BlockSpec((tm, 128), lambda i: (i, 0))],
        out_specs=[pl.BlockSpec((tm, 1024), lambda i: (i, 0))] * 3 + [pl.BlockSpec((1024, tm), lambda i: (0, i))] * 2,
        out_shape=[SDS((t, 1024), BF16)] * 3 + [SDS((1024, t), BF16)] * 2,
        compiler_params=_cp(dimension_semantics=("arbitrary",)),
    )(proj, proj, proj, cos_t, sin_t)


def _attn_valid(n):
    kj, qi = _iota((2 * CHUNK, CHUNK), 0), _iota((2 * CHUNK, CHUNK), 1)
    return (kj > qi) & (kj <= qi + CHUNK) & ((n > 0) | (kj >= CHUNK))


def _attn_fwd(qr, kp, vt, sinks):
    t = qr.shape[0]
    nb = t // CHUNK

    def body(q_ref, kc_ref, kprev_ref, vc_ref, vprev_ref, sk_ref, o_ref, lse_ref):
        n = pl.program_id(0)
        valid = _attn_valid(n)
        head_row = _iota((16, CHUNK), 0)
        lse_all = jnp.zeros((16, CHUNK), F32)
        for m in range(8):
            g = m // 2
            qch = q_ref[:, m * 128:(m + 1) * 128]
            o_t = jnp.zeros((128, CHUNK), F32)
            for e in range(2):
                h = 2 * m + e
                sl = slice((2 * g + e) * 128, (2 * g + e + 1) * 128)
                kk = jnp.concatenate([kprev_ref[:, sl], kc_ref[:, sl]], axis=0)
                vv_t = jnp.concatenate([vprev_ref[sl, :], vc_ref[sl, :]], axis=1)
                s = jnp.where(valid, lax.dot_general(kk, qch, NT, preferred_element_type=F32), NEG)
                sink = sk_ref[0:1, h:h + 1]
                mx = jnp.maximum(jnp.max(s, axis=0, keepdims=True), sink)
                p = jnp.exp(s - mx)
                den = jnp.sum(p, axis=0, keepdims=True) + jnp.exp(sink - mx)
                o_t = o_t + jnp.dot(vv_t, p.astype(BF16), preferred_element_type=F32) * (1.0 / den)
                lse_all = jnp.where(head_row == h, mx + jnp.log(den), lse_all)
            o_ref[:, m * 128:(m + 1) * 128] = o_t.T.astype(BF16)
        lse_ref[0] = lse_all

    cur = pl.BlockSpec((CHUNK, 1024), lambda n: (n, 0))
    prev = pl.BlockSpec((CHUNK, 1024), lambda n: (jnp.maximum(n - 1, 0), 0))
    cur_t = pl.BlockSpec((1024, CHUNK), lambda n: (0, n))
    prev_t = pl.BlockSpec((1024, CHUNK), lambda n: (0, jnp.maximum(n - 1, 0)))
    return pl.pallas_call(
        body, name="attn_fwd", grid=(nb,),
        in_specs=[cur, cur, prev, cur_t, prev_t, pl.BlockSpec((1, 128), lambda n: (0, 0))],
        out_specs=[cur, pl.BlockSpec((1, 16, CHUNK), lambda n: (n, 0, 0))],
        out_shape=[SDS((t, 1024), BF16), SDS((nb, 16, CHUNK), F32)],
        compiler_params=_cp(dimension_semantics=("arbitrary",)),
    )(qr, kp, kp, vt, vt, sinks)


def _attn_bwd(qr, kp, vp, kt, d_o, o, lse, sinks, cos_t, sin_t, d_proj, ride=None):
    t = qr.shape[0]
    nb = t // CHUNK

    r_ops, r_in_specs, r_outs, r_out_specs, r_scratch = _ride_parts(ride)

    def body(q_ref, kc_ref, kprev_ref, vc_ref, vprev_ref, ktc_ref, ktprev_ref, do_ref, o_ref, lse_ref, sk_ref,
             c_ref, s_ref, cp_ref, sp_ref, _, *rest):
        r_in, rest = rest[:len(r_ops)], rest[len(r_ops):]
        dqkv_ref, dsk_ref = rest[:2]
        r_out, rest = rest[2:2 + len(r_outs)], rest[2 + len(r_outs):]
        acc_k, acc_v, dq_scr = rest[:3]
        n = pl.program_id(0)
        _ride_run(ride, n == 0, n == nb, r_in, r_out, rest[3:])
        lane = _iota((CHUNK, 128), 1)
        lo = lane < 64
        lane1 = _iota((1, 128), 1)

        @pl.when(n == 0)
        def _():
            acc_k[...] = jnp.zeros_like(acc_k)
            acc_v[...] = jnp.zeros_like(acc_v)
            dsk_ref[...] = jnp.zeros((1, 128), F32)

        @pl.when(n > 0)
        def _():
            dqkv_ref[:, 0:1024] = dq_scr[...]
            for r in range(8):
                acc_k[r, 0:CHUNK] = acc_k[r, CHUNK:2 * CHUNK]
                acc_v[r, 0:CHUNK] = acc_v[r, CHUNK:2 * CHUNK]
                acc_k[r, CHUNK:2 * CHUNK] = jnp.zeros((CHUNK, 128), F32)
                acc_v[r, CHUNK:2 * CHUNK] = jnp.zeros((CHUNK, 128), F32)

        @pl.when(n < nb)
        def _():
            valid = _attn_valid(n)
            lse_all = lse_ref[0]
            dsk = jnp.zeros((1, 128), F32)
            for m in range(8):
                g = m // 2
                csl = slice(m * 128, (m + 1) * 128)
                qch = q_ref[:, csl]
                doch = do_ref[:, csl]
                prod_t = (doch.astype(F32) * o_ref[:, csl].astype(F32)).T
                dq_t = jnp.zeros((128, CHUNK), F32)
                for e in range(2):
                    h = 2 * m + e
                    sl = slice((2 * g + e) * 128, (2 * g + e + 1) * 128)
                    kk = jnp.concatenate([kprev_ref[:, sl], kc_ref[:, sl]], axis=0)
                    vv = jnp.concatenate([vprev_ref[:, sl], vc_ref[:, sl]], axis=0)
                    kk_t = jnp.concatenate([ktprev_ref[sl, :], ktc_ref[sl, :]], axis=1)
                    lse_h = lse_all[h:h + 1, :]
                    s = lax.dot_general(kk, qch, NT, preferred_element_type=F32)
                    p = jnp.exp(jnp.where(valid, s, NEG) - lse_h)
                    delta = jnp.sum(prod_t[64 * e:64 * (e + 1)], axis=0, keepdims=True)
                    d_p = lax.dot_general(vv, doch, NT, preferred_element_type=F32)
                    d_s16 = (p * (d_p - delta)).astype(BF16)
                    dq_t = dq_t + jnp.dot(kk_t, d_s16, preferred_element_type=F32)
                    acc_k[2 * g + e] += jnp.dot(d_s16, qch, preferred_element_type=F32)
                    acc_v[2 * g + e] += jnp.dot(p.astype(BF16), doch, preferred_element_type=F32)
                    p_sink = jnp.exp(sk_ref[0:1, h:h + 1] - lse_h)
                    dsk = jnp.where(lane1 == h, -jnp.sum(p_sink * delta), dsk)
                dq_scr[:, csl] = (_rope(dq_t.T, c_ref[...], -s_ref[...]) * 0.125).astype(BF16)
            dsk_ref[...] += dsk

        @pl.when(n > 0)
        def _():
            for m2 in range(2):
                halves = []
                for g in (2 * m2, 2 * m2 + 1):
                    for acc in (acc_k, acc_v):
                        comb = jnp.where(lo, acc[2 * g, 0:CHUNK], acc[2 * g + 1, 0:CHUNK])
                        halves.append(comb + pltpu.roll(comb, 64, axis=1))
                d_kr = jnp.where(lo, halves[0], halves[2])
                d_v = jnp.where(lo, halves[1], halves[3])
                dqkv_ref[:, OFF_K + m2 * 128:OFF_K + (m2 + 1) * 128] = _rope(d_kr, cp_ref[...], -sp_ref[...]).astype(BF16)
                dqkv_ref[:, OFF_V + m2 * 128:OFF_V + (m2 + 1) * 128] = d_v.astype(BF16)

    qn = lambda n: jnp.minimum(n, nb - 1)
    pn = lambda n: jnp.maximum(jnp.minimum(n, nb) - 1, 0)
    cur = pl.BlockSpec((CHUNK, 1024), lambda n: (qn(n), 0))
    prev = pl.BlockSpec((CHUNK, 1024), lambda n: (pn(n), 0))
    cur128 = pl.BlockSpec((CHUNK, 128), lambda n: (qn(n), 0))
    prev128 = pl.BlockSpec((CHUNK, 128), lambda n: (pn(n), 0))
    cur_t = pl.BlockSpec((1024, CHUNK), lambda n: (0, qn(n)))
    prev_t = pl.BlockSpec((1024, CHUNK), lambda n: (0, pn(n)))
    one = pl.BlockSpec((1, 128), lambda n: (0, 0))
    return pl.pallas_call(
        body, name="attn_bwd", grid=(nb + 1,),
        in_specs=[cur, cur, prev, cur, prev, cur_t, prev_t, cur, cur, pl.BlockSpec((1, 16, CHUNK), lambda n: (qn(n), 0, 0)),
                  one, cur128, cur128, prev128, prev128, pl.BlockSpec(memory_space=pl.ANY)] + r_in_specs,
        out_specs=[pl.BlockSpec((CHUNK, 1536), lambda n: (pn(n), 0)), one] + r_out_specs,
        out_shape=[SDS((t, PROJ_W), BF16), SDS((1, 128), F32)] + r_outs,
        scratch_shapes=[pltpu.VMEM((8, 2 * CHUNK, 128), F32), pltpu.VMEM((8, 2 * CHUNK, 128), F32),
                        pltpu.VMEM((CHUNK, 1024), BF16)] + r_scratch,
        input_output_aliases={15: 0},
        compiler_params=_cp(dimension_semantics=("arbitrary",)),
    )(qr, kp, kp, vp, vp, kt, kt, d_o, o, lse, sinks, cos_t, sin_t, cos_t, sin_t, d_proj, *r_ops)


def _adamw(name, w, g, m, v, *, tr):
    rows, cols = w.shape
    tr = min(tr, rows)
    assert rows % tr == 0

    def body(w_ref, g_ref, m_ref, v_ref, d_ref, nm_ref, nv_ref):
        gv = g_ref[...]
        nm = ADAM_B1 * m_ref[...] + (1.0 - ADAM_B1) * gv
        nv = ADAM_B2 * v_ref[...] + (1.0 - ADAM_B2) * (gv * gv)
        m_hat = nm / (1.0 - ADAM_B1 ** ADAM_STEP)
        v_hat = nv / (1.0 - ADAM_B2 ** ADAM_STEP)
        d_ref[...] = -ADAM_LR * (m_hat / (jnp.sqrt(v_hat) + ADAM_EPS) + ADAM_WD * w_ref[...])
        nm_ref[...] = nm
        nv_ref[...] = nv

    tile = pl.BlockSpec((tr, cols), lambda i: (i, 0))
    return pl.pallas_call(
        body, name=name, grid=(rows // tr,), in_specs=[tile] * 4, out_specs=[tile] * 3,
        out_shape=[SDS((rows, cols), F32)] * 3, compiler_params=_cp(dimension_semantics=("arbitrary",)),
    )(w, g, m, v)


def _local_step(x, cos_t, sin_t, tgt, wb, ps, late=None, rides=None):
    t = x.shape[0]
    tm = min(512, t)
    tmw = min(1024, t)
    ij = lambda i, j, k: (i, j)
    i0 = lambda i, j, k: (i, 0)
    c0 = lambda i, j, k: (0, 0)
    cj = lambda i, j, k: (0, j)
    rides = rides or (lambda group, grads: None)
    rode = {}

    tkt = min(2048, t)
    proj, u, *arrived = _norm_mm("in_proj", x, ps['norm_mix_pre_w'], wb['cat'], tm=tmw, tn=1024,
                                 ride=late[0] if late else None)
    if late:
        more_wb, more_ps = late[1](arrived)
        wb, ps = {**wb, **more_wb}, {**ps, **more_ps}
    xc, xc_pre = _conv_silu_fwd(proj, ps['ssd_conv_w'], ps['ssd_conv_b'], tm=tm)
    bias_pad = jnp.pad(ps['ssd_dt_bias'], ((0, 0), (0, 96)))
    dtg, dtg_t = _dt_fwd(proj, bias_pad, tm=tmw)
    alog = jnp.pad(ps['ssd_a_log'].reshape(SSD_N_GROUPS, 1, 8), ((0, 0), (0, 0), (0, 120)))
    alog_t = ps['ssd_a_log'].reshape(SSD_N_GROUPS, 8, 1)
    d_exp = jnp.repeat(ps['ssd_d'], SSD_HEAD_DIM, axis=1)
    y, hs = _ssd_fwd(xc, dtg, dtg_t, alog, alog_t, d_exp)
    gn = _gated_norm_fwd(y, proj, ps['ssd_norm_w'], tm=tmw)
    qr, kp, vp, kt, vt = _rope_qkv(proj, cos_t, sin_t, tm=tm)
    sinks = jnp.pad(ps['attn_sinks'], ((0, 0), (0, 112)))
    ao, lse = _attn_fwd(qr, kp, vt, sinks)
    y_attn = _mm_plain("attn_out", ao, wb['ao'], tm=tmw, tn=512, tk=1024)

    def merge_ep(acc, i, j, ins, outs):
        gs, ga, ya = ins
        outs[0][...] = (_sigmoid(gs[...]) * acc + _sigmoid(ga[...]) * ya[...]).astype(BF16)
        outs[1][...] = acc

    merged, y_ssd = _mm_call(
        "ssd_out_merge", gn, wb['so'], tm=tmw, tn=512, tk=2048, epilogue=merge_ep,
        extra_in=[(proj, (tmw, 512), lambda i, j, k: (i, OFF_GS // 512 + j)),
                  (proj, (tmw, 512), lambda i, j, k: (i, OFF_GA // 512 + j)), (y_attn, (tmw, 512), ij)],
        outs=[((t, D_MODEL), BF16, (tmw, 512), ij), ((t, D_MODEL), F32, (tmw, 512), ij)])

    def mix_ep(acc, i, j, ins, outs):
        xv, wn = ins
        r = lax.rsqrt(jnp.mean(acc * acc, axis=-1, keepdims=True) + NORM_EPS)
        outs[0][...] = xv[...] + acc * r * wn[...]
        outs[1][...] = acc

    x1, mmix = _mm_call(
        "mix_out", merged, wb['mix'], tm=tm, tn=D_MODEL, tk=1024, epilogue=mix_ep,
        extra_in=[(x, (tm, D_MODEL), i0), (ps['norm_mix_post_w'], (1, D_MODEL), c0)],
        outs=[((t, D_MODEL), F32, (tm, D_MODEL), i0), ((t, D_MODEL), F32, (tm, D_MODEL), i0)])

    up_raw, h = _norm_mm("ffn_up", x1, ps['norm_ffn_pre_w'], wb['up'], tm=tmw, tn=1408)
    act, ffn_gate, ffn_val = _ffn_act_fwd(up_raw, ps['ffn_conv_w'], ps['ffn_conv_b'], tm=tm)

    def loss_ep(acc, i, j, ins, outs):
        x1v, tg, wn = ins
        d_ff_ref, dout_ref, loss_ref, dw_ref = outs
        wv = wn[...]
        r = lax.rsqrt(jnp.mean(acc * acc, axis=-1, keepdims=True) + NORM_EPS)
        err = x1v[...] + acc * r * wv - tg[...]
        dout = err * (1.0 / D_MODEL)
        dout_ref[...] = dout
        d_ff, dw = _rms_bwd(acc, wv, dout)
        d_ff_ref[...] = d_ff.astype(BF16)
        _accumulate(dw_ref, i == 0, dw)
        _accumulate(loss_ref, i == 0, jnp.sum(err * err, keepdims=True) * (0.5 / D_MODEL))

    d_ff, dout, loss, g_norm_ffn_post = _mm_call(
        "ffn_down_loss", act, wb['dn'], tm=tm, tn=D_MODEL, tk=FFN_D_FF, epilogue=loss_ep,
        extra_in=[(x1, (tm, D_MODEL), i0), (tgt, (tm, D_MODEL), i0), (ps['norm_ffn_post_w'], (1, D_MODEL), c0)],
        outs=[((t, D_MODEL), BF16, (tm, D_MODEL), i0), ((t, D_MODEL), F32, (tm, D_MODEL), i0),
              ((1, 1), F32, (1, 1), c0), ((1, D_MODEL), F32, (1, D_MODEL), c0)])

    d_act = _mm_plain("d_act", d_ff, wb['dn_t'], tm=tmw, tn=1408, tk=1024, out_dtype=BF16)
    g_w_down = _mm_plain("g_w_down", act, d_ff, tm=1408, tn=1024, tk=tkt, trans_a=True, out_dtype=BF16)
    d_gate, d_val, db_g, db_v = _ffn_act_bwd(ffn_gate, ffn_val, d_act, tm=tm)
    d_up_raw, gcw_g = _conv_bwd2("ffn_conv_bwd2_gate", d_gate, up_raw, 0, ps['ffn_conv_w'][:, :FFN_D_FF], tm=tm,
                                 tc=1408, out_cols=2 * FFN_D_FF, out_col0=0)
    d_up_raw, gcw_v = _conv_bwd2("ffn_conv_bwd2_val", d_val, up_raw, FFN_D_FF, ps['ffn_conv_w'][:, FFN_D_FF:], tm=tm,
                                 tc=1408, out_cols=2 * FFN_D_FF, out_col0=FFN_D_FF, fill=d_up_raw)
    g_ffn_conv_w = jnp.concatenate([gcw_g, gcw_v], axis=1)

    def dx1_ep(acc, i, j, ins, outs):
        x1v, wpre, dout_v, mmv, wpost = ins
        d_x1_ref, d_mm_ref, dwpre_ref, dwpost_ref = outs
        d_n, dw_pre = _rms_bwd(x1v[...], wpre[...], acc)
        d_x1 = dout_v[...] + d_n
        d_x1_ref[...] = d_x1
        d_mm, dw_post = _rms_bwd(mmv[...], wpost[...], d_x1)
        d_mm_ref[...] = d_mm.astype(BF16)
        _accumulate(dwpre_ref, i == 0, dw_pre)
        _accumulate(dwpost_ref, i == 0, dw_post)

    d_x1, d_mm, g_norm_ffn_pre, g_norm_mix_post = _mm_call(
        "d_h", d_up_raw, wb['up_t'], tm=tm, tn=D_MODEL, tk=FFN_D_FF, epilogue=dx1_ep,
        extra_in=[(x1, (tm, D_MODEL), i0), (ps['norm_ffn_pre_w'], (1, D_MODEL), c0), (dout, (tm, D_MODEL), i0),
                  (mmix, (tm, D_MODEL), i0), (ps['norm_mix_post_w'], (1, D_MODEL), c0)],
        outs=[((t, D_MODEL), F32, (tm, D_MODEL), i0), ((t, D_MODEL), BF16, (tm, D_MODEL), i0),
              ((1, D_MODEL), F32, (1, D_MODEL), c0), ((1, D_MODEL), F32, (1, D_MODEL), c0)])
    g_w_up_t = _mm_plain("g_w_up", d_up_raw, h, tm=1408, tn=1024, tk=tkt, trans_a=True, out_dtype=BF16)
    ride_ffn = rides('ffn', {'ffn_w_up': g_w_up_t, 'ffn_w_down': g_w_down})

    def dmerge_ep(acc, i, j, ins, outs):
        gs, ga, ys, ya = ins
        sg_s, sg_a = _sigmoid(gs[...]), _sigmoid(ga[...])
        outs[0][...] = (acc * sg_s).astype(BF16)
        outs[1][...] = (acc * sg_a).astype(BF16)
        outs[2][:, 0:D_MODEL] = (acc * ys[...] * sg_s * (1.0 - sg_s)).astype(BF16)
        outs[2][:, D_MODEL:2 * D_MODEL] = (acc * ya[...] * sg_a * (1.0 - sg_a)).astype(BF16)

    d_yssd, d_yattn, d_proj = _mm_call(
        "d_merged", d_mm, wb['mix_t'], tm=tm, tn=D_MODEL, tk=1024, epilogue=dmerge_ep,
        extra_in=[(proj, (tm, D_MODEL), lambda i, j, k: (i, OFF_GS // D_MODEL)),
                  (proj, (tm, D_MODEL), lambda i, j, k: (i, OFF_GA // D_MODEL)), (y_ssd, (tm, D_MODEL), i0), (y_attn, (tm, D_MODEL), i0)],
        outs=[((t, D_MODEL), BF16, (tm, D_MODEL), i0), ((t, D_MODEL), BF16, (tm, D_MODEL), i0),
              ((t, PROJ_W), BF16, (tm, 2 * D_MODEL), lambda i, j, k: (i, OFF_GS // (2 * D_MODEL)))])
    g_w_mix = _mm_plain("g_w_mix", merged, d_mm, tm=1024, tn=1024, tk=tkt, trans_a=True, out_dtype=BF16)

    def dgn_ep(acc, i, j, ins, outs):
        yv, zv, wn = ins
        d_y_ref, d_z_ref, dw_ref = outs
        zz = zv[...]
        sz = _sigmoid(zz)
        silu = zz * sz
        gv = yv[...] * silu
        r = lax.rsqrt(jnp.mean(gv * gv, axis=-1, keepdims=True) + NORM_EPS)
        gh = gv * r
        dgh = acc * wn[...]
        dg = r * (dgh - gh * jnp.mean(dgh * gh, axis=-1, keepdims=True))
        d_y_ref[...] = dg * silu
        d_z_ref[...] = (dg * yv[...] * (sz * (1.0 + zz * (1.0 - sz)))).astype(BF16)
        dw = jnp.sum(acc * gh, axis=0, keepdims=True)

        @pl.when(i == 0)
        def _():
            dw_ref[j] = dw

        @pl.when(i > 0)
        def _():
            dw_ref[j] += dw

    d_y, d_proj, g_ssd_norm = _mm_call(
        "d_gn", d_yssd, wb['so_t'], tm=tmw, tn=512, tk=1024, epilogue=dgn_ep, fill=(d_proj, 1),
        extra_in=[(y, (tmw, 512), ij), (proj, (tmw, 512), lambda i, j, k: (i, OFF_Z // 512 + j)), (ps['ssd_norm_w'], (1, 512), cj)],
        outs=[((t, SSD_D_INNER), F32, (tmw, 512), ij), ((t, PROJ_W), BF16, (tmw, 512), lambda i, j, k: (i, OFF_Z // 512 + j)),
              ((SSD_N_GROUPS, 1, 512), F32, (SSD_N_GROUPS, 1, 512), lambda i, j, k: (0, 0, 0))])
    g_ssd_norm = g_ssd_norm.reshape(1, SSD_D_INNER)
    g_w_so = _mm_plain("g_w_so", gn, d_yssd, tm=1024, tn=1024, tk=tkt, trans_a=True, out_dtype=BF16)
    d_xc, d_dtg, d_alog, d_dd, *rode['ffn'] = _ssd_bwd(xc, dtg, dtg_t, alog, alog_t, d_exp, d_y, hs, ride=ride_ffn)
    d_pre, g_ssd_conv_b = _conv_silu_bwd1(d_xc, xc_pre, tm=tm)
    d_proj, g_ssd_conv_w = _conv_bwd2("ssd_conv_bwd2", d_pre, proj, OFF_XBC, ps['ssd_conv_w'], tm=tm, tc=1536,
                                      out_cols=PROJ_W, out_col0=OFF_XBC, fill=d_proj)
    d_proj, g_dt_bias = _dt_bwd(d_dtg, proj, bias_pad, d_proj, tm=tmw)

    d_ao = _mm_plain("d_ao", d_yattn, wb['ao_t'], tm=tmw, tn=512, tk=1024, out_dtype=BF16)
    g_w_ao = _mm_plain("g_w_ao", ao, d_yattn, tm=1024, tn=1024, tk=tkt, trans_a=True, out_dtype=BF16)
    ride_mix = rides('mix', {'ssd_w_out': g_w_so, 'attn_w_out': g_w_ao, 'w_mix_out': g_w_mix})
    d_proj, g_sinks, *rode['mix'] = _attn_bwd(qr, kp, vp, kt, d_ao, ao, lse, sinks, cos_t, sin_t, d_proj, ride=ride_mix)

    def dx_ep(acc, i, j, ins, outs):
        xv, wn, dx1v = ins
        d_n, dw = _rms_bwd(xv[...], wn[...], acc)
        outs[0][...] = dx1v[...] + d_n
        _accumulate(outs[1], i == 0, dw)

    g_cat_t = _mm_plain("g_w_in", d_proj, u, tm=1024, tn=1024, tk=tkt, trans_a=True, out_dtype=BF16)
    grad_x, g_norm_mix_pre, *rode['w_in'] = _mm_call(
        "d_u", d_proj, wb['cat_t'], tm=tm, tn=D_MODEL, tk=2304, epilogue=dx_ep, ride=rides('w_in', {'w_in': g_cat_t}),
        extra_in=[(x, (tm, D_MODEL), i0), (ps['norm_mix_pre_w'], (1, D_MODEL), c0), (d_x1, (tm, D_MODEL), i0)],
        outs=[((t, D_MODEL), F32, (tm, D_MODEL), i0), ((1, D_MODEL), F32, (1, D_MODEL), c0)])

    grads = {
        'norm_mix_pre_w': g_norm_mix_pre, 'w_in': g_cat_t, 'ssd_conv_w': g_ssd_conv_w, 'ssd_conv_b': g_ssd_conv_b,
        'ssd_dt_bias': g_dt_bias[:, :SSD_N_HEADS], 'ssd_a_log': d_alog[:, 0, :8].reshape(1, SSD_N_HEADS),
        'ssd_d': d_dd[:, 0, :8].reshape(1, SSD_N_HEADS), 'ssd_norm_w': g_ssd_norm, 'ssd_w_out': g_w_so,
        'attn_sinks': g_sinks[:, :ATTN_N_HEADS], 'attn_w_out': g_w_ao, 'w_mix_out': g_w_mix,
        'norm_mix_post_w': g_norm_mix_post, 'norm_ffn_pre_w': g_norm_ffn_pre, 'ffn_w_up': g_w_up_t,
        'ffn_conv_w': g_ffn_conv_w, 'ffn_conv_b': jnp.concatenate([db_g, db_v], axis=1), 'ffn_w_down': g_w_down,
        'norm_ffn_post_w': g_norm_ffn_post,
    }
    return loss, grad_x, grads, rode


def _group_channels(a):
    parts = []
    for g in range(SSD_N_GROUPS):
        parts += [a[..., 512 * g:512 * (g + 1)], a[..., 2048 + 128 * g:2048 + 128 * (g + 1)],
                  a[..., 2560 + 128 * g:2560 + 128 * (g + 1)]]
    return jnp.concatenate(parts, axis=-1)


def _ungroup_channels(a):
    xs = [a[..., GROUP_W * g:GROUP_W * g + 512] for g in range(SSD_N_GROUPS)]
    bs = [a[..., GROUP_W * g + 512:GROUP_W * g + 640] for g in range(SSD_N_GROUPS)]
    cs = [a[..., GROUP_W * g + 640:GROUP_W * (g + 1)] for g in range(SSD_N_GROUPS)]
    return jnp.concatenate(xs + bs + cs, axis=-1)


def _proj_rows(a_t, lo, hi):
    out = []
    for start, length, dst in sorted(PROJ_SEGS):
        s, e = max(lo, start), min(hi, start + length)
        if s < e:
            out.append(a_t[dst + s - start:dst + e - start])
    return out


def _to_proj_layout(w_in_t):
    pieces, pos = [], 0
    for start, length, dst in sorted(PROJ_SEGS, key=lambda s: s[2]):
        if dst > pos:
            pieces.append(jnp.zeros((dst - pos, w_in_t.shape[1]), w_in_t.dtype))
        pieces.append(w_in_t[start:start + length])
        pos = dst + length
    if pos < PROJ_W:
        pieces.append(jnp.zeros((PROJ_W - pos, w_in_t.shape[1]), w_in_t.dtype))
    return jnp.concatenate(pieces, axis=0)


def _rope_tables(positions):
    half = 32
    inv_freq = ROPE_THETA ** (-jnp.arange(half, dtype=F32) * 2.0 / 64)
    ang = positions.astype(F32)[:, None] * inv_freq
    cos, sin = jnp.cos(ang), jnp.sin(ang)
    return jnp.concatenate([cos, cos, cos, cos], axis=1), jnp.concatenate([-sin, sin, -sin, sin], axis=1)


def _matmul_weights(w_in_t):
    cat_t = _to_proj_layout(w_in_t)
    return {'cat': cat_t.T, 'cat_t': cat_t}


def _late_weights(so, ao, mix, up_t, dn):
    return {'so': so, 'so_t': so.T, 'ao': ao, 'ao_t': ao.T, 'mix': mix, 'mix_t': mix.T,
            'up': up_t.T, 'up_t': up_t, 'dn': dn, 'dn_t': dn.T}


ANY = pl.BlockSpec(memory_space=pl.ANY)
MESH = pl.DeviceIdType.MESH
ROW_ALIGN = 32


def _mesh_pos():
    return lax.axis_index("x"), lax.axis_index("y"), lax.axis_index("c")


def _other_chips(x, y):
    return [(1 - x, y), (x, 1 - y), (1 - x, 1 - y)]


def _remote(src, dst, send_sems, recv_sems, k, to):
    return pltpu.make_async_remote_copy(src_ref=src, dst_ref=dst, send_sem=send_sems.at[k], recv_sem=recv_sems.at[k],
                                        device_id=to, device_id_type=MESH)


def _half(c, rh):
    return pl.ds(pl.multiple_of(c * rh, 16), rh)


def _ag_ride(shard):
    r = shard.shape[0]
    rh = r // 2

    def first_copies(w_ref, out_ref, send_sems, recv_sems):
        x, y, c = _mesh_pos()
        p = 2 * x + y
        mine = _half(c, rh)
        cps = [_remote(w_ref, out_ref.at[p], send_sems, recv_sems, 6, (x, y, 1 - c))]
        return cps + [_remote(w_ref.at[mine], out_ref.at[p, mine], send_sems, recv_sems, j, (cx, cy, c))
                      for j, (cx, cy) in enumerate(_other_chips(x, y))]

    def start(ins, outs, send_sems, recv_sems):
        for cp in first_copies(ins[0], outs[0], send_sems, recv_sems):
            cp.start()

    def forwards(out_ref, send_sems, recv_sems, half):
        x, y, c = _mesh_pos()
        return [_remote(out_ref.at[2 * cx + cy, half], out_ref.at[2 * cx + cy, half], send_sems, recv_sems, 3 + j, (x, y, 1 - c))
                for j, (cx, cy) in enumerate(_other_chips(x, y))]

    def middle(ins, outs, send_sems, recv_sems):
        x, y, c = _mesh_pos()
        mine = _half(c, rh)
        for j, (fwd, (cx, cy)) in enumerate(zip(forwards(outs[0], send_sems, recv_sems, mine), _other_chips(x, y))):
            slab = outs[0].at[2 * cx + cy, mine]
            _remote(slab, slab, send_sems, recv_sems, j, (x, y, 1 - c)).wait_recv()
            fwd.start()

    def finish(ins, outs, send_sems, recv_sems):
        w_ref, out_ref = ins[0], outs[0]
        x, y, c = _mesh_pos()
        for cp in forwards(out_ref, send_sems, recv_sems, _half(1 - c, rh)):
            cp.wait_recv()
        _remote(w_ref, out_ref.at[2 * x + y], send_sems, recv_sems, 6, (x, y, 1 - c)).wait_recv()
        for cp in first_copies(w_ref, out_ref, send_sems, recv_sems) + forwards(out_ref, send_sems, recv_sems, _half(c, rh)):
            cp.wait_send()

    return _Ride((shard,), (SDS((N_CHIPS, r, COMM_LANES), shard.dtype),), 7, start, finish, middle)


def _rs_ride(gbuf):
    rh = gbuf.shape[1] // 2

    def copies(g_ref, r_ref, send_sems, recv_sems, landing):
        x, y, c = _mesh_pos()
        cps = []
        for k, (cx, cy) in enumerate(_other_chips(x, y)):
            for h in range(2):
                slot = 2 * k + c if landing else 2 * k + h
                cps.append(pltpu.make_async_remote_copy(
                    src_ref=g_ref.at[2 * cx + cy, pl.ds(h * rh, rh)], dst_ref=r_ref.at[slot],
                    send_sem=send_sems.at[2 * k + h], recv_sem=recv_sems.at[slot],
                    device_id=(cx, cy, h), device_id_type=MESH))
        cps.append(_remote(g_ref.at[2 * x + y, _half(1 - c, rh)], r_ref.at[6], send_sems, recv_sems, 6, (x, y, 1 - c)))
        return cps

    def start(ins, outs, send_sems, recv_sems):
        for cp in copies(ins[0], outs[0], send_sems, recv_sems, True):
            cp.start()

    def finish(ins, outs, send_sems, recv_sems):
        for cp in copies(ins[0], outs[0], send_sems, recv_sems, False):
            cp.wait()

    return _Ride((gbuf,), (SDS((7, rh, COMM_LANES), gbuf.dtype),), 7, start, finish)


def _rs_sum(name, gbuf, got, pc_idx):
    rh = got.shape[1]
    tr = max(d for d in range(16, 513, 16) if rh % d == 0)
    nb = rh // tr

    def body(pc_ref, own_ref, *refs):
        o_ref = refs[7]
        p, c = pc_ref[0], pc_ref[1]
        own = own_ref[0].astype(F32)
        slots = [r[0].astype(F32) for r in refs[:7]]

        def term(q, h):
            code = p ^ q
            far = jnp.where(code == 2, slots[h], jnp.where(code == 1, slots[2 + h], slots[4 + h]))
            return jnp.where(code == 0, jnp.where(c == h, own, slots[6]), far)

        acc = term(0, 0)
        for q, h in [(0, 1), (1, 0), (1, 1), (2, 0), (2, 1), (3, 0), (3, 1)]:
            acc = acc + term(q, h)
        o_ref[0] = acc

    slot = lambda s: pl.BlockSpec((1, tr, COMM_LANES), lambda i, pc: (s, i, 0))
    return pl.pallas_call(
        body, name=name,
        grid_spec=pltpu.PrefetchScalarGridSpec(
            num_scalar_prefetch=1, grid=(nb,),
            in_specs=[pl.BlockSpec((1, tr, COMM_LANES), lambda i, pc: (pc[0], pc[1] * nb + i, 0))] + [slot(s) for s in range(7)],
            out_specs=pl.BlockSpec((1, tr, COMM_LANES), lambda i, pc: (pc[1], i, 0))),
        out_shape=SDS((2, rh, COMM_LANES), F32), compiler_params=_cp(dimension_semantics=("arbitrary",)),
    )(pc_idx, gbuf, *([got] * 7))


def _pair_gather_all(bufs):
    n = len(bufs)

    def body(*refs):
        outs, send_sems, recv_sems = refs[n:2 * n], refs[2 * n], refs[2 * n + 1]
        x, y, c = _mesh_pos()
        cps = [_remote(o.at[c], o.at[c], send_sems, recv_sems, k, (x, y, 1 - c)) for k, o in enumerate(outs)]
        for cp in cps:
            cp.start()
        for k, o in enumerate(outs):
            _remote(o.at[1 - c], o.at[1 - c], send_sems, recv_sems, k, (x, y, 1 - c)).wait_recv()
        for cp in cps:
            cp.wait_send()

    return pl.pallas_call(
        body, name="grad_pair_gather", in_specs=[ANY] * n, out_specs=[ANY] * n,
        out_shape=[SDS(b.shape, b.dtype) for b in bufs],
        scratch_shapes=[pltpu.SemaphoreType.DMA((n,)), pltpu.SemaphoreType.DMA((n,))],
        input_output_aliases={k: k for k in range(n)},
    )(*bufs)


def _pack_rows(big, small=()):
    parts = list(big)
    if small:
        flat = jnp.concatenate([p.reshape(-1) for p in small])
        k = -(-flat.shape[0] // (16 * COMM_LANES)) * 16
        parts.append(jnp.pad(flat, (0, k * COMM_LANES - flat.shape[0])).reshape(k, COMM_LANES))
    pad = -sum(p.shape[0] for p in parts) % ROW_ALIGN
    if pad:
        parts.append(jnp.zeros((pad, COMM_LANES), parts[0].dtype))
    return jnp.concatenate(parts, axis=0) if len(parts) > 1 else parts[0]


def _take(flat, off, shape):
    n = 1
    for d in shape:
        n *= d
    return flat[off:off + n].reshape(shape), off + n


BIG_ROWS = {'w_in': 2184, 'ssd_w_out': 512, 'attn_w_out': 256, 'w_mix_out': 256, 'ffn_w_up': 1408, 'ffn_w_down': 704}
TRANSPOSED = ('w_in', 'ffn_w_up')
LATE = ('ssd_w_out', 'attn_w_out', 'w_mix_out', 'ffn_w_up', 'ffn_w_down')
CONV_TAPS = ('ssd_conv_w', 'ffn_conv_w')
RS_GROUPS = {'ffn': ('ffn_w_up', 'ffn_w_down'), 'mix': ('ssd_w_out', 'attn_w_out', 'w_mix_out'), 'w_in': ('w_in',)}


def _exchange(name, ride):
    n_in, n_out = len(ride.ins), len(ride.outs)

    def body(*refs):
        ins, outs, sems = refs[:n_in], refs[n_in:n_in + n_out], refs[n_in + n_out:]
        ride.start(ins, outs, *sems)
        if ride.middle is not None:
            ride.middle(ins, outs, *sems)
        ride.finish(ins, outs, *sems)

    return pl.pallas_call(
        body, name=name, in_specs=[ANY] * n_in, out_specs=[ANY] * n_out, out_shape=list(ride.outs),
        scratch_shapes=[pltpu.SemaphoreType.DMA((ride.n_sems,)), pltpu.SemaphoreType.DMA((ride.n_sems,))],
    )(*ride.ins)


def kernel(x, positions, norm_mix_pre_w, w_in, ssd_conv_w, ssd_conv_b, ssd_dt_bias, ssd_a_log, ssd_d, ssd_norm_w, ssd_w_out, attn_sinks, attn_w_out, w_mix_out, norm_mix_post_w, norm_ffn_pre_w, ffn_w_up, ffn_conv_w, ffn_conv_b, ffn_w_down, norm_ffn_post_w, loss_target, m_norm_mix_pre_w, m_w_in, m_ssd_conv_w, m_ssd_conv_b, m_ssd_dt_bias, m_ssd_a_log, m_ssd_d, m_ssd_norm_w, m_ssd_w_out, m_attn_sinks, m_attn_w_out, m_w_mix_out, m_norm_mix_post_w, m_norm_ffn_pre_w, m_ffn_w_up, m_ffn_conv_w, m_ffn_conv_b, m_ffn_w_down, m_norm_ffn_post_w, v_norm_mix_pre_w, v_w_in, v_ssd_conv_w, v_ssd_conv_b, v_ssd_dt_bias, v_ssd_a_log, v_ssd_d, v_ssd_norm_w, v_ssd_w_out, v_attn_sinks, v_attn_w_out, v_w_mix_out, v_norm_mix_post_w, v_norm_ffn_pre_w, v_ffn_w_up, v_ffn_conv_w, v_ffn_conv_b, v_ffn_w_down, v_norm_ffn_post_w):
    given = dict(locals())
    w = {n: given[n][0] for n in WEIGHTS}
    w = {n: (a if a.ndim == 2 else a[None]) for n, a in w.items()}
    mom_m = {n: given['m_' + n].reshape(w[n].shape) for n in WEIGHTS}
    mom_v = {n: given['v_' + n].reshape(w[n].shape) for n in WEIGHTS}
    cx, cy, cc = _mesh_pos()
    pc_idx = jnp.stack([2 * cx + cy, cc]).astype(jnp.int32)

    rows_of = lambda n: (w[n].T if n in TRANSPOSED else w[n]).astype(BF16)
    gathered = _exchange("w_in_all_gather", _ag_ride(_pack_rows([rows_of('w_in')])))[0]
    wb = _matmul_weights(jnp.concatenate([gathered[s, :BIG_ROWS['w_in']] for s in range(N_CHIPS)], axis=0))
    taps = [lax.bitcast_convert_type(w[n], BF16) for n in CONV_TAPS]

    def unpack_late(arrived):
        rows, conv = {n: [] for n in LATE}, {n: [] for n in CONV_TAPS}
        for s in range(N_CHIPS):
            r0 = 0
            for n in LATE:
                rows[n].append(arrived[0][s, r0:r0 + BIG_ROWS[n]])
                r0 += BIG_ROWS[n]
            flat, off = arrived[0][s, r0:r0 + 16].reshape(-1), 0
            for n in CONV_TAPS:
                a, off = _take(flat, off, w[n].shape + (2,))
                conv[n].append(lax.bitcast_convert_type(a, F32))
        full = {n: jnp.concatenate(rows[n], axis=0) for n in LATE}
        return (_late_weights(*[full[n] for n in LATE]),
                {'ssd_conv_w': _group_channels(jnp.concatenate(conv['ssd_conv_w'], axis=1)),
                 'ffn_conv_w': jnp.concatenate(conv['ffn_conv_w'], axis=1)})

    late = (_ag_ride(_pack_rows([rows_of(n) for n in LATE], taps)), unpack_late)

    sent = {}

    def rides(group, g):
        parts = []
        for s in range(N_CHIPS):
            slab = []
            for n in RS_GROUPS[group]:
                lo, hi = BIG_ROWS[n] * s, BIG_ROWS[n] * (s + 1)
                slab += _proj_rows(g[n], lo, hi) if n == 'w_in' else [g[n][lo:hi]]
            slab = [a.astype(BF16) for a in slab]
            pad = -sum(a.shape[0] for a in slab) % ROW_ALIGN
            parts += slab + ([jnp.zeros((pad, COMM_LANES), BF16)] if pad else [])
        sent[group] = jnp.concatenate(parts, axis=0).reshape(N_CHIPS, -1, COMM_LANES)
        return _rs_ride(sent[group])

    ps = {n: w[n] for n in REPLICATED}
    ps['ssd_conv_b'] = _group_channels(w['ssd_conv_b'])
    cos_t, sin_t = _rope_tables(positions[0])
    loss, grad_x, grads, rode = _local_step(x[0], cos_t, sin_t, loss_target[0], wb, ps, late, rides)
    grads['ssd_conv_w'] = _ungroup_channels(grads['ssd_conv_w'])
    grads['ssd_conv_b'] = _ungroup_channels(grads['ssd_conv_b'])

    shard_cols = {n: sh[1] for n, _, sh in SHARDED}
    parts = []
    for s in range(N_CHIPS):
        small = [grads[n][:, shard_cols[n] * s:shard_cols[n] * (s + 1)] for n in CONV_TAPS] + [grads[n] for n in REPLICATED]
        flat = _pack_rows([], small)
        high = flat.astype(BF16)
        parts += [high, (flat - high.astype(F32)).astype(BF16)]
    sent['small'] = jnp.concatenate(parts, axis=0).reshape(N_CHIPS, -1, COMM_LANES)
    rode['small'] = _exchange("grad_small_exchange", _rs_ride(sent['small']))

    groups = ('ffn', 'mix', 'w_in', 'small')
    red = _pair_gather_all([_rs_sum("grad_sum_" + g, sent[g], rode[g][0], pc_idx) for g in groups])
    red = {g: r.reshape(-1, COMM_LANES) for g, r in zip(groups, red)}
    g_red = {}
    for g in groups[:3]:
        r0 = 0
        for n in RS_GROUPS[g]:
            g_red[n] = red[g][r0:r0 + BIG_ROWS[n]].T if n in TRANSPOSED else red[g][r0:r0 + BIG_ROWS[n]]
            r0 += BIG_ROWS[n]
    half = red['small'].shape[0] // 2
    flat, off = (red['small'][:half] + red['small'][half:]).reshape(-1), 0
    for n in CONV_TAPS + REPLICATED:
        g_red[n], off = _take(flat, off, w[n].shape)

    small_names = [n for n in WEIGHTS if n not in MATMUL_WEIGHTS]
    delta, new_m, new_v = {}, {}, {}
    for n in MATMUL_WEIGHTS:
        delta[n], new_m[n], new_v[n] = _adamw("adamw_" + n, w[n], g_red[n], mom_m[n], mom_v[n],
                                                  tr=max(d for d in range(8, 353, 8) if w[n].shape[0] % d == 0))
    packed = [_pack_small([d[n] for n in small_names]) for d in (w, g_red, mom_m, mom_v)]
    outs = _adamw("adamw_small", *packed, tr=packed[0].shape[0])
    for res, o in zip((delta, new_m, new_v), outs):
        fl, off = o.reshape(-1), 0
        for n in small_names:
            res[n], off = _take(fl, off, w[n].shape)

    loss_all = lax.psum(loss[0, 0], ("x", "y", "c"))
    shaped = lambda d: [d[n].reshape(given[n].shape) for n in WEIGHTS]
    return (loss_all, grad_x[None], *shaped(g_red), *shaped(delta), *shaped(new_m), *shaped(new_v))


def _pack_small(pieces):
    flat = jnp.concatenate([p.reshape(-1) for p in pieces])
    rows = -(-flat.shape[0] // (128 * 8)) * 8
    return jnp.pad(flat, (0, rows * 128 - flat.shape[0])).reshape(rows, 128)
```

```python
from typing import Callable, NamedTuple

import jax
import jax.numpy as jnp
from jax import lax
from jax.experimental import pallas as pl
from jax.experimental.pallas import tpu as pltpu

F32 = jnp.float32
BF16 = jnp.bfloat16
SDS = jax.ShapeDtypeStruct
HIGHEST = lax.Precision.HIGHEST

D_MODEL = 1024
SSD_D_INNER = 2048
SSD_N_HEADS = 32
SSD_HEAD_DIM = 64
SSD_N_GROUPS = 4
SSD_HEADS_PER_GROUP = 8
SSD_D_STATE = 128
SSD_CONV_DIM = 3072
CHUNK = 128
ATTN_N_HEADS = 16
KV_WIDTH = 256
FFN_D_FF = 2816
IN_PROJ_DIM = 8736
ROPE_THETA = 10000.0
NORM_EPS = 1e-6
ADAM_LR, ADAM_B1, ADAM_B2, ADAM_EPS, ADAM_WD, ADAM_STEP = 0.001, 0.9, 0.999, 1e-08, 0.01, 10

PROJ_W = 9216
OFF_Q, OFF_K, OFF_V, OFF_Z, OFF_DT, OFF_GS, OFF_GA, OFF_XBC = 0, 1024, 1280, 1536, 3584, 4096, 5120, 6144
GROUP_W = 768
PROJ_SEGS = ([(0, 2048, OFF_Z)]
             + [(2048 + 512 * g, 512, OFF_XBC + GROUP_W * g) for g in range(4)]
             + [(4096 + 128 * g, 128, OFF_XBC + GROUP_W * g + 512) for g in range(4)]
             + [(4608 + 128 * g, 128, OFF_XBC + GROUP_W * g + 640) for g in range(4)]
             + [(5120, 32, OFF_DT), (5152, 1024, OFF_Q), (6176, 256, OFF_K), (6432, 256, OFF_V),
                (6688, 1024, OFF_GS), (7712, 1024, OFF_GA)])
VMEM_LIMIT_MB = 48
NEG = -1e30

WEIGHTS = ('norm_mix_pre_w', 'w_in', 'ssd_conv_w', 'ssd_conv_b', 'ssd_dt_bias', 'ssd_a_log', 'ssd_d', 'ssd_norm_w',
           'ssd_w_out', 'attn_sinks', 'attn_w_out', 'w_mix_out', 'norm_mix_post_w', 'norm_ffn_pre_w', 'ffn_w_up',
           'ffn_conv_w', 'ffn_conv_b', 'ffn_w_down', 'norm_ffn_post_w')
SHARDED = (('w_in', 1, (1024, 2184)), ('ssd_conv_w', 1, (4, 768)), ('ssd_w_out', 0, (512, 1024)),
           ('attn_w_out', 0, (256, 1024)), ('w_mix_out', 0, (256, 1024)), ('ffn_w_up', 1, (1024, 1408)),
           ('ffn_conv_w', 1, (3, 1408)), ('ffn_w_down', 0, (704, 1024)))
MATMUL_WEIGHTS = ('w_in', 'ssd_w_out', 'attn_w_out', 'w_mix_out', 'ffn_w_up', 'ffn_w_down')
REPLICATED = tuple(n for n in WEIGHTS if n not in {s[0] for s in SHARDED})
N_CHIPS = 4
COMM_LANES = 1024


def _cp(vmem_mb=VMEM_LIMIT_MB, **kw):
    return pltpu.CompilerParams(vmem_limit_bytes=vmem_mb << 20, **kw)


class _Ride(NamedTuple):
    ins: tuple
    outs: tuple
    n_sems: int
    start: Callable
    finish: Callable
    middle: Callable = None


def _ride_parts(ride):
    if ride is None:
        return [], [], [], [], []
    hbm = pl.BlockSpec(memory_space=pl.ANY)
    return (list(ride.ins), [hbm] * len(ride.ins), list(ride.outs), [hbm] * len(ride.outs),
            [pltpu.SemaphoreType.DMA((ride.n_sems,)), pltpu.SemaphoreType.DMA((ride.n_sems,))])


def _ride_run(ride, first, last, in_refs, out_refs, sems, middle=None):
    if ride is None:
        return

    @pl.when(first)
    def _():
        ride.start(in_refs, out_refs, *sems)

    if ride.middle is not None:
        @pl.when(last if middle is None else middle)
        def _():
            ride.middle(in_refs, out_refs, *sems)

    @pl.when(last)
    def _():
        ride.finish(in_refs, out_refs, *sems)


def _iota(shape, axis):
    return lax.broadcasted_iota(jnp.int32, shape, axis)


def _sigmoid(v):
    return 1.0 / (1.0 + jnp.exp(-v))


def _mm_call(name, a, b, *, tm, tn, tk, epilogue, outs, extra_in=(), trans_a=False, fill=None, ride=None):
    if trans_a:
        kdim, m = a.shape
    else:
        m, kdim = a.shape
    n = b.shape[1]
    assert b.shape[0] == kdim and m % tm == 0 and n % tn == 0 and kdim % tk == 0, (name, a.shape, b.shape, tm, tn, tk)
    gi, gj, gk = m // tm, n // tn, kdim // tk
    n_in, n_out = len(extra_in), len(outs)

    n_fill = 0 if fill is None else 1
    r_ops, r_in_specs, r_outs, r_out_specs, r_scratch = _ride_parts(ride)

    def body(a_ref, b_ref, *rest):
        ins = rest[:n_in]
        rest = rest[n_in + n_fill:]
        r_in, rest = rest[:len(r_ops)], rest[len(r_ops):]
        out_refs, rest = rest[:n_out], rest[n_out:]
        r_out, scratch = rest[:len(r_outs)], rest[len(r_outs):]
        i, j, k = pl.program_id(0), pl.program_id(1), pl.program_id(2)
        _ride_run(ride, (i == 0) & (j == 0) & (k == 0), (i == gi - 1) & (j == gj - 1) & (k == gk - 1),
                  r_in, r_out, scratch[-2:])
        av = a_ref[...].astype(BF16)
        bv = b_ref[...].astype(BF16)
        if trans_a:
            part = lax.dot_general(av, bv, (((0,), (0,)), ((), ())), preferred_element_type=F32)
        else:
            part = jnp.dot(av, bv, preferred_element_type=F32)
        if gk == 1:
            epilogue(part, i, j, ins, out_refs)
        else:
            acc = scratch[0]

            @pl.when(k == 0)
            def _():
                acc[...] = part

            @pl.when(k > 0)
            def _():
                acc[...] += part

            @pl.when(k == gk - 1)
            def _():
                epilogue(acc[...], i, j, ins, out_refs)

    a_spec = pl.BlockSpec((tk, tm), lambda i, j, k: (k, i)) if trans_a else pl.BlockSpec((tm, tk), lambda i, j, k: (i, k))
    in_specs = [a_spec, pl.BlockSpec((tk, tn), lambda i, j, k: (k, j))]
    in_specs += [pl.BlockSpec(bs, im) for _, bs, im in extra_in]
    operands = [a, b] + [e[0] for e in extra_in]
    aliases = {}
    if fill is not None:
        in_specs.append(pl.BlockSpec(memory_space=pl.ANY))
        aliases = {len(operands): fill[1]}
        operands.append(fill[0])
    return pl.pallas_call(
        body, name=name, grid=(gi, gj, gk), in_specs=in_specs + r_in_specs,
        out_specs=[pl.BlockSpec(bs, im) for _, _, bs, im in outs] + r_out_specs,
        out_shape=[SDS(s, d) for s, d, _, _ in outs] + r_outs,
        scratch_shapes=([pltpu.VMEM((tm, tn), F32)] if gk > 1 else []) + r_scratch,
        input_output_aliases=aliases,
        compiler_params=_cp(dimension_semantics=("arbitrary", "arbitrary", "arbitrary")),
    )(*operands, *r_ops)


def _mm_plain(name, a, b, *, tm, tn, tk, out_dtype=F32, trans_a=False):
    m = a.shape[1] if trans_a else a.shape[0]

    def epilogue(acc, i, j, ins, outs):
        outs[0][...] = acc.astype(out_dtype)

    return _mm_call(name, a, b, tm=tm, tn=tn, tk=tk, epilogue=epilogue, trans_a=trans_a,
                    outs=[((m, b.shape[1]), out_dtype, (tm, tn), lambda i, j, k: (i, j))])[0]


def _accumulate(ref, first, value):
    @pl.when(first)
    def _():
        ref[...] = value

    @pl.when(jnp.logical_not(first))
    def _():
        ref[...] += value


def _rms_bwd(xv, w, dy):
    r = lax.rsqrt(jnp.mean(xv * xv, axis=-1, keepdims=True) + NORM_EPS)
    xn = xv * r
    dxh = dy * w
    dx = r * (dxh - xn * jnp.mean(dxh * xn, axis=-1, keepdims=True))
    return dx, jnp.sum(dy * xn, axis=0, keepdims=True)


def _norm_mm(name, x, wn, w, *, tm, tn, ride=None):
    t, dm = x.shape
    n = w.shape[1]
    tm = min(tm, t)
    gi, gj = t // tm, n // tn
    r_ops, r_in_specs, r_outs, r_out_specs, r_scratch = _ride_parts(ride)

    def body(x_ref, wn_ref, w_ref, *rest):
        r_in, rest = rest[:len(r_ops)], rest[len(r_ops):]
        o_ref, u_ref = rest[:2]
        r_out, sems = rest[2:2 + len(r_outs)], rest[2 + len(r_outs):]
        i, j = pl.program_id(0), pl.program_id(1)
        _ride_run(ride, (i == 0) & (j == 0), (i == gi - 1) & (j == gj - 1), r_in, r_out, sems,
                  middle=(i == (3 * gi) // 4) & (j == 0) if gi > 1 else None)

        @pl.when(j == 0)
        def _():
            xv = x_ref[...]
            r = lax.rsqrt(jnp.mean(xv * xv, axis=-1, keepdims=True) + NORM_EPS)
            u_ref[...] = (xv * r * wn_ref[...]).astype(BF16)

        o_ref[...] = jnp.dot(u_ref[...], w_ref[...], preferred_element_type=F32)

    return pl.pallas_call(
        body, name=name, grid=(gi, gj),
        in_specs=[pl.BlockSpec((tm, dm), lambda i, j: (i, 0)), pl.BlockSpec((1, dm), lambda i, j: (0, 0)),
                  pl.BlockSpec((dm, tn), lambda i, j: (0, j))] + r_in_specs,
        out_specs=[pl.BlockSpec((tm, tn), lambda i, j: (i, j)), pl.BlockSpec((tm, dm), lambda i, j: (i, 0))] + r_out_specs,
        out_shape=[SDS((t, n), F32), SDS((t, dm), BF16)] + r_outs, scratch_shapes=r_scratch,
        compiler_params=_cp(dimension_semantics=("arbitrary", "arbitrary")),
    )(x, wn, w, *r_ops)


def _shift_down(tile, halo, s):
    if s == 0:
        return tile
    r = pltpu.roll(tile, s, axis=0)
    h = pltpu.roll(halo, s, axis=0)
    head = jnp.where(_iota(h.shape, 0) < s, h, r[0:8])
    return jnp.concatenate([head, r[8:]], axis=0)


def _shift_up(tile, halo, s):
    if s == 0:
        return tile
    n = tile.shape[0]
    r = pltpu.roll(tile, n - s, axis=0)
    h = pltpu.roll(halo, 8 - s, axis=0)
    tail = jnp.where(_iota(h.shape, 0) >= 8 - s, h, r[n - 8:])
    return jnp.concatenate([r[:n - 8], tail], axis=0)


def _conv_apply(tile, halo, wv, bv, kw):
    acc = bv + wv[kw - 1:kw, :] * tile
    for k in range(kw - 1):
        acc = acc + wv[k:k + 1, :] * _shift_down(tile, halo, kw - 1 - k)
    return acc


def _prev_halo_spec(tm, tc, col0):
    return pl.BlockSpec((8, tc), lambda i, j: (jnp.maximum(i * (tm // 8) - 1, 0), col0 + j))


def _silu_parts(pre):
    sg = _sigmoid(pre)
    return pre * sg, sg * (1.0 + pre * (1.0 - sg))


def _conv_silu_fwd(proj, w, b, *, tm, tc=1536):
    t = proj.shape[0]
    c = w.shape[1]
    tm = min(tm, t)
    col0 = OFF_XBC // tc

    def body(x_ref, h_ref, w_ref, b_ref, o_ref, pre_ref):
        halo = jnp.where(pl.program_id(0) > 0, h_ref[...], 0.0)
        pre = _conv_apply(x_ref[...], halo, w_ref[...], b_ref[...], 4)
        o_ref[...] = _silu_parts(pre)[0]
        pre_ref[...] = pre.astype(BF16)

    tile = pl.BlockSpec((tm, tc), lambda i, j: (i, j))
    return pl.pallas_call(
        body, name="ssd_conv_fwd", grid=(t // tm, c // tc),
        in_specs=[pl.BlockSpec((tm, tc), lambda i, j: (i, col0 + j)), _prev_halo_spec(tm, tc, col0),
                  pl.BlockSpec((4, tc), lambda i, j: (0, j)), pl.BlockSpec((1, tc), lambda i, j: (0, j))],
        out_specs=[tile, tile], out_shape=[SDS((t, c), F32), SDS((t, c), BF16)],
        compiler_params=_cp(dimension_semantics=("arbitrary", "arbitrary")),
    )(proj, proj, w, b)


def _conv_silu_bwd1(d_out, pre, *, tm, tc=1536):
    t, c = pre.shape
    tm = min(tm, t)

    def body(g_ref, p_ref, o_ref, db_ref):
        i = pl.program_id(1)
        d_pre = g_ref[...] * _silu_parts(p_ref[...].astype(F32))[1]
        o_ref[...] = d_pre.astype(BF16)
        _accumulate(db_ref, i == 0, jnp.sum(d_pre, axis=0, keepdims=True))

    tile = pl.BlockSpec((tm, tc), lambda j, i: (i, j))
    return pl.pallas_call(
        body, name="ssd_conv_bwd1", grid=(c // tc, t // tm), in_specs=[tile, tile],
        out_specs=[tile, pl.BlockSpec((1, tc), lambda j, i: (0, j))],
        out_shape=[SDS((t, c), BF16), SDS((1, c), F32)],
        compiler_params=_cp(dimension_semantics=("arbitrary", "arbitrary")),
    )(d_out, pre)


def _conv_bwd2(name, d_pre, src, src_col0, w, *, tm, tc, out_cols, out_col0, fill=None):
    t, c = d_pre.shape
    kw = w.shape[0]
    tm = min(tm, t)
    ni = t // tm
    col0 = src_col0 // tc
    ocol0 = out_col0 // tc

    def body(g_ref, gn_ref, x_ref, w_ref, *rest):
        o_ref, dw_ref = rest[-2:]
        i = pl.program_id(1)
        g = g_ref[...].astype(F32)
        g_next = jnp.where(i < ni - 1, gn_ref[...].astype(F32)[0:8], 0.0)
        xv = x_ref[...]
        wv = w_ref[...]
        shifted = [_shift_up(g, g_next, kw - 1 - k) for k in range(kw)]
        d_in = wv[0:1, :] * shifted[0]
        for k in range(1, kw):
            d_in = d_in + wv[k:k + 1, :] * shifted[k]
        o_ref[...] = d_in.astype(o_ref.dtype)
        rows = [jnp.sum(shifted[k] * xv, axis=0, keepdims=True) for k in range(kw)]

        @pl.when(i == 0)
        def _():
            for k in range(kw):
                dw_ref[k:k + 1, :] = rows[k]

        @pl.when(i > 0)
        def _():
            for k in range(kw):
                dw_ref[k:k + 1, :] += rows[k]

    in_specs = [pl.BlockSpec((tm, tc), lambda j, i: (i, j)),
                pl.BlockSpec((16, tc), lambda j, i: (jnp.minimum((i + 1) * (tm // 16), t // 16 - 1), j)),
                pl.BlockSpec((tm, tc), lambda j, i: (i, col0 + j)),
                pl.BlockSpec((kw, tc), lambda j, i: (0, j))]
    operands = [d_pre, d_pre, src, w]
    if fill is not None:
        in_specs.append(pl.BlockSpec(memory_space=pl.ANY))
        operands.append(fill)
    return pl.pallas_call(
        body, name=name, grid=(c // tc, ni), in_specs=in_specs,
        out_specs=[pl.BlockSpec((tm, tc), lambda j, i: (i, ocol0 + j)), pl.BlockSpec((kw, tc), lambda j, i: (0, j))],
        out_shape=[SDS((t, out_cols), BF16), SDS((kw, c), F32)],
        input_output_aliases={} if fill is None else {4: 0},
        compiler_params=_cp(dimension_semantics=("arbitrary", "arbitrary")),
    )(*operands)


GELU_C = 0.7978845608028654


def _gelu_parts(v):
    inner = GELU_C * (v + 0.044715 * v * v * v)
    th = jnp.tanh(inner)
    val = 0.5 * v * (1.0 + th)
    grad = 0.5 * (1.0 + th) + 0.5 * v * (1.0 - th * th) * GELU_C * (1.0 + 3.0 * 0.044715 * v * v)
    return val, grad


def _ffn_act_fwd(up_raw, w, b, *, tm, tc=1408):
    t = up_raw.shape[0]
    tm = min(tm, t)
    nj = FFN_D_FF // tc
    halo = lambda i: jnp.maximum(i * (tm // 8) - 1, 0)

    def body(g_ref, gh_ref, v_ref, vh_ref, wg_ref, wv_ref, bg_ref, bv_ref, o_ref, gate_ref, val_ref):
        first = pl.program_id(0) > 0
        gate = _conv_apply(g_ref[...], jnp.where(first, gh_ref[...], 0.0), wg_ref[...], bg_ref[...], 3)
        val = _conv_apply(v_ref[...], jnp.where(first, vh_ref[...], 0.0), wv_ref[...], bv_ref[...], 3)
        o_ref[...] = (_gelu_parts(gate)[0] * val).astype(BF16)
        gate_ref[...] = gate.astype(BF16)
        val_ref[...] = val.astype(BF16)

    tile = pl.BlockSpec((tm, tc), lambda i, j: (i, j))
    return pl.pallas_call(
        body, name="ffn_act_fwd", grid=(t // tm, nj),
        in_specs=[tile, pl.BlockSpec((8, tc), lambda i, j: (halo(i), j)),
                  pl.BlockSpec((tm, tc), lambda i, j: (i, nj + j)), pl.BlockSpec((8, tc), lambda i, j: (halo(i), nj + j)),
                  pl.BlockSpec((3, tc), lambda i, j: (0, j)), pl.BlockSpec((3, tc), lambda i, j: (0, nj + j)),
                  pl.BlockSpec((1, tc), lambda i, j: (0, j)), pl.BlockSpec((1, tc), lambda i, j: (0, nj + j))],
        out_specs=[tile] * 3, out_shape=[SDS((t, FFN_D_FF), BF16)] * 3,
        compiler_params=_cp(dimension_semantics=("arbitrary", "arbitrary")),
    )(up_raw, up_raw, up_raw, up_raw, w, w, b, b)


def _ffn_act_bwd(gate, val, d_act, *, tm, tc=1408):
    t = gate.shape[0]
    tm = min(tm, t)
    nj = FFN_D_FF // tc

    def body(g_ref, v_ref, da_ref, dg_ref, dv_ref, dbg_ref, dbv_ref):
        i = pl.program_id(1)
        val = v_ref[...].astype(F32)
        ge, dge = _gelu_parts(g_ref[...].astype(F32))
        da = da_ref[...].astype(F32)
        d_gate = da * val * dge
        d_val = da * ge
        dg_ref[...] = d_gate.astype(BF16)
        dv_ref[...] = d_val.astype(BF16)
        _accumulate(dbg_ref, i == 0, jnp.sum(d_gate, axis=0, keepdims=True))
        _accumulate(dbv_ref, i == 0, jnp.sum(d_val, axis=0, keepdims=True))

    tile = pl.BlockSpec((tm, tc), lambda j, i: (i, j))
    row = pl.BlockSpec((1, tc), lambda j, i: (0, j))
    return pl.pallas_call(
        body, name="ffn_act_bwd", grid=(nj, t // tm), in_specs=[tile] * 3, out_specs=[tile, tile, row, row],
        out_shape=[SDS((t, FFN_D_FF), BF16), SDS((t, FFN_D_FF), BF16), SDS((1, FFN_D_FF), F32), SDS((1, FFN_D_FF), F32)],
        compiler_params=_cp(dimension_semantics=("arbitrary", "arbitrary")),
    )(gate, val, d_act)


def _softplus(v):
    e = jnp.exp(-jnp.abs(v))
    small = e * (1.0 - 0.5 * e)
    return jnp.maximum(v, 0.0) + jnp.where(e < 1e-4, small, jnp.log(1.0 + e))


def _dt_fwd(proj, bias_pad, *, tm):
    t = proj.shape[0]
    tm = min(tm, t)

    def body(x_ref, b_ref, g_ref, gt_ref):
        dt = _softplus(x_ref[...] + b_ref[...])
        first8 = _iota((tm, 128), 1) < 8
        for g in range(SSD_N_GROUPS):
            dg = jnp.where(first8, dt if g == 0 else pltpu.roll(dt, 128 - 8 * g, axis=1), 0.0)
            g_ref[g] = dg
            gt_ref[g] = dg.T[0:8, :]

    return pl.pallas_call(
        body, name="dt_fwd", grid=(t // tm,),
        in_specs=[pl.BlockSpec((tm, 128), lambda i: (i, OFF_DT // 128)), pl.BlockSpec((1, 128), lambda i: (0, 0))],
        out_specs=[pl.BlockSpec((SSD_N_GROUPS, tm, 128), lambda i: (0, i, 0)), pl.BlockSpec((SSD_N_GROUPS, 8, tm), lambda i: (0, 0, i))],
        out_shape=[SDS((SSD_N_GROUPS, t, 128), F32), SDS((SSD_N_GROUPS, 8, t), F32)],
        compiler_params=_cp(dimension_semantics=("arbitrary",)),
    )(proj, bias_pad)


def _dt_bwd(d_dtg, proj, bias_pad, d_proj, *, tm):
    t = proj.shape[0]
    tm = min(tm, t)

    def body(g_ref, x_ref, b_ref, _, o_ref, db_ref):
        first8 = _iota((tm, 128), 1) < 8
        d_dt = jnp.where(first8, g_ref[0], 0.0)
        for g in range(1, SSD_N_GROUPS):
            d_dt = d_dt + pltpu.roll(jnp.where(first8, g_ref[g], 0.0), 8 * g, axis=1)
        d_raw = d_dt * _sigmoid(x_ref[...] + b_ref[...])
        o_ref[:, 0:128] = d_raw.astype(BF16)
        o_ref[:, 128:512] = jnp.zeros((tm, 384), BF16)
        _accumulate(db_ref, pl.program_id(0) == 0, jnp.sum(d_raw, axis=0, keepdims=True))

    return pl.pallas_call(
        body, name="dt_bwd", grid=(t // tm,),
        in_specs=[pl.BlockSpec((SSD_N_GROUPS, tm, 128), lambda i: (0, i, 0)), pl.BlockSpec((tm, 128), lambda i: (i, OFF_DT // 128)),
                  pl.BlockSpec((1, 128), lambda i: (0, 0)), pl.BlockSpec(memory_space=pl.ANY)],
        out_specs=[pl.BlockSpec((tm, 512), lambda i: (i, OFF_DT // 512)), pl.BlockSpec((1, 128), lambda i: (0, 0))],
        out_shape=[SDS((t, PROJ_W), BF16), SDS((1, 128), F32)],
        input_output_aliases={3: 0},
        compiler_params=_cp(dimension_semantics=("arbitrary",)),
    )(d_dtg, proj, bias_pad, d_proj)


def _split3(v):
    hi = v.astype(BF16)
    r1 = v - hi.astype(F32)
    mid = r1.astype(BF16)
    return hi, mid, (r1 - mid.astype(F32)).astype(BF16)


def _times01(v, m3):
    return jnp.dot(jnp.concatenate(_split3(v), axis=1), m3, preferred_element_type=F32)


def _01times(m3, v):
    return jnp.dot(m3, jnp.concatenate(_split3(v), axis=0), preferred_element_type=F32)


def _ssd_decay(dt_ref, dtT_ref, al_ref, alT_ref, k):
    dt = dt_ref[0]
    a_row = -jnp.exp(al_ref[0])
    adt_t = dtT_ref[0] * (-jnp.exp(alT_ref[0]))
    return dt, a_row, _01times(k['low3'][...], dt * a_row), _times01(adt_t, k['up3v'][...])


def _ssd_specs(nc, rev):
    ci = (lambda c: nc - 1 - c) if rev else (lambda c: c)
    return [pl.BlockSpec((CHUNK, SSD_CONV_DIM), lambda c: (ci(c), 0)),
            pl.BlockSpec((SSD_N_GROUPS, CHUNK, 128), lambda c: (0, ci(c), 0)),
            pl.BlockSpec((SSD_N_GROUPS, 8, CHUNK), lambda c: (0, 0, ci(c))),
            pl.BlockSpec((SSD_N_GROUPS, 1, 128), lambda c: (0, 0, 0)),
            pl.BlockSpec((SSD_N_GROUPS, 8, 1), lambda c: (0, 0, 0)),
            pl.BlockSpec((1, SSD_D_INNER), lambda c: (0, 0))]


def _ssd_group_views(g, x_ref, dt_ref, dtT_ref, al_ref, alT_ref, d_ref):
    return (x_ref.at[:, g * GROUP_W:(g + 1) * GROUP_W], dt_ref.at[g:g + 1], dtT_ref.at[g:g + 1], al_ref.at[g:g + 1],
            alT_ref.at[g:g + 1], d_ref.at[:, g * 512:(g + 1) * 512])


NT = (((1,), (1,)), ((), ()))
WIDE = 8 * CHUNK
SSD_CONST_NAMES = ('e128', 'e64', 's64', 'mlo', 'mup', 'low3', 'up3', 'up3v')
SSD_CONST_SHAPES = [pltpu.VMEM((3 * CHUNK, WIDE), BF16), pltpu.VMEM((3 * CHUNK, 512), BF16), pltpu.VMEM((512, CHUNK), BF16),
                    pltpu.VMEM((CHUNK, WIDE), F32), pltpu.VMEM((CHUNK, WIDE), F32), pltpu.VMEM((CHUNK, 3 * CHUNK), BF16),
                    pltpu.VMEM((CHUNK, 3 * CHUNK), BF16), pltpu.VMEM((3 * CHUNK, CHUNK), BF16)]


def _ssd_init_consts(k):
    row, col = _iota((3 * CHUNK, WIDE), 0), _iota((3 * CHUNK, WIDE), 1)
    k['e128'][...] = ((col >> 7) == (row & 127)).astype(BF16)
    k['e64'][...] = ((_iota((3 * CHUNK, 512), 1) >> 6) == (_iota((3 * CHUNK, 512), 0) & 127)).astype(BF16)
    k['s64'][...] = ((_iota((512, CHUNK), 0) >> 6) == _iota((512, CHUNK), 1)).astype(BF16)
    row, col = _iota((CHUNK, WIDE), 0), _iota((CHUNK, WIDE), 1)
    k['mlo'][...] = (row >= (col & 127)).astype(F32)
    k['mup'][...] = (row <= (col & 127)).astype(F32)
    row, col = _iota((CHUNK, 3 * CHUNK), 0), _iota((CHUNK, 3 * CHUNK), 1) & 127
    k['low3'][...] = (row >= col).astype(BF16)
    k['up3'][...] = (row <= col).astype(BF16)
    row, col = _iota((3 * CHUNK, CHUNK), 0) & 127, _iota((3 * CHUNK, CHUNK), 1)
    k['up3v'][...] = (row <= col).astype(BF16)


def _ssd_common(x_ref, dt_ref, dtT_ref, al_ref, alT_ref, k):
    dt, a_row, acs, acs_t = _ssd_decay(dt_ref, dtT_ref, al_ref, alT_ref, k)
    ecol = _times01(acs, k['e128'][...])
    rrow = jnp.concatenate([jnp.broadcast_to(acs_t[j:j + 1, :], (CHUNK, CHUNK)) for j in range(8)], axis=1)
    a64 = _times01(acs, k['e64'][...])
    dt64 = _times01(dt, k['e64'][...])
    a_end64 = a64[CHUNK - 1:CHUNK, :]
    xs = x_ref[:, 0:512]
    return dict(dt=dt, a_row=a_row, acs=acs, seg=ecol - rrow, dt64=dt64, e_a=jnp.exp(a64), decay=jnp.exp(a_end64 - a64),
                e_end64=jnp.exp(a_end64), xs=xs, xdt=xs * dt64, bm=x_ref[:, 512:640], cm=x_ref[:, 640:768])


def _pair_blocks(v):
    lo = _iota((CHUNK, 128), 1) < 64
    out = []
    for i in range(4):
        ch = v[:, i * 128:(i + 1) * 128]
        out.append(jnp.concatenate([jnp.where(lo, ch, 0.0), jnp.where(lo, 0.0, ch)], axis=0).astype(BF16))
    return out


def _tile8(m):
    return jnp.concatenate([m] * 8, axis=1)


def _ssd_fwd(xc, dtg, dtg_t, alog, alog_t, d_exp):
    t = xc.shape[0]
    nc = t // CHUNK

    def body(xa_ref, dta_ref, dtTa_ref, ala_ref, alTa_ref, da_ref, ya_ref, hs_ref, h_scr, *consts):
        c = pl.program_id(0)
        k = dict(zip(SSD_CONST_NAMES, consts))

        @pl.when(c == 0)
        def _():
            _ssd_init_consts(k)
            h_scr[...] = jnp.zeros_like(h_scr)

        for g in range(SSD_N_GROUPS):
            x_ref, dt_ref, dtT_ref, al_ref, alT_ref, d_ref = _ssd_group_views(g, xa_ref, dta_ref, dtTa_ref, ala_ref, alTa_ref, da_ref)
            v = _ssd_common(x_ref, dt_ref, dtT_ref, al_ref, alT_ref, k)
            b16, c16 = v['bm'].astype(BF16), v['cm'].astype(BF16)
            cb = lax.dot_general(c16, b16, NT, preferred_element_type=F32)
            m16 = (jnp.exp(jnp.minimum(v['seg'], 0.0)) * k['mlo'][...] * _tile8(cb)).astype(BF16)
            xbd = _pair_blocks(v['xdt'])
            y_diag = jnp.concatenate([jnp.dot(m16[:, i * 256:(i + 1) * 256], xbd[i], preferred_element_type=F32)
                                      for i in range(4)], axis=1)
            ht = h_scr[g]
            y_off = jnp.dot(c16, ht.astype(BF16), preferred_element_type=F32)
            ya_ref[:, g * 512:(g + 1) * 512] = y_diag + v['e_a'] * y_off + d_ref[...] * v['xs']
            st = jnp.dot(v['bm'].T.astype(BF16), (v['xdt'] * v['decay']).astype(BF16), preferred_element_type=F32)
            hs_ref[0, g] = ht
            h_scr[g] = ht * v['e_end64'] + st

    return pl.pallas_call(
        body, name="ssd_fwd", grid=(nc,), in_specs=_ssd_specs(nc, False),
        out_specs=[pl.BlockSpec((CHUNK, SSD_D_INNER), lambda c: (c, 0)),
                   pl.BlockSpec((1, SSD_N_GROUPS, SSD_D_STATE, 512), lambda c: (c, 0, 0, 0))],
        out_shape=[SDS((t, SSD_D_INNER), F32), SDS((nc, SSD_N_GROUPS, SSD_D_STATE, 512), F32)],
        scratch_shapes=[pltpu.VMEM((SSD_N_GROUPS, SSD_D_STATE, 512), F32)] + SSD_CONST_SHAPES,
        compiler_params=_cp(dimension_semantics=("arbitrary",)),
    )(xc, dtg, dtg_t, alog, alog_t, d_exp)


def _ssd_bwd(xc, dtg, dtg_t, alog, alog_t, d_exp, d_y, hs, ride=None):
    t = xc.shape[0]
    nc = t // CHUNK

    r_ops, r_in_specs, r_outs, r_out_specs, r_scratch = _ride_parts(ride)

    def body(xa_ref, dta_ref, dtTa_ref, ala_ref, alTa_ref, da_ref, dya_ref, hs_ref, *rest):
        r_in, rest = rest[:len(r_ops)], rest[len(r_ops):]
        dxa_ref, ddta_ref, dal_ref, dd_ref = rest[:4]
        r_out, rest = rest[4:4 + len(r_outs)], rest[4 + len(r_outs):]
        g_scr, consts, sems = rest[0], rest[1:1 + len(SSD_CONST_NAMES)], rest[1 + len(SSD_CONST_NAMES):]
        c = pl.program_id(0)
        _ride_run(ride, c == 0, c == nc - 1, r_in, r_out, sems)
        k = dict(zip(SSD_CONST_NAMES, consts))

        @pl.when(c == 0)
        def _():
            _ssd_init_consts(k)
            g_scr[...] = jnp.zeros_like(g_scr)

        for g in range(SSD_N_GROUPS):
            views = _ssd_group_views(g, xa_ref, dta_ref, dtTa_ref, ala_ref, alTa_ref, da_ref)
            one_group(c, g, k, *views, dya_ref.at[:, g * 512:(g + 1) * 512], hs_ref, g_scr,
                      dxa_ref.at[:, g * GROUP_W:(g + 1) * GROUP_W], ddta_ref.at[g:g + 1], dal_ref, dd_ref)

    def one_group(c, g, k, x_ref, dt_ref, dtT_ref, al_ref, alT_ref, d_ref, dy_ref, hs_ref, g_scr, dx_ref, ddt_ref,
                  dal_ref, dd_ref):
        s64, mlo, mup = k['s64'], k['mlo'], k['mup']
        v = _ssd_common(x_ref, dt_ref, dtT_ref, al_ref, alT_ref, k)
        dt, a_row, xs, xdt, e_a, decay = v['dt'], v['a_row'], v['xs'], v['xdt'], v['e_a'], v['decay']
        row, col = _iota((CHUNK, CHUNK), 0), _iota((CHUNK, CHUNK), 1)
        b16, c16 = v['bm'].astype(BF16), v['cm'].astype(BF16)
        ct16 = v['cm'].T.astype(BF16)
        cb = lax.dot_general(c16, b16, NT, preferred_element_type=F32)
        cbt = lax.dot_general(b16, c16, NT, preferred_element_type=F32)
        lmat = jnp.exp(jnp.minimum(v['seg'], 0.0)) * mlo[...]
        lmat_t = jnp.exp(jnp.minimum(-v['seg'], 0.0)) * mup[...]
        mmat, mmat_t = lmat * _tile8(cb), lmat_t * _tile8(cbt)
        mt16 = mmat_t.astype(BF16)
        dy = dy_ref[...]
        dye, xdec = dy * e_a, xdt * decay
        dy16, dye16, xdec16 = dy.astype(BF16), dye.astype(BF16), xdec.astype(BF16)
        xdt16 = xdt.astype(BF16)
        ht, gt = hs_ref[0, g], g_scr[g]
        ht16, gt16 = ht.astype(BF16), gt.astype(BF16)
        xbd, dybd = _pair_blocks(xdt), _pair_blocks(dy)
        d_m, d_mt, d_x = [], [], []
        for i in range(4):
            csl = slice(i * 128, (i + 1) * 128)
            d_m.append(lax.dot_general(dy16[:, csl], xbd[i], NT, preferred_element_type=F32))
            d_mt.append(lax.dot_general(xdt16[:, csl], dybd[i], NT, preferred_element_type=F32))
            d_x.append(jnp.dot(mt16[:, i * 256:(i + 1) * 256], dybd[i], preferred_element_type=F32))
        d_m, d_mt, d_x = jnp.concatenate(d_m, axis=1), jnp.concatenate(d_mt, axis=1), jnp.concatenate(d_x, axis=1)

        def head_sum(m):
            acc = m[:, 0:CHUNK]
            for j in range(1, 8):
                acc = acc + m[:, j * CHUNK:(j + 1) * CHUNK]
            return acc

        def seg64(p):
            return jnp.dot(p.astype(BF16), s64[...], preferred_element_type=F32)

        d_cb16 = head_sum(d_m * lmat).astype(BF16)
        d_cbt16 = head_sum(d_mt * lmat_t).astype(BF16)
        dseg = d_m * mmat - d_mt * mmat_t
        da_seg = jnp.zeros((CHUNK, CHUNK), F32)
        for j in range(8):
            da_seg = jnp.where(col == j, jnp.sum(dseg[:, j * CHUNK:(j + 1) * CHUNK], axis=1, keepdims=True), da_seg)
        ch = jnp.dot(c16, ht16, preferred_element_type=F32)
        bg = jnp.dot(b16, gt16, preferred_element_type=F32)
        d_x = d_x + decay * bg
        d_decay = seg64(xdec * bg)
        e_end = jnp.exp(v['acs'][CHUNK - 1:CHUNK, :])
        d_end = e_end * jnp.sum(seg64(gt * ht), axis=0, keepdims=True) + jnp.sum(d_decay, axis=0, keepdims=True)
        d_a = seg64(dye * ch) - d_decay + da_seg + jnp.where(row == CHUNK - 1, d_end, 0.0)
        dx_ref[:, 0:512] = d_x * v['dt64'] + d_ref[...] * dy
        dx_ref[:, 640:768] = (lax.dot_general(dye16, ht16, NT, preferred_element_type=F32)
                              + jnp.dot(d_cb16, b16, preferred_element_type=F32))
        dx_ref[:, 512:640] = (lax.dot_general(xdec16, gt16, NT, preferred_element_type=F32)
                              + jnp.dot(d_cbt16, c16, preferred_element_type=F32))
        g_scr[g] = gt * v['e_end64'] + jnp.dot(ct16, dye16, preferred_element_type=F32)
        d_adt = _01times(k['up3'][...], d_a)
        ddt_ref[0] = d_adt * a_row + seg64(d_x * xs)
        d_alog = jnp.sum(d_adt * dt, axis=0, keepdims=True) * a_row
        dd_row = jnp.sum(seg64(dy * xs), axis=0, keepdims=True)
        first = c == 0

        @pl.when(first)
        def _():
            dal_ref[g] = d_alog
            dd_ref[g] = dd_row

        @pl.when(jnp.logical_not(first))
        def _():
            dal_ref[g] += d_alog
            dd_ref[g] += dd_row

    rc = lambda c: nc - 1 - c
    whole = pl.BlockSpec((SSD_N_GROUPS, 1, 128), lambda c: (0, 0, 0))
    return pl.pallas_call(
        body, name="ssd_bwd", grid=(nc,),
        in_specs=_ssd_specs(nc, True) + [pl.BlockSpec((CHUNK, SSD_D_INNER), lambda c: (rc(c), 0)),
                                        pl.BlockSpec((1, SSD_N_GROUPS, SSD_D_STATE, 512), lambda c: (rc(c), 0, 0, 0))] + r_in_specs,
        out_specs=[pl.BlockSpec((CHUNK, SSD_CONV_DIM), lambda c: (rc(c), 0)),
                   pl.BlockSpec((SSD_N_GROUPS, CHUNK, 128), lambda c: (0, rc(c), 0)), whole, whole] + r_out_specs,
        out_shape=[SDS((t, SSD_CONV_DIM), F32), SDS((SSD_N_GROUPS, t, 128), F32),
                   SDS((SSD_N_GROUPS, 1, 128), F32), SDS((SSD_N_GROUPS, 1, 128), F32)] + r_outs,
        scratch_shapes=[pltpu.VMEM((SSD_N_GROUPS, SSD_D_STATE, 512), F32)] + SSD_CONST_SHAPES + r_scratch,
        compiler_params=_cp(dimension_semantics=("arbitrary",)),
    )(xc, dtg, dtg_t, alog, alog_t, d_exp, d_y, hs, *r_ops)


def _gated_norm_fwd(y, proj, w, *, tm):
    t = y.shape[0]
    tm = min(tm, t)

    def body(y_ref, z_ref, w_ref, o_ref):
        gv = y_ref[...] * _silu_parts(z_ref[...])[0]
        r = lax.rsqrt(jnp.mean(gv * gv, axis=-1, keepdims=True) + NORM_EPS)
        o_ref[...] = (gv * r * w_ref[...]).astype(BF16)

    tile = pl.BlockSpec((tm, 512), lambda i, g: (i, g))
    return pl.pallas_call(
        body, name="gated_norm_fwd", grid=(t // tm, SSD_N_GROUPS),
        in_specs=[tile, pl.BlockSpec((tm, 512), lambda i, g: (i, OFF_Z // 512 + g)),
                  pl.BlockSpec((1, 512), lambda i, g: (0, g))], out_specs=tile,
        out_shape=SDS((t, SSD_D_INNER), BF16),
        compiler_params=_cp(dimension_semantics=("arbitrary", "arbitrary")),
    )(y, proj, w)


def _rope(ch, cos_t, sin_t):
    first = (_iota(ch.shape, 1) & 32) == 0
    partner = jnp.where(first, pltpu.roll(ch, 96, axis=1), pltpu.roll(ch, 32, axis=1))
    return ch * cos_t + partner * sin_t


def _rope_qkv(proj, cos_t, sin_t, *, tm):
    t = proj.shape[0]
    tm = min(tm, t)

    def body(q_ref, k_ref, v_ref, c_ref, s_ref, qr_ref, kp_ref, vp_ref, kt_ref, vt_ref):
        cv, sv = c_ref[...], s_ref[...]
        lo = _iota((tm, 128), 1) < 64
        for m in range(8):
            sl = slice(m * 128, (m + 1) * 128)
            qr_ref[:, sl] = (_rope(q_ref[:, sl], cv, sv) * 0.125).astype(BF16)
        for m2 in range(2):
            sl = slice(m2 * 128, (m2 + 1) * 128)
            for src, dst, dst_t in ((_rope(k_ref[:, sl], cv, sv), kp_ref, kt_ref), (v_ref[:, sl], vp_ref, vt_ref)):
                sw = pltpu.roll(src, 64, axis=1)
                padded = (jnp.where(lo, src, 0.0), jnp.where(lo, 0.0, sw), jnp.where(lo, sw, 0.0), jnp.where(lo, 0.0, src))
                for i, pad in enumerate(padded):
                    rows = slice((4 * m2 + i) * 128, (4 * m2 + i + 1) * 128)
                    dst[:, rows] = pad.astype(BF16)
                    dst_t[rows, :] = pad.T.astype(BF16)

    return pl.pallas_call(
        body, name="rope_qkv", grid=(t // tm,),
        in_specs=[pl.BlockSpec((tm, 1024), lambda i: (i, OFF_Q // 1024)), pl.BlockSpec((tm, 256), lambda i: (i, OFF_K // 256)),
                  pl.BlockSpec((tm, 256), lambda i: (i, OFF_V // 256)), pl.BlockSpec((tm, 128), lambda i: (i, 0)),
                  pl.BlockSpec((tm, 128), lambda i: (i, 0))],
        out_specs=[pl.BlockSpec((tm, 1024), lambda i: (i, 0))] * 3 + [pl.BlockSpec((1024, tm), lambda i: (0, i))] * 2,
        out_shape=[SDS((t, 1024), BF16)] * 3 + [SDS((1024, t), BF16)] * 2,
        compiler_params=_cp(dimension_semantics=("arbitrary",)),
    )(proj, proj, proj, cos_t, sin_t)


def _attn_valid(n):
    kj, qi = _iota((2 * CHUNK, CHUNK), 0), _iota((2 * CHUNK, CHUNK), 1)
    return (kj > qi) & (kj <= qi + CHUNK) & ((n > 0) | (kj >= CHUNK))


def _attn_fwd(qr, kp, vt, sinks):
    t = qr.shape[0]
    nb = t // CHUNK

    def body(q_ref, kc_ref, kprev_ref, vc_ref, vprev_ref, sk_ref, o_ref, lse_ref):
        n = pl.program_id(0)
        valid = _attn_valid(n)
        head_row = _iota((16, CHUNK), 0)
        lse_all = jnp.zeros((16, CHUNK), F32)
        for m in range(8):
            g = m // 2
            qch = q_ref[:, m * 128:(m + 1) * 128]
            sls = [slice((2 * g + e) * 128, (2 * g + e + 1) * 128) for e in range(2)]
            kk2 = jnp.concatenate([r[:, sl] for sl in sls for r in (kprev_ref, kc_ref)], axis=0)
            vv2_t = jnp.concatenate([r[sl, :] for sl in sls for r in (vprev_ref, vc_ref)], axis=1)
            s2 = lax.dot_general(kk2, qch, NT, preferred_element_type=F32)
            probs = []
            for e in range(2):
                h = 2 * m + e
                s = jnp.where(valid, s2[2 * CHUNK * e:2 * CHUNK * (e + 1)], NEG)
                sink = sk_ref[0:1, h:h + 1]
                mx = jnp.maximum(jnp.max(s, axis=0, keepdims=True), sink)
                p = jnp.exp(s - mx)
                den = jnp.sum(p, axis=0, keepdims=True) + jnp.exp(sink - mx)
                probs.append((p * (1.0 / den)).astype(BF16))
                lse_all = jnp.where(head_row == h, mx + jnp.log(den), lse_all)
            o_t = jnp.dot(vv2_t, jnp.concatenate(probs, axis=0), preferred_element_type=F32)
            o_ref[:, m * 128:(m + 1) * 128] = o_t.T.astype(BF16)
        lse_ref[0] = lse_all

    cur = pl.BlockSpec((CHUNK, 1024), lambda n: (n, 0))
    prev = pl.BlockSpec((CHUNK, 1024), lambda n: (jnp.maximum(n - 1, 0), 0))
    cur_t = pl.BlockSpec((1024, CHUNK), lambda n: (0, n))
    prev_t = pl.BlockSpec((1024, CHUNK), lambda n: (0, jnp.maximum(n - 1, 0)))
    return pl.pallas_call(
        body, name="attn_fwd", grid=(nb,),
        in_specs=[cur, cur, prev, cur_t, prev_t, pl.BlockSpec((1, 128), lambda n: (0, 0))],
        out_specs=[cur, pl.BlockSpec((1, 16, CHUNK), lambda n: (n, 0, 0))],
        out_shape=[SDS((t, 1024), BF16), SDS((nb, 16, CHUNK), F32)],
        compiler_params=_cp(dimension_semantics=("arbitrary",)),
    )(qr, kp, kp, vt, vt, sinks)


def _attn_bwd(qr, kp, vp, kt, d_o, o, lse, sinks, cos_t, sin_t, d_proj, ride=None):
    t = qr.shape[0]
    nb = t // CHUNK

    r_ops, r_in_specs, r_outs, r_out_specs, r_scratch = _ride_parts(ride)

    def body(q_ref, kc_ref, kprev_ref, vc_ref, vprev_ref, ktc_ref, ktprev_ref, do_ref, o_ref, lse_ref, sk_ref,
             c_ref, s_ref, cp_ref, sp_ref, _, *rest):
        r_in, rest = rest[:len(r_ops)], rest[len(r_ops):]
        dqkv_ref, dsk_ref = rest[:2]
        r_out, rest = rest[2:2 + len(r_outs)], rest[2 + len(r_outs):]
        acc_k, acc_v, dq_scr = rest[:3]
        n = pl.program_id(0)
        _ride_run(ride, n == 0, n == nb, r_in, r_out, rest[3:])
        lane = _iota((CHUNK, 128), 1)
        lo = lane < 64
        lane1 = _iota((1, 128), 1)

        @pl.when(n == 0)
        def _():
            acc_k[...] = jnp.zeros_like(acc_k)
            acc_v[...] = jnp.zeros_like(acc_v)
            dsk_ref[...] = jnp.zeros((1, 128), F32)

        @pl.when(n > 0)
        def _():
            dqkv_ref[:, 0:1024] = dq_scr[...]
            for r in range(8):
                acc_k[r, 0:CHUNK] = acc_k[r, CHUNK:2 * CHUNK]
                acc_v[r, 0:CHUNK] = acc_v[r, CHUNK:2 * CHUNK]
                acc_k[r, CHUNK:2 * CHUNK] = jnp.zeros((CHUNK, 128), F32)
                acc_v[r, CHUNK:2 * CHUNK] = jnp.zeros((CHUNK, 128), F32)

        @pl.when(n < nb)
        def _():
            valid = _attn_valid(n)
            lse_all = lse_ref[0]
            dsk = jnp.zeros((1, 128), F32)
            for m in range(8):
                g = m // 2
                csl = slice(m * 128, (m + 1) * 128)
                qch = q_ref[:, csl]
                doch = do_ref[:, csl]
                prod_t = (doch.astype(F32) * o_ref[:, csl].astype(F32)).T
                sls = [slice((2 * g + e) * 128, (2 * g + e + 1) * 128) for e in range(2)]
                kk2 = jnp.concatenate([r[:, sl] for sl in sls for r in (kprev_ref, kc_ref)], axis=0)
                vv2 = jnp.concatenate([r[:, sl] for sl in sls for r in (vprev_ref, vc_ref)], axis=0)
                kk2_t = jnp.concatenate([r[sl, :] for sl in sls for r in (ktprev_ref, ktc_ref)], axis=1)
                s2 = lax.dot_general(kk2, qch, NT, preferred_element_type=F32)
                d_p2 = lax.dot_general(vv2, doch, NT, preferred_element_type=F32)
                ps, d_ss = [], []
                for e in range(2):
                    h = 2 * m + e
                    rows = slice(2 * CHUNK * e, 2 * CHUNK * (e + 1))
                    lse_h = lse_all[h:h + 1, :]
                    p = jnp.exp(jnp.where(valid, s2[rows], NEG) - lse_h)
                    delta = jnp.sum(prod_t[64 * e:64 * (e + 1)], axis=0, keepdims=True)
                    ps.append(p.astype(BF16))
                    d_ss.append((p * (d_p2[rows] - delta)).astype(BF16))
                    p_sink = jnp.exp(sk_ref[0:1, h:h + 1] - lse_h)
                    dsk = jnp.where(lane1 == h, -jnp.sum(p_sink * delta), dsk)
                d_s2, p2 = jnp.concatenate(d_ss, axis=0), jnp.concatenate(ps, axis=0)
                d_k2 = jnp.dot(d_s2, qch, preferred_element_type=F32)
                d_v2 = jnp.dot(p2, doch, preferred_element_type=F32)
                for e in range(2):
                    rows = slice(2 * CHUNK * e, 2 * CHUNK * (e + 1))
                    acc_k[2 * g + e] += d_k2[rows]
                    acc_v[2 * g + e] += d_v2[rows]
                dq_t = jnp.dot(kk2_t, d_s2, preferred_element_type=F32)
                dq_scr[:, csl] = (_rope(dq_t.T, c_ref[...], -s_ref[...]) * 0.125).astype(BF16)
            dsk_ref[...] += dsk

        @pl.when(n > 0)
        def _():
            for m2 in range(2):
                halves = []
                for g in (2 * m2, 2 * m2 + 1):
                    for acc in (acc_k, acc_v):
                        comb = jnp.where(lo, acc[2 * g, 0:CHUNK], acc[2 * g + 1, 0:CHUNK])
                        halves.append(comb + pltpu.roll(comb, 64, axis=1))
                d_kr = jnp.where(lo, halves[0], halves[2])
                d_v = jnp.where(lo, halves[1], halves[3])
                dqkv_ref[:, OFF_K + m2 * 128:OFF_K + (m2 + 1) * 128] = _rope(d_kr, cp_ref[...], -sp_ref[...]).astype(BF16)
                dqkv_ref[:, OFF_V + m2 * 128:OFF_V + (m2 + 1) * 128] = d_v.astype(BF16)

    qn = lambda n: jnp.minimum(n, nb - 1)
    pn = lambda n: jnp.maximum(jnp.minimum(n, nb) - 1, 0)
    cur = pl.BlockSpec((CHUNK, 1024), lambda n: (qn(n), 0))
    prev = pl.BlockSpec((CHUNK, 1024), lambda n: (pn(n), 0))
    cur128 = pl.BlockSpec((CHUNK, 128), lambda n: (qn(n), 0))
    prev128 = pl.BlockSpec((CHUNK, 128), lambda n: (pn(n), 0))
    cur_t = pl.BlockSpec((1024, CHUNK), lambda n: (0, qn(n)))
    prev_t = pl.BlockSpec((1024, CHUNK), lambda n: (0, pn(n)))
    one = pl.BlockSpec((1, 128), lambda n: (0, 0))
    return pl.pallas_call(
        body, name="attn_bwd", grid=(nb + 1,),
        in_specs=[cur, cur, prev, cur, prev, cur_t, prev_t, cur, cur, pl.BlockSpec((1, 16, CHUNK), lambda n: (qn(n), 0, 0)),
                  one, cur128, cur128, prev128, prev128, pl.BlockSpec(memory_space=pl.ANY)] + r_in_specs,
        out_specs=[pl.BlockSpec((CHUNK, 1536), lambda n: (pn(n), 0)), one] + r_out_specs,
        out_shape=[SDS((t, PROJ_W), BF16), SDS((1, 128), F32)] + r_outs,
        scratch_shapes=[pltpu.VMEM((8, 2 * CHUNK, 128), F32), pltpu.VMEM((8, 2 * CHUNK, 128), F32),
                        pltpu.VMEM((CHUNK, 1024), BF16)] + r_scratch,
        input_output_aliases={15: 0},
        compiler_params=_cp(dimension_semantics=("arbitrary",)),
    )(qr, kp, kp, vp, vp, kt, kt, d_o, o, lse, sinks, cos_t, sin_t, cos_t, sin_t, d_proj, *r_ops)


def _adamw(name, w, g, m, v, *, tr):
    rows, cols = w.shape
    tr = min(tr, rows)
    assert rows % tr == 0

    def body(w_ref, g_ref, m_ref, v_ref, d_ref, nm_ref, nv_ref):
        gv = g_ref[...]
        nm = ADAM_B1 * m_ref[...] + (1.0 - ADAM_B1) * gv
        nv = ADAM_B2 * v_ref[...] + (1.0 - ADAM_B2) * (gv * gv)
        m_hat = nm / (1.0 - ADAM_B1 ** ADAM_STEP)
        v_hat = nv / (1.0 - ADAM_B2 ** ADAM_STEP)
        d_ref[...] = -ADAM_LR * (m_hat / (jnp.sqrt(v_hat) + ADAM_EPS) + ADAM_WD * w_ref[...])
        nm_ref[...] = nm
        nv_ref[...] = nv

    tile = pl.BlockSpec((tr, cols), lambda i: (i, 0))
    return pl.pallas_call(
        body, name=name, grid=(rows // tr,), in_specs=[tile] * 4, out_specs=[tile] * 3,
        out_shape=[SDS((rows, cols), F32)] * 3, compiler_params=_cp(dimension_semantics=("arbitrary",)),
    )(w, g, m, v)


def _local_step(x, cos_t, sin_t, tgt, wb, ps, late=None, rides=None):
    t = x.shape[0]
    tm = min(512, t)
    tmw = min(1024, t)
    ij = lambda i, j, k: (i, j)
    i0 = lambda i, j, k: (i, 0)
    c0 = lambda i, j, k: (0, 0)
    cj = lambda i, j, k: (0, j)
    rides = rides or (lambda group, grads: None)
    rode = {}

    tkt = min(2048, t)
    proj, u, *arrived = _norm_mm("in_proj", x, ps['norm_mix_pre_w'], wb['cat'], tm=tmw, tn=1024,
                                 ride=late[0] if late else None)
    if late:
        more_wb, more_ps = late[1](arrived)
        wb, ps = {**wb, **more_wb}, {**ps, **more_ps}
    xc, xc_pre = _conv_silu_fwd(proj, ps['ssd_conv_w'], ps['ssd_conv_b'], tm=tm)
    bias_pad = jnp.pad(ps['ssd_dt_bias'], ((0, 0), (0, 96)))
    dtg, dtg_t = _dt_fwd(proj, bias_pad, tm=tmw)
    alog = jnp.pad(ps['ssd_a_log'].reshape(SSD_N_GROUPS, 1, 8), ((0, 0), (0, 0), (0, 120)))
    alog_t = ps['ssd_a_log'].reshape(SSD_N_GROUPS, 8, 1)
    d_exp = jnp.repeat(ps['ssd_d'], SSD_HEAD_DIM, axis=1)
    y, hs = _ssd_fwd(xc, dtg, dtg_t, alog, alog_t, d_exp)
    gn = _gated_norm_fwd(y, proj, ps['ssd_norm_w'], tm=tmw)
    qr, kp, vp, kt, vt = _rope_qkv(proj, cos_t, sin_t, tm=tm)
    sinks = jnp.pad(ps['attn_sinks'], ((0, 0), (0, 112)))
    ao, lse = _attn_fwd(qr, kp, vt, sinks)
    y_attn = _mm_plain("attn_out", ao, wb['ao'], tm=tmw, tn=512, tk=1024)

    def merge_ep(acc, i, j, ins, outs):
        gs, ga, ya = ins
        outs[0][...] = (_sigmoid(gs[...]) * acc + _sigmoid(ga[...]) * ya[...]).astype(BF16)
        outs[1][...] = acc

    merged, y_ssd = _mm_call(
        "ssd_out_merge", gn, wb['so'], tm=tmw, tn=512, tk=2048, epilogue=merge_ep,
        extra_in=[(proj, (tmw, 512), lambda i, j, k: (i, OFF_GS // 512 + j)),
                  (proj, (tmw, 512), lambda i, j, k: (i, OFF_GA // 512 + j)), (y_attn, (tmw, 512), ij)],
        outs=[((t, D_MODEL), BF16, (tmw, 512), ij), ((t, D_MODEL), F32, (tmw, 512), ij)])

    def mix_ep(acc, i, j, ins, outs):
        xv, wn = ins
        r = lax.rsqrt(jnp.mean(acc * acc, axis=-1, keepdims=True) + NORM_EPS)
        outs[0][...] = xv[...] + acc * r * wn[...]
        outs[1][...] = acc

    x1, mmix = _mm_call(
        "mix_out", merged, wb['mix'], tm=tm, tn=D_MODEL, tk=1024, epilogue=mix_ep,
        extra_in=[(x, (tm, D_MODEL), i0), (ps['norm_mix_post_w'], (1, D_MODEL), c0)],
        outs=[((t, D_MODEL), F32, (tm, D_MODEL), i0), ((t, D_MODEL), F32, (tm, D_MODEL), i0)])

    up_raw, h = _norm_mm("ffn_up", x1, ps['norm_ffn_pre_w'], wb['up'], tm=tmw, tn=1408)
    act, ffn_gate, ffn_val = _ffn_act_fwd(up_raw, ps['ffn_conv_w'], ps['ffn_conv_b'], tm=tm)

    def loss_ep(acc, i, j, ins, outs):
        x1v, tg, wn = ins
        d_ff_ref, dout_ref, loss_ref, dw_ref = outs
        wv = wn[...]
        r = lax.rsqrt(jnp.mean(acc * acc, axis=-1, keepdims=True) + NORM_EPS)
        err = x1v[...] + acc * r * wv - tg[...]
        dout = err * (1.0 / D_MODEL)
        dout_ref[...] = dout
        d_ff, dw = _rms_bwd(acc, wv, dout)
        d_ff_ref[...] = d_ff.astype(BF16)
        _accumulate(dw_ref, i == 0, dw)
        _accumulate(loss_ref, i == 0, jnp.sum(err * err, keepdims=True) * (0.5 / D_MODEL))

    d_ff, dout, loss, g_norm_ffn_post = _mm_call(
        "ffn_down_loss", act, wb['dn'], tm=tm, tn=D_MODEL, tk=FFN_D_FF, epilogue=loss_ep,
        extra_in=[(x1, (tm, D_MODEL), i0), (tgt, (tm, D_MODEL), i0), (ps['norm_ffn_post_w'], (1, D_MODEL), c0)],
        outs=[((t, D_MODEL), BF16, (tm, D_MODEL), i0), ((t, D_MODEL), F32, (tm, D_MODEL), i0),
              ((1, 1), F32, (1, 1), c0), ((1, D_MODEL), F32, (1, D_MODEL), c0)])

    d_act = _mm_plain("d_act", d_ff, wb['dn_t'], tm=tmw, tn=1408, tk=1024, out_dtype=BF16)
    g_w_down = _mm_plain("g_w_down", act, d_ff, tm=1408, tn=1024, tk=tkt, trans_a=True, out_dtype=BF16)
    d_gate, d_val, db_g, db_v = _ffn_act_bwd(ffn_gate, ffn_val, d_act, tm=tm)
    d_up_raw, gcw_g = _conv_bwd2("ffn_conv_bwd2_gate", d_gate, up_raw, 0, ps['ffn_conv_w'][:, :FFN_D_FF], tm=tm,
                                 tc=1408, out_cols=2 * FFN_D_FF, out_col0=0)
    d_up_raw, gcw_v = _conv_bwd2("ffn_conv_bwd2_val", d_val, up_raw, FFN_D_FF, ps['ffn_conv_w'][:, FFN_D_FF:], tm=tm,
                                 tc=1408, out_cols=2 * FFN_D_FF, out_col0=FFN_D_FF, fill=d_up_raw)
    g_ffn_conv_w = jnp.concatenate([gcw_g, gcw_v], axis=1)

    def dx1_ep(acc, i, j, ins, outs):
        x1v, wpre, dout_v, mmv, wpost = ins
        d_x1_ref, d_mm_ref, dwpre_ref, dwpost_ref = outs
        d_n, dw_pre = _rms_bwd(x1v[...], wpre[...], acc)
        d_x1 = dout_v[...] + d_n
        d_x1_ref[...] = d_x1
        d_mm, dw_post = _rms_bwd(mmv[...], wpost[...], d_x1)
        d_mm_ref[...] = d_mm.astype(BF16)
        _accumulate(dwpre_ref, i == 0, dw_pre)
        _accumulate(dwpost_ref, i == 0, dw_post)

    d_x1, d_mm, g_norm_ffn_pre, g_norm_mix_post = _mm_call(
        "d_h", d_up_raw, wb['up_t'], tm=tm, tn=D_MODEL, tk=FFN_D_FF, epilogue=dx1_ep,
        extra_in=[(x1, (tm, D_MODEL), i0), (ps['norm_ffn_pre_w'], (1, D_MODEL), c0), (dout, (tm, D_MODEL), i0),
                  (mmix, (tm, D_MODEL), i0), (ps['norm_mix_post_w'], (1, D_MODEL), c0)],
        outs=[((t, D_MODEL), F32, (tm, D_MODEL), i0), ((t, D_MODEL), BF16, (tm, D_MODEL), i0),
              ((1, D_MODEL), F32, (1, D_MODEL), c0), ((1, D_MODEL), F32, (1, D_MODEL), c0)])
    g_w_up_t = _mm_plain("g_w_up", d_up_raw, h, tm=1408, tn=1024, tk=tkt, trans_a=True, out_dtype=BF16)
    ride_ffn = rides('ffn', {'ffn_w_up': g_w_up_t, 'ffn_w_down': g_w_down})

    def dmerge_ep(acc, i, j, ins, outs):
        gs, ga, ys, ya = ins
        sg_s, sg_a = _sigmoid(gs[...]), _sigmoid(ga[...])
        outs[0][...] = (acc * sg_s).astype(BF16)
        outs[1][...] = (acc * sg_a).astype(BF16)
        outs[2][:, 0:D_MODEL] = (acc * ys[...] * sg_s * (1.0 - sg_s)).astype(BF16)
        outs[2][:, D_MODEL:2 * D_MODEL] = (acc * ya[...] * sg_a * (1.0 - sg_a)).astype(BF16)

    d_yssd, d_yattn, d_proj = _mm_call(
        "d_merged", d_mm, wb['mix_t'], tm=tm, tn=D_MODEL, tk=1024, epilogue=dmerge_ep,
        extra_in=[(proj, (tm, D_MODEL), lambda i, j, k: (i, OFF_GS // D_MODEL)),
                  (proj, (tm, D_MODEL), lambda i, j, k: (i, OFF_GA // D_MODEL)), (y_ssd, (tm, D_MODEL), i0), (y_attn, (tm, D_MODEL), i0)],
        outs=[((t, D_MODEL), BF16, (tm, D_MODEL), i0), ((t, D_MODEL), BF16, (tm, D_MODEL), i0),
              ((t, PROJ_W), BF16, (tm, 2 * D_MODEL), lambda i, j, k: (i, OFF_GS // (2 * D_MODEL)))])
    g_w_mix = _mm_plain("g_w_mix", merged, d_mm, tm=1024, tn=1024, tk=tkt, trans_a=True, out_dtype=BF16)

    def dgn_ep(acc, i, j, ins, outs):
        yv, zv, wn = ins
        d_y_ref, d_z_ref, dw_ref = outs
        zz = zv[...]
        sz = _sigmoid(zz)
        silu = zz * sz
        gv = yv[...] * silu
        r = lax.rsqrt(jnp.mean(gv * gv, axis=-1, keepdims=True) + NORM_EPS)
        gh = gv * r
        dgh = acc * wn[...]
        dg = r * (dgh - gh * jnp.mean(dgh * gh, axis=-1, keepdims=True))
        d_y_ref[...] = dg * silu
        d_z_ref[...] = (dg * yv[...] * (sz * (1.0 + zz * (1.0 - sz)))).astype(BF16)
        dw = jnp.sum(acc * gh, axis=0, keepdims=True)

        @pl.when(i == 0)
        def _():
            dw_ref[j] = dw

        @pl.when(i > 0)
        def _():
            dw_ref[j] += dw

    d_y, d_proj, g_ssd_norm = _mm_call(
        "d_gn", d_yssd, wb['so_t'], tm=tmw, tn=512, tk=1024, epilogue=dgn_ep, fill=(d_proj, 1),
        extra_in=[(y, (tmw, 512), ij), (proj, (tmw, 512), lambda i, j, k: (i, OFF_Z // 512 + j)), (ps['ssd_norm_w'], (1, 512), cj)],
        outs=[((t, SSD_D_INNER), F32, (tmw, 512), ij), ((t, PROJ_W), BF16, (tmw, 512), lambda i, j, k: (i, OFF_Z // 512 + j)),
              ((SSD_N_GROUPS, 1, 512), F32, (SSD_N_GROUPS, 1, 512), lambda i, j, k: (0, 0, 0))])
    g_ssd_norm = g_ssd_norm.reshape(1, SSD_D_INNER)
    g_w_so = _mm_plain("g_w_so", gn, d_yssd, tm=1024, tn=1024, tk=tkt, trans_a=True, out_dtype=BF16)
    d_xc, d_dtg, d_alog, d_dd, *rode['ffn'] = _ssd_bwd(xc, dtg, dtg_t, alog, alog_t, d_exp, d_y, hs, ride=ride_ffn)
    d_pre, g_ssd_conv_b = _conv_silu_bwd1(d_xc, xc_pre, tm=tm)
    d_proj, g_ssd_conv_w = _conv_bwd2("ssd_conv_bwd2", d_pre, proj, OFF_XBC, ps['ssd_conv_w'], tm=tm, tc=1536,
                                      out_cols=PROJ_W, out_col0=OFF_XBC, fill=d_proj)
    d_proj, g_dt_bias = _dt_bwd(d_dtg, proj, bias_pad, d_proj, tm=tmw)

    d_ao = _mm_plain("d_ao", d_yattn, wb['ao_t'], tm=tmw, tn=512, tk=1024, out_dtype=BF16)
    g_w_ao = _mm_plain("g_w_ao", ao, d_yattn, tm=1024, tn=1024, tk=tkt, trans_a=True, out_dtype=BF16)
    ride_mix = rides('mix', {'ssd_w_out': g_w_so, 'attn_w_out': g_w_ao, 'w_mix_out': g_w_mix})
    d_proj, g_sinks, *rode['mix'] = _attn_bwd(qr, kp, vp, kt, d_ao, ao, lse, sinks, cos_t, sin_t, d_proj, ride=ride_mix)

    def dx_ep(acc, i, j, ins, outs):
        xv, wn, dx1v = ins
        d_n, dw = _rms_bwd(xv[...], wn[...], acc)
        outs[0][...] = dx1v[...] + d_n
        _accumulate(outs[1], i == 0, dw)

    g_cat_t = _mm_plain("g_w_in", d_proj, u, tm=1024, tn=1024, tk=tkt, trans_a=True, out_dtype=BF16)
    grad_x, g_norm_mix_pre, *rode['w_in'] = _mm_call(
        "d_u", d_proj, wb['cat_t'], tm=tm, tn=D_MODEL, tk=2304, epilogue=dx_ep, ride=rides('w_in', {'w_in': g_cat_t}),
        extra_in=[(x, (tm, D_MODEL), i0), (ps['norm_mix_pre_w'], (1, D_MODEL), c0), (d_x1, (tm, D_MODEL), i0)],
        outs=[((t, D_MODEL), F32, (tm, D_MODEL), i0), ((1, D_MODEL), F32, (1, D_MODEL), c0)])

    grads = {
        'norm_mix_pre_w': g_norm_mix_pre, 'w_in': g_cat_t, 'ssd_conv_w': g_ssd_conv_w, 'ssd_conv_b': g_ssd_conv_b,
        'ssd_dt_bias': g_dt_bias[:, :SSD_N_HEADS], 'ssd_a_log': d_alog[:, 0, :8].reshape(1, SSD_N_HEADS),
        'ssd_d': d_dd[:, 0, :8].reshape(1, SSD_N_HEADS), 'ssd_norm_w': g_ssd_norm, 'ssd_w_out': g_w_so,
        'attn_sinks': g_sinks[:, :ATTN_N_HEADS], 'attn_w_out': g_w_ao, 'w_mix_out': g_w_mix,
        'norm_mix_post_w': g_norm_mix_post, 'norm_ffn_pre_w': g_norm_ffn_pre, 'ffn_w_up': g_w_up_t,
        'ffn_conv_w': g_ffn_conv_w, 'ffn_conv_b': jnp.concatenate([db_g, db_v], axis=1), 'ffn_w_down': g_w_down,
        'norm_ffn_post_w': g_norm_ffn_post,
    }
    return loss, grad_x, grads, rode


def _group_channels(a):
    parts = []
    for g in range(SSD_N_GROUPS):
        parts += [a[..., 512 * g:512 * (g + 1)], a[..., 2048 + 128 * g:2048 + 128 * (g + 1)],
                  a[..., 2560 + 128 * g:2560 + 128 * (g + 1)]]
    return jnp.concatenate(parts, axis=-1)


def _ungroup_channels(a):
    xs = [a[..., GROUP_W * g:GROUP_W * g + 512] for g in range(SSD_N_GROUPS)]
    bs = [a[..., GROUP_W * g + 512:GROUP_W * g + 640] for g in range(SSD_N_GROUPS)]
    cs = [a[..., GROUP_W * g + 640:GROUP_W * (g + 1)] for g in range(SSD_N_GROUPS)]
    return jnp.concatenate(xs + bs + cs, axis=-1)


def _proj_rows(a_t, lo, hi):
    out = []
    for start, length, dst in sorted(PROJ_SEGS):
        s, e = max(lo, start), min(hi, start + length)
        if s < e:
            out.append(a_t[dst + s - start:dst + e - start])
    return out


def _to_proj_layout(w_in_t):
    pieces, pos = [], 0
    for start, length, dst in sorted(PROJ_SEGS, key=lambda s: s[2]):
        if dst > pos:
            pieces.append(jnp.zeros((dst - pos, w_in_t.shape[1]), w_in_t.dtype))
        pieces.append(w_in_t[start:start + length])
        pos = dst + length
    if pos < PROJ_W:
        pieces.append(jnp.zeros((PROJ_W - pos, w_in_t.shape[1]), w_in_t.dtype))
    return jnp.concatenate(pieces, axis=0)


def _rope_tables(positions):
    half = 32
    inv_freq = ROPE_THETA ** (-jnp.arange(half, dtype=F32) * 2.0 / 64)
    ang = positions.astype(F32)[:, None] * inv_freq
    cos, sin = jnp.cos(ang), jnp.sin(ang)
    return jnp.concatenate([cos, cos, cos, cos], axis=1), jnp.concatenate([-sin, sin, -sin, sin], axis=1)


def _matmul_weights(w_in_t):
    cat_t = _to_proj_layout(w_in_t)
    return {'cat': cat_t.T, 'cat_t': cat_t}


def _late_weights(so, ao, mix, up_t, dn):
    return {'so': so, 'so_t': so.T, 'ao': ao, 'ao_t': ao.T, 'mix': mix, 'mix_t': mix.T,
            'up': up_t.T, 'up_t': up_t, 'dn': dn, 'dn_t': dn.T}


ANY = pl.BlockSpec(memory_space=pl.ANY)
MESH = pl.DeviceIdType.MESH
ROW_ALIGN = 32


def _mesh_pos():
    return lax.axis_index("x"), lax.axis_index("y"), lax.axis_index("c")


def _other_chips(x, y):
    return [(1 - x, y), (x, 1 - y), (1 - x, 1 - y)]


def _remote(src, dst, send_sems, recv_sems, k, to):
    return pltpu.make_async_remote_copy(src_ref=src, dst_ref=dst, send_sem=send_sems.at[k], recv_sem=recv_sems.at[k],
                                        device_id=to, device_id_type=MESH)


def _half(c, rh):
    return pl.ds(pl.multiple_of(c * rh, 16), rh)


def _ag_ride(shard):
    r = shard.shape[0]
    rh = r // 2

    def first_copies(w_ref, out_ref, send_sems, recv_sems):
        x, y, c = _mesh_pos()
        p = 2 * x + y
        mine = _half(c, rh)
        cps = [_remote(w_ref, out_ref.at[p], send_sems, recv_sems, 6, (x, y, 1 - c))]
        return cps + [_remote(w_ref.at[mine], out_ref.at[p, mine], send_sems, recv_sems, j, (cx, cy, c))
                      for j, (cx, cy) in enumerate(_other_chips(x, y))]

    def start(ins, outs, send_sems, recv_sems):
        for cp in first_copies(ins[0], outs[0], send_sems, recv_sems):
            cp.start()

    def forwards(out_ref, send_sems, recv_sems, half):
        x, y, c = _mesh_pos()
        return [_remote(out_ref.at[2 * cx + cy, half], out_ref.at[2 * cx + cy, half], send_sems, recv_sems, 3 + j, (x, y, 1 - c))
                for j, (cx, cy) in enumerate(_other_chips(x, y))]

    def middle(ins, outs, send_sems, recv_sems):
        x, y, c = _mesh_pos()
        mine = _half(c, rh)
        for j, (fwd, (cx, cy)) in enumerate(zip(forwards(outs[0], send_sems, recv_sems, mine), _other_chips(x, y))):
            slab = outs[0].at[2 * cx + cy, mine]
            _remote(slab, slab, send_sems, recv_sems, j, (x, y, 1 - c)).wait_recv()
            fwd.start()

    def finish(ins, outs, send_sems, recv_sems):
        w_ref, out_ref = ins[0], outs[0]
        x, y, c = _mesh_pos()
        for cp in forwards(out_ref, send_sems, recv_sems, _half(1 - c, rh)):
            cp.wait_recv()
        _remote(w_ref, out_ref.at[2 * x + y], send_sems, recv_sems, 6, (x, y, 1 - c)).wait_recv()
        for cp in first_copies(w_ref, out_ref, send_sems, recv_sems) + forwards(out_ref, send_sems, recv_sems, _half(c, rh)):
            cp.wait_send()

    return _Ride((shard,), (SDS((N_CHIPS, r, COMM_LANES), shard.dtype),), 7, start, finish, middle)


def _rs_ride(gbuf):
    rh = gbuf.shape[1] // 2

    def copies(g_ref, r_ref, send_sems, recv_sems, landing):
        x, y, c = _mesh_pos()
        cps = []
        for k, (cx, cy) in enumerate(_other_chips(x, y)):
            for h in range(2):
                slot = 2 * k + c if landing else 2 * k + h
                cps.append(pltpu.make_async_remote_copy(
                    src_ref=g_ref.at[2 * cx + cy, pl.ds(h * rh, rh)], dst_ref=r_ref.at[slot],
                    send_sem=send_sems.at[2 * k + h], recv_sem=recv_sems.at[slot],
                    device_id=(cx, cy, h), device_id_type=MESH))
        cps.append(_remote(g_ref.at[2 * x + y, _half(1 - c, rh)], r_ref.at[6], send_sems, recv_sems, 6, (x, y, 1 - c)))
        return cps

    def start(ins, outs, send_sems, recv_sems):
        for cp in copies(ins[0], outs[0], send_sems, recv_sems, True):
            cp.start()

    def finish(ins, outs, send_sems, recv_sems):
        for cp in copies(ins[0], outs[0], send_sems, recv_sems, False):
            cp.wait()

    return _Ride((gbuf,), (SDS((7, rh, COMM_LANES), gbuf.dtype),), 7, start, finish)


def _rs_sum(name, gbuf, got, pc_idx):
    rh = got.shape[1]
    tr = max(d for d in range(16, 513, 16) if rh % d == 0)
    nb = rh // tr

    def body(pc_ref, own_ref, *refs):
        o_ref = refs[7]
        p, c = pc_ref[0], pc_ref[1]
        own = own_ref[0].astype(F32)
        slots = [r[0].astype(F32) for r in refs[:7]]

        def term(q, h):
            code = p ^ q
            far = jnp.where(code == 2, slots[h], jnp.where(code == 1, slots[2 + h], slots[4 + h]))
            return jnp.where(code == 0, jnp.where(c == h, own, slots[6]), far)

        acc = term(0, 0)
        for q, h in [(0, 1), (1, 0), (1, 1), (2, 0), (2, 1), (3, 0), (3, 1)]:
            acc = acc + term(q, h)
        o_ref[0] = acc

    slot = lambda s: pl.BlockSpec((1, tr, COMM_LANES), lambda i, pc: (s, i, 0))
    return pl.pallas_call(
        body, name=name,
        grid_spec=pltpu.PrefetchScalarGridSpec(
            num_scalar_prefetch=1, grid=(nb,),
            in_specs=[pl.BlockSpec((1, tr, COMM_LANES), lambda i, pc: (pc[0], pc[1] * nb + i, 0))] + [slot(s) for s in range(7)],
            out_specs=pl.BlockSpec((1, tr, COMM_LANES), lambda i, pc: (pc[1], i, 0))),
        out_shape=SDS((2, rh, COMM_LANES), F32), compiler_params=_cp(dimension_semantics=("arbitrary",)),
    )(pc_idx, gbuf, *([got] * 7))


def _pair_gather_all(bufs):
    n = len(bufs)

    def body(*refs):
        outs, send_sems, recv_sems = refs[n:2 * n], refs[2 * n], refs[2 * n + 1]
        x, y, c = _mesh_pos()
        cps = [_remote(o.at[c], o.at[c], send_sems, recv_sems, k, (x, y, 1 - c)) for k, o in enumerate(outs)]
        for cp in cps:
            cp.start()
        for k, o in enumerate(outs):
            _remote(o.at[1 - c], o.at[1 - c], send_sems, recv_sems, k, (x, y, 1 - c)).wait_recv()
        for cp in cps:
            cp.wait_send()

    return pl.pallas_call(
        body, name="grad_pair_gather", in_specs=[ANY] * n, out_specs=[ANY] * n,
        out_shape=[SDS(b.shape, b.dtype) for b in bufs],
        scratch_shapes=[pltpu.SemaphoreType.DMA((n,)), pltpu.SemaphoreType.DMA((n,))],
        input_output_aliases={k: k for k in range(n)},
    )(*bufs)


def _pack_rows(big, small=()):
    parts = list(big)
    if small:
        flat = jnp.concatenate([p.reshape(-1) for p in small])
        k = -(-flat.shape[0] // (16 * COMM_LANES)) * 16
        parts.append(jnp.pad(flat, (0, k * COMM_LANES - flat.shape[0])).reshape(k, COMM_LANES))
    pad = -sum(p.shape[0] for p in parts) % ROW_ALIGN
    if pad:
        parts.append(jnp.zeros((pad, COMM_LANES), parts[0].dtype))
    return jnp.concatenate(parts, axis=0) if len(parts) > 1 else parts[0]


def _take(flat, off, shape):
    n = 1
    for d in shape:
        n *= d
    return flat[off:off + n].reshape(shape), off + n


BIG_ROWS = {'w_in': 2184, 'ssd_w_out': 512, 'attn_w_out': 256, 'w_mix_out': 256, 'ffn_w_up': 1408, 'ffn_w_down': 704}
TRANSPOSED = ('w_in', 'ffn_w_up')
LATE = ('ssd_w_out', 'attn_w_out', 'w_mix_out', 'ffn_w_up', 'ffn_w_down')
CONV_TAPS = ('ssd_conv_w', 'ffn_conv_w')
RS_GROUPS = {'ffn': ('ffn_w_up', 'ffn_w_down'), 'mix': ('ssd_w_out', 'attn_w_out', 'w_mix_out'), 'w_in': ('w_in',)}


def _exchange(name, ride):
    n_in, n_out = len(ride.ins), len(ride.outs)

    def body(*refs):
        ins, outs, sems = refs[:n_in], refs[n_in:n_in + n_out], refs[n_in + n_out:]
        ride.start(ins, outs, *sems)
        if ride.middle is not None:
            ride.middle(ins, outs, *sems)
        ride.finish(ins, outs, *sems)

    return pl.pallas_call(
        body, name=name, in_specs=[ANY] * n_in, out_specs=[ANY] * n_out, out_shape=list(ride.outs),
        scratch_shapes=[pltpu.SemaphoreType.DMA((ride.n_sems,)), pltpu.SemaphoreType.DMA((ride.n_sems,))],
    )(*ride.ins)


def kernel(x, positions, norm_mix_pre_w, w_in, ssd_conv_w, ssd_conv_b, ssd_dt_bias, ssd_a_log, ssd_d, ssd_norm_w, ssd_w_out, attn_sinks, attn_w_out, w_mix_out, norm_mix_post_w, norm_ffn_pre_w, ffn_w_up, ffn_conv_w, ffn_conv_b, ffn_w_down, norm_ffn_post_w, loss_target, m_norm_mix_pre_w, m_w_in, m_ssd_conv_w, m_ssd_conv_b, m_ssd_dt_bias, m_ssd_a_log, m_ssd_d, m_ssd_norm_w, m_ssd_w_out, m_attn_sinks, m_attn_w_out, m_w_mix_out, m_norm_mix_post_w, m_norm_ffn_pre_w, m_ffn_w_up, m_ffn_conv_w, m_ffn_conv_b, m_ffn_w_down, m_norm_ffn_post_w, v_norm_mix_pre_w, v_w_in, v_ssd_conv_w, v_ssd_conv_b, v_ssd_dt_bias, v_ssd_a_log, v_ssd_d, v_ssd_norm_w, v_ssd_w_out, v_attn_sinks, v_attn_w_out, v_w_mix_out, v_norm_mix_post_w, v_norm_ffn_pre_w, v_ffn_w_up, v_ffn_conv_w, v_ffn_conv_b, v_ffn_w_down, v_norm_ffn_post_w):
    given = dict(locals())
    w = {n: given[n][0] for n in WEIGHTS}
    w = {n: (a if a.ndim == 2 else a[None]) for n, a in w.items()}
    mom_m = {n: given['m_' + n].reshape(w[n].shape) for n in WEIGHTS}
    mom_v = {n: given['v_' + n].reshape(w[n].shape) for n in WEIGHTS}
    cx, cy, cc = _mesh_pos()
    pc_idx = jnp.stack([2 * cx + cy, cc]).astype(jnp.int32)

    rows_of = lambda n: (w[n].T if n in TRANSPOSED else w[n]).astype(BF16)
    gathered = _exchange("w_in_all_gather", _ag_ride(_pack_rows([rows_of('w_in')])))[0]
    wb = _matmul_weights(jnp.concatenate([gathered[s, :BIG_ROWS['w_in']] for s in range(N_CHIPS)], axis=0))
    taps = [lax.bitcast_convert_type(w[n], BF16) for n in CONV_TAPS]

    def unpack_late(arrived):
        rows, conv = {n: [] for n in LATE}, {n: [] for n in CONV_TAPS}
        for s in range(N_CHIPS):
            r0 = 0
            for n in LATE:
                rows[n].append(arrived[0][s, r0:r0 + BIG_ROWS[n]])
                r0 += BIG_ROWS[n]
            flat, off = arrived[0][s, r0:r0 + 16].reshape(-1), 0
            for n in CONV_TAPS:
                a, off = _take(flat, off, w[n].shape + (2,))
                conv[n].append(lax.bitcast_convert_type(a, F32))
        full = {n: jnp.concatenate(rows[n], axis=0) for n in LATE}
        return (_late_weights(*[full[n] for n in LATE]),
                {'ssd_conv_w': _group_channels(jnp.concatenate(conv['ssd_conv_w'], axis=1)),
                 'ffn_conv_w': jnp.concatenate(conv['ffn_conv_w'], axis=1)})

    late = (_ag_ride(_pack_rows([rows_of(n) for n in LATE], taps)), unpack_late)

    sent = {}

    def rides(group, g):
        parts = []
        for s in range(N_CHIPS):
            slab = []
            for n in RS_GROUPS[group]:
                lo, hi = BIG_ROWS[n] * s, BIG_ROWS[n] * (s + 1)
                slab += _proj_rows(g[n], lo, hi) if n == 'w_in' else [g[n][lo:hi]]
            slab = [a.astype(BF16) for a in slab]
            pad = -sum(a.shape[0] for a in slab) % ROW_ALIGN
            parts += slab + ([jnp.zeros((pad, COMM_LANES), BF16)] if pad else [])
        sent[group] = jnp.concatenate(parts, axis=0).reshape(N_CHIPS, -1, COMM_LANES)
        return _rs_ride(sent[group])

    ps = {n: w[n] for n in REPLICATED}
    ps['ssd_conv_b'] = _group_channels(w['ssd_conv_b'])
    cos_t, sin_t = _rope_tables(positions[0])
    loss, grad_x, grads, rode = _local_step(x[0], cos_t, sin_t, loss_target[0], wb, ps, late, rides)
    grads['ssd_conv_w'] = _ungroup_channels(grads['ssd_conv_w'])
    grads['ssd_conv_b'] = _ungroup_channels(grads['ssd_conv_b'])

    shard_cols = {n: sh[1] for n, _, sh in SHARDED}
    parts = []
    for s in range(N_CHIPS):
        small = [grads[n][:, shard_cols[n] * s:shard_cols[n] * (s + 1)] for n in CONV_TAPS] + [grads[n] for n in REPLICATED]
        flat = _pack_rows([], small)
        high = flat.astype(BF16)
        parts += [high, (flat - high.astype(F32)).astype(BF16)]
    sent['small'] = jnp.concatenate(parts, axis=0).reshape(N_CHIPS, -1, COMM_LANES)
    rode['small'] = _exchange("grad_small_exchange", _rs_ride(sent['small']))

    groups = ('ffn', 'mix', 'w_in', 'small')
    red = _pair_gather_all([_rs_sum("grad_sum_" + g, sent[g], rode[g][0], pc_idx) for g in groups])
    red = {g: r.reshape(-1, COMM_LANES) for g, r in zip(groups, red)}
    g_red = {}
    for g in groups[:3]:
        r0 = 0
        for n in RS_GROUPS[g]:
            g_red[n] = red[g][r0:r0 + BIG_ROWS[n]].T if n in TRANSPOSED else red[g][r0:r0 + BIG_ROWS[n]]
            r0 += BIG_ROWS[n]
    half = red['small'].shape[0] // 2
    flat, off = (red['small'][:half] + red['small'][half:]).reshape(-1), 0
    for n in CONV_TAPS + REPLICATED:
        g_red[n], off = _take(flat, off, w[n].shape)

    small_names = [n for n in WEIGHTS if n not in MATMUL_WEIGHTS]
    delta, new_m, new_v = {}, {}, {}
    for n in MATMUL_WEIGHTS:
        delta[n], new_m[n], new_v[n] = _adamw("adamw_" + n, w[n], g_red[n], mom_m[n], mom_v[n],
                                                  tr=max(d for d in range(8, 353, 8) if w[n].shape[0] % d == 0))
    packed = [_pack_small([d[n] for n in small_names]) for d in (w, g_red, mom_m, mom_v)]
    outs = _adamw("adamw_small", *packed, tr=packed[0].shape[0])
    for res, o in zip((delta, new_m, new_v), outs):
        fl, off = o.reshape(-1), 0
        for n in small_names:
            res[n], off = _take(fl, off, w[n].shape)

    loss_all = lax.psum(loss[0, 0], ("x", "y", "c"))
    shaped = lambda d: [d[n].reshape(given[n].shape) for n in WEIGHTS]
    return (loss_all, grad_x[None], *shaped(g_red), *shaped(delta), *shaped(new_m), *shaped(new_v))


def _pack_small(pieces):
    flat = jnp.concatenate([p.reshape(-1) for p in pieces])
    rows = -(-flat.shape[0] // (128 * 8)) * 8
    return jnp.pad(flat, (0, rows * 128 - flat.shape[0])).reshape(rows, 128)
```

```python
from typing import Callable, NamedTuple

import jax
import jax.numpy as jnp
from jax import lax
from jax.experimental import pallas as pl
from jax.experimental.pallas import tpu as pltpu

F32 = jnp.float32
BF16 = jnp.bfloat16
SDS = jax.ShapeDtypeStruct
HIGHEST = lax.Precision.HIGHEST

D_MODEL = 1024
SSD_D_INNER = 2048
SSD_N_HEADS = 32
SSD_HEAD_DIM = 64
SSD_N_GROUPS = 4
SSD_HEADS_PER_GROUP = 8
SSD_D_STATE = 128
SSD_CONV_DIM = 3072
CHUNK = 128
ATTN_N_HEADS = 16
KV_WIDTH = 256
FFN_D_FF = 2816
IN_PROJ_DIM = 8736
ROPE_THETA = 10000.0
NORM_EPS = 1e-6
ADAM_LR, ADAM_B1, ADAM_B2, ADAM_EPS, ADAM_WD, ADAM_STEP = 0.001, 0.9, 0.999, 1e-08, 0.01, 10

PROJ_W = 9216
OFF_Q, OFF_K, OFF_V, OFF_Z, OFF_DT, OFF_GS, OFF_GA, OFF_XBC = 0, 1024, 1280, 1536, 3584, 4096, 5120, 6144
GROUP_W = 768
PROJ_SEGS = ([(0, 2048, OFF_Z)]
             + [(2048 + 512 * g, 512, OFF_XBC + GROUP_W * g) for g in range(4)]
             + [(4096 + 128 * g, 128, OFF_XBC + GROUP_W * g + 512) for g in range(4)]
             + [(4608 + 128 * g, 128, OFF_XBC + GROUP_W * g + 640) for g in range(4)]
             + [(5120, 32, OFF_DT), (5152, 1024, OFF_Q), (6176, 256, OFF_K), (6432, 256, OFF_V),
                (6688, 1024, OFF_GS), (7712, 1024, OFF_GA)])
VMEM_LIMIT_MB = 48
VMEM_BIG_MB = 58
NEG = -1e30

WEIGHTS = ('norm_mix_pre_w', 'w_in', 'ssd_conv_w', 'ssd_conv_b', 'ssd_dt_bias', 'ssd_a_log', 'ssd_d', 'ssd_norm_w',
           'ssd_w_out', 'attn_sinks', 'attn_w_out', 'w_mix_out', 'norm_mix_post_w', 'norm_ffn_pre_w', 'ffn_w_up',
           'ffn_conv_w', 'ffn_conv_b', 'ffn_w_down', 'norm_ffn_post_w')
SHARDED = (('w_in', 1, (1024, 2184)), ('ssd_conv_w', 1, (4, 768)), ('ssd_w_out', 0, (512, 1024)),
           ('attn_w_out', 0, (256, 1024)), ('w_mix_out', 0, (256, 1024)), ('ffn_w_up', 1, (1024, 1408)),
           ('ffn_conv_w', 1, (3, 1408)), ('ffn_w_down', 0, (704, 1024)))
MATMUL_WEIGHTS = ('w_in', 'ssd_w_out', 'attn_w_out', 'w_mix_out', 'ffn_w_up', 'ffn_w_down')
REPLICATED = tuple(n for n in WEIGHTS if n not in {s[0] for s in SHARDED})
N_CHIPS = 4
COMM_LANES = 1024


def _cp(vmem_mb=VMEM_LIMIT_MB, **kw):
    return pltpu.CompilerParams(vmem_limit_bytes=vmem_mb << 20, **kw)


class _Ride(NamedTuple):
    ins: tuple
    outs: tuple
    n_sems: int
    start: Callable
    finish: Callable
    middle: Callable = None


def _ride_parts(ride):
    if ride is None:
        return [], [], [], [], []
    hbm = pl.BlockSpec(memory_space=pl.ANY)
    return (list(ride.ins), [hbm] * len(ride.ins), list(ride.outs), [hbm] * len(ride.outs),
            [pltpu.SemaphoreType.DMA((ride.n_sems,)), pltpu.SemaphoreType.DMA((ride.n_sems,))])


def _ride_run(ride, first, last, in_refs, out_refs, sems, middle=None):
    if ride is None:
        return

    @pl.when(first)
    def _():
        ride.start(in_refs, out_refs, *sems)

    if ride.middle is not None:
        @pl.when(last if middle is None else middle)
        def _():
            ride.middle(in_refs, out_refs, *sems)

    @pl.when(last)
    def _():
        ride.finish(in_refs, out_refs, *sems)


def _iota(shape, axis):
    return lax.broadcasted_iota(jnp.int32, shape, axis)


def _sigmoid(v):
    return 1.0 / (1.0 + jnp.exp(-v))


def _mm_call(name, a, b, *, tm, tn, tk, epilogue, outs, extra_in=(), trans_a=False, fill=None, ride=None,
             vmem_mb=VMEM_LIMIT_MB):
    if trans_a:
        kdim, m = a.shape
    else:
        m, kdim = a.shape
    n = b.shape[1]
    assert b.shape[0] == kdim and m % tm == 0 and n % tn == 0 and kdim % tk == 0, (name, a.shape, b.shape, tm, tn, tk)
    gi, gj, gk = m // tm, n // tn, kdim // tk
    n_in, n_out = len(extra_in), len(outs)

    n_fill = 0 if fill is None else 1
    r_ops, r_in_specs, r_outs, r_out_specs, r_scratch = _ride_parts(ride)

    def body(a_ref, b_ref, *rest):
        ins = rest[:n_in]
        rest = rest[n_in + n_fill:]
        r_in, rest = rest[:len(r_ops)], rest[len(r_ops):]
        out_refs, rest = rest[:n_out], rest[n_out:]
        r_out, scratch = rest[:len(r_outs)], rest[len(r_outs):]
        i, j, k = pl.program_id(0), pl.program_id(1), pl.program_id(2)
        _ride_run(ride, (i == 0) & (j == 0) & (k == 0), (i == gi - 1) & (j == gj - 1) & (k == gk - 1),
                  r_in, r_out, scratch[-2:])
        av = a_ref[...].astype(BF16)
        bv = b_ref[...].astype(BF16)
        if trans_a:
            part = lax.dot_general(av, bv, (((0,), (0,)), ((), ())), preferred_element_type=F32)
        else:
            part = jnp.dot(av, bv, preferred_element_type=F32)
        if gk == 1:
            epilogue(part, i, j, ins, out_refs)
        else:
            acc = scratch[0]

            @pl.when(k == 0)
            def _():
                acc[...] = part

            @pl.when(k > 0)
            def _():
                acc[...] += part

            @pl.when(k == gk - 1)
            def _():
                epilogue(acc[...], i, j, ins, out_refs)

    a_spec = pl.BlockSpec((tk, tm), lambda i, j, k: (k, i)) if trans_a else pl.BlockSpec((tm, tk), lambda i, j, k: (i, k))
    if gj == 1 and gk == 1:
        b_spec = pl.BlockSpec((tk, tn), lambda i, j, k: (0, 0), pipeline_mode=pl.Buffered(1))
    else:
        b_spec = pl.BlockSpec((tk, tn), lambda i, j, k: (k, j))
    in_specs = [a_spec, b_spec]
    in_specs += [pl.BlockSpec(bs, im) for _, bs, im in extra_in]
    operands = [a, b] + [e[0] for e in extra_in]
    aliases = {}
    if fill is not None:
        in_specs.append(pl.BlockSpec(memory_space=pl.ANY))
        aliases = {len(operands): fill[1]}
        operands.append(fill[0])
    return pl.pallas_call(
        body, name=name, grid=(gi, gj, gk), in_specs=in_specs + r_in_specs,
        out_specs=[pl.BlockSpec(bs, im) for _, _, bs, im in outs] + r_out_specs,
        out_shape=[SDS(s, d) for s, d, _, _ in outs] + r_outs,
        scratch_shapes=([pltpu.VMEM((tm, tn), F32)] if gk > 1 else []) + r_scratch,
        input_output_aliases=aliases,
        compiler_params=_cp(vmem_mb, dimension_semantics=("arbitrary", "arbitrary", "arbitrary")),
    )(*operands, *r_ops)


def _mm_plain(name, a, b, *, tm, tn, tk, out_dtype=F32, trans_a=False):
    m = a.shape[1] if trans_a else a.shape[0]

    def epilogue(acc, i, j, ins, outs):
        outs[0][...] = acc.astype(out_dtype)

    return _mm_call(name, a, b, tm=tm, tn=tn, tk=tk, epilogue=epilogue, trans_a=trans_a,
                    outs=[((m, b.shape[1]), out_dtype, (tm, tn), lambda i, j, k: (i, j))])[0]


def _accumulate(ref, first, value):
    @pl.when(first)
    def _():
        ref[...] = value

    @pl.when(jnp.logical_not(first))
    def _():
        ref[...] += value


def _rms_bwd(xv, w, dy):
    r = lax.rsqrt(jnp.mean(xv * xv, axis=-1, keepdims=True) + NORM_EPS)
    xn = xv * r
    dxh = dy * w
    dx = r * (dxh - xn * jnp.mean(dxh * xn, axis=-1, keepdims=True))
    return dx, jnp.sum(dy * xn, axis=0, keepdims=True)


def _norm_mm(name, x, wn, w, *, tm, tn, ride=None):
    t, dm = x.shape
    n = w.shape[1]
    tm = min(tm, t)
    gi, gj = t // tm, n // tn
    r_ops, r_in_specs, r_outs, r_out_specs, r_scratch = _ride_parts(ride)

    def body(x_ref, wn_ref, w_ref, *rest):
        r_in, rest = rest[:len(r_ops)], rest[len(r_ops):]
        o_ref, u_ref = rest[:2]
        r_out, sems = rest[2:2 + len(r_outs)], rest[2 + len(r_outs):]
        i, j = pl.program_id(0), pl.program_id(1)
        _ride_run(ride, (i == 0) & (j == 0), (i == gi - 1) & (j == gj - 1), r_in, r_out, sems,
                  middle=(i == (3 * gi) // 4) & (j == 0) if gi > 1 else None)

        @pl.when(j == 0)
        def _():
            xv = x_ref[...]
            r = lax.rsqrt(jnp.mean(xv * xv, axis=-1, keepdims=True) + NORM_EPS)
            u_ref[...] = (xv * r * wn_ref[...]).astype(BF16)

        o_ref[...] = jnp.dot(u_ref[...], w_ref[...], preferred_element_type=F32)

    return pl.pallas_call(
        body, name=name, grid=(gi, gj),
        in_specs=[pl.BlockSpec((tm, dm), lambda i, j: (i, 0)), pl.BlockSpec((1, dm), lambda i, j: (0, 0)),
                  pl.BlockSpec((dm, tn), lambda i, j: (0, j))] + r_in_specs,
        out_specs=[pl.BlockSpec((tm, tn), lambda i, j: (i, j)), pl.BlockSpec((tm, dm), lambda i, j: (i, 0))] + r_out_specs,
        out_shape=[SDS((t, n), F32), SDS((t, dm), BF16)] + r_outs, scratch_shapes=r_scratch,
        compiler_params=_cp(dimension_semantics=("arbitrary", "arbitrary")),
    )(x, wn, w, *r_ops)


def _shift_down(tile, halo, s):
    if s == 0:
        return tile
    r = pltpu.roll(tile, s, axis=0)
    h = pltpu.roll(halo, s, axis=0)
    head = jnp.where(_iota(h.shape, 0) < s, h, r[0:8])
    return jnp.concatenate([head, r[8:]], axis=0)


def _shift_up(tile, halo, s):
    if s == 0:
        return tile
    n = tile.shape[0]
    r = pltpu.roll(tile, n - s, axis=0)
    h = pltpu.roll(halo, 8 - s, axis=0)
    tail = jnp.where(_iota(h.shape, 0) >= 8 - s, h, r[n - 8:])
    return jnp.concatenate([r[:n - 8], tail], axis=0)


def _conv_apply(tile, halo, wv, bv, kw):
    acc = bv + wv[kw - 1:kw, :] * tile
    for k in range(kw - 1):
        acc = acc + wv[k:k + 1, :] * _shift_down(tile, halo, kw - 1 - k)
    return acc


def _prev_halo_spec(tm, tc, col0):
    return pl.BlockSpec((8, tc), lambda i, j: (jnp.maximum(i * (tm // 8) - 1, 0), col0 + j))


def _silu_parts(pre):
    sg = _sigmoid(pre)
    return pre * sg, sg * (1.0 + pre * (1.0 - sg))


def _conv_silu_fwd(proj, w, b, *, tm, tc=1536):
    t = proj.shape[0]
    c = w.shape[1]
    tm = min(tm, t)
    col0 = OFF_XBC // tc

    def body(x_ref, h_ref, w_ref, b_ref, o_ref, pre_ref):
        halo = jnp.where(pl.program_id(0) > 0, h_ref[...], 0.0)
        pre = _conv_apply(x_ref[...], halo, w_ref[...], b_ref[...], 4)
        o_ref[...] = _silu_parts(pre)[0]
        pre_ref[...] = pre.astype(BF16)

    tile = pl.BlockSpec((tm, tc), lambda i, j: (i, j))
    return pl.pallas_call(
        body, name="ssd_conv_fwd", grid=(t // tm, c // tc),
        in_specs=[pl.BlockSpec((tm, tc), lambda i, j: (i, col0 + j)), _prev_halo_spec(tm, tc, col0),
                  pl.BlockSpec((4, tc), lambda i, j: (0, j)), pl.BlockSpec((1, tc), lambda i, j: (0, j))],
        out_specs=[tile, tile], out_shape=[SDS((t, c), F32), SDS((t, c), BF16)],
        compiler_params=_cp(dimension_semantics=("arbitrary", "arbitrary")),
    )(proj, proj, w, b)


def _conv_silu_bwd1(d_out, pre, *, tm, tc=1536):
    t, c = pre.shape
    tm = min(tm, t)

    def body(g_ref, p_ref, o_ref, db_ref):
        i = pl.program_id(1)
        d_pre = g_ref[...] * _silu_parts(p_ref[...].astype(F32))[1]
        o_ref[...] = d_pre.astype(BF16)
        _accumulate(db_ref, i == 0, jnp.sum(d_pre, axis=0, keepdims=True))

    tile = pl.BlockSpec((tm, tc), lambda j, i: (i, j))
    return pl.pallas_call(
        body, name="ssd_conv_bwd1", grid=(c // tc, t // tm), in_specs=[tile, tile],
        out_specs=[tile, pl.BlockSpec((1, tc), lambda j, i: (0, j))],
        out_shape=[SDS((t, c), BF16), SDS((1, c), F32)],
        compiler_params=_cp(dimension_semantics=("arbitrary", "arbitrary")),
    )(d_out, pre)


def _conv_bwd2(name, d_pre, src, src_col0, w, *, tm, tc, out_cols, out_col0, fill=None):
    t, c = d_pre.shape
    kw = w.shape[0]
    tm = min(tm, t)
    ni = t // tm
    col0 = src_col0 // tc
    ocol0 = out_col0 // tc

    def body(g_ref, gn_ref, x_ref, w_ref, *rest):
        o_ref, dw_ref = rest[-2:]
        i = pl.program_id(1)
        g = g_ref[...].astype(F32)
        g_next = jnp.where(i < ni - 1, gn_ref[...].astype(F32)[0:8], 0.0)
        xv = x_ref[...]
        wv = w_ref[...]
        shifted = [_shift_up(g, g_next, kw - 1 - k) for k in range(kw)]
        d_in = wv[0:1, :] * shifted[0]
        for k in range(1, kw):
            d_in = d_in + wv[k:k + 1, :] * shifted[k]
        o_ref[...] = d_in.astype(o_ref.dtype)
        rows = [jnp.sum(shifted[k] * xv, axis=0, keepdims=True) for k in range(kw)]

        @pl.when(i == 0)
        def _():
            for k in range(kw):
                dw_ref[k:k + 1, :] = rows[k]

        @pl.when(i > 0)
        def _():
            for k in range(kw):
                dw_ref[k:k + 1, :] += rows[k]

    in_specs = [pl.BlockSpec((tm, tc), lambda j, i: (i, j)),
                pl.BlockSpec((16, tc), lambda j, i: (jnp.minimum((i + 1) * (tm // 16), t // 16 - 1), j)),
                pl.BlockSpec((tm, tc), lambda j, i: (i, col0 + j)),
                pl.BlockSpec((kw, tc), lambda j, i: (0, j))]
    operands = [d_pre, d_pre, src, w]
    if fill is not None:
        in_specs.append(pl.BlockSpec(memory_space=pl.ANY))
        operands.append(fill)
    return pl.pallas_call(
        body, name=name, grid=(c // tc, ni), in_specs=in_specs,
        out_specs=[pl.BlockSpec((tm, tc), lambda j, i: (i, ocol0 + j)), pl.BlockSpec((kw, tc), lambda j, i: (0, j))],
        out_shape=[SDS((t, out_cols), BF16), SDS((kw, c), F32)],
        input_output_aliases={} if fill is None else {4: 0},
        compiler_params=_cp(dimension_semantics=("arbitrary", "arbitrary")),
    )(*operands)


GELU_C = 0.7978845608028654


def _gelu_parts(v):
    inner = GELU_C * (v + 0.044715 * v * v * v)
    th = jnp.tanh(inner)
    val = 0.5 * v * (1.0 + th)
    grad = 0.5 * (1.0 + th) + 0.5 * v * (1.0 - th * th) * GELU_C * (1.0 + 3.0 * 0.044715 * v * v)
    return val, grad


def _ffn_act_fwd(up_raw, w, b, *, tm, tc=1408):
    t = up_raw.shape[0]
    tm = min(tm, t)
    nj = FFN_D_FF // tc
    halo = lambda i: jnp.maximum(i * (tm // 8) - 1, 0)

    def body(g_ref, gh_ref, v_ref, vh_ref, wg_ref, wv_ref, bg_ref, bv_ref, o_ref, gate_ref, val_ref):
        first = pl.program_id(0) > 0
        gate = _conv_apply(g_ref[...], jnp.where(first, gh_ref[...], 0.0), wg_ref[...], bg_ref[...], 3)
        val = _conv_apply(v_ref[...], jnp.where(first, vh_ref[...], 0.0), wv_ref[...], bv_ref[...], 3)
        o_ref[...] = (_gelu_parts(gate)[0] * val).astype(BF16)
        gate_ref[...] = gate.astype(BF16)
        val_ref[...] = val.astype(BF16)

    tile = pl.BlockSpec((tm, tc), lambda i, j: (i, j))
    return pl.pallas_call(
        body, name="ffn_act_fwd", grid=(t // tm, nj),
        in_specs=[tile, pl.BlockSpec((8, tc), lambda i, j: (halo(i), j)),
                  pl.BlockSpec((tm, tc), lambda i, j: (i, nj + j)), pl.BlockSpec((8, tc), lambda i, j: (halo(i), nj + j)),
                  pl.BlockSpec((3, tc), lambda i, j: (0, j)), pl.BlockSpec((3, tc), lambda i, j: (0, nj + j)),
                  pl.BlockSpec((1, tc), lambda i, j: (0, j)), pl.BlockSpec((1, tc), lambda i, j: (0, nj + j))],
        out_specs=[tile] * 3, out_shape=[SDS((t, FFN_D_FF), BF16)] * 3,
        compiler_params=_cp(dimension_semantics=("arbitrary", "arbitrary")),
    )(up_raw, up_raw, up_raw, up_raw, w, w, b, b)


def _ffn_act_bwd(gate, val, d_act, *, tm, tc=1408):
    t = gate.shape[0]
    tm = min(tm, t)
    nj = FFN_D_FF // tc

    def body(g_ref, v_ref, da_ref, dg_ref, dv_ref, dbg_ref, dbv_ref):
        i = pl.program_id(1)
        val = v_ref[...].astype(F32)
        ge, dge = _gelu_parts(g_ref[...].astype(F32))
        da = da_ref[...].astype(F32)
        d_gate = da * val * dge
        d_val = da * ge
        dg_ref[...] = d_gate.astype(BF16)
        dv_ref[...] = d_val.astype(BF16)
        _accumulate(dbg_ref, i == 0, jnp.sum(d_gate, axis=0, keepdims=True))
        _accumulate(dbv_ref, i == 0, jnp.sum(d_val, axis=0, keepdims=True))

    tile = pl.BlockSpec((tm, tc), lambda j, i: (i, j))
    row = pl.BlockSpec((1, tc), lambda j, i: (0, j))
    return pl.pallas_call(
        body, name="ffn_act_bwd", grid=(nj, t // tm), in_specs=[tile] * 3, out_specs=[tile, tile, row, row],
        out_shape=[SDS((t, FFN_D_FF), BF16), SDS((t, FFN_D_FF), BF16), SDS((1, FFN_D_FF), F32), SDS((1, FFN_D_FF), F32)],
        compiler_params=_cp(dimension_semantics=("arbitrary", "arbitrary")),
    )(gate, val, d_act)


def _softplus(v):
    e = jnp.exp(-jnp.abs(v))
    small = e * (1.0 - 0.5 * e)
    return jnp.maximum(v, 0.0) + jnp.where(e < 1e-4, small, jnp.log(1.0 + e))


def _dt_fwd(proj, bias_pad, *, tm):
    t = proj.shape[0]
    tm = min(tm, t)

    def body(x_ref, b_ref, g_ref, gt_ref):
        dt = _softplus(x_ref[...] + b_ref[...])
        first8 = _iota((tm, 128), 1) < 8
        for g in range(SSD_N_GROUPS):
            dg = jnp.where(first8, dt if g == 0 else pltpu.roll(dt, 128 - 8 * g, axis=1), 0.0)
            g_ref[g] = dg
            gt_ref[g] = dg.T[0:8, :]

    return pl.pallas_call(
        body, name="dt_fwd", grid=(t // tm,),
        in_specs=[pl.BlockSpec((tm, 128), lambda i: (i, OFF_DT // 128)), pl.BlockSpec((1, 128), lambda i: (0, 0))],
        out_specs=[pl.BlockSpec((SSD_N_GROUPS, tm, 128), lambda i: (0, i, 0)), pl.BlockSpec((SSD_N_GROUPS, 8, tm), lambda i: (0, 0, i))],
        out_shape=[SDS((SSD_N_GROUPS, t, 128), F32), SDS((SSD_N_GROUPS, 8, t), F32)],
        compiler_params=_cp(dimension_semantics=("arbitrary",)),
    )(proj, bias_pad)


def _dt_bwd(d_dtg, proj, bias_pad, d_proj, *, tm):
    t = proj.shape[0]
    tm = min(tm, t)

    def body(g_ref, x_ref, b_ref, _, o_ref, db_ref):
        first8 = _iota((tm, 128), 1) < 8
        d_dt = jnp.where(first8, g_ref[0], 0.0)
        for g in range(1, SSD_N_GROUPS):
            d_dt = d_dt + pltpu.roll(jnp.where(first8, g_ref[g], 0.0), 8 * g, axis=1)
        d_raw = d_dt * _sigmoid(x_ref[...] + b_ref[...])
        o_ref[:, 0:128] = d_raw.astype(BF16)
        o_ref[:, 128:512] = jnp.zeros((tm, 384), BF16)
        _accumulate(db_ref, pl.program_id(0) == 0, jnp.sum(d_raw, axis=0, keepdims=True))

    return pl.pallas_call(
        body, name="dt_bwd", grid=(t // tm,),
        in_specs=[pl.BlockSpec((SSD_N_GROUPS, tm, 128), lambda i: (0, i, 0)), pl.BlockSpec((tm, 128), lambda i: (i, OFF_DT // 128)),
                  pl.BlockSpec((1, 128), lambda i: (0, 0)), pl.BlockSpec(memory_space=pl.ANY)],
        out_specs=[pl.BlockSpec((tm, 512), lambda i: (i, OFF_DT // 512)), pl.BlockSpec((1, 128), lambda i: (0, 0))],
        out_shape=[SDS((t, PROJ_W), BF16), SDS((1, 128), F32)],
        input_output_aliases={3: 0},
        compiler_params=_cp(dimension_semantics=("arbitrary",)),
    )(d_dtg, proj, bias_pad, d_proj)


def _split3(v):
    hi = v.astype(BF16)
    r1 = v - hi.astype(F32)
    mid = r1.astype(BF16)
    return hi, mid, (r1 - mid.astype(F32)).astype(BF16)


def _times01(v, m3):
    return jnp.dot(jnp.concatenate(_split3(v), axis=1), m3, preferred_element_type=F32)


def _01times(m3, v):
    return jnp.dot(m3, jnp.concatenate(_split3(v), axis=0), preferred_element_type=F32)


def _ssd_decay(dt_ref, dtT_ref, al_ref, alT_ref, k):
    dt = dt_ref[0]
    a_row = -jnp.exp(al_ref[0])
    adt_t = dtT_ref[0] * (-jnp.exp(alT_ref[0]))
    return dt, a_row, _01times(k['low3'][...], dt * a_row), _times01(adt_t, k['up3v'][...])


def _ssd_specs(nc, rev):
    ci = (lambda c: nc - 1 - c) if rev else (lambda c: c)
    return [pl.BlockSpec((CHUNK, SSD_CONV_DIM), lambda c: (ci(c), 0)),
            pl.BlockSpec((SSD_N_GROUPS, CHUNK, 128), lambda c: (0, ci(c), 0)),
            pl.BlockSpec((SSD_N_GROUPS, 8, CHUNK), lambda c: (0, 0, ci(c))),
            pl.BlockSpec((SSD_N_GROUPS, 1, 128), lambda c: (0, 0, 0)),
            pl.BlockSpec((SSD_N_GROUPS, 8, 1), lambda c: (0, 0, 0)),
            pl.BlockSpec((1, SSD_D_INNER), lambda c: (0, 0))]


def _ssd_group_views(g, x_ref, dt_ref, dtT_ref, al_ref, alT_ref, d_ref):
    return (x_ref.at[:, g * GROUP_W:(g + 1) * GROUP_W], dt_ref.at[g:g + 1], dtT_ref.at[g:g + 1], al_ref.at[g:g + 1],
            alT_ref.at[g:g + 1], d_ref.at[:, g * 512:(g + 1) * 512])


NT = (((1,), (1,)), ((), ()))
WIDE = 8 * CHUNK
SSD_CONST_NAMES = ('e128', 'e64', 's64', 'mlo', 'mup', 'low3', 'up3', 'up3v')
SSD_CONST_SHAPES = [pltpu.VMEM((3 * CHUNK, WIDE), BF16), pltpu.VMEM((3 * CHUNK, 512), BF16), pltpu.VMEM((512, CHUNK), BF16),
                    pltpu.VMEM((CHUNK, WIDE), F32), pltpu.VMEM((CHUNK, WIDE), F32), pltpu.VMEM((CHUNK, 3 * CHUNK), BF16),
                    pltpu.VMEM((CHUNK, 3 * CHUNK), BF16), pltpu.VMEM((3 * CHUNK, CHUNK), BF16)]


def _ssd_init_consts(k):
    row, col = _iota((3 * CHUNK, WIDE), 0), _iota((3 * CHUNK, WIDE), 1)
    k['e128'][...] = ((col >> 7) == (row & 127)).astype(BF16)
    k['e64'][...] = ((_iota((3 * CHUNK, 512), 1) >> 6) == (_iota((3 * CHUNK, 512), 0) & 127)).astype(BF16)
    k['s64'][...] = ((_iota((512, CHUNK), 0) >> 6) == _iota((512, CHUNK), 1)).astype(BF16)
    row, col = _iota((CHUNK, WIDE), 0), _iota((CHUNK, WIDE), 1)
    k['mlo'][...] = (row >= (col & 127)).astype(F32)
    k['mup'][...] = (row <= (col & 127)).astype(F32)
    row, col = _iota((CHUNK, 3 * CHUNK), 0), _iota((CHUNK, 3 * CHUNK), 1) & 127
    k['low3'][...] = (row >= col).astype(BF16)
    k['up3'][...] = (row <= col).astype(BF16)
    row, col = _iota((3 * CHUNK, CHUNK), 0) & 127, _iota((3 * CHUNK, CHUNK), 1)
    k['up3v'][...] = (row <= col).astype(BF16)


def _ssd_common(x_ref, dt_ref, dtT_ref, al_ref, alT_ref, k):
    dt, a_row, acs, acs_t = _ssd_decay(dt_ref, dtT_ref, al_ref, alT_ref, k)
    ecol = _times01(acs, k['e128'][...])
    rrow = jnp.concatenate([jnp.broadcast_to(acs_t[j:j + 1, :], (CHUNK, CHUNK)) for j in range(8)], axis=1)
    a64 = _times01(acs, k['e64'][...])
    dt64 = _times01(dt, k['e64'][...])
    a_end64 = a64[CHUNK - 1:CHUNK, :]
    xs = x_ref[:, 0:512]
    return dict(dt=dt, a_row=a_row, acs=acs, seg=ecol - rrow, dt64=dt64, e_a=jnp.exp(a64), decay=jnp.exp(a_end64 - a64),
                e_end64=jnp.exp(a_end64), xs=xs, xdt=xs * dt64, bm=x_ref[:, 512:640], cm=x_ref[:, 640:768])


def _pair_blocks(v):
    lo = _iota((CHUNK, 128), 1) < 64
    out = []
    for i in range(4):
        ch = v[:, i * 128:(i + 1) * 128]
        out.append(jnp.concatenate([jnp.where(lo, ch, 0.0), jnp.where(lo, 0.0, ch)], axis=0).astype(BF16))
    return out


def _tile8(m):
    return jnp.concatenate([m] * 8, axis=1)


def _ssd_fwd(xc, dtg, dtg_t, alog, alog_t, d_exp):
    t = xc.shape[0]
    nc = t // CHUNK

    def body(xa_ref, dta_ref, dtTa_ref, ala_ref, alTa_ref, da_ref, ya_ref, hs_ref, h_scr, *consts):
        c = pl.program_id(0)
        k = dict(zip(SSD_CONST_NAMES, consts))

        @pl.when(c == 0)
        def _():
            _ssd_init_consts(k)
            h_scr[...] = jnp.zeros_like(h_scr)

        for g in range(SSD_N_GROUPS):
            x_ref, dt_ref, dtT_ref, al_ref, alT_ref, d_ref = _ssd_group_views(g, xa_ref, dta_ref, dtTa_ref, ala_ref, alTa_ref, da_ref)
            v = _ssd_common(x_ref, dt_ref, dtT_ref, al_ref, alT_ref, k)
            b16, c16 = v['bm'].astype(BF16), v['cm'].astype(BF16)
            cb = lax.dot_general(c16, b16, NT, preferred_element_type=F32)
            m16 = (jnp.exp(jnp.minimum(v['seg'], 0.0)) * k['mlo'][...] * _tile8(cb)).astype(BF16)
            xbd = _pair_blocks(v['xdt'])
            y_diag = jnp.concatenate([jnp.dot(m16[:, i * 256:(i + 1) * 256], xbd[i], preferred_element_type=F32)
                                      for i in range(4)], axis=1)
            ht = h_scr[g]
            y_off = jnp.dot(c16, ht.astype(BF16), preferred_element_type=F32)
            ya_ref[:, g * 512:(g + 1) * 512] = y_diag + v['e_a'] * y_off + d_ref[...] * v['xs']
            st = jnp.dot(v['bm'].T.astype(BF16), (v['xdt'] * v['decay']).astype(BF16), preferred_element_type=F32)
            hs_ref[0, g] = ht
            h_scr[g] = ht * v['e_end64'] + st

    return pl.pallas_call(
        body, name="ssd_fwd", grid=(nc,), in_specs=_ssd_specs(nc, False),
        out_specs=[pl.BlockSpec((CHUNK, SSD_D_INNER), lambda c: (c, 0)),
                   pl.BlockSpec((1, SSD_N_GROUPS, SSD_D_STATE, 512), lambda c: (c, 0, 0, 0))],
        out_shape=[SDS((t, SSD_D_INNER), F32), SDS((nc, SSD_N_GROUPS, SSD_D_STATE, 512), F32)],
        scratch_shapes=[pltpu.VMEM((SSD_N_GROUPS, SSD_D_STATE, 512), F32)] + SSD_CONST_SHAPES,
        compiler_params=_cp(dimension_semantics=("arbitrary",)),
    )(xc, dtg, dtg_t, alog, alog_t, d_exp)


def _ssd_bwd(xc, dtg, dtg_t, alog, alog_t, d_exp, d_y, hs, ride=None):
    t = xc.shape[0]
    nc = t // CHUNK

    r_ops, r_in_specs, r_outs, r_out_specs, r_scratch = _ride_parts(ride)

    def body(xa_ref, dta_ref, dtTa_ref, ala_ref, alTa_ref, da_ref, dya_ref, hs_ref, *rest):
        r_in, rest = rest[:len(r_ops)], rest[len(r_ops):]
        dxa_ref, ddta_ref, dal_ref, dd_ref = rest[:4]
        r_out, rest = rest[4:4 + len(r_outs)], rest[4 + len(r_outs):]
        g_scr, consts, sems = rest[0], rest[1:1 + len(SSD_CONST_NAMES)], rest[1 + len(SSD_CONST_NAMES):]
        c = pl.program_id(0)
        _ride_run(ride, c == 0, c == nc - 1, r_in, r_out, sems)
        k = dict(zip(SSD_CONST_NAMES, consts))

        @pl.when(c == 0)
        def _():
            _ssd_init_consts(k)
            g_scr[...] = jnp.zeros_like(g_scr)

        for g in range(SSD_N_GROUPS):
            views = _ssd_group_views(g, xa_ref, dta_ref, dtTa_ref, ala_ref, alTa_ref, da_ref)
            one_group(c, g, k, *views, dya_ref.at[:, g * 512:(g + 1) * 512], hs_ref, g_scr,
                      dxa_ref.at[:, g * GROUP_W:(g + 1) * GROUP_W], ddta_ref.at[g:g + 1], dal_ref, dd_ref)

    def one_group(c, g, k, x_ref, dt_ref, dtT_ref, al_ref, alT_ref, d_ref, dy_ref, hs_ref, g_scr, dx_ref, ddt_ref,
                  dal_ref, dd_ref):
        s64, mlo, mup = k['s64'], k['mlo'], k['mup']
        v = _ssd_common(x_ref, dt_ref, dtT_ref, al_ref, alT_ref, k)
        dt, a_row, xs, xdt, e_a, decay = v['dt'], v['a_row'], v['xs'], v['xdt'], v['e_a'], v['decay']
        row, col = _iota((CHUNK, CHUNK), 0), _iota((CHUNK, CHUNK), 1)
        b16, c16 = v['bm'].astype(BF16), v['cm'].astype(BF16)
        ct16 = v['cm'].T.astype(BF16)
        cb = lax.dot_general(c16, b16, NT, preferred_element_type=F32)
        cbt = lax.dot_general(b16, c16, NT, preferred_element_type=F32)
        lmat = jnp.exp(jnp.minimum(v['seg'], 0.0)) * mlo[...]
        lmat_t = jnp.exp(jnp.minimum(-v['seg'], 0.0)) * mup[...]
        mmat, mmat_t = lmat * _tile8(cb), lmat_t * _tile8(cbt)
        mt16 = mmat_t.astype(BF16)
        dy = dy_ref[...]
        dye, xdec = dy * e_a, xdt * decay
        dy16, dye16, xdec16 = dy.astype(BF16), dye.astype(BF16), xdec.astype(BF16)
        xdt16 = xdt.astype(BF16)
        ht, gt = hs_ref[0, g], g_scr[g]
        ht16, gt16 = ht.astype(BF16), gt.astype(BF16)
        xbd, dybd = _pair_blocks(xdt), _pair_blocks(dy)
        d_m, d_mt, d_x = [], [], []
        for i in range(4):
            csl = slice(i * 128, (i + 1) * 128)
            d_m.append(lax.dot_general(dy16[:, csl], xbd[i], NT, preferred_element_type=F32))
            d_mt.append(lax.dot_general(xdt16[:, csl], dybd[i], NT, preferred_element_type=F32))
            d_x.append(jnp.dot(mt16[:, i * 256:(i + 1) * 256], dybd[i], preferred_element_type=F32))
        d_m, d_mt, d_x = jnp.concatenate(d_m, axis=1), jnp.concatenate(d_mt, axis=1), jnp.concatenate(d_x, axis=1)

        def head_sum(m):
            acc = m[:, 0:CHUNK]
            for j in range(1, 8):
                acc = acc + m[:, j * CHUNK:(j + 1) * CHUNK]
            return acc

        def seg64(p):
            return jnp.dot(p.astype(BF16), s64[...], preferred_element_type=F32)

        d_cb16 = head_sum(d_m * lmat).astype(BF16)
        d_cbt16 = head_sum(d_mt * lmat_t).astype(BF16)
        dseg = d_m * mmat - d_mt * mmat_t
        da_seg = jnp.zeros((CHUNK, CHUNK), F32)
        for j in range(8):
            da_seg = jnp.where(col == j, jnp.sum(dseg[:, j * CHUNK:(j + 1) * CHUNK], axis=1, keepdims=True), da_seg)
        ch = jnp.dot(c16, ht16, preferred_element_type=F32)
        bg = jnp.dot(b16, gt16, preferred_element_type=F32)
        d_x = d_x + decay * bg
        d_decay = seg64(xdec * bg)
        e_end = jnp.exp(v['acs'][CHUNK - 1:CHUNK, :])
        d_end = e_end * jnp.sum(seg64(gt * ht), axis=0, keepdims=True) + jnp.sum(d_decay, axis=0, keepdims=True)
        d_a = seg64(dye * ch) - d_decay + da_seg + jnp.where(row == CHUNK - 1, d_end, 0.0)
        dx_ref[:, 0:512] = d_x * v['dt64'] + d_ref[...] * dy
        dx_ref[:, 640:768] = (lax.dot_general(dye16, ht16, NT, preferred_element_type=F32)
                              + jnp.dot(d_cb16, b16, preferred_element_type=F32))
        dx_ref[:, 512:640] = (lax.dot_general(xdec16, gt16, NT, preferred_element_type=F32)
                              + jnp.dot(d_cbt16, c16, preferred_element_type=F32))
        g_scr[g] = gt * v['e_end64'] + jnp.dot(ct16, dye16, preferred_element_type=F32)
        d_adt = _01times(k['up3'][...], d_a)
        ddt_ref[0] = d_adt * a_row + seg64(d_x * xs)
        d_alog = jnp.sum(d_adt * dt, axis=0, keepdims=True) * a_row
        dd_row = jnp.sum(seg64(dy * xs), axis=0, keepdims=True)
        first = c == 0

        @pl.when(first)
        def _():
            dal_ref[g] = d_alog
            dd_ref[g] = dd_row

        @pl.when(jnp.logical_not(first))
        def _():
            dal_ref[g] += d_alog
            dd_ref[g] += dd_row

    rc = lambda c: nc - 1 - c
    whole = pl.BlockSpec((SSD_N_GROUPS, 1, 128), lambda c: (0, 0, 0))
    return pl.pallas_call(
        body, name="ssd_bwd", grid=(nc,),
        in_specs=_ssd_specs(nc, True) + [pl.BlockSpec((CHUNK, SSD_D_INNER), lambda c: (rc(c), 0)),
                                        pl.BlockSpec((1, SSD_N_GROUPS, SSD_D_STATE, 512), lambda c: (rc(c), 0, 0, 0))] + r_in_specs,
        out_specs=[pl.BlockSpec((CHUNK, SSD_CONV_DIM), lambda c: (rc(c), 0)),
                   pl.BlockSpec((SSD_N_GROUPS, CHUNK, 128), lambda c: (0, rc(c), 0)), whole, whole] + r_out_specs,
        out_shape=[SDS((t, SSD_CONV_DIM), F32), SDS((SSD_N_GROUPS, t, 128), F32),
                   SDS((SSD_N_GROUPS, 1, 128), F32), SDS((SSD_N_GROUPS, 1, 128), F32)] + r_outs,
        scratch_shapes=[pltpu.VMEM((SSD_N_GROUPS, SSD_D_STATE, 512), F32)] + SSD_CONST_SHAPES + r_scratch,
        compiler_params=_cp(dimension_semantics=("arbitrary",)),
    )(xc, dtg, dtg_t, alog, alog_t, d_exp, d_y, hs, *r_ops)


def _gated_norm_fwd(y, proj, w, *, tm):
    t = y.shape[0]
    tm = min(tm, t)

    def body(y_ref, z_ref, w_ref, o_ref):
        gv = y_ref[...] * _silu_parts(z_ref[...])[0]
        r = lax.rsqrt(jnp.mean(gv * gv, axis=-1, keepdims=True) + NORM_EPS)
        o_ref[...] = (gv * r * w_ref[...]).astype(BF16)

    tile = pl.BlockSpec((tm, 512), lambda i, g: (i, g))
    return pl.pallas_call(
        body, name="gated_norm_fwd", grid=(t // tm, SSD_N_GROUPS),
        in_specs=[tile, pl.BlockSpec((tm, 512), lambda i, g: (i, OFF_Z // 512 + g)),
                  pl.BlockSpec((1, 512), lambda i, g: (0, g))], out_specs=tile,
        out_shape=SDS((t, SSD_D_INNER), BF16),
        compiler_params=_cp(dimension_semantics=("arbitrary", "arbitrary")),
    )(y, proj, w)


def _rope(ch, cos_t, sin_t):
    first = (_iota(ch.shape, 1) & 32) == 0
    partner = jnp.where(first, pltpu.roll(ch, 96, axis=1), pltpu.roll(ch, 32, axis=1))
    return ch * cos_t + partner * sin_t


def _rope_qkv(proj, cos_t, sin_t, *, tm):
    t = proj.shape[0]
    tm = min(tm, t)

    def body(q_ref, k_ref, v_ref, c_ref, s_ref, qr_ref, kp_ref, vp_ref, kt_ref, vt_ref):
        cv, sv = c_ref[...], s_ref[...]
        lo = _iota((tm, 128), 1) < 64
        for m in range(8):
            sl = slice(m * 128, (m + 1) * 128)
            qr_ref[:, sl] = (_rope(q_ref[:, sl], cv, sv) * 0.125).astype(BF16)
        for m2 in range(2):
            sl = slice(m2 * 128, (m2 + 1) * 128)
            for src, dst, dst_t in ((_rope(k_ref[:, sl], cv, sv), kp_ref, kt_ref), (v_ref[:, sl], vp_ref, vt_ref)):
                sw = pltpu.roll(src, 64, axis=1)
                padded = (jnp.where(lo, src, 0.0), jnp.where(lo, 0.0, sw), jnp.where(lo, sw, 0.0), jnp.where(lo, 0.0, src))
                for i, pad in enumerate(padded):
                    rows = slice((4 * m2 + i) * 128, (4 * m2 + i + 1) * 128)
                    dst[:, rows] = pad.astype(BF16)
                    dst_t[rows, :] = pad.T.astype(BF16)

    return pl.pallas_call(
        body, name="rope_qkv", grid=(t // tm,),
        in_specs=[pl.BlockSpec((tm, 1024), lambda i: (i, OFF_Q // 1024)), pl.BlockSpec((tm, 256), lambda i: (i, OFF_K // 256)),
                  pl.BlockSpec((tm, 256), lambda i: (i, OFF_V // 256)), pl.BlockSpec((tm, 128), lambda i: (i, 0)),
                  pl.BlockSpec((tm, 128), lambda i: (i, 0))],
        out_specs=[pl.BlockSpec((tm, 1024), lambda i: (i, 0))] * 3 + [pl.BlockSpec((1024, tm), lambda i: (0, i))] * 2,
        out_shape=[SDS((t, 1024), BF16)] * 3 + [SDS((1024, t), BF16)] * 2,
        compiler_params=_cp(dimension_semantics=("arbitrary",)),
    )(proj, proj, proj, cos_t, sin_t)


def _attn_valid(n):
    kj, qi = _iota((2 * CHUNK, CHUNK), 0), _iota((2 * CHUNK, CHUNK), 1)
    return (kj > qi) & (kj <= qi + CHUNK) & ((n > 0) | (kj >= CHUNK))


def _attn_fwd(qr, kp, vt, sinks):
    t = qr.shape[0]
    nb = t // CHUNK

    def body(q_ref, kc_ref, kprev_ref, vc_ref, vprev_ref, sk_ref, o_ref, lse_ref):
        n = pl.program_id(0)
        valid = _attn_valid(n)
        head_row = _iota((16, CHUNK), 0)
        lse_all = jnp.zeros((16, CHUNK), F32)
        for m in range(8):
            g = m // 2
            qch = q_ref[:, m * 128:(m + 1) * 128]
            sls = [slice((2 * g + e) * 128, (2 * g + e + 1) * 128) for e in range(2)]
            kk2 = jnp.concatenate([r[:, sl] for sl in sls for r in (kprev_ref, kc_ref)], axis=0)
            vv2_t = jnp.concatenate([r[sl, :] for sl in sls for r in (vprev_ref, vc_ref)], axis=1)
            s2 = lax.dot_general(kk2, qch, NT, preferred_element_type=F32)
            probs = []
            for e in range(2):
                h = 2 * m + e
                s = jnp.where(valid, s2[2 * CHUNK * e:2 * CHUNK * (e + 1)], NEG)
                sink = sk_ref[0:1, h:h + 1]
                mx = jnp.maximum(jnp.max(s, axis=0, keepdims=True), sink)
                p = jnp.exp(s - mx)
                den = jnp.sum(p, axis=0, keepdims=True) + jnp.exp(sink - mx)
                probs.append((p * (1.0 / den)).astype(BF16))
                lse_all = jnp.where(head_row == h, mx + jnp.log(den), lse_all)
            o_t = jnp.dot(vv2_t, jnp.concatenate(probs, axis=0), preferred_element_type=F32)
            o_ref[:, m * 128:(m + 1) * 128] = o_t.T.astype(BF16)
        lse_ref[0] = lse_all

    cur = pl.BlockSpec((CHUNK, 1024), lambda n: (n, 0))
    prev = pl.BlockSpec((CHUNK, 1024), lambda n: (jnp.maximum(n - 1, 0), 0))
    cur_t = pl.BlockSpec((1024, CHUNK), lambda n: (0, n))
    prev_t = pl.BlockSpec((1024, CHUNK), lambda n: (0, jnp.maximum(n - 1, 0)))
    return pl.pallas_call(
        body, name="attn_fwd", grid=(nb,),
        in_specs=[cur, cur, prev, cur_t, prev_t, pl.BlockSpec((1, 128), lambda n: (0, 0))],
        out_specs=[cur, pl.BlockSpec((1, 16, CHUNK), lambda n: (n, 0, 0))],
        out_shape=[SDS((t, 1024), BF16), SDS((nb, 16, CHUNK), F32)],
        compiler_params=_cp(dimension_semantics=("arbitrary",)),
    )(qr, kp, kp, vt, vt, sinks)


def _attn_bwd(qr, kp, vp, kt, d_o, o, lse, sinks, cos_t, sin_t, d_proj, ride=None):
    t = qr.shape[0]
    nb = t // CHUNK

    r_ops, r_in_specs, r_outs, r_out_specs, r_scratch = _ride_parts(ride)

    def body(q_ref, kc_ref, kprev_ref, vc_ref, vprev_ref, ktc_ref, ktprev_ref, do_ref, o_ref, lse_ref, sk_ref,
             c_ref, s_ref, cp_ref, sp_ref, _, *rest):
        r_in, rest = rest[:len(r_ops)], rest[len(r_ops):]
        dqkv_ref, dsk_ref = rest[:2]
        r_out, rest = rest[2:2 + len(r_outs)], rest[2 + len(r_outs):]
        acc_k, acc_v, dq_scr = rest[:3]
        n = pl.program_id(0)
        _ride_run(ride, n == 0, n == nb, r_in, r_out, rest[3:])
        lane = _iota((CHUNK, 128), 1)
        lo = lane < 64
        lane1 = _iota((1, 128), 1)

        @pl.when(n == 0)
        def _():
            acc_k[...] = jnp.zeros_like(acc_k)
            acc_v[...] = jnp.zeros_like(acc_v)
            dsk_ref[...] = jnp.zeros((1, 128), F32)

        @pl.when(n > 0)
        def _():
            dqkv_ref[:, 0:1024] = dq_scr[...]
            for r in range(8):
                acc_k[r, 0:CHUNK] = acc_k[r, CHUNK:2 * CHUNK]
                acc_v[r, 0:CHUNK] = acc_v[r, CHUNK:2 * CHUNK]
                acc_k[r, CHUNK:2 * CHUNK] = jnp.zeros((CHUNK, 128), F32)
                acc_v[r, CHUNK:2 * CHUNK] = jnp.zeros((CHUNK, 128), F32)

        @pl.when(n < nb)
        def _():
            valid = _attn_valid(n)
            lse_all = lse_ref[0]
            dsk = jnp.zeros((1, 128), F32)
            for m in range(8):
                g = m // 2
                csl = slice(m * 128, (m + 1) * 128)
                qch = q_ref[:, csl]
                doch = do_ref[:, csl]
                prod_t = (doch.astype(F32) * o_ref[:, csl].astype(F32)).T
                sls = [slice((2 * g + e) * 128, (2 * g + e + 1) * 128) for e in range(2)]
                kk2 = jnp.concatenate([r[:, sl] for sl in sls for r in (kprev_ref, kc_ref)], axis=0)
                vv2 = jnp.concatenate([r[:, sl] for sl in sls for r in (vprev_ref, vc_ref)], axis=0)
                kk2_t = jnp.concatenate([r[sl, :] for sl in sls for r in (ktprev_ref, ktc_ref)], axis=1)
                s2 = lax.dot_general(kk2, qch, NT, preferred_element_type=F32)
                d_p2 = lax.dot_general(vv2, doch, NT, preferred_element_type=F32)
                ps, d_ss = [], []
                for e in range(2):
                    h = 2 * m + e
                    rows = slice(2 * CHUNK * e, 2 * CHUNK * (e + 1))
                    lse_h = lse_all[h:h + 1, :]
                    p = jnp.exp(jnp.where(valid, s2[rows], NEG) - lse_h)
                    delta = jnp.sum(prod_t[64 * e:64 * (e + 1)], axis=0, keepdims=True)
                    ps.append(p.astype(BF16))
                    d_ss.append((p * (d_p2[rows] - delta)).astype(BF16))
                    p_sink = jnp.exp(sk_ref[0:1, h:h + 1] - lse_h)
                    dsk = jnp.where(lane1 == h, -jnp.sum(p_sink * delta), dsk)
                d_s2, p2 = jnp.concatenate(d_ss, axis=0), jnp.concatenate(ps, axis=0)
                d_k2 = jnp.dot(d_s2, qch, preferred_element_type=F32)
                d_v2 = jnp.dot(p2, doch, preferred_element_type=F32)
                for e in range(2):
                    rows = slice(2 * CHUNK * e, 2 * CHUNK * (e + 1))
                    acc_k[2 * g + e] += d_k2[rows]
                    acc_v[2 * g + e] += d_v2[rows]
                dq_t = jnp.dot(kk2_t, d_s2, preferred_element_type=F32)
                dq_scr[:, csl] = (_rope(dq_t.T, c_ref[...], -s_ref[...]) * 0.125).astype(BF16)
            dsk_ref[...] += dsk

        @pl.when(n > 0)
        def _():
            for m2 in range(2):
                halves = []
                for g in (2 * m2, 2 * m2 + 1):
                    for acc in (acc_k, acc_v):
                        comb = jnp.where(lo, acc[2 * g, 0:CHUNK], acc[2 * g + 1, 0:CHUNK])
                        halves.append(comb + pltpu.roll(comb, 64, axis=1))
                d_kr = jnp.where(lo, halves[0], halves[2])
                d_v = jnp.where(lo, halves[1], halves[3])
                dqkv_ref[:, OFF_K + m2 * 128:OFF_K + (m2 + 1) * 128] = _rope(d_kr, cp_ref[...], -sp_ref[...]).astype(BF16)
                dqkv_ref[:, OFF_V + m2 * 128:OFF_V + (m2 + 1) * 128] = d_v.astype(BF16)

    qn = lambda n: jnp.minimum(n, nb - 1)
    pn = lambda n: jnp.maximum(jnp.minimum(n, nb) - 1, 0)
    cur = pl.BlockSpec((CHUNK, 1024), lambda n: (qn(n), 0))
    prev = pl.BlockSpec((CHUNK, 1024), lambda n: (pn(n), 0))
    cur128 = pl.BlockSpec((CHUNK, 128), lambda n: (qn(n), 0))
    prev128 = pl.BlockSpec((CHUNK, 128), lambda n: (pn(n), 0))
    cur_t = pl.BlockSpec((1024, CHUNK), lambda n: (0, qn(n)))
    prev_t = pl.BlockSpec((1024, CHUNK), lambda n: (0, pn(n)))
    one = pl.BlockSpec((1, 128), lambda n: (0, 0))
    return pl.pallas_call(
        body, name="attn_bwd", grid=(nb + 1,),
        in_specs=[cur, cur, prev, cur, prev, cur_t, prev_t, cur, cur, pl.BlockSpec((1, 16, CHUNK), lambda n: (qn(n), 0, 0)),
                  one, cur128, cur128, prev128, prev128, pl.BlockSpec(memory_space=pl.ANY)] + r_in_specs,
        out_specs=[pl.BlockSpec((CHUNK, 1536), lambda n: (pn(n), 0)), one] + r_out_specs,
        out_shape=[SDS((t, PROJ_W), BF16), SDS((1, 128), F32)] + r_outs,
        scratch_shapes=[pltpu.VMEM((8, 2 * CHUNK, 128), F32), pltpu.VMEM((8, 2 * CHUNK, 128), F32),
                        pltpu.VMEM((CHUNK, 1024), BF16)] + r_scratch,
        input_output_aliases={15: 0},
        compiler_params=_cp(dimension_semantics=("arbitrary",)),
    )(qr, kp, kp, vp, vp, kt, kt, d_o, o, lse, sinks, cos_t, sin_t, cos_t, sin_t, d_proj, *r_ops)


def _adamw(name, w, g, m, v, *, tr):
    rows, cols = w.shape
    tr = min(tr, rows)
    assert rows % tr == 0

    def body(w_ref, g_ref, m_ref, v_ref, d_ref, nm_ref, nv_ref):
        gv = g_ref[...]
        nm = ADAM_B1 * m_ref[...] + (1.0 - ADAM_B1) * gv
        nv = ADAM_B2 * v_ref[...] + (1.0 - ADAM_B2) * (gv * gv)
        m_hat = nm / (1.0 - ADAM_B1 ** ADAM_STEP)
        v_hat = nv / (1.0 - ADAM_B2 ** ADAM_STEP)
        d_ref[...] = -ADAM_LR * (m_hat / (jnp.sqrt(v_hat) + ADAM_EPS) + ADAM_WD * w_ref[...])
        nm_ref[...] = nm
        nv_ref[...] = nv

    tile = pl.BlockSpec((tr, cols), lambda i: (i, 0))
    return pl.pallas_call(
        body, name=name, grid=(rows // tr,), in_specs=[tile] * 4, out_specs=[tile] * 3,
        out_shape=[SDS((rows, cols), F32)] * 3, compiler_params=_cp(dimension_semantics=("arbitrary",)),
    )(w, g, m, v)


def _local_step(x, cos_t, sin_t, tgt, wb, ps, late=None, rides=None):
    t = x.shape[0]
    tm = min(512, t)
    tmw = min(1024, t)
    ij = lambda i, j, k: (i, j)
    i0 = lambda i, j, k: (i, 0)
    c0 = lambda i, j, k: (0, 0)
    cj = lambda i, j, k: (0, j)
    rides = rides or (lambda group, grads: None)
    rode = {}

    tkt = min(2048, t)
    proj, u, *arrived = _norm_mm("in_proj", x, ps['norm_mix_pre_w'], wb['cat'], tm=tmw, tn=1024,
                                 ride=late[0] if late else None)
    if late:
        more_wb, more_ps = late[1](arrived)
        wb, ps = {**wb, **more_wb}, {**ps, **more_ps}
    xc, xc_pre = _conv_silu_fwd(proj, ps['ssd_conv_w'], ps['ssd_conv_b'], tm=tm)
    bias_pad = jnp.pad(ps['ssd_dt_bias'], ((0, 0), (0, 96)))
    dtg, dtg_t = _dt_fwd(proj, bias_pad, tm=tmw)
    alog = jnp.pad(ps['ssd_a_log'].reshape(SSD_N_GROUPS, 1, 8), ((0, 0), (0, 0), (0, 120)))
    alog_t = ps['ssd_a_log'].reshape(SSD_N_GROUPS, 8, 1)
    d_exp = jnp.repeat(ps['ssd_d'], SSD_HEAD_DIM, axis=1)
    y, hs = _ssd_fwd(xc, dtg, dtg_t, alog, alog_t, d_exp)
    gn = _gated_norm_fwd(y, proj, ps['ssd_norm_w'], tm=tmw)
    qr, kp, vp, kt, vt = _rope_qkv(proj, cos_t, sin_t, tm=tm)
    sinks = jnp.pad(ps['attn_sinks'], ((0, 0), (0, 112)))
    ao, lse = _attn_fwd(qr, kp, vt, sinks)
    y_attn = _mm_plain("attn_out", ao, wb['ao'], tm=tmw, tn=512, tk=1024)

    def merge_ep(acc, i, j, ins, outs):
        gs, ga, ya = ins
        outs[0][...] = (_sigmoid(gs[...]) * acc + _sigmoid(ga[...]) * ya[...]).astype(BF16)
        outs[1][...] = acc

    merged, y_ssd = _mm_call(
        "ssd_out_merge", gn, wb['so'], tm=tmw, tn=512, tk=2048, epilogue=merge_ep,
        extra_in=[(proj, (tmw, 512), lambda i, j, k: (i, OFF_GS // 512 + j)),
                  (proj, (tmw, 512), lambda i, j, k: (i, OFF_GA // 512 + j)), (y_attn, (tmw, 512), ij)],
        outs=[((t, D_MODEL), BF16, (tmw, 512), ij), ((t, D_MODEL), F32, (tmw, 512), ij)])

    def mix_ep(acc, i, j, ins, outs):
        xv, wn = ins
        r = lax.rsqrt(jnp.mean(acc * acc, axis=-1, keepdims=True) + NORM_EPS)
        outs[0][...] = xv[...] + acc * r * wn[...]
        outs[1][...] = acc

    x1, mmix = _mm_call(
        "mix_out", merged, wb['mix'], tm=tm, tn=D_MODEL, tk=1024, epilogue=mix_ep,
        extra_in=[(x, (tm, D_MODEL), i0), (ps['norm_mix_post_w'], (1, D_MODEL), c0)],
        outs=[((t, D_MODEL), F32, (tm, D_MODEL), i0), ((t, D_MODEL), F32, (tm, D_MODEL), i0)])

    up_raw, h = _norm_mm("ffn_up", x1, ps['norm_ffn_pre_w'], wb['up'], tm=tmw, tn=1408)
    act, ffn_gate, ffn_val = _ffn_act_fwd(up_raw, ps['ffn_conv_w'], ps['ffn_conv_b'], tm=tm)

    def loss_ep(acc, i, j, ins, outs):
        x1v, tg, wn = ins
        d_ff_ref, dout_ref, loss_ref, dw_ref = outs
        wv = wn[...]
        r = lax.rsqrt(jnp.mean(acc * acc, axis=-1, keepdims=True) + NORM_EPS)
        err = x1v[...] + acc * r * wv - tg[...]
        dout = err * (1.0 / D_MODEL)
        dout_ref[...] = dout
        d_ff, dw = _rms_bwd(acc, wv, dout)
        d_ff_ref[...] = d_ff.astype(BF16)
        _accumulate(dw_ref, i == 0, dw)
        _accumulate(loss_ref, i == 0, jnp.sum(err * err, keepdims=True) * (0.5 / D_MODEL))

    d_ff, dout, loss, g_norm_ffn_post = _mm_call(
        "ffn_down_loss", act, wb['dn'], tm=tm, tn=D_MODEL, tk=FFN_D_FF, epilogue=loss_ep,
        extra_in=[(x1, (tm, D_MODEL), i0), (tgt, (tm, D_MODEL), i0), (ps['norm_ffn_post_w'], (1, D_MODEL), c0)],
        outs=[((t, D_MODEL), BF16, (tm, D_MODEL), i0), ((t, D_MODEL), F32, (tm, D_MODEL), i0),
              ((1, 1), F32, (1, 1), c0), ((1, D_MODEL), F32, (1, D_MODEL), c0)])

    d_act = _mm_plain("d_act", d_ff, wb['dn_t'], tm=tmw, tn=1408, tk=1024, out_dtype=BF16)
    g_w_down = _mm_plain("g_w_down", act, d_ff, tm=1408, tn=1024, tk=tkt, trans_a=True, out_dtype=BF16)
    d_gate, d_val, db_g, db_v = _ffn_act_bwd(ffn_gate, ffn_val, d_act, tm=tm)
    d_up_raw, gcw_g = _conv_bwd2("ffn_conv_bwd2_gate", d_gate, up_raw, 0, ps['ffn_conv_w'][:, :FFN_D_FF], tm=tm,
                                 tc=1408, out_cols=2 * FFN_D_FF, out_col0=0)
    d_up_raw, gcw_v = _conv_bwd2("ffn_conv_bwd2_val", d_val, up_raw, FFN_D_FF, ps['ffn_conv_w'][:, FFN_D_FF:], tm=tm,
                                 tc=1408, out_cols=2 * FFN_D_FF, out_col0=FFN_D_FF, fill=d_up_raw)
    g_ffn_conv_w = jnp.concatenate([gcw_g, gcw_v], axis=1)

    def dx1_ep(acc, i, j, ins, outs):
        x1v, wpre, dout_v, mmv, wpost = ins
        d_x1_ref, d_mm_ref, dwpre_ref, dwpost_ref = outs
        d_n, dw_pre = _rms_bwd(x1v[...], wpre[...], acc)
        d_x1 = dout_v[...] + d_n
        d_x1_ref[...] = d_x1
        d_mm, dw_post = _rms_bwd(mmv[...], wpost[...], d_x1)
        d_mm_ref[...] = d_mm.astype(BF16)
        _accumulate(dwpre_ref, i == 0, dw_pre)
        _accumulate(dwpost_ref, i == 0, dw_post)

    d_x1, d_mm, g_norm_ffn_pre, g_norm_mix_post = _mm_call(
        "d_h", d_up_raw, wb['up_t'], tm=tm, tn=D_MODEL, tk=2 * FFN_D_FF, epilogue=dx1_ep, vmem_mb=VMEM_BIG_MB,
        extra_in=[(x1, (tm, D_MODEL), i0), (ps['norm_ffn_pre_w'], (1, D_MODEL), c0), (dout, (tm, D_MODEL), i0),
                  (mmix, (tm, D_MODEL), i0), (ps['norm_mix_post_w'], (1, D_MODEL), c0)],
        outs=[((t, D_MODEL), F32, (tm, D_MODEL), i0), ((t, D_MODEL), BF16, (tm, D_MODEL), i0),
              ((1, D_MODEL), F32, (1, D_MODEL), c0), ((1, D_MODEL), F32, (1, D_MODEL), c0)])
    g_w_up_t = _mm_plain("g_w_up", d_up_raw, h, tm=1408, tn=1024, tk=tkt, trans_a=True, out_dtype=BF16)
    ride_ffn = rides('ffn', {'ffn_w_up': g_w_up_t, 'ffn_w_down': g_w_down})

    def dmerge_ep(acc, i, j, ins, outs):
        gs, ga, ys, ya = ins
        sg_s, sg_a = _sigmoid(gs[...]), _sigmoid(ga[...])
        outs[0][...] = (acc * sg_s).astype(BF16)
        outs[1][...] = (acc * sg_a).astype(BF16)
        outs[2][:, 0:D_MODEL] = (acc * ys[...] * sg_s * (1.0 - sg_s)).astype(BF16)
        outs[2][:, D_MODEL:2 * D_MODEL] = (acc * ya[...] * sg_a * (1.0 - sg_a)).astype(BF16)

    d_yssd, d_yattn, d_proj = _mm_call(
        "d_merged", d_mm, wb['mix_t'], tm=tm, tn=D_MODEL, tk=1024, epilogue=dmerge_ep,
        extra_in=[(proj, (tm, D_MODEL), lambda i, j, k: (i, OFF_GS // D_MODEL)),
                  (proj, (tm, D_MODEL), lambda i, j, k: (i, OFF_GA // D_MODEL)), (y_ssd, (tm, D_MODEL), i0), (y_attn, (tm, D_MODEL), i0)],
        outs=[((t, D_MODEL), BF16, (tm, D_MODEL), i0), ((t, D_MODEL), BF16, (tm, D_MODEL), i0),
              ((t, PROJ_W), BF16, (tm, 2 * D_MODEL), lambda i, j, k: (i, OFF_GS // (2 * D_MODEL)))])
    g_w_mix = _mm_plain("g_w_mix", merged, d_mm, tm=1024, tn=1024, tk=tkt, trans_a=True, out_dtype=BF16)

    def dgn_ep(acc, i, j, ins, outs):
        yv, zv, wn = ins
        d_y_ref, d_z_ref, dw_ref = outs
        zz = zv[...]
        sz = _sigmoid(zz)
        silu = zz * sz
        gv = yv[...] * silu
        r = lax.rsqrt(jnp.mean(gv * gv, axis=-1, keepdims=True) + NORM_EPS)
        gh = gv * r
        dgh = acc * wn[...]
        dg = r * (dgh - gh * jnp.mean(dgh * gh, axis=-1, keepdims=True))
        d_y_ref[...] = dg * silu
        d_z_ref[...] = (dg * yv[...] * (sz * (1.0 + zz * (1.0 - sz)))).astype(BF16)
        dw = jnp.sum(acc * gh, axis=0, keepdims=True)

        @pl.when(i == 0)
        def _():
            dw_ref[j] = dw

        @pl.when(i > 0)
        def _():
            dw_ref[j] += dw

    d_y, d_proj, g_ssd_norm = _mm_call(
        "d_gn", d_yssd, wb['so_t'], tm=tmw, tn=512, tk=1024, epilogue=dgn_ep, fill=(d_proj, 1),
        extra_in=[(y, (tmw, 512), ij), (proj, (tmw, 512), lambda i, j, k: (i, OFF_Z // 512 + j)), (ps['ssd_norm_w'], (1, 512), cj)],
        outs=[((t, SSD_D_INNER), F32, (tmw, 512), ij), ((t, PROJ_W), BF16, (tmw, 512), lambda i, j, k: (i, OFF_Z // 512 + j)),
              ((SSD_N_GROUPS, 1, 512), F32, (SSD_N_GROUPS, 1, 512), lambda i, j, k: (0, 0, 0))])
    g_ssd_norm = g_ssd_norm.reshape(1, SSD_D_INNER)
    g_w_so = _mm_plain("g_w_so", gn, d_yssd, tm=1024, tn=1024, tk=tkt, trans_a=True, out_dtype=BF16)
    d_xc, d_dtg, d_alog, d_dd, *rode['ffn'] = _ssd_bwd(xc, dtg, dtg_t, alog, alog_t, d_exp, d_y, hs, ride=ride_ffn)
    d_pre, g_ssd_conv_b = _conv_silu_bwd1(d_xc, xc_pre, tm=tm)
    d_proj, g_ssd_conv_w = _conv_bwd2("ssd_conv_bwd2", d_pre, proj, OFF_XBC, ps['ssd_conv_w'], tm=tm, tc=1536,
                                      out_cols=PROJ_W, out_col0=OFF_XBC, fill=d_proj)
    d_proj, g_dt_bias = _dt_bwd(d_dtg, proj, bias_pad, d_proj, tm=tmw)

    d_ao = _mm_plain("d_ao", d_yattn, wb['ao_t'], tm=tmw, tn=512, tk=1024, out_dtype=BF16)
    g_w_ao = _mm_plain("g_w_ao", ao, d_yattn, tm=1024, tn=1024, tk=tkt, trans_a=True, out_dtype=BF16)
    ride_mix = rides('mix', {'ssd_w_out': g_w_so, 'attn_w_out': g_w_ao, 'w_mix_out': g_w_mix})
    d_proj, g_sinks, *rode['mix'] = _attn_bwd(qr, kp, vp, kt, d_ao, ao, lse, sinks, cos_t, sin_t, d_proj, ride=ride_mix)

    def dx_ep(acc, i, j, ins, outs):
        xv, wn, dx1v = ins
        d_n, dw = _rms_bwd(xv[...], wn[...], acc)
        outs[0][...] = dx1v[...] + d_n
        _accumulate(outs[1], i == 0, dw)

    g_cat_t = _mm_plain("g_w_in", d_proj, u, tm=1024, tn=1024, tk=tkt, trans_a=True, out_dtype=BF16)
    grad_x, g_norm_mix_pre, *rode['w_in'] = _mm_call(
        "d_u", d_proj, wb['cat_t'], tm=tmw, tn=D_MODEL, tk=1152, epilogue=dx_ep, ride=rides('w_in', {'w_in': g_cat_t}),
        vmem_mb=VMEM_BIG_MB,
        extra_in=[(x, (tmw, D_MODEL), i0), (ps['norm_mix_pre_w'], (1, D_MODEL), c0), (d_x1, (tmw, D_MODEL), i0)],
        outs=[((t, D_MODEL), F32, (tmw, D_MODEL), i0), ((1, D_MODEL), F32, (1, D_MODEL), c0)])

    grads = {
        'norm_mix_pre_w': g_norm_mix_pre, 'w_in': g_cat_t, 'ssd_conv_w': g_ssd_conv_w, 'ssd_conv_b': g_ssd_conv_b,
        'ssd_dt_bias': g_dt_bias[:, :SSD_N_HEADS], 'ssd_a_log': d_alog[:, 0, :8].reshape(1, SSD_N_HEADS),
        'ssd_d': d_dd[:, 0, :8].reshape(1, SSD_N_HEADS), 'ssd_norm_w': g_ssd_norm, 'ssd_w_out': g_w_so,
        'attn_sinks': g_sinks[:, :ATTN_N_HEADS], 'attn_w_out': g_w_ao, 'w_mix_out': g_w_mix,
        'norm_mix_post_w': g_norm_mix_post, 'norm_ffn_pre_w': g_norm_ffn_pre, 'ffn_w_up': g_w_up_t,
        'ffn_conv_w': g_ffn_conv_w, 'ffn_conv_b': jnp.concatenate([db_g, db_v], axis=1), 'ffn_w_down': g_w_down,
        'norm_ffn_post_w': g_norm_ffn_post,
    }
    return loss, grad_x, grads, rode


def _group_channels(a):
    parts = []
    for g in range(SSD_N_GROUPS):
        parts += [a[..., 512 * g:512 * (g + 1)], a[..., 2048 + 128 * g:2048 + 128 * (g + 1)],
                  a[..., 2560 + 128 * g:2560 + 128 * (g + 1)]]
    return jnp.concatenate(parts, axis=-1)


def _ungroup_channels(a):
    xs = [a[..., GROUP_W * g:GROUP_W * g + 512] for g in range(SSD_N_GROUPS)]
    bs = [a[..., GROUP_W * g + 512:GROUP_W * g + 640] for g in range(SSD_N_GROUPS)]
    cs = [a[..., GROUP_W * g + 640:GROUP_W * (g + 1)] for g in range(SSD_N_GROUPS)]
    return jnp.concatenate(xs + bs + cs, axis=-1)


def _proj_rows(a_t, lo, hi):
    out = []
    for start, length, dst in sorted(PROJ_SEGS):
        s, e = max(lo, start), min(hi, start + length)
        if s < e:
            out.append(a_t[dst + s - start:dst + e - start])
    return out


def _to_proj_layout(w_in_t):
    pieces, pos = [], 0
    for start, length, dst in sorted(PROJ_SEGS, key=lambda s: s[2]):
        if dst > pos:
            pieces.append(jnp.zeros((dst - pos, w_in_t.shape[1]), w_in_t.dtype))
        pieces.append(w_in_t[start:start + length])
        pos = dst + length
    if pos < PROJ_W:
        pieces.append(jnp.zeros((PROJ_W - pos, w_in_t.shape[1]), w_in_t.dtype))
    return jnp.concatenate(pieces, axis=0)


def _rope_tables(positions):
    half = 32
    inv_freq = ROPE_THETA ** (-jnp.arange(half, dtype=F32) * 2.0 / 64)
    ang = positions.astype(F32)[:, None] * inv_freq
    cos, sin = jnp.cos(ang), jnp.sin(ang)
    return jnp.concatenate([cos, cos, cos, cos], axis=1), jnp.concatenate([-sin, sin, -sin, sin], axis=1)


def _matmul_weights(w_in_t):
    cat_t = _to_proj_layout(w_in_t)
    return {'cat': cat_t.T, 'cat_t': cat_t}


def _late_weights(so, ao, mix, up_t, dn):
    return {'so': so, 'so_t': so.T, 'ao': ao, 'ao_t': ao.T, 'mix': mix, 'mix_t': mix.T,
            'up': up_t.T, 'up_t': up_t, 'dn': dn, 'dn_t': dn.T}


ANY = pl.BlockSpec(memory_space=pl.ANY)
MESH = pl.DeviceIdType.MESH
ROW_ALIGN = 32


def _mesh_pos():
    return lax.axis_index("x"), lax.axis_index("y"), lax.axis_index("c")


def _other_chips(x, y):
    return [(1 - x, y), (x, 1 - y), (1 - x, 1 - y)]


def _remote(src, dst, send_sems, recv_sems, k, to):
    return pltpu.make_async_remote_copy(src_ref=src, dst_ref=dst, send_sem=send_sems.at[k], recv_sem=recv_sems.at[k],
                                        device_id=to, device_id_type=MESH)


def _half(c, rh):
    return pl.ds(pl.multiple_of(c * rh, 16), rh)


def _ag_ride(shard):
    r = shard.shape[0]
    rh = r // 2

    def first_copies(w_ref, out_ref, send_sems, recv_sems):
        x, y, c = _mesh_pos()
        p = 2 * x + y
        mine = _half(c, rh)
        cps = [_remote(w_ref, out_ref.at[p], send_sems, recv_sems, 6, (x, y, 1 - c))]
        return cps + [_remote(w_ref.at[mine], out_ref.at[p, mine], send_sems, recv_sems, j, (cx, cy, c))
                      for j, (cx, cy) in enumerate(_other_chips(x, y))]

    def start(ins, outs, send_sems, recv_sems):
        for cp in first_copies(ins[0], outs[0], send_sems, recv_sems):
            cp.start()

    def forwards(out_ref, send_sems, recv_sems, half):
        x, y, c = _mesh_pos()
        return [_remote(out_ref.at[2 * cx + cy, half], out_ref.at[2 * cx + cy, half], send_sems, recv_sems, 3 + j, (x, y, 1 - c))
                for j, (cx, cy) in enumerate(_other_chips(x, y))]

    def middle(ins, outs, send_sems, recv_sems):
        x, y, c = _mesh_pos()
        mine = _half(c, rh)
        for j, (fwd, (cx, cy)) in enumerate(zip(forwards(outs[0], send_sems, recv_sems, mine), _other_chips(x, y))):
            slab = outs[0].at[2 * cx + cy, mine]
            _remote(slab, slab, send_sems, recv_sems, j, (x, y, 1 - c)).wait_recv()
            fwd.start()

    def finish(ins, outs, send_sems, recv_sems):
        w_ref, out_ref = ins[0], outs[0]
        x, y, c = _mesh_pos()
        for cp in forwards(out_ref, send_sems, recv_sems, _half(1 - c, rh)):
            cp.wait_recv()
        _remote(w_ref, out_ref.at[2 * x + y], send_sems, recv_sems, 6, (x, y, 1 - c)).wait_recv()
        for cp in first_copies(w_ref, out_ref, send_sems, recv_sems) + forwards(out_ref, send_sems, recv_sems, _half(c, rh)):
            cp.wait_send()

    return _Ride((shard,), (SDS((N_CHIPS, r, COMM_LANES), shard.dtype),), 7, start, finish, middle)


def _rs_ride(gbuf):
    rh = gbuf.shape[1] // 2

    def copies(g_ref, r_ref, send_sems, recv_sems, landing):
        x, y, c = _mesh_pos()
        cps = []
        for k, (cx, cy) in enumerate(_other_chips(x, y)):
            for h in range(2):
                slot = 2 * k + c if landing else 2 * k + h
                cps.append(pltpu.make_async_remote_copy(
                    src_ref=g_ref.at[2 * cx + cy, pl.ds(h * rh, rh)], dst_ref=r_ref.at[slot],
                    send_sem=send_sems.at[2 * k + h], recv_sem=recv_sems.at[slot],
                    device_id=(cx, cy, h), device_id_type=MESH))
        cps.append(_remote(g_ref.at[2 * x + y, _half(1 - c, rh)], r_ref.at[6], send_sems, recv_sems, 6, (x, y, 1 - c)))
        return cps

    def start(ins, outs, send_sems, recv_sems):
        for cp in copies(ins[0], outs[0], send_sems, recv_sems, True):
            cp.start()

    def finish(ins, outs, send_sems, recv_sems):
        for cp in copies(ins[0], outs[0], send_sems, recv_sems, False):
            cp.wait()

    return _Ride((gbuf,), (SDS((7, rh, COMM_LANES), gbuf.dtype),), 7, start, finish)


def _rs_sum(name, gbuf, got, pc_idx):
    rh = got.shape[1]
    tr = max(d for d in range(16, 513, 16) if rh % d == 0)
    nb = rh // tr

    def body(pc_ref, own_ref, *refs):
        o_ref = refs[7]
        p, c = pc_ref[0], pc_ref[1]
        own = own_ref[0].astype(F32)
        slots = [r[0].astype(F32) for r in refs[:7]]

        def term(q, h):
            code = p ^ q
            far = jnp.where(code == 2, slots[h], jnp.where(code == 1, slots[2 + h], slots[4 + h]))
            return jnp.where(code == 0, jnp.where(c == h, own, slots[6]), far)

        acc = term(0, 0)
        for q, h in [(0, 1), (1, 0), (1, 1), (2, 0), (2, 1), (3, 0), (3, 1)]:
            acc = acc + term(q, h)
        o_ref[0] = acc

    slot = lambda s: pl.BlockSpec((1, tr, COMM_LANES), lambda i, pc: (s, i, 0))
    return pl.pallas_call(
        body, name=name,
        grid_spec=pltpu.PrefetchScalarGridSpec(
            num_scalar_prefetch=1, grid=(nb,),
            in_specs=[pl.BlockSpec((1, tr, COMM_LANES), lambda i, pc: (pc[0], pc[1] * nb + i, 0))] + [slot(s) for s in range(7)],
            out_specs=pl.BlockSpec((1, tr, COMM_LANES), lambda i, pc: (pc[1], i, 0))),
        out_shape=SDS((2, rh, COMM_LANES), F32), compiler_params=_cp(dimension_semantics=("arbitrary",)),
    )(pc_idx, gbuf, *([got] * 7))


def _pair_gather_all(bufs):
    n = len(bufs)

    def body(*refs):
        outs, send_sems, recv_sems = refs[n:2 * n], refs[2 * n], refs[2 * n + 1]
        x, y, c = _mesh_pos()
        cps = [_remote(o.at[c], o.at[c], send_sems, recv_sems, k, (x, y, 1 - c)) for k, o in enumerate(outs)]
        for cp in cps:
            cp.start()
        for k, o in enumerate(outs):
            _remote(o.at[1 - c], o.at[1 - c], send_sems, recv_sems, k, (x, y, 1 - c)).wait_recv()
        for cp in cps:
            cp.wait_send()

    return pl.pallas_call(
        body, name="grad_pair_gather", in_specs=[ANY] * n, out_specs=[ANY] * n,
        out_shape=[SDS(b.shape, b.dtype) for b in bufs],
        scratch_shapes=[pltpu.SemaphoreType.DMA((n,)), pltpu.SemaphoreType.DMA((n,))],
        input_output_aliases={k: k for k in range(n)},
    )(*bufs)


def _pack_rows(big, small=()):
    parts = list(big)
    if small:
        flat = jnp.concatenate([p.reshape(-1) for p in small])
        k = -(-flat.shape[0] // (16 * COMM_LANES)) * 16
        parts.append(jnp.pad(flat, (0, k * COMM_LANES - flat.shape[0])).reshape(k, COMM_LANES))
    pad = -sum(p.shape[0] for p in parts) % ROW_ALIGN
    if pad:
        parts.append(jnp.zeros((pad, COMM_LANES), parts[0].dtype))
    return jnp.concatenate(parts, axis=0) if len(parts) > 1 else parts[0]


def _take(flat, off, shape):
    n = 1
    for d in shape:
        n *= d
    return flat[off:off + n].reshape(shape), off + n


BIG_ROWS = {'w_in': 2184, 'ssd_w_out': 512, 'attn_w_out': 256, 'w_mix_out': 256, 'ffn_w_up': 1408, 'ffn_w_down': 704}
TRANSPOSED = ('w_in', 'ffn_w_up')
LATE = ('ssd_w_out', 'attn_w_out', 'w_mix_out', 'ffn_w_up', 'ffn_w_down')
CONV_TAPS = ('ssd_conv_w', 'ffn_conv_w')
RS_GROUPS = {'ffn': ('ffn_w_up', 'ffn_w_down'), 'mix': ('ssd_w_out', 'attn_w_out', 'w_mix_out'), 'w_in': ('w_in',)}


def _exchange(name, ride):
    n_in, n_out = len(ride.ins), len(ride.outs)

    def body(*refs):
        ins, outs, sems = refs[:n_in], refs[n_in:n_in + n_out], refs[n_in + n_out:]
        ride.start(ins, outs, *sems)
        if ride.middle is not None:
            ride.middle(ins, outs, *sems)
        ride.finish(ins, outs, *sems)

    return pl.pallas_call(
        body, name=name, in_specs=[ANY] * n_in, out_specs=[ANY] * n_out, out_shape=list(ride.outs),
        scratch_shapes=[pltpu.SemaphoreType.DMA((ride.n_sems,)), pltpu.SemaphoreType.DMA((ride.n_sems,))],
    )(*ride.ins)


def kernel(x, positions, norm_mix_pre_w, w_in, ssd_conv_w, ssd_conv_b, ssd_dt_bias, ssd_a_log, ssd_d, ssd_norm_w, ssd_w_out, attn_sinks, attn_w_out, w_mix_out, norm_mix_post_w, norm_ffn_pre_w, ffn_w_up, ffn_conv_w, ffn_conv_b, ffn_w_down, norm_ffn_post_w, loss_target, m_norm_mix_pre_w, m_w_in, m_ssd_conv_w, m_ssd_conv_b, m_ssd_dt_bias, m_ssd_a_log, m_ssd_d, m_ssd_norm_w, m_ssd_w_out, m_attn_sinks, m_attn_w_out, m_w_mix_out, m_norm_mix_post_w, m_norm_ffn_pre_w, m_ffn_w_up, m_ffn_conv_w, m_ffn_conv_b, m_ffn_w_down, m_norm_ffn_post_w, v_norm_mix_pre_w, v_w_in, v_ssd_conv_w, v_ssd_conv_b, v_ssd_dt_bias, v_ssd_a_log, v_ssd_d, v_ssd_norm_w, v_ssd_w_out, v_attn_sinks, v_attn_w_out, v_w_mix_out, v_norm_mix_post_w, v_norm_ffn_pre_w, v_ffn_w_up, v_ffn_conv_w, v_ffn_conv_b, v_ffn_w_down, v_norm_ffn_post_w):
    given = dict(locals())
    w = {n: given[n][0] for n in WEIGHTS}
    w = {n: (a if a.ndim == 2 else a[None]) for n, a in w.items()}
    mom_m = {n: given['m_' + n].reshape(w[n].shape) for n in WEIGHTS}
    mom_v = {n: given['v_' + n].reshape(w[n].shape) for n in WEIGHTS}
    cx, cy, cc = _mesh_pos()
    pc_idx = jnp.stack([2 * cx + cy, cc]).astype(jnp.int32)

    rows_of = lambda n: (w[n].T if n in TRANSPOSED else w[n]).astype(BF16)
    gathered = _exchange("w_in_all_gather", _ag_ride(_pack_rows([rows_of('w_in')])))[0]
    wb = _matmul_weights(jnp.concatenate([gathered[s, :BIG_ROWS['w_in']] for s in range(N_CHIPS)], axis=0))
    taps = [lax.bitcast_convert_type(w[n], BF16) for n in CONV_TAPS]

    def unpack_late(arrived):
        rows, conv = {n: [] for n in LATE}, {n: [] for n in CONV_TAPS}
        for s in range(N_CHIPS):
            r0 = 0
            for n in LATE:
                rows[n].append(arrived[0][s, r0:r0 + BIG_ROWS[n]])
                r0 += BIG_ROWS[n]
            flat, off = arrived[0][s, r0:r0 + 16].reshape(-1), 0
            for n in CONV_TAPS:
                a, off = _take(flat, off, w[n].shape + (2,))
                conv[n].append(lax.bitcast_convert_type(a, F32))
        full = {n: jnp.concatenate(rows[n], axis=0) for n in LATE}
        return (_late_weights(*[full[n] for n in LATE]),
                {'ssd_conv_w': _group_channels(jnp.concatenate(conv['ssd_conv_w'], axis=1)),
                 'ffn_conv_w': jnp.concatenate(conv['ffn_conv_w'], axis=1)})

    late = (_ag_ride(_pack_rows([rows_of(n) for n in LATE], taps)), unpack_late)

    sent = {}

    def rides(group, g):
        parts = []
        for s in range(N_CHIPS):
            slab = []
            for n in RS_GROUPS[group]:
                lo, hi = BIG_ROWS[n] * s, BIG_ROWS[n] * (s + 1)
                slab += _proj_rows(g[n], lo, hi) if n == 'w_in' else [g[n][lo:hi]]
            slab = [a.astype(BF16) for a in slab]
            pad = -sum(a.shape[0] for a in slab) % ROW_ALIGN
            parts += slab + ([jnp.zeros((pad, COMM_LANES), BF16)] if pad else [])
        sent[group] = jnp.concatenate(parts, axis=0).reshape(N_CHIPS, -1, COMM_LANES)
        return _rs_ride(sent[group])

    ps = {n: w[n] for n in REPLICATED}
    ps['ssd_conv_b'] = _group_channels(w['ssd_conv_b'])
    cos_t, sin_t = _rope_tables(positions[0])
    loss, grad_x, grads, rode = _local_step(x[0], cos_t, sin_t, loss_target[0], wb, ps, late, rides)
    grads['ssd_conv_w'] = _ungroup_channels(grads['ssd_conv_w'])
    grads['ssd_conv_b'] = _ungroup_channels(grads['ssd_conv_b'])

    shard_cols = {n: sh[1] for n, _, sh in SHARDED}
    parts = []
    for s in range(N_CHIPS):
        small = [grads[n][:, shard_cols[n] * s:shard_cols[n] * (s + 1)] for n in CONV_TAPS] + [grads[n] for n in REPLICATED]
        flat = _pack_rows([], small)
        high = flat.astype(BF16)
        parts += [high, (flat - high.astype(F32)).astype(BF16)]
    sent['small'] = jnp.concatenate(parts, axis=0).reshape(N_CHIPS, -1, COMM_LANES)
    rode['small'] = _exchange("grad_small_exchange", _rs_ride(sent['small']))

    groups = ('ffn', 'mix', 'w_in', 'small')
    red = _pair_gather_all([_rs_sum("grad_sum_" + g, sent[g], rode[g][0], pc_idx) for g in groups])
    red = {g: r.reshape(-1, COMM_LANES) for g, r in zip(groups, red)}
    g_red = {}
    for g in groups[:3]:
        r0 = 0
        for n in RS_GROUPS[g]:
            g_red[n] = red[g][r0:r0 + BIG_ROWS[n]].T if n in TRANSPOSED else red[g][r0:r0 + BIG_ROWS[n]]
            r0 += BIG_ROWS[n]
    half = red['small'].shape[0] // 2
    flat, off = (red['small'][:half] + red['small'][half:]).reshape(-1), 0
    for n in CONV_TAPS + REPLICATED:
        g_red[n], off = _take(flat, off, w[n].shape)

    small_names = [n for n in WEIGHTS if n not in MATMUL_WEIGHTS]
    delta, new_m, new_v = {}, {}, {}
    for n in MATMUL_WEIGHTS:
        delta[n], new_m[n], new_v[n] = _adamw("adamw_" + n, w[n], g_red[n], mom_m[n], mom_v[n],
                                                  tr=max(d for d in range(8, 353, 8) if w[n].shape[0] % d == 0))
    packed = [_pack_small([d[n] for n in small_names]) for d in (w, g_red, mom_m, mom_v)]
    outs = _adamw("adamw_small", *packed, tr=packed[0].shape[0])
    for res, o in zip((delta, new_m, new_v), outs):
        fl, off = o.reshape(-1), 0
        for n in small_names:
            res[n], off = _take(fl, off, w[n].shape)

    loss_all = lax.psum(loss[0, 0], ("x", "y", "c"))
    shaped = lambda d: [d[n].reshape(given[n].shape) for n in WEIGHTS]
    return (loss_all, grad_x[None], *shaped(g_red), *shaped(delta), *shaped(new_m), *shaped(new_v))


def _pack_small(pieces):
    flat = jnp.concatenate([p.reshape(-1) for p in pieces])
    rows = -(-flat.shape[0] // (128 * 8)) * 8
    return jnp.pad(flat, (0, rows * 128 - flat.shape[0])).reshape(rows, 128)
```

```python
from typing import Callable, NamedTuple

import jax
import jax.numpy as jnp
from jax import lax
from jax.experimental import pallas as pl
from jax.experimental.pallas import tpu as pltpu

F32 = jnp.float32
BF16 = jnp.bfloat16
SDS = jax.ShapeDtypeStruct
HIGHEST = lax.Precision.HIGHEST

D_MODEL = 1024
SSD_D_INNER = 2048
SSD_N_HEADS = 32
SSD_HEAD_DIM = 64
SSD_N_GROUPS = 4
SSD_HEADS_PER_GROUP = 8
SSD_D_STATE = 128
SSD_CONV_DIM = 3072
CHUNK = 128
ATTN_N_HEADS = 16
KV_WIDTH = 256
FFN_D_FF = 2816
IN_PROJ_DIM = 8736
ROPE_THETA = 10000.0
NORM_EPS = 1e-6
ADAM_LR, ADAM_B1, ADAM_B2, ADAM_EPS, ADAM_WD, ADAM_STEP = 0.001, 0.9, 0.999, 1e-08, 0.01, 10

PROJ_W = 9216
OFF_Q, OFF_K, OFF_V, OFF_Z, OFF_DT, OFF_GS, OFF_GA, OFF_XBC = 0, 1024, 1280, 1536, 3584, 4096, 5120, 6144
GROUP_W = 768
PROJ_SEGS = ([(0, 2048, OFF_Z)]
             + [(2048 + 512 * g, 512, OFF_XBC + GROUP_W * g) for g in range(4)]
             + [(4096 + 128 * g, 128, OFF_XBC + GROUP_W * g + 512) for g in range(4)]
             + [(4608 + 128 * g, 128, OFF_XBC + GROUP_W * g + 640) for g in range(4)]
             + [(5120, 32, OFF_DT), (5152, 1024, OFF_Q), (6176, 256, OFF_K), (6432, 256, OFF_V),
                (6688, 1024, OFF_GS), (7712, 1024, OFF_GA)])
VMEM_LIMIT_MB = 48
VMEM_BIG_MB = 60
NEG = -1e30

WEIGHTS = ('norm_mix_pre_w', 'w_in', 'ssd_conv_w', 'ssd_conv_b', 'ssd_dt_bias', 'ssd_a_log', 'ssd_d', 'ssd_norm_w',
           'ssd_w_out', 'attn_sinks', 'attn_w_out', 'w_mix_out', 'norm_mix_post_w', 'norm_ffn_pre_w', 'ffn_w_up',
           'ffn_conv_w', 'ffn_conv_b', 'ffn_w_down', 'norm_ffn_post_w')
SHARDED = (('w_in', 1, (1024, 2184)), ('ssd_conv_w', 1, (4, 768)), ('ssd_w_out', 0, (512, 1024)),
           ('attn_w_out', 0, (256, 1024)), ('w_mix_out', 0, (256, 1024)), ('ffn_w_up', 1, (1024, 1408)),
           ('ffn_conv_w', 1, (3, 1408)), ('ffn_w_down', 0, (704, 1024)))
MATMUL_WEIGHTS = ('w_in', 'ssd_w_out', 'attn_w_out', 'w_mix_out', 'ffn_w_up', 'ffn_w_down')
REPLICATED = tuple(n for n in WEIGHTS if n not in {s[0] for s in SHARDED})
N_CHIPS = 4
COMM_LANES = 1024


def _cp(vmem_mb=VMEM_LIMIT_MB, **kw):
    return pltpu.CompilerParams(vmem_limit_bytes=vmem_mb << 20, **kw)


class _Ride(NamedTuple):
    ins: tuple
    outs: tuple
    n_sems: int
    start: Callable
    finish: Callable
    middle: Callable = None


def _ride_parts(ride):
    if ride is None:
        return [], [], [], [], []
    hbm = pl.BlockSpec(memory_space=pl.ANY)
    return (list(ride.ins), [hbm] * len(ride.ins), list(ride.outs), [hbm] * len(ride.outs),
            [pltpu.SemaphoreType.DMA((ride.n_sems,)), pltpu.SemaphoreType.DMA((ride.n_sems,))])


def _ride_run(ride, first, last, in_refs, out_refs, sems, middle=None):
    if ride is None:
        return

    @pl.when(first)
    def _():
        ride.start(in_refs, out_refs, *sems)

    if ride.middle is not None:
        @pl.when(last if middle is None else middle)
        def _():
            ride.middle(in_refs, out_refs, *sems)

    @pl.when(last)
    def _():
        ride.finish(in_refs, out_refs, *sems)


def _iota(shape, axis):
    return lax.broadcasted_iota(jnp.int32, shape, axis)


def _sigmoid(v):
    return 1.0 / (1.0 + jnp.exp(-v))


def _mm_call(name, a, b, *, tm, tn, tk, epilogue, outs, extra_in=(), trans_a=False, fill=None, ride=None,
             vmem_mb=VMEM_LIMIT_MB):
    if trans_a:
        kdim, m = a.shape
    else:
        m, kdim = a.shape
    n = b.shape[1]
    assert b.shape[0] == kdim and m % tm == 0 and n % tn == 0 and kdim % tk == 0, (name, a.shape, b.shape, tm, tn, tk)
    gi, gj, gk = m // tm, n // tn, kdim // tk
    n_in, n_out = len(extra_in), len(outs)

    n_fill = 0 if fill is None else 1
    r_ops, r_in_specs, r_outs, r_out_specs, r_scratch = _ride_parts(ride)

    def body(a_ref, b_ref, *rest):
        ins = rest[:n_in]
        rest = rest[n_in + n_fill:]
        r_in, rest = rest[:len(r_ops)], rest[len(r_ops):]
        out_refs, rest = rest[:n_out], rest[n_out:]
        r_out, scratch = rest[:len(r_outs)], rest[len(r_outs):]
        i, j, k = pl.program_id(0), pl.program_id(1), pl.program_id(2)
        _ride_run(ride, (i == 0) & (j == 0) & (k == 0), (i == gi - 1) & (j == gj - 1) & (k == gk - 1),
                  r_in, r_out, scratch[-2:])
        av = a_ref[...].astype(BF16)
        bv = b_ref[...].astype(BF16)
        if trans_a:
            part = lax.dot_general(av, bv, (((0,), (0,)), ((), ())), preferred_element_type=F32)
        else:
            part = jnp.dot(av, bv, preferred_element_type=F32)
        if gk == 1:
            epilogue(part, i, j, ins, out_refs)
        else:
            acc = scratch[0]

            @pl.when(k == 0)
            def _():
                acc[...] = part

            @pl.when(k > 0)
            def _():
                acc[...] += part

            @pl.when(k == gk - 1)
            def _():
                epilogue(acc[...], i, j, ins, out_refs)

    a_spec = pl.BlockSpec((tk, tm), lambda i, j, k: (k, i)) if trans_a else pl.BlockSpec((tm, tk), lambda i, j, k: (i, k))
    if gj == 1 and gk == 1:
        b_spec = pl.BlockSpec((tk, tn), lambda i, j, k: (0, 0), pipeline_mode=pl.Buffered(1))
    else:
        b_spec = pl.BlockSpec((tk, tn), lambda i, j, k: (k, j))
    in_specs = [a_spec, b_spec]
    in_specs += [pl.BlockSpec(bs, im) for _, bs, im in extra_in]
    operands = [a, b] + [e[0] for e in extra_in]
    aliases = {}
    if fill is not None:
        in_specs.append(pl.BlockSpec(memory_space=pl.ANY))
        aliases = {len(operands): fill[1]}
        operands.append(fill[0])
    return pl.pallas_call(
        body, name=name, grid=(gi, gj, gk), in_specs=in_specs + r_in_specs,
        out_specs=[pl.BlockSpec(bs, im) for _, _, bs, im in outs] + r_out_specs,
        out_shape=[SDS(s, d) for s, d, _, _ in outs] + r_outs,
        scratch_shapes=([pltpu.VMEM((tm, tn), F32)] if gk > 1 else []) + r_scratch,
        input_output_aliases=aliases,
        compiler_params=_cp(vmem_mb, dimension_semantics=("arbitrary", "arbitrary", "arbitrary")),
    )(*operands, *r_ops)


def _mm_plain(name, a, b, *, tm, tn, tk, out_dtype=F32, trans_a=False):
    m = a.shape[1] if trans_a else a.shape[0]

    def epilogue(acc, i, j, ins, outs):
        outs[0][...] = acc.astype(out_dtype)

    return _mm_call(name, a, b, tm=tm, tn=tn, tk=tk, epilogue=epilogue, trans_a=trans_a,
                    outs=[((m, b.shape[1]), out_dtype, (tm, tn), lambda i, j, k: (i, j))])[0]


def _accumulate(ref, first, value):
    @pl.when(first)
    def _():
        ref[...] = value

    @pl.when(jnp.logical_not(first))
    def _():
        ref[...] += value


def _rms_bwd(xv, w, dy):
    r = lax.rsqrt(jnp.mean(xv * xv, axis=-1, keepdims=True) + NORM_EPS)
    xn = xv * r
    dxh = dy * w
    dx = r * (dxh - xn * jnp.mean(dxh * xn, axis=-1, keepdims=True))
    return dx, jnp.sum(dy * xn, axis=0, keepdims=True)


def _norm_mm(name, x, wn, w, *, tm, tn, ride=None):
    t, dm = x.shape
    n = w.shape[1]
    tm = min(tm, t)
    gi, gj = t // tm, n // tn
    r_ops, r_in_specs, r_outs, r_out_specs, r_scratch = _ride_parts(ride)

    def body(x_ref, wn_ref, w_ref, *rest):
        r_in, rest = rest[:len(r_ops)], rest[len(r_ops):]
        o_ref, u_ref = rest[:2]
        r_out, sems = rest[2:2 + len(r_outs)], rest[2 + len(r_outs):]
        i, j = pl.program_id(0), pl.program_id(1)
        _ride_run(ride, (i == 0) & (j == 0), (i == gi - 1) & (j == gj - 1), r_in, r_out, sems,
                  middle=(i == (3 * gi) // 4) & (j == 0) if gi > 1 else None)

        @pl.when(j == 0)
        def _():
            xv = x_ref[...]
            r = lax.rsqrt(jnp.mean(xv * xv, axis=-1, keepdims=True) + NORM_EPS)
            u_ref[...] = (xv * r * wn_ref[...]).astype(BF16)

        o_ref[...] = jnp.dot(u_ref[...], w_ref[...], preferred_element_type=F32)

    return pl.pallas_call(
        body, name=name, grid=(gi, gj),
        in_specs=[pl.BlockSpec((tm, dm), lambda i, j: (i, 0)), pl.BlockSpec((1, dm), lambda i, j: (0, 0)),
                  pl.BlockSpec((dm, tn), lambda i, j: (0, j))] + r_in_specs,
        out_specs=[pl.BlockSpec((tm, tn), lambda i, j: (i, j)), pl.BlockSpec((tm, dm), lambda i, j: (i, 0))] + r_out_specs,
        out_shape=[SDS((t, n), F32), SDS((t, dm), BF16)] + r_outs, scratch_shapes=r_scratch,
        compiler_params=_cp(dimension_semantics=("arbitrary", "arbitrary")),
    )(x, wn, w, *r_ops)


def _shift_down(tile, halo, s):
    if s == 0:
        return tile
    r = pltpu.roll(tile, s, axis=0)
    h = pltpu.roll(halo, s, axis=0)
    head = jnp.where(_iota(h.shape, 0) < s, h, r[0:8])
    return jnp.concatenate([head, r[8:]], axis=0)


def _shift_up(tile, halo, s):
    if s == 0:
        return tile
    n = tile.shape[0]
    r = pltpu.roll(tile, n - s, axis=0)
    h = pltpu.roll(halo, 8 - s, axis=0)
    tail = jnp.where(_iota(h.shape, 0) >= 8 - s, h, r[n - 8:])
    return jnp.concatenate([r[:n - 8], tail], axis=0)


def _conv_apply(tile, halo, wv, bv, kw):
    acc = bv + wv[kw - 1:kw, :] * tile
    for k in range(kw - 1):
        acc = acc + wv[k:k + 1, :] * _shift_down(tile, halo, kw - 1 - k)
    return acc


def _prev_halo_spec(tm, tc, col0):
    return pl.BlockSpec((8, tc), lambda i, j: (jnp.maximum(i * (tm // 8) - 1, 0), col0 + j))


def _silu_parts(pre):
    sg = _sigmoid(pre)
    return pre * sg, sg * (1.0 + pre * (1.0 - sg))


def _conv_silu_fwd(proj, w, b, *, tm, tc=1536):
    t = proj.shape[0]
    c = w.shape[1]
    tm = min(tm, t)
    col0 = OFF_XBC // tc

    def body(x_ref, h_ref, w_ref, b_ref, o_ref, pre_ref):
        halo = jnp.where(pl.program_id(0) > 0, h_ref[...], 0.0)
        pre = _conv_apply(x_ref[...], halo, w_ref[...], b_ref[...], 4)
        o_ref[...] = _silu_parts(pre)[0]
        pre_ref[...] = pre.astype(BF16)

    tile = pl.BlockSpec((tm, tc), lambda i, j: (i, j))
    return pl.pallas_call(
        body, name="ssd_conv_fwd", grid=(t // tm, c // tc),
        in_specs=[pl.BlockSpec((tm, tc), lambda i, j: (i, col0 + j)), _prev_halo_spec(tm, tc, col0),
                  pl.BlockSpec((4, tc), lambda i, j: (0, j)), pl.BlockSpec((1, tc), lambda i, j: (0, j))],
        out_specs=[tile, tile], out_shape=[SDS((t, c), F32), SDS((t, c), BF16)],
        compiler_params=_cp(dimension_semantics=("arbitrary", "arbitrary")),
    )(proj, proj, w, b)


def _conv_silu_bwd1(d_out, pre, *, tm, tc=1536):
    t, c = pre.shape
    tm = min(tm, t)

    def body(g_ref, p_ref, o_ref, db_ref):
        i = pl.program_id(1)
        d_pre = g_ref[...] * _silu_parts(p_ref[...].astype(F32))[1]
        o_ref[...] = d_pre.astype(BF16)
        _accumulate(db_ref, i == 0, jnp.sum(d_pre, axis=0, keepdims=True))

    tile = pl.BlockSpec((tm, tc), lambda j, i: (i, j))
    return pl.pallas_call(
        body, name="ssd_conv_bwd1", grid=(c // tc, t // tm), in_specs=[tile, tile],
        out_specs=[tile, pl.BlockSpec((1, tc), lambda j, i: (0, j))],
        out_shape=[SDS((t, c), BF16), SDS((1, c), F32)],
        compiler_params=_cp(dimension_semantics=("arbitrary", "arbitrary")),
    )(d_out, pre)


def _conv_bwd2(name, d_pre, src, src_col0, w, *, tm, tc, out_cols, out_col0, fill=None):
    t, c = d_pre.shape
    kw = w.shape[0]
    tm = min(tm, t)
    ni = t // tm
    col0 = src_col0 // tc
    ocol0 = out_col0 // tc

    def body(g_ref, gn_ref, x_ref, w_ref, *rest):
        o_ref, dw_ref = rest[-2:]
        i = pl.program_id(1)
        g = g_ref[...].astype(F32)
        g_next = jnp.where(i < ni - 1, gn_ref[...].astype(F32)[0:8], 0.0)
        xv = x_ref[...]
        wv = w_ref[...]
        shifted = [_shift_up(g, g_next, kw - 1 - k) for k in range(kw)]
        d_in = wv[0:1, :] * shifted[0]
        for k in range(1, kw):
            d_in = d_in + wv[k:k + 1, :] * shifted[k]
        o_ref[...] = d_in.astype(o_ref.dtype)
        rows = [jnp.sum(shifted[k] * xv, axis=0, keepdims=True) for k in range(kw)]

        @pl.when(i == 0)
        def _():
            for k in range(kw):
                dw_ref[k:k + 1, :] = rows[k]

        @pl.when(i > 0)
        def _():
            for k in range(kw):
                dw_ref[k:k + 1, :] += rows[k]

    in_specs = [pl.BlockSpec((tm, tc), lambda j, i: (i, j)),
                pl.BlockSpec((16, tc), lambda j, i: (jnp.minimum((i + 1) * (tm // 16), t // 16 - 1), j)),
                pl.BlockSpec((tm, tc), lambda j, i: (i, col0 + j)),
                pl.BlockSpec((kw, tc), lambda j, i: (0, j))]
    operands = [d_pre, d_pre, src, w]
    if fill is not None:
        in_specs.append(pl.BlockSpec(memory_space=pl.ANY))
        operands.append(fill)
    return pl.pallas_call(
        body, name=name, grid=(c // tc, ni), in_specs=in_specs,
        out_specs=[pl.BlockSpec((tm, tc), lambda j, i: (i, ocol0 + j)), pl.BlockSpec((kw, tc), lambda j, i: (0, j))],
        out_shape=[SDS((t, out_cols), BF16), SDS((kw, c), F32)],
        input_output_aliases={} if fill is None else {4: 0},
        compiler_params=_cp(dimension_semantics=("arbitrary", "arbitrary")),
    )(*operands)


GELU_C = 0.7978845608028654


def _gelu_parts(v):
    inner = GELU_C * (v + 0.044715 * v * v * v)
    th = jnp.tanh(inner)
    val = 0.5 * v * (1.0 + th)
    grad = 0.5 * (1.0 + th) + 0.5 * v * (1.0 - th * th) * GELU_C * (1.0 + 3.0 * 0.044715 * v * v)
    return val, grad


def _ffn_act_fwd(up_raw, w, b, *, tm, tc=1408):
    t = up_raw.shape[0]
    tm = min(tm, t)
    nj = FFN_D_FF // tc
    halo = lambda i: jnp.maximum(i * (tm // 8) - 1, 0)

    def body(g_ref, gh_ref, v_ref, vh_ref, wg_ref, wv_ref, bg_ref, bv_ref, o_ref, gate_ref, val_ref):
        first = pl.program_id(0) > 0
        gate = _conv_apply(g_ref[...], jnp.where(first, gh_ref[...], 0.0), wg_ref[...], bg_ref[...], 3)
        val = _conv_apply(v_ref[...], jnp.where(first, vh_ref[...], 0.0), wv_ref[...], bv_ref[...], 3)
        o_ref[...] = (_gelu_parts(gate)[0] * val).astype(BF16)
        gate_ref[...] = gate.astype(BF16)
        val_ref[...] = val.astype(BF16)

    tile = pl.BlockSpec((tm, tc), lambda i, j: (i, j))
    return pl.pallas_call(
        body, name="ffn_act_fwd", grid=(t // tm, nj),
        in_specs=[tile, pl.BlockSpec((8, tc), lambda i, j: (halo(i), j)),
                  pl.BlockSpec((tm, tc), lambda i, j: (i, nj + j)), pl.BlockSpec((8, tc), lambda i, j: (halo(i), nj + j)),
                  pl.BlockSpec((3, tc), lambda i, j: (0, j)), pl.BlockSpec((3, tc), lambda i, j: (0, nj + j)),
                  pl.BlockSpec((1, tc), lambda i, j: (0, j)), pl.BlockSpec((1, tc), lambda i, j: (0, nj + j))],
        out_specs=[tile] * 3, out_shape=[SDS((t, FFN_D_FF), BF16)] * 3,
        compiler_params=_cp(dimension_semantics=("arbitrary", "arbitrary")),
    )(up_raw, up_raw, up_raw, up_raw, w, w, b, b)


def _ffn_act_bwd(gate, val, d_act, *, tm, tc=1408):
    t = gate.shape[0]
    tm = min(tm, t)
    nj = FFN_D_FF // tc

    def body(g_ref, v_ref, da_ref, dg_ref, dv_ref, dbg_ref, dbv_ref):
        i = pl.program_id(1)
        val = v_ref[...].astype(F32)
        ge, dge = _gelu_parts(g_ref[...].astype(F32))
        da = da_ref[...].astype(F32)
        d_gate = da * val * dge
        d_val = da * ge
        dg_ref[...] = d_gate.astype(BF16)
        dv_ref[...] = d_val.astype(BF16)
        _accumulate(dbg_ref, i == 0, jnp.sum(d_gate, axis=0, keepdims=True))
        _accumulate(dbv_ref, i == 0, jnp.sum(d_val, axis=0, keepdims=True))

    tile = pl.BlockSpec((tm, tc), lambda j, i: (i, j))
    row = pl.BlockSpec((1, tc), lambda j, i: (0, j))
    return pl.pallas_call(
        body, name="ffn_act_bwd", grid=(nj, t // tm), in_specs=[tile] * 3, out_specs=[tile, tile, row, row],
        out_shape=[SDS((t, FFN_D_FF), BF16), SDS((t, FFN_D_FF), BF16), SDS((1, FFN_D_FF), F32), SDS((1, FFN_D_FF), F32)],
        compiler_params=_cp(dimension_semantics=("arbitrary", "arbitrary")),
    )(gate, val, d_act)


def _softplus(v):
    e = jnp.exp(-jnp.abs(v))
    small = e * (1.0 - 0.5 * e)
    return jnp.maximum(v, 0.0) + jnp.where(e < 1e-4, small, jnp.log(1.0 + e))


def _dt_fwd(proj, bias_pad, *, tm):
    t = proj.shape[0]
    tm = min(tm, t)

    def body(x_ref, b_ref, g_ref, gt_ref):
        dt = _softplus(x_ref[...] + b_ref[...])
        first8 = _iota((tm, 128), 1) < 8
        for g in range(SSD_N_GROUPS):
            dg = jnp.where(first8, dt if g == 0 else pltpu.roll(dt, 128 - 8 * g, axis=1), 0.0)
            g_ref[g] = dg
            gt_ref[g] = dg.T[0:8, :]

    return pl.pallas_call(
        body, name="dt_fwd", grid=(t // tm,),
        in_specs=[pl.BlockSpec((tm, 128), lambda i: (i, OFF_DT // 128)), pl.BlockSpec((1, 128), lambda i: (0, 0))],
        out_specs=[pl.BlockSpec((SSD_N_GROUPS, tm, 128), lambda i: (0, i, 0)), pl.BlockSpec((SSD_N_GROUPS, 8, tm), lambda i: (0, 0, i))],
        out_shape=[SDS((SSD_N_GROUPS, t, 128), F32), SDS((SSD_N_GROUPS, 8, t), F32)],
        compiler_params=_cp(dimension_semantics=("arbitrary",)),
    )(proj, bias_pad)


def _dt_bwd(d_dtg, proj, bias_pad, d_proj, *, tm):
    t = proj.shape[0]
    tm = min(tm, t)

    def body(g_ref, x_ref, b_ref, _, o_ref, db_ref):
        first8 = _iota((tm, 128), 1) < 8
        d_dt = jnp.where(first8, g_ref[0], 0.0)
        for g in range(1, SSD_N_GROUPS):
            d_dt = d_dt + pltpu.roll(jnp.where(first8, g_ref[g], 0.0), 8 * g, axis=1)
        d_raw = d_dt * _sigmoid(x_ref[...] + b_ref[...])
        o_ref[:, 0:128] = d_raw.astype(BF16)
        o_ref[:, 128:512] = jnp.zeros((tm, 384), BF16)
        _accumulate(db_ref, pl.program_id(0) == 0, jnp.sum(d_raw, axis=0, keepdims=True))

    return pl.pallas_call(
        body, name="dt_bwd", grid=(t // tm,),
        in_specs=[pl.BlockSpec((SSD_N_GROUPS, tm, 128), lambda i: (0, i, 0)), pl.BlockSpec((tm, 128), lambda i: (i, OFF_DT // 128)),
                  pl.BlockSpec((1, 128), lambda i: (0, 0)), pl.BlockSpec(memory_space=pl.ANY)],
        out_specs=[pl.BlockSpec((tm, 512), lambda i: (i, OFF_DT // 512)), pl.BlockSpec((1, 128), lambda i: (0, 0))],
        out_shape=[SDS((t, PROJ_W), BF16), SDS((1, 128), F32)],
        input_output_aliases={3: 0},
        compiler_params=_cp(dimension_semantics=("arbitrary",)),
    )(d_dtg, proj, bias_pad, d_proj)


def _split3(v):
    hi = v.astype(BF16)
    r1 = v - hi.astype(F32)
    mid = r1.astype(BF16)
    return hi, mid, (r1 - mid.astype(F32)).astype(BF16)


def _times01(v, m3):
    return jnp.dot(jnp.concatenate(_split3(v), axis=1), m3, preferred_element_type=F32)


def _01times(m3, v):
    return jnp.dot(m3, jnp.concatenate(_split3(v), axis=0), preferred_element_type=F32)


def _ssd_decay(dt_ref, dtT_ref, al_ref, alT_ref, k):
    dt = dt_ref[0]
    a_row = -jnp.exp(al_ref[0])
    adt_t = dtT_ref[0] * (-jnp.exp(alT_ref[0]))
    return dt, a_row, _01times(k['low3'][...], dt * a_row), _times01(adt_t, k['up3v'][...])


def _ssd_specs(nc, rev):
    ci = (lambda c: nc - 1 - c) if rev else (lambda c: c)
    return [pl.BlockSpec((CHUNK, SSD_CONV_DIM), lambda c: (ci(c), 0)),
            pl.BlockSpec((SSD_N_GROUPS, CHUNK, 128), lambda c: (0, ci(c), 0)),
            pl.BlockSpec((SSD_N_GROUPS, 8, CHUNK), lambda c: (0, 0, ci(c))),
            pl.BlockSpec((SSD_N_GROUPS, 1, 128), lambda c: (0, 0, 0)),
            pl.BlockSpec((SSD_N_GROUPS, 8, 1), lambda c: (0, 0, 0)),
            pl.BlockSpec((1, SSD_D_INNER), lambda c: (0, 0))]


def _ssd_group_views(g, x_ref, dt_ref, dtT_ref, al_ref, alT_ref, d_ref):
    return (x_ref.at[:, g * GROUP_W:(g + 1) * GROUP_W], dt_ref.at[g:g + 1], dtT_ref.at[g:g + 1], al_ref.at[g:g + 1],
            alT_ref.at[g:g + 1], d_ref.at[:, g * 512:(g + 1) * 512])


NT = (((1,), (1,)), ((), ()))
WIDE = 8 * CHUNK
SSD_CONST_NAMES = ('e128', 'e64', 's64', 'mlo', 'mup', 'low3', 'up3', 'up3v')
SSD_CONST_SHAPES = [pltpu.VMEM((3 * CHUNK, WIDE), BF16), pltpu.VMEM((3 * CHUNK, 512), BF16), pltpu.VMEM((512, CHUNK), BF16),
                    pltpu.VMEM((CHUNK, WIDE), F32), pltpu.VMEM((CHUNK, WIDE), F32), pltpu.VMEM((CHUNK, 3 * CHUNK), BF16),
                    pltpu.VMEM((CHUNK, 3 * CHUNK), BF16), pltpu.VMEM((3 * CHUNK, CHUNK), BF16)]


def _ssd_init_consts(k):
    row, col = _iota((3 * CHUNK, WIDE), 0), _iota((3 * CHUNK, WIDE), 1)
    k['e128'][...] = ((col >> 7) == (row & 127)).astype(BF16)
    k['e64'][...] = ((_iota((3 * CHUNK, 512), 1) >> 6) == (_iota((3 * CHUNK, 512), 0) & 127)).astype(BF16)
    k['s64'][...] = ((_iota((512, CHUNK), 0) >> 6) == _iota((512, CHUNK), 1)).astype(BF16)
    row, col = _iota((CHUNK, WIDE), 0), _iota((CHUNK, WIDE), 1)
    k['mlo'][...] = (row >= (col & 127)).astype(F32)
    k['mup'][...] = (row <= (col & 127)).astype(F32)
    row, col = _iota((CHUNK, 3 * CHUNK), 0), _iota((CHUNK, 3 * CHUNK), 1) & 127
    k['low3'][...] = (row >= col).astype(BF16)
    k['up3'][...] = (row <= col).astype(BF16)
    row, col = _iota((3 * CHUNK, CHUNK), 0) & 127, _iota((3 * CHUNK, CHUNK), 1)
    k['up3v'][...] = (row <= col).astype(BF16)


def _ssd_common(x_ref, dt_ref, dtT_ref, al_ref, alT_ref, k):
    dt, a_row, acs, acs_t = _ssd_decay(dt_ref, dtT_ref, al_ref, alT_ref, k)
    ecol = _times01(acs, k['e128'][...])
    rrow = jnp.concatenate([jnp.broadcast_to(acs_t[j:j + 1, :], (CHUNK, CHUNK)) for j in range(8)], axis=1)
    a64 = _times01(acs, k['e64'][...])
    dt64 = _times01(dt, k['e64'][...])
    a_end64 = a64[CHUNK - 1:CHUNK, :]
    xs = x_ref[:, 0:512]
    return dict(dt=dt, a_row=a_row, acs=acs, seg=ecol - rrow, dt64=dt64, e_a=jnp.exp(a64), decay=jnp.exp(a_end64 - a64),
                e_end64=jnp.exp(a_end64), xs=xs, xdt=xs * dt64, bm=x_ref[:, 512:640], cm=x_ref[:, 640:768])


def _pair_blocks(v):
    lo = _iota((CHUNK, 128), 1) < 64
    out = []
    for i in range(4):
        ch = v[:, i * 128:(i + 1) * 128]
        out.append(jnp.concatenate([jnp.where(lo, ch, 0.0), jnp.where(lo, 0.0, ch)], axis=0).astype(BF16))
    return out


def _tile8(m):
    return jnp.concatenate([m] * 8, axis=1)


def _ssd_fwd(xc, dtg, dtg_t, alog, alog_t, d_exp):
    t = xc.shape[0]
    nc = t // CHUNK

    def body(xa_ref, dta_ref, dtTa_ref, ala_ref, alTa_ref, da_ref, ya_ref, hs_ref, h_scr, *consts):
        c = pl.program_id(0)
        k = dict(zip(SSD_CONST_NAMES, consts))

        @pl.when(c == 0)
        def _():
            _ssd_init_consts(k)
            h_scr[...] = jnp.zeros_like(h_scr)

        for g in range(SSD_N_GROUPS):
            x_ref, dt_ref, dtT_ref, al_ref, alT_ref, d_ref = _ssd_group_views(g, xa_ref, dta_ref, dtTa_ref, ala_ref, alTa_ref, da_ref)
            v = _ssd_common(x_ref, dt_ref, dtT_ref, al_ref, alT_ref, k)
            b16, c16 = v['bm'].astype(BF16), v['cm'].astype(BF16)
            cb = lax.dot_general(c16, b16, NT, preferred_element_type=F32)
            m16 = (jnp.exp(jnp.minimum(v['seg'], 0.0)) * k['mlo'][...] * _tile8(cb)).astype(BF16)
            xbd = _pair_blocks(v['xdt'])
            y_diag = jnp.concatenate([jnp.dot(m16[:, i * 256:(i + 1) * 256], xbd[i], preferred_element_type=F32)
                                      for i in range(4)], axis=1)
            ht = h_scr[g]
            y_off = jnp.dot(c16, ht.astype(BF16), preferred_element_type=F32)
            ya_ref[:, g * 512:(g + 1) * 512] = y_diag + v['e_a'] * y_off + d_ref[...] * v['xs']
            st = jnp.dot(v['bm'].T.astype(BF16), (v['xdt'] * v['decay']).astype(BF16), preferred_element_type=F32)
            hs_ref[0, g] = ht
            h_scr[g] = ht * v['e_end64'] + st

    return pl.pallas_call(
        body, name="ssd_fwd", grid=(nc,), in_specs=_ssd_specs(nc, False),
        out_specs=[pl.BlockSpec((CHUNK, SSD_D_INNER), lambda c: (c, 0)),
                   pl.BlockSpec((1, SSD_N_GROUPS, SSD_D_STATE, 512), lambda c: (c, 0, 0, 0))],
        out_shape=[SDS((t, SSD_D_INNER), F32), SDS((nc, SSD_N_GROUPS, SSD_D_STATE, 512), F32)],
        scratch_shapes=[pltpu.VMEM((SSD_N_GROUPS, SSD_D_STATE, 512), F32)] + SSD_CONST_SHAPES,
        compiler_params=_cp(dimension_semantics=("arbitrary",)),
    )(xc, dtg, dtg_t, alog, alog_t, d_exp)


def _ssd_bwd(xc, dtg, dtg_t, alog, alog_t, d_exp, d_y, hs, ride=None):
    t = xc.shape[0]
    nc = t // CHUNK

    r_ops, r_in_specs, r_outs, r_out_specs, r_scratch = _ride_parts(ride)

    def body(xa_ref, dta_ref, dtTa_ref, ala_ref, alTa_ref, da_ref, dya_ref, hs_ref, *rest):
        r_in, rest = rest[:len(r_ops)], rest[len(r_ops):]
        dxa_ref, ddta_ref, dal_ref, dd_ref = rest[:4]
        r_out, rest = rest[4:4 + len(r_outs)], rest[4 + len(r_outs):]
        g_scr, consts, sems = rest[0], rest[1:1 + len(SSD_CONST_NAMES)], rest[1 + len(SSD_CONST_NAMES):]
        c = pl.program_id(0)
        _ride_run(ride, c == 0, c == nc - 1, r_in, r_out, sems)
        k = dict(zip(SSD_CONST_NAMES, consts))

        @pl.when(c == 0)
        def _():
            _ssd_init_consts(k)
            g_scr[...] = jnp.zeros_like(g_scr)

        for g in range(SSD_N_GROUPS):
            views = _ssd_group_views(g, xa_ref, dta_ref, dtTa_ref, ala_ref, alTa_ref, da_ref)
            one_group(c, g, k, *views, dya_ref.at[:, g * 512:(g + 1) * 512], hs_ref, g_scr,
                      dxa_ref.at[:, g * GROUP_W:(g + 1) * GROUP_W], ddta_ref.at[g:g + 1], dal_ref, dd_ref)

    def one_group(c, g, k, x_ref, dt_ref, dtT_ref, al_ref, alT_ref, d_ref, dy_ref, hs_ref, g_scr, dx_ref, ddt_ref,
                  dal_ref, dd_ref):
        s64, mlo, mup = k['s64'], k['mlo'], k['mup']
        v = _ssd_common(x_ref, dt_ref, dtT_ref, al_ref, alT_ref, k)
        dt, a_row, xs, xdt, e_a, decay = v['dt'], v['a_row'], v['xs'], v['xdt'], v['e_a'], v['decay']
        row, col = _iota((CHUNK, CHUNK), 0), _iota((CHUNK, CHUNK), 1)
        b16, c16 = v['bm'].astype(BF16), v['cm'].astype(BF16)
        ct16 = v['cm'].T.astype(BF16)
        cb = lax.dot_general(c16, b16, NT, preferred_element_type=F32)
        cbt = lax.dot_general(b16, c16, NT, preferred_element_type=F32)
        lmat = jnp.exp(jnp.minimum(v['seg'], 0.0)) * mlo[...]
        lmat_t = jnp.exp(jnp.minimum(-v['seg'], 0.0)) * mup[...]
        mmat, mmat_t = lmat * _tile8(cb), lmat_t * _tile8(cbt)
        mt16 = mmat_t.astype(BF16)
        dy = dy_ref[...]
        dye, xdec = dy * e_a, xdt * decay
        dy16, dye16, xdec16 = dy.astype(BF16), dye.astype(BF16), xdec.astype(BF16)
        xdt16 = xdt.astype(BF16)
        ht, gt = hs_ref[0, g], g_scr[g]
        ht16, gt16 = ht.astype(BF16), gt.astype(BF16)
        xbd, dybd = _pair_blocks(xdt), _pair_blocks(dy)
        d_m, d_mt, d_x = [], [], []
        for i in range(4):
            csl = slice(i * 128, (i + 1) * 128)
            d_m.append(lax.dot_general(dy16[:, csl], xbd[i], NT, preferred_element_type=F32))
            d_mt.append(lax.dot_general(xdt16[:, csl], dybd[i], NT, preferred_element_type=F32))
            d_x.append(jnp.dot(mt16[:, i * 256:(i + 1) * 256], dybd[i], preferred_element_type=F32))
        d_m, d_mt, d_x = jnp.concatenate(d_m, axis=1), jnp.concatenate(d_mt, axis=1), jnp.concatenate(d_x, axis=1)

        def head_sum(m):
            acc = m[:, 0:CHUNK]
            for j in range(1, 8):
                acc = acc + m[:, j * CHUNK:(j + 1) * CHUNK]
            return acc

        def seg64(p):
            return jnp.dot(p.astype(BF16), s64[...], preferred_element_type=F32)

        d_cb16 = head_sum(d_m * lmat).astype(BF16)
        d_cbt16 = head_sum(d_mt * lmat_t).astype(BF16)
        dseg = d_m * mmat - d_mt * mmat_t
        da_seg = jnp.zeros((CHUNK, CHUNK), F32)
        for j in range(8):
            da_seg = jnp.where(col == j, jnp.sum(dseg[:, j * CHUNK:(j + 1) * CHUNK], axis=1, keepdims=True), da_seg)
        ch = jnp.dot(c16, ht16, preferred_element_type=F32)
        bg = jnp.dot(b16, gt16, preferred_element_type=F32)
        d_x = d_x + decay * bg
        d_decay = seg64(xdec * bg)
        e_end = jnp.exp(v['acs'][CHUNK - 1:CHUNK, :])
        d_end = e_end * jnp.sum(seg64(gt * ht), axis=0, keepdims=True) + jnp.sum(d_decay, axis=0, keepdims=True)
        d_a = seg64(dye * ch) - d_decay + da_seg + jnp.where(row == CHUNK - 1, d_end, 0.0)
        dx_ref[:, 0:512] = d_x * v['dt64'] + d_ref[...] * dy
        dx_ref[:, 640:768] = (lax.dot_general(dye16, ht16, NT, preferred_element_type=F32)
                              + jnp.dot(d_cb16, b16, preferred_element_type=F32))
        dx_ref[:, 512:640] = (lax.dot_general(xdec16, gt16, NT, preferred_element_type=F32)
                              + jnp.dot(d_cbt16, c16, preferred_element_type=F32))
        g_scr[g] = gt * v['e_end64'] + jnp.dot(ct16, dye16, preferred_element_type=F32)
        d_adt = _01times(k['up3'][...], d_a)
        ddt_ref[0] = d_adt * a_row + seg64(d_x * xs)
        d_alog = jnp.sum(d_adt * dt, axis=0, keepdims=True) * a_row
        dd_row = jnp.sum(seg64(dy * xs), axis=0, keepdims=True)
        first = c == 0

        @pl.when(first)
        def _():
            dal_ref[g] = d_alog
            dd_ref[g] = dd_row

        @pl.when(jnp.logical_not(first))
        def _():
            dal_ref[g] += d_alog
            dd_ref[g] += dd_row

    rc = lambda c: nc - 1 - c
    whole = pl.BlockSpec((SSD_N_GROUPS, 1, 128), lambda c: (0, 0, 0))
    return pl.pallas_call(
        body, name="ssd_bwd", grid=(nc,),
        in_specs=_ssd_specs(nc, True) + [pl.BlockSpec((CHUNK, SSD_D_INNER), lambda c: (rc(c), 0)),
                                        pl.BlockSpec((1, SSD_N_GROUPS, SSD_D_STATE, 512), lambda c: (rc(c), 0, 0, 0))] + r_in_specs,
        out_specs=[pl.BlockSpec((CHUNK, SSD_CONV_DIM), lambda c: (rc(c), 0)),
                   pl.BlockSpec((SSD_N_GROUPS, CHUNK, 128), lambda c: (0, rc(c), 0)), whole, whole] + r_out_specs,
        out_shape=[SDS((t, SSD_CONV_DIM), F32), SDS((SSD_N_GROUPS, t, 128), F32),
                   SDS((SSD_N_GROUPS, 1, 128), F32), SDS((SSD_N_GROUPS, 1, 128), F32)] + r_outs,
        scratch_shapes=[pltpu.VMEM((SSD_N_GROUPS, SSD_D_STATE, 512), F32)] + SSD_CONST_SHAPES + r_scratch,
        compiler_params=_cp(dimension_semantics=("arbitrary",)),
    )(xc, dtg, dtg_t, alog, alog_t, d_exp, d_y, hs, *r_ops)


def _gated_norm_fwd(y, proj, w, *, tm):
    t = y.shape[0]
    tm = min(tm, t)

    def body(y_ref, z_ref, w_ref, o_ref):
        gv = y_ref[...] * _silu_parts(z_ref[...])[0]
        r = lax.rsqrt(jnp.mean(gv * gv, axis=-1, keepdims=True) + NORM_EPS)
        o_ref[...] = (gv * r * w_ref[...]).astype(BF16)

    tile = pl.BlockSpec((tm, 512), lambda i, g: (i, g))
    return pl.pallas_call(
        body, name="gated_norm_fwd", grid=(t // tm, SSD_N_GROUPS),
        in_specs=[tile, pl.BlockSpec((tm, 512), lambda i, g: (i, OFF_Z // 512 + g)),
                  pl.BlockSpec((1, 512), lambda i, g: (0, g))], out_specs=tile,
        out_shape=SDS((t, SSD_D_INNER), BF16),
        compiler_params=_cp(dimension_semantics=("arbitrary", "arbitrary")),
    )(y, proj, w)


def _rope(ch, cos_t, sin_t):
    first = (_iota(ch.shape, 1) & 32) == 0
    partner = jnp.where(first, pltpu.roll(ch, 96, axis=1), pltpu.roll(ch, 32, axis=1))
    return ch * cos_t + partner * sin_t


def _rope_qkv(proj, cos_t, sin_t, *, tm):
    t = proj.shape[0]
    tm = min(tm, t)

    def body(q_ref, k_ref, v_ref, c_ref, s_ref, qr_ref, kp_ref, vp_ref, kt_ref, vt_ref):
        cv, sv = c_ref[...], s_ref[...]
        lo = _iota((tm, 128), 1) < 64
        for m in range(8):
            sl = slice(m * 128, (m + 1) * 128)
            qr_ref[:, sl] = (_rope(q_ref[:, sl], cv, sv) * 0.125).astype(BF16)
        for m2 in range(2):
            sl = slice(m2 * 128, (m2 + 1) * 128)
            for src, dst, dst_t in ((_rope(k_ref[:, sl], cv, sv), kp_ref, kt_ref), (v_ref[:, sl], vp_ref, vt_ref)):
                sw = pltpu.roll(src, 64, axis=1)
                padded = (jnp.where(lo, src, 0.0), jnp.where(lo, 0.0, sw), jnp.where(lo, sw, 0.0), jnp.where(lo, 0.0, src))
                for i, pad in enumerate(padded):
                    rows = slice((4 * m2 + i) * 128, (4 * m2 + i + 1) * 128)
                    dst[:, rows] = pad.astype(BF16)
                    dst_t[rows, :] = pad.T.astype(BF16)

    return pl.pallas_call(
        body, name="rope_qkv", grid=(t // tm,),
        in_specs=[pl.BlockSpec((tm, 1024), lambda i: (i, OFF_Q // 1024)), pl.BlockSpec((tm, 256), lambda i: (i, OFF_K // 256)),
                  pl.BlockSpec((tm, 256), lambda i: (i, OFF_V // 256)), pl.BlockSpec((tm, 128), lambda i: (i, 0)),
                  pl.BlockSpec((tm, 128), lambda i: (i, 0))],
        out_specs=[pl.BlockSpec((tm, 1024), lambda i: (i, 0))] * 3 + [pl.BlockSpec((1024, tm), lambda i: (0, i))] * 2,
        out_shape=[SDS((t, 1024), BF16)] * 3 + [SDS((1024, t), BF16)] * 2,
        compiler_params=_cp(dimension_semantics=("arbitrary",)),
    )(proj, proj, proj, cos_t, sin_t)


def _attn_valid(n):
    kj, qi = _iota((2 * CHUNK, CHUNK), 0), _iota((2 * CHUNK, CHUNK), 1)
    return (kj > qi) & (kj <= qi + CHUNK) & ((n > 0) | (kj >= CHUNK))


def _attn_fwd(qr, kp, vt, sinks):
    t = qr.shape[0]
    nb = t // CHUNK

    def body(q_ref, kc_ref, kprev_ref, vc_ref, vprev_ref, sk_ref, o_ref, lse_ref):
        n = pl.program_id(0)
        valid = _attn_valid(n)
        head_row = _iota((16, CHUNK), 0)
        lse_all = jnp.zeros((16, CHUNK), F32)
        for m in range(8):
            g = m // 2
            qch = q_ref[:, m * 128:(m + 1) * 128]
            sls = [slice((2 * g + e) * 128, (2 * g + e + 1) * 128) for e in range(2)]
            kk2 = jnp.concatenate([r[:, sl] for sl in sls for r in (kprev_ref, kc_ref)], axis=0)
            vv2_t = jnp.concatenate([r[sl, :] for sl in sls for r in (vprev_ref, vc_ref)], axis=1)
            s2 = lax.dot_general(kk2, qch, NT, preferred_element_type=F32)
            probs = []
            for e in range(2):
                h = 2 * m + e
                s = jnp.where(valid, s2[2 * CHUNK * e:2 * CHUNK * (e + 1)], NEG)
                sink = sk_ref[0:1, h:h + 1]
                mx = jnp.maximum(jnp.max(s, axis=0, keepdims=True), sink)
                p = jnp.exp(s - mx)
                den = jnp.sum(p, axis=0, keepdims=True) + jnp.exp(sink - mx)
                probs.append((p * (1.0 / den)).astype(BF16))
                lse_all = jnp.where(head_row == h, mx + jnp.log(den), lse_all)
            o_t = jnp.dot(vv2_t, jnp.concatenate(probs, axis=0), preferred_element_type=F32)
            o_ref[:, m * 128:(m + 1) * 128] = o_t.T.astype(BF16)
        lse_ref[0] = lse_all

    cur = pl.BlockSpec((CHUNK, 1024), lambda n: (n, 0))
    prev = pl.BlockSpec((CHUNK, 1024), lambda n: (jnp.maximum(n - 1, 0), 0))
    cur_t = pl.BlockSpec((1024, CHUNK), lambda n: (0, n))
    prev_t = pl.BlockSpec((1024, CHUNK), lambda n: (0, jnp.maximum(n - 1, 0)))
    return pl.pallas_call(
        body, name="attn_fwd", grid=(nb,),
        in_specs=[cur, cur, prev, cur_t, prev_t, pl.BlockSpec((1, 128), lambda n: (0, 0))],
        out_specs=[cur, pl.BlockSpec((1, 16, CHUNK), lambda n: (n, 0, 0))],
        out_shape=[SDS((t, 1024), BF16), SDS((nb, 16, CHUNK), F32)],
        compiler_params=_cp(dimension_semantics=("arbitrary",)),
    )(qr, kp, kp, vt, vt, sinks)


def _attn_bwd(qr, kp, vp, kt, d_o, o, lse, sinks, cos_t, sin_t, d_proj, ride=None):
    t = qr.shape[0]
    nb = t // CHUNK

    r_ops, r_in_specs, r_outs, r_out_specs, r_scratch = _ride_parts(ride)

    def body(q_ref, kc_ref, kprev_ref, vc_ref, vprev_ref, ktc_ref, ktprev_ref, do_ref, o_ref, lse_ref, sk_ref,
             c_ref, s_ref, cp_ref, sp_ref, _, *rest):
        r_in, rest = rest[:len(r_ops)], rest[len(r_ops):]
        dqkv_ref, dsk_ref = rest[:2]
        r_out, rest = rest[2:2 + len(r_outs)], rest[2 + len(r_outs):]
        acc_k, acc_v, dq_scr = rest[:3]
        n = pl.program_id(0)
        _ride_run(ride, n == 0, n == nb, r_in, r_out, rest[3:])
        lane = _iota((CHUNK, 128), 1)
        lo = lane < 64
        lane1 = _iota((1, 128), 1)

        @pl.when(n == 0)
        def _():
            acc_k[...] = jnp.zeros_like(acc_k)
            acc_v[...] = jnp.zeros_like(acc_v)
            dsk_ref[...] = jnp.zeros((1, 128), F32)

        @pl.when(n > 0)
        def _():
            dqkv_ref[:, 0:1024] = dq_scr[...]
            for r in range(8):
                acc_k[r, 0:CHUNK] = acc_k[r, CHUNK:2 * CHUNK]
                acc_v[r, 0:CHUNK] = acc_v[r, CHUNK:2 * CHUNK]
                acc_k[r, CHUNK:2 * CHUNK] = jnp.zeros((CHUNK, 128), F32)
                acc_v[r, CHUNK:2 * CHUNK] = jnp.zeros((CHUNK, 128), F32)

        @pl.when(n < nb)
        def _():
            valid = _attn_valid(n)
            lse_all = lse_ref[0]
            dsk = jnp.zeros((1, 128), F32)
            for m in range(8):
                g = m // 2
                csl = slice(m * 128, (m + 1) * 128)
                qch = q_ref[:, csl]
                doch = do_ref[:, csl]
                prod_t = (doch.astype(F32) * o_ref[:, csl].astype(F32)).T
                sls = [slice((2 * g + e) * 128, (2 * g + e + 1) * 128) for e in range(2)]
                kk2 = jnp.concatenate([r[:, sl] for sl in sls for r in (kprev_ref, kc_ref)], axis=0)
                vv2 = jnp.concatenate([r[:, sl] for sl in sls for r in (vprev_ref, vc_ref)], axis=0)
                kk2_t = jnp.concatenate([r[sl, :] for sl in sls for r in (ktprev_ref, ktc_ref)], axis=1)
                s2 = lax.dot_general(kk2, qch, NT, preferred_element_type=F32)
                d_p2 = lax.dot_general(vv2, doch, NT, preferred_element_type=F32)
                ps, d_ss = [], []
                for e in range(2):
                    h = 2 * m + e
                    rows = slice(2 * CHUNK * e, 2 * CHUNK * (e + 1))
                    lse_h = lse_all[h:h + 1, :]
                    p = jnp.exp(jnp.where(valid, s2[rows], NEG) - lse_h)
                    delta = jnp.sum(prod_t[64 * e:64 * (e + 1)], axis=0, keepdims=True)
                    ps.append(p.astype(BF16))
                    d_ss.append((p * (d_p2[rows] - delta)).astype(BF16))
                    p_sink = jnp.exp(sk_ref[0:1, h:h + 1] - lse_h)
                    dsk = jnp.where(lane1 == h, -jnp.sum(p_sink * delta), dsk)
                d_s2, p2 = jnp.concatenate(d_ss, axis=0), jnp.concatenate(ps, axis=0)
                d_k2 = jnp.dot(d_s2, qch, preferred_element_type=F32)
                d_v2 = jnp.dot(p2, doch, preferred_element_type=F32)
                for e in range(2):
                    rows = slice(2 * CHUNK * e, 2 * CHUNK * (e + 1))
                    acc_k[2 * g + e] += d_k2[rows]
                    acc_v[2 * g + e] += d_v2[rows]
                dq_t = jnp.dot(kk2_t, d_s2, preferred_element_type=F32)
                dq_scr[:, csl] = (_rope(dq_t.T, c_ref[...], -s_ref[...]) * 0.125).astype(BF16)
            dsk_ref[...] += dsk

        @pl.when(n > 0)
        def _():
            for m2 in range(2):
                halves = []
                for g in (2 * m2, 2 * m2 + 1):
                    for acc in (acc_k, acc_v):
                        comb = jnp.where(lo, acc[2 * g, 0:CHUNK], acc[2 * g + 1, 0:CHUNK])
                        halves.append(comb + pltpu.roll(comb, 64, axis=1))
                d_kr = jnp.where(lo, halves[0], halves[2])
                d_v = jnp.where(lo, halves[1], halves[3])
                dqkv_ref[:, OFF_K + m2 * 128:OFF_K + (m2 + 1) * 128] = _rope(d_kr, cp_ref[...], -sp_ref[...]).astype(BF16)
                dqkv_ref[:, OFF_V + m2 * 128:OFF_V + (m2 + 1) * 128] = d_v.astype(BF16)

    qn = lambda n: jnp.minimum(n, nb - 1)
    pn = lambda n: jnp.maximum(jnp.minimum(n, nb) - 1, 0)
    cur = pl.BlockSpec((CHUNK, 1024), lambda n: (qn(n), 0))
    prev = pl.BlockSpec((CHUNK, 1024), lambda n: (pn(n), 0))
    cur128 = pl.BlockSpec((CHUNK, 128), lambda n: (qn(n), 0))
    prev128 = pl.BlockSpec((CHUNK, 128), lambda n: (pn(n), 0))
    cur_t = pl.BlockSpec((1024, CHUNK), lambda n: (0, qn(n)))
    prev_t = pl.BlockSpec((1024, CHUNK), lambda n: (0, pn(n)))
    one = pl.BlockSpec((1, 128), lambda n: (0, 0))
    return pl.pallas_call(
        body, name="attn_bwd", grid=(nb + 1,),
        in_specs=[cur, cur, prev, cur, prev, cur_t, prev_t, cur, cur, pl.BlockSpec((1, 16, CHUNK), lambda n: (qn(n), 0, 0)),
                  one, cur128, cur128, prev128, prev128, pl.BlockSpec(memory_space=pl.ANY)] + r_in_specs,
        out_specs=[pl.BlockSpec((CHUNK, 1536), lambda n: (pn(n), 0)), one] + r_out_specs,
        out_shape=[SDS((t, PROJ_W), BF16), SDS((1, 128), F32)] + r_outs,
        scratch_shapes=[pltpu.VMEM((8, 2 * CHUNK, 128), F32), pltpu.VMEM((8, 2 * CHUNK, 128), F32),
                        pltpu.VMEM((CHUNK, 1024), BF16)] + r_scratch,
        input_output_aliases={15: 0},
        compiler_params=_cp(dimension_semantics=("arbitrary",)),
    )(qr, kp, kp, vp, vp, kt, kt, d_o, o, lse, sinks, cos_t, sin_t, cos_t, sin_t, d_proj, *r_ops)


def _adamw(name, w, g, m, v, *, tr):
    rows, cols = w.shape
    tr = min(tr, rows)
    assert rows % tr == 0

    def body(w_ref, g_ref, m_ref, v_ref, d_ref, nm_ref, nv_ref):
        gv = g_ref[...]
        nm = ADAM_B1 * m_ref[...] + (1.0 - ADAM_B1) * gv
        nv = ADAM_B2 * v_ref[...] + (1.0 - ADAM_B2) * (gv * gv)
        m_hat = nm / (1.0 - ADAM_B1 ** ADAM_STEP)
        v_hat = nv / (1.0 - ADAM_B2 ** ADAM_STEP)
        d_ref[...] = -ADAM_LR * (m_hat / (jnp.sqrt(v_hat) + ADAM_EPS) + ADAM_WD * w_ref[...])
        nm_ref[...] = nm
        nv_ref[...] = nv

    tile = pl.BlockSpec((tr, cols), lambda i: (i, 0))
    return pl.pallas_call(
        body, name=name, grid=(rows // tr,), in_specs=[tile] * 4, out_specs=[tile] * 3,
        out_shape=[SDS((rows, cols), F32)] * 3, compiler_params=_cp(dimension_semantics=("arbitrary",)),
    )(w, g, m, v)


def _local_step(x, cos_t, sin_t, tgt, wb, ps, late=None, rides=None):
    t = x.shape[0]
    tm = min(512, t)
    tmw = min(1024, t)
    ij = lambda i, j, k: (i, j)
    i0 = lambda i, j, k: (i, 0)
    c0 = lambda i, j, k: (0, 0)
    cj = lambda i, j, k: (0, j)
    rides = rides or (lambda group, grads: None)
    rode = {}

    tkt = min(2048, t)
    proj, u, *arrived = _norm_mm("in_proj", x, ps['norm_mix_pre_w'], wb['cat'], tm=tmw, tn=1024,
                                 ride=late[0] if late else None)
    if late:
        more_wb, more_ps = late[1](arrived)
        wb, ps = {**wb, **more_wb}, {**ps, **more_ps}
    xc, xc_pre = _conv_silu_fwd(proj, ps['ssd_conv_w'], ps['ssd_conv_b'], tm=tm)
    bias_pad = jnp.pad(ps['ssd_dt_bias'], ((0, 0), (0, 96)))
    dtg, dtg_t = _dt_fwd(proj, bias_pad, tm=tmw)
    alog = jnp.pad(ps['ssd_a_log'].reshape(SSD_N_GROUPS, 1, 8), ((0, 0), (0, 0), (0, 120)))
    alog_t = ps['ssd_a_log'].reshape(SSD_N_GROUPS, 8, 1)
    d_exp = jnp.repeat(ps['ssd_d'], SSD_HEAD_DIM, axis=1)
    y, hs = _ssd_fwd(xc, dtg, dtg_t, alog, alog_t, d_exp)
    gn = _gated_norm_fwd(y, proj, ps['ssd_norm_w'], tm=tmw)
    qr, kp, vp, kt, vt = _rope_qkv(proj, cos_t, sin_t, tm=tm)
    sinks = jnp.pad(ps['attn_sinks'], ((0, 0), (0, 112)))
    ao, lse = _attn_fwd(qr, kp, vt, sinks)
    y_attn = _mm_plain("attn_out", ao, wb['ao'], tm=tmw, tn=512, tk=1024)

    def merge_ep(acc, i, j, ins, outs):
        gs, ga, ya = ins
        outs[0][...] = (_sigmoid(gs[...]) * acc + _sigmoid(ga[...]) * ya[...]).astype(BF16)
        outs[1][...] = acc

    merged, y_ssd = _mm_call(
        "ssd_out_merge", gn, wb['so'], tm=tmw, tn=512, tk=2048, epilogue=merge_ep,
        extra_in=[(proj, (tmw, 512), lambda i, j, k: (i, OFF_GS // 512 + j)),
                  (proj, (tmw, 512), lambda i, j, k: (i, OFF_GA // 512 + j)), (y_attn, (tmw, 512), ij)],
        outs=[((t, D_MODEL), BF16, (tmw, 512), ij), ((t, D_MODEL), F32, (tmw, 512), ij)])

    def mix_ep(acc, i, j, ins, outs):
        xv, wn = ins
        r = lax.rsqrt(jnp.mean(acc * acc, axis=-1, keepdims=True) + NORM_EPS)
        outs[0][...] = xv[...] + acc * r * wn[...]
        outs[1][...] = acc

    x1, mmix = _mm_call(
        "mix_out", merged, wb['mix'], tm=tm, tn=D_MODEL, tk=1024, epilogue=mix_ep,
        extra_in=[(x, (tm, D_MODEL), i0), (ps['norm_mix_post_w'], (1, D_MODEL), c0)],
        outs=[((t, D_MODEL), F32, (tm, D_MODEL), i0), ((t, D_MODEL), F32, (tm, D_MODEL), i0)])

    up_raw, h = _norm_mm("ffn_up", x1, ps['norm_ffn_pre_w'], wb['up'], tm=tmw, tn=1408)
    act, ffn_gate, ffn_val = _ffn_act_fwd(up_raw, ps['ffn_conv_w'], ps['ffn_conv_b'], tm=tm)

    def loss_ep(acc, i, j, ins, outs):
        x1v, tg, wn = ins
        d_ff_ref, dout_ref, loss_ref, dw_ref = outs
        wv = wn[...]
        r = lax.rsqrt(jnp.mean(acc * acc, axis=-1, keepdims=True) + NORM_EPS)
        err = x1v[...] + acc * r * wv - tg[...]
        dout = err * (1.0 / D_MODEL)
        dout_ref[...] = dout
        d_ff, dw = _rms_bwd(acc, wv, dout)
        d_ff_ref[...] = d_ff.astype(BF16)
        _accumulate(dw_ref, i == 0, dw)
        _accumulate(loss_ref, i == 0, jnp.sum(err * err, keepdims=True) * (0.5 / D_MODEL))

    d_ff, dout, loss, g_norm_ffn_post = _mm_call(
        "ffn_down_loss", act, wb['dn'], tm=tm, tn=D_MODEL, tk=FFN_D_FF, epilogue=loss_ep,
        extra_in=[(x1, (tm, D_MODEL), i0), (tgt, (tm, D_MODEL), i0), (ps['norm_ffn_post_w'], (1, D_MODEL), c0)],
        outs=[((t, D_MODEL), BF16, (tm, D_MODEL), i0), ((t, D_MODEL), F32, (tm, D_MODEL), i0),
              ((1, 1), F32, (1, 1), c0), ((1, D_MODEL), F32, (1, D_MODEL), c0)])

    d_act = _mm_plain("d_act", d_ff, wb['dn_t'], tm=tmw, tn=1408, tk=1024, out_dtype=BF16)
    g_w_down = _mm_plain("g_w_down", act, d_ff, tm=1408, tn=1024, tk=tkt, trans_a=True, out_dtype=BF16)
    d_gate, d_val, db_g, db_v = _ffn_act_bwd(ffn_gate, ffn_val, d_act, tm=tm)
    d_up_raw, gcw_g = _conv_bwd2("ffn_conv_bwd2_gate", d_gate, up_raw, 0, ps['ffn_conv_w'][:, :FFN_D_FF], tm=tm,
                                 tc=1408, out_cols=2 * FFN_D_FF, out_col0=0)
    d_up_raw, gcw_v = _conv_bwd2("ffn_conv_bwd2_val", d_val, up_raw, FFN_D_FF, ps['ffn_conv_w'][:, FFN_D_FF:], tm=tm,
                                 tc=1408, out_cols=2 * FFN_D_FF, out_col0=FFN_D_FF, fill=d_up_raw)
    g_ffn_conv_w = jnp.concatenate([gcw_g, gcw_v], axis=1)

    def dx1_ep(acc, i, j, ins, outs):
        x1v, wpre, dout_v, mmv, wpost = ins
        d_x1_ref, d_mm_ref, dwpre_ref, dwpost_ref = outs
        d_n, dw_pre = _rms_bwd(x1v[...], wpre[...], acc)
        d_x1 = dout_v[...] + d_n
        d_x1_ref[...] = d_x1
        d_mm, dw_post = _rms_bwd(mmv[...], wpost[...], d_x1)
        d_mm_ref[...] = d_mm.astype(BF16)
        _accumulate(dwpre_ref, i == 0, dw_pre)
        _accumulate(dwpost_ref, i == 0, dw_post)

    d_x1, d_mm, g_norm_ffn_pre, g_norm_mix_post = _mm_call(
        "d_h", d_up_raw, wb['up_t'], tm=tm, tn=D_MODEL, tk=2 * FFN_D_FF, epilogue=dx1_ep, vmem_mb=VMEM_BIG_MB,
        extra_in=[(x1, (tm, D_MODEL), i0), (ps['norm_ffn_pre_w'], (1, D_MODEL), c0), (dout, (tm, D_MODEL), i0),
                  (mmix, (tm, D_MODEL), i0), (ps['norm_mix_post_w'], (1, D_MODEL), c0)],
        outs=[((t, D_MODEL), F32, (tm, D_MODEL), i0), ((t, D_MODEL), BF16, (tm, D_MODEL), i0),
              ((1, D_MODEL), F32, (1, D_MODEL), c0), ((1, D_MODEL), F32, (1, D_MODEL), c0)])
    g_w_up_t = _mm_plain("g_w_up", d_up_raw, h, tm=1408, tn=1024, tk=tkt, trans_a=True, out_dtype=BF16)
    ride_ffn = rides('ffn', {'ffn_w_up': g_w_up_t, 'ffn_w_down': g_w_down})

    def dmerge_ep(acc, i, j, ins, outs):
        gs, ga, ys, ya = ins
        sg_s, sg_a = _sigmoid(gs[...]), _sigmoid(ga[...])
        outs[0][...] = (acc * sg_s).astype(BF16)
        outs[1][...] = (acc * sg_a).astype(BF16)
        outs[2][:, 0:D_MODEL] = (acc * ys[...] * sg_s * (1.0 - sg_s)).astype(BF16)
        outs[2][:, D_MODEL:2 * D_MODEL] = (acc * ya[...] * sg_a * (1.0 - sg_a)).astype(BF16)

    d_yssd, d_yattn, d_proj = _mm_call(
        "d_merged", d_mm, wb['mix_t'], tm=tm, tn=D_MODEL, tk=1024, epilogue=dmerge_ep,
        extra_in=[(proj, (tm, D_MODEL), lambda i, j, k: (i, OFF_GS // D_MODEL)),
                  (proj, (tm, D_MODEL), lambda i, j, k: (i, OFF_GA // D_MODEL)), (y_ssd, (tm, D_MODEL), i0), (y_attn, (tm, D_MODEL), i0)],
        outs=[((t, D_MODEL), BF16, (tm, D_MODEL), i0), ((t, D_MODEL), BF16, (tm, D_MODEL), i0),
              ((t, PROJ_W), BF16, (tm, 2 * D_MODEL), lambda i, j, k: (i, OFF_GS // (2 * D_MODEL)))])
    g_w_mix = _mm_plain("g_w_mix", merged, d_mm, tm=1024, tn=1024, tk=tkt, trans_a=True, out_dtype=BF16)

    def dgn_ep(acc, i, j, ins, outs):
        yv, zv, wn = ins
        d_y_ref, d_z_ref, dw_ref = outs
        zz = zv[...]
        sz = _sigmoid(zz)
        silu = zz * sz
        gv = yv[...] * silu
        r = lax.rsqrt(jnp.mean(gv * gv, axis=-1, keepdims=True) + NORM_EPS)
        gh = gv * r
        dgh = acc * wn[...]
        dg = r * (dgh - gh * jnp.mean(dgh * gh, axis=-1, keepdims=True))
        d_y_ref[...] = dg * silu
        d_z_ref[...] = (dg * yv[...] * (sz * (1.0 + zz * (1.0 - sz)))).astype(BF16)
        dw = jnp.sum(acc * gh, axis=0, keepdims=True)

        @pl.when(i == 0)
        def _():
            dw_ref[j] = dw

        @pl.when(i > 0)
        def _():
            dw_ref[j] += dw

    d_y, d_proj, g_ssd_norm = _mm_call(
        "d_gn", d_yssd, wb['so_t'], tm=tmw, tn=512, tk=1024, epilogue=dgn_ep, fill=(d_proj, 1),
        extra_in=[(y, (tmw, 512), ij), (proj, (tmw, 512), lambda i, j, k: (i, OFF_Z // 512 + j)), (ps['ssd_norm_w'], (1, 512), cj)],
        outs=[((t, SSD_D_INNER), F32, (tmw, 512), ij), ((t, PROJ_W), BF16, (tmw, 512), lambda i, j, k: (i, OFF_Z // 512 + j)),
              ((SSD_N_GROUPS, 1, 512), F32, (SSD_N_GROUPS, 1, 512), lambda i, j, k: (0, 0, 0))])
    g_ssd_norm = g_ssd_norm.reshape(1, SSD_D_INNER)
    g_w_so = _mm_plain("g_w_so", gn, d_yssd, tm=1024, tn=1024, tk=tkt, trans_a=True, out_dtype=BF16)
    d_xc, d_dtg, d_alog, d_dd, *rode['ffn'] = _ssd_bwd(xc, dtg, dtg_t, alog, alog_t, d_exp, d_y, hs, ride=ride_ffn)
    d_pre, g_ssd_conv_b = _conv_silu_bwd1(d_xc, xc_pre, tm=tm)
    d_proj, g_ssd_conv_w = _conv_bwd2("ssd_conv_bwd2", d_pre, proj, OFF_XBC, ps['ssd_conv_w'], tm=tm, tc=1536,
                                      out_cols=PROJ_W, out_col0=OFF_XBC, fill=d_proj)
    d_proj, g_dt_bias = _dt_bwd(d_dtg, proj, bias_pad, d_proj, tm=tmw)

    d_ao = _mm_plain("d_ao", d_yattn, wb['ao_t'], tm=tmw, tn=512, tk=1024, out_dtype=BF16)
    g_w_ao = _mm_plain("g_w_ao", ao, d_yattn, tm=1024, tn=1024, tk=tkt, trans_a=True, out_dtype=BF16)
    ride_mix = rides('mix', {'ssd_w_out': g_w_so, 'attn_w_out': g_w_ao, 'w_mix_out': g_w_mix})
    d_proj, g_sinks, *rode['mix'] = _attn_bwd(qr, kp, vp, kt, d_ao, ao, lse, sinks, cos_t, sin_t, d_proj, ride=ride_mix)

    def dx_ep(acc, i, j, ins, outs):
        xv, wn, dx1v = ins
        d_n, dw = _rms_bwd(xv[...], wn[...], acc)
        outs[0][...] = dx1v[...] + d_n
        _accumulate(outs[1], i == 0, dw)

    g_cat_t = _mm_plain("g_w_in", d_proj, u, tm=1024, tn=1024, tk=tkt, trans_a=True, out_dtype=BF16)
    grad_x, g_norm_mix_pre, *rode['w_in'] = _mm_call(
        "d_u", d_proj, wb['cat_t'], tm=tm, tn=D_MODEL, tk=PROJ_W, epilogue=dx_ep, ride=rides('w_in', {'w_in': g_cat_t}),
        vmem_mb=VMEM_BIG_MB,
        extra_in=[(x, (tm, D_MODEL), i0), (ps['norm_mix_pre_w'], (1, D_MODEL), c0), (d_x1, (tm, D_MODEL), i0)],
        outs=[((t, D_MODEL), F32, (tm, D_MODEL), i0), ((1, D_MODEL), F32, (1, D_MODEL), c0)])

    grads = {
        'norm_mix_pre_w': g_norm_mix_pre, 'w_in': g_cat_t, 'ssd_conv_w': g_ssd_conv_w, 'ssd_conv_b': g_ssd_conv_b,
        'ssd_dt_bias': g_dt_bias[:, :SSD_N_HEADS], 'ssd_a_log': d_alog[:, 0, :8].reshape(1, SSD_N_HEADS),
        'ssd_d': d_dd[:, 0, :8].reshape(1, SSD_N_HEADS), 'ssd_norm_w': g_ssd_norm, 'ssd_w_out': g_w_so,
        'attn_sinks': g_sinks[:, :ATTN_N_HEADS], 'attn_w_out': g_w_ao, 'w_mix_out': g_w_mix,
        'norm_mix_post_w': g_norm_mix_post, 'norm_ffn_pre_w': g_norm_ffn_pre, 'ffn_w_up': g_w_up_t,
        'ffn_conv_w': g_ffn_conv_w, 'ffn_conv_b': jnp.concatenate([db_g, db_v], axis=1), 'ffn_w_down': g_w_down,
        'norm_ffn_post_w': g_norm_ffn_post,
    }
    return loss, grad_x, grads, rode


def _group_channels(a):
    parts = []
    for g in range(SSD_N_GROUPS):
        parts += [a[..., 512 * g:512 * (g + 1)], a[..., 2048 + 128 * g:2048 + 128 * (g + 1)],
                  a[..., 2560 + 128 * g:2560 + 128 * (g + 1)]]
    return jnp.concatenate(parts, axis=-1)


def _ungroup_channels(a):
    xs = [a[..., GROUP_W * g:GROUP_W * g + 512] for g in range(SSD_N_GROUPS)]
    bs = [a[..., GROUP_W * g + 512:GROUP_W * g + 640] for g in range(SSD_N_GROUPS)]
    cs = [a[..., GROUP_W * g + 640:GROUP_W * (g + 1)] for g in range(SSD_N_GROUPS)]
    return jnp.concatenate(xs + bs + cs, axis=-1)


def _proj_rows(a_t, lo, hi):
    out = []
    for start, length, dst in sorted(PROJ_SEGS):
        s, e = max(lo, start), min(hi, start + length)
        if s < e:
            out.append(a_t[dst + s - start:dst + e - start])
    return out


def _to_proj_layout(w_in_t):
    pieces, pos = [], 0
    for start, length, dst in sorted(PROJ_SEGS, key=lambda s: s[2]):
        if dst > pos:
            pieces.append(jnp.zeros((dst - pos, w_in_t.shape[1]), w_in_t.dtype))
        pieces.append(w_in_t[start:start + length])
        pos = dst + length
    if pos < PROJ_W:
        pieces.append(jnp.zeros((PROJ_W - pos, w_in_t.shape[1]), w_in_t.dtype))
    return jnp.concatenate(pieces, axis=0)


def _rope_tables(positions):
    half = 32
    inv_freq = ROPE_THETA ** (-jnp.arange(half, dtype=F32) * 2.0 / 64)
    ang = positions.astype(F32)[:, None] * inv_freq
    cos, sin = jnp.cos(ang), jnp.sin(ang)
    return jnp.concatenate([cos, cos, cos, cos], axis=1), jnp.concatenate([-sin, sin, -sin, sin], axis=1)


def _matmul_weights(w_in_t):
    cat_t = _to_proj_layout(w_in_t)
    return {'cat': cat_t.T, 'cat_t': cat_t}


def _late_weights(so, ao, mix, up_t, dn):
    return {'so': so, 'so_t': so.T, 'ao': ao, 'ao_t': ao.T, 'mix': mix, 'mix_t': mix.T,
            'up': up_t.T, 'up_t': up_t, 'dn': dn, 'dn_t': dn.T}


ANY = pl.BlockSpec(memory_space=pl.ANY)
MESH = pl.DeviceIdType.MESH
ROW_ALIGN = 32


def _mesh_pos():
    return lax.axis_index("x"), lax.axis_index("y"), lax.axis_index("c")


def _other_chips(x, y):
    return [(1 - x, y), (x, 1 - y), (1 - x, 1 - y)]


def _remote(src, dst, send_sems, recv_sems, k, to):
    return pltpu.make_async_remote_copy(src_ref=src, dst_ref=dst, send_sem=send_sems.at[k], recv_sem=recv_sems.at[k],
                                        device_id=to, device_id_type=MESH)


def _half(c, rh):
    return pl.ds(pl.multiple_of(c * rh, 16), rh)


def _ag_ride(shard):
    r = shard.shape[0]
    rh = r // 2

    def first_copies(w_ref, out_ref, send_sems, recv_sems):
        x, y, c = _mesh_pos()
        p = 2 * x + y
        mine = _half(c, rh)
        cps = [_remote(w_ref, out_ref.at[p], send_sems, recv_sems, 6, (x, y, 1 - c))]
        return cps + [_remote(w_ref.at[mine], out_ref.at[p, mine], send_sems, recv_sems, j, (cx, cy, c))
                      for j, (cx, cy) in enumerate(_other_chips(x, y))]

    def start(ins, outs, send_sems, recv_sems):
        for cp in first_copies(ins[0], outs[0], send_sems, recv_sems):
            cp.start()

    def forwards(out_ref, send_sems, recv_sems, half):
        x, y, c = _mesh_pos()
        return [_remote(out_ref.at[2 * cx + cy, half], out_ref.at[2 * cx + cy, half], send_sems, recv_sems, 3 + j, (x, y, 1 - c))
                for j, (cx, cy) in enumerate(_other_chips(x, y))]

    def middle(ins, outs, send_sems, recv_sems):
        x, y, c = _mesh_pos()
        mine = _half(c, rh)
        for j, (fwd, (cx, cy)) in enumerate(zip(forwards(outs[0], send_sems, recv_sems, mine), _other_chips(x, y))):
            slab = outs[0].at[2 * cx + cy, mine]
            _remote(slab, slab, send_sems, recv_sems, j, (x, y, 1 - c)).wait_recv()
            fwd.start()

    def finish(ins, outs, send_sems, recv_sems):
        w_ref, out_ref = ins[0], outs[0]
        x, y, c = _mesh_pos()
        for cp in forwards(out_ref, send_sems, recv_sems, _half(1 - c, rh)):
            cp.wait_recv()
        _remote(w_ref, out_ref.at[2 * x + y], send_sems, recv_sems, 6, (x, y, 1 - c)).wait_recv()
        for cp in first_copies(w_ref, out_ref, send_sems, recv_sems) + forwards(out_ref, send_sems, recv_sems, _half(c, rh)):
            cp.wait_send()

    return _Ride((shard,), (SDS((N_CHIPS, r, COMM_LANES), shard.dtype),), 7, start, finish, middle)


def _rs_ride(gbuf):
    rh = gbuf.shape[1] // 2

    def copies(g_ref, r_ref, send_sems, recv_sems, landing):
        x, y, c = _mesh_pos()
        cps = []
        for k, (cx, cy) in enumerate(_other_chips(x, y)):
            for h in range(2):
                slot = 2 * k + c if landing else 2 * k + h
                cps.append(pltpu.make_async_remote_copy(
                    src_ref=g_ref.at[2 * cx + cy, pl.ds(h * rh, rh)], dst_ref=r_ref.at[slot],
                    send_sem=send_sems.at[2 * k + h], recv_sem=recv_sems.at[slot],
                    device_id=(cx, cy, h), device_id_type=MESH))
        cps.append(_remote(g_ref.at[2 * x + y, _half(1 - c, rh)], r_ref.at[6], send_sems, recv_sems, 6, (x, y, 1 - c)))
        return cps

    def start(ins, outs, send_sems, recv_sems):
        for cp in copies(ins[0], outs[0], send_sems, recv_sems, True):
            cp.start()

    def finish(ins, outs, send_sems, recv_sems):
        for cp in copies(ins[0], outs[0], send_sems, recv_sems, False):
            cp.wait()

    return _Ride((gbuf,), (SDS((7, rh, COMM_LANES), gbuf.dtype),), 7, start, finish)


def _rs_sum(name, gbuf, got, pc_idx):
    rh = got.shape[1]
    tr = max(d for d in range(16, 513, 16) if rh % d == 0)
    nb = rh // tr

    def body(pc_ref, own_ref, *refs):
        o_ref = refs[7]
        p, c = pc_ref[0], pc_ref[1]
        own = own_ref[0].astype(F32)
        slots = [r[0].astype(F32) for r in refs[:7]]

        def term(q, h):
            code = p ^ q
            far = jnp.where(code == 2, slots[h], jnp.where(code == 1, slots[2 + h], slots[4 + h]))
            return jnp.where(code == 0, jnp.where(c == h, own, slots[6]), far)

        acc = term(0, 0)
        for q, h in [(0, 1), (1, 0), (1, 1), (2, 0), (2, 1), (3, 0), (3, 1)]:
            acc = acc + term(q, h)
        o_ref[0] = acc

    slot = lambda s: pl.BlockSpec((1, tr, COMM_LANES), lambda i, pc: (s, i, 0))
    return pl.pallas_call(
        body, name=name,
        grid_spec=pltpu.PrefetchScalarGridSpec(
            num_scalar_prefetch=1, grid=(nb,),
            in_specs=[pl.BlockSpec((1, tr, COMM_LANES), lambda i, pc: (pc[0], pc[1] * nb + i, 0))] + [slot(s) for s in range(7)],
            out_specs=pl.BlockSpec((1, tr, COMM_LANES), lambda i, pc: (pc[1], i, 0))),
        out_shape=SDS((2, rh, COMM_LANES), F32), compiler_params=_cp(dimension_semantics=("arbitrary",)),
    )(pc_idx, gbuf, *([got] * 7))


def _pair_gather_all(bufs):
    n = len(bufs)

    def body(*refs):
        outs, send_sems, recv_sems = refs[n:2 * n], refs[2 * n], refs[2 * n + 1]
        x, y, c = _mesh_pos()
        cps = [_remote(o.at[c], o.at[c], send_sems, recv_sems, k, (x, y, 1 - c)) for k, o in enumerate(outs)]
        for cp in cps:
            cp.start()
        for k, o in enumerate(outs):
            _remote(o.at[1 - c], o.at[1 - c], send_sems, recv_sems, k, (x, y, 1 - c)).wait_recv()
        for cp in cps:
            cp.wait_send()

    return pl.pallas_call(
        body, name="grad_pair_gather", in_specs=[ANY] * n, out_specs=[ANY] * n,
        out_shape=[SDS(b.shape, b.dtype) for b in bufs],
        scratch_shapes=[pltpu.SemaphoreType.DMA((n,)), pltpu.SemaphoreType.DMA((n,))],
        input_output_aliases={k: k for k in range(n)},
    )(*bufs)


def _pack_rows(big, small=()):
    parts = list(big)
    if small:
        flat = jnp.concatenate([p.reshape(-1) for p in small])
        k = -(-flat.shape[0] // (16 * COMM_LANES)) * 16
        parts.append(jnp.pad(flat, (0, k * COMM_LANES - flat.shape[0])).reshape(k, COMM_LANES))
    pad = -sum(p.shape[0] for p in parts) % ROW_ALIGN
    if pad:
        parts.append(jnp.zeros((pad, COMM_LANES), parts[0].dtype))
    return jnp.concatenate(parts, axis=0) if len(parts) > 1 else parts[0]


def _take(flat, off, shape):
    n = 1
    for d in shape:
        n *= d
    return flat[off:off + n].reshape(shape), off + n


BIG_ROWS = {'w_in': 2184, 'ssd_w_out': 512, 'attn_w_out': 256, 'w_mix_out': 256, 'ffn_w_up': 1408, 'ffn_w_down': 704}
TRANSPOSED = ('w_in', 'ffn_w_up')
LATE = ('ssd_w_out', 'attn_w_out', 'w_mix_out', 'ffn_w_up', 'ffn_w_down')
CONV_TAPS = ('ssd_conv_w', 'ffn_conv_w')
RS_GROUPS = {'ffn': ('ffn_w_up', 'ffn_w_down'), 'mix': ('ssd_w_out', 'attn_w_out', 'w_mix_out'), 'w_in': ('w_in',)}


def _exchange(name, ride):
    n_in, n_out = len(ride.ins), len(ride.outs)

    def body(*refs):
        ins, outs, sems = refs[:n_in], refs[n_in:n_in + n_out], refs[n_in + n_out:]
        ride.start(ins, outs, *sems)
        if ride.middle is not None:
            ride.middle(ins, outs, *sems)
        ride.finish(ins, outs, *sems)

    return pl.pallas_call(
        body, name=name, in_specs=[ANY] * n_in, out_specs=[ANY] * n_out, out_shape=list(ride.outs),
        scratch_shapes=[pltpu.SemaphoreType.DMA((ride.n_sems,)), pltpu.SemaphoreType.DMA((ride.n_sems,))],
    )(*ride.ins)


def kernel(x, positions, norm_mix_pre_w, w_in, ssd_conv_w, ssd_conv_b, ssd_dt_bias, ssd_a_log, ssd_d, ssd_norm_w, ssd_w_out, attn_sinks, attn_w_out, w_mix_out, norm_mix_post_w, norm_ffn_pre_w, ffn_w_up, ffn_conv_w, ffn_conv_b, ffn_w_down, norm_ffn_post_w, loss_target, m_norm_mix_pre_w, m_w_in, m_ssd_conv_w, m_ssd_conv_b, m_ssd_dt_bias, m_ssd_a_log, m_ssd_d, m_ssd_norm_w, m_ssd_w_out, m_attn_sinks, m_attn_w_out, m_w_mix_out, m_norm_mix_post_w, m_norm_ffn_pre_w, m_ffn_w_up, m_ffn_conv_w, m_ffn_conv_b, m_ffn_w_down, m_norm_ffn_post_w, v_norm_mix_pre_w, v_w_in, v_ssd_conv_w, v_ssd_conv_b, v_ssd_dt_bias, v_ssd_a_log, v_ssd_d, v_ssd_norm_w, v_ssd_w_out, v_attn_sinks, v_attn_w_out, v_w_mix_out, v_norm_mix_post_w, v_norm_ffn_pre_w, v_ffn_w_up, v_ffn_conv_w, v_ffn_conv_b, v_ffn_w_down, v_norm_ffn_post_w):
    given = dict(locals())
    w = {n: given[n][0] for n in WEIGHTS}
    w = {n: (a if a.ndim == 2 else a[None]) for n, a in w.items()}
    mom_m = {n: given['m_' + n].reshape(w[n].shape) for n in WEIGHTS}
    mom_v = {n: given['v_' + n].reshape(w[n].shape) for n in WEIGHTS}
    cx, cy, cc = _mesh_pos()
    pc_idx = jnp.stack([2 * cx + cy, cc]).astype(jnp.int32)

    rows_of = lambda n: (w[n].T if n in TRANSPOSED else w[n]).astype(BF16)
    gathered = _exchange("w_in_all_gather", _ag_ride(_pack_rows([rows_of('w_in')])))[0]
    wb = _matmul_weights(jnp.concatenate([gathered[s, :BIG_ROWS['w_in']] for s in range(N_CHIPS)], axis=0))
    taps = [lax.bitcast_convert_type(w[n], BF16) for n in CONV_TAPS]

    def unpack_late(arrived):
        rows, conv = {n: [] for n in LATE}, {n: [] for n in CONV_TAPS}
        for s in range(N_CHIPS):
            r0 = 0
            for n in LATE:
                rows[n].append(arrived[0][s, r0:r0 + BIG_ROWS[n]])
                r0 += BIG_ROWS[n]
            flat, off = arrived[0][s, r0:r0 + 16].reshape(-1), 0
            for n in CONV_TAPS:
                a, off = _take(flat, off, w[n].shape + (2,))
                conv[n].append(lax.bitcast_convert_type(a, F32))
        full = {n: jnp.concatenate(rows[n], axis=0) for n in LATE}
        return (_late_weights(*[full[n] for n in LATE]),
                {'ssd_conv_w': _group_channels(jnp.concatenate(conv['ssd_conv_w'], axis=1)),
                 'ffn_conv_w': jnp.concatenate(conv['ffn_conv_w'], axis=1)})

    late = (_ag_ride(_pack_rows([rows_of(n) for n in LATE], taps)), unpack_late)

    sent = {}

    def rides(group, g):
        parts = []
        for s in range(N_CHIPS):
            slab = []
            for n in RS_GROUPS[group]:
                lo, hi = BIG_ROWS[n] * s, BIG_ROWS[n] * (s + 1)
                slab += _proj_rows(g[n], lo, hi) if n == 'w_in' else [g[n][lo:hi]]
            slab = [a.astype(BF16) for a in slab]
            pad = -sum(a.shape[0] for a in slab) % ROW_ALIGN
            parts += slab + ([jnp.zeros((pad, COMM_LANES), BF16)] if pad else [])
        sent[group] = jnp.concatenate(parts, axis=0).reshape(N_CHIPS, -1, COMM_LANES)
        return _rs_ride(sent[group])

    ps = {n: w[n] for n in REPLICATED}
    ps['ssd_conv_b'] = _group_channels(w['ssd_conv_b'])
    cos_t, sin_t = _rope_tables(positions[0])
    loss, grad_x, grads, rode = _local_step(x[0], cos_t, sin_t, loss_target[0], wb, ps, late, rides)
    grads['ssd_conv_w'] = _ungroup_channels(grads['ssd_conv_w'])
    grads['ssd_conv_b'] = _ungroup_channels(grads['ssd_conv_b'])

    shard_cols = {n: sh[1] for n, _, sh in SHARDED}
    parts = []
    for s in range(N_CHIPS):
        small = [grads[n][:, shard_cols[n] * s:shard_cols[n] * (s + 1)] for n in CONV_TAPS] + [grads[n] for n in REPLICATED]
        flat = _pack_rows([], small)
        high = flat.astype(BF16)
        parts += [high, (flat - high.astype(F32)).astype(BF16)]
    sent['small'] = jnp.concatenate(parts, axis=0).reshape(N_CHIPS, -1, COMM_LANES)
    rode['small'] = _exchange("grad_small_exchange", _rs_ride(sent['small']))

    groups = ('ffn', 'mix', 'w_in', 'small')
    red = _pair_gather_all([_rs_sum("grad_sum_" + g, sent[g], rode[g][0], pc_idx) for g in groups])
    red = {g: r.reshape(-1, COMM_LANES) for g, r in zip(groups, red)}
    g_red = {}
    for g in groups[:3]:
        r0 = 0
        for n in RS_GROUPS[g]:
            g_red[n] = red[g][r0:r0 + BIG_ROWS[n]].T if n in TRANSPOSED else red[g][r0:r0 + BIG_ROWS[n]]
            r0 += BIG_ROWS[n]
    half = red['small'].shape[0] // 2
    flat, off = (red['small'][:half] + red['small'][half:]).reshape(-1), 0
    for n in CONV_TAPS + REPLICATED:
        g_red[n], off = _take(flat, off, w[n].shape)

    small_names = [n for n in WEIGHTS if n not in MATMUL_WEIGHTS]
    delta, new_m, new_v = {}, {}, {}
    for n in MATMUL_WEIGHTS:
        delta[n], new_m[n], new_v[n] = _adamw("adamw_" + n, w[n], g_red[n], mom_m[n], mom_v[n],
                                                  tr=max(d for d in range(8, 353, 8) if w[n].shape[0] % d == 0))
    packed = [_pack_small([d[n] for n in small_names]) for d in (w, g_red, mom_m, mom_v)]
    outs = _adamw("adamw_small", *packed, tr=packed[0].shape[0])
    for res, o in zip((delta, new_m, new_v), outs):
        fl, off = o.reshape(-1), 0
        for n in small_names:
            res[n], off = _take(fl, off, w[n].shape)

    loss_all = lax.psum(loss[0, 0], ("x", "y", "c"))
    shaped = lambda d: [d[n].reshape(given[n].shape) for n in WEIGHTS]
    return (loss_all, grad_x[None], *shaped(g_red), *shaped(delta), *shaped(new_m), *shaped(new_v))


def _pack_small(pieces):
    flat = jnp.concatenate([p.reshape(-1) for p in pieces])
    rows = -(-flat.shape[0] // (128 * 8)) * 8
    return jnp.pad(flat, (0, rows * 128 - flat.shape[0])).reshape(rows, 128)
```

```python
from typing import Callable, NamedTuple

import jax
import jax.numpy as jnp
from jax import lax
from jax.experimental import pallas as pl
from jax.experimental.pallas import tpu as pltpu

F32 = jnp.float32
BF16 = jnp.bfloat16
SDS = jax.ShapeDtypeStruct

D_MODEL = 1024
SSD_D_INNER = 2048
SSD_N_HEADS = 32
SSD_HEAD_DIM = 64
SSD_N_GROUPS = 4
SSD_D_STATE = 128
SSD_CONV_DIM = 3072
CHUNK = 128
ATTN_N_HEADS = 16
FFN_D_FF = 2816
ROPE_THETA = 10000.0
NORM_EPS = 1e-6
ADAM_LR, ADAM_B1, ADAM_B2, ADAM_EPS, ADAM_WD, ADAM_STEP = 0.001, 0.9, 0.999, 1e-08, 0.01, 10

PROJ_W = 9216
OFF_Q, OFF_K, OFF_V, OFF_Z, OFF_DT, OFF_GS, OFF_GA, OFF_XBC = 0, 1024, 1280, 1536, 3584, 4096, 5120, 6144
GROUP_W = 768
PROJ_SEGS = ([(0, 2048, OFF_Z)]
             + [(2048 + 512 * g, 512, OFF_XBC + GROUP_W * g) for g in range(4)]
             + [(4096 + 128 * g, 128, OFF_XBC + GROUP_W * g + 512) for g in range(4)]
             + [(4608 + 128 * g, 128, OFF_XBC + GROUP_W * g + 640) for g in range(4)]
             + [(5120, 32, OFF_DT), (5152, 1024, OFF_Q), (6176, 256, OFF_K), (6432, 256, OFF_V),
                (6688, 1024, OFF_GS), (7712, 1024, OFF_GA)])
VMEM_LIMIT_MB = 48
VMEM_BIG_MB = 60
NEG = -1e30

WEIGHTS = ('norm_mix_pre_w', 'w_in', 'ssd_conv_w', 'ssd_conv_b', 'ssd_dt_bias', 'ssd_a_log', 'ssd_d', 'ssd_norm_w',
           'ssd_w_out', 'attn_sinks', 'attn_w_out', 'w_mix_out', 'norm_mix_post_w', 'norm_ffn_pre_w', 'ffn_w_up',
           'ffn_conv_w', 'ffn_conv_b', 'ffn_w_down', 'norm_ffn_post_w')
SHARDED = (('w_in', 1, (1024, 2184)), ('ssd_conv_w', 1, (4, 768)), ('ssd_w_out', 0, (512, 1024)),
           ('attn_w_out', 0, (256, 1024)), ('w_mix_out', 0, (256, 1024)), ('ffn_w_up', 1, (1024, 1408)),
           ('ffn_conv_w', 1, (3, 1408)), ('ffn_w_down', 0, (704, 1024)))
MATMUL_WEIGHTS = ('w_in', 'ssd_w_out', 'attn_w_out', 'w_mix_out', 'ffn_w_up', 'ffn_w_down')
REPLICATED = tuple(n for n in WEIGHTS if n not in {s[0] for s in SHARDED})
N_CHIPS = 4
COMM_LANES = 1024


def _cp(vmem_mb=VMEM_LIMIT_MB, **kw):
    return pltpu.CompilerParams(vmem_limit_bytes=vmem_mb << 20, **kw)


class _Ride(NamedTuple):
    ins: tuple
    outs: tuple
    n_sems: int
    start: Callable
    finish: Callable
    middle: Callable = None


def _ride_parts(ride):
    if ride is None:
        return [], [], [], [], []
    hbm = pl.BlockSpec(memory_space=pl.ANY)
    return (list(ride.ins), [hbm] * len(ride.ins), list(ride.outs), [hbm] * len(ride.outs),
            [pltpu.SemaphoreType.DMA((ride.n_sems,)), pltpu.SemaphoreType.DMA((ride.n_sems,))])


def _ride_run(ride, first, last, in_refs, out_refs, sems, middle=None):
    if ride is None:
        return

    @pl.when(first)
    def _():
        ride.start(in_refs, out_refs, *sems)

    if ride.middle is not None:
        @pl.when(last if middle is None else middle)
        def _():
            ride.middle(in_refs, out_refs, *sems)

    @pl.when(last)
    def _():
        ride.finish(in_refs, out_refs, *sems)


def _iota(shape, axis):
    return lax.broadcasted_iota(jnp.int32, shape, axis)


def _sigmoid(v):
    return 1.0 / (1.0 + jnp.exp(-v))


def _mm_call(name, a, b, *, tm, tn, tk, epilogue, outs, extra_in=(), trans_a=False, fill=None, ride=None,
             vmem_mb=VMEM_LIMIT_MB):
    if trans_a:
        kdim, m = a.shape
    else:
        m, kdim = a.shape
    n = b.shape[1]
    assert b.shape[0] == kdim and m % tm == 0 and n % tn == 0 and kdim % tk == 0, (name, a.shape, b.shape, tm, tn, tk)
    gi, gj, gk = m // tm, n // tn, kdim // tk
    n_in, n_out = len(extra_in), len(outs)

    n_fill = 0 if fill is None else 1
    r_ops, r_in_specs, r_outs, r_out_specs, r_scratch = _ride_parts(ride)

    def body(a_ref, b_ref, *rest):
        ins = rest[:n_in]
        rest = rest[n_in + n_fill:]
        r_in, rest = rest[:len(r_ops)], rest[len(r_ops):]
        out_refs, rest = rest[:n_out], rest[n_out:]
        r_out, scratch = rest[:len(r_outs)], rest[len(r_outs):]
        i, j, k = pl.program_id(0), pl.program_id(1), pl.program_id(2)
        _ride_run(ride, (i == 0) & (j == 0) & (k == 0), (i == gi - 1) & (j == gj - 1) & (k == gk - 1),
                  r_in, r_out, scratch[-2:])
        av = a_ref[...].astype(BF16)
        bv = b_ref[...].astype(BF16)
        if trans_a:
            part = lax.dot_general(av, bv, (((0,), (0,)), ((), ())), preferred_element_type=F32)
        else:
            part = jnp.dot(av, bv, preferred_element_type=F32)
        if gk == 1:
            epilogue(part, i, j, ins, out_refs)
        else:
            acc = scratch[0]

            @pl.when(k == 0)
            def _():
                acc[...] = part

            @pl.when(k > 0)
            def _():
                acc[...] += part

            @pl.when(k == gk - 1)
            def _():
                epilogue(acc[...], i, j, ins, out_refs)

    a_spec = pl.BlockSpec((tk, tm), lambda i, j, k: (k, i)) if trans_a else pl.BlockSpec((tm, tk), lambda i, j, k: (i, k))
    if gj == 1 and gk == 1:
        b_spec = pl.BlockSpec((tk, tn), lambda i, j, k: (0, 0), pipeline_mode=pl.Buffered(1))
    else:
        b_spec = pl.BlockSpec((tk, tn), lambda i, j, k: (k, j))
    in_specs = [a_spec, b_spec]
    in_specs += [pl.BlockSpec(bs, im) for _, bs, im in extra_in]
    operands = [a, b] + [e[0] for e in extra_in]
    aliases = {}
    if fill is not None:
        in_specs.append(pl.BlockSpec(memory_space=pl.ANY))
        aliases = {len(operands): fill[1]}
        operands.append(fill[0])
    return pl.pallas_call(
        body, name=name, grid=(gi, gj, gk), in_specs=in_specs + r_in_specs,
        out_specs=[pl.BlockSpec(bs, im) for _, _, bs, im in outs] + r_out_specs,
        out_shape=[SDS(s, d) for s, d, _, _ in outs] + r_outs,
        scratch_shapes=([pltpu.VMEM((tm, tn), F32)] if gk > 1 else []) + r_scratch,
        input_output_aliases=aliases,
        compiler_params=_cp(vmem_mb, dimension_semantics=("arbitrary", "arbitrary", "arbitrary")),
    )(*operands, *r_ops)


def _mm_plain(name, a, b, *, tm, tn, tk, out_dtype=F32, trans_a=False):
    m = a.shape[1] if trans_a else a.shape[0]

    def epilogue(acc, i, j, ins, outs):
        outs[0][...] = acc.astype(out_dtype)

    return _mm_call(name, a, b, tm=tm, tn=tn, tk=tk, epilogue=epilogue, trans_a=trans_a,
                    outs=[((m, b.shape[1]), out_dtype, (tm, tn), lambda i, j, k: (i, j))])[0]


def _accumulate(ref, first, value):
    @pl.when(first)
    def _():
        ref[...] = value

    @pl.when(jnp.logical_not(first))
    def _():
        ref[...] += value


def _rms_bwd(xv, w, dy):
    r = lax.rsqrt(jnp.mean(xv * xv, axis=-1, keepdims=True) + NORM_EPS)
    xn = xv * r
    dxh = dy * w
    dx = r * (dxh - xn * jnp.mean(dxh * xn, axis=-1, keepdims=True))
    return dx, jnp.sum(dy * xn, axis=0, keepdims=True)


def _norm_mm(name, x, wn, w, *, tm, tn, ride=None):
    t, dm = x.shape
    n = w.shape[1]
    tm = min(tm, t)
    gi, gj = t // tm, n // tn
    r_ops, r_in_specs, r_outs, r_out_specs, r_scratch = _ride_parts(ride)

    def body(x_ref, wn_ref, w_ref, *rest):
        r_in, rest = rest[:len(r_ops)], rest[len(r_ops):]
        o_ref, u_ref = rest[:2]
        r_out, sems = rest[2:2 + len(r_outs)], rest[2 + len(r_outs):]
        i, j = pl.program_id(0), pl.program_id(1)
        _ride_run(ride, (i == 0) & (j == 0), (i == gi - 1) & (j == gj - 1), r_in, r_out, sems,
                  middle=(i == (3 * gi) // 4) & (j == 0) if gi > 1 else None)

        @pl.when(j == 0)
        def _():
            xv = x_ref[...]
            r = lax.rsqrt(jnp.mean(xv * xv, axis=-1, keepdims=True) + NORM_EPS)
            u_ref[...] = (xv * r * wn_ref[...]).astype(BF16)

        o_ref[...] = jnp.dot(u_ref[...], w_ref[...], preferred_element_type=F32)

    return pl.pallas_call(
        body, name=name, grid=(gi, gj),
        in_specs=[pl.BlockSpec((tm, dm), lambda i, j: (i, 0)), pl.BlockSpec((1, dm), lambda i, j: (0, 0)),
                  pl.BlockSpec((dm, tn), lambda i, j: (0, j))] + r_in_specs,
        out_specs=[pl.BlockSpec((tm, tn), lambda i, j: (i, j)), pl.BlockSpec((tm, dm), lambda i, j: (i, 0))] + r_out_specs,
        out_shape=[SDS((t, n), F32), SDS((t, dm), BF16)] + r_outs, scratch_shapes=r_scratch,
        compiler_params=_cp(dimension_semantics=("arbitrary", "arbitrary")),
    )(x, wn, w, *r_ops)


def _shift_down(tile, halo, s):
    if s == 0:
        return tile
    r = pltpu.roll(tile, s, axis=0)
    h = pltpu.roll(halo, s, axis=0)
    head = jnp.where(_iota(h.shape, 0) < s, h, r[0:8])
    return jnp.concatenate([head, r[8:]], axis=0)


def _shift_up(tile, halo, s):
    if s == 0:
        return tile
    n = tile.shape[0]
    r = pltpu.roll(tile, n - s, axis=0)
    h = pltpu.roll(halo, 8 - s, axis=0)
    tail = jnp.where(_iota(h.shape, 0) >= 8 - s, h, r[n - 8:])
    return jnp.concatenate([r[:n - 8], tail], axis=0)


def _conv_apply(tile, halo, wv, bv, kw):
    acc = bv + wv[kw - 1:kw, :] * tile
    for k in range(kw - 1):
        acc = acc + wv[k:k + 1, :] * _shift_down(tile, halo, kw - 1 - k)
    return acc


def _prev_halo_spec(tm, tc, col0):
    return pl.BlockSpec((8, tc), lambda i, j: (jnp.maximum(i * (tm // 8) - 1, 0), col0 + j))


def _silu_parts(pre):
    sg = _sigmoid(pre)
    return pre * sg, sg * (1.0 + pre * (1.0 - sg))


def _conv_silu_fwd(proj, w, b, *, tm, tc=1536):
    t = proj.shape[0]
    c = w.shape[1]
    tm = min(tm, t)
    col0 = OFF_XBC // tc

    def body(x_ref, h_ref, w_ref, b_ref, o_ref, pre_ref):
        halo = jnp.where(pl.program_id(0) > 0, h_ref[...], 0.0)
        pre = _conv_apply(x_ref[...], halo, w_ref[...], b_ref[...], 4)
        o_ref[...] = _silu_parts(pre)[0]
        pre_ref[...] = pre.astype(BF16)

    tile = pl.BlockSpec((tm, tc), lambda i, j: (i, j))
    return pl.pallas_call(
        body, name="ssd_conv_fwd", grid=(t // tm, c // tc),
        in_specs=[pl.BlockSpec((tm, tc), lambda i, j: (i, col0 + j)), _prev_halo_spec(tm, tc, col0),
                  pl.BlockSpec((4, tc), lambda i, j: (0, j)), pl.BlockSpec((1, tc), lambda i, j: (0, j))],
        out_specs=[tile, tile], out_shape=[SDS((t, c), F32), SDS((t, c), BF16)],
        compiler_params=_cp(dimension_semantics=("arbitrary", "arbitrary")),
    )(proj, proj, w, b)


def _conv_silu_bwd1(d_out, pre, *, tm, tc=1536):
    t, c = pre.shape
    tm = min(tm, t)

    def body(g_ref, p_ref, o_ref, db_ref):
        i = pl.program_id(1)
        d_pre = g_ref[...] * _silu_parts(p_ref[...].astype(F32))[1]
        o_ref[...] = d_pre.astype(BF16)
        _accumulate(db_ref, i == 0, jnp.sum(d_pre, axis=0, keepdims=True))

    tile = pl.BlockSpec((tm, tc), lambda j, i: (i, j))
    return pl.pallas_call(
        body, name="ssd_conv_bwd1", grid=(c // tc, t // tm), in_specs=[tile, tile],
        out_specs=[tile, pl.BlockSpec((1, tc), lambda j, i: (0, j))],
        out_shape=[SDS((t, c), BF16), SDS((1, c), F32)],
        compiler_params=_cp(dimension_semantics=("arbitrary", "arbitrary")),
    )(d_out, pre)


def _conv_bwd2(name, d_pre, src, src_col0, w, *, tm, tc, out_cols, out_col0, fill=None):
    t, c = d_pre.shape
    kw = w.shape[0]
    tm = min(tm, t)
    ni = t // tm
    col0 = src_col0 // tc
    ocol0 = out_col0 // tc

    def body(g_ref, gn_ref, x_ref, w_ref, *rest):
        o_ref, dw_ref = rest[-2:]
        i = pl.program_id(1)
        g = g_ref[...].astype(F32)
        g_next = jnp.where(i < ni - 1, gn_ref[...].astype(F32)[0:8], 0.0)
        xv = x_ref[...]
        wv = w_ref[...]
        shifted = [_shift_up(g, g_next, kw - 1 - k) for k in range(kw)]
        d_in = wv[0:1, :] * shifted[0]
        for k in range(1, kw):
            d_in = d_in + wv[k:k + 1, :] * shifted[k]
        o_ref[...] = d_in.astype(o_ref.dtype)
        rows = [jnp.sum(shifted[k] * xv, axis=0, keepdims=True) for k in range(kw)]

        @pl.when(i == 0)
        def _():
            for k in range(kw):
                dw_ref[k:k + 1, :] = rows[k]

        @pl.when(i > 0)
        def _():
            for k in range(kw):
                dw_ref[k:k + 1, :] += rows[k]

    in_specs = [pl.BlockSpec((tm, tc), lambda j, i: (i, j)),
                pl.BlockSpec((16, tc), lambda j, i: (jnp.minimum((i + 1) * (tm // 16), t // 16 - 1), j)),
                pl.BlockSpec((tm, tc), lambda j, i: (i, col0 + j)),
                pl.BlockSpec((kw, tc), lambda j, i: (0, j))]
    operands = [d_pre, d_pre, src, w]
    if fill is not None:
        in_specs.append(pl.BlockSpec(memory_space=pl.ANY))
        operands.append(fill)
    return pl.pallas_call(
        body, name=name, grid=(c // tc, ni), in_specs=in_specs,
        out_specs=[pl.BlockSpec((tm, tc), lambda j, i: (i, ocol0 + j)), pl.BlockSpec((kw, tc), lambda j, i: (0, j))],
        out_shape=[SDS((t, out_cols), BF16), SDS((kw, c), F32)],
        input_output_aliases={} if fill is None else {4: 0},
        compiler_params=_cp(dimension_semantics=("arbitrary", "arbitrary")),
    )(*operands)


GELU_C = 0.7978845608028654


def _gelu_parts(v):
    inner = GELU_C * (v + 0.044715 * v * v * v)
    th = jnp.tanh(inner)
    val = 0.5 * v * (1.0 + th)
    grad = 0.5 * (1.0 + th) + 0.5 * v * (1.0 - th * th) * GELU_C * (1.0 + 3.0 * 0.044715 * v * v)
    return val, grad


def _ffn_act_fwd(up_raw, w, b, *, tm, tc=1408):
    t = up_raw.shape[0]
    tm = min(tm, t)
    nj = FFN_D_FF // tc
    halo = lambda i: jnp.maximum(i * (tm // 8) - 1, 0)

    def body(g_ref, gh_ref, v_ref, vh_ref, wg_ref, wv_ref, bg_ref, bv_ref, o_ref, gate_ref, val_ref):
        first = pl.program_id(0) > 0
        gate = _conv_apply(g_ref[...], jnp.where(first, gh_ref[...], 0.0), wg_ref[...], bg_ref[...], 3)
        val = _conv_apply(v_ref[...], jnp.where(first, vh_ref[...], 0.0), wv_ref[...], bv_ref[...], 3)
        o_ref[...] = (_gelu_parts(gate)[0] * val).astype(BF16)
        gate_ref[...] = gate.astype(BF16)
        val_ref[...] = val.astype(BF16)

    tile = pl.BlockSpec((tm, tc), lambda i, j: (i, j))
    return pl.pallas_call(
        body, name="ffn_act_fwd", grid=(t // tm, nj),
        in_specs=[tile, pl.BlockSpec((8, tc), lambda i, j: (halo(i), j)),
                  pl.BlockSpec((tm, tc), lambda i, j: (i, nj + j)), pl.BlockSpec((8, tc), lambda i, j: (halo(i), nj + j)),
                  pl.BlockSpec((3, tc), lambda i, j: (0, j)), pl.BlockSpec((3, tc), lambda i, j: (0, nj + j)),
                  pl.BlockSpec((1, tc), lambda i, j: (0, j)), pl.BlockSpec((1, tc), lambda i, j: (0, nj + j))],
        out_specs=[tile] * 3, out_shape=[SDS((t, FFN_D_FF), BF16)] * 3,
        compiler_params=_cp(dimension_semantics=("arbitrary", "arbitrary")),
    )(up_raw, up_raw, up_raw, up_raw, w, w, b, b)


def _ffn_act_bwd(gate, val, d_act, *, tm, tc=1408):
    t = gate.shape[0]
    tm = min(tm, t)
    nj = FFN_D_FF // tc

    def body(g_ref, v_ref, da_ref, dg_ref, dv_ref, dbg_ref, dbv_ref):
        i = pl.program_id(1)
        val = v_ref[...].astype(F32)
        ge, dge = _gelu_parts(g_ref[...].astype(F32))
        da = da_ref[...].astype(F32)
        d_gate = da * val * dge
        d_val = da * ge
        dg_ref[...] = d_gate.astype(BF16)
        dv_ref[...] = d_val.astype(BF16)
        _accumulate(dbg_ref, i == 0, jnp.sum(d_gate, axis=0, keepdims=True))
        _accumulate(dbv_ref, i == 0, jnp.sum(d_val, axis=0, keepdims=True))

    tile = pl.BlockSpec((tm, tc), lambda j, i: (i, j))
    row = pl.BlockSpec((1, tc), lambda j, i: (0, j))
    return pl.pallas_call(
        body, name="ffn_act_bwd", grid=(nj, t // tm), in_specs=[tile] * 3, out_specs=[tile, tile, row, row],
        out_shape=[SDS((t, FFN_D_FF), BF16), SDS((t, FFN_D_FF), BF16), SDS((1, FFN_D_FF), F32), SDS((1, FFN_D_FF), F32)],
        compiler_params=_cp(dimension_semantics=("arbitrary", "arbitrary")),
    )(gate, val, d_act)


def _softplus(v):
    e = jnp.exp(-jnp.abs(v))
    small = e * (1.0 - 0.5 * e)
    return jnp.maximum(v, 0.0) + jnp.where(e < 1e-4, small, jnp.log(1.0 + e))


def _dt_fwd(proj, bias_pad, *, tm):
    t = proj.shape[0]
    tm = min(tm, t)

    def body(x_ref, b_ref, g_ref, gt_ref):
        dt = _softplus(x_ref[...] + b_ref[...])
        first8 = _iota((tm, 128), 1) < 8
        for g in range(SSD_N_GROUPS):
            dg = jnp.where(first8, dt if g == 0 else pltpu.roll(dt, 128 - 8 * g, axis=1), 0.0)
            g_ref[g] = dg
            gt_ref[g] = dg.T[0:8, :]

    return pl.pallas_call(
        body, name="dt_fwd", grid=(t // tm,),
        in_specs=[pl.BlockSpec((tm, 128), lambda i: (i, OFF_DT // 128)), pl.BlockSpec((1, 128), lambda i: (0, 0))],
        out_specs=[pl.BlockSpec((SSD_N_GROUPS, tm, 128), lambda i: (0, i, 0)), pl.BlockSpec((SSD_N_GROUPS, 8, tm), lambda i: (0, 0, i))],
        out_shape=[SDS((SSD_N_GROUPS, t, 128), F32), SDS((SSD_N_GROUPS, 8, t), F32)],
        compiler_params=_cp(dimension_semantics=("arbitrary",)),
    )(proj, bias_pad)


def _dt_bwd(d_dtg, proj, bias_pad, d_proj, *, tm):
    t = proj.shape[0]
    tm = min(tm, t)

    def body(g_ref, x_ref, b_ref, _, o_ref, db_ref):
        first8 = _iota((tm, 128), 1) < 8
        d_dt = jnp.where(first8, g_ref[0], 0.0)
        for g in range(1, SSD_N_GROUPS):
            d_dt = d_dt + pltpu.roll(jnp.where(first8, g_ref[g], 0.0), 8 * g, axis=1)
        d_raw = d_dt * _sigmoid(x_ref[...] + b_ref[...])
        o_ref[:, 0:128] = d_raw.astype(BF16)
        o_ref[:, 128:512] = jnp.zeros((tm, 384), BF16)
        _accumulate(db_ref, pl.program_id(0) == 0, jnp.sum(d_raw, axis=0, keepdims=True))

    return pl.pallas_call(
        body, name="dt_bwd", grid=(t // tm,),
        in_specs=[pl.BlockSpec((SSD_N_GROUPS, tm, 128), lambda i: (0, i, 0)), pl.BlockSpec((tm, 128), lambda i: (i, OFF_DT // 128)),
                  pl.BlockSpec((1, 128), lambda i: (0, 0)), pl.BlockSpec(memory_space=pl.ANY)],
        out_specs=[pl.BlockSpec((tm, 512), lambda i: (i, OFF_DT // 512)), pl.BlockSpec((1, 128), lambda i: (0, 0))],
        out_shape=[SDS((t, PROJ_W), BF16), SDS((1, 128), F32)],
        input_output_aliases={3: 0},
        compiler_params=_cp(dimension_semantics=("arbitrary",)),
    )(d_dtg, proj, bias_pad, d_proj)


def _split3(v):
    hi = v.astype(BF16)
    r1 = v - hi.astype(F32)
    mid = r1.astype(BF16)
    return hi, mid, (r1 - mid.astype(F32)).astype(BF16)


def _times01(v, m3):
    return jnp.dot(jnp.concatenate(_split3(v), axis=1), m3, preferred_element_type=F32)


def _01times(m3, v):
    return jnp.dot(m3, jnp.concatenate(_split3(v), axis=0), preferred_element_type=F32)


def _ssd_decay(dt_ref, dtT_ref, al_ref, alT_ref, k):
    dt = dt_ref[0]
    a_row = -jnp.exp(al_ref[0])
    adt_t = dtT_ref[0] * (-jnp.exp(alT_ref[0]))
    return dt, a_row, _01times(k['low3'][...], dt * a_row), _times01(adt_t, k['up3v'][...])


def _ssd_specs(nc, rev):
    ci = (lambda c: nc - 1 - c) if rev else (lambda c: c)
    return [pl.BlockSpec((CHUNK, SSD_CONV_DIM), lambda c: (ci(c), 0)),
            pl.BlockSpec((SSD_N_GROUPS, CHUNK, 128), lambda c: (0, ci(c), 0)),
            pl.BlockSpec((SSD_N_GROUPS, 8, CHUNK), lambda c: (0, 0, ci(c))),
            pl.BlockSpec((SSD_N_GROUPS, 1, 128), lambda c: (0, 0, 0)),
            pl.BlockSpec((SSD_N_GROUPS, 8, 1), lambda c: (0, 0, 0)),
            pl.BlockSpec((1, SSD_D_INNER), lambda c: (0, 0))]


def _ssd_group_views(g, x_ref, dt_ref, dtT_ref, al_ref, alT_ref, d_ref):
    return (x_ref.at[:, g * GROUP_W:(g + 1) * GROUP_W], dt_ref.at[g:g + 1], dtT_ref.at[g:g + 1], al_ref.at[g:g + 1],
            alT_ref.at[g:g + 1], d_ref.at[:, g * 512:(g + 1) * 512])


NT = (((1,), (1,)), ((), ()))
WIDE = 8 * CHUNK
SSD_CONST_NAMES = ('e128', 'e64', 's64', 'mlo', 'mup', 'low3', 'up3', 'up3v')
SSD_CONST_SHAPES = [pltpu.VMEM((3 * CHUNK, WIDE), BF16), pltpu.VMEM((3 * CHUNK, 512), BF16), pltpu.VMEM((512, CHUNK), BF16),
                    pltpu.VMEM((CHUNK, WIDE), F32), pltpu.VMEM((CHUNK, WIDE), F32), pltpu.VMEM((CHUNK, 3 * CHUNK), BF16),
                    pltpu.VMEM((CHUNK, 3 * CHUNK), BF16), pltpu.VMEM((3 * CHUNK, CHUNK), BF16)]


def _ssd_init_consts(k):
    row, col = _iota((3 * CHUNK, WIDE), 0), _iota((3 * CHUNK, WIDE), 1)
    k['e128'][...] = ((col >> 7) == (row & 127)).astype(BF16)
    k['e64'][...] = ((_iota((3 * CHUNK, 512), 1) >> 6) == (_iota((3 * CHUNK, 512), 0) & 127)).astype(BF16)
    k['s64'][...] = ((_iota((512, CHUNK), 0) >> 6) == _iota((512, CHUNK), 1)).astype(BF16)
    row, col = _iota((CHUNK, WIDE), 0), _iota((CHUNK, WIDE), 1)
    k['mlo'][...] = (row >= (col & 127)).astype(F32)
    k['mup'][...] = (row <= (col & 127)).astype(F32)
    row, col = _iota((CHUNK, 3 * CHUNK), 0), _iota((CHUNK, 3 * CHUNK), 1) & 127
    k['low3'][...] = (row >= col).astype(BF16)
    k['up3'][...] = (row <= col).astype(BF16)
    row, col = _iota((3 * CHUNK, CHUNK), 0) & 127, _iota((3 * CHUNK, CHUNK), 1)
    k['up3v'][...] = (row <= col).astype(BF16)


def _ssd_common(x_ref, dt_ref, dtT_ref, al_ref, alT_ref, k):
    dt, a_row, acs, acs_t = _ssd_decay(dt_ref, dtT_ref, al_ref, alT_ref, k)
    ecol = _times01(acs, k['e128'][...])
    rrow = jnp.concatenate([jnp.broadcast_to(acs_t[j:j + 1, :], (CHUNK, CHUNK)) for j in range(8)], axis=1)
    both64 = _times01(jnp.concatenate([acs, dt], axis=0), k['e64'][...])
    a64, dt64 = both64[0:CHUNK], both64[CHUNK:2 * CHUNK]
    a_end64 = a64[CHUNK - 1:CHUNK, :]
    xs = x_ref[:, 0:512]
    return dict(dt=dt, a_row=a_row, acs=acs, seg=ecol - rrow, dt64=dt64, e_a=jnp.exp(a64), decay=jnp.exp(a_end64 - a64),
                e_end64=jnp.exp(a_end64), xs=xs, xdt=xs * dt64, bm=x_ref[:, 512:640], cm=x_ref[:, 640:768])


def _pair_blocks(v):
    lo = _iota((CHUNK, 128), 1) < 64
    out = []
    for i in range(4):
        ch = v[:, i * 128:(i + 1) * 128]
        out.append(jnp.concatenate([jnp.where(lo, ch, 0.0), jnp.where(lo, 0.0, ch)], axis=0).astype(BF16))
    return out


def _tile8(m):
    return jnp.concatenate([m] * 8, axis=1)


def _ssd_fwd(xc, dtg, dtg_t, alog, alog_t, d_exp):
    t = xc.shape[0]
    nc = t // CHUNK

    def body(xa_ref, dta_ref, dtTa_ref, ala_ref, alTa_ref, da_ref, ya_ref, hs_ref, h_scr, *consts):
        c = pl.program_id(0)
        k = dict(zip(SSD_CONST_NAMES, consts))

        @pl.when(c == 0)
        def _():
            _ssd_init_consts(k)
            h_scr[...] = jnp.zeros_like(h_scr)

        for g in range(SSD_N_GROUPS):
            x_ref, dt_ref, dtT_ref, al_ref, alT_ref, d_ref = _ssd_group_views(g, xa_ref, dta_ref, dtTa_ref, ala_ref, alTa_ref, da_ref)
            v = _ssd_common(x_ref, dt_ref, dtT_ref, al_ref, alT_ref, k)
            b16, c16 = v['bm'].astype(BF16), v['cm'].astype(BF16)
            cb = lax.dot_general(c16, b16, NT, preferred_element_type=F32)
            m16 = (jnp.exp(jnp.minimum(v['seg'], 0.0)) * k['mlo'][...] * _tile8(cb)).astype(BF16)
            xbd = _pair_blocks(v['xdt'])
            y_diag = jnp.concatenate([jnp.dot(m16[:, i * 256:(i + 1) * 256], xbd[i], preferred_element_type=F32)
                                      for i in range(4)], axis=1)
            ht = h_scr[g]
            y_off = jnp.dot(c16, ht.astype(BF16), preferred_element_type=F32)
            ya_ref[:, g * 512:(g + 1) * 512] = y_diag + v['e_a'] * y_off + d_ref[...] * v['xs']
            st = jnp.dot(v['bm'].T.astype(BF16), (v['xdt'] * v['decay']).astype(BF16), preferred_element_type=F32)
            hs_ref[0, g] = ht
            h_scr[g] = ht * v['e_end64'] + st

    return pl.pallas_call(
        body, name="ssd_fwd", grid=(nc,), in_specs=_ssd_specs(nc, False),
        out_specs=[pl.BlockSpec((CHUNK, SSD_D_INNER), lambda c: (c, 0)),
                   pl.BlockSpec((1, SSD_N_GROUPS, SSD_D_STATE, 512), lambda c: (c, 0, 0, 0))],
        out_shape=[SDS((t, SSD_D_INNER), F32), SDS((nc, SSD_N_GROUPS, SSD_D_STATE, 512), F32)],
        scratch_shapes=[pltpu.VMEM((SSD_N_GROUPS, SSD_D_STATE, 512), F32)] + SSD_CONST_SHAPES,
        compiler_params=_cp(dimension_semantics=("arbitrary",)),
    )(xc, dtg, dtg_t, alog, alog_t, d_exp)


def _ssd_bwd(xc, dtg, dtg_t, alog, alog_t, d_exp, d_y, hs, ride=None):
    t = xc.shape[0]
    nc = t // CHUNK

    r_ops, r_in_specs, r_outs, r_out_specs, r_scratch = _ride_parts(ride)

    def body(xa_ref, dta_ref, dtTa_ref, ala_ref, alTa_ref, da_ref, dya_ref, hs_ref, *rest):
        r_in, rest = rest[:len(r_ops)], rest[len(r_ops):]
        dxa_ref, ddta_ref, dal_ref, dd_ref = rest[:4]
        r_out, rest = rest[4:4 + len(r_outs)], rest[4 + len(r_outs):]
        g_scr, consts, sems = rest[0], rest[1:1 + len(SSD_CONST_NAMES)], rest[1 + len(SSD_CONST_NAMES):]
        c = pl.program_id(0)
        _ride_run(ride, c == 0, c == nc - 1, r_in, r_out, sems)
        k = dict(zip(SSD_CONST_NAMES, consts))

        @pl.when(c == 0)
        def _():
            _ssd_init_consts(k)
            g_scr[...] = jnp.zeros_like(g_scr)

        for g in range(SSD_N_GROUPS):
            views = _ssd_group_views(g, xa_ref, dta_ref, dtTa_ref, ala_ref, alTa_ref, da_ref)
            one_group(c, g, k, *views, dya_ref.at[:, g * 512:(g + 1) * 512], hs_ref, g_scr,
                      dxa_ref.at[:, g * GROUP_W:(g + 1) * GROUP_W], ddta_ref.at[g:g + 1], dal_ref, dd_ref)

    def one_group(c, g, k, x_ref, dt_ref, dtT_ref, al_ref, alT_ref, d_ref, dy_ref, hs_ref, g_scr, dx_ref, ddt_ref,
                  dal_ref, dd_ref):
        s64, mlo, mup = k['s64'], k['mlo'], k['mup']
        v = _ssd_common(x_ref, dt_ref, dtT_ref, al_ref, alT_ref, k)
        dt, a_row, xs, xdt, e_a, decay = v['dt'], v['a_row'], v['xs'], v['xdt'], v['e_a'], v['decay']
        row, col = _iota((CHUNK, CHUNK), 0), _iota((CHUNK, CHUNK), 1)
        b16, c16 = v['bm'].astype(BF16), v['cm'].astype(BF16)
        ct16 = v['cm'].T.astype(BF16)
        cb = lax.dot_general(c16, b16, NT, preferred_element_type=F32)
        cbt = lax.dot_general(b16, c16, NT, preferred_element_type=F32)
        lmat = jnp.exp(jnp.minimum(v['seg'], 0.0)) * mlo[...]
        lmat_t = jnp.exp(jnp.minimum(-v['seg'], 0.0)) * mup[...]
        mmat, mmat_t = lmat * _tile8(cb), lmat_t * _tile8(cbt)
        mt16 = mmat_t.astype(BF16)
        dy = dy_ref[...]
        dye, xdec = dy * e_a, xdt * decay
        dy16, dye16, xdec16 = dy.astype(BF16), dye.astype(BF16), xdec.astype(BF16)
        xdt16 = xdt.astype(BF16)
        ht, gt = hs_ref[0, g], g_scr[g]
        ht16, gt16 = ht.astype(BF16), gt.astype(BF16)
        xbd, dybd = _pair_blocks(xdt), _pair_blocks(dy)
        d_m, d_mt, d_x = [], [], []
        for i in range(4):
            csl = slice(i * 128, (i + 1) * 128)
            d_m.append(lax.dot_general(dy16[:, csl], xbd[i], NT, preferred_element_type=F32))
            d_mt.append(lax.dot_general(xdt16[:, csl], dybd[i], NT, preferred_element_type=F32))
            d_x.append(jnp.dot(mt16[:, i * 256:(i + 1) * 256], dybd[i], preferred_element_type=F32))
        d_m, d_mt, d_x = jnp.concatenate(d_m, axis=1), jnp.concatenate(d_mt, axis=1), jnp.concatenate(d_x, axis=1)

        def head_sum(m):
            acc = m[:, 0:CHUNK]
            for j in range(1, 8):
                acc = acc + m[:, j * CHUNK:(j + 1) * CHUNK]
            return acc

        d_cb16 = head_sum(d_m * lmat).astype(BF16)
        d_cbt16 = head_sum(d_mt * lmat_t).astype(BF16)
        dseg = d_m * mmat - d_mt * mmat_t
        da_seg = jnp.zeros((CHUNK, CHUNK), F32)
        for j in range(8):
            da_seg = jnp.where(col == j, jnp.sum(dseg[:, j * CHUNK:(j + 1) * CHUNK], axis=1, keepdims=True), da_seg)
        ch = jnp.dot(c16, ht16, preferred_element_type=F32)
        bg = jnp.dot(b16, gt16, preferred_element_type=F32)
        d_x = d_x + decay * bg
        sums = jnp.dot(jnp.concatenate([p.astype(BF16) for p in (xdec * bg, gt * ht, dye * ch, d_x * xs, dy * xs)], axis=0),
                       s64[...], preferred_element_type=F32)
        d_decay, gh, da_off, dxx, dyx = (sums[i * CHUNK:(i + 1) * CHUNK] for i in range(5))
        e_end = jnp.exp(v['acs'][CHUNK - 1:CHUNK, :])
        d_end = e_end * jnp.sum(gh, axis=0, keepdims=True) + jnp.sum(d_decay, axis=0, keepdims=True)
        d_a = da_off - d_decay + da_seg + jnp.where(row == CHUNK - 1, d_end, 0.0)
        dx_ref[:, 0:512] = d_x * v['dt64'] + d_ref[...] * dy
        dx_ref[:, 640:768] = (lax.dot_general(dye16, ht16, NT, preferred_element_type=F32)
                              + jnp.dot(d_cb16, b16, preferred_element_type=F32))
        dx_ref[:, 512:640] = (lax.dot_general(xdec16, gt16, NT, preferred_element_type=F32)
                              + jnp.dot(d_cbt16, c16, preferred_element_type=F32))
        g_scr[g] = gt * v['e_end64'] + jnp.dot(ct16, dye16, preferred_element_type=F32)
        d_adt = _01times(k['up3'][...], d_a)
        ddt_ref[0] = d_adt * a_row + dxx
        d_alog = jnp.sum(d_adt * dt, axis=0, keepdims=True) * a_row
        dd_row = jnp.sum(dyx, axis=0, keepdims=True)
        first = c == 0

        @pl.when(first)
        def _():
            dal_ref[g] = d_alog
            dd_ref[g] = dd_row

        @pl.when(jnp.logical_not(first))
        def _():
            dal_ref[g] += d_alog
            dd_ref[g] += dd_row

    rc = lambda c: nc - 1 - c
    whole = pl.BlockSpec((SSD_N_GROUPS, 1, 128), lambda c: (0, 0, 0))
    return pl.pallas_call(
        body, name="ssd_bwd", grid=(nc,),
        in_specs=_ssd_specs(nc, True) + [pl.BlockSpec((CHUNK, SSD_D_INNER), lambda c: (rc(c), 0)),
                                        pl.BlockSpec((1, SSD_N_GROUPS, SSD_D_STATE, 512), lambda c: (rc(c), 0, 0, 0))] + r_in_specs,
        out_specs=[pl.BlockSpec((CHUNK, SSD_CONV_DIM), lambda c: (rc(c), 0)),
                   pl.BlockSpec((SSD_N_GROUPS, CHUNK, 128), lambda c: (0, rc(c), 0)), whole, whole] + r_out_specs,
        out_shape=[SDS((t, SSD_CONV_DIM), F32), SDS((SSD_N_GROUPS, t, 128), F32),
                   SDS((SSD_N_GROUPS, 1, 128), F32), SDS((SSD_N_GROUPS, 1, 128), F32)] + r_outs,
        scratch_shapes=[pltpu.VMEM((SSD_N_GROUPS, SSD_D_STATE, 512), F32)] + SSD_CONST_SHAPES + r_scratch,
        compiler_params=_cp(dimension_semantics=("arbitrary",)),
    )(xc, dtg, dtg_t, alog, alog_t, d_exp, d_y, hs, *r_ops)


def _gated_norm_fwd(y, proj, w, *, tm):
    t = y.shape[0]
    tm = min(tm, t)

    def body(y_ref, z_ref, w_ref, o_ref):
        gv = y_ref[...] * _silu_parts(z_ref[...])[0]
        r = lax.rsqrt(jnp.mean(gv * gv, axis=-1, keepdims=True) + NORM_EPS)
        o_ref[...] = (gv * r * w_ref[...]).astype(BF16)

    tile = pl.BlockSpec((tm, 512), lambda i, g: (i, g))
    return pl.pallas_call(
        body, name="gated_norm_fwd", grid=(t // tm, SSD_N_GROUPS),
        in_specs=[tile, pl.BlockSpec((tm, 512), lambda i, g: (i, OFF_Z // 512 + g)),
                  pl.BlockSpec((1, 512), lambda i, g: (0, g))], out_specs=tile,
        out_shape=SDS((t, SSD_D_INNER), BF16),
        compiler_params=_cp(dimension_semantics=("arbitrary", "arbitrary")),
    )(y, proj, w)


def _rope(ch, cos_t, sin_t):
    first = (_iota(ch.shape, 1) & 32) == 0
    partner = jnp.where(first, pltpu.roll(ch, 96, axis=1), pltpu.roll(ch, 32, axis=1))
    return ch * cos_t + partner * sin_t


def _rope_qkv(proj, cos_t, sin_t, *, tm):
    t = proj.shape[0]
    tm = min(tm, t)

    def body(q_ref, k_ref, v_ref, c_ref, s_ref, qr_ref, kp_ref, vp_ref, kt_ref, vt_ref):
        cv, sv = c_ref[...], s_ref[...]
        lo = _iota((tm, 128), 1) < 64
        for m in range(8):
            sl = slice(m * 128, (m + 1) * 128)
            qr_ref[:, sl] = (_rope(q_ref[:, sl], cv, sv) * 0.125).astype(BF16)
        for m2 in range(2):
            sl = slice(m2 * 128, (m2 + 1) * 128)
            for src, dst, dst_t in ((_rope(k_ref[:, sl], cv, sv), kp_ref, kt_ref), (v_ref[:, sl], vp_ref, vt_ref)):
                sw = pltpu.roll(src, 64, axis=1)
                padded = (jnp.where(lo, src, 0.0), jnp.where(lo, 0.0, sw), jnp.where(lo, sw, 0.0), jnp.where(lo, 0.0, src))
                for i, pad in enumerate(padded):
                    rows = slice((4 * m2 + i) * 128, (4 * m2 + i + 1) * 128)
                    dst[:, rows] = pad.astype(BF16)
                    dst_t[rows, :] = pad.T.astype(BF16)

    return pl.pallas_call(
        body, name="rope_qkv", grid=(t // tm,),
        in_specs=[pl.BlockSpec((tm, 1024), lambda i: (i, OFF_Q // 1024)), pl.BlockSpec((tm, 256), lambda i: (i, OFF_K // 256)),
                  pl.BlockSpec((tm, 256), lambda i: (i, OFF_V // 256)), pl.BlockSpec((tm, 128), lambda i: (i, 0)),
                  pl.BlockSpec((tm, 128), lambda i: (i, 0))],
        out_specs=[pl.BlockSpec((tm, 1024), lambda i: (i, 0))] * 3 + [pl.BlockSpec((1024, tm), lambda i: (0, i))] * 2,
        out_shape=[SDS((t, 1024), BF16)] * 3 + [SDS((1024, t), BF16)] * 2,
        compiler_params=_cp(dimension_semantics=("arbitrary",)),
    )(proj, proj, proj, cos_t, sin_t)


def _attn_valid(n):
    kj, qi = _iota((2 * CHUNK, CHUNK), 0), _iota((2 * CHUNK, CHUNK), 1)
    return (kj > qi) & (kj <= qi + CHUNK) & ((n > 0) | (kj >= CHUNK))


def _attn_fwd(qr, kp, vt, sinks):
    t = qr.shape[0]
    nb = t // CHUNK

    def body(q_ref, kc_ref, kprev_ref, vc_ref, vprev_ref, sk_ref, o_ref, lse_ref):
        n = pl.program_id(0)
        valid = _attn_valid(n)
        head_row = _iota((16, CHUNK), 0)
        lse_all = jnp.zeros((16, CHUNK), F32)
        for m in range(8):
            g = m // 2
            qch = q_ref[:, m * 128:(m + 1) * 128]
            sls = [slice((2 * g + e) * 128, (2 * g + e + 1) * 128) for e in range(2)]
            kk2 = jnp.concatenate([r[:, sl] for sl in sls for r in (kprev_ref, kc_ref)], axis=0)
            vv2_t = jnp.concatenate([r[sl, :] for sl in sls for r in (vprev_ref, vc_ref)], axis=1)
            s2 = lax.dot_general(kk2, qch, NT, preferred_element_type=F32)
            probs = []
            for e in range(2):
                h = 2 * m + e
                s = jnp.where(valid, s2[2 * CHUNK * e:2 * CHUNK * (e + 1)], NEG)
                sink = sk_ref[0:1, h:h + 1]
                mx = jnp.maximum(jnp.max(s, axis=0, keepdims=True), sink)
                p = jnp.exp(s - mx)
                den = jnp.sum(p, axis=0, keepdims=True) + jnp.exp(sink - mx)
                probs.append((p * (1.0 / den)).astype(BF16))
                lse_all = jnp.where(head_row == h, mx + jnp.log(den), lse_all)
            o_t = jnp.dot(vv2_t, jnp.concatenate(probs, axis=0), preferred_element_type=F32)
            o_ref[:, m * 128:(m + 1) * 128] = o_t.T.astype(BF16)
        lse_ref[0] = lse_all

    cur = pl.BlockSpec((CHUNK, 1024), lambda n: (n, 0))
    prev = pl.BlockSpec((CHUNK, 1024), lambda n: (jnp.maximum(n - 1, 0), 0))
    cur_t = pl.BlockSpec((1024, CHUNK), lambda n: (0, n))
    prev_t = pl.BlockSpec((1024, CHUNK), lambda n: (0, jnp.maximum(n - 1, 0)))
    return pl.pallas_call(
        body, name="attn_fwd", grid=(nb,),
        in_specs=[cur, cur, prev, cur_t, prev_t, pl.BlockSpec((1, 128), lambda n: (0, 0))],
        out_specs=[cur, pl.BlockSpec((1, 16, CHUNK), lambda n: (n, 0, 0))],
        out_shape=[SDS((t, 1024), BF16), SDS((nb, 16, CHUNK), F32)],
        compiler_params=_cp(dimension_semantics=("arbitrary",)),
    )(qr, kp, kp, vt, vt, sinks)


def _attn_bwd(qr, kp, vp, kt, d_o, o, lse, sinks, cos_t, sin_t, d_proj, ride=None):
    t = qr.shape[0]
    nb = t // CHUNK

    r_ops, r_in_specs, r_outs, r_out_specs, r_scratch = _ride_parts(ride)

    def body(q_ref, kc_ref, kprev_ref, vc_ref, vprev_ref, ktc_ref, ktprev_ref, do_ref, o_ref, lse_ref, sk_ref,
             c_ref, s_ref, cp_ref, sp_ref, _, *rest):
        r_in, rest = rest[:len(r_ops)], rest[len(r_ops):]
        dqkv_ref, dsk_ref = rest[:2]
        r_out, rest = rest[2:2 + len(r_outs)], rest[2 + len(r_outs):]
        acc_k, acc_v, dq_scr = rest[:3]
        n = pl.program_id(0)
        _ride_run(ride, n == 0, n == nb, r_in, r_out, rest[3:])
        lane = _iota((CHUNK, 128), 1)
        lo = lane < 64
        lane1 = _iota((1, 128), 1)

        @pl.when(n == 0)
        def _():
            acc_k[...] = jnp.zeros_like(acc_k)
            acc_v[...] = jnp.zeros_like(acc_v)
            dsk_ref[...] = jnp.zeros((1, 128), F32)

        @pl.when(n > 0)
        def _():
            dqkv_ref[:, 0:1024] = dq_scr[...]
            for r in range(8):
                acc_k[r, 0:CHUNK] = acc_k[r, CHUNK:2 * CHUNK]
                acc_v[r, 0:CHUNK] = acc_v[r, CHUNK:2 * CHUNK]
                acc_k[r, CHUNK:2 * CHUNK] = jnp.zeros((CHUNK, 128), F32)
                acc_v[r, CHUNK:2 * CHUNK] = jnp.zeros((CHUNK, 128), F32)

        @pl.when(n < nb)
        def _():
            valid = _attn_valid(n)
            lse_all = lse_ref[0]
            dsk = jnp.zeros((1, 128), F32)
            for m in range(8):
                g = m // 2
                csl = slice(m * 128, (m + 1) * 128)
                qch = q_ref[:, csl]
                doch = do_ref[:, csl]
                prod_t = (doch.astype(F32) * o_ref[:, csl].astype(F32)).T
                sls = [slice((2 * g + e) * 128, (2 * g + e + 1) * 128) for e in range(2)]
                kk2 = jnp.concatenate([r[:, sl] for sl in sls for r in (kprev_ref, kc_ref)], axis=0)
                vv2 = jnp.concatenate([r[:, sl] for sl in sls for r in (vprev_ref, vc_ref)], axis=0)
                kk2_t = jnp.concatenate([r[sl, :] for sl in sls for r in (ktprev_ref, ktc_ref)], axis=1)
                s2 = lax.dot_general(kk2, qch, NT, preferred_element_type=F32)
                d_p2 = lax.dot_general(vv2, doch, NT, preferred_element_type=F32)
                ps, d_ss = [], []
                for e in range(2):
                    h = 2 * m + e
                    rows = slice(2 * CHUNK * e, 2 * CHUNK * (e + 1))
                    lse_h = lse_all[h:h + 1, :]
                    p = jnp.exp(jnp.where(valid, s2[rows], NEG) - lse_h)
                    delta = jnp.sum(prod_t[64 * e:64 * (e + 1)], axis=0, keepdims=True)
                    ps.append(p.astype(BF16))
                    d_ss.append((p * (d_p2[rows] - delta)).astype(BF16))
                    p_sink = jnp.exp(sk_ref[0:1, h:h + 1] - lse_h)
                    dsk = jnp.where(lane1 == h, -jnp.sum(p_sink * delta), dsk)
                d_s2, p2 = jnp.concatenate(d_ss, axis=0), jnp.concatenate(ps, axis=0)
                d_k2 = jnp.dot(d_s2, qch, preferred_element_type=F32)
                d_v2 = jnp.dot(p2, doch, preferred_element_type=F32)
                for e in range(2):
                    rows = slice(2 * CHUNK * e, 2 * CHUNK * (e + 1))
                    acc_k[2 * g + e] += d_k2[rows]
                    acc_v[2 * g + e] += d_v2[rows]
                dq_t = jnp.dot(kk2_t, d_s2, preferred_element_type=F32)
                dq_scr[:, csl] = (_rope(dq_t.T, c_ref[...], -s_ref[...]) * 0.125).astype(BF16)
            dsk_ref[...] += dsk

        @pl.when(n > 0)
        def _():
            for m2 in range(2):
                halves = []
                for g in (2 * m2, 2 * m2 + 1):
                    for acc in (acc_k, acc_v):
                        comb = jnp.where(lo, acc[2 * g, 0:CHUNK], acc[2 * g + 1, 0:CHUNK])
                        halves.append(comb + pltpu.roll(comb, 64, axis=1))
                d_kr = jnp.where(lo, halves[0], halves[2])
                d_v = jnp.where(lo, halves[1], halves[3])
                dqkv_ref[:, OFF_K + m2 * 128:OFF_K + (m2 + 1) * 128] = _rope(d_kr, cp_ref[...], -sp_ref[...]).astype(BF16)
                dqkv_ref[:, OFF_V + m2 * 128:OFF_V + (m2 + 1) * 128] = d_v.astype(BF16)

    qn = lambda n: jnp.minimum(n, nb - 1)
    pn = lambda n: jnp.maximum(jnp.minimum(n, nb) - 1, 0)
    cur = pl.BlockSpec((CHUNK, 1024), lambda n: (qn(n), 0))
    prev = pl.BlockSpec((CHUNK, 1024), lambda n: (pn(n), 0))
    cur128 = pl.BlockSpec((CHUNK, 128), lambda n: (qn(n), 0))
    prev128 = pl.BlockSpec((CHUNK, 128), lambda n: (pn(n), 0))
    cur_t = pl.BlockSpec((1024, CHUNK), lambda n: (0, qn(n)))
    prev_t = pl.BlockSpec((1024, CHUNK), lambda n: (0, pn(n)))
    one = pl.BlockSpec((1, 128), lambda n: (0, 0))
    return pl.pallas_call(
        body, name="attn_bwd", grid=(nb + 1,),
        in_specs=[cur, cur, prev, cur, prev, cur_t, prev_t, cur, cur, pl.BlockSpec((1, 16, CHUNK), lambda n: (qn(n), 0, 0)),
                  one, cur128, cur128, prev128, prev128, pl.BlockSpec(memory_space=pl.ANY)] + r_in_specs,
        out_specs=[pl.BlockSpec((CHUNK, 1536), lambda n: (pn(n), 0)), one] + r_out_specs,
        out_shape=[SDS((t, PROJ_W), BF16), SDS((1, 128), F32)] + r_outs,
        scratch_shapes=[pltpu.VMEM((8, 2 * CHUNK, 128), F32), pltpu.VMEM((8, 2 * CHUNK, 128), F32),
                        pltpu.VMEM((CHUNK, 1024), BF16)] + r_scratch,
        input_output_aliases={15: 0},
        compiler_params=_cp(dimension_semantics=("arbitrary",)),
    )(qr, kp, kp, vp, vp, kt, kt, d_o, o, lse, sinks, cos_t, sin_t, cos_t, sin_t, d_proj, *r_ops)


def _adamw(name, w, g, m, v, *, tr):
    rows, cols = w.shape
    tr = min(tr, rows)
    assert rows % tr == 0

    def body(w_ref, g_ref, m_ref, v_ref, d_ref, nm_ref, nv_ref):
        gv = g_ref[...]
        nm = ADAM_B1 * m_ref[...] + (1.0 - ADAM_B1) * gv
        nv = ADAM_B2 * v_ref[...] + (1.0 - ADAM_B2) * (gv * gv)
        m_hat = nm / (1.0 - ADAM_B1 ** ADAM_STEP)
        v_hat = nv / (1.0 - ADAM_B2 ** ADAM_STEP)
        d_ref[...] = -ADAM_LR * (m_hat / (jnp.sqrt(v_hat) + ADAM_EPS) + ADAM_WD * w_ref[...])
        nm_ref[...] = nm
        nv_ref[...] = nv

    tile = pl.BlockSpec((tr, cols), lambda i: (i, 0))
    return pl.pallas_call(
        body, name=name, grid=(rows // tr,), in_specs=[tile] * 4, out_specs=[tile] * 3,
        out_shape=[SDS((rows, cols), F32)] * 3, compiler_params=_cp(dimension_semantics=("arbitrary",)),
    )(w, g, m, v)


def _local_step(x, cos_t, sin_t, tgt, wb, ps, late=None, rides=None):
    t = x.shape[0]
    tm = min(512, t)
    tmw = min(1024, t)
    ij = lambda i, j, k: (i, j)
    i0 = lambda i, j, k: (i, 0)
    c0 = lambda i, j, k: (0, 0)
    cj = lambda i, j, k: (0, j)
    rides = rides or (lambda group, grads: None)
    rode = {}

    tkt = min(2048, t)
    proj, u, *arrived = _norm_mm("in_proj", x, ps['norm_mix_pre_w'], wb['cat'], tm=tmw, tn=1024,
                                 ride=late[0] if late else None)
    if late:
        more_wb, more_ps = late[1](arrived)
        wb, ps = {**wb, **more_wb}, {**ps, **more_ps}
    xc, xc_pre = _conv_silu_fwd(proj, ps['ssd_conv_w'], ps['ssd_conv_b'], tm=tm)
    bias_pad = jnp.pad(ps['ssd_dt_bias'], ((0, 0), (0, 96)))
    dtg, dtg_t = _dt_fwd(proj, bias_pad, tm=tmw)
    alog = jnp.pad(ps['ssd_a_log'].reshape(SSD_N_GROUPS, 1, 8), ((0, 0), (0, 0), (0, 120)))
    alog_t = ps['ssd_a_log'].reshape(SSD_N_GROUPS, 8, 1)
    d_exp = jnp.repeat(ps['ssd_d'], SSD_HEAD_DIM, axis=1)
    y, hs = _ssd_fwd(xc, dtg, dtg_t, alog, alog_t, d_exp)
    gn = _gated_norm_fwd(y, proj, ps['ssd_norm_w'], tm=tmw)
    qr, kp, vp, kt, vt = _rope_qkv(proj, cos_t, sin_t, tm=tm)
    sinks = jnp.pad(ps['attn_sinks'], ((0, 0), (0, 112)))
    ao, lse = _attn_fwd(qr, kp, vt, sinks)
    y_attn = _mm_plain("attn_out", ao, wb['ao'], tm=tmw, tn=512, tk=1024)

    def merge_ep(acc, i, j, ins, outs):
        gs, ga, ya = ins
        outs[0][...] = (_sigmoid(gs[...]) * acc + _sigmoid(ga[...]) * ya[...]).astype(BF16)
        outs[1][...] = acc

    merged, y_ssd = _mm_call(
        "ssd_out_merge", gn, wb['so'], tm=tmw, tn=512, tk=2048, epilogue=merge_ep,
        extra_in=[(proj, (tmw, 512), lambda i, j, k: (i, OFF_GS // 512 + j)),
                  (proj, (tmw, 512), lambda i, j, k: (i, OFF_GA // 512 + j)), (y_attn, (tmw, 512), ij)],
        outs=[((t, D_MODEL), BF16, (tmw, 512), ij), ((t, D_MODEL), F32, (tmw, 512), ij)])

    def mix_ep(acc, i, j, ins, outs):
        xv, wn = ins
        r = lax.rsqrt(jnp.mean(acc * acc, axis=-1, keepdims=True) + NORM_EPS)
        outs[0][...] = xv[...] + acc * r * wn[...]
        outs[1][...] = acc

    x1, mmix = _mm_call(
        "mix_out", merged, wb['mix'], tm=tm, tn=D_MODEL, tk=1024, epilogue=mix_ep,
        extra_in=[(x, (tm, D_MODEL), i0), (ps['norm_mix_post_w'], (1, D_MODEL), c0)],
        outs=[((t, D_MODEL), F32, (tm, D_MODEL), i0), ((t, D_MODEL), F32, (tm, D_MODEL), i0)])

    up_raw, h = _norm_mm("ffn_up", x1, ps['norm_ffn_pre_w'], wb['up'], tm=tmw, tn=1408)
    act, ffn_gate, ffn_val = _ffn_act_fwd(up_raw, ps['ffn_conv_w'], ps['ffn_conv_b'], tm=tm)

    def loss_ep(acc, i, j, ins, outs):
        x1v, tg, wn = ins
        d_ff_ref, dout_ref, loss_ref, dw_ref = outs
        wv = wn[...]
        r = lax.rsqrt(jnp.mean(acc * acc, axis=-1, keepdims=True) + NORM_EPS)
        err = x1v[...] + acc * r * wv - tg[...]
        dout = err * (1.0 / D_MODEL)
        dout_ref[...] = dout
        d_ff, dw = _rms_bwd(acc, wv, dout)
        d_ff_ref[...] = d_ff.astype(BF16)
        _accumulate(dw_ref, i == 0, dw)
        _accumulate(loss_ref, i == 0, jnp.sum(err * err, keepdims=True) * (0.5 / D_MODEL))

    d_ff, dout, loss, g_norm_ffn_post = _mm_call(
        "ffn_down_loss", act, wb['dn'], tm=tm, tn=D_MODEL, tk=FFN_D_FF, epilogue=loss_ep,
        extra_in=[(x1, (tm, D_MODEL), i0), (tgt, (tm, D_MODEL), i0), (ps['norm_ffn_post_w'], (1, D_MODEL), c0)],
        outs=[((t, D_MODEL), BF16, (tm, D_MODEL), i0), ((t, D_MODEL), F32, (tm, D_MODEL), i0),
              ((1, 1), F32, (1, 1), c0), ((1, D_MODEL), F32, (1, D_MODEL), c0)])

    d_act = _mm_plain("d_act", d_ff, wb['dn_t'], tm=tmw, tn=1408, tk=1024, out_dtype=BF16)
    g_w_down = _mm_plain("g_w_down", act, d_ff, tm=1408, tn=1024, tk=tkt, trans_a=True, out_dtype=BF16)
    d_gate, d_val, db_g, db_v = _ffn_act_bwd(ffn_gate, ffn_val, d_act, tm=tm)
    d_up_raw, gcw_g = _conv_bwd2("ffn_conv_bwd2_gate", d_gate, up_raw, 0, ps['ffn_conv_w'][:, :FFN_D_FF], tm=tm,
                                 tc=1408, out_cols=2 * FFN_D_FF, out_col0=0)
    d_up_raw, gcw_v = _conv_bwd2("ffn_conv_bwd2_val", d_val, up_raw, FFN_D_FF, ps['ffn_conv_w'][:, FFN_D_FF:], tm=tm,
                                 tc=1408, out_cols=2 * FFN_D_FF, out_col0=FFN_D_FF, fill=d_up_raw)
    g_ffn_conv_w = jnp.concatenate([gcw_g, gcw_v], axis=1)

    def dx1_ep(acc, i, j, ins, outs):
        x1v, wpre, dout_v, mmv, wpost = ins
        d_x1_ref, d_mm_ref, dwpre_ref, dwpost_ref = outs
        d_n, dw_pre = _rms_bwd(x1v[...], wpre[...], acc)
        d_x1 = dout_v[...] + d_n
        d_x1_ref[...] = d_x1
        d_mm, dw_post = _rms_bwd(mmv[...], wpost[...], d_x1)
        d_mm_ref[...] = d_mm.astype(BF16)
        _accumulate(dwpre_ref, i == 0, dw_pre)
        _accumulate(dwpost_ref, i == 0, dw_post)

    d_x1, d_mm, g_norm_ffn_pre, g_norm_mix_post = _mm_call(
        "d_h", d_up_raw, wb['up_t'], tm=tm, tn=D_MODEL, tk=2 * FFN_D_FF, epilogue=dx1_ep, vmem_mb=VMEM_BIG_MB,
        extra_in=[(x1, (tm, D_MODEL), i0), (ps['norm_ffn_pre_w'], (1, D_MODEL), c0), (dout, (tm, D_MODEL), i0),
                  (mmix, (tm, D_MODEL), i0), (ps['norm_mix_post_w'], (1, D_MODEL), c0)],
        outs=[((t, D_MODEL), F32, (tm, D_MODEL), i0), ((t, D_MODEL), BF16, (tm, D_MODEL), i0),
              ((1, D_MODEL), F32, (1, D_MODEL), c0), ((1, D_MODEL), F32, (1, D_MODEL), c0)])
    g_w_up_t = _mm_plain("g_w_up", d_up_raw, h, tm=1408, tn=1024, tk=tkt, trans_a=True, out_dtype=BF16)
    ride_ffn = rides('ffn', {'ffn_w_up': g_w_up_t, 'ffn_w_down': g_w_down})

    def dmerge_ep(acc, i, j, ins, outs):
        gs, ga, ys, ya = ins
        sg_s, sg_a = _sigmoid(gs[...]), _sigmoid(ga[...])
        outs[0][...] = (acc * sg_s).astype(BF16)
        outs[1][...] = (acc * sg_a).astype(BF16)
        outs[2][:, 0:D_MODEL] = (acc * ys[...] * sg_s * (1.0 - sg_s)).astype(BF16)
        outs[2][:, D_MODEL:2 * D_MODEL] = (acc * ya[...] * sg_a * (1.0 - sg_a)).astype(BF16)

    d_yssd, d_yattn, d_proj = _mm_call(
        "d_merged", d_mm, wb['mix_t'], tm=tm, tn=D_MODEL, tk=1024, epilogue=dmerge_ep,
        extra_in=[(proj, (tm, D_MODEL), lambda i, j, k: (i, OFF_GS // D_MODEL)),
                  (proj, (tm, D_MODEL), lambda i, j, k: (i, OFF_GA // D_MODEL)), (y_ssd, (tm, D_MODEL), i0), (y_attn, (tm, D_MODEL), i0)],
        outs=[((t, D_MODEL), BF16, (tm, D_MODEL), i0), ((t, D_MODEL), BF16, (tm, D_MODEL), i0),
              ((t, PROJ_W), BF16, (tm, 2 * D_MODEL), lambda i, j, k: (i, OFF_GS // (2 * D_MODEL)))])
    g_w_mix = _mm_plain("g_w_mix", merged, d_mm, tm=1024, tn=1024, tk=tkt, trans_a=True, out_dtype=BF16)

    def dgn_ep(acc, i, j, ins, outs):
        yv, zv, wn = ins
        d_y_ref, d_z_ref, dw_ref = outs
        zz = zv[...]
        sz = _sigmoid(zz)
        silu = zz * sz
        gv = yv[...] * silu
        r = lax.rsqrt(jnp.mean(gv * gv, axis=-1, keepdims=True) + NORM_EPS)
        gh = gv * r
        dgh = acc * wn[...]
        dg = r * (dgh - gh * jnp.mean(dgh * gh, axis=-1, keepdims=True))
        d_y_ref[...] = dg * silu
        d_z_ref[...] = (dg * yv[...] * (sz * (1.0 + zz * (1.0 - sz)))).astype(BF16)
        dw = jnp.sum(acc * gh, axis=0, keepdims=True)

        @pl.when(i == 0)
        def _():
            dw_ref[j] = dw

        @pl.when(i > 0)
        def _():
            dw_ref[j] += dw

    d_y, d_proj, g_ssd_norm = _mm_call(
        "d_gn", d_yssd, wb['so_t'], tm=tmw, tn=512, tk=1024, epilogue=dgn_ep, fill=(d_proj, 1),
        extra_in=[(y, (tmw, 512), ij), (proj, (tmw, 512), lambda i, j, k: (i, OFF_Z // 512 + j)), (ps['ssd_norm_w'], (1, 512), cj)],
        outs=[((t, SSD_D_INNER), F32, (tmw, 512), ij), ((t, PROJ_W), BF16, (tmw, 512), lambda i, j, k: (i, OFF_Z // 512 + j)),
              ((SSD_N_GROUPS, 1, 512), F32, (SSD_N_GROUPS, 1, 512), lambda i, j, k: (0, 0, 0))])
    g_ssd_norm = g_ssd_norm.reshape(1, SSD_D_INNER)
    g_w_so = _mm_plain("g_w_so", gn, d_yssd, tm=1024, tn=1024, tk=tkt, trans_a=True, out_dtype=BF16)
    d_xc, d_dtg, d_alog, d_dd, *rode['ffn'] = _ssd_bwd(xc, dtg, dtg_t, alog, alog_t, d_exp, d_y, hs, ride=ride_ffn)
    d_pre, g_ssd_conv_b = _conv_silu_bwd1(d_xc, xc_pre, tm=tm)
    d_proj, g_ssd_conv_w = _conv_bwd2("ssd_conv_bwd2", d_pre, proj, OFF_XBC, ps['ssd_conv_w'], tm=tm, tc=1536,
                                      out_cols=PROJ_W, out_col0=OFF_XBC, fill=d_proj)
    d_proj, g_dt_bias = _dt_bwd(d_dtg, proj, bias_pad, d_proj, tm=tmw)

    d_ao = _mm_plain("d_ao", d_yattn, wb['ao_t'], tm=tmw, tn=512, tk=1024, out_dtype=BF16)
    g_w_ao = _mm_plain("g_w_ao", ao, d_yattn, tm=1024, tn=1024, tk=tkt, trans_a=True, out_dtype=BF16)
    ride_mix = rides('mix', {'ssd_w_out': g_w_so, 'attn_w_out': g_w_ao, 'w_mix_out': g_w_mix})
    d_proj, g_sinks, *rode['mix'] = _attn_bwd(qr, kp, vp, kt, d_ao, ao, lse, sinks, cos_t, sin_t, d_proj, ride=ride_mix)

    def dx_ep(acc, i, j, ins, outs):
        xv, wn, dx1v = ins
        d_n, dw = _rms_bwd(xv[...], wn[...], acc)
        outs[0][...] = dx1v[...] + d_n
        _accumulate(outs[1], i == 0, dw)

    g_cat_t = _mm_plain("g_w_in", d_proj, u, tm=1024, tn=1024, tk=tkt, trans_a=True, out_dtype=BF16)
    grad_x, g_norm_mix_pre, *rode['w_in'] = _mm_call(
        "d_u", d_proj, wb['cat_t'], tm=tm, tn=D_MODEL, tk=PROJ_W, epilogue=dx_ep, ride=rides('w_in', {'w_in': g_cat_t}),
        vmem_mb=VMEM_BIG_MB,
        extra_in=[(x, (tm, D_MODEL), i0), (ps['norm_mix_pre_w'], (1, D_MODEL), c0), (d_x1, (tm, D_MODEL), i0)],
        outs=[((t, D_MODEL), F32, (tm, D_MODEL), i0), ((1, D_MODEL), F32, (1, D_MODEL), c0)])

    grads = {
        'norm_mix_pre_w': g_norm_mix_pre, 'w_in': g_cat_t, 'ssd_conv_w': g_ssd_conv_w, 'ssd_conv_b': g_ssd_conv_b,
        'ssd_dt_bias': g_dt_bias[:, :SSD_N_HEADS], 'ssd_a_log': d_alog[:, 0, :8].reshape(1, SSD_N_HEADS),
        'ssd_d': d_dd[:, 0, :8].reshape(1, SSD_N_HEADS), 'ssd_norm_w': g_ssd_norm, 'ssd_w_out': g_w_so,
        'attn_sinks': g_sinks[:, :ATTN_N_HEADS], 'attn_w_out': g_w_ao, 'w_mix_out': g_w_mix,
        'norm_mix_post_w': g_norm_mix_post, 'norm_ffn_pre_w': g_norm_ffn_pre, 'ffn_w_up': g_w_up_t,
        'ffn_conv_w': g_ffn_conv_w, 'ffn_conv_b': jnp.concatenate([db_g, db_v], axis=1), 'ffn_w_down': g_w_down,
        'norm_ffn_post_w': g_norm_ffn_post,
    }
    return loss, grad_x, grads, rode


def _group_channels(a):
    parts = []
    for g in range(SSD_N_GROUPS):
        parts += [a[..., 512 * g:512 * (g + 1)], a[..., 2048 + 128 * g:2048 + 128 * (g + 1)],
                  a[..., 2560 + 128 * g:2560 + 128 * (g + 1)]]
    return jnp.concatenate(parts, axis=-1)


def _ungroup_channels(a):
    xs = [a[..., GROUP_W * g:GROUP_W * g + 512] for g in range(SSD_N_GROUPS)]
    bs = [a[..., GROUP_W * g + 512:GROUP_W * g + 640] for g in range(SSD_N_GROUPS)]
    cs = [a[..., GROUP_W * g + 640:GROUP_W * (g + 1)] for g in range(SSD_N_GROUPS)]
    return jnp.concatenate(xs + bs + cs, axis=-1)


def _proj_rows(a_t, lo, hi):
    out = []
    for start, length, dst in sorted(PROJ_SEGS):
        s, e = max(lo, start), min(hi, start + length)
        if s < e:
            out.append(a_t[dst + s - start:dst + e - start])
    return out


def _to_proj_layout(w_in_t):
    pieces, pos = [], 0
    for start, length, dst in sorted(PROJ_SEGS, key=lambda s: s[2]):
        if dst > pos:
            pieces.append(jnp.zeros((dst - pos, w_in_t.shape[1]), w_in_t.dtype))
        pieces.append(w_in_t[start:start + length])
        pos = dst + length
    if pos < PROJ_W:
        pieces.append(jnp.zeros((PROJ_W - pos, w_in_t.shape[1]), w_in_t.dtype))
    return jnp.concatenate(pieces, axis=0)


def _rope_tables(positions):
    half = 32
    inv_freq = ROPE_THETA ** (-jnp.arange(half, dtype=F32) * 2.0 / 64)
    ang = positions.astype(F32)[:, None] * inv_freq
    cos, sin = jnp.cos(ang), jnp.sin(ang)
    return jnp.concatenate([cos, cos, cos, cos], axis=1), jnp.concatenate([-sin, sin, -sin, sin], axis=1)


def _matmul_weights(w_in_t):
    cat_t = _to_proj_layout(w_in_t)
    return {'cat': cat_t.T, 'cat_t': cat_t}


def _late_weights(so, ao, mix, up_t, dn):
    return {'so': so, 'so_t': so.T, 'ao': ao, 'ao_t': ao.T, 'mix': mix, 'mix_t': mix.T,
            'up': up_t.T, 'up_t': up_t, 'dn': dn, 'dn_t': dn.T}


ANY = pl.BlockSpec(memory_space=pl.ANY)
MESH = pl.DeviceIdType.MESH
ROW_ALIGN = 32


def _mesh_pos():
    return lax.axis_index("x"), lax.axis_index("y"), lax.axis_index("c")


def _other_chips(x, y):
    return [(1 - x, y), (x, 1 - y), (1 - x, 1 - y)]


def _remote(src, dst, send_sems, recv_sems, k, to):
    return pltpu.make_async_remote_copy(src_ref=src, dst_ref=dst, send_sem=send_sems.at[k], recv_sem=recv_sems.at[k],
                                        device_id=to, device_id_type=MESH)


def _half(c, rh):
    return pl.ds(pl.multiple_of(c * rh, 16), rh)


def _ag_ride(shard):
    r = shard.shape[0]
    rh = r // 2

    def first_copies(w_ref, out_ref, send_sems, recv_sems):
        x, y, c = _mesh_pos()
        p = 2 * x + y
        mine = _half(c, rh)
        cps = [_remote(w_ref, out_ref.at[p], send_sems, recv_sems, 6, (x, y, 1 - c))]
        return cps + [_remote(w_ref.at[mine], out_ref.at[p, mine], send_sems, recv_sems, j, (cx, cy, c))
                      for j, (cx, cy) in enumerate(_other_chips(x, y))]

    def start(ins, outs, send_sems, recv_sems):
        for cp in first_copies(ins[0], outs[0], send_sems, recv_sems):
            cp.start()

    def forwards(out_ref, send_sems, recv_sems, half):
        x, y, c = _mesh_pos()
        return [_remote(out_ref.at[2 * cx + cy, half], out_ref.at[2 * cx + cy, half], send_sems, recv_sems, 3 + j, (x, y, 1 - c))
                for j, (cx, cy) in enumerate(_other_chips(x, y))]

    def middle(ins, outs, send_sems, recv_sems):
        x, y, c = _mesh_pos()
        mine = _half(c, rh)
        for j, (fwd, (cx, cy)) in enumerate(zip(forwards(outs[0], send_sems, recv_sems, mine), _other_chips(x, y))):
            slab = outs[0].at[2 * cx + cy, mine]
            _remote(slab, slab, send_sems, recv_sems, j, (x, y, 1 - c)).wait_recv()
            fwd.start()

    def finish(ins, outs, send_sems, recv_sems):
        w_ref, out_ref = ins[0], outs[0]
        x, y, c = _mesh_pos()
        for cp in forwards(out_ref, send_sems, recv_sems, _half(1 - c, rh)):
            cp.wait_recv()
        _remote(w_ref, out_ref.at[2 * x + y], send_sems, recv_sems, 6, (x, y, 1 - c)).wait_recv()
        for cp in first_copies(w_ref, out_ref, send_sems, recv_sems) + forwards(out_ref, send_sems, recv_sems, _half(c, rh)):
            cp.wait_send()

    return _Ride((shard,), (SDS((N_CHIPS, r, COMM_LANES), shard.dtype),), 7, start, finish, middle)


def _rs_ride(gbuf):
    rh = gbuf.shape[1] // 2

    def copies(g_ref, r_ref, send_sems, recv_sems, landing):
        x, y, c = _mesh_pos()
        cps = []
        for k, (cx, cy) in enumerate(_other_chips(x, y)):
            for h in range(2):
                slot = 2 * k + c if landing else 2 * k + h
                cps.append(pltpu.make_async_remote_copy(
                    src_ref=g_ref.at[2 * cx + cy, pl.ds(h * rh, rh)], dst_ref=r_ref.at[slot],
                    send_sem=send_sems.at[2 * k + h], recv_sem=recv_sems.at[slot],
                    device_id=(cx, cy, h), device_id_type=MESH))
        cps.append(_remote(g_ref.at[2 * x + y, _half(1 - c, rh)], r_ref.at[6], send_sems, recv_sems, 6, (x, y, 1 - c)))
        return cps

    def start(ins, outs, send_sems, recv_sems):
        for cp in copies(ins[0], outs[0], send_sems, recv_sems, True):
            cp.start()

    def finish(ins, outs, send_sems, recv_sems):
        for cp in copies(ins[0], outs[0], send_sems, recv_sems, False):
            cp.wait()

    return _Ride((gbuf,), (SDS((7, rh, COMM_LANES), gbuf.dtype),), 7, start, finish)


def _rs_sum(name, gbuf, got, pc_idx):
    rh = got.shape[1]
    tr = max(d for d in range(16, 513, 16) if rh % d == 0)
    nb = rh // tr

    def body(pc_ref, own_ref, *refs):
        o_ref = refs[7]
        p, c = pc_ref[0], pc_ref[1]
        own = own_ref[0].astype(F32)
        slots = [r[0].astype(F32) for r in refs[:7]]

        def term(q, h):
            code = p ^ q
            far = jnp.where(code == 2, slots[h], jnp.where(code == 1, slots[2 + h], slots[4 + h]))
            return jnp.where(code == 0, jnp.where(c == h, own, slots[6]), far)

        acc = term(0, 0)
        for q, h in [(0, 1), (1, 0), (1, 1), (2, 0), (2, 1), (3, 0), (3, 1)]:
            acc = acc + term(q, h)
        o_ref[0] = acc

    slot = lambda s: pl.BlockSpec((1, tr, COMM_LANES), lambda i, pc: (s, i, 0))
    return pl.pallas_call(
        body, name=name,
        grid_spec=pltpu.PrefetchScalarGridSpec(
            num_scalar_prefetch=1, grid=(nb,),
            in_specs=[pl.BlockSpec((1, tr, COMM_LANES), lambda i, pc: (pc[0], pc[1] * nb + i, 0))] + [slot(s) for s in range(7)],
            out_specs=pl.BlockSpec((1, tr, COMM_LANES), lambda i, pc: (pc[1], i, 0))),
        out_shape=SDS((2, rh, COMM_LANES), F32), compiler_params=_cp(dimension_semantics=("arbitrary",)),
    )(pc_idx, gbuf, *([got] * 7))


def _pair_gather_all(bufs):
    n = len(bufs)

    def body(*refs):
        outs, send_sems, recv_sems = refs[n:2 * n], refs[2 * n], refs[2 * n + 1]
        x, y, c = _mesh_pos()
        cps = [_remote(o.at[c], o.at[c], send_sems, recv_sems, k, (x, y, 1 - c)) for k, o in enumerate(outs)]
        for cp in cps:
            cp.start()
        for k, o in enumerate(outs):
            _remote(o.at[1 - c], o.at[1 - c], send_sems, recv_sems, k, (x, y, 1 - c)).wait_recv()
        for cp in cps:
            cp.wait_send()

    return pl.pallas_call(
        body, name="grad_pair_gather", in_specs=[ANY] * n, out_specs=[ANY] * n,
        out_shape=[SDS(b.shape, b.dtype) for b in bufs],
        scratch_shapes=[pltpu.SemaphoreType.DMA((n,)), pltpu.SemaphoreType.DMA((n,))],
        input_output_aliases={k: k for k in range(n)},
    )(*bufs)


def _pack_rows(big, small=()):
    parts = list(big)
    if small:
        flat = jnp.concatenate([p.reshape(-1) for p in small])
        k = -(-flat.shape[0] // (16 * COMM_LANES)) * 16
        parts.append(jnp.pad(flat, (0, k * COMM_LANES - flat.shape[0])).reshape(k, COMM_LANES))
    pad = -sum(p.shape[0] for p in parts) % ROW_ALIGN
    if pad:
        parts.append(jnp.zeros((pad, COMM_LANES), parts[0].dtype))
    return jnp.concatenate(parts, axis=0) if len(parts) > 1 else parts[0]


def _take(flat, off, shape):
    n = 1
    for d in shape:
        n *= d
    return flat[off:off + n].reshape(shape), off + n


BIG_ROWS = {'w_in': 2184, 'ssd_w_out': 512, 'attn_w_out': 256, 'w_mix_out': 256, 'ffn_w_up': 1408, 'ffn_w_down': 704}
TRANSPOSED = ('w_in', 'ffn_w_up')
LATE = ('ssd_w_out', 'attn_w_out', 'w_mix_out', 'ffn_w_up', 'ffn_w_down')
CONV_TAPS = ('ssd_conv_w', 'ffn_conv_w')
RS_GROUPS = {'ffn': ('ffn_w_up', 'ffn_w_down'), 'mix': ('ssd_w_out', 'attn_w_out', 'w_mix_out'), 'w_in': ('w_in',)}


def _exchange(name, ride):
    n_in, n_out = len(ride.ins), len(ride.outs)

    def body(*refs):
        ins, outs, sems = refs[:n_in], refs[n_in:n_in + n_out], refs[n_in + n_out:]
        ride.start(ins, outs, *sems)
        if ride.middle is not None:
            ride.middle(ins, outs, *sems)
        ride.finish(ins, outs, *sems)

    return pl.pallas_call(
        body, name=name, in_specs=[ANY] * n_in, out_specs=[ANY] * n_out, out_shape=list(ride.outs),
        scratch_shapes=[pltpu.SemaphoreType.DMA((ride.n_sems,)), pltpu.SemaphoreType.DMA((ride.n_sems,))],
    )(*ride.ins)


def kernel(x, positions, norm_mix_pre_w, w_in, ssd_conv_w, ssd_conv_b, ssd_dt_bias, ssd_a_log, ssd_d, ssd_norm_w, ssd_w_out, attn_sinks, attn_w_out, w_mix_out, norm_mix_post_w, norm_ffn_pre_w, ffn_w_up, ffn_conv_w, ffn_conv_b, ffn_w_down, norm_ffn_post_w, loss_target, m_norm_mix_pre_w, m_w_in, m_ssd_conv_w, m_ssd_conv_b, m_ssd_dt_bias, m_ssd_a_log, m_ssd_d, m_ssd_norm_w, m_ssd_w_out, m_attn_sinks, m_attn_w_out, m_w_mix_out, m_norm_mix_post_w, m_norm_ffn_pre_w, m_ffn_w_up, m_ffn_conv_w, m_ffn_conv_b, m_ffn_w_down, m_norm_ffn_post_w, v_norm_mix_pre_w, v_w_in, v_ssd_conv_w, v_ssd_conv_b, v_ssd_dt_bias, v_ssd_a_log, v_ssd_d, v_ssd_norm_w, v_ssd_w_out, v_attn_sinks, v_attn_w_out, v_w_mix_out, v_norm_mix_post_w, v_norm_ffn_pre_w, v_ffn_w_up, v_ffn_conv_w, v_ffn_conv_b, v_ffn_w_down, v_norm_ffn_post_w):
    given = dict(locals())
    w = {n: given[n][0] for n in WEIGHTS}
    w = {n: (a if a.ndim == 2 else a[None]) for n, a in w.items()}
    mom_m = {n: given['m_' + n].reshape(w[n].shape) for n in WEIGHTS}
    mom_v = {n: given['v_' + n].reshape(w[n].shape) for n in WEIGHTS}
    cx, cy, cc = _mesh_pos()
    pc_idx = jnp.stack([2 * cx + cy, cc]).astype(jnp.int32)

    rows_of = lambda n: (w[n].T if n in TRANSPOSED else w[n]).astype(BF16)
    gathered = _exchange("w_in_all_gather", _ag_ride(_pack_rows([rows_of('w_in')])))[0]
    wb = _matmul_weights(jnp.concatenate([gathered[s, :BIG_ROWS['w_in']] for s in range(N_CHIPS)], axis=0))
    taps = [lax.bitcast_convert_type(w[n], BF16) for n in CONV_TAPS]

    def unpack_late(arrived):
        rows, conv = {n: [] for n in LATE}, {n: [] for n in CONV_TAPS}
        for s in range(N_CHIPS):
            r0 = 0
            for n in LATE:
                rows[n].append(arrived[0][s, r0:r0 + BIG_ROWS[n]])
                r0 += BIG_ROWS[n]
            flat, off = arrived[0][s, r0:r0 + 16].reshape(-1), 0
            for n in CONV_TAPS:
                a, off = _take(flat, off, w[n].shape + (2,))
                conv[n].append(lax.bitcast_convert_type(a, F32))
        full = {n: jnp.concatenate(rows[n], axis=0) for n in LATE}
        return (_late_weights(*[full[n] for n in LATE]),
                {'ssd_conv_w': _group_channels(jnp.concatenate(conv['ssd_conv_w'], axis=1)),
                 'ffn_conv_w': jnp.concatenate(conv['ffn_conv_w'], axis=1)})

    late = (_ag_ride(_pack_rows([rows_of(n) for n in LATE], taps)), unpack_late)

    sent = {}

    def rides(group, g):
        parts = []
        for s in range(N_CHIPS):
            slab = []
            for n in RS_GROUPS[group]:
                lo, hi = BIG_ROWS[n] * s, BIG_ROWS[n] * (s + 1)
                slab += _proj_rows(g[n], lo, hi) if n == 'w_in' else [g[n][lo:hi]]
            slab = [a.astype(BF16) for a in slab]
            pad = -sum(a.shape[0] for a in slab) % ROW_ALIGN
            parts += slab + ([jnp.zeros((pad, COMM_LANES), BF16)] if pad else [])
        sent[group] = jnp.concatenate(parts, axis=0).reshape(N_CHIPS, -1, COMM_LANES)
        return _rs_ride(sent[group])

    ps = {n: w[n] for n in REPLICATED}
    ps['ssd_conv_b'] = _group_channels(w['ssd_conv_b'])
    cos_t, sin_t = _rope_tables(positions[0])
    loss, grad_x, grads, rode = _local_step(x[0], cos_t, sin_t, loss_target[0], wb, ps, late, rides)
    grads['ssd_conv_w'] = _ungroup_channels(grads['ssd_conv_w'])
    grads['ssd_conv_b'] = _ungroup_channels(grads['ssd_conv_b'])

    shard_cols = {n: sh[1] for n, _, sh in SHARDED}
    parts = []
    for s in range(N_CHIPS):
        small = [grads[n][:, shard_cols[n] * s:shard_cols[n] * (s + 1)] for n in CONV_TAPS] + [grads[n] for n in REPLICATED]
        flat = _pack_rows([], small)
        high = flat.astype(BF16)
        parts += [high, (flat - high.astype(F32)).astype(BF16)]
    sent['small'] = jnp.concatenate(parts, axis=0).reshape(N_CHIPS, -1, COMM_LANES)
    rode['small'] = _exchange("grad_small_exchange", _rs_ride(sent['small']))

    groups = ('ffn', 'mix', 'w_in', 'small')
    red = _pair_gather_all([_rs_sum("grad_sum_" + g, sent[g], rode[g][0], pc_idx) for g in groups])
    red = {g: r.reshape(-1, COMM_LANES) for g, r in zip(groups, red)}
    g_red = {}
    for g in groups[:3]:
        r0 = 0
        for n in RS_GROUPS[g]:
            g_red[n] = red[g][r0:r0 + BIG_ROWS[n]].T if n in TRANSPOSED else red[g][r0:r0 + BIG_ROWS[n]]
            r0 += BIG_ROWS[n]
    half = red['small'].shape[0] // 2
    flat, off = (red['small'][:half] + red['small'][half:]).reshape(-1), 0
    for n in CONV_TAPS + REPLICATED:
        g_red[n], off = _take(flat, off, w[n].shape)

    small_names = [n for n in WEIGHTS if n not in MATMUL_WEIGHTS]
    delta, new_m, new_v = {}, {}, {}
    for n in MATMUL_WEIGHTS:
        delta[n], new_m[n], new_v[n] = _adamw("adamw_" + n, w[n], g_red[n], mom_m[n], mom_v[n],
                                                  tr=max(d for d in range(8, 353, 8) if w[n].shape[0] % d == 0))
    packed = [_pack_small([d[n] for n in small_names]) for d in (w, g_red, mom_m, mom_v)]
    outs = _adamw("adamw_small", *packed, tr=packed[0].shape[0])
    for res, o in zip((delta, new_m, new_v), outs):
        fl, off = o.reshape(-1), 0
        for n in small_names:
            res[n], off = _take(fl, off, w[n].shape)

    loss_all = lax.psum(loss[0, 0], ("x", "y", "c"))
    shaped = lambda d: [d[n].reshape(given[n].shape) for n in WEIGHTS]
    return (loss_all, grad_x[None], *shaped(g_red), *shaped(delta), *shaped(new_m), *shaped(new_v))


def _pack_small(pieces):
    flat = jnp.concatenate([p.reshape(-1) for p in pieces])
    rows = -(-flat.shape[0] // (128 * 8)) * 8
    return jnp.pad(flat, (0, rows * 128 - flat.shape[0])).reshape(rows, 128)
```

```python
from typing import Callable, NamedTuple

import jax
import jax.numpy as jnp
from jax import lax
from jax.experimental import pallas as pl
from jax.experimental.pallas import tpu as pltpu

F32 = jnp.float32
BF16 = jnp.bfloat16
SDS = jax.ShapeDtypeStruct

D_MODEL = 1024
SSD_D_INNER = 2048
SSD_N_HEADS = 32
SSD_HEAD_DIM = 64
SSD_N_GROUPS = 4
SSD_D_STATE = 128
SSD_CONV_DIM = 3072
CHUNK = 128
ATTN_N_HEADS = 16
FFN_D_FF = 2816
ROPE_THETA = 10000.0
NORM_EPS = 1e-6
ADAM_LR, ADAM_B1, ADAM_B2, ADAM_EPS, ADAM_WD, ADAM_STEP = 0.001, 0.9, 0.999, 1e-08, 0.01, 10

PROJ_W = 9216
OFF_Q, OFF_K, OFF_V, OFF_Z, OFF_DT, OFF_GS, OFF_GA, OFF_XBC = 0, 1024, 1280, 1536, 3584, 4096, 5120, 6144
GROUP_W = 768
PROJ_SEGS = ([(0, 2048, OFF_Z)]
             + [(2048 + 512 * g, 512, OFF_XBC + GROUP_W * g) for g in range(4)]
             + [(4096 + 128 * g, 128, OFF_XBC + GROUP_W * g + 512) for g in range(4)]
             + [(4608 + 128 * g, 128, OFF_XBC + GROUP_W * g + 640) for g in range(4)]
             + [(5120, 32, OFF_DT), (5152, 1024, OFF_Q), (6176, 256, OFF_K), (6432, 256, OFF_V),
                (6688, 1024, OFF_GS), (7712, 1024, OFF_GA)])
VMEM_LIMIT_MB = 48
VMEM_BIG_MB = 60
NEG = -1e30

WEIGHTS = ('norm_mix_pre_w', 'w_in', 'ssd_conv_w', 'ssd_conv_b', 'ssd_dt_bias', 'ssd_a_log', 'ssd_d', 'ssd_norm_w',
           'ssd_w_out', 'attn_sinks', 'attn_w_out', 'w_mix_out', 'norm_mix_post_w', 'norm_ffn_pre_w', 'ffn_w_up',
           'ffn_conv_w', 'ffn_conv_b', 'ffn_w_down', 'norm_ffn_post_w')
SHARDED = (('w_in', 1, (1024, 2184)), ('ssd_conv_w', 1, (4, 768)), ('ssd_w_out', 0, (512, 1024)),
           ('attn_w_out', 0, (256, 1024)), ('w_mix_out', 0, (256, 1024)), ('ffn_w_up', 1, (1024, 1408)),
           ('ffn_conv_w', 1, (3, 1408)), ('ffn_w_down', 0, (704, 1024)))
MATMUL_WEIGHTS = ('w_in', 'ssd_w_out', 'attn_w_out', 'w_mix_out', 'ffn_w_up', 'ffn_w_down')
REPLICATED = tuple(n for n in WEIGHTS if n not in {s[0] for s in SHARDED})
N_CHIPS = 4
COMM_LANES = 1024


def _cp(vmem_mb=VMEM_LIMIT_MB, **kw):
    return pltpu.CompilerParams(vmem_limit_bytes=vmem_mb << 20, **kw)


class _Ride(NamedTuple):
    ins: tuple
    outs: tuple
    n_sems: int
    start: Callable
    finish: Callable
    middle: Callable = None


def _ride_parts(ride):
    if ride is None:
        return [], [], [], [], []
    hbm = pl.BlockSpec(memory_space=pl.ANY)
    return (list(ride.ins), [hbm] * len(ride.ins), list(ride.outs), [hbm] * len(ride.outs),
            [pltpu.SemaphoreType.DMA((ride.n_sems,)), pltpu.SemaphoreType.DMA((ride.n_sems,))])


def _ride_run(ride, first, last, in_refs, out_refs, sems, middle=None):
    if ride is None:
        return

    @pl.when(first)
    def _():
        ride.start(in_refs, out_refs, *sems)

    if ride.middle is not None:
        @pl.when(last if middle is None else middle)
        def _():
            ride.middle(in_refs, out_refs, *sems)

    @pl.when(last)
    def _():
        ride.finish(in_refs, out_refs, *sems)


def _iota(shape, axis):
    return lax.broadcasted_iota(jnp.int32, shape, axis)


def _sigmoid(v):
    return 1.0 / (1.0 + jnp.exp(-v))


def _mm_call(name, a, b, *, tm, tn, tk, epilogue, outs, extra_in=(), trans_a=False, fill=None, ride=None,
             vmem_mb=VMEM_LIMIT_MB):
    if trans_a:
        kdim, m = a.shape
    else:
        m, kdim = a.shape
    n = b.shape[1]
    assert b.shape[0] == kdim and m % tm == 0 and n % tn == 0 and kdim % tk == 0, (name, a.shape, b.shape, tm, tn, tk)
    gi, gj, gk = m // tm, n // tn, kdim // tk
    n_in, n_out = len(extra_in), len(outs)

    n_fill = 0 if fill is None else 1
    r_ops, r_in_specs, r_outs, r_out_specs, r_scratch = _ride_parts(ride)

    def body(a_ref, b_ref, *rest):
        ins = rest[:n_in]
        rest = rest[n_in + n_fill:]
        r_in, rest = rest[:len(r_ops)], rest[len(r_ops):]
        out_refs, rest = rest[:n_out], rest[n_out:]
        r_out, scratch = rest[:len(r_outs)], rest[len(r_outs):]
        i, j, k = pl.program_id(0), pl.program_id(1), pl.program_id(2)
        _ride_run(ride, (i == 0) & (j == 0) & (k == 0), (i == gi - 1) & (j == gj - 1) & (k == gk - 1),
                  r_in, r_out, scratch[-2:])
        av = a_ref[...].astype(BF16)
        bv = b_ref[...].astype(BF16)
        if trans_a:
            part = lax.dot_general(av, bv, (((0,), (0,)), ((), ())), preferred_element_type=F32)
        else:
            part = jnp.dot(av, bv, preferred_element_type=F32)
        if gk == 1:
            epilogue(part, i, j, ins, out_refs)
        else:
            acc = scratch[0]

            @pl.when(k == 0)
            def _():
                acc[...] = part

            @pl.when(k > 0)
            def _():
                acc[...] += part

            @pl.when(k == gk - 1)
            def _():
                epilogue(acc[...], i, j, ins, out_refs)

    a_spec = pl.BlockSpec((tk, tm), lambda i, j, k: (k, i)) if trans_a else pl.BlockSpec((tm, tk), lambda i, j, k: (i, k))
    if gj == 1 and gk == 1:
        b_spec = pl.BlockSpec((tk, tn), lambda i, j, k: (0, 0), pipeline_mode=pl.Buffered(1))
    else:
        b_spec = pl.BlockSpec((tk, tn), lambda i, j, k: (k, j))
    in_specs = [a_spec, b_spec]
    in_specs += [pl.BlockSpec(bs, im) for _, bs, im in extra_in]
    operands = [a, b] + [e[0] for e in extra_in]
    aliases = {}
    if fill is not None:
        in_specs.append(pl.BlockSpec(memory_space=pl.ANY))
        aliases = {len(operands): fill[1]}
        operands.append(fill[0])
    return pl.pallas_call(
        body, name=name, grid=(gi, gj, gk), in_specs=in_specs + r_in_specs,
        out_specs=[pl.BlockSpec(bs, im) for _, _, bs, im in outs] + r_out_specs,
        out_shape=[SDS(s, d) for s, d, _, _ in outs] + r_outs,
        scratch_shapes=([pltpu.VMEM((tm, tn), F32)] if gk > 1 else []) + r_scratch,
        input_output_aliases=aliases,
        compiler_params=_cp(vmem_mb, dimension_semantics=("arbitrary", "arbitrary", "arbitrary")),
    )(*operands, *r_ops)


def _mm_plain(name, a, b, *, tm, tn, tk, out_dtype=F32, trans_a=False):
    m = a.shape[1] if trans_a else a.shape[0]

    def epilogue(acc, i, j, ins, outs):
        outs[0][...] = acc.astype(out_dtype)

    return _mm_call(name, a, b, tm=tm, tn=tn, tk=tk, epilogue=epilogue, trans_a=trans_a,
                    outs=[((m, b.shape[1]), out_dtype, (tm, tn), lambda i, j, k: (i, j))])[0]


def _accumulate(ref, first, value):
    @pl.when(first)
    def _():
        ref[...] = value

    @pl.when(jnp.logical_not(first))
    def _():
        ref[...] += value


def _rms_bwd(xv, w, dy):
    r = lax.rsqrt(jnp.mean(xv * xv, axis=-1, keepdims=True) + NORM_EPS)
    xn = xv * r
    dxh = dy * w
    dx = r * (dxh - xn * jnp.mean(dxh * xn, axis=-1, keepdims=True))
    return dx, jnp.sum(dy * xn, axis=0, keepdims=True)


def _norm_mm(name, x, wn, w3, *, tm, ride=None):
    t, dm = x.shape
    gj, _, tn = w3.shape
    n = gj * tn
    tm = min(tm, t)
    gi = t // tm
    r_ops, r_in_specs, r_outs, r_out_specs, r_scratch = _ride_parts(ride)

    def body(x_ref, wn_ref, w_ref, *rest):
        r_in, rest = rest[:len(r_ops)], rest[len(r_ops):]
        o_ref, u_ref = rest[:2]
        r_out, sems = rest[2:2 + len(r_outs)], rest[2 + len(r_outs):]
        i, j = pl.program_id(0), pl.program_id(1)
        _ride_run(ride, (i == 0) & (j == 0), (i == gi - 1) & (j == gj - 1), r_in, r_out, sems,
                  middle=(i == (3 * gi) // 4) & (j == 0) if gi > 1 else None)

        @pl.when(j == 0)
        def _():
            xv = x_ref[...]
            r = lax.rsqrt(jnp.mean(xv * xv, axis=-1, keepdims=True) + NORM_EPS)
            u_ref[...] = (xv * r * wn_ref[...]).astype(BF16)

        o_ref[...] = jnp.dot(u_ref[...], w_ref[j], preferred_element_type=F32)

    return pl.pallas_call(
        body, name=name, grid=(gi, gj),
        in_specs=[pl.BlockSpec((tm, dm), lambda i, j: (i, 0)), pl.BlockSpec((1, dm), lambda i, j: (0, 0)),
                  pl.BlockSpec((gj, dm, tn), lambda i, j: (0, 0, 0), pipeline_mode=pl.Buffered(1))] + r_in_specs,
        out_specs=[pl.BlockSpec((tm, tn), lambda i, j: (i, j)), pl.BlockSpec((tm, dm), lambda i, j: (i, 0))] + r_out_specs,
        out_shape=[SDS((t, n), F32), SDS((t, dm), BF16)] + r_outs, scratch_shapes=r_scratch,
        compiler_params=_cp(dimension_semantics=("arbitrary", "arbitrary")),
    )(x, wn, w3, *r_ops)


def _shift_down(tile, halo, s):
    if s == 0:
        return tile
    r = pltpu.roll(tile, s, axis=0)
    h = pltpu.roll(halo, s, axis=0)
    head = jnp.where(_iota(h.shape, 0) < s, h, r[0:8])
    return jnp.concatenate([head, r[8:]], axis=0)


def _shift_up(tile, halo, s):
    if s == 0:
        return tile
    n = tile.shape[0]
    r = pltpu.roll(tile, n - s, axis=0)
    h = pltpu.roll(halo, 8 - s, axis=0)
    tail = jnp.where(_iota(h.shape, 0) >= 8 - s, h, r[n - 8:])
    return jnp.concatenate([r[:n - 8], tail], axis=0)


def _conv_apply(tile, halo, wv, bv, kw):
    acc = bv + wv[kw - 1:kw, :] * tile
    for k in range(kw - 1):
        acc = acc + wv[k:k + 1, :] * _shift_down(tile, halo, kw - 1 - k)
    return acc


def _prev_halo_spec(tm, tc, col0):
    return pl.BlockSpec((8, tc), lambda i, j: (jnp.maximum(i * (tm // 8) - 1, 0), col0 + j))


def _silu_parts(pre):
    sg = _sigmoid(pre)
    return pre * sg, sg * (1.0 + pre * (1.0 - sg))


def _conv_silu_fwd(proj, w, b, *, tm, tc=1536):
    t = proj.shape[0]
    c = w.shape[1]
    tm = min(tm, t)
    col0 = OFF_XBC // tc

    def body(x_ref, h_ref, w_ref, b_ref, o_ref, pre_ref):
        halo = jnp.where(pl.program_id(0) > 0, h_ref[...], 0.0)
        pre = _conv_apply(x_ref[...], halo, w_ref[...], b_ref[...], 4)
        o_ref[...] = _silu_parts(pre)[0]
        pre_ref[...] = pre.astype(BF16)

    tile = pl.BlockSpec((tm, tc), lambda i, j: (i, j))
    return pl.pallas_call(
        body, name="ssd_conv_fwd", grid=(t // tm, c // tc),
        in_specs=[pl.BlockSpec((tm, tc), lambda i, j: (i, col0 + j)), _prev_halo_spec(tm, tc, col0),
                  pl.BlockSpec((4, tc), lambda i, j: (0, j)), pl.BlockSpec((1, tc), lambda i, j: (0, j))],
        out_specs=[tile, tile], out_shape=[SDS((t, c), F32), SDS((t, c), BF16)],
        compiler_params=_cp(dimension_semantics=("arbitrary", "arbitrary")),
    )(proj, proj, w, b)


def _conv_silu_bwd1(d_out, pre, *, tm, tc=1536):
    t, c = pre.shape
    tm = min(tm, t)

    def body(g_ref, p_ref, o_ref, db_ref):
        i = pl.program_id(1)
        d_pre = g_ref[...] * _silu_parts(p_ref[...].astype(F32))[1]
        o_ref[...] = d_pre.astype(BF16)
        _accumulate(db_ref, i == 0, jnp.sum(d_pre, axis=0, keepdims=True))

    tile = pl.BlockSpec((tm, tc), lambda j, i: (i, j))
    return pl.pallas_call(
        body, name="ssd_conv_bwd1", grid=(c // tc, t // tm), in_specs=[tile, tile],
        out_specs=[tile, pl.BlockSpec((1, tc), lambda j, i: (0, j))],
        out_shape=[SDS((t, c), BF16), SDS((1, c), F32)],
        compiler_params=_cp(dimension_semantics=("arbitrary", "arbitrary")),
    )(d_out, pre)


def _conv_bwd2(name, d_pre, src, src_col0, w, *, tm, tc, out_cols, out_col0, fill=None):
    t, c = d_pre.shape
    kw = w.shape[0]
    tm = min(tm, t)
    ni = t // tm
    col0 = src_col0 // tc
    ocol0 = out_col0 // tc

    def body(g_ref, gn_ref, x_ref, w_ref, *rest):
        o_ref, dw_ref = rest[-2:]
        i = pl.program_id(1)
        g = g_ref[...].astype(F32)
        g_next = jnp.where(i < ni - 1, gn_ref[...].astype(F32)[0:8], 0.0)
        xv = x_ref[...]
        wv = w_ref[...]
        shifted = [_shift_up(g, g_next, kw - 1 - k) for k in range(kw)]
        d_in = wv[0:1, :] * shifted[0]
        for k in range(1, kw):
            d_in = d_in + wv[k:k + 1, :] * shifted[k]
        o_ref[...] = d_in.astype(o_ref.dtype)
        rows = [jnp.sum(shifted[k] * xv, axis=0, keepdims=True) for k in range(kw)]

        @pl.when(i == 0)
        def _():
            for k in range(kw):
                dw_ref[k:k + 1, :] = rows[k]

        @pl.when(i > 0)
        def _():
            for k in range(kw):
                dw_ref[k:k + 1, :] += rows[k]

    in_specs = [pl.BlockSpec((tm, tc), lambda j, i: (i, j)),
                pl.BlockSpec((16, tc), lambda j, i: (jnp.minimum((i + 1) * (tm // 16), t // 16 - 1), j)),
                pl.BlockSpec((tm, tc), lambda j, i: (i, col0 + j)),
                pl.BlockSpec((kw, tc), lambda j, i: (0, j))]
    operands = [d_pre, d_pre, src, w]
    if fill is not None:
        in_specs.append(pl.BlockSpec(memory_space=pl.ANY))
        operands.append(fill)
    return pl.pallas_call(
        body, name=name, grid=(c // tc, ni), in_specs=in_specs,
        out_specs=[pl.BlockSpec((tm, tc), lambda j, i: (i, ocol0 + j)), pl.BlockSpec((kw, tc), lambda j, i: (0, j))],
        out_shape=[SDS((t, out_cols), BF16), SDS((kw, c), F32)],
        input_output_aliases={} if fill is None else {4: 0},
        compiler_params=_cp(dimension_semantics=("arbitrary", "arbitrary")),
    )(*operands)


GELU_C = 0.7978845608028654


def _gelu_parts(v):
    inner = GELU_C * (v + 0.044715 * v * v * v)
    th = jnp.tanh(inner)
    val = 0.5 * v * (1.0 + th)
    grad = 0.5 * (1.0 + th) + 0.5 * v * (1.0 - th * th) * GELU_C * (1.0 + 3.0 * 0.044715 * v * v)
    return val, grad


def _ffn_act_fwd(up_raw, w, b, *, tm, tc=1408):
    t = up_raw.shape[0]
    tm = min(tm, t)
    nj = FFN_D_FF // tc
    halo = lambda i: jnp.maximum(i * (tm // 8) - 1, 0)

    def body(g_ref, gh_ref, v_ref, vh_ref, wg_ref, wv_ref, bg_ref, bv_ref, o_ref, gate_ref, val_ref):
        first = pl.program_id(0) > 0
        gate = _conv_apply(g_ref[...], jnp.where(first, gh_ref[...], 0.0), wg_ref[...], bg_ref[...], 3)
        val = _conv_apply(v_ref[...], jnp.where(first, vh_ref[...], 0.0), wv_ref[...], bv_ref[...], 3)
        o_ref[...] = (_gelu_parts(gate)[0] * val).astype(BF16)
        gate_ref[...] = gate.astype(BF16)
        val_ref[...] = val.astype(BF16)

    tile = pl.BlockSpec((tm, tc), lambda i, j: (i, j))
    return pl.pallas_call(
        body, name="ffn_act_fwd", grid=(t // tm, nj),
        in_specs=[tile, pl.BlockSpec((8, tc), lambda i, j: (halo(i), j)),
                  pl.BlockSpec((tm, tc), lambda i, j: (i, nj + j)), pl.BlockSpec((8, tc), lambda i, j: (halo(i), nj + j)),
                  pl.BlockSpec((3, tc), lambda i, j: (0, j)), pl.BlockSpec((3, tc), lambda i, j: (0, nj + j)),
                  pl.BlockSpec((1, tc), lambda i, j: (0, j)), pl.BlockSpec((1, tc), lambda i, j: (0, nj + j))],
        out_specs=[tile] * 3, out_shape=[SDS((t, FFN_D_FF), BF16)] * 3,
        compiler_params=_cp(dimension_semantics=("arbitrary", "arbitrary")),
    )(up_raw, up_raw, up_raw, up_raw, w, w, b, b)


def _ffn_act_bwd(gate, val, d_act, *, tm, tc=1408):
    t = gate.shape[0]
    tm = min(tm, t)
    nj = FFN_D_FF // tc

    def body(g_ref, v_ref, da_ref, dg_ref, dv_ref, dbg_ref, dbv_ref):
        i = pl.program_id(1)
        val = v_ref[...].astype(F32)
        ge, dge = _gelu_parts(g_ref[...].astype(F32))
        da = da_ref[...].astype(F32)
        d_gate = da * val * dge
        d_val = da * ge
        dg_ref[...] = d_gate.astype(BF16)
        dv_ref[...] = d_val.astype(BF16)
        _accumulate(dbg_ref, i == 0, jnp.sum(d_gate, axis=0, keepdims=True))
        _accumulate(dbv_ref, i == 0, jnp.sum(d_val, axis=0, keepdims=True))

    tile = pl.BlockSpec((tm, tc), lambda j, i: (i, j))
    row = pl.BlockSpec((1, tc), lambda j, i: (0, j))
    return pl.pallas_call(
        body, name="ffn_act_bwd", grid=(nj, t // tm), in_specs=[tile] * 3, out_specs=[tile, tile, row, row],
        out_shape=[SDS((t, FFN_D_FF), BF16), SDS((t, FFN_D_FF), BF16), SDS((1, FFN_D_FF), F32), SDS((1, FFN_D_FF), F32)],
        compiler_params=_cp(dimension_semantics=("arbitrary", "arbitrary")),
    )(gate, val, d_act)


def _softplus(v):
    e = jnp.exp(-jnp.abs(v))
    small = e * (1.0 - 0.5 * e)
    return jnp.maximum(v, 0.0) + jnp.where(e < 1e-4, small, jnp.log(1.0 + e))


def _dt_fwd(proj, bias_pad, *, tm):
    t = proj.shape[0]
    tm = min(tm, t)

    def body(x_ref, b_ref, g_ref, gt_ref):
        dt = _softplus(x_ref[...] + b_ref[...])
        first8 = _iota((tm, 128), 1) < 8
        for g in range(SSD_N_GROUPS):
            dg = jnp.where(first8, dt if g == 0 else pltpu.roll(dt, 128 - 8 * g, axis=1), 0.0)
            g_ref[g] = dg
            gt_ref[g] = dg.T[0:8, :]

    return pl.pallas_call(
        body, name="dt_fwd", grid=(t // tm,),
        in_specs=[pl.BlockSpec((tm, 128), lambda i: (i, OFF_DT // 128)), pl.BlockSpec((1, 128), lambda i: (0, 0))],
        out_specs=[pl.BlockSpec((SSD_N_GROUPS, tm, 128), lambda i: (0, i, 0)), pl.BlockSpec((SSD_N_GROUPS, 8, tm), lambda i: (0, 0, i))],
        out_shape=[SDS((SSD_N_GROUPS, t, 128), F32), SDS((SSD_N_GROUPS, 8, t), F32)],
        compiler_params=_cp(dimension_semantics=("arbitrary",)),
    )(proj, bias_pad)


def _dt_bwd(d_dtg, proj, bias_pad, d_proj, *, tm):
    t = proj.shape[0]
    tm = min(tm, t)

    def body(g_ref, x_ref, b_ref, _, o_ref, db_ref):
        first8 = _iota((tm, 128), 1) < 8
        d_dt = jnp.where(first8, g_ref[0], 0.0)
        for g in range(1, SSD_N_GROUPS):
            d_dt = d_dt + pltpu.roll(jnp.where(first8, g_ref[g], 0.0), 8 * g, axis=1)
        d_raw = d_dt * _sigmoid(x_ref[...] + b_ref[...])
        o_ref[:, 0:128] = d_raw.astype(BF16)
        o_ref[:, 128:512] = jnp.zeros((tm, 384), BF16)
        _accumulate(db_ref, pl.program_id(0) == 0, jnp.sum(d_raw, axis=0, keepdims=True))

    return pl.pallas_call(
        body, name="dt_bwd", grid=(t // tm,),
        in_specs=[pl.BlockSpec((SSD_N_GROUPS, tm, 128), lambda i: (0, i, 0)), pl.BlockSpec((tm, 128), lambda i: (i, OFF_DT // 128)),
                  pl.BlockSpec((1, 128), lambda i: (0, 0)), pl.BlockSpec(memory_space=pl.ANY)],
        out_specs=[pl.BlockSpec((tm, 512), lambda i: (i, OFF_DT // 512)), pl.BlockSpec((1, 128), lambda i: (0, 0))],
        out_shape=[SDS((t, PROJ_W), BF16), SDS((1, 128), F32)],
        input_output_aliases={3: 0},
        compiler_params=_cp(dimension_semantics=("arbitrary",)),
    )(d_dtg, proj, bias_pad, d_proj)


def _split3(v):
    hi = v.astype(BF16)
    r1 = v - hi.astype(F32)
    mid = r1.astype(BF16)
    return hi, mid, (r1 - mid.astype(F32)).astype(BF16)


def _times01(v, m3):
    return jnp.dot(jnp.concatenate(_split3(v), axis=1), m3, preferred_element_type=F32)


def _01times(m3, v):
    return jnp.dot(m3, jnp.concatenate(_split3(v), axis=0), preferred_element_type=F32)


def _ssd_decay(dt_ref, dtT_ref, al_ref, alT_ref, k):
    dt = dt_ref[0]
    a_row = -jnp.exp(al_ref[0])
    adt_t = dtT_ref[0] * (-jnp.exp(alT_ref[0]))
    return dt, a_row, _01times(k['low3'][...], dt * a_row), _times01(adt_t, k['up3v'][...])


def _ssd_specs(nc, rev):
    ci = (lambda c: nc - 1 - c) if rev else (lambda c: c)
    return [pl.BlockSpec((CHUNK, SSD_CONV_DIM), lambda c: (ci(c), 0)),
            pl.BlockSpec((SSD_N_GROUPS, CHUNK, 128), lambda c: (0, ci(c), 0)),
            pl.BlockSpec((SSD_N_GROUPS, 8, CHUNK), lambda c: (0, 0, ci(c))),
            pl.BlockSpec((SSD_N_GROUPS, 1, 128), lambda c: (0, 0, 0)),
            pl.BlockSpec((SSD_N_GROUPS, 8, 1), lambda c: (0, 0, 0)),
            pl.BlockSpec((1, SSD_D_INNER), lambda c: (0, 0))]


def _ssd_group_views(g, x_ref, dt_ref, dtT_ref, al_ref, alT_ref, d_ref):
    return (x_ref.at[:, g * GROUP_W:(g + 1) * GROUP_W], dt_ref.at[g:g + 1], dtT_ref.at[g:g + 1], al_ref.at[g:g + 1],
            alT_ref.at[g:g + 1], d_ref.at[:, g * 512:(g + 1) * 512])


NT = (((1,), (1,)), ((), ()))
WIDE = 8 * CHUNK
SSD_CONST_NAMES = ('e128', 'e64', 's64', 'mlo', 'mup', 'low3', 'up3', 'up3v')
SSD_CONST_SHAPES = [pltpu.VMEM((3 * CHUNK, WIDE), BF16), pltpu.VMEM((3 * CHUNK, 512), BF16), pltpu.VMEM((512, CHUNK), BF16),
                    pltpu.VMEM((CHUNK, WIDE), F32), pltpu.VMEM((CHUNK, WIDE), F32), pltpu.VMEM((CHUNK, 3 * CHUNK), BF16),
                    pltpu.VMEM((CHUNK, 3 * CHUNK), BF16), pltpu.VMEM((3 * CHUNK, CHUNK), BF16)]


def _ssd_init_consts(k):
    row, col = _iota((3 * CHUNK, WIDE), 0), _iota((3 * CHUNK, WIDE), 1)
    k['e128'][...] = ((col >> 7) == (row & 127)).astype(BF16)
    k['e64'][...] = ((_iota((3 * CHUNK, 512), 1) >> 6) == (_iota((3 * CHUNK, 512), 0) & 127)).astype(BF16)
    k['s64'][...] = ((_iota((512, CHUNK), 0) >> 6) == _iota((512, CHUNK), 1)).astype(BF16)
    row, col = _iota((CHUNK, WIDE), 0), _iota((CHUNK, WIDE), 1)
    k['mlo'][...] = (row >= (col & 127)).astype(F32)
    k['mup'][...] = (row <= (col & 127)).astype(F32)
    row, col = _iota((CHUNK, 3 * CHUNK), 0), _iota((CHUNK, 3 * CHUNK), 1) & 127
    k['low3'][...] = (row >= col).astype(BF16)
    k['up3'][...] = (row <= col).astype(BF16)
    row, col = _iota((3 * CHUNK, CHUNK), 0) & 127, _iota((3 * CHUNK, CHUNK), 1)
    k['up3v'][...] = (row <= col).astype(BF16)


def _ssd_common(x_ref, dt_ref, dtT_ref, al_ref, alT_ref, k):
    dt, a_row, acs, acs_t = _ssd_decay(dt_ref, dtT_ref, al_ref, alT_ref, k)
    ecol = _times01(acs, k['e128'][...])
    rrow = jnp.concatenate([jnp.broadcast_to(acs_t[j:j + 1, :], (CHUNK, CHUNK)) for j in range(8)], axis=1)
    a64 = _times01(acs, k['e64'][...])
    dt64 = _times01(dt, k['e64'][...])
    a_end64 = a64[CHUNK - 1:CHUNK, :]
    xs = x_ref[:, 0:512]
    return dict(dt=dt, a_row=a_row, acs=acs, seg=ecol - rrow, dt64=dt64, e_a=jnp.exp(a64), decay=jnp.exp(a_end64 - a64),
                e_end64=jnp.exp(a_end64), xs=xs, xdt=xs * dt64, bm=x_ref[:, 512:640], cm=x_ref[:, 640:768])


def _pair_blocks(v):
    lo = _iota((CHUNK, 128), 1) < 64
    out = []
    for i in range(4):
        ch = v[:, i * 128:(i + 1) * 128]
        out.append(jnp.concatenate([jnp.where(lo, ch, 0.0), jnp.where(lo, 0.0, ch)], axis=0).astype(BF16))
    return out


def _tile8(m):
    return jnp.concatenate([m] * 8, axis=1)


def _ssd_fwd(xc, dtg, dtg_t, alog, alog_t, d_exp):
    t = xc.shape[0]
    nc = t // CHUNK

    def body(xa_ref, dta_ref, dtTa_ref, ala_ref, alTa_ref, da_ref, ya_ref, hs_ref, h_scr, *consts):
        c = pl.program_id(0)
        k = dict(zip(SSD_CONST_NAMES, consts))

        @pl.when(c == 0)
        def _():
            _ssd_init_consts(k)
            h_scr[...] = jnp.zeros_like(h_scr)

        for g in range(SSD_N_GROUPS):
            x_ref, dt_ref, dtT_ref, al_ref, alT_ref, d_ref = _ssd_group_views(g, xa_ref, dta_ref, dtTa_ref, ala_ref, alTa_ref, da_ref)
            v = _ssd_common(x_ref, dt_ref, dtT_ref, al_ref, alT_ref, k)
            b16, c16 = v['bm'].astype(BF16), v['cm'].astype(BF16)
            cb = lax.dot_general(c16, b16, NT, preferred_element_type=F32)
            m16 = (jnp.exp(jnp.minimum(v['seg'], 0.0)) * k['mlo'][...] * _tile8(cb)).astype(BF16)
            xbd = _pair_blocks(v['xdt'])
            y_diag = jnp.concatenate([jnp.dot(m16[:, i * 256:(i + 1) * 256], xbd[i], preferred_element_type=F32)
                                      for i in range(4)], axis=1)
            ht = h_scr[g]
            y_off = jnp.dot(c16, ht.astype(BF16), preferred_element_type=F32)
            ya_ref[:, g * 512:(g + 1) * 512] = y_diag + v['e_a'] * y_off + d_ref[...] * v['xs']
            st = jnp.dot(v['bm'].T.astype(BF16), (v['xdt'] * v['decay']).astype(BF16), preferred_element_type=F32)
            hs_ref[0, g] = ht
            h_scr[g] = ht * v['e_end64'] + st

    return pl.pallas_call(
        body, name="ssd_fwd", grid=(nc,), in_specs=_ssd_specs(nc, False),
        out_specs=[pl.BlockSpec((CHUNK, SSD_D_INNER), lambda c: (c, 0)),
                   pl.BlockSpec((1, SSD_N_GROUPS, SSD_D_STATE, 512), lambda c: (c, 0, 0, 0))],
        out_shape=[SDS((t, SSD_D_INNER), F32), SDS((nc, SSD_N_GROUPS, SSD_D_STATE, 512), F32)],
        scratch_shapes=[pltpu.VMEM((SSD_N_GROUPS, SSD_D_STATE, 512), F32)] + SSD_CONST_SHAPES,
        compiler_params=_cp(dimension_semantics=("arbitrary",)),
    )(xc, dtg, dtg_t, alog, alog_t, d_exp)


def _ssd_bwd(xc, dtg, dtg_t, alog, alog_t, d_exp, d_y, hs, ride=None):
    t = xc.shape[0]
    nc = t // CHUNK

    r_ops, r_in_specs, r_outs, r_out_specs, r_scratch = _ride_parts(ride)

    def body(xa_ref, dta_ref, dtTa_ref, ala_ref, alTa_ref, da_ref, dya_ref, hs_ref, *rest):
        r_in, rest = rest[:len(r_ops)], rest[len(r_ops):]
        dxa_ref, ddta_ref, dal_ref, dd_ref = rest[:4]
        r_out, rest = rest[4:4 + len(r_outs)], rest[4 + len(r_outs):]
        g_scr, consts, sems = rest[0], rest[1:1 + len(SSD_CONST_NAMES)], rest[1 + len(SSD_CONST_NAMES):]
        c = pl.program_id(0)
        _ride_run(ride, c == 0, c == nc - 1, r_in, r_out, sems)
        k = dict(zip(SSD_CONST_NAMES, consts))

        @pl.when(c == 0)
        def _():
            _ssd_init_consts(k)
            g_scr[...] = jnp.zeros_like(g_scr)

        for g in range(SSD_N_GROUPS):
            views = _ssd_group_views(g, xa_ref, dta_ref, dtTa_ref, ala_ref, alTa_ref, da_ref)
            one_group(c, g, k, *views, dya_ref.at[:, g * 512:(g + 1) * 512], hs_ref, g_scr,
                      dxa_ref.at[:, g * GROUP_W:(g + 1) * GROUP_W], ddta_ref.at[g:g + 1], dal_ref, dd_ref)

    def one_group(c, g, k, x_ref, dt_ref, dtT_ref, al_ref, alT_ref, d_ref, dy_ref, hs_ref, g_scr, dx_ref, ddt_ref,
                  dal_ref, dd_ref):
        s64, mlo, mup = k['s64'], k['mlo'], k['mup']
        v = _ssd_common(x_ref, dt_ref, dtT_ref, al_ref, alT_ref, k)
        dt, a_row, xs, xdt, e_a, decay = v['dt'], v['a_row'], v['xs'], v['xdt'], v['e_a'], v['decay']
        row, col = _iota((CHUNK, CHUNK), 0), _iota((CHUNK, CHUNK), 1)
        b16, c16 = v['bm'].astype(BF16), v['cm'].astype(BF16)
        ct16 = v['cm'].T.astype(BF16)
        cb = lax.dot_general(c16, b16, NT, preferred_element_type=F32)
        cbt = lax.dot_general(b16, c16, NT, preferred_element_type=F32)
        lmat = jnp.exp(jnp.minimum(v['seg'], 0.0)) * mlo[...]
        lmat_t = jnp.exp(jnp.minimum(-v['seg'], 0.0)) * mup[...]
        mmat, mmat_t = lmat * _tile8(cb), lmat_t * _tile8(cbt)
        mt16 = mmat_t.astype(BF16)
        dy = dy_ref[...]
        dye, xdec = dy * e_a, xdt * decay
        dy16, dye16, xdec16 = dy.astype(BF16), dye.astype(BF16), xdec.astype(BF16)
        xdt16 = xdt.astype(BF16)
        ht, gt = hs_ref[0, g], g_scr[g]
        ht16, gt16 = ht.astype(BF16), gt.astype(BF16)
        xbd, dybd = _pair_blocks(xdt), _pair_blocks(dy)
        d_m, d_mt, d_x = [], [], []
        for i in range(4):
            csl = slice(i * 128, (i + 1) * 128)
            d_m.append(lax.dot_general(dy16[:, csl], xbd[i], NT, preferred_element_type=F32))
            d_mt.append(lax.dot_general(xdt16[:, csl], dybd[i], NT, preferred_element_type=F32))
            d_x.append(jnp.dot(mt16[:, i * 256:(i + 1) * 256], dybd[i], preferred_element_type=F32))
        d_m, d_mt, d_x = jnp.concatenate(d_m, axis=1), jnp.concatenate(d_mt, axis=1), jnp.concatenate(d_x, axis=1)

        def head_sum(m):
            acc = m[:, 0:CHUNK]
            for j in range(1, 8):
                acc = acc + m[:, j * CHUNK:(j + 1) * CHUNK]
            return acc

        def seg64(p):
            return jnp.dot(p.astype(BF16), s64[...], preferred_element_type=F32)

        d_cb16 = head_sum(d_m * lmat).astype(BF16)
        d_cbt16 = head_sum(d_mt * lmat_t).astype(BF16)
        dseg = d_m * mmat - d_mt * mmat_t
        da_seg = jnp.zeros((CHUNK, CHUNK), F32)
        for j in range(8):
            da_seg = jnp.where(col == j, jnp.sum(dseg[:, j * CHUNK:(j + 1) * CHUNK], axis=1, keepdims=True), da_seg)
        ch = jnp.dot(c16, ht16, preferred_element_type=F32)
        bg = jnp.dot(b16, gt16, preferred_element_type=F32)
        d_x = d_x + decay * bg
        d_decay = seg64(xdec * bg)
        e_end = jnp.exp(v['acs'][CHUNK - 1:CHUNK, :])
        d_end = e_end * jnp.sum(seg64(gt * ht), axis=0, keepdims=True) + jnp.sum(d_decay, axis=0, keepdims=True)
        d_a = seg64(dye * ch) - d_decay + da_seg + jnp.where(row == CHUNK - 1, d_end, 0.0)
        dx_ref[:, 0:512] = d_x * v['dt64'] + d_ref[...] * dy
        dx_ref[:, 640:768] = (lax.dot_general(dye16, ht16, NT, preferred_element_type=F32)
                              + jnp.dot(d_cb16, b16, preferred_element_type=F32))
        dx_ref[:, 512:640] = (lax.dot_general(xdec16, gt16, NT, preferred_element_type=F32)
                              + jnp.dot(d_cbt16, c16, preferred_element_type=F32))
        g_scr[g] = gt * v['e_end64'] + jnp.dot(ct16, dye16, preferred_element_type=F32)
        d_adt = _01times(k['up3'][...], d_a)
        ddt_ref[0] = d_adt * a_row + seg64(d_x * xs)
        d_alog = jnp.sum(d_adt * dt, axis=0, keepdims=True) * a_row
        dd_row = jnp.sum(seg64(dy * xs), axis=0, keepdims=True)
        first = c == 0

        @pl.when(first)
        def _():
            dal_ref[g] = d_alog
            dd_ref[g] = dd_row

        @pl.when(jnp.logical_not(first))
        def _():
            dal_ref[g] += d_alog
            dd_ref[g] += dd_row

    rc = lambda c: nc - 1 - c
    whole = pl.BlockSpec((SSD_N_GROUPS, 1, 128), lambda c: (0, 0, 0))
    return pl.pallas_call(
        body, name="ssd_bwd", grid=(nc,),
        in_specs=_ssd_specs(nc, True) + [pl.BlockSpec((CHUNK, SSD_D_INNER), lambda c: (rc(c), 0)),
                                        pl.BlockSpec((1, SSD_N_GROUPS, SSD_D_STATE, 512), lambda c: (rc(c), 0, 0, 0))] + r_in_specs,
        out_specs=[pl.BlockSpec((CHUNK, SSD_CONV_DIM), lambda c: (rc(c), 0)),
                   pl.BlockSpec((SSD_N_GROUPS, CHUNK, 128), lambda c: (0, rc(c), 0)), whole, whole] + r_out_specs,
        out_shape=[SDS((t, SSD_CONV_DIM), F32), SDS((SSD_N_GROUPS, t, 128), F32),
                   SDS((SSD_N_GROUPS, 1, 128), F32), SDS((SSD_N_GROUPS, 1, 128), F32)] + r_outs,
        scratch_shapes=[pltpu.VMEM((SSD_N_GROUPS, SSD_D_STATE, 512), F32)] + SSD_CONST_SHAPES + r_scratch,
        compiler_params=_cp(dimension_semantics=("arbitrary",)),
    )(xc, dtg, dtg_t, alog, alog_t, d_exp, d_y, hs, *r_ops)


def _gated_norm_fwd(y, proj, w, *, tm):
    t = y.shape[0]
    tm = min(tm, t)

    def body(y_ref, z_ref, w_ref, o_ref):
        gv = y_ref[...] * _silu_parts(z_ref[...])[0]
        r = lax.rsqrt(jnp.mean(gv * gv, axis=-1, keepdims=True) + NORM_EPS)
        o_ref[...] = (gv * r * w_ref[...]).astype(BF16)

    tile = pl.BlockSpec((tm, 512), lambda i, g: (i, g))
    return pl.pallas_call(
        body, name="gated_norm_fwd", grid=(t // tm, SSD_N_GROUPS),
        in_specs=[tile, pl.BlockSpec((tm, 512), lambda i, g: (i, OFF_Z // 512 + g)),
                  pl.BlockSpec((1, 512), lambda i, g: (0, g))], out_specs=tile,
        out_shape=SDS((t, SSD_D_INNER), BF16),
        compiler_params=_cp(dimension_semantics=("arbitrary", "arbitrary")),
    )(y, proj, w)


def _rope(ch, cos_t, sin_t):
    first = (_iota(ch.shape, 1) & 32) == 0
    partner = jnp.where(first, pltpu.roll(ch, 96, axis=1), pltpu.roll(ch, 32, axis=1))
    return ch * cos_t + partner * sin_t


def _rope_qkv(proj, cos_t, sin_t, *, tm):
    t = proj.shape[0]
    tm = min(tm, t)

    def body(q_ref, k_ref, v_ref, c_ref, s_ref, qr_ref, kp_ref, vp_ref, kt_ref, vt_ref):
        cv, sv = c_ref[...], s_ref[...]
        lo = _iota((tm, 128), 1) < 64
        for m in range(8):
            sl = slice(m * 128, (m + 1) * 128)
            qr_ref[:, sl] = (_rope(q_ref[:, sl], cv, sv) * 0.125).astype(BF16)
        for m2 in range(2):
            sl = slice(m2 * 128, (m2 + 1) * 128)
            for src, dst, dst_t in ((_rope(k_ref[:, sl], cv, sv), kp_ref, kt_ref), (v_ref[:, sl], vp_ref, vt_ref)):
                sw = pltpu.roll(src, 64, axis=1)
                padded = (jnp.where(lo, src, 0.0), jnp.where(lo, 0.0, sw), jnp.where(lo, sw, 0.0), jnp.where(lo, 0.0, src))
                for i, pad in enumerate(padded):
                    rows = slice((4 * m2 + i) * 128, (4 * m2 + i + 1) * 128)
                    dst[:, rows] = pad.astype(BF16)
                    dst_t[rows, :] = pad.T.astype(BF16)

    return pl.pallas_call(
        body, name="rope_qkv", grid=(t // tm,),
        in_specs=[pl.BlockSpec((tm, 1024), lambda i: (i, OFF_Q // 1024)), pl.BlockSpec((tm, 256), lambda i: (i, OFF_K // 256)),
                  pl.BlockSpec((tm, 256), lambda i: (i, OFF_V // 256)), pl.BlockSpec((tm, 128), lambda i: (i, 0)),
                  pl.BlockSpec((tm, 128), lambda i: (i, 0))],
        out_specs=[pl.BlockSpec((tm, 1024), lambda i: (i, 0))] * 3 + [pl.BlockSpec((1024, tm), lambda i: (0, i))] * 2,
        out_shape=[SDS((t, 1024), BF16)] * 3 + [SDS((1024, t), BF16)] * 2,
        compiler_params=_cp(dimension_semantics=("arbitrary",)),
    )(proj, proj, proj, cos_t, sin_t)


def _attn_valid(n):
    kj, qi = _iota((2 * CHUNK, CHUNK), 0), _iota((2 * CHUNK, CHUNK), 1)
    return (kj > qi) & (kj <= qi + CHUNK) & ((n > 0) | (kj >= CHUNK))


def _attn_fwd(qr, kp, vt, sinks):
    t = qr.shape[0]
    nb = t // CHUNK

    def body(q_ref, kc_ref, kprev_ref, vc_ref, vprev_ref, sk_ref, o_ref, lse_ref):
        n = pl.program_id(0)
        valid = _attn_valid(n)
        head_row = _iota((16, CHUNK), 0)
        lse_all = jnp.zeros((16, CHUNK), F32)
        for m in range(8):
            g = m // 2
            qch = q_ref[:, m * 128:(m + 1) * 128]
            sls = [slice((2 * g + e) * 128, (2 * g + e + 1) * 128) for e in range(2)]
            kk2 = jnp.concatenate([r[:, sl] for sl in sls for r in (kprev_ref, kc_ref)], axis=0)
            vv2_t = jnp.concatenate([r[sl, :] for sl in sls for r in (vprev_ref, vc_ref)], axis=1)
            s2 = lax.dot_general(kk2, qch, NT, preferred_element_type=F32)
            probs = []
            for e in range(2):
                h = 2 * m + e
                s = jnp.where(valid, s2[2 * CHUNK * e:2 * CHUNK * (e + 1)], NEG)
                sink = sk_ref[0:1, h:h + 1]
                mx = jnp.maximum(jnp.max(s, axis=0, keepdims=True), sink)
                p = jnp.exp(s - mx)
                den = jnp.sum(p, axis=0, keepdims=True) + jnp.exp(sink - mx)
                probs.append((p * (1.0 / den)).astype(BF16))
                lse_all = jnp.where(head_row == h, mx + jnp.log(den), lse_all)
            o_t = jnp.dot(vv2_t, jnp.concatenate(probs, axis=0), preferred_element_type=F32)
            o_ref[:, m * 128:(m + 1) * 128] = o_t.T.astype(BF16)
        lse_ref[0] = lse_all

    cur = pl.BlockSpec((CHUNK, 1024), lambda n: (n, 0))
    prev = pl.BlockSpec((CHUNK, 1024), lambda n: (jnp.maximum(n - 1, 0), 0))
    cur_t = pl.BlockSpec((1024, CHUNK), lambda n: (0, n))
    prev_t = pl.BlockSpec((1024, CHUNK), lambda n: (0, jnp.maximum(n - 1, 0)))
    return pl.pallas_call(
        body, name="attn_fwd", grid=(nb,),
        in_specs=[cur, cur, prev, cur_t, prev_t, pl.BlockSpec((1, 128), lambda n: (0, 0))],
        out_specs=[cur, pl.BlockSpec((1, 16, CHUNK), lambda n: (n, 0, 0))],
        out_shape=[SDS((t, 1024), BF16), SDS((nb, 16, CHUNK), F32)],
        compiler_params=_cp(dimension_semantics=("arbitrary",)),
    )(qr, kp, kp, vt, vt, sinks)


def _attn_bwd(qr, kp, vp, kt, d_o, o, lse, sinks, cos_t, sin_t, d_proj, ride=None):
    t = qr.shape[0]
    nb = t // CHUNK

    r_ops, r_in_specs, r_outs, r_out_specs, r_scratch = _ride_parts(ride)

    def body(q_ref, kc_ref, kprev_ref, vc_ref, vprev_ref, ktc_ref, ktprev_ref, do_ref, o_ref, lse_ref, sk_ref,
             c_ref, s_ref, cp_ref, sp_ref, _, *rest):
        r_in, rest = rest[:len(r_ops)], rest[len(r_ops):]
        dqkv_ref, dsk_ref = rest[:2]
        r_out, rest = rest[2:2 + len(r_outs)], rest[2 + len(r_outs):]
        acc_k, acc_v, dq_scr = rest[:3]
        n = pl.program_id(0)
        _ride_run(ride, n == 0, n == nb, r_in, r_out, rest[3:])
        lane = _iota((CHUNK, 128), 1)
        lo = lane < 64
        lane1 = _iota((1, 128), 1)

        @pl.when(n == 0)
        def _():
            acc_k[...] = jnp.zeros_like(acc_k)
            acc_v[...] = jnp.zeros_like(acc_v)
            dsk_ref[...] = jnp.zeros((1, 128), F32)

        @pl.when(n > 0)
        def _():
            dqkv_ref[:, 0:1024] = dq_scr[...]
            for r in range(8):
                acc_k[r, 0:CHUNK] = acc_k[r, CHUNK:2 * CHUNK]
                acc_v[r, 0:CHUNK] = acc_v[r, CHUNK:2 * CHUNK]
                acc_k[r, CHUNK:2 * CHUNK] = jnp.zeros((CHUNK, 128), F32)
                acc_v[r, CHUNK:2 * CHUNK] = jnp.zeros((CHUNK, 128), F32)

        @pl.when(n < nb)
        def _():
            valid = _attn_valid(n)
            lse_all = lse_ref[0]
            dsk = jnp.zeros((1, 128), F32)
            for m in range(8):
                g = m // 2
                csl = slice(m * 128, (m + 1) * 128)
                qch = q_ref[:, csl]
                doch = do_ref[:, csl]
                prod_t = (doch.astype(F32) * o_ref[:, csl].astype(F32)).T
                sls = [slice((2 * g + e) * 128, (2 * g + e + 1) * 128) for e in range(2)]
                kk2 = jnp.concatenate([r[:, sl] for sl in sls for r in (kprev_ref, kc_ref)], axis=0)
                vv2 = jnp.concatenate([r[:, sl] for sl in sls for r in (vprev_ref, vc_ref)], axis=0)
                kk2_t = jnp.concatenate([r[sl, :] for sl in sls for r in (ktprev_ref, ktc_ref)], axis=1)
                s2 = lax.dot_general(kk2, qch, NT, preferred_element_type=F32)
                d_p2 = lax.dot_general(vv2, doch, NT, preferred_element_type=F32)
                ps, d_ss = [], []
                for e in range(2):
                    h = 2 * m + e
                    rows = slice(2 * CHUNK * e, 2 * CHUNK * (e + 1))
                    lse_h = lse_all[h:h + 1, :]
                    p = jnp.exp(jnp.where(valid, s2[rows], NEG) - lse_h)
                    delta = jnp.sum(prod_t[64 * e:64 * (e + 1)], axis=0, keepdims=True)
                    ps.append(p.astype(BF16))
                    d_ss.append((p * (d_p2[rows] - delta)).astype(BF16))
                    p_sink = jnp.exp(sk_ref[0:1, h:h + 1] - lse_h)
                    dsk = jnp.where(lane1 == h, -jnp.sum(p_sink * delta), dsk)
                d_s2, p2 = jnp.concatenate(d_ss, axis=0), jnp.concatenate(ps, axis=0)
                d_k2 = jnp.dot(d_s2, qch, preferred_element_type=F32)
                d_v2 = jnp.dot(p2, doch, preferred_element_type=F32)
                for e in range(2):
                    rows = slice(2 * CHUNK * e, 2 * CHUNK * (e + 1))
                    acc_k[2 * g + e] += d_k2[rows]
                    acc_v[2 * g + e] += d_v2[rows]
                dq_t = jnp.dot(kk2_t, d_s2, preferred_element_type=F32)
                dq_scr[:, csl] = (_rope(dq_t.T, c_ref[...], -s_ref[...]) * 0.125).astype(BF16)
            dsk_ref[...] += dsk

        @pl.when(n > 0)
        def _():
            for m2 in range(2):
                halves = []
                for g in (2 * m2, 2 * m2 + 1):
                    for acc in (acc_k, acc_v):
                        comb = jnp.where(lo, acc[2 * g, 0:CHUNK], acc[2 * g + 1, 0:CHUNK])
                        halves.append(comb + pltpu.roll(comb, 64, axis=1))
                d_kr = jnp.where(lo, halves[0], halves[2])
                d_v = jnp.where(lo, halves[1], halves[3])
                dqkv_ref[:, OFF_K + m2 * 128:OFF_K + (m2 + 1) * 128] = _rope(d_kr, cp_ref[...], -sp_ref[...]).astype(BF16)
                dqkv_ref[:, OFF_V + m2 * 128:OFF_V + (m2 + 1) * 128] = d_v.astype(BF16)

    qn = lambda n: jnp.minimum(n, nb - 1)
    pn = lambda n: jnp.maximum(jnp.minimum(n, nb) - 1, 0)
    cur = pl.BlockSpec((CHUNK, 1024), lambda n: (qn(n), 0))
    prev = pl.BlockSpec((CHUNK, 1024), lambda n: (pn(n), 0))
    cur128 = pl.BlockSpec((CHUNK, 128), lambda n: (qn(n), 0))
    prev128 = pl.BlockSpec((CHUNK, 128), lambda n: (pn(n), 0))
    cur_t = pl.BlockSpec((1024, CHUNK), lambda n: (0, qn(n)))
    prev_t = pl.BlockSpec((1024, CHUNK), lambda n: (0, pn(n)))
    one = pl.BlockSpec((1, 128), lambda n: (0, 0))
    return pl.pallas_call(
        body, name="attn_bwd", grid=(nb + 1,),
        in_specs=[cur, cur, prev, cur, prev, cur_t, prev_t, cur, cur, pl.BlockSpec((1, 16, CHUNK), lambda n: (qn(n), 0, 0)),
                  one, cur128, cur128, prev128, prev128, pl.BlockSpec(memory_space=pl.ANY)] + r_in_specs,
        out_specs=[pl.BlockSpec((CHUNK, 1536), lambda n: (pn(n), 0)), one] + r_out_specs,
        out_shape=[SDS((t, PROJ_W), BF16), SDS((1, 128), F32)] + r_outs,
        scratch_shapes=[pltpu.VMEM((8, 2 * CHUNK, 128), F32), pltpu.VMEM((8, 2 * CHUNK, 128), F32),
                        pltpu.VMEM((CHUNK, 1024), BF16)] + r_scratch,
        input_output_aliases={15: 0},
        compiler_params=_cp(dimension_semantics=("arbitrary",)),
    )(qr, kp, kp, vp, vp, kt, kt, d_o, o, lse, sinks, cos_t, sin_t, cos_t, sin_t, d_proj, *r_ops)


def _adamw(name, w, g, m, v, *, tr):
    rows, cols = w.shape
    tr = min(tr, rows)
    assert rows % tr == 0

    def body(w_ref, g_ref, m_ref, v_ref, d_ref, nm_ref, nv_ref):
        gv = g_ref[...]
        nm = ADAM_B1 * m_ref[...] + (1.0 - ADAM_B1) * gv
        nv = ADAM_B2 * v_ref[...] + (1.0 - ADAM_B2) * (gv * gv)
        m_hat = nm / (1.0 - ADAM_B1 ** ADAM_STEP)
        v_hat = nv / (1.0 - ADAM_B2 ** ADAM_STEP)
        d_ref[...] = -ADAM_LR * (m_hat / (jnp.sqrt(v_hat) + ADAM_EPS) + ADAM_WD * w_ref[...])
        nm_ref[...] = nm
        nv_ref[...] = nv

    tile = pl.BlockSpec((tr, cols), lambda i: (i, 0))
    return pl.pallas_call(
        body, name=name, grid=(rows // tr,), in_specs=[tile] * 4, out_specs=[tile] * 3,
        out_shape=[SDS((rows, cols), F32)] * 3, compiler_params=_cp(dimension_semantics=("arbitrary",)),
    )(w, g, m, v)


def _local_step(x, cos_t, sin_t, tgt, wb, ps, late=None, rides=None):
    t = x.shape[0]
    tm = min(512, t)
    tmw = min(1024, t)
    ij = lambda i, j, k: (i, j)
    i0 = lambda i, j, k: (i, 0)
    c0 = lambda i, j, k: (0, 0)
    cj = lambda i, j, k: (0, j)
    rides = rides or (lambda group, grads: None)
    rode = {}

    tkt = min(2048, t)
    proj, u, *arrived = _norm_mm("in_proj", x, ps['norm_mix_pre_w'], wb['cat'], tm=tmw, ride=late[0] if late else None)
    if late:
        more_wb, more_ps = late[1](arrived)
        wb, ps = {**wb, **more_wb}, {**ps, **more_ps}
    xc, xc_pre = _conv_silu_fwd(proj, ps['ssd_conv_w'], ps['ssd_conv_b'], tm=tm)
    bias_pad = jnp.pad(ps['ssd_dt_bias'], ((0, 0), (0, 96)))
    dtg, dtg_t = _dt_fwd(proj, bias_pad, tm=tmw)
    alog = jnp.pad(ps['ssd_a_log'].reshape(SSD_N_GROUPS, 1, 8), ((0, 0), (0, 0), (0, 120)))
    alog_t = ps['ssd_a_log'].reshape(SSD_N_GROUPS, 8, 1)
    d_exp = jnp.repeat(ps['ssd_d'], SSD_HEAD_DIM, axis=1)
    y, hs = _ssd_fwd(xc, dtg, dtg_t, alog, alog_t, d_exp)
    gn = _gated_norm_fwd(y, proj, ps['ssd_norm_w'], tm=tmw)
    qr, kp, vp, kt, vt = _rope_qkv(proj, cos_t, sin_t, tm=tm)
    sinks = jnp.pad(ps['attn_sinks'], ((0, 0), (0, 112)))
    ao, lse = _attn_fwd(qr, kp, vt, sinks)
    y_attn = _mm_plain("attn_out", ao, wb['ao'], tm=tmw, tn=512, tk=1024)

    def merge_ep(acc, i, j, ins, outs):
        gs, ga, ya = ins
        outs[0][...] = (_sigmoid(gs[...]) * acc + _sigmoid(ga[...]) * ya[...]).astype(BF16)
        outs[1][...] = acc

    merged, y_ssd = _mm_call(
        "ssd_out_merge", gn, wb['so'], tm=tmw, tn=512, tk=2048, epilogue=merge_ep,
        extra_in=[(proj, (tmw, 512), lambda i, j, k: (i, OFF_GS // 512 + j)),
                  (proj, (tmw, 512), lambda i, j, k: (i, OFF_GA // 512 + j)), (y_attn, (tmw, 512), ij)],
        outs=[((t, D_MODEL), BF16, (tmw, 512), ij), ((t, D_MODEL), F32, (tmw, 512), ij)])

    def mix_ep(acc, i, j, ins, outs):
        xv, wn = ins
        r = lax.rsqrt(jnp.mean(acc * acc, axis=-1, keepdims=True) + NORM_EPS)
        outs[0][...] = xv[...] + acc * r * wn[...]
        outs[1][...] = acc

    x1, mmix = _mm_call(
        "mix_out", merged, wb['mix'], tm=tm, tn=D_MODEL, tk=1024, epilogue=mix_ep,
        extra_in=[(x, (tm, D_MODEL), i0), (ps['norm_mix_post_w'], (1, D_MODEL), c0)],
        outs=[((t, D_MODEL), F32, (tm, D_MODEL), i0), ((t, D_MODEL), F32, (tm, D_MODEL), i0)])

    up_raw, h = _norm_mm("ffn_up", x1, ps['norm_ffn_pre_w'], wb['up'], tm=tmw)
    act, ffn_gate, ffn_val = _ffn_act_fwd(up_raw, ps['ffn_conv_w'], ps['ffn_conv_b'], tm=tm)

    def loss_ep(acc, i, j, ins, outs):
        x1v, tg, wn = ins
        d_ff_ref, dout_ref, loss_ref, dw_ref = outs
        wv = wn[...]
        r = lax.rsqrt(jnp.mean(acc * acc, axis=-1, keepdims=True) + NORM_EPS)
        err = x1v[...] + acc * r * wv - tg[...]
        dout = err * (1.0 / D_MODEL)
        dout_ref[...] = dout
        d_ff, dw = _rms_bwd(acc, wv, dout)
        d_ff_ref[...] = d_ff.astype(BF16)
        _accumulate(dw_ref, i == 0, dw)
        _accumulate(loss_ref, i == 0, jnp.sum(err * err, keepdims=True) * (0.5 / D_MODEL))

    d_ff, dout, loss, g_norm_ffn_post = _mm_call(
        "ffn_down_loss", act, wb['dn'], tm=tm, tn=D_MODEL, tk=FFN_D_FF, epilogue=loss_ep,
        extra_in=[(x1, (tm, D_MODEL), i0), (tgt, (tm, D_MODEL), i0), (ps['norm_ffn_post_w'], (1, D_MODEL), c0)],
        outs=[((t, D_MODEL), BF16, (tm, D_MODEL), i0), ((t, D_MODEL), F32, (tm, D_MODEL), i0),
              ((1, 1), F32, (1, 1), c0), ((1, D_MODEL), F32, (1, D_MODEL), c0)])

    d_act = _mm_plain("d_act", d_ff, wb['dn_t'], tm=tmw, tn=1408, tk=1024, out_dtype=BF16)
    g_w_down = _mm_plain("g_w_down", act, d_ff, tm=1408, tn=1024, tk=tkt, trans_a=True, out_dtype=BF16)
    d_gate, d_val, db_g, db_v = _ffn_act_bwd(ffn_gate, ffn_val, d_act, tm=tm)
    d_up_raw, gcw_g = _conv_bwd2("ffn_conv_bwd2_gate", d_gate, up_raw, 0, ps['ffn_conv_w'][:, :FFN_D_FF], tm=tm,
                                 tc=1408, out_cols=2 * FFN_D_FF, out_col0=0)
    d_up_raw, gcw_v = _conv_bwd2("ffn_conv_bwd2_val", d_val, up_raw, FFN_D_FF, ps['ffn_conv_w'][:, FFN_D_FF:], tm=tm,
                                 tc=1408, out_cols=2 * FFN_D_FF, out_col0=FFN_D_FF, fill=d_up_raw)
    g_ffn_conv_w = jnp.concatenate([gcw_g, gcw_v], axis=1)

    def dx1_ep(acc, i, j, ins, outs):
        x1v, wpre, dout_v, mmv, wpost = ins
        d_x1_ref, d_mm_ref, dwpre_ref, dwpost_ref = outs
        d_n, dw_pre = _rms_bwd(x1v[...], wpre[...], acc)
        d_x1 = dout_v[...] + d_n
        d_x1_ref[...] = d_x1
        d_mm, dw_post = _rms_bwd(mmv[...], wpost[...], d_x1)
        d_mm_ref[...] = d_mm.astype(BF16)
        _accumulate(dwpre_ref, i == 0, dw_pre)
        _accumulate(dwpost_ref, i == 0, dw_post)

    d_x1, d_mm, g_norm_ffn_pre, g_norm_mix_post = _mm_call(
        "d_h", d_up_raw, wb['up_t'], tm=tm, tn=D_MODEL, tk=2 * FFN_D_FF, epilogue=dx1_ep, vmem_mb=VMEM_BIG_MB,
        extra_in=[(x1, (tm, D_MODEL), i0), (ps['norm_ffn_pre_w'], (1, D_MODEL), c0), (dout, (tm, D_MODEL), i0),
                  (mmix, (tm, D_MODEL), i0), (ps['norm_mix_post_w'], (1, D_MODEL), c0)],
        outs=[((t, D_MODEL), F32, (tm, D_MODEL), i0), ((t, D_MODEL), BF16, (tm, D_MODEL), i0),
              ((1, D_MODEL), F32, (1, D_MODEL), c0), ((1, D_MODEL), F32, (1, D_MODEL), c0)])
    g_w_up_t = _mm_plain("g_w_up", d_up_raw, h, tm=1408, tn=1024, tk=tkt, trans_a=True, out_dtype=BF16)
    ride_ffn = rides('ffn', {'ffn_w_up': g_w_up_t, 'ffn_w_down': g_w_down})

    def dmerge_ep(acc, i, j, ins, outs):
        gs, ga, ys, ya = ins
        sg_s, sg_a = _sigmoid(gs[...]), _sigmoid(ga[...])
        outs[0][...] = (acc * sg_s).astype(BF16)
        outs[1][...] = (acc * sg_a).astype(BF16)
        outs[2][:, 0:D_MODEL] = (acc * ys[...] * sg_s * (1.0 - sg_s)).astype(BF16)
        outs[2][:, D_MODEL:2 * D_MODEL] = (acc * ya[...] * sg_a * (1.0 - sg_a)).astype(BF16)

    d_yssd, d_yattn, d_proj = _mm_call(
        "d_merged", d_mm, wb['mix_t'], tm=tm, tn=D_MODEL, tk=1024, epilogue=dmerge_ep,
        extra_in=[(proj, (tm, D_MODEL), lambda i, j, k: (i, OFF_GS // D_MODEL)),
                  (proj, (tm, D_MODEL), lambda i, j, k: (i, OFF_GA // D_MODEL)), (y_ssd, (tm, D_MODEL), i0), (y_attn, (tm, D_MODEL), i0)],
        outs=[((t, D_MODEL), BF16, (tm, D_MODEL), i0), ((t, D_MODEL), BF16, (tm, D_MODEL), i0),
              ((t, PROJ_W), BF16, (tm, 2 * D_MODEL), lambda i, j, k: (i, OFF_GS // (2 * D_MODEL)))])
    g_w_mix = _mm_plain("g_w_mix", merged, d_mm, tm=1024, tn=1024, tk=tkt, trans_a=True, out_dtype=BF16)

    def dgn_ep(acc, i, j, ins, outs):
        yv, zv, wn = ins
        d_y_ref, d_z_ref, dw_ref = outs
        zz = zv[...]
        sz = _sigmoid(zz)
        silu = zz * sz
        gv = yv[...] * silu
        r = lax.rsqrt(jnp.mean(gv * gv, axis=-1, keepdims=True) + NORM_EPS)
        gh = gv * r
        dgh = acc * wn[...]
        dg = r * (dgh - gh * jnp.mean(dgh * gh, axis=-1, keepdims=True))
        d_y_ref[...] = dg * silu
        d_z_ref[...] = (dg * yv[...] * (sz * (1.0 + zz * (1.0 - sz)))).astype(BF16)
        dw = jnp.sum(acc * gh, axis=0, keepdims=True)

        @pl.when(i == 0)
        def _():
            dw_ref[j] = dw

        @pl.when(i > 0)
        def _():
            dw_ref[j] += dw

    d_y, d_proj, g_ssd_norm = _mm_call(
        "d_gn", d_yssd, wb['so_t'], tm=tmw, tn=512, tk=1024, epilogue=dgn_ep, fill=(d_proj, 1),
        extra_in=[(y, (tmw, 512), ij), (proj, (tmw, 512), lambda i, j, k: (i, OFF_Z // 512 + j)), (ps['ssd_norm_w'], (1, 512), cj)],
        outs=[((t, SSD_D_INNER), F32, (tmw, 512), ij), ((t, PROJ_W), BF16, (tmw, 512), lambda i, j, k: (i, OFF_Z // 512 + j)),
              ((SSD_N_GROUPS, 1, 512), F32, (SSD_N_GROUPS, 1, 512), lambda i, j, k: (0, 0, 0))])
    g_ssd_norm = g_ssd_norm.reshape(1, SSD_D_INNER)
    g_w_so = _mm_plain("g_w_so", gn, d_yssd, tm=1024, tn=1024, tk=tkt, trans_a=True, out_dtype=BF16)
    d_xc, d_dtg, d_alog, d_dd, *rode['ffn'] = _ssd_bwd(xc, dtg, dtg_t, alog, alog_t, d_exp, d_y, hs, ride=ride_ffn)
    d_pre, g_ssd_conv_b = _conv_silu_bwd1(d_xc, xc_pre, tm=tm)
    d_proj, g_ssd_conv_w = _conv_bwd2("ssd_conv_bwd2", d_pre, proj, OFF_XBC, ps['ssd_conv_w'], tm=tm, tc=1536,
                                      out_cols=PROJ_W, out_col0=OFF_XBC, fill=d_proj)
    d_proj, g_dt_bias = _dt_bwd(d_dtg, proj, bias_pad, d_proj, tm=tmw)

    d_ao = _mm_plain("d_ao", d_yattn, wb['ao_t'], tm=tmw, tn=512, tk=1024, out_dtype=BF16)
    g_w_ao = _mm_plain("g_w_ao", ao, d_yattn, tm=1024, tn=1024, tk=tkt, trans_a=True, out_dtype=BF16)
    ride_mix = rides('mix', {'ssd_w_out': g_w_so, 'attn_w_out': g_w_ao, 'w_mix_out': g_w_mix})
    d_proj, g_sinks, *rode['mix'] = _attn_bwd(qr, kp, vp, kt, d_ao, ao, lse, sinks, cos_t, sin_t, d_proj, ride=ride_mix)

    def dx_ep(acc, i, j, ins, outs):
        xv, wn, dx1v = ins
        d_n, dw = _rms_bwd(xv[...], wn[...], acc)
        outs[0][...] = dx1v[...] + d_n
        _accumulate(outs[1], i == 0, dw)

    g_cat_t = _mm_plain("g_w_in", d_proj, u, tm=1024, tn=1024, tk=tkt, trans_a=True, out_dtype=BF16)
    grad_x, g_norm_mix_pre, *rode['w_in'] = _mm_call(
        "d_u", d_proj, wb['cat_t'], tm=tm, tn=D_MODEL, tk=PROJ_W, epilogue=dx_ep, ride=rides('w_in', {'w_in': g_cat_t}),
        vmem_mb=VMEM_BIG_MB,
        extra_in=[(x, (tm, D_MODEL), i0), (ps['norm_mix_pre_w'], (1, D_MODEL), c0), (d_x1, (tm, D_MODEL), i0)],
        outs=[((t, D_MODEL), F32, (tm, D_MODEL), i0), ((1, D_MODEL), F32, (1, D_MODEL), c0)])

    grads = {
        'norm_mix_pre_w': g_norm_mix_pre, 'w_in': g_cat_t, 'ssd_conv_w': g_ssd_conv_w, 'ssd_conv_b': g_ssd_conv_b,
        'ssd_dt_bias': g_dt_bias[:, :SSD_N_HEADS], 'ssd_a_log': d_alog[:, 0, :8].reshape(1, SSD_N_HEADS),
        'ssd_d': d_dd[:, 0, :8].reshape(1, SSD_N_HEADS), 'ssd_norm_w': g_ssd_norm, 'ssd_w_out': g_w_so,
        'attn_sinks': g_sinks[:, :ATTN_N_HEADS], 'attn_w_out': g_w_ao, 'w_mix_out': g_w_mix,
        'norm_mix_post_w': g_norm_mix_post, 'norm_ffn_pre_w': g_norm_ffn_pre, 'ffn_w_up': g_w_up_t,
        'ffn_conv_w': g_ffn_conv_w, 'ffn_conv_b': jnp.concatenate([db_g, db_v], axis=1), 'ffn_w_down': g_w_down,
        'norm_ffn_post_w': g_norm_ffn_post,
    }
    return loss, grad_x, grads, rode


def _group_channels(a):
    parts = []
    for g in range(SSD_N_GROUPS):
        parts += [a[..., 512 * g:512 * (g + 1)], a[..., 2048 + 128 * g:2048 + 128 * (g + 1)],
                  a[..., 2560 + 128 * g:2560 + 128 * (g + 1)]]
    return jnp.concatenate(parts, axis=-1)


def _ungroup_channels(a):
    xs = [a[..., GROUP_W * g:GROUP_W * g + 512] for g in range(SSD_N_GROUPS)]
    bs = [a[..., GROUP_W * g + 512:GROUP_W * g + 640] for g in range(SSD_N_GROUPS)]
    cs = [a[..., GROUP_W * g + 640:GROUP_W * (g + 1)] for g in range(SSD_N_GROUPS)]
    return jnp.concatenate(xs + bs + cs, axis=-1)


def _proj_rows(a_t, lo, hi):
    out = []
    for start, length, dst in sorted(PROJ_SEGS):
        s, e = max(lo, start), min(hi, start + length)
        if s < e:
            out.append(a_t[dst + s - start:dst + e - start])
    return out


def _to_proj_layout(w_in_t):
    pieces, pos = [], 0
    for start, length, dst in sorted(PROJ_SEGS, key=lambda s: s[2]):
        if dst > pos:
            pieces.append(jnp.zeros((dst - pos, w_in_t.shape[1]), w_in_t.dtype))
        pieces.append(w_in_t[start:start + length])
        pos = dst + length
    if pos < PROJ_W:
        pieces.append(jnp.zeros((PROJ_W - pos, w_in_t.shape[1]), w_in_t.dtype))
    return jnp.concatenate(pieces, axis=0)


def _rope_tables(positions):
    half = 32
    inv_freq = ROPE_THETA ** (-jnp.arange(half, dtype=F32) * 2.0 / 64)
    ang = positions.astype(F32)[:, None] * inv_freq
    cos, sin = jnp.cos(ang), jnp.sin(ang)
    return jnp.concatenate([cos, cos, cos, cos], axis=1), jnp.concatenate([-sin, sin, -sin, sin], axis=1)


def _matmul_weights(w_in_t):
    cat_t = _to_proj_layout(w_in_t)
    return {'cat': _column_tiles(cat_t, 1024), 'cat_t': cat_t}


def _late_weights(so, ao, mix, up_t, dn):
    return {'so': so, 'so_t': so.T, 'ao': ao, 'ao_t': ao.T, 'mix': mix, 'mix_t': mix.T,
            'up': _column_tiles(up_t, 1408), 'up_t': up_t, 'dn': dn, 'dn_t': dn.T}


def _column_tiles(w_t, tn):
    n, dm = w_t.shape
    return w_t.reshape(n // tn, tn, dm).transpose(0, 2, 1)


ANY = pl.BlockSpec(memory_space=pl.ANY)
MESH = pl.DeviceIdType.MESH
ROW_ALIGN = 32


def _mesh_pos():
    return lax.axis_index("x"), lax.axis_index("y"), lax.axis_index("c")


def _other_chips(x, y):
    return [(1 - x, y), (x, 1 - y), (1 - x, 1 - y)]


def _remote(src, dst, send_sems, recv_sems, k, to):
    return pltpu.make_async_remote_copy(src_ref=src, dst_ref=dst, send_sem=send_sems.at[k], recv_sem=recv_sems.at[k],
                                        device_id=to, device_id_type=MESH)


def _half(c, rh):
    return pl.ds(pl.multiple_of(c * rh, 16), rh)


def _ag_ride(shard):
    r = shard.shape[0]
    rh = r // 2

    def first_copies(w_ref, out_ref, send_sems, recv_sems):
        x, y, c = _mesh_pos()
        p = 2 * x + y
        mine = _half(c, rh)
        cps = [_remote(w_ref, out_ref.at[p], send_sems, recv_sems, 6, (x, y, 1 - c))]
        return cps + [_remote(w_ref.at[mine], out_ref.at[p, mine], send_sems, recv_sems, j, (cx, cy, c))
                      for j, (cx, cy) in enumerate(_other_chips(x, y))]

    def start(ins, outs, send_sems, recv_sems):
        for cp in first_copies(ins[0], outs[0], send_sems, recv_sems):
            cp.start()

    def forwards(out_ref, send_sems, recv_sems, half):
        x, y, c = _mesh_pos()
        return [_remote(out_ref.at[2 * cx + cy, half], out_ref.at[2 * cx + cy, half], send_sems, recv_sems, 3 + j, (x, y, 1 - c))
                for j, (cx, cy) in enumerate(_other_chips(x, y))]

    def middle(ins, outs, send_sems, recv_sems):
        x, y, c = _mesh_pos()
        mine = _half(c, rh)
        for j, (fwd, (cx, cy)) in enumerate(zip(forwards(outs[0], send_sems, recv_sems, mine), _other_chips(x, y))):
            slab = outs[0].at[2 * cx + cy, mine]
            _remote(slab, slab, send_sems, recv_sems, j, (x, y, 1 - c)).wait_recv()
            fwd.start()

    def finish(ins, outs, send_sems, recv_sems):
        w_ref, out_ref = ins[0], outs[0]
        x, y, c = _mesh_pos()
        for cp in forwards(out_ref, send_sems, recv_sems, _half(1 - c, rh)):
            cp.wait_recv()
        _remote(w_ref, out_ref.at[2 * x + y], send_sems, recv_sems, 6, (x, y, 1 - c)).wait_recv()
        for cp in first_copies(w_ref, out_ref, send_sems, recv_sems) + forwards(out_ref, send_sems, recv_sems, _half(c, rh)):
            cp.wait_send()

    return _Ride((shard,), (SDS((N_CHIPS, r, COMM_LANES), shard.dtype),), 7, start, finish, middle)


def _rs_ride(gbuf):
    rh = gbuf.shape[1] // 2

    def copies(g_ref, r_ref, send_sems, recv_sems, landing):
        x, y, c = _mesh_pos()
        cps = []
        for k, (cx, cy) in enumerate(_other_chips(x, y)):
            for h in range(2):
                slot = 2 * k + c if landing else 2 * k + h
                cps.append(pltpu.make_async_remote_copy(
                    src_ref=g_ref.at[2 * cx + cy, pl.ds(h * rh, rh)], dst_ref=r_ref.at[slot],
                    send_sem=send_sems.at[2 * k + h], recv_sem=recv_sems.at[slot],
                    device_id=(cx, cy, h), device_id_type=MESH))
        cps.append(_remote(g_ref.at[2 * x + y, _half(1 - c, rh)], r_ref.at[6], send_sems, recv_sems, 6, (x, y, 1 - c)))
        return cps

    def start(ins, outs, send_sems, recv_sems):
        for cp in copies(ins[0], outs[0], send_sems, recv_sems, True):
            cp.start()

    def finish(ins, outs, send_sems, recv_sems):
        for cp in copies(ins[0], outs[0], send_sems, recv_sems, False):
            cp.wait()

    return _Ride((gbuf,), (SDS((7, rh, COMM_LANES), gbuf.dtype),), 7, start, finish)


def _rs_sum(name, gbuf, got, pc_idx):
    rh = got.shape[1]
    tr = max(d for d in range(16, 513, 16) if rh % d == 0)
    nb = rh // tr

    def body(pc_ref, own_ref, *refs):
        o_ref = refs[7]
        p, c = pc_ref[0], pc_ref[1]
        own = own_ref[0].astype(F32)
        slots = [r[0].astype(F32) for r in refs[:7]]

        def term(q, h):
            code = p ^ q
            far = jnp.where(code == 2, slots[h], jnp.where(code == 1, slots[2 + h], slots[4 + h]))
            return jnp.where(code == 0, jnp.where(c == h, own, slots[6]), far)

        acc = term(0, 0)
        for q, h in [(0, 1), (1, 0), (1, 1), (2, 0), (2, 1), (3, 0), (3, 1)]:
            acc = acc + term(q, h)
        o_ref[0] = acc

    slot = lambda s: pl.BlockSpec((1, tr, COMM_LANES), lambda i, pc: (s, i, 0))
    return pl.pallas_call(
        body, name=name,
        grid_spec=pltpu.PrefetchScalarGridSpec(
            num_scalar_prefetch=1, grid=(nb,),
            in_specs=[pl.BlockSpec((1, tr, COMM_LANES), lambda i, pc: (pc[0], pc[1] * nb + i, 0))] + [slot(s) for s in range(7)],
            out_specs=pl.BlockSpec((1, tr, COMM_LANES), lambda i, pc: (pc[1], i, 0))),
        out_shape=SDS((2, rh, COMM_LANES), F32), compiler_params=_cp(dimension_semantics=("arbitrary",)),
    )(pc_idx, gbuf, *([got] * 7))


def _pair_gather_all(bufs):
    n = len(bufs)

    def body(*refs):
        outs, send_sems, recv_sems = refs[n:2 * n], refs[2 * n], refs[2 * n + 1]
        x, y, c = _mesh_pos()
        cps = [_remote(o.at[c], o.at[c], send_sems, recv_sems, k, (x, y, 1 - c)) for k, o in enumerate(outs)]
        for cp in cps:
            cp.start()
        for k, o in enumerate(outs):
            _remote(o.at[1 - c], o.at[1 - c], send_sems, recv_sems, k, (x, y, 1 - c)).wait_recv()
        for cp in cps:
            cp.wait_send()

    return pl.pallas_call(
        body, name="grad_pair_gather", in_specs=[ANY] * n, out_specs=[ANY] * n,
        out_shape=[SDS(b.shape, b.dtype) for b in bufs],
        scratch_shapes=[pltpu.SemaphoreType.DMA((n,)), pltpu.SemaphoreType.DMA((n,))],
        input_output_aliases={k: k for k in range(n)},
    )(*bufs)


def _pack_rows(big, small=()):
    parts = list(big)
    if small:
        flat = jnp.concatenate([p.reshape(-1) for p in small])
        k = -(-flat.shape[0] // (16 * COMM_LANES)) * 16
        parts.append(jnp.pad(flat, (0, k * COMM_LANES - flat.shape[0])).reshape(k, COMM_LANES))
    pad = -sum(p.shape[0] for p in parts) % ROW_ALIGN
    if pad:
        parts.append(jnp.zeros((pad, COMM_LANES), parts[0].dtype))
    return jnp.concatenate(parts, axis=0) if len(parts) > 1 else parts[0]


def _take(flat, off, shape):
    n = 1
    for d in shape:
        n *= d
    return flat[off:off + n].reshape(shape), off + n


BIG_ROWS = {'w_in': 2184, 'ssd_w_out': 512, 'attn_w_out': 256, 'w_mix_out': 256, 'ffn_w_up': 1408, 'ffn_w_down': 704}
TRANSPOSED = ('w_in', 'ffn_w_up')
LATE = ('ssd_w_out', 'attn_w_out', 'w_mix_out', 'ffn_w_up', 'ffn_w_down')
CONV_TAPS = ('ssd_conv_w', 'ffn_conv_w')
RS_GROUPS = {'ffn': ('ffn_w_up', 'ffn_w_down'), 'mix': ('ssd_w_out', 'attn_w_out', 'w_mix_out'), 'w_in': ('w_in',)}


def _exchange(name, ride):
    n_in, n_out = len(ride.ins), len(ride.outs)

    def body(*refs):
        ins, outs, sems = refs[:n_in], refs[n_in:n_in + n_out], refs[n_in + n_out:]
        ride.start(ins, outs, *sems)
        if ride.middle is not None:
            ride.middle(ins, outs, *sems)
        ride.finish(ins, outs, *sems)

    return pl.pallas_call(
        body, name=name, in_specs=[ANY] * n_in, out_specs=[ANY] * n_out, out_shape=list(ride.outs),
        scratch_shapes=[pltpu.SemaphoreType.DMA((ride.n_sems,)), pltpu.SemaphoreType.DMA((ride.n_sems,))],
    )(*ride.ins)


def kernel(x, positions, norm_mix_pre_w, w_in, ssd_conv_w, ssd_conv_b, ssd_dt_bias, ssd_a_log, ssd_d, ssd_norm_w, ssd_w_out, attn_sinks, attn_w_out, w_mix_out, norm_mix_post_w, norm_ffn_pre_w, ffn_w_up, ffn_conv_w, ffn_conv_b, ffn_w_down, norm_ffn_post_w, loss_target, m_norm_mix_pre_w, m_w_in, m_ssd_conv_w, m_ssd_conv_b, m_ssd_dt_bias, m_ssd_a_log, m_ssd_d, m_ssd_norm_w, m_ssd_w_out, m_attn_sinks, m_attn_w_out, m_w_mix_out, m_norm_mix_post_w, m_norm_ffn_pre_w, m_ffn_w_up, m_ffn_conv_w, m_ffn_conv_b, m_ffn_w_down, m_norm_ffn_post_w, v_norm_mix_pre_w, v_w_in, v_ssd_conv_w, v_ssd_conv_b, v_ssd_dt_bias, v_ssd_a_log, v_ssd_d, v_ssd_norm_w, v_ssd_w_out, v_attn_sinks, v_attn_w_out, v_w_mix_out, v_norm_mix_post_w, v_norm_ffn_pre_w, v_ffn_w_up, v_ffn_conv_w, v_ffn_conv_b, v_ffn_w_down, v_norm_ffn_post_w):
    given = dict(locals())
    w = {n: given[n][0] for n in WEIGHTS}
    w = {n: (a if a.ndim == 2 else a[None]) for n, a in w.items()}
    mom_m = {n: given['m_' + n].reshape(w[n].shape) for n in WEIGHTS}
    mom_v = {n: given['v_' + n].reshape(w[n].shape) for n in WEIGHTS}
    cx, cy, cc = _mesh_pos()
    pc_idx = jnp.stack([2 * cx + cy, cc]).astype(jnp.int32)

    rows_of = lambda n: (w[n].T if n in TRANSPOSED else w[n]).astype(BF16)
    gathered = _exchange("w_in_all_gather", _ag_ride(_pack_rows([rows_of('w_in')])))[0]
    wb = _matmul_weights(jnp.concatenate([gathered[s, :BIG_ROWS['w_in']] for s in range(N_CHIPS)], axis=0))
    taps = [lax.bitcast_convert_type(w[n], BF16) for n in CONV_TAPS]

    def unpack_late(arrived):
        rows, conv = {n: [] for n in LATE}, {n: [] for n in CONV_TAPS}
        for s in range(N_CHIPS):
            r0 = 0
            for n in LATE:
                rows[n].append(arrived[0][s, r0:r0 + BIG_ROWS[n]])
                r0 += BIG_ROWS[n]
            flat, off = arrived[0][s, r0:r0 + 16].reshape(-1), 0
            for n in CONV_TAPS:
                a, off = _take(flat, off, w[n].shape + (2,))
                conv[n].append(lax.bitcast_convert_type(a, F32))
        full = {n: jnp.concatenate(rows[n], axis=0) for n in LATE}
        return (_late_weights(*[full[n] for n in LATE]),
                {'ssd_conv_w': _group_channels(jnp.concatenate(conv['ssd_conv_w'], axis=1)),
                 'ffn_conv_w': jnp.concatenate(conv['ffn_conv_w'], axis=1)})

    late = (_ag_ride(_pack_rows([rows_of(n) for n in LATE], taps)), unpack_late)

    sent = {}

    def rides(group, g):
        parts = []
        for s in range(N_CHIPS):
            slab = []
            for n in RS_GROUPS[group]:
                lo, hi = BIG_ROWS[n] * s, BIG_ROWS[n] * (s + 1)
                slab += _proj_rows(g[n], lo, hi) if n == 'w_in' else [g[n][lo:hi]]
            slab = [a.astype(BF16) for a in slab]
            pad = -sum(a.shape[0] for a in slab) % ROW_ALIGN
            parts += slab + ([jnp.zeros((pad, COMM_LANES), BF16)] if pad else [])
        sent[group] = jnp.concatenate(parts, axis=0).reshape(N_CHIPS, -1, COMM_LANES)
        return _rs_ride(sent[group])

    ps = {n: w[n] for n in REPLICATED}
    ps['ssd_conv_b'] = _group_channels(w['ssd_conv_b'])
    cos_t, sin_t = _rope_tables(positions[0])
    loss, grad_x, grads, rode = _local_step(x[0], cos_t, sin_t, loss_target[0], wb, ps, late, rides)
    grads['ssd_conv_w'] = _ungroup_channels(grads['ssd_conv_w'])
    grads['ssd_conv_b'] = _ungroup_channels(grads['ssd_conv_b'])

    shard_cols = {n: sh[1] for n, _, sh in SHARDED}
    parts = []
    for s in range(N_CHIPS):
        small = [grads[n][:, shard_cols[n] * s:shard_cols[n] * (s + 1)] for n in CONV_TAPS] + [grads[n] for n in REPLICATED]
        flat = _pack_rows([], small)
        high = flat.astype(BF16)
        parts += [high, (flat - high.astype(F32)).astype(BF16)]
    sent['small'] = jnp.concatenate(parts, axis=0).reshape(N_CHIPS, -1, COMM_LANES)
    rode['small'] = _exchange("grad_small_exchange", _rs_ride(sent['small']))

    groups = ('ffn', 'mix', 'w_in', 'small')
    red = _pair_gather_all([_rs_sum("grad_sum_" + g, sent[g], rode[g][0], pc_idx) for g in groups])
    red = {g: r.reshape(-1, COMM_LANES) for g, r in zip(groups, red)}
    g_red = {}
    for g in groups[:3]:
        r0 = 0
        for n in RS_GROUPS[g]:
            g_red[n] = red[g][r0:r0 + BIG_ROWS[n]].T if n in TRANSPOSED else red[g][r0:r0 + BIG_ROWS[n]]
            r0 += BIG_ROWS[n]
    half = red['small'].shape[0] // 2
    flat, off = (red['small'][:half] + red['small'][half:]).reshape(-1), 0
    for n in CONV_TAPS + REPLICATED:
        g_red[n], off = _take(flat, off, w[n].shape)

    small_names = [n for n in WEIGHTS if n not in MATMUL_WEIGHTS]
    delta, new_m, new_v = {}, {}, {}
    for n in MATMUL_WEIGHTS:
        delta[n], new_m[n], new_v[n] = _adamw("adamw_" + n, w[n], g_red[n], mom_m[n], mom_v[n],
                                                  tr=max(d for d in range(8, 353, 8) if w[n].shape[0] % d == 0))
    packed = [_pack_small([d[n] for n in small_names]) for d in (w, g_red, mom_m, mom_v)]
    outs = _adamw("adamw_small", *packed, tr=packed[0].shape[0])
    for res, o in zip((delta, new_m, new_v), outs):
        fl, off = o.reshape(-1), 0
        for n in small_names:
            res[n], off = _take(fl, off, w[n].shape)

    loss_all = lax.psum(loss[0, 0], ("x", "y", "c"))
    shaped = lambda d: [d[n].reshape(given[n].shape) for n in WEIGHTS]
    return (loss_all, grad_x[None], *shaped(g_red), *shaped(delta), *shaped(new_m), *shaped(new_v))


def _pack_small(pieces):
    flat = jnp.concatenate([p.reshape(-1) for p in pieces])
    rows = -(-flat.shape[0] // (128 * 8)) * 8
    return jnp.pad(flat, (0, rows * 128 - flat.shape[0])).reshape(rows, 128)
```

```python
from typing import Callable, NamedTuple

import jax
import jax.numpy as jnp
from jax import lax
from jax.experimental import pallas as pl
from jax.experimental.pallas import tpu as pltpu

F32 = jnp.float32
BF16 = jnp.bfloat16
SDS = jax.ShapeDtypeStruct

D_MODEL = 1024
SSD_D_INNER = 2048
SSD_N_HEADS = 32
SSD_HEAD_DIM = 64
SSD_N_GROUPS = 4
SSD_D_STATE = 128
SSD_CONV_DIM = 3072
CHUNK = 128
ATTN_N_HEADS = 16
FFN_D_FF = 2816
ROPE_THETA = 10000.0
NORM_EPS = 1e-6
ADAM_LR, ADAM_B1, ADAM_B2, ADAM_EPS, ADAM_WD, ADAM_STEP = 0.001, 0.9, 0.999, 1e-08, 0.01, 10

PROJ_W = 9216
OFF_Q, OFF_K, OFF_V, OFF_Z, OFF_DT, OFF_GS, OFF_GA, OFF_XBC = 0, 1024, 1280, 1536, 3584, 4096, 5120, 6144
GROUP_W = 768
PROJ_SEGS = ([(0, 2048, OFF_Z)]
             + [(2048 + 512 * g, 512, OFF_XBC + GROUP_W * g) for g in range(4)]
             + [(4096 + 128 * g, 128, OFF_XBC + GROUP_W * g + 512) for g in range(4)]
             + [(4608 + 128 * g, 128, OFF_XBC + GROUP_W * g + 640) for g in range(4)]
             + [(5120, 32, OFF_DT), (5152, 1024, OFF_Q), (6176, 256, OFF_K), (6432, 256, OFF_V),
                (6688, 1024, OFF_GS), (7712, 1024, OFF_GA)])
VMEM_LIMIT_MB = 48
VMEM_BIG_MB = 60
NEG = -1e30

WEIGHTS = ('norm_mix_pre_w', 'w_in', 'ssd_conv_w', 'ssd_conv_b', 'ssd_dt_bias', 'ssd_a_log', 'ssd_d', 'ssd_norm_w',
           'ssd_w_out', 'attn_sinks', 'attn_w_out', 'w_mix_out', 'norm_mix_post_w', 'norm_ffn_pre_w', 'ffn_w_up',
           'ffn_conv_w', 'ffn_conv_b', 'ffn_w_down', 'norm_ffn_post_w')
SHARDED = (('w_in', 1, (1024, 2184)), ('ssd_conv_w', 1, (4, 768)), ('ssd_w_out', 0, (512, 1024)),
           ('attn_w_out', 0, (256, 1024)), ('w_mix_out', 0, (256, 1024)), ('ffn_w_up', 1, (1024, 1408)),
           ('ffn_conv_w', 1, (3, 1408)), ('ffn_w_down', 0, (704, 1024)))
MATMUL_WEIGHTS = ('w_in', 'ssd_w_out', 'attn_w_out', 'w_mix_out', 'ffn_w_up', 'ffn_w_down')
REPLICATED = tuple(n for n in WEIGHTS if n not in {s[0] for s in SHARDED})
N_CHIPS = 4
COMM_LANES = 1024


def _cp(vmem_mb=VMEM_LIMIT_MB, **kw):
    return pltpu.CompilerParams(vmem_limit_bytes=vmem_mb << 20, **kw)


class _Ride(NamedTuple):
    ins: tuple
    outs: tuple
    n_sems: int
    start: Callable
    finish: Callable
    middle: Callable = None


def _ride_parts(ride):
    if ride is None:
        return [], [], [], [], []
    hbm = pl.BlockSpec(memory_space=pl.ANY)
    return (list(ride.ins), [hbm] * len(ride.ins), list(ride.outs), [hbm] * len(ride.outs),
            [pltpu.SemaphoreType.DMA((ride.n_sems,)), pltpu.SemaphoreType.DMA((ride.n_sems,))])


def _ride_run(ride, first, last, in_refs, out_refs, sems, middle=None):
    if ride is None:
        return

    @pl.when(first)
    def _():
        ride.start(in_refs, out_refs, *sems)

    if ride.middle is not None:
        @pl.when(last if middle is None else middle)
        def _():
            ride.middle(in_refs, out_refs, *sems)

    @pl.when(last)
    def _():
        ride.finish(in_refs, out_refs, *sems)


def _iota(shape, axis):
    return lax.broadcasted_iota(jnp.int32, shape, axis)


def _sigmoid(v):
    return 1.0 / (1.0 + jnp.exp(-v))


def _mm_call(name, a, b, *, tm, tn, tk, epilogue, outs, extra_in=(), trans_a=False, fill=None, ride=None,
             vmem_mb=VMEM_LIMIT_MB):
    if trans_a:
        kdim, m = a.shape
    else:
        m, kdim = a.shape
    n = b.shape[1]
    assert b.shape[0] == kdim and m % tm == 0 and n % tn == 0 and kdim % tk == 0, (name, a.shape, b.shape, tm, tn, tk)
    gi, gj, gk = m // tm, n // tn, kdim // tk
    n_in, n_out = len(extra_in), len(outs)

    n_fill = 0 if fill is None else 1
    r_ops, r_in_specs, r_outs, r_out_specs, r_scratch = _ride_parts(ride)

    def body(a_ref, b_ref, *rest):
        ins = rest[:n_in]
        rest = rest[n_in + n_fill:]
        r_in, rest = rest[:len(r_ops)], rest[len(r_ops):]
        out_refs, rest = rest[:n_out], rest[n_out:]
        r_out, scratch = rest[:len(r_outs)], rest[len(r_outs):]
        i, j, k = pl.program_id(0), pl.program_id(1), pl.program_id(2)
        _ride_run(ride, (i == 0) & (j == 0) & (k == 0), (i == gi - 1) & (j == gj - 1) & (k == gk - 1),
                  r_in, r_out, scratch[-2:])
        av = a_ref[...].astype(BF16)
        bv = (b_ref[pl.ds(pl.multiple_of(k * tk, tk), tk), :] if b_whole else b_ref[...]).astype(BF16)
        if trans_a:
            part = lax.dot_general(av, bv, (((0,), (0,)), ((), ())), preferred_element_type=F32)
        else:
            part = jnp.dot(av, bv, preferred_element_type=F32)
        if gk == 1:
            epilogue(part, i, j, ins, out_refs)
        else:
            acc = scratch[0]

            @pl.when(k == 0)
            def _():
                acc[...] = part

            @pl.when(k > 0)
            def _():
                acc[...] += part

            @pl.when(k == gk - 1)
            def _():
                epilogue(acc[...], i, j, ins, out_refs)

    a_spec = pl.BlockSpec((tk, tm), lambda i, j, k: (k, i)) if trans_a else pl.BlockSpec((tm, tk), lambda i, j, k: (i, k))
    b_whole = gj == 1 and gk > 1 and gi > 1 and b.dtype == BF16 and kdim * tn * 2 <= (16 << 20)
    if gj == 1 and (gk == 1 or b_whole):
        b_spec = pl.BlockSpec((kdim, tn), lambda i, j, k: (0, 0), pipeline_mode=pl.Buffered(1))
    else:
        b_spec = pl.BlockSpec((tk, tn), lambda i, j, k: (k, j))
    in_specs = [a_spec, b_spec]
    in_specs += [pl.BlockSpec(bs, im) for _, bs, im in extra_in]
    operands = [a, b] + [e[0] for e in extra_in]
    aliases = {}
    if fill is not None:
        in_specs.append(pl.BlockSpec(memory_space=pl.ANY))
        aliases = {len(operands): fill[1]}
        operands.append(fill[0])
    return pl.pallas_call(
        body, name=name, grid=(gi, gj, gk), in_specs=in_specs + r_in_specs,
        out_specs=[pl.BlockSpec(bs, im) for _, _, bs, im in outs] + r_out_specs,
        out_shape=[SDS(s, d) for s, d, _, _ in outs] + r_outs,
        scratch_shapes=([pltpu.VMEM((tm, tn), F32)] if gk > 1 else []) + r_scratch,
        input_output_aliases=aliases,
        compiler_params=_cp(vmem_mb, dimension_semantics=("arbitrary", "arbitrary", "arbitrary")),
    )(*operands, *r_ops)


def _mm_plain(name, a, b, *, tm, tn, tk, out_dtype=F32, trans_a=False):
    m = a.shape[1] if trans_a else a.shape[0]

    def epilogue(acc, i, j, ins, outs):
        outs[0][...] = acc.astype(out_dtype)

    return _mm_call(name, a, b, tm=tm, tn=tn, tk=tk, epilogue=epilogue, trans_a=trans_a,
                    outs=[((m, b.shape[1]), out_dtype, (tm, tn), lambda i, j, k: (i, j))])[0]


def _accumulate(ref, first, value):
    @pl.when(first)
    def _():
        ref[...] = value

    @pl.when(jnp.logical_not(first))
    def _():
        ref[...] += value


def _rms_bwd(xv, w, dy):
    r = lax.rsqrt(jnp.mean(xv * xv, axis=-1, keepdims=True) + NORM_EPS)
    xn = xv * r
    dxh = dy * w
    dx = r * (dxh - xn * jnp.mean(dxh * xn, axis=-1, keepdims=True))
    return dx, jnp.sum(dy * xn, axis=0, keepdims=True)


def _norm_mm(name, x, wn, w3, *, tm, ride=None):
    t, dm = x.shape
    gj, _, tn = w3.shape
    n = gj * tn
    tm = min(tm, t)
    gi = t // tm
    r_ops, r_in_specs, r_outs, r_out_specs, r_scratch = _ride_parts(ride)

    def body(x_ref, wn_ref, w_ref, *rest):
        r_in, rest = rest[:len(r_ops)], rest[len(r_ops):]
        o_ref, u_ref = rest[:2]
        r_out, sems = rest[2:2 + len(r_outs)], rest[2 + len(r_outs):]
        i, j = pl.program_id(0), pl.program_id(1)
        _ride_run(ride, (i == 0) & (j == 0), (i == gi - 1) & (j == gj - 1), r_in, r_out, sems,
                  middle=(i == (3 * gi) // 4) & (j == 0) if gi > 1 else None)

        @pl.when(j == 0)
        def _():
            xv = x_ref[...]
            r = lax.rsqrt(jnp.mean(xv * xv, axis=-1, keepdims=True) + NORM_EPS)
            u_ref[...] = (xv * r * wn_ref[...]).astype(BF16)

        o_ref[...] = jnp.dot(u_ref[...], w_ref[j], preferred_element_type=F32)

    return pl.pallas_call(
        body, name=name, grid=(gi, gj),
        in_specs=[pl.BlockSpec((tm, dm), lambda i, j: (i, 0)), pl.BlockSpec((1, dm), lambda i, j: (0, 0)),
                  pl.BlockSpec((gj, dm, tn), lambda i, j: (0, 0, 0), pipeline_mode=pl.Buffered(1))] + r_in_specs,
        out_specs=[pl.BlockSpec((tm, tn), lambda i, j: (i, j)), pl.BlockSpec((tm, dm), lambda i, j: (i, 0))] + r_out_specs,
        out_shape=[SDS((t, n), F32), SDS((t, dm), BF16)] + r_outs, scratch_shapes=r_scratch,
        compiler_params=_cp(dimension_semantics=("arbitrary", "arbitrary")),
    )(x, wn, w3, *r_ops)


def _shift_down(tile, halo, s):
    if s == 0:
        return tile
    r = pltpu.roll(tile, s, axis=0)
    h = pltpu.roll(halo, s, axis=0)
    head = jnp.where(_iota(h.shape, 0) < s, h, r[0:8])
    return jnp.concatenate([head, r[8:]], axis=0)


def _shift_up(tile, halo, s):
    if s == 0:
        return tile
    n = tile.shape[0]
    r = pltpu.roll(tile, n - s, axis=0)
    h = pltpu.roll(halo, 8 - s, axis=0)
    tail = jnp.where(_iota(h.shape, 0) >= 8 - s, h, r[n - 8:])
    return jnp.concatenate([r[:n - 8], tail], axis=0)


def _conv_apply(tile, halo, wv, bv, kw):
    acc = bv + wv[kw - 1:kw, :] * tile
    for k in range(kw - 1):
        acc = acc + wv[k:k + 1, :] * _shift_down(tile, halo, kw - 1 - k)
    return acc


def _prev_halo_spec(tm, tc, col0):
    return pl.BlockSpec((8, tc), lambda i, j: (jnp.maximum(i * (tm // 8) - 1, 0), col0 + j))


def _silu_parts(pre):
    sg = _sigmoid(pre)
    return pre * sg, sg * (1.0 + pre * (1.0 - sg))


def _conv_silu_fwd(proj, w, b, *, tm, tc=1536):
    t = proj.shape[0]
    c = w.shape[1]
    tm = min(tm, t)
    col0 = OFF_XBC // tc

    def body(x_ref, h_ref, w_ref, b_ref, o_ref, pre_ref):
        halo = jnp.where(pl.program_id(0) > 0, h_ref[...], 0.0)
        pre = _conv_apply(x_ref[...], halo, w_ref[...], b_ref[...], 4)
        o_ref[...] = _silu_parts(pre)[0]
        pre_ref[...] = pre.astype(BF16)

    tile = pl.BlockSpec((tm, tc), lambda i, j: (i, j))
    return pl.pallas_call(
        body, name="ssd_conv_fwd", grid=(t // tm, c // tc),
        in_specs=[pl.BlockSpec((tm, tc), lambda i, j: (i, col0 + j)), _prev_halo_spec(tm, tc, col0),
                  pl.BlockSpec((4, tc), lambda i, j: (0, j)), pl.BlockSpec((1, tc), lambda i, j: (0, j))],
        out_specs=[tile, tile], out_shape=[SDS((t, c), F32), SDS((t, c), BF16)],
        compiler_params=_cp(dimension_semantics=("arbitrary", "arbitrary")),
    )(proj, proj, w, b)


def _conv_silu_bwd1(d_out, pre, *, tm, tc=1536):
    t, c = pre.shape
    tm = min(tm, t)

    def body(g_ref, p_ref, o_ref, db_ref):
        i = pl.program_id(1)
        d_pre = g_ref[...] * _silu_parts(p_ref[...].astype(F32))[1]
        o_ref[...] = d_pre.astype(BF16)
        _accumulate(db_ref, i == 0, jnp.sum(d_pre, axis=0, keepdims=True))

    tile = pl.BlockSpec((tm, tc), lambda j, i: (i, j))
    return pl.pallas_call(
        body, name="ssd_conv_bwd1", grid=(c // tc, t // tm), in_specs=[tile, tile],
        out_specs=[tile, pl.BlockSpec((1, tc), lambda j, i: (0, j))],
        out_shape=[SDS((t, c), BF16), SDS((1, c), F32)],
        compiler_params=_cp(dimension_semantics=("arbitrary", "arbitrary")),
    )(d_out, pre)


def _conv_bwd2(name, d_pre, src, src_col0, w, *, tm, tc, out_cols, out_col0, fill=None):
    t, c = d_pre.shape
    kw = w.shape[0]
    tm = min(tm, t)
    ni = t // tm
    col0 = src_col0 // tc
    ocol0 = out_col0 // tc

    def body(g_ref, gn_ref, x_ref, w_ref, *rest):
        o_ref, dw_ref = rest[-2:]
        i = pl.program_id(1)
        g = g_ref[...].astype(F32)
        g_next = jnp.where(i < ni - 1, gn_ref[...].astype(F32)[0:8], 0.0)
        xv = x_ref[...]
        wv = w_ref[...]
        shifted = [_shift_up(g, g_next, kw - 1 - k) for k in range(kw)]
        d_in = wv[0:1, :] * shifted[0]
        for k in range(1, kw):
            d_in = d_in + wv[k:k + 1, :] * shifted[k]
        o_ref[...] = d_in.astype(o_ref.dtype)
        rows = [jnp.sum(shifted[k] * xv, axis=0, keepdims=True) for k in range(kw)]

        @pl.when(i == 0)
        def _():
            for k in range(kw):
                dw_ref[k:k + 1, :] = rows[k]

        @pl.when(i > 0)
        def _():
            for k in range(kw):
                dw_ref[k:k + 1, :] += rows[k]

    in_specs = [pl.BlockSpec((tm, tc), lambda j, i: (i, j)),
                pl.BlockSpec((16, tc), lambda j, i: (jnp.minimum((i + 1) * (tm // 16), t // 16 - 1), j)),
                pl.BlockSpec((tm, tc), lambda j, i: (i, col0 + j)),
                pl.BlockSpec((kw, tc), lambda j, i: (0, j))]
    operands = [d_pre, d_pre, src, w]
    if fill is not None:
        in_specs.append(pl.BlockSpec(memory_space=pl.ANY))
        operands.append(fill)
    return pl.pallas_call(
        body, name=name, grid=(c // tc, ni), in_specs=in_specs,
        out_specs=[pl.BlockSpec((tm, tc), lambda j, i: (i, ocol0 + j)), pl.BlockSpec((kw, tc), lambda j, i: (0, j))],
        out_shape=[SDS((t, out_cols), BF16), SDS((kw, c), F32)],
        input_output_aliases={} if fill is None else {4: 0},
        compiler_params=_cp(dimension_semantics=("arbitrary", "arbitrary")),
    )(*operands)


GELU_C = 0.7978845608028654


def _gelu_parts(v):
    inner = GELU_C * (v + 0.044715 * v * v * v)
    th = jnp.tanh(inner)
    val = 0.5 * v * (1.0 + th)
    grad = 0.5 * (1.0 + th) + 0.5 * v * (1.0 - th * th) * GELU_C * (1.0 + 3.0 * 0.044715 * v * v)
    return val, grad


def _ffn_act_fwd(up_raw, w, b, *, tm, tc=1408):
    t = up_raw.shape[0]
    tm = min(tm, t)
    nj = FFN_D_FF // tc
    halo = lambda i: jnp.maximum(i * (tm // 8) - 1, 0)

    def body(g_ref, gh_ref, v_ref, vh_ref, wg_ref, wv_ref, bg_ref, bv_ref, o_ref, gate_ref, val_ref):
        first = pl.program_id(0) > 0
        gate = _conv_apply(g_ref[...], jnp.where(first, gh_ref[...], 0.0), wg_ref[...], bg_ref[...], 3)
        val = _conv_apply(v_ref[...], jnp.where(first, vh_ref[...], 0.0), wv_ref[...], bv_ref[...], 3)
        o_ref[...] = (_gelu_parts(gate)[0] * val).astype(BF16)
        gate_ref[...] = gate.astype(BF16)
        val_ref[...] = val.astype(BF16)

    tile = pl.BlockSpec((tm, tc), lambda i, j: (i, j))
    return pl.pallas_call(
        body, name="ffn_act_fwd", grid=(t // tm, nj),
        in_specs=[tile, pl.BlockSpec((8, tc), lambda i, j: (halo(i), j)),
                  pl.BlockSpec((tm, tc), lambda i, j: (i, nj + j)), pl.BlockSpec((8, tc), lambda i, j: (halo(i), nj + j)),
                  pl.BlockSpec((3, tc), lambda i, j: (0, j)), pl.BlockSpec((3, tc), lambda i, j: (0, nj + j)),
                  pl.BlockSpec((1, tc), lambda i, j: (0, j)), pl.BlockSpec((1, tc), lambda i, j: (0, nj + j))],
        out_specs=[tile] * 3, out_shape=[SDS((t, FFN_D_FF), BF16)] * 3,
        compiler_params=_cp(dimension_semantics=("arbitrary", "arbitrary")),
    )(up_raw, up_raw, up_raw, up_raw, w, w, b, b)


def _ffn_act_bwd(gate, val, d_act, *, tm, tc=1408):
    t = gate.shape[0]
    tm = min(tm, t)
    nj = FFN_D_FF // tc

    def body(g_ref, v_ref, da_ref, dg_ref, dv_ref, dbg_ref, dbv_ref):
        i = pl.program_id(1)
        val = v_ref[...].astype(F32)
        ge, dge = _gelu_parts(g_ref[...].astype(F32))
        da = da_ref[...].astype(F32)
        d_gate = da * val * dge
        d_val = da * ge
        dg_ref[...] = d_gate.astype(BF16)
        dv_ref[...] = d_val.astype(BF16)
        _accumulate(dbg_ref, i == 0, jnp.sum(d_gate, axis=0, keepdims=True))
        _accumulate(dbv_ref, i == 0, jnp.sum(d_val, axis=0, keepdims=True))

    tile = pl.BlockSpec((tm, tc), lambda j, i: (i, j))
    row = pl.BlockSpec((1, tc), lambda j, i: (0, j))
    return pl.pallas_call(
        body, name="ffn_act_bwd", grid=(nj, t // tm), in_specs=[tile] * 3, out_specs=[tile, tile, row, row],
        out_shape=[SDS((t, FFN_D_FF), BF16), SDS((t, FFN_D_FF), BF16), SDS((1, FFN_D_FF), F32), SDS((1, FFN_D_FF), F32)],
        compiler_params=_cp(dimension_semantics=("arbitrary", "arbitrary")),
    )(gate, val, d_act)


def _softplus(v):
    e = jnp.exp(-jnp.abs(v))
    small = e * (1.0 - 0.5 * e)
    return jnp.maximum(v, 0.0) + jnp.where(e < 1e-4, small, jnp.log(1.0 + e))


def _dt_fwd(proj, bias_pad, *, tm):
    t = proj.shape[0]
    tm = min(tm, t)

    def body(x_ref, b_ref, g_ref, gt_ref):
        dt = _softplus(x_ref[...] + b_ref[...])
        first8 = _iota((tm, 128), 1) < 8
        for g in range(SSD_N_GROUPS):
            dg = jnp.where(first8, dt if g == 0 else pltpu.roll(dt, 128 - 8 * g, axis=1), 0.0)
            g_ref[g] = dg
            gt_ref[g] = dg.T[0:8, :]

    return pl.pallas_call(
        body, name="dt_fwd", grid=(t // tm,),
        in_specs=[pl.BlockSpec((tm, 128), lambda i: (i, OFF_DT // 128)), pl.BlockSpec((1, 128), lambda i: (0, 0))],
        out_specs=[pl.BlockSpec((SSD_N_GROUPS, tm, 128), lambda i: (0, i, 0)), pl.BlockSpec((SSD_N_GROUPS, 8, tm), lambda i: (0, 0, i))],
        out_shape=[SDS((SSD_N_GROUPS, t, 128), F32), SDS((SSD_N_GROUPS, 8, t), F32)],
        compiler_params=_cp(dimension_semantics=("arbitrary",)),
    )(proj, bias_pad)


def _dt_bwd(d_dtg, proj, bias_pad, d_proj, *, tm):
    t = proj.shape[0]
    tm = min(tm, t)

    def body(g_ref, x_ref, b_ref, _, o_ref, db_ref):
        first8 = _iota((tm, 128), 1) < 8
        d_dt = jnp.where(first8, g_ref[0], 0.0)
        for g in range(1, SSD_N_GROUPS):
            d_dt = d_dt + pltpu.roll(jnp.where(first8, g_ref[g], 0.0), 8 * g, axis=1)
        d_raw = d_dt * _sigmoid(x_ref[...] + b_ref[...])
        o_ref[:, 0:128] = d_raw.astype(BF16)
        o_ref[:, 128:512] = jnp.zeros((tm, 384), BF16)
        _accumulate(db_ref, pl.program_id(0) == 0, jnp.sum(d_raw, axis=0, keepdims=True))

    return pl.pallas_call(
        body, name="dt_bwd", grid=(t // tm,),
        in_specs=[pl.BlockSpec((SSD_N_GROUPS, tm, 128), lambda i: (0, i, 0)), pl.BlockSpec((tm, 128), lambda i: (i, OFF_DT // 128)),
                  pl.BlockSpec((1, 128), lambda i: (0, 0)), pl.BlockSpec(memory_space=pl.ANY)],
        out_specs=[pl.BlockSpec((tm, 512), lambda i: (i, OFF_DT // 512)), pl.BlockSpec((1, 128), lambda i: (0, 0))],
        out_shape=[SDS((t, PROJ_W), BF16), SDS((1, 128), F32)],
        input_output_aliases={3: 0},
        compiler_params=_cp(dimension_semantics=("arbitrary",)),
    )(d_dtg, proj, bias_pad, d_proj)


def _split3(v):
    hi = v.astype(BF16)
    r1 = v - hi.astype(F32)
    mid = r1.astype(BF16)
    return hi, mid, (r1 - mid.astype(F32)).astype(BF16)


def _times01(v, m3):
    return jnp.dot(jnp.concatenate(_split3(v), axis=1), m3, preferred_element_type=F32)


def _01times(m3, v):
    return jnp.dot(m3, jnp.concatenate(_split3(v), axis=0), preferred_element_type=F32)


def _ssd_decay(dt_ref, dtT_ref, al_ref, alT_ref, k):
    dt = dt_ref[0]
    a_row = -jnp.exp(al_ref[0])
    adt_t = dtT_ref[0] * (-jnp.exp(alT_ref[0]))
    return dt, a_row, _01times(k['low3'][...], dt * a_row), _times01(adt_t, k['up3v'][...])


def _ssd_specs(nc, rev):
    ci = (lambda c: nc - 1 - c) if rev else (lambda c: c)
    return [pl.BlockSpec((CHUNK, SSD_CONV_DIM), lambda c: (ci(c), 0)),
            pl.BlockSpec((SSD_N_GROUPS, CHUNK, 128), lambda c: (0, ci(c), 0)),
            pl.BlockSpec((SSD_N_GROUPS, 8, CHUNK), lambda c: (0, 0, ci(c))),
            pl.BlockSpec((SSD_N_GROUPS, 1, 128), lambda c: (0, 0, 0)),
            pl.BlockSpec((SSD_N_GROUPS, 8, 1), lambda c: (0, 0, 0)),
            pl.BlockSpec((1, SSD_D_INNER), lambda c: (0, 0))]


def _ssd_group_views(g, x_ref, dt_ref, dtT_ref, al_ref, alT_ref, d_ref):
    return (x_ref.at[:, g * GROUP_W:(g + 1) * GROUP_W], dt_ref.at[g:g + 1], dtT_ref.at[g:g + 1], al_ref.at[g:g + 1],
            alT_ref.at[g:g + 1], d_ref.at[:, g * 512:(g + 1) * 512])


NT = (((1,), (1,)), ((), ()))
WIDE = 8 * CHUNK
SSD_CONST_NAMES = ('e128', 'e64', 's64', 'mlo', 'mup', 'low3', 'up3', 'up3v')
SSD_CONST_SHAPES = [pltpu.VMEM((3 * CHUNK, WIDE), BF16), pltpu.VMEM((3 * CHUNK, 512), BF16), pltpu.VMEM((512, CHUNK), BF16),
                    pltpu.VMEM((CHUNK, WIDE), F32), pltpu.VMEM((CHUNK, WIDE), F32), pltpu.VMEM((CHUNK, 3 * CHUNK), BF16),
                    pltpu.VMEM((CHUNK, 3 * CHUNK), BF16), pltpu.VMEM((3 * CHUNK, CHUNK), BF16)]


def _ssd_init_consts(k):
    row, col = _iota((3 * CHUNK, WIDE), 0), _iota((3 * CHUNK, WIDE), 1)
    k['e128'][...] = ((col >> 7) == (row & 127)).astype(BF16)
    k['e64'][...] = ((_iota((3 * CHUNK, 512), 1) >> 6) == (_iota((3 * CHUNK, 512), 0) & 127)).astype(BF16)
    k['s64'][...] = ((_iota((512, CHUNK), 0) >> 6) == _iota((512, CHUNK), 1)).astype(BF16)
    row, col = _iota((CHUNK, WIDE), 0), _iota((CHUNK, WIDE), 1)
    k['mlo'][...] = (row >= (col & 127)).astype(F32)
    k['mup'][...] = (row <= (col & 127)).astype(F32)
    row, col = _iota((CHUNK, 3 * CHUNK), 0), _iota((CHUNK, 3 * CHUNK), 1) & 127
    k['low3'][...] = (row >= col).astype(BF16)
    k['up3'][...] = (row <= col).astype(BF16)
    row, col = _iota((3 * CHUNK, CHUNK), 0) & 127, _iota((3 * CHUNK, CHUNK), 1)
    k['up3v'][...] = (row <= col).astype(BF16)


def _ssd_common(x_ref, dt_ref, dtT_ref, al_ref, alT_ref, k):
    dt, a_row, acs, acs_t = _ssd_decay(dt_ref, dtT_ref, al_ref, alT_ref, k)
    ecol = _times01(acs, k['e128'][...])
    rrow = jnp.concatenate([jnp.broadcast_to(acs_t[j:j + 1, :], (CHUNK, CHUNK)) for j in range(8)], axis=1)
    a64 = _times01(acs, k['e64'][...])
    dt64 = _times01(dt, k['e64'][...])
    a_end64 = a64[CHUNK - 1:CHUNK, :]
    xs = x_ref[:, 0:512]
    return dict(dt=dt, a_row=a_row, acs=acs, seg=ecol - rrow, dt64=dt64, e_a=jnp.exp(a64), decay=jnp.exp(a_end64 - a64),
                e_end64=jnp.exp(a_end64), xs=xs, xdt=xs * dt64, bm=x_ref[:, 512:640], cm=x_ref[:, 640:768])


def _pair_blocks(v):
    lo = _iota((CHUNK, 128), 1) < 64
    out = []
    for i in range(4):
        ch = v[:, i * 128:(i + 1) * 128]
        out.append(jnp.concatenate([jnp.where(lo, ch, 0.0), jnp.where(lo, 0.0, ch)], axis=0).astype(BF16))
    return out


def _tile8(m):
    return jnp.concatenate([m] * 8, axis=1)


def _ssd_fwd(xc, dtg, dtg_t, alog, alog_t, d_exp):
    t = xc.shape[0]
    nc = t // CHUNK

    def body(xa_ref, dta_ref, dtTa_ref, ala_ref, alTa_ref, da_ref, ya_ref, hs_ref, h_scr, *consts):
        c = pl.program_id(0)
        k = dict(zip(SSD_CONST_NAMES, consts))

        @pl.when(c == 0)
        def _():
            _ssd_init_consts(k)
            h_scr[...] = jnp.zeros_like(h_scr)

        for g in range(SSD_N_GROUPS):
            x_ref, dt_ref, dtT_ref, al_ref, alT_ref, d_ref = _ssd_group_views(g, xa_ref, dta_ref, dtTa_ref, ala_ref, alTa_ref, da_ref)
            v = _ssd_common(x_ref, dt_ref, dtT_ref, al_ref, alT_ref, k)
            b16, c16 = v['bm'].astype(BF16), v['cm'].astype(BF16)
            cb = lax.dot_general(c16, b16, NT, preferred_element_type=F32)
            m16 = (jnp.exp(jnp.minimum(v['seg'], 0.0)) * k['mlo'][...] * _tile8(cb)).astype(BF16)
            xbd = _pair_blocks(v['xdt'])
            y_diag = jnp.concatenate([jnp.dot(m16[:, i * 256:(i + 1) * 256], xbd[i], preferred_element_type=F32)
                                      for i in range(4)], axis=1)
            ht = h_scr[g]
            y_off = jnp.dot(c16, ht.astype(BF16), preferred_element_type=F32)
            ya_ref[:, g * 512:(g + 1) * 512] = y_diag + v['e_a'] * y_off + d_ref[...] * v['xs']
            st = jnp.dot(v['bm'].T.astype(BF16), (v['xdt'] * v['decay']).astype(BF16), preferred_element_type=F32)
            hs_ref[0, g] = ht
            h_scr[g] = ht * v['e_end64'] + st

    return pl.pallas_call(
        body, name="ssd_fwd", grid=(nc,), in_specs=_ssd_specs(nc, False),
        out_specs=[pl.BlockSpec((CHUNK, SSD_D_INNER), lambda c: (c, 0)),
                   pl.BlockSpec((1, SSD_N_GROUPS, SSD_D_STATE, 512), lambda c: (c, 0, 0, 0))],
        out_shape=[SDS((t, SSD_D_INNER), F32), SDS((nc, SSD_N_GROUPS, SSD_D_STATE, 512), F32)],
        scratch_shapes=[pltpu.VMEM((SSD_N_GROUPS, SSD_D_STATE, 512), F32)] + SSD_CONST_SHAPES,
        compiler_params=_cp(dimension_semantics=("arbitrary",)),
    )(xc, dtg, dtg_t, alog, alog_t, d_exp)


def _ssd_bwd(xc, dtg, dtg_t, alog, alog_t, d_exp, d_y, hs, ride=None):
    t = xc.shape[0]
    nc = t // CHUNK

    r_ops, r_in_specs, r_outs, r_out_specs, r_scratch = _ride_parts(ride)

    def body(xa_ref, dta_ref, dtTa_ref, ala_ref, alTa_ref, da_ref, dya_ref, hs_ref, *rest):
        r_in, rest = rest[:len(r_ops)], rest[len(r_ops):]
        dxa_ref, ddta_ref, dal_ref, dd_ref = rest[:4]
        r_out, rest = rest[4:4 + len(r_outs)], rest[4 + len(r_outs):]
        g_scr, consts, sems = rest[0], rest[1:1 + len(SSD_CONST_NAMES)], rest[1 + len(SSD_CONST_NAMES):]
        c = pl.program_id(0)
        _ride_run(ride, c == 0, c == nc - 1, r_in, r_out, sems)
        k = dict(zip(SSD_CONST_NAMES, consts))

        @pl.when(c == 0)
        def _():
            _ssd_init_consts(k)
            g_scr[...] = jnp.zeros_like(g_scr)

        for g in range(SSD_N_GROUPS):
            views = _ssd_group_views(g, xa_ref, dta_ref, dtTa_ref, ala_ref, alTa_ref, da_ref)
            one_group(c, g, k, *views, dya_ref.at[:, g * 512:(g + 1) * 512], hs_ref, g_scr,
                      dxa_ref.at[:, g * GROUP_W:(g + 1) * GROUP_W], ddta_ref.at[g:g + 1], dal_ref, dd_ref)

    def one_group(c, g, k, x_ref, dt_ref, dtT_ref, al_ref, alT_ref, d_ref, dy_ref, hs_ref, g_scr, dx_ref, ddt_ref,
                  dal_ref, dd_ref):
        s64, mlo, mup = k['s64'], k['mlo'], k['mup']
        v = _ssd_common(x_ref, dt_ref, dtT_ref, al_ref, alT_ref, k)
        dt, a_row, xs, xdt, e_a, decay = v['dt'], v['a_row'], v['xs'], v['xdt'], v['e_a'], v['decay']
        row, col = _iota((CHUNK, CHUNK), 0), _iota((CHUNK, CHUNK), 1)
        b16, c16 = v['bm'].astype(BF16), v['cm'].astype(BF16)
        ct16 = v['cm'].T.astype(BF16)
        cb = lax.dot_general(c16, b16, NT, preferred_element_type=F32)
        cbt = lax.dot_general(b16, c16, NT, preferred_element_type=F32)
        lmat = jnp.exp(jnp.minimum(v['seg'], 0.0)) * mlo[...]
        lmat_t = jnp.exp(jnp.minimum(-v['seg'], 0.0)) * mup[...]
        mmat, mmat_t = lmat * _tile8(cb), lmat_t * _tile8(cbt)
        mt16 = mmat_t.astype(BF16)
        dy = dy_ref[...]
        dye, xdec = dy * e_a, xdt * decay
        dy16, dye16, xdec16 = dy.astype(BF16), dye.astype(BF16), xdec.astype(BF16)
        xdt16 = xdt.astype(BF16)
        ht, gt = hs_ref[0, g], g_scr[g]
        ht16, gt16 = ht.astype(BF16), gt.astype(BF16)
        xbd, dybd = _pair_blocks(xdt), _pair_blocks(dy)
        d_m, d_mt, d_x = [], [], []
        for i in range(4):
            csl = slice(i * 128, (i + 1) * 128)
            d_m.append(lax.dot_general(dy16[:, csl], xbd[i], NT, preferred_element_type=F32))
            d_mt.append(lax.dot_general(xdt16[:, csl], dybd[i], NT, preferred_element_type=F32))
            d_x.append(jnp.dot(mt16[:, i * 256:(i + 1) * 256], dybd[i], preferred_element_type=F32))
        d_m, d_mt, d_x = jnp.concatenate(d_m, axis=1), jnp.concatenate(d_mt, axis=1), jnp.concatenate(d_x, axis=1)

        def head_sum(m):
            acc = m[:, 0:CHUNK]
            for j in range(1, 8):
                acc = acc + m[:, j * CHUNK:(j + 1) * CHUNK]
            return acc

        def seg64(p):
            return jnp.dot(p.astype(BF16), s64[...], preferred_element_type=F32)

        d_cb16 = head_sum(d_m * lmat).astype(BF16)
        d_cbt16 = head_sum(d_mt * lmat_t).astype(BF16)
        dseg = d_m * mmat - d_mt * mmat_t
        da_seg = jnp.zeros((CHUNK, CHUNK), F32)
        for j in range(8):
            da_seg = jnp.where(col == j, jnp.sum(dseg[:, j * CHUNK:(j + 1) * CHUNK], axis=1, keepdims=True), da_seg)
        ch = jnp.dot(c16, ht16, preferred_element_type=F32)
        bg = jnp.dot(b16, gt16, preferred_element_type=F32)
        d_x = d_x + decay * bg
        d_decay = seg64(xdec * bg)
        e_end = jnp.exp(v['acs'][CHUNK - 1:CHUNK, :])
        d_end = e_end * jnp.sum(seg64(gt * ht), axis=0, keepdims=True) + jnp.sum(d_decay, axis=0, keepdims=True)
        d_a = seg64(dye * ch) - d_decay + da_seg + jnp.where(row == CHUNK - 1, d_end, 0.0)
        dx_ref[:, 0:512] = d_x * v['dt64'] + d_ref[...] * dy
        dx_ref[:, 640:768] = (lax.dot_general(dye16, ht16, NT, preferred_element_type=F32)
                              + jnp.dot(d_cb16, b16, preferred_element_type=F32))
        dx_ref[:, 512:640] = (lax.dot_general(xdec16, gt16, NT, preferred_element_type=F32)
                              + jnp.dot(d_cbt16, c16, preferred_element_type=F32))
        g_scr[g] = gt * v['e_end64'] + jnp.dot(ct16, dye16, preferred_element_type=F32)
        d_adt = _01times(k['up3'][...], d_a)
        ddt_ref[0] = d_adt * a_row + seg64(d_x * xs)
        d_alog = jnp.sum(d_adt * dt, axis=0, keepdims=True) * a_row
        dd_row = jnp.sum(seg64(dy * xs), axis=0, keepdims=True)
        first = c == 0

        @pl.when(first)
        def _():
            dal_ref[g] = d_alog
            dd_ref[g] = dd_row

        @pl.when(jnp.logical_not(first))
        def _():
            dal_ref[g] += d_alog
            dd_ref[g] += dd_row

    rc = lambda c: nc - 1 - c
    whole = pl.BlockSpec((SSD_N_GROUPS, 1, 128), lambda c: (0, 0, 0))
    return pl.pallas_call(
        body, name="ssd_bwd", grid=(nc,),
        in_specs=_ssd_specs(nc, True) + [pl.BlockSpec((CHUNK, SSD_D_INNER), lambda c: (rc(c), 0)),
                                        pl.BlockSpec((1, SSD_N_GROUPS, SSD_D_STATE, 512), lambda c: (rc(c), 0, 0, 0))] + r_in_specs,
        out_specs=[pl.BlockSpec((CHUNK, SSD_CONV_DIM), lambda c: (rc(c), 0)),
                   pl.BlockSpec((SSD_N_GROUPS, CHUNK, 128), lambda c: (0, rc(c), 0)), whole, whole] + r_out_specs,
        out_shape=[SDS((t, SSD_CONV_DIM), F32), SDS((SSD_N_GROUPS, t, 128), F32),
                   SDS((SSD_N_GROUPS, 1, 128), F32), SDS((SSD_N_GROUPS, 1, 128), F32)] + r_outs,
        scratch_shapes=[pltpu.VMEM((SSD_N_GROUPS, SSD_D_STATE, 512), F32)] + SSD_CONST_SHAPES + r_scratch,
        compiler_params=_cp(dimension_semantics=("arbitrary",)),
    )(xc, dtg, dtg_t, alog, alog_t, d_exp, d_y, hs, *r_ops)


def _gated_norm_fwd(y, proj, w, *, tm):
    t = y.shape[0]
    tm = min(tm, t)

    def body(y_ref, z_ref, w_ref, o_ref):
        gv = y_ref[...] * _silu_parts(z_ref[...])[0]
        r = lax.rsqrt(jnp.mean(gv * gv, axis=-1, keepdims=True) + NORM_EPS)
        o_ref[...] = (gv * r * w_ref[...]).astype(BF16)

    tile = pl.BlockSpec((tm, 512), lambda i, g: (i, g))
    return pl.pallas_call(
        body, name="gated_norm_fwd", grid=(t // tm, SSD_N_GROUPS),
        in_specs=[tile, pl.BlockSpec((tm, 512), lambda i, g: (i, OFF_Z // 512 + g)),
                  pl.BlockSpec((1, 512), lambda i, g: (0, g))], out_specs=tile,
        out_shape=SDS((t, SSD_D_INNER), BF16),
        compiler_params=_cp(dimension_semantics=("arbitrary", "arbitrary")),
    )(y, proj, w)


def _rope(ch, cos_t, sin_t):
    first = (_iota(ch.shape, 1) & 32) == 0
    partner = jnp.where(first, pltpu.roll(ch, 96, axis=1), pltpu.roll(ch, 32, axis=1))
    return ch * cos_t + partner * sin_t


def _rope_qkv(proj, cos_t, sin_t, *, tm):
    t = proj.shape[0]
    tm = min(tm, t)

    def body(q_ref, k_ref, v_ref, c_ref, s_ref, qr_ref, kp_ref, vp_ref, kt_ref, vt_ref):
        cv, sv = c_ref[...], s_ref[...]
        lo = _iota((tm, 128), 1) < 64
        for m in range(8):
            sl = slice(m * 128, (m + 1) * 128)
            qr_ref[:, sl] = (_rope(q_ref[:, sl], cv, sv) * 0.125).astype(BF16)
        for m2 in range(2):
            sl = slice(m2 * 128, (m2 + 1) * 128)
            for src, dst, dst_t in ((_rope(k_ref[:, sl], cv, sv), kp_ref, kt_ref), (v_ref[:, sl], vp_ref, vt_ref)):
                sw = pltpu.roll(src, 64, axis=1)
                padded = (jnp.where(lo, src, 0.0), jnp.where(lo, 0.0, sw), jnp.where(lo, sw, 0.0), jnp.where(lo, 0.0, src))
                for i, pad in enumerate(padded):
                    rows = slice((4 * m2 + i) * 128, (4 * m2 + i + 1) * 128)
                    dst[:, rows] = pad.astype(BF16)
                    dst_t[rows, :] = pad.T.astype(BF16)

    return pl.pallas_call(
        body, name="rope_qkv", grid=(t // tm,),
        in_specs=[pl.BlockSpec((tm, 1024), lambda i: (i, OFF_Q // 1024)), pl.BlockSpec((tm, 256), lambda i: (i, OFF_K // 256)),
                  pl.BlockSpec((tm, 256), lambda i: (i, OFF_V // 256)), pl.BlockSpec((tm, 128), lambda i: (i, 0)),
                  pl.BlockSpec((tm, 128), lambda i: (i, 0))],
        out_specs=[pl.BlockSpec((tm, 1024), lambda i: (i, 0))] * 3 + [pl.BlockSpec((1024, tm), lambda i: (0, i))] * 2,
        out_shape=[SDS((t, 1024), BF16)] * 3 + [SDS((1024, t), BF16)] * 2,
        compiler_params=_cp(dimension_semantics=("arbitrary",)),
    )(proj, proj, proj, cos_t, sin_t)


def _attn_valid(n):
    kj, qi = _iota((2 * CHUNK, CHUNK), 0), _iota((2 * CHUNK, CHUNK), 1)
    return (kj > qi) & (kj <= qi + CHUNK) & ((n > 0) | (kj >= CHUNK))


def _attn_fwd(qr, kp, vt, sinks):
    t = qr.shape[0]
    nb = t // CHUNK

    def body(q_ref, kc_ref, kprev_ref, vc_ref, vprev_ref, sk_ref, o_ref, lse_ref):
        n = pl.program_id(0)
        valid = _attn_valid(n)
        head_row = _iota((16, CHUNK), 0)
        lse_all = jnp.zeros((16, CHUNK), F32)
        for m in range(8):
            g = m // 2
            qch = q_ref[:, m * 128:(m + 1) * 128]
            sls = [slice((2 * g + e) * 128, (2 * g + e + 1) * 128) for e in range(2)]
            kk2 = jnp.concatenate([r[:, sl] for sl in sls for r in (kprev_ref, kc_ref)], axis=0)
            vv2_t = jnp.concatenate([r[sl, :] for sl in sls for r in (vprev_ref, vc_ref)], axis=1)
            s2 = lax.dot_general(kk2, qch, NT, preferred_element_type=F32)
            probs = []
            for e in range(2):
                h = 2 * m + e
                s = jnp.where(valid, s2[2 * CHUNK * e:2 * CHUNK * (e + 1)], NEG)
                sink = sk_ref[0:1, h:h + 1]
                mx = jnp.maximum(jnp.max(s, axis=0, keepdims=True), sink)
                p = jnp.exp(s - mx)
                den = jnp.sum(p, axis=0, keepdims=True) + jnp.exp(sink - mx)
                probs.append((p * (1.0 / den)).astype(BF16))
                lse_all = jnp.where(head_row == h, mx + jnp.log(den), lse_all)
            o_t = jnp.dot(vv2_t, jnp.concatenate(probs, axis=0), preferred_element_type=F32)
            o_ref[:, m * 128:(m + 1) * 128] = o_t.T.astype(BF16)
        lse_ref[0] = lse_all

    cur = pl.BlockSpec((CHUNK, 1024), lambda n: (n, 0))
    prev = pl.BlockSpec((CHUNK, 1024), lambda n: (jnp.maximum(n - 1, 0), 0))
    cur_t = pl.BlockSpec((1024, CHUNK), lambda n: (0, n))
    prev_t = pl.BlockSpec((1024, CHUNK), lambda n: (0, jnp.maximum(n - 1, 0)))
    return pl.pallas_call(
        body, name="attn_fwd", grid=(nb,),
        in_specs=[cur, cur, prev, cur_t, prev_t, pl.BlockSpec((1, 128), lambda n: (0, 0))],
        out_specs=[cur, pl.BlockSpec((1, 16, CHUNK), lambda n: (n, 0, 0))],
        out_shape=[SDS((t, 1024), BF16), SDS((nb, 16, CHUNK), F32)],
        compiler_params=_cp(dimension_semantics=("arbitrary",)),
    )(qr, kp, kp, vt, vt, sinks)


def _attn_bwd(qr, kp, vp, kt, d_o, o, lse, sinks, cos_t, sin_t, d_proj, ride=None):
    t = qr.shape[0]
    nb = t // CHUNK

    r_ops, r_in_specs, r_outs, r_out_specs, r_scratch = _ride_parts(ride)

    def body(q_ref, kc_ref, kprev_ref, vc_ref, vprev_ref, ktc_ref, ktprev_ref, do_ref, o_ref, lse_ref, sk_ref,
             c_ref, s_ref, cp_ref, sp_ref, _, *rest):
        r_in, rest = rest[:len(r_ops)], rest[len(r_ops):]
        dqkv_ref, dsk_ref = rest[:2]
        r_out, rest = rest[2:2 + len(r_outs)], rest[2 + len(r_outs):]
        acc_k, acc_v, dq_scr = rest[:3]
        n = pl.program_id(0)
        _ride_run(ride, n == 0, n == nb, r_in, r_out, rest[3:])
        lane = _iota((CHUNK, 128), 1)
        lo = lane < 64
        lane1 = _iota((1, 128), 1)

        @pl.when(n == 0)
        def _():
            acc_k[...] = jnp.zeros_like(acc_k)
            acc_v[...] = jnp.zeros_like(acc_v)
            dsk_ref[...] = jnp.zeros((1, 128), F32)

        @pl.when(n > 0)
        def _():
            dqkv_ref[:, 0:1024] = dq_scr[...]
            for r in range(8):
                acc_k[r, 0:CHUNK] = acc_k[r, CHUNK:2 * CHUNK]
                acc_v[r, 0:CHUNK] = acc_v[r, CHUNK:2 * CHUNK]
                acc_k[r, CHUNK:2 * CHUNK] = jnp.zeros((CHUNK, 128), F32)
                acc_v[r, CHUNK:2 * CHUNK] = jnp.zeros((CHUNK, 128), F32)

        @pl.when(n < nb)
        def _():
            valid = _attn_valid(n)
            lse_all = lse_ref[0]
            dsk = jnp.zeros((1, 128), F32)
            for m in range(8):
                g = m // 2
                csl = slice(m * 128, (m + 1) * 128)
                qch = q_ref[:, csl]
                doch = do_ref[:, csl]
                prod_t = (doch.astype(F32) * o_ref[:, csl].astype(F32)).T
                sls = [slice((2 * g + e) * 128, (2 * g + e + 1) * 128) for e in range(2)]
                kk2 = jnp.concatenate([r[:, sl] for sl in sls for r in (kprev_ref, kc_ref)], axis=0)
                vv2 = jnp.concatenate([r[:, sl] for sl in sls for r in (vprev_ref, vc_ref)], axis=0)
                kk2_t = jnp.concatenate([r[sl, :] for sl in sls for r in (ktprev_ref, ktc_ref)], axis=1)
                s2 = lax.dot_general(kk2, qch, NT, preferred_element_type=F32)
                d_p2 = lax.dot_general(vv2, doch, NT, preferred_element_type=F32)
                ps, d_ss = [], []
                for e in range(2):
                    h = 2 * m + e
                    rows = slice(2 * CHUNK * e, 2 * CHUNK * (e + 1))
                    lse_h = lse_all[h:h + 1, :]
                    p = jnp.exp(jnp.where(valid, s2[rows], NEG) - lse_h)
                    delta = jnp.sum(prod_t[64 * e:64 * (e + 1)], axis=0, keepdims=True)
                    ps.append(p.astype(BF16))
                    d_ss.append((p * (d_p2[rows] - delta)).astype(BF16))
                    p_sink = jnp.exp(sk_ref[0:1, h:h + 1] - lse_h)
                    dsk = jnp.where(lane1 == h, -jnp.sum(p_sink * delta), dsk)
                d_s2, p2 = jnp.concatenate(d_ss, axis=0), jnp.concatenate(ps, axis=0)
                d_k2 = jnp.dot(d_s2, qch, preferred_element_type=F32)
                d_v2 = jnp.dot(p2, doch, preferred_element_type=F32)
                for e in range(2):
                    rows = slice(2 * CHUNK * e, 2 * CHUNK * (e + 1))
                    acc_k[2 * g + e] += d_k2[rows]
                    acc_v[2 * g + e] += d_v2[rows]
                dq_t = jnp.dot(kk2_t, d_s2, preferred_element_type=F32)
                dq_scr[:, csl] = (_rope(dq_t.T, c_ref[...], -s_ref[...]) * 0.125).astype(BF16)
            dsk_ref[...] += dsk

        @pl.when(n > 0)
        def _():
            for m2 in range(2):
                halves = []
                for g in (2 * m2, 2 * m2 + 1):
                    for acc in (acc_k, acc_v):
                        comb = jnp.where(lo, acc[2 * g, 0:CHUNK], acc[2 * g + 1, 0:CHUNK])
                        halves.append(comb + pltpu.roll(comb, 64, axis=1))
                d_kr = jnp.where(lo, halves[0], halves[2])
                d_v = jnp.where(lo, halves[1], halves[3])
                dqkv_ref[:, OFF_K + m2 * 128:OFF_K + (m2 + 1) * 128] = _rope(d_kr, cp_ref[...], -sp_ref[...]).astype(BF16)
                dqkv_ref[:, OFF_V + m2 * 128:OFF_V + (m2 + 1) * 128] = d_v.astype(BF16)

    qn = lambda n: jnp.minimum(n, nb - 1)
    pn = lambda n: jnp.maximum(jnp.minimum(n, nb) - 1, 0)
    cur = pl.BlockSpec((CHUNK, 1024), lambda n: (qn(n), 0))
    prev = pl.BlockSpec((CHUNK, 1024), lambda n: (pn(n), 0))
    cur128 = pl.BlockSpec((CHUNK, 128), lambda n: (qn(n), 0))
    prev128 = pl.BlockSpec((CHUNK, 128), lambda n: (pn(n), 0))
    cur_t = pl.BlockSpec((1024, CHUNK), lambda n: (0, qn(n)))
    prev_t = pl.BlockSpec((1024, CHUNK), lambda n: (0, pn(n)))
    one = pl.BlockSpec((1, 128), lambda n: (0, 0))
    return pl.pallas_call(
        body, name="attn_bwd", grid=(nb + 1,),
        in_specs=[cur, cur, prev, cur, prev, cur_t, prev_t, cur, cur, pl.BlockSpec((1, 16, CHUNK), lambda n: (qn(n), 0, 0)),
                  one, cur128, cur128, prev128, prev128, pl.BlockSpec(memory_space=pl.ANY)] + r_in_specs,
        out_specs=[pl.BlockSpec((CHUNK, 1536), lambda n: (pn(n), 0)), one] + r_out_specs,
        out_shape=[SDS((t, PROJ_W), BF16), SDS((1, 128), F32)] + r_outs,
        scratch_shapes=[pltpu.VMEM((8, 2 * CHUNK, 128), F32), pltpu.VMEM((8, 2 * CHUNK, 128), F32),
                        pltpu.VMEM((CHUNK, 1024), BF16)] + r_scratch,
        input_output_aliases={15: 0},
        compiler_params=_cp(dimension_semantics=("arbitrary",)),
    )(qr, kp, kp, vp, vp, kt, kt, d_o, o, lse, sinks, cos_t, sin_t, cos_t, sin_t, d_proj, *r_ops)


def _adamw(name, w, g, m, v, *, tr):
    rows, cols = w.shape
    tr = min(tr, rows)
    assert rows % tr == 0

    def body(w_ref, g_ref, m_ref, v_ref, d_ref, nm_ref, nv_ref):
        gv = g_ref[...]
        nm = ADAM_B1 * m_ref[...] + (1.0 - ADAM_B1) * gv
        nv = ADAM_B2 * v_ref[...] + (1.0 - ADAM_B2) * (gv * gv)
        m_hat = nm / (1.0 - ADAM_B1 ** ADAM_STEP)
        v_hat = nv / (1.0 - ADAM_B2 ** ADAM_STEP)
        d_ref[...] = -ADAM_LR * (m_hat / (jnp.sqrt(v_hat) + ADAM_EPS) + ADAM_WD * w_ref[...])
        nm_ref[...] = nm
        nv_ref[...] = nv

    tile = pl.BlockSpec((tr, cols), lambda i: (i, 0))
    return pl.pallas_call(
        body, name=name, grid=(rows // tr,), in_specs=[tile] * 4, out_specs=[tile] * 3,
        out_shape=[SDS((rows, cols), F32)] * 3, compiler_params=_cp(dimension_semantics=("arbitrary",)),
    )(w, g, m, v)


def _local_step(x, cos_t, sin_t, tgt, wb, ps, late=None, rides=None):
    t = x.shape[0]
    tm = min(512, t)
    tmw = min(1024, t)
    ij = lambda i, j, k: (i, j)
    i0 = lambda i, j, k: (i, 0)
    c0 = lambda i, j, k: (0, 0)
    cj = lambda i, j, k: (0, j)
    rides = rides or (lambda group, grads: None)
    rode = {}

    tkt = min(2048, t)
    proj, u, *arrived = _norm_mm("in_proj", x, ps['norm_mix_pre_w'], wb['cat'], tm=tmw, ride=late[0] if late else None)
    if late:
        more_wb, more_ps = late[1](arrived)
        wb, ps = {**wb, **more_wb}, {**ps, **more_ps}
    xc, xc_pre = _conv_silu_fwd(proj, ps['ssd_conv_w'], ps['ssd_conv_b'], tm=tm)
    bias_pad = jnp.pad(ps['ssd_dt_bias'], ((0, 0), (0, 96)))
    dtg, dtg_t = _dt_fwd(proj, bias_pad, tm=tmw)
    alog = jnp.pad(ps['ssd_a_log'].reshape(SSD_N_GROUPS, 1, 8), ((0, 0), (0, 0), (0, 120)))
    alog_t = ps['ssd_a_log'].reshape(SSD_N_GROUPS, 8, 1)
    d_exp = jnp.repeat(ps['ssd_d'], SSD_HEAD_DIM, axis=1)
    y, hs = _ssd_fwd(xc, dtg, dtg_t, alog, alog_t, d_exp)
    gn = _gated_norm_fwd(y, proj, ps['ssd_norm_w'], tm=tmw)
    qr, kp, vp, kt, vt = _rope_qkv(proj, cos_t, sin_t, tm=tm)
    sinks = jnp.pad(ps['attn_sinks'], ((0, 0), (0, 112)))
    ao, lse = _attn_fwd(qr, kp, vt, sinks)
    y_attn = _mm_plain("attn_out", ao, wb['ao'], tm=tmw, tn=512, tk=1024)

    def merge_ep(acc, i, j, ins, outs):
        gs, ga, ya = ins
        outs[0][...] = (_sigmoid(gs[...]) * acc + _sigmoid(ga[...]) * ya[...]).astype(BF16)
        outs[1][...] = acc

    merged, y_ssd = _mm_call(
        "ssd_out_merge", gn, wb['so'], tm=tmw, tn=512, tk=2048, epilogue=merge_ep,
        extra_in=[(proj, (tmw, 512), lambda i, j, k: (i, OFF_GS // 512 + j)),
                  (proj, (tmw, 512), lambda i, j, k: (i, OFF_GA // 512 + j)), (y_attn, (tmw, 512), ij)],
        outs=[((t, D_MODEL), BF16, (tmw, 512), ij), ((t, D_MODEL), F32, (tmw, 512), ij)])

    def mix_ep(acc, i, j, ins, outs):
        xv, wn = ins
        r = lax.rsqrt(jnp.mean(acc * acc, axis=-1, keepdims=True) + NORM_EPS)
        outs[0][...] = xv[...] + acc * r * wn[...]
        outs[1][...] = acc

    x1, mmix = _mm_call(
        "mix_out", merged, wb['mix'], tm=tm, tn=D_MODEL, tk=1024, epilogue=mix_ep,
        extra_in=[(x, (tm, D_MODEL), i0), (ps['norm_mix_post_w'], (1, D_MODEL), c0)],
        outs=[((t, D_MODEL), F32, (tm, D_MODEL), i0), ((t, D_MODEL), F32, (tm, D_MODEL), i0)])

    up_raw, h = _norm_mm("ffn_up", x1, ps['norm_ffn_pre_w'], wb['up'], tm=tmw)
    act, ffn_gate, ffn_val = _ffn_act_fwd(up_raw, ps['ffn_conv_w'], ps['ffn_conv_b'], tm=tm)

    def loss_ep(acc, i, j, ins, outs):
        x1v, tg, wn = ins
        d_ff_ref, dout_ref, loss_ref, dw_ref = outs
        wv = wn[...]
        r = lax.rsqrt(jnp.mean(acc * acc, axis=-1, keepdims=True) + NORM_EPS)
        err = x1v[...] + acc * r * wv - tg[...]
        dout = err * (1.0 / D_MODEL)
        dout_ref[...] = dout
        d_ff, dw = _rms_bwd(acc, wv, dout)
        d_ff_ref[...] = d_ff.astype(BF16)
        _accumulate(dw_ref, i == 0, dw)
        _accumulate(loss_ref, i == 0, jnp.sum(err * err, keepdims=True) * (0.5 / D_MODEL))

    d_ff, dout, loss, g_norm_ffn_post = _mm_call(
        "ffn_down_loss", act, wb['dn'], tm=tm, tn=D_MODEL, tk=FFN_D_FF, epilogue=loss_ep,
        extra_in=[(x1, (tm, D_MODEL), i0), (tgt, (tm, D_MODEL), i0), (ps['norm_ffn_post_w'], (1, D_MODEL), c0)],
        outs=[((t, D_MODEL), BF16, (tm, D_MODEL), i0), ((t, D_MODEL), F32, (tm, D_MODEL), i0),
              ((1, 1), F32, (1, 1), c0), ((1, D_MODEL), F32, (1, D_MODEL), c0)])

    d_act = _mm_plain("d_act", d_ff, wb['dn_t'], tm=tmw, tn=1408, tk=1024, out_dtype=BF16)
    g_w_down = _mm_plain("g_w_down", act, d_ff, tm=1408, tn=1024, tk=tkt, trans_a=True, out_dtype=BF16)
    d_gate, d_val, db_g, db_v = _ffn_act_bwd(ffn_gate, ffn_val, d_act, tm=tm)
    d_up_raw, gcw_g = _conv_bwd2("ffn_conv_bwd2_gate", d_gate, up_raw, 0, ps['ffn_conv_w'][:, :FFN_D_FF], tm=tm,
                                 tc=1408, out_cols=2 * FFN_D_FF, out_col0=0)
    d_up_raw, gcw_v = _conv_bwd2("ffn_conv_bwd2_val", d_val, up_raw, FFN_D_FF, ps['ffn_conv_w'][:, FFN_D_FF:], tm=tm,
                                 tc=1408, out_cols=2 * FFN_D_FF, out_col0=FFN_D_FF, fill=d_up_raw)
    g_ffn_conv_w = jnp.concatenate([gcw_g, gcw_v], axis=1)

    def dx1_ep(acc, i, j, ins, outs):
        x1v, wpre, dout_v, mmv, wpost = ins
        d_x1_ref, d_mm_ref, dwpre_ref, dwpost_ref = outs
        d_n, dw_pre = _rms_bwd(x1v[...], wpre[...], acc)
        d_x1 = dout_v[...] + d_n
        d_x1_ref[...] = d_x1
        d_mm, dw_post = _rms_bwd(mmv[...], wpost[...], d_x1)
        d_mm_ref[...] = d_mm.astype(BF16)
        _accumulate(dwpre_ref, i == 0, dw_pre)
        _accumulate(dwpost_ref, i == 0, dw_post)

    d_x1, d_mm, g_norm_ffn_pre, g_norm_mix_post = _mm_call(
        "d_h", d_up_raw, wb['up_t'], tm=tm, tn=D_MODEL, tk=2 * FFN_D_FF, epilogue=dx1_ep, vmem_mb=VMEM_BIG_MB,
        extra_in=[(x1, (tm, D_MODEL), i0), (ps['norm_ffn_pre_w'], (1, D_MODEL), c0), (dout, (tm, D_MODEL), i0),
                  (mmix, (tm, D_MODEL), i0), (ps['norm_mix_post_w'], (1, D_MODEL), c0)],
        outs=[((t, D_MODEL), F32, (tm, D_MODEL), i0), ((t, D_MODEL), BF16, (tm, D_MODEL), i0),
              ((1, D_MODEL), F32, (1, D_MODEL), c0), ((1, D_MODEL), F32, (1, D_MODEL), c0)])
    g_w_up_t = _mm_plain("g_w_up", d_up_raw, h, tm=1408, tn=1024, tk=tkt, trans_a=True, out_dtype=BF16)
    ride_ffn = rides('ffn', {'ffn_w_up': g_w_up_t, 'ffn_w_down': g_w_down})

    def dmerge_ep(acc, i, j, ins, outs):
        gs, ga, ys, ya = ins
        sg_s, sg_a = _sigmoid(gs[...]), _sigmoid(ga[...])
        outs[0][...] = (acc * sg_s).astype(BF16)
        outs[1][...] = (acc * sg_a).astype(BF16)
        outs[2][:, 0:D_MODEL] = (acc * ys[...] * sg_s * (1.0 - sg_s)).astype(BF16)
        outs[2][:, D_MODEL:2 * D_MODEL] = (acc * ya[...] * sg_a * (1.0 - sg_a)).astype(BF16)

    d_yssd, d_yattn, d_proj = _mm_call(
        "d_merged", d_mm, wb['mix_t'], tm=tm, tn=D_MODEL, tk=1024, epilogue=dmerge_ep,
        extra_in=[(proj, (tm, D_MODEL), lambda i, j, k: (i, OFF_GS // D_MODEL)),
                  (proj, (tm, D_MODEL), lambda i, j, k: (i, OFF_GA // D_MODEL)), (y_ssd, (tm, D_MODEL), i0), (y_attn, (tm, D_MODEL), i0)],
        outs=[((t, D_MODEL), BF16, (tm, D_MODEL), i0), ((t, D_MODEL), BF16, (tm, D_MODEL), i0),
              ((t, PROJ_W), BF16, (tm, 2 * D_MODEL), lambda i, j, k: (i, OFF_GS // (2 * D_MODEL)))])
    g_w_mix = _mm_plain("g_w_mix", merged, d_mm, tm=1024, tn=1024, tk=tkt, trans_a=True, out_dtype=BF16)

    def dgn_ep(acc, i, j, ins, outs):
        yv, zv, wn = ins
        d_y_ref, d_z_ref, dw_ref = outs
        zz = zv[...]
        sz = _sigmoid(zz)
        silu = zz * sz
        gv = yv[...] * silu
        r = lax.rsqrt(jnp.mean(gv * gv, axis=-1, keepdims=True) + NORM_EPS)
        gh = gv * r
        dgh = acc * wn[...]
        dg = r * (dgh - gh * jnp.mean(dgh * gh, axis=-1, keepdims=True))
        d_y_ref[...] = dg * silu
        d_z_ref[...] = (dg * yv[...] * (sz * (1.0 + zz * (1.0 - sz)))).astype(BF16)
        dw = jnp.sum(acc * gh, axis=0, keepdims=True)

        @pl.when(i == 0)
        def _():
            dw_ref[j] = dw

        @pl.when(i > 0)
        def _():
            dw_ref[j] += dw

    d_y, d_proj, g_ssd_norm = _mm_call(
        "d_gn", d_yssd, wb['so_t'], tm=tmw, tn=512, tk=1024, epilogue=dgn_ep, fill=(d_proj, 1),
        extra_in=[(y, (tmw, 512), ij), (proj, (tmw, 512), lambda i, j, k: (i, OFF_Z // 512 + j)), (ps['ssd_norm_w'], (1, 512), cj)],
        outs=[((t, SSD_D_INNER), F32, (tmw, 512), ij), ((t, PROJ_W), BF16, (tmw, 512), lambda i, j, k: (i, OFF_Z // 512 + j)),
              ((SSD_N_GROUPS, 1, 512), F32, (SSD_N_GROUPS, 1, 512), lambda i, j, k: (0, 0, 0))])
    g_ssd_norm = g_ssd_norm.reshape(1, SSD_D_INNER)
    g_w_so = _mm_plain("g_w_so", gn, d_yssd, tm=1024, tn=1024, tk=tkt, trans_a=True, out_dtype=BF16)
    d_xc, d_dtg, d_alog, d_dd, *rode['ffn'] = _ssd_bwd(xc, dtg, dtg_t, alog, alog_t, d_exp, d_y, hs, ride=ride_ffn)
    d_pre, g_ssd_conv_b = _conv_silu_bwd1(d_xc, xc_pre, tm=tm)
    d_proj, g_ssd_conv_w = _conv_bwd2("ssd_conv_bwd2", d_pre, proj, OFF_XBC, ps['ssd_conv_w'], tm=tm, tc=1536,
                                      out_cols=PROJ_W, out_col0=OFF_XBC, fill=d_proj)
    d_proj, g_dt_bias = _dt_bwd(d_dtg, proj, bias_pad, d_proj, tm=tmw)

    d_ao = _mm_plain("d_ao", d_yattn, wb['ao_t'], tm=tmw, tn=512, tk=1024, out_dtype=BF16)
    g_w_ao = _mm_plain("g_w_ao", ao, d_yattn, tm=1024, tn=1024, tk=tkt, trans_a=True, out_dtype=BF16)
    ride_mix = rides('mix', {'ssd_w_out': g_w_so, 'attn_w_out': g_w_ao, 'w_mix_out': g_w_mix})
    d_proj, g_sinks, *rode['mix'] = _attn_bwd(qr, kp, vp, kt, d_ao, ao, lse, sinks, cos_t, sin_t, d_proj, ride=ride_mix)

    def dx_ep(acc, i, j, ins, outs):
        xv, wn, dx1v = ins
        d_n, dw = _rms_bwd(xv[...], wn[...], acc)
        outs[0][...] = dx1v[...] + d_n
        _accumulate(outs[1], i == 0, dw)

    g_cat_t = _mm_plain("g_w_in", d_proj, u, tm=1024, tn=1024, tk=tkt, trans_a=True, out_dtype=BF16)
    grad_x, g_norm_mix_pre, *rode['w_in'] = _mm_call(
        "d_u", d_proj, wb['cat_t'], tm=tm, tn=D_MODEL, tk=PROJ_W, epilogue=dx_ep, ride=rides('w_in', {'w_in': g_cat_t}),
        vmem_mb=VMEM_BIG_MB,
        extra_in=[(x, (tm, D_MODEL), i0), (ps['norm_mix_pre_w'], (1, D_MODEL), c0), (d_x1, (tm, D_MODEL), i0)],
        outs=[((t, D_MODEL), F32, (tm, D_MODEL), i0), ((1, D_MODEL), F32, (1, D_MODEL), c0)])

    grads = {
        'norm_mix_pre_w': g_norm_mix_pre, 'w_in': g_cat_t, 'ssd_conv_w': g_ssd_conv_w, 'ssd_conv_b': g_ssd_conv_b,
        'ssd_dt_bias': g_dt_bias[:, :SSD_N_HEADS], 'ssd_a_log': d_alog[:, 0, :8].reshape(1, SSD_N_HEADS),
        'ssd_d': d_dd[:, 0, :8].reshape(1, SSD_N_HEADS), 'ssd_norm_w': g_ssd_norm, 'ssd_w_out': g_w_so,
        'attn_sinks': g_sinks[:, :ATTN_N_HEADS], 'attn_w_out': g_w_ao, 'w_mix_out': g_w_mix,
        'norm_mix_post_w': g_norm_mix_post, 'norm_ffn_pre_w': g_norm_ffn_pre, 'ffn_w_up': g_w_up_t,
        'ffn_conv_w': g_ffn_conv_w, 'ffn_conv_b': jnp.concatenate([db_g, db_v], axis=1), 'ffn_w_down': g_w_down,
        'norm_ffn_post_w': g_norm_ffn_post,
    }
    return loss, grad_x, grads, rode


def _group_channels(a):
    parts = []
    for g in range(SSD_N_GROUPS):
        parts += [a[..., 512 * g:512 * (g + 1)], a[..., 2048 + 128 * g:2048 + 128 * (g + 1)],
                  a[..., 2560 + 128 * g:2560 + 128 * (g + 1)]]
    return jnp.concatenate(parts, axis=-1)


def _ungroup_channels(a):
    xs = [a[..., GROUP_W * g:GROUP_W * g + 512] for g in range(SSD_N_GROUPS)]
    bs = [a[..., GROUP_W * g + 512:GROUP_W * g + 640] for g in range(SSD_N_GROUPS)]
    cs = [a[..., GROUP_W * g + 640:GROUP_W * (g + 1)] for g in range(SSD_N_GROUPS)]
    return jnp.concatenate(xs + bs + cs, axis=-1)


def _proj_rows(a_t, lo, hi):
    out = []
    for start, length, dst in sorted(PROJ_SEGS):
        s, e = max(lo, start), min(hi, start + length)
        if s < e:
            out.append(a_t[dst + s - start:dst + e - start])
    return out


def _to_proj_layout(w_in_t):
    pieces, pos = [], 0
    for start, length, dst in sorted(PROJ_SEGS, key=lambda s: s[2]):
        if dst > pos:
            pieces.append(jnp.zeros((dst - pos, w_in_t.shape[1]), w_in_t.dtype))
        pieces.append(w_in_t[start:start + length])
        pos = dst + length
    if pos < PROJ_W:
        pieces.append(jnp.zeros((PROJ_W - pos, w_in_t.shape[1]), w_in_t.dtype))
    return jnp.concatenate(pieces, axis=0)


def _rope_tables(positions):
    half = 32
    inv_freq = ROPE_THETA ** (-jnp.arange(half, dtype=F32) * 2.0 / 64)
    ang = positions.astype(F32)[:, None] * inv_freq
    cos, sin = jnp.cos(ang), jnp.sin(ang)
    return jnp.concatenate([cos, cos, cos, cos], axis=1), jnp.concatenate([-sin, sin, -sin, sin], axis=1)


def _matmul_weights(w_in_t):
    cat_t = _to_proj_layout(w_in_t)
    return {'cat': _column_tiles(cat_t, 1024), 'cat_t': cat_t}


def _late_weights(so, ao, mix, up_t, dn):
    return {'so': so, 'so_t': so.T, 'ao': ao, 'ao_t': ao.T, 'mix': mix, 'mix_t': mix.T,
            'up': _column_tiles(up_t, 1408), 'up_t': up_t, 'dn': dn, 'dn_t': dn.T}


def _column_tiles(w_t, tn):
    n, dm = w_t.shape
    return w_t.reshape(n // tn, tn, dm).transpose(0, 2, 1)


ANY = pl.BlockSpec(memory_space=pl.ANY)
MESH = pl.DeviceIdType.MESH
ROW_ALIGN = 32


def _mesh_pos():
    return lax.axis_index("x"), lax.axis_index("y"), lax.axis_index("c")


def _other_chips(x, y):
    return [(1 - x, y), (x, 1 - y), (1 - x, 1 - y)]


def _remote(src, dst, send_sems, recv_sems, k, to):
    return pltpu.make_async_remote_copy(src_ref=src, dst_ref=dst, send_sem=send_sems.at[k], recv_sem=recv_sems.at[k],
                                        device_id=to, device_id_type=MESH)


def _half(c, rh):
    return pl.ds(pl.multiple_of(c * rh, 16), rh)


def _ag_ride(shard):
    r = shard.shape[0]
    rh = r // 2

    def first_copies(w_ref, out_ref, send_sems, recv_sems):
        x, y, c = _mesh_pos()
        p = 2 * x + y
        mine = _half(c, rh)
        cps = [_remote(w_ref, out_ref.at[p], send_sems, recv_sems, 6, (x, y, 1 - c))]
        return cps + [_remote(w_ref.at[mine], out_ref.at[p, mine], send_sems, recv_sems, j, (cx, cy, c))
                      for j, (cx, cy) in enumerate(_other_chips(x, y))]

    def start(ins, outs, send_sems, recv_sems):
        for cp in first_copies(ins[0], outs[0], send_sems, recv_sems):
            cp.start()

    def forwards(out_ref, send_sems, recv_sems, half):
        x, y, c = _mesh_pos()
        return [_remote(out_ref.at[2 * cx + cy, half], out_ref.at[2 * cx + cy, half], send_sems, recv_sems, 3 + j, (x, y, 1 - c))
                for j, (cx, cy) in enumerate(_other_chips(x, y))]

    def middle(ins, outs, send_sems, recv_sems):
        x, y, c = _mesh_pos()
        mine = _half(c, rh)
        for j, (fwd, (cx, cy)) in enumerate(zip(forwards(outs[0], send_sems, recv_sems, mine), _other_chips(x, y))):
            slab = outs[0].at[2 * cx + cy, mine]
            _remote(slab, slab, send_sems, recv_sems, j, (x, y, 1 - c)).wait_recv()
            fwd.start()

    def finish(ins, outs, send_sems, recv_sems):
        w_ref, out_ref = ins[0], outs[0]
        x, y, c = _mesh_pos()
        for cp in forwards(out_ref, send_sems, recv_sems, _half(1 - c, rh)):
            cp.wait_recv()
        _remote(w_ref, out_ref.at[2 * x + y], send_sems, recv_sems, 6, (x, y, 1 - c)).wait_recv()
        for cp in first_copies(w_ref, out_ref, send_sems, recv_sems) + forwards(out_ref, send_sems, recv_sems, _half(c, rh)):
            cp.wait_send()

    return _Ride((shard,), (SDS((N_CHIPS, r, COMM_LANES), shard.dtype),), 7, start, finish, middle)


def _rs_ride(gbuf):
    rh = gbuf.shape[1] // 2

    def copies(g_ref, r_ref, send_sems, recv_sems, landing):
        x, y, c = _mesh_pos()
        cps = []
        for k, (cx, cy) in enumerate(_other_chips(x, y)):
            for h in range(2):
                slot = 2 * k + c if landing else 2 * k + h
                cps.append(pltpu.make_async_remote_copy(
                    src_ref=g_ref.at[2 * cx + cy, pl.ds(h * rh, rh)], dst_ref=r_ref.at[slot],
                    send_sem=send_sems.at[2 * k + h], recv_sem=recv_sems.at[slot],
                    device_id=(cx, cy, h), device_id_type=MESH))
        cps.append(_remote(g_ref.at[2 * x + y, _half(1 - c, rh)], r_ref.at[6], send_sems, recv_sems, 6, (x, y, 1 - c)))
        return cps

    def start(ins, outs, send_sems, recv_sems):
        for cp in copies(ins[0], outs[0], send_sems, recv_sems, True):
            cp.start()

    def finish(ins, outs, send_sems, recv_sems):
        for cp in copies(ins[0], outs[0], send_sems, recv_sems, False):
            cp.wait()

    return _Ride((gbuf,), (SDS((7, rh, COMM_LANES), gbuf.dtype),), 7, start, finish)


def _rs_sum(name, gbuf, got, pc_idx):
    rh = got.shape[1]
    tr = max(d for d in range(16, 513, 16) if rh % d == 0)
    nb = rh // tr

    def body(pc_ref, own_ref, *refs):
        o_ref = refs[7]
        p, c = pc_ref[0], pc_ref[1]
        own = own_ref[0].astype(F32)
        slots = [r[0].astype(F32) for r in refs[:7]]

        def term(q, h):
            code = p ^ q
            far = jnp.where(code == 2, slots[h], jnp.where(code == 1, slots[2 + h], slots[4 + h]))
            return jnp.where(code == 0, jnp.where(c == h, own, slots[6]), far)

        acc = term(0, 0)
        for q, h in [(0, 1), (1, 0), (1, 1), (2, 0), (2, 1), (3, 0), (3, 1)]:
            acc = acc + term(q, h)
        o_ref[0] = acc

    slot = lambda s: pl.BlockSpec((1, tr, COMM_LANES), lambda i, pc: (s, i, 0))
    return pl.pallas_call(
        body, name=name,
        grid_spec=pltpu.PrefetchScalarGridSpec(
            num_scalar_prefetch=1, grid=(nb,),
            in_specs=[pl.BlockSpec((1, tr, COMM_LANES), lambda i, pc: (pc[0], pc[1] * nb + i, 0))] + [slot(s) for s in range(7)],
            out_specs=pl.BlockSpec((1, tr, COMM_LANES), lambda i, pc: (pc[1], i, 0))),
        out_shape=SDS((2, rh, COMM_LANES), F32), compiler_params=_cp(dimension_semantics=("arbitrary",)),
    )(pc_idx, gbuf, *([got] * 7))


def _pair_gather_all(bufs):
    n = len(bufs)

    def body(*refs):
        outs, send_sems, recv_sems = refs[n:2 * n], refs[2 * n], refs[2 * n + 1]
        x, y, c = _mesh_pos()
        cps = [_remote(o.at[c], o.at[c], send_sems, recv_sems, k, (x, y, 1 - c)) for k, o in enumerate(outs)]
        for cp in cps:
            cp.start()
        for k, o in enumerate(outs):
            _remote(o.at[1 - c], o.at[1 - c], send_sems, recv_sems, k, (x, y, 1 - c)).wait_recv()
        for cp in cps:
            cp.wait_send()

    return pl.pallas_call(
        body, name="grad_pair_gather", in_specs=[ANY] * n, out_specs=[ANY] * n,
        out_shape=[SDS(b.shape, b.dtype) for b in bufs],
        scratch_shapes=[pltpu.SemaphoreType.DMA((n,)), pltpu.SemaphoreType.DMA((n,))],
        input_output_aliases={k: k for k in range(n)},
    )(*bufs)


def _pack_rows(big, small=()):
    parts = list(big)
    if small:
        flat = jnp.concatenate([p.reshape(-1) for p in small])
        k = -(-flat.shape[0] // (16 * COMM_LANES)) * 16
        parts.append(jnp.pad(flat, (0, k * COMM_LANES - flat.shape[0])).reshape(k, COMM_LANES))
    pad = -sum(p.shape[0] for p in parts) % ROW_ALIGN
    if pad:
        parts.append(jnp.zeros((pad, COMM_LANES), parts[0].dtype))
    return jnp.concatenate(parts, axis=0) if len(parts) > 1 else parts[0]


def _take(flat, off, shape):
    n = 1
    for d in shape:
        n *= d
    return flat[off:off + n].reshape(shape), off + n


BIG_ROWS = {'w_in': 2184, 'ssd_w_out': 512, 'attn_w_out': 256, 'w_mix_out': 256, 'ffn_w_up': 1408, 'ffn_w_down': 704}
TRANSPOSED = ('w_in', 'ffn_w_up')
LATE = ('ssd_w_out', 'attn_w_out', 'w_mix_out', 'ffn_w_up', 'ffn_w_down')
CONV_TAPS = ('ssd_conv_w', 'ffn_conv_w')
RS_GROUPS = {'ffn': ('ffn_w_up', 'ffn_w_down'), 'mix': ('ssd_w_out', 'attn_w_out', 'w_mix_out'), 'w_in': ('w_in',)}


def _exchange(name, ride):
    n_in, n_out = len(ride.ins), len(ride.outs)

    def body(*refs):
        ins, outs, sems = refs[:n_in], refs[n_in:n_in + n_out], refs[n_in + n_out:]
        ride.start(ins, outs, *sems)
        if ride.middle is not None:
            ride.middle(ins, outs, *sems)
        ride.finish(ins, outs, *sems)

    return pl.pallas_call(
        body, name=name, in_specs=[ANY] * n_in, out_specs=[ANY] * n_out, out_shape=list(ride.outs),
        scratch_shapes=[pltpu.SemaphoreType.DMA((ride.n_sems,)), pltpu.SemaphoreType.DMA((ride.n_sems,))],
    )(*ride.ins)


def kernel(x, positions, norm_mix_pre_w, w_in, ssd_conv_w, ssd_conv_b, ssd_dt_bias, ssd_a_log, ssd_d, ssd_norm_w, ssd_w_out, attn_sinks, attn_w_out, w_mix_out, norm_mix_post_w, norm_ffn_pre_w, ffn_w_up, ffn_conv_w, ffn_conv_b, ffn_w_down, norm_ffn_post_w, loss_target, m_norm_mix_pre_w, m_w_in, m_ssd_conv_w, m_ssd_conv_b, m_ssd_dt_bias, m_ssd_a_log, m_ssd_d, m_ssd_norm_w, m_ssd_w_out, m_attn_sinks, m_attn_w_out, m_w_mix_out, m_norm_mix_post_w, m_norm_ffn_pre_w, m_ffn_w_up, m_ffn_conv_w, m_ffn_conv_b, m_ffn_w_down, m_norm_ffn_post_w, v_norm_mix_pre_w, v_w_in, v_ssd_conv_w, v_ssd_conv_b, v_ssd_dt_bias, v_ssd_a_log, v_ssd_d, v_ssd_norm_w, v_ssd_w_out, v_attn_sinks, v_attn_w_out, v_w_mix_out, v_norm_mix_post_w, v_norm_ffn_pre_w, v_ffn_w_up, v_ffn_conv_w, v_ffn_conv_b, v_ffn_w_down, v_norm_ffn_post_w):
    given = dict(locals())
    w = {n: given[n][0] for n in WEIGHTS}
    w = {n: (a if a.ndim == 2 else a[None]) for n, a in w.items()}
    mom_m = {n: given['m_' + n].reshape(w[n].shape) for n in WEIGHTS}
    mom_v = {n: given['v_' + n].reshape(w[n].shape) for n in WEIGHTS}
    cx, cy, cc = _mesh_pos()
    pc_idx = jnp.stack([2 * cx + cy, cc]).astype(jnp.int32)

    rows_of = lambda n: (w[n].T if n in TRANSPOSED else w[n]).astype(BF16)
    gathered = _exchange("w_in_all_gather", _ag_ride(_pack_rows([rows_of('w_in')])))[0]
    wb = _matmul_weights(jnp.concatenate([gathered[s, :BIG_ROWS['w_in']] for s in range(N_CHIPS)], axis=0))
    taps = [lax.bitcast_convert_type(w[n], BF16) for n in CONV_TAPS]

    def unpack_late(arrived):
        rows, conv = {n: [] for n in LATE}, {n: [] for n in CONV_TAPS}
        for s in range(N_CHIPS):
            r0 = 0
            for n in LATE:
                rows[n].append(arrived[0][s, r0:r0 + BIG_ROWS[n]])
                r0 += BIG_ROWS[n]
            flat, off = arrived[0][s, r0:r0 + 16].reshape(-1), 0
            for n in CONV_TAPS:
                a, off = _take(flat, off, w[n].shape + (2,))
                conv[n].append(lax.bitcast_convert_type(a, F32))
        full = {n: jnp.concatenate(rows[n], axis=0) for n in LATE}
        return (_late_weights(*[full[n] for n in LATE]),
                {'ssd_conv_w': _group_channels(jnp.concatenate(conv['ssd_conv_w'], axis=1)),
                 'ffn_conv_w': jnp.concatenate(conv['ffn_conv_w'], axis=1)})

    late = (_ag_ride(_pack_rows([rows_of(n) for n in LATE], taps)), unpack_late)

    sent = {}

    def rides(group, g):
        parts = []
        for s in range(N_CHIPS):
            slab = []
            for n in RS_GROUPS[group]:
                lo, hi = BIG_ROWS[n] * s, BIG_ROWS[n] * (s + 1)
                slab += _proj_rows(g[n], lo, hi) if n == 'w_in' else [g[n][lo:hi]]
            slab = [a.astype(BF16) for a in slab]
            pad = -sum(a.shape[0] for a in slab) % ROW_ALIGN
            parts += slab + ([jnp.zeros((pad, COMM_LANES), BF16)] if pad else [])
        sent[group] = jnp.concatenate(parts, axis=0).reshape(N_CHIPS, -1, COMM_LANES)
        return _rs_ride(sent[group])

    ps = {n: w[n] for n in REPLICATED}
    ps['ssd_conv_b'] = _group_channels(w['ssd_conv_b'])
    cos_t, sin_t = _rope_tables(positions[0])
    loss, grad_x, grads, rode = _local_step(x[0], cos_t, sin_t, loss_target[0], wb, ps, late, rides)
    grads['ssd_conv_w'] = _ungroup_channels(grads['ssd_conv_w'])
    grads['ssd_conv_b'] = _ungroup_channels(grads['ssd_conv_b'])

    shard_cols = {n: sh[1] for n, _, sh in SHARDED}
    parts = []
    for s in range(N_CHIPS):
        small = [grads[n][:, shard_cols[n] * s:shard_cols[n] * (s + 1)] for n in CONV_TAPS] + [grads[n] for n in REPLICATED]
        flat = _pack_rows([], small)
        high = flat.astype(BF16)
        parts += [high, (flat - high.astype(F32)).astype(BF16)]
    sent['small'] = jnp.concatenate(parts, axis=0).reshape(N_CHIPS, -1, COMM_LANES)
    rode['small'] = _exchange("grad_small_exchange", _rs_ride(sent['small']))

    groups = ('ffn', 'mix', 'w_in', 'small')
    red = _pair_gather_all([_rs_sum("grad_sum_" + g, sent[g], rode[g][0], pc_idx) for g in groups])
    red = {g: r.reshape(-1, COMM_LANES) for g, r in zip(groups, red)}
    g_red = {}
    for g in groups[:3]:
        r0 = 0
        for n in RS_GROUPS[g]:
            g_red[n] = red[g][r0:r0 + BIG_ROWS[n]].T if n in TRANSPOSED else red[g][r0:r0 + BIG_ROWS[n]]
            r0 += BIG_ROWS[n]
    half = red['small'].shape[0] // 2
    flat, off = (red['small'][:half] + red['small'][half:]).reshape(-1), 0
    for n in CONV_TAPS + REPLICATED:
        g_red[n], off = _take(flat, off, w[n].shape)

    small_names = [n for n in WEIGHTS if n not in MATMUL_WEIGHTS]
    delta, new_m, new_v = {}, {}, {}
    for n in MATMUL_WEIGHTS:
        delta[n], new_m[n], new_v[n] = _adamw("adamw_" + n, w[n], g_red[n], mom_m[n], mom_v[n],
                                                  tr=max(d for d in range(8, 353, 8) if w[n].shape[0] % d == 0))
    packed = [_pack_small([d[n] for n in small_names]) for d in (w, g_red, mom_m, mom_v)]
    outs = _adamw("adamw_small", *packed, tr=packed[0].shape[0])
    for res, o in zip((delta, new_m, new_v), outs):
        fl, off = o.reshape(-1), 0
        for n in small_names:
            res[n], off = _take(fl, off, w[n].shape)

    loss_all = lax.psum(loss[0, 0], ("x", "y", "c"))
    shaped = lambda d: [d[n].reshape(given[n].shape) for n in WEIGHTS]
    return (loss_all, grad_x[None], *shaped(g_red), *shaped(delta), *shaped(new_m), *shaped(new_v))


def _pack_small(pieces):
    flat = jnp.concatenate([p.reshape(-1) for p in pieces])
    rows = -(-flat.shape[0] // (128 * 8)) * 8
    return jnp.pad(flat, (0, rows * 128 - flat.shape[0])).reshape(rows, 128)
```

```python
from typing import Callable, NamedTuple

import jax
import jax.numpy as jnp
from jax import lax
from jax.experimental import pallas as pl
from jax.experimental.pallas import tpu as pltpu

F32 = jnp.float32
BF16 = jnp.bfloat16
SDS = jax.ShapeDtypeStruct

D_MODEL = 1024
SSD_D_INNER = 2048
SSD_N_HEADS = 32
SSD_HEAD_DIM = 64
SSD_N_GROUPS = 4
SSD_D_STATE = 128
SSD_CONV_DIM = 3072
CHUNK = 128
ATTN_N_HEADS = 16
FFN_D_FF = 2816
ROPE_THETA = 10000.0
NORM_EPS = 1e-6
ADAM_LR, ADAM_B1, ADAM_B2, ADAM_EPS, ADAM_WD, ADAM_STEP = 0.001, 0.9, 0.999, 1e-08, 0.01, 10

PROJ_W = 9216
OFF_Q, OFF_K, OFF_V, OFF_Z, OFF_DT, OFF_GS, OFF_GA, OFF_XBC = 0, 1024, 1280, 1536, 3584, 4096, 5120, 6144
GROUP_W = 768
PROJ_SEGS = ([(0, 2048, OFF_Z)]
             + [(2048 + 512 * g, 512, OFF_XBC + GROUP_W * g) for g in range(4)]
             + [(4096 + 128 * g, 128, OFF_XBC + GROUP_W * g + 512) for g in range(4)]
             + [(4608 + 128 * g, 128, OFF_XBC + GROUP_W * g + 640) for g in range(4)]
             + [(5120, 32, OFF_DT), (5152, 1024, OFF_Q), (6176, 256, OFF_K), (6432, 256, OFF_V),
                (6688, 1024, OFF_GS), (7712, 1024, OFF_GA)])
VMEM_LIMIT_MB = 48
VMEM_BIG_MB = 60
NEG = -1e30

WEIGHTS = ('norm_mix_pre_w', 'w_in', 'ssd_conv_w', 'ssd_conv_b', 'ssd_dt_bias', 'ssd_a_log', 'ssd_d', 'ssd_norm_w',
           'ssd_w_out', 'attn_sinks', 'attn_w_out', 'w_mix_out', 'norm_mix_post_w', 'norm_ffn_pre_w', 'ffn_w_up',
           'ffn_conv_w', 'ffn_conv_b', 'ffn_w_down', 'norm_ffn_post_w')
SHARDED = (('w_in', 1, (1024, 2184)), ('ssd_conv_w', 1, (4, 768)), ('ssd_w_out', 0, (512, 1024)),
           ('attn_w_out', 0, (256, 1024)), ('w_mix_out', 0, (256, 1024)), ('ffn_w_up', 1, (1024, 1408)),
           ('ffn_conv_w', 1, (3, 1408)), ('ffn_w_down', 0, (704, 1024)))
MATMUL_WEIGHTS = ('w_in', 'ssd_w_out', 'attn_w_out', 'w_mix_out', 'ffn_w_up', 'ffn_w_down')
REPLICATED = tuple(n for n in WEIGHTS if n not in {s[0] for s in SHARDED})
N_CHIPS = 4
COMM_LANES = 1024


def _cp(vmem_mb=VMEM_LIMIT_MB, **kw):
    return pltpu.CompilerParams(vmem_limit_bytes=vmem_mb << 20, **kw)


class _Ride(NamedTuple):
    ins: tuple
    outs: tuple
    n_sems: int
    start: Callable
    finish: Callable
    middle: Callable = None


def _ride_parts(ride):
    if ride is None:
        return [], [], [], [], []
    hbm = pl.BlockSpec(memory_space=pl.ANY)
    return (list(ride.ins), [hbm] * len(ride.ins), list(ride.outs), [hbm] * len(ride.outs),
            [pltpu.SemaphoreType.DMA((ride.n_sems,)), pltpu.SemaphoreType.DMA((ride.n_sems,))])


def _ride_run(ride, first, last, in_refs, out_refs, sems, middle=None):
    if ride is None:
        return

    @pl.when(first)
    def _():
        ride.start(in_refs, out_refs, *sems)

    if ride.middle is not None:
        @pl.when(last if middle is None else middle)
        def _():
            ride.middle(in_refs, out_refs, *sems)

    @pl.when(last)
    def _():
        ride.finish(in_refs, out_refs, *sems)


def _iota(shape, axis):
    return lax.broadcasted_iota(jnp.int32, shape, axis)


def _sigmoid(v):
    return 1.0 / (1.0 + jnp.exp(-v))


def _mm_call(name, a, b, *, tm, tn, tk, epilogue, outs, extra_in=(), trans_a=False, fill=None, ride=None,
             vmem_mb=VMEM_LIMIT_MB):
    if trans_a:
        kdim, m = a.shape
    else:
        m, kdim = a.shape
    n = b.shape[1]
    assert b.shape[0] == kdim and m % tm == 0 and n % tn == 0 and kdim % tk == 0, (name, a.shape, b.shape, tm, tn, tk)
    gi, gj, gk = m // tm, n // tn, kdim // tk
    n_in, n_out = len(extra_in), len(outs)

    n_fill = 0 if fill is None else 1
    r_ops, r_in_specs, r_outs, r_out_specs, r_scratch = _ride_parts(ride)

    def body(a_ref, b_ref, *rest):
        ins = rest[:n_in]
        rest = rest[n_in + n_fill:]
        r_in, rest = rest[:len(r_ops)], rest[len(r_ops):]
        out_refs, rest = rest[:n_out], rest[n_out:]
        r_out, scratch = rest[:len(r_outs)], rest[len(r_outs):]
        i, j, k = pl.program_id(0), pl.program_id(1), pl.program_id(2)
        _ride_run(ride, (i == 0) & (j == 0) & (k == 0), (i == gi - 1) & (j == gj - 1) & (k == gk - 1),
                  r_in, r_out, scratch[-2:])
        av = a_ref[...].astype(BF16)
        bv = b_ref[...].astype(BF16)
        if trans_a:
            part = lax.dot_general(av, bv, (((0,), (0,)), ((), ())), preferred_element_type=F32)
        else:
            part = jnp.dot(av, bv, preferred_element_type=F32)
        if gk == 1:
            epilogue(part, i, j, ins, out_refs)
        else:
            acc = scratch[0]

            @pl.when(k == 0)
            def _():
                acc[...] = part

            @pl.when(k > 0)
            def _():
                acc[...] += part

            @pl.when(k == gk - 1)
            def _():
                epilogue(acc[...], i, j, ins, out_refs)

    a_spec = pl.BlockSpec((tk, tm), lambda i, j, k: (k, i)) if trans_a else pl.BlockSpec((tm, tk), lambda i, j, k: (i, k))
    if gj == 1 and gk == 1:
        b_spec = pl.BlockSpec((tk, tn), lambda i, j, k: (0, 0), pipeline_mode=pl.Buffered(1))
    else:
        b_spec = pl.BlockSpec((tk, tn), lambda i, j, k: (k, j))
    in_specs = [a_spec, b_spec]
    in_specs += [pl.BlockSpec(bs, im) for _, bs, im in extra_in]
    operands = [a, b] + [e[0] for e in extra_in]
    aliases = {}
    if fill is not None:
        in_specs.append(pl.BlockSpec(memory_space=pl.ANY))
        aliases = {len(operands): fill[1]}
        operands.append(fill[0])
    return pl.pallas_call(
        body, name=name, grid=(gi, gj, gk), in_specs=in_specs + r_in_specs,
        out_specs=[pl.BlockSpec(bs, im) for _, _, bs, im in outs] + r_out_specs,
        out_shape=[SDS(s, d) for s, d, _, _ in outs] + r_outs,
        scratch_shapes=([pltpu.VMEM((tm, tn), F32)] if gk > 1 else []) + r_scratch,
        input_output_aliases=aliases,
        compiler_params=_cp(vmem_mb, dimension_semantics=("arbitrary", "arbitrary", "arbitrary")),
    )(*operands, *r_ops)


def _mm_plain(name, a, b, *, tm, tn, tk, out_dtype=F32, trans_a=False):
    m = a.shape[1] if trans_a else a.shape[0]

    def epilogue(acc, i, j, ins, outs):
        outs[0][...] = acc.astype(out_dtype)

    return _mm_call(name, a, b, tm=tm, tn=tn, tk=tk, epilogue=epilogue, trans_a=trans_a,
                    outs=[((m, b.shape[1]), out_dtype, (tm, tn), lambda i, j, k: (i, j))])[0]


def _accumulate(ref, first, value):
    @pl.when(first)
    def _():
        ref[...] = value

    @pl.when(jnp.logical_not(first))
    def _():
        ref[...] += value


def _rms_bwd(xv, w, dy):
    r = lax.rsqrt(jnp.mean(xv * xv, axis=-1, keepdims=True) + NORM_EPS)
    xn = xv * r
    dxh = dy * w
    dx = r * (dxh - xn * jnp.mean(dxh * xn, axis=-1, keepdims=True))
    return dx, jnp.sum(dy * xn, axis=0, keepdims=True)


def _norm_mm(name, x, wn, w3, *, tm, ride=None):
    t, dm = x.shape
    gj, _, tn = w3.shape
    n = gj * tn
    tm = min(tm, t)
    gi = t // tm
    r_ops, r_in_specs, r_outs, r_out_specs, r_scratch = _ride_parts(ride)

    def body(x_ref, wn_ref, w_ref, *rest):
        r_in, rest = rest[:len(r_ops)], rest[len(r_ops):]
        o_ref, u_ref = rest[:2]
        r_out, sems = rest[2:2 + len(r_outs)], rest[2 + len(r_outs):]
        i, j = pl.program_id(0), pl.program_id(1)
        _ride_run(ride, (i == 0) & (j == 0), (i == gi - 1) & (j == gj - 1), r_in, r_out, sems,
                  middle=(i == (3 * gi) // 4) & (j == 0) if gi > 1 else None)

        @pl.when(j == 0)
        def _():
            xv = x_ref[...]
            r = lax.rsqrt(jnp.mean(xv * xv, axis=-1, keepdims=True) + NORM_EPS)
            u_ref[...] = (xv * r * wn_ref[...]).astype(BF16)

        o_ref[...] = jnp.dot(u_ref[...], w_ref[j], preferred_element_type=F32)

    return pl.pallas_call(
        body, name=name, grid=(gi, gj),
        in_specs=[pl.BlockSpec((tm, dm), lambda i, j: (i, 0)), pl.BlockSpec((1, dm), lambda i, j: (0, 0)),
                  pl.BlockSpec((gj, dm, tn), lambda i, j: (0, 0, 0), pipeline_mode=pl.Buffered(1))] + r_in_specs,
        out_specs=[pl.BlockSpec((tm, tn), lambda i, j: (i, j)), pl.BlockSpec((tm, dm), lambda i, j: (i, 0))] + r_out_specs,
        out_shape=[SDS((t, n), F32), SDS((t, dm), BF16)] + r_outs, scratch_shapes=r_scratch,
        compiler_params=_cp(dimension_semantics=("arbitrary", "arbitrary")),
    )(x, wn, w3, *r_ops)


def _shift_down(tile, halo, s):
    if s == 0:
        return tile
    r = pltpu.roll(tile, s, axis=0)
    h = pltpu.roll(halo, s, axis=0)
    head = jnp.where(_iota(h.shape, 0) < s, h, r[0:8])
    return jnp.concatenate([head, r[8:]], axis=0)


def _shift_up(tile, halo, s):
    if s == 0:
        return tile
    n = tile.shape[0]
    r = pltpu.roll(tile, n - s, axis=0)
    h = pltpu.roll(halo, 8 - s, axis=0)
    tail = jnp.where(_iota(h.shape, 0) >= 8 - s, h, r[n - 8:])
    return jnp.concatenate([r[:n - 8], tail], axis=0)


def _conv_apply(tile, halo, wv, bv, kw):
    acc = bv + wv[kw - 1:kw, :] * tile
    for k in range(kw - 1):
        acc = acc + wv[k:k + 1, :] * _shift_down(tile, halo, kw - 1 - k)
    return acc


def _prev_halo_spec(tm, tc, col0):
    return pl.BlockSpec((8, tc), lambda i, j: (jnp.maximum(i * (tm // 8) - 1, 0), col0 + j))


def _silu_parts(pre):
    sg = _sigmoid(pre)
    return pre * sg, sg * (1.0 + pre * (1.0 - sg))


def _conv_silu_fwd(proj, w, b, *, tm, tc=1536):
    t = proj.shape[0]
    c = w.shape[1]
    tm = min(tm, t)
    col0 = OFF_XBC // tc

    def body(x_ref, h_ref, w_ref, b_ref, o_ref, pre_ref):
        halo = jnp.where(pl.program_id(0) > 0, h_ref[...], 0.0)
        pre = _conv_apply(x_ref[...], halo, w_ref[...], b_ref[...], 4)
        o_ref[...] = _silu_parts(pre)[0]
        pre_ref[...] = pre.astype(BF16)

    tile = pl.BlockSpec((tm, tc), lambda i, j: (i, j))
    return pl.pallas_call(
        body, name="ssd_conv_fwd", grid=(t // tm, c // tc),
        in_specs=[pl.BlockSpec((tm, tc), lambda i, j: (i, col0 + j)), _prev_halo_spec(tm, tc, col0),
                  pl.BlockSpec((4, tc), lambda i, j: (0, j)), pl.BlockSpec((1, tc), lambda i, j: (0, j))],
        out_specs=[tile, tile], out_shape=[SDS((t, c), F32), SDS((t, c), BF16)],
        compiler_params=_cp(dimension_semantics=("arbitrary", "arbitrary")),
    )(proj, proj, w, b)


def _conv_silu_bwd1(d_out, pre, *, tm, tc=1536):
    t, c = pre.shape
    tm = min(tm, t)

    def body(g_ref, p_ref, o_ref, db_ref):
        i = pl.program_id(1)
        d_pre = g_ref[...] * _silu_parts(p_ref[...].astype(F32))[1]
        o_ref[...] = d_pre.astype(BF16)
        _accumulate(db_ref, i == 0, jnp.sum(d_pre, axis=0, keepdims=True))

    tile = pl.BlockSpec((tm, tc), lambda j, i: (i, j))
    return pl.pallas_call(
        body, name="ssd_conv_bwd1", grid=(c // tc, t // tm), in_specs=[tile, tile],
        out_specs=[tile, pl.BlockSpec((1, tc), lambda j, i: (0, j))],
        out_shape=[SDS((t, c), BF16), SDS((1, c), F32)],
        compiler_params=_cp(dimension_semantics=("arbitrary", "arbitrary")),
    )(d_out, pre)


def _conv_bwd2(name, d_pre, src, src_col0, w, *, tm, tc, out_cols, out_col0, fill=None):
    t, c = d_pre.shape
    kw = w.shape[0]
    tm = min(tm, t)
    ni = t // tm
    col0 = src_col0 // tc
    ocol0 = out_col0 // tc

    def body(g_ref, gn_ref, x_ref, w_ref, *rest):
        o_ref, dw_ref = rest[-2:]
        i = pl.program_id(1)
        g = g_ref[...].astype(F32)
        g_next = jnp.where(i < ni - 1, gn_ref[...].astype(F32)[0:8], 0.0)
        xv = x_ref[...]
        wv = w_ref[...]
        shifted = [_shift_up(g, g_next, kw - 1 - k) for k in range(kw)]
        d_in = wv[0:1, :] * shifted[0]
        for k in range(1, kw):
            d_in = d_in + wv[k:k + 1, :] * shifted[k]
        o_ref[...] = d_in.astype(o_ref.dtype)
        rows = [jnp.sum(shifted[k] * xv, axis=0, keepdims=True) for k in range(kw)]

        @pl.when(i == 0)
        def _():
            for k in range(kw):
                dw_ref[k:k + 1, :] = rows[k]

        @pl.when(i > 0)
        def _():
            for k in range(kw):
                dw_ref[k:k + 1, :] += rows[k]

    in_specs = [pl.BlockSpec((tm, tc), lambda j, i: (i, j)),
                pl.BlockSpec((16, tc), lambda j, i: (jnp.minimum((i + 1) * (tm // 16), t // 16 - 1), j)),
                pl.BlockSpec((tm, tc), lambda j, i: (i, col0 + j)),
                pl.BlockSpec((kw, tc), lambda j, i: (0, j))]
    operands = [d_pre, d_pre, src, w]
    if fill is not None:
        in_specs.append(pl.BlockSpec(memory_space=pl.ANY))
        operands.append(fill)
    return pl.pallas_call(
        body, name=name, grid=(c // tc, ni), in_specs=in_specs,
        out_specs=[pl.BlockSpec((tm, tc), lambda j, i: (i, ocol0 + j)), pl.BlockSpec((kw, tc), lambda j, i: (0, j))],
        out_shape=[SDS((t, out_cols), BF16), SDS((kw, c), F32)],
        input_output_aliases={} if fill is None else {4: 0},
        compiler_params=_cp(dimension_semantics=("arbitrary", "arbitrary")),
    )(*operands)


GELU_C = 0.7978845608028654


def _gelu_parts(v):
    inner = GELU_C * (v + 0.044715 * v * v * v)
    th = jnp.tanh(inner)
    val = 0.5 * v * (1.0 + th)
    grad = 0.5 * (1.0 + th) + 0.5 * v * (1.0 - th * th) * GELU_C * (1.0 + 3.0 * 0.044715 * v * v)
    return val, grad


def _ffn_act_fwd(up_raw, w, b, *, tm, tc=1408):
    t = up_raw.shape[0]
    tm = min(tm, t)
    nj = FFN_D_FF // tc
    halo = lambda i: jnp.maximum(i * (tm // 8) - 1, 0)

    def body(g_ref, gh_ref, v_ref, vh_ref, wg_ref, wv_ref, bg_ref, bv_ref, o_ref, gate_ref, val_ref):
        first = pl.program_id(0) > 0
        gate = _conv_apply(g_ref[...], jnp.where(first, gh_ref[...], 0.0), wg_ref[...], bg_ref[...], 3)
        val = _conv_apply(v_ref[...], jnp.where(first, vh_ref[...], 0.0), wv_ref[...], bv_ref[...], 3)
        o_ref[...] = (_gelu_parts(gate)[0] * val).astype(BF16)
        gate_ref[...] = gate.astype(BF16)
        val_ref[...] = val.astype(BF16)

    tile = pl.BlockSpec((tm, tc), lambda i, j: (i, j))
    return pl.pallas_call(
        body, name="ffn_act_fwd", grid=(t // tm, nj),
        in_specs=[tile, pl.BlockSpec((8, tc), lambda i, j: (halo(i), j)),
                  pl.BlockSpec((tm, tc), lambda i, j: (i, nj + j)), pl.BlockSpec((8, tc), lambda i, j: (halo(i), nj + j)),
                  pl.BlockSpec((3, tc), lambda i, j: (0, j)), pl.BlockSpec((3, tc), lambda i, j: (0, nj + j)),
                  pl.BlockSpec((1, tc), lambda i, j: (0, j)), pl.BlockSpec((1, tc), lambda i, j: (0, nj + j))],
        out_specs=[tile] * 3, out_shape=[SDS((t, FFN_D_FF), BF16)] * 3,
        compiler_params=_cp(dimension_semantics=("arbitrary", "arbitrary")),
    )(up_raw, up_raw, up_raw, up_raw, w, w, b, b)


def _ffn_act_bwd(gate, val, d_act, *, tm, tc=1408):
    t = gate.shape[0]
    tm = min(tm, t)
    nj = FFN_D_FF // tc

    def body(g_ref, v_ref, da_ref, dg_ref, dv_ref, dbg_ref, dbv_ref):
        i = pl.program_id(1)
        val = v_ref[...].astype(F32)
        ge, dge = _gelu_parts(g_ref[...].astype(F32))
        da = da_ref[...].astype(F32)
        d_gate = da * val * dge
        d_val = da * ge
        dg_ref[...] = d_gate.astype(BF16)
        dv_ref[...] = d_val.astype(BF16)
        _accumulate(dbg_ref, i == 0, jnp.sum(d_gate, axis=0, keepdims=True))
        _accumulate(dbv_ref, i == 0, jnp.sum(d_val, axis=0, keepdims=True))

    tile = pl.BlockSpec((tm, tc), lambda j, i: (i, j))
    row = pl.BlockSpec((1, tc), lambda j, i: (0, j))
    return pl.pallas_call(
        body, name="ffn_act_bwd", grid=(nj, t // tm), in_specs=[tile] * 3, out_specs=[tile, tile, row, row],
        out_shape=[SDS((t, FFN_D_FF), BF16), SDS((t, FFN_D_FF), BF16), SDS((1, FFN_D_FF), F32), SDS((1, FFN_D_FF), F32)],
        compiler_params=_cp(dimension_semantics=("arbitrary", "arbitrary")),
    )(gate, val, d_act)


def _softplus(v):
    e = jnp.exp(-jnp.abs(v))
    small = e * (1.0 - 0.5 * e)
    return jnp.maximum(v, 0.0) + jnp.where(e < 1e-4, small, jnp.log(1.0 + e))


def _dt_fwd(proj, bias_pad, *, tm):
    t = proj.shape[0]
    tm = min(tm, t)

    def body(x_ref, b_ref, g_ref, gt_ref):
        dt = _softplus(x_ref[...] + b_ref[...])
        first8 = _iota((tm, 128), 1) < 8
        for g in range(SSD_N_GROUPS):
            dg = jnp.where(first8, dt if g == 0 else pltpu.roll(dt, 128 - 8 * g, axis=1), 0.0)
            g_ref[g] = dg
            gt_ref[g] = dg.T[0:8, :]

    return pl.pallas_call(
        body, name="dt_fwd", grid=(t // tm,),
        in_specs=[pl.BlockSpec((tm, 128), lambda i: (i, OFF_DT // 128)), pl.BlockSpec((1, 128), lambda i: (0, 0))],
        out_specs=[pl.BlockSpec((SSD_N_GROUPS, tm, 128), lambda i: (0, i, 0)), pl.BlockSpec((SSD_N_GROUPS, 8, tm), lambda i: (0, 0, i))],
        out_shape=[SDS((SSD_N_GROUPS, t, 128), F32), SDS((SSD_N_GROUPS, 8, t), F32)],
        compiler_params=_cp(dimension_semantics=("arbitrary",)),
    )(proj, bias_pad)


def _dt_bwd(d_dtg, proj, bias_pad, d_proj, *, tm):
    t = proj.shape[0]
    tm = min(tm, t)

    def body(g_ref, x_ref, b_ref, _, o_ref, db_ref):
        first8 = _iota((tm, 128), 1) < 8
        d_dt = jnp.where(first8, g_ref[0], 0.0)
        for g in range(1, SSD_N_GROUPS):
            d_dt = d_dt + pltpu.roll(jnp.where(first8, g_ref[g], 0.0), 8 * g, axis=1)
        d_raw = d_dt * _sigmoid(x_ref[...] + b_ref[...])
        o_ref[:, 0:128] = d_raw.astype(BF16)
        o_ref[:, 128:512] = jnp.zeros((tm, 384), BF16)
        _accumulate(db_ref, pl.program_id(0) == 0, jnp.sum(d_raw, axis=0, keepdims=True))

    return pl.pallas_call(
        body, name="dt_bwd", grid=(t // tm,),
        in_specs=[pl.BlockSpec((SSD_N_GROUPS, tm, 128), lambda i: (0, i, 0)), pl.BlockSpec((tm, 128), lambda i: (i, OFF_DT // 128)),
                  pl.BlockSpec((1, 128), lambda i: (0, 0)), pl.BlockSpec(memory_space=pl.ANY)],
        out_specs=[pl.BlockSpec((tm, 512), lambda i: (i, OFF_DT // 512)), pl.BlockSpec((1, 128), lambda i: (0, 0))],
        out_shape=[SDS((t, PROJ_W), BF16), SDS((1, 128), F32)],
        input_output_aliases={3: 0},
        compiler_params=_cp(dimension_semantics=("arbitrary",)),
    )(d_dtg, proj, bias_pad, d_proj)


def _split3(v):
    hi = v.astype(BF16)
    r1 = v - hi.astype(F32)
    mid = r1.astype(BF16)
    return hi, mid, (r1 - mid.astype(F32)).astype(BF16)


def _times01(v, m3):
    return jnp.dot(jnp.concatenate(_split3(v), axis=1), m3, preferred_element_type=F32)


def _01times(m3, v):
    return jnp.dot(m3, jnp.concatenate(_split3(v), axis=0), preferred_element_type=F32)


def _ssd_decay(dt_ref, dtT_ref, al_ref, alT_ref, k):
    dt = dt_ref[0]
    a_row = -jnp.exp(al_ref[0])
    adt_t = dtT_ref[0] * (-jnp.exp(alT_ref[0]))
    return dt, a_row, _01times(k['low3'][...], dt * a_row), _times01(adt_t, k['up3v'][...])


def _ssd_specs(nc, rev):
    ci = (lambda c: nc - 1 - c) if rev else (lambda c: c)
    return [pl.BlockSpec((CHUNK, SSD_CONV_DIM), lambda c: (ci(c), 0)),
            pl.BlockSpec((SSD_N_GROUPS, CHUNK, 128), lambda c: (0, ci(c), 0)),
            pl.BlockSpec((SSD_N_GROUPS, 8, CHUNK), lambda c: (0, 0, ci(c))),
            pl.BlockSpec((SSD_N_GROUPS, 1, 128), lambda c: (0, 0, 0)),
            pl.BlockSpec((SSD_N_GROUPS, 8, 1), lambda c: (0, 0, 0)),
            pl.BlockSpec((1, SSD_D_INNER), lambda c: (0, 0))]


def _ssd_group_views(g, x_ref, dt_ref, dtT_ref, al_ref, alT_ref, d_ref):
    return (x_ref.at[:, g * GROUP_W:(g + 1) * GROUP_W], dt_ref.at[g:g + 1], dtT_ref.at[g:g + 1], al_ref.at[g:g + 1],
            alT_ref.at[g:g + 1], d_ref.at[:, g * 512:(g + 1) * 512])


NT = (((1,), (1,)), ((), ()))
WIDE = 8 * CHUNK
SSD_CONST_NAMES = ('e128', 'e64', 's64', 'mlo', 'mup', 'low3', 'up3', 'up3v')
SSD_CONST_SHAPES = [pltpu.VMEM((3 * CHUNK, WIDE), BF16), pltpu.VMEM((3 * CHUNK, 512), BF16), pltpu.VMEM((512, CHUNK), BF16),
                    pltpu.VMEM((CHUNK, WIDE), F32), pltpu.VMEM((CHUNK, WIDE), F32), pltpu.VMEM((CHUNK, 3 * CHUNK), BF16),
                    pltpu.VMEM((CHUNK, 3 * CHUNK), BF16), pltpu.VMEM((3 * CHUNK, CHUNK), BF16)]


def _ssd_init_consts(k):
    row, col = _iota((3 * CHUNK, WIDE), 0), _iota((3 * CHUNK, WIDE), 1)
    k['e128'][...] = ((col >> 7) == (row & 127)).astype(BF16)
    k['e64'][...] = ((_iota((3 * CHUNK, 512), 1) >> 6) == (_iota((3 * CHUNK, 512), 0) & 127)).astype(BF16)
    k['s64'][...] = ((_iota((512, CHUNK), 0) >> 6) == _iota((512, CHUNK), 1)).astype(BF16)
    row, col = _iota((CHUNK, WIDE), 0), _iota((CHUNK, WIDE), 1)
    k['mlo'][...] = (row >= (col & 127)).astype(F32)
    k['mup'][...] = (row <= (col & 127)).astype(F32)
    row, col = _iota((CHUNK, 3 * CHUNK), 0), _iota((CHUNK, 3 * CHUNK), 1) & 127
    k['low3'][...] = (row >= col).astype(BF16)
    k['up3'][...] = (row <= col).astype(BF16)
    row, col = _iota((3 * CHUNK, CHUNK), 0) & 127, _iota((3 * CHUNK, CHUNK), 1)
    k['up3v'][...] = (row <= col).astype(BF16)


def _ssd_common(x_ref, dt_ref, dtT_ref, al_ref, alT_ref, k):
    dt, a_row, acs, acs_t = _ssd_decay(dt_ref, dtT_ref, al_ref, alT_ref, k)
    ecol = _times01(acs, k['e128'][...])
    rrow = jnp.concatenate([jnp.broadcast_to(acs_t[j:j + 1, :], (CHUNK, CHUNK)) for j in range(8)], axis=1)
    a64 = _times01(acs, k['e64'][...])
    dt64 = _times01(dt, k['e64'][...])
    a_end64 = a64[CHUNK - 1:CHUNK, :]
    xs = x_ref[:, 0:512]
    return dict(dt=dt, a_row=a_row, acs=acs, seg=ecol - rrow, dt64=dt64, e_a=jnp.exp(a64), decay=jnp.exp(a_end64 - a64),
                e_end64=jnp.exp(a_end64), xs=xs, xdt=xs * dt64, bm=x_ref[:, 512:640], cm=x_ref[:, 640:768])


def _pair_blocks(v):
    lo = _iota((CHUNK, 128), 1) < 64
    out = []
    for i in range(4):
        ch = v[:, i * 128:(i + 1) * 128]
        out.append(jnp.concatenate([jnp.where(lo, ch, 0.0), jnp.where(lo, 0.0, ch)], axis=0).astype(BF16))
    return out


def _tile8(m):
    return jnp.concatenate([m] * 8, axis=1)


def _ssd_fwd(xc, dtg, dtg_t, alog, alog_t, d_exp):
    t = xc.shape[0]
    nc = t // CHUNK

    def body(xa_ref, dta_ref, dtTa_ref, ala_ref, alTa_ref, da_ref, ya_ref, hs_ref, h_scr, *consts):
        c = pl.program_id(0)
        k = dict(zip(SSD_CONST_NAMES, consts))

        @pl.when(c == 0)
        def _():
            _ssd_init_consts(k)
            h_scr[...] = jnp.zeros_like(h_scr)

        for g in range(SSD_N_GROUPS):
            x_ref, dt_ref, dtT_ref, al_ref, alT_ref, d_ref = _ssd_group_views(g, xa_ref, dta_ref, dtTa_ref, ala_ref, alTa_ref, da_ref)
            v = _ssd_common(x_ref, dt_ref, dtT_ref, al_ref, alT_ref, k)
            b16, c16 = v['bm'].astype(BF16), v['cm'].astype(BF16)
            cb = lax.dot_general(c16, b16, NT, preferred_element_type=F32)
            m16 = (jnp.exp(jnp.minimum(v['seg'], 0.0)) * k['mlo'][...] * _tile8(cb)).astype(BF16)
            xbd = _pair_blocks(v['xdt'])
            y_diag = jnp.concatenate([jnp.dot(m16[:, i * 256:(i + 1) * 256], xbd[i], preferred_element_type=F32)
                                      for i in range(4)], axis=1)
            ht = h_scr[g]
            y_off = jnp.dot(c16, ht.astype(BF16), preferred_element_type=F32)
            ya_ref[:, g * 512:(g + 1) * 512] = y_diag + v['e_a'] * y_off + d_ref[...] * v['xs']
            st = jnp.dot(v['bm'].T.astype(BF16), (v['xdt'] * v['decay']).astype(BF16), preferred_element_type=F32)
            hs_ref[0, g] = ht
            h_scr[g] = ht * v['e_end64'] + st

    return pl.pallas_call(
        body, name="ssd_fwd", grid=(nc,), in_specs=_ssd_specs(nc, False),
        out_specs=[pl.BlockSpec((CHUNK, SSD_D_INNER), lambda c: (c, 0)),
                   pl.BlockSpec((1, SSD_N_GROUPS, SSD_D_STATE, 512), lambda c: (c, 0, 0, 0))],
        out_shape=[SDS((t, SSD_D_INNER), F32), SDS((nc, SSD_N_GROUPS, SSD_D_STATE, 512), F32)],
        scratch_shapes=[pltpu.VMEM((SSD_N_GROUPS, SSD_D_STATE, 512), F32)] + SSD_CONST_SHAPES,
        compiler_params=_cp(dimension_semantics=("arbitrary",)),
    )(xc, dtg, dtg_t, alog, alog_t, d_exp)


def _ssd_bwd(xc, dtg, dtg_t, alog, alog_t, d_exp, d_y, hs, ride=None):
    t = xc.shape[0]
    nc = t // CHUNK

    r_ops, r_in_specs, r_outs, r_out_specs, r_scratch = _ride_parts(ride)

    def body(xa_ref, dta_ref, dtTa_ref, ala_ref, alTa_ref, da_ref, dya_ref, hs_ref, *rest):
        r_in, rest = rest[:len(r_ops)], rest[len(r_ops):]
        dxa_ref, ddta_ref, dal_ref, dd_ref = rest[:4]
        r_out, rest = rest[4:4 + len(r_outs)], rest[4 + len(r_outs):]
        g_scr, consts, sems = rest[0], rest[1:1 + len(SSD_CONST_NAMES)], rest[1 + len(SSD_CONST_NAMES):]
        c = pl.program_id(0)
        _ride_run(ride, c == 0, c == nc - 1, r_in, r_out, sems)
        k = dict(zip(SSD_CONST_NAMES, consts))

        @pl.when(c == 0)
        def _():
            _ssd_init_consts(k)
            g_scr[...] = jnp.zeros_like(g_scr)

        for g in range(SSD_N_GROUPS):
            views = _ssd_group_views(g, xa_ref, dta_ref, dtTa_ref, ala_ref, alTa_ref, da_ref)
            one_group(c, g, k, *views, dya_ref.at[:, g * 512:(g + 1) * 512], hs_ref, g_scr,
                      dxa_ref.at[:, g * GROUP_W:(g + 1) * GROUP_W], ddta_ref.at[g:g + 1], dal_ref, dd_ref)

    def one_group(c, g, k, x_ref, dt_ref, dtT_ref, al_ref, alT_ref, d_ref, dy_ref, hs_ref, g_scr, dx_ref, ddt_ref,
                  dal_ref, dd_ref):
        s64, mlo, mup = k['s64'], k['mlo'], k['mup']
        v = _ssd_common(x_ref, dt_ref, dtT_ref, al_ref, alT_ref, k)
        dt, a_row, xs, xdt, e_a, decay = v['dt'], v['a_row'], v['xs'], v['xdt'], v['e_a'], v['decay']
        row, col = _iota((CHUNK, CHUNK), 0), _iota((CHUNK, CHUNK), 1)
        b16, c16 = v['bm'].astype(BF16), v['cm'].astype(BF16)
        ct16 = v['cm'].T.astype(BF16)
        cb = lax.dot_general(c16, b16, NT, preferred_element_type=F32)
        cbt = lax.dot_general(b16, c16, NT, preferred_element_type=F32)
        lmat = jnp.exp(jnp.minimum(v['seg'], 0.0)) * mlo[...]
        lmat_t = jnp.exp(jnp.minimum(-v['seg'], 0.0)) * mup[...]
        mmat, mmat_t = lmat * _tile8(cb), lmat_t * _tile8(cbt)
        mt16 = mmat_t.astype(BF16)
        dy = dy_ref[...]
        dye, xdec = dy * e_a, xdt * decay
        dy16, dye16, xdec16 = dy.astype(BF16), dye.astype(BF16), xdec.astype(BF16)
        xdt16 = xdt.astype(BF16)
        ht, gt = hs_ref[0, g], g_scr[g]
        ht16, gt16 = ht.astype(BF16), gt.astype(BF16)
        xbd, dybd = _pair_blocks(xdt), _pair_blocks(dy)
        d_m, d_mt, d_x = [], [], []
        for i in range(4):
            csl = slice(i * 128, (i + 1) * 128)
            d_m.append(lax.dot_general(dy16[:, csl], xbd[i], NT, preferred_element_type=F32))
            d_mt.append(lax.dot_general(xdt16[:, csl], dybd[i], NT, preferred_element_type=F32))
            d_x.append(jnp.dot(mt16[:, i * 256:(i + 1) * 256], dybd[i], preferred_element_type=F32))
        d_m, d_mt, d_x = jnp.concatenate(d_m, axis=1), jnp.concatenate(d_mt, axis=1), jnp.concatenate(d_x, axis=1)

        def head_sum(m):
            acc = m[:, 0:CHUNK]
            for j in range(1, 8):
                acc = acc + m[:, j * CHUNK:(j + 1) * CHUNK]
            return acc

        def seg64(p):
            return jnp.dot(p.astype(BF16), s64[...], preferred_element_type=F32)

        d_cb16 = head_sum(d_m * lmat).astype(BF16)
        d_cbt16 = head_sum(d_mt * lmat_t).astype(BF16)
        dseg = d_m * mmat - d_mt * mmat_t
        da_seg = jnp.zeros((CHUNK, CHUNK), F32)
        for j in range(8):
            da_seg = jnp.where(col == j, jnp.sum(dseg[:, j * CHUNK:(j + 1) * CHUNK], axis=1, keepdims=True), da_seg)
        ch = jnp.dot(c16, ht16, preferred_element_type=F32)
        bg = jnp.dot(b16, gt16, preferred_element_type=F32)
        d_x = d_x + decay * bg
        d_decay = seg64(xdec * bg)
        e_end = jnp.exp(v['acs'][CHUNK - 1:CHUNK, :])
        d_end = e_end * jnp.sum(seg64(gt * ht), axis=0, keepdims=True) + jnp.sum(d_decay, axis=0, keepdims=True)
        d_a = seg64(dye * ch) - d_decay + da_seg + jnp.where(row == CHUNK - 1, d_end, 0.0)
        dx_ref[:, 0:512] = d_x * v['dt64'] + d_ref[...] * dy
        dx_ref[:, 640:768] = (lax.dot_general(dye16, ht16, NT, preferred_element_type=F32)
                              + jnp.dot(d_cb16, b16, preferred_element_type=F32))
        dx_ref[:, 512:640] = (lax.dot_general(xdec16, gt16, NT, preferred_element_type=F32)
                              + jnp.dot(d_cbt16, c16, preferred_element_type=F32))
        g_scr[g] = gt * v['e_end64'] + jnp.dot(ct16, dye16, preferred_element_type=F32)
        d_adt = _01times(k['up3'][...], d_a)
        ddt_ref[0] = d_adt * a_row + seg64(d_x * xs)
        d_alog = jnp.sum(d_adt * dt, axis=0, keepdims=True) * a_row
        dd_row = jnp.sum(seg64(dy * xs), axis=0, keepdims=True)
        first = c == 0

        @pl.when(first)
        def _():
            dal_ref[g] = d_alog
            dd_ref[g] = dd_row

        @pl.when(jnp.logical_not(first))
        def _():
            dal_ref[g] += d_alog
            dd_ref[g] += dd_row

    rc = lambda c: nc - 1 - c
    whole = pl.BlockSpec((SSD_N_GROUPS, 1, 128), lambda c: (0, 0, 0))
    return pl.pallas_call(
        body, name="ssd_bwd", grid=(nc,),
        in_specs=_ssd_specs(nc, True) + [pl.BlockSpec((CHUNK, SSD_D_INNER), lambda c: (rc(c), 0)),
                                        pl.BlockSpec((1, SSD_N_GROUPS, SSD_D_STATE, 512), lambda c: (rc(c), 0, 0, 0))] + r_in_specs,
        out_specs=[pl.BlockSpec((CHUNK, SSD_CONV_DIM), lambda c: (rc(c), 0)),
                   pl.BlockSpec((SSD_N_GROUPS, CHUNK, 128), lambda c: (0, rc(c), 0)), whole, whole] + r_out_specs,
        out_shape=[SDS((t, SSD_CONV_DIM), F32), SDS((SSD_N_GROUPS, t, 128), F32),
                   SDS((SSD_N_GROUPS, 1, 128), F32), SDS((SSD_N_GROUPS, 1, 128), F32)] + r_outs,
        scratch_shapes=[pltpu.VMEM((SSD_N_GROUPS, SSD_D_STATE, 512), F32)] + SSD_CONST_SHAPES + r_scratch,
        compiler_params=_cp(dimension_semantics=("arbitrary",)),
    )(xc, dtg, dtg_t, alog, alog_t, d_exp, d_y, hs, *r_ops)


def _gated_norm_fwd(y, proj, w, *, tm):
    t = y.shape[0]
    tm = min(tm, t)

    def body(y_ref, z_ref, w_ref, o_ref):
        gv = y_ref[...] * _silu_parts(z_ref[...])[0]
        r = lax.rsqrt(jnp.mean(gv * gv, axis=-1, keepdims=True) + NORM_EPS)
        o_ref[...] = (gv * r * w_ref[...]).astype(BF16)

    tile = pl.BlockSpec((tm, 512), lambda i, g: (i, g))
    return pl.pallas_call(
        body, name="gated_norm_fwd", grid=(t // tm, SSD_N_GROUPS),
        in_specs=[tile, pl.BlockSpec((tm, 512), lambda i, g: (i, OFF_Z // 512 + g)),
                  pl.BlockSpec((1, 512), lambda i, g: (0, g))], out_specs=tile,
        out_shape=SDS((t, SSD_D_INNER), BF16),
        compiler_params=_cp(dimension_semantics=("arbitrary", "arbitrary")),
    )(y, proj, w)


def _rope(ch, cos_t, sin_t):
    first = (_iota(ch.shape, 1) & 32) == 0
    partner = jnp.where(first, pltpu.roll(ch, 96, axis=1), pltpu.roll(ch, 32, axis=1))
    return ch * cos_t + partner * sin_t


def _rope_qkv(proj, cos_t, sin_t, *, tm):
    t = proj.shape[0]
    tm = min(tm, t)

    def body(q_ref, k_ref, v_ref, c_ref, s_ref, qr_ref, kp_ref, vp_ref, kt_ref, vt_ref):
        cv, sv = c_ref[...], s_ref[...]
        lo = _iota((tm, 128), 1) < 64
        for m in range(8):
            sl = slice(m * 128, (m + 1) * 128)
            qr_ref[:, sl] = (_rope(q_ref[:, sl], cv, sv) * 0.125).astype(BF16)
        for m2 in range(2):
            sl = slice(m2 * 128, (m2 + 1) * 128)
            for src, dst, dst_t in ((_rope(k_ref[:, sl], cv, sv), kp_ref, kt_ref), (v_ref[:, sl], vp_ref, vt_ref)):
                sw = pltpu.roll(src, 64, axis=1)
                padded = (jnp.where(lo, src, 0.0), jnp.where(lo, 0.0, sw), jnp.where(lo, sw, 0.0), jnp.where(lo, 0.0, src))
                for i, pad in enumerate(padded):
                    rows = slice((4 * m2 + i) * 128, (4 * m2 + i + 1) * 128)
                    dst[:, rows] = pad.astype(BF16)
                    dst_t[rows, :] = pad.T.astype(BF16)

    return pl.pallas_call(
        body, name="rope_qkv", grid=(t // tm,),
        in_specs=[pl.BlockSpec((tm, 1024), lambda i: (i, OFF_Q // 1024)), pl.BlockSpec((tm, 256), lambda i: (i, OFF_K // 256)),
                  pl.BlockSpec((tm, 256), lambda i: (i, OFF_V // 256)), pl.BlockSpec((tm, 128), lambda i: (i, 0)),
                  pl.BlockSpec((tm, 128), lambda i: (i, 0))],
        out_specs=[pl.BlockSpec((tm, 1024), lambda i: (i, 0))] * 3 + [pl.BlockSpec((1024, tm), lambda i: (0, i))] * 2,
        out_shape=[SDS((t, 1024), BF16)] * 3 + [SDS((1024, t), BF16)] * 2,
        compiler_params=_cp(dimension_semantics=("arbitrary",)),
    )(proj, proj, proj, cos_t, sin_t)


def _attn_valid(n):
    kj, qi = _iota((2 * CHUNK, CHUNK), 0), _iota((2 * CHUNK, CHUNK), 1)
    return (kj > qi) & (kj <= qi + CHUNK) & ((n > 0) | (kj >= CHUNK))


def _attn_fwd(qr, kp, vt, sinks):
    t = qr.shape[0]
    nb = t // CHUNK

    def body(q_ref, kc_ref, kprev_ref, vc_ref, vprev_ref, sk_ref, o_ref, lse_ref):
        n = pl.program_id(0)
        valid = _attn_valid(n)
        head_row = _iota((16, CHUNK), 0)
        lse_all = jnp.zeros((16, CHUNK), F32)
        for m in range(8):
            g = m // 2
            qch = q_ref[:, m * 128:(m + 1) * 128]
            sls = [slice((2 * g + e) * 128, (2 * g + e + 1) * 128) for e in range(2)]
            kk2 = jnp.concatenate([r[:, sl] for sl in sls for r in (kprev_ref, kc_ref)], axis=0)
            vv2_t = jnp.concatenate([r[sl, :] for sl in sls for r in (vprev_ref, vc_ref)], axis=1)
            s2 = lax.dot_general(kk2, qch, NT, preferred_element_type=F32)
            probs = []
            for e in range(2):
                h = 2 * m + e
                s = jnp.where(valid, s2[2 * CHUNK * e:2 * CHUNK * (e + 1)], NEG)
                sink = sk_ref[0:1, h:h + 1]
                mx = jnp.maximum(jnp.max(s, axis=0, keepdims=True), sink)
                p = jnp.exp(s - mx)
                den = jnp.sum(p, axis=0, keepdims=True) + jnp.exp(sink - mx)
                probs.append((p * (1.0 / den)).astype(BF16))
                lse_all = jnp.where(head_row == h, mx + jnp.log(den), lse_all)
            o_t = jnp.dot(vv2_t, jnp.concatenate(probs, axis=0), preferred_element_type=F32)
            o_ref[:, m * 128:(m + 1) * 128] = o_t.T.astype(BF16)
        lse_ref[0] = lse_all

    cur = pl.BlockSpec((CHUNK, 1024), lambda n: (n, 0))
    prev = pl.BlockSpec((CHUNK, 1024), lambda n: (jnp.maximum(n - 1, 0), 0))
    cur_t = pl.BlockSpec((1024, CHUNK), lambda n: (0, n))
    prev_t = pl.BlockSpec((1024, CHUNK), lambda n: (0, jnp.maximum(n - 1, 0)))
    return pl.pallas_call(
        body, name="attn_fwd", grid=(nb,),
        in_specs=[cur, cur, prev, cur_t, prev_t, pl.BlockSpec((1, 128), lambda n: (0, 0))],
        out_specs=[cur, pl.BlockSpec((1, 16, CHUNK), lambda n: (n, 0, 0))],
        out_shape=[SDS((t, 1024), BF16), SDS((nb, 16, CHUNK), F32)],
        compiler_params=_cp(dimension_semantics=("arbitrary",)),
    )(qr, kp, kp, vt, vt, sinks)


def _attn_bwd(qr, kp, vp, kt, d_o, o, lse, sinks, cos_t, sin_t, d_proj, ride=None):
    t = qr.shape[0]
    nb = t // CHUNK

    r_ops, r_in_specs, r_outs, r_out_specs, r_scratch = _ride_parts(ride)

    def body(q_ref, kc_ref, kprev_ref, vc_ref, vprev_ref, ktc_ref, ktprev_ref, do_ref, o_ref, lse_ref, sk_ref,
             c_ref, s_ref, cp_ref, sp_ref, _, *rest):
        r_in, rest = rest[:len(r_ops)], rest[len(r_ops):]
        dqkv_ref, dsk_ref = rest[:2]
        r_out, rest = rest[2:2 + len(r_outs)], rest[2 + len(r_outs):]
        acc_k, acc_v, dq_scr = rest[:3]
        n = pl.program_id(0)
        _ride_run(ride, n == 0, n == nb, r_in, r_out, rest[3:])
        lane = _iota((CHUNK, 128), 1)
        lo = lane < 64
        lane1 = _iota((1, 128), 1)

        @pl.when(n == 0)
        def _():
            acc_k[...] = jnp.zeros_like(acc_k)
            acc_v[...] = jnp.zeros_like(acc_v)
            dsk_ref[...] = jnp.zeros((1, 128), F32)

        @pl.when(n > 0)
        def _():
            dqkv_ref[:, 0:1024] = dq_scr[...]
            for r in range(8):
                acc_k[r, 0:CHUNK] = acc_k[r, CHUNK:2 * CHUNK]
                acc_v[r, 0:CHUNK] = acc_v[r, CHUNK:2 * CHUNK]
                acc_k[r, CHUNK:2 * CHUNK] = jnp.zeros((CHUNK, 128), F32)
                acc_v[r, CHUNK:2 * CHUNK] = jnp.zeros((CHUNK, 128), F32)

        @pl.when(n < nb)
        def _():
            valid = _attn_valid(n)
            lse_all = lse_ref[0]
            dsk = jnp.zeros((1, 128), F32)
            for m in range(8):
                g = m // 2
                csl = slice(m * 128, (m + 1) * 128)
                qch = q_ref[:, csl]
                doch = do_ref[:, csl]
                prod_t = (doch.astype(F32) * o_ref[:, csl].astype(F32)).T
                sls = [slice((2 * g + e) * 128, (2 * g + e + 1) * 128) for e in range(2)]
                kk2 = jnp.concatenate([r[:, sl] for sl in sls for r in (kprev_ref, kc_ref)], axis=0)
                vv2 = jnp.concatenate([r[:, sl] for sl in sls for r in (vprev_ref, vc_ref)], axis=0)
                kk2_t = jnp.concatenate([r[sl, :] for sl in sls for r in (ktprev_ref, ktc_ref)], axis=1)
                s2 = lax.dot_general(kk2, qch, NT, preferred_element_type=F32)
                d_p2 = lax.dot_general(vv2, doch, NT, preferred_element_type=F32)
                ps, d_ss = [], []
                for e in range(2):
                    h = 2 * m + e
                    rows = slice(2 * CHUNK * e, 2 * CHUNK * (e + 1))
                    lse_h = lse_all[h:h + 1, :]
                    p = jnp.exp(jnp.where(valid, s2[rows], NEG) - lse_h)
                    delta = jnp.sum(prod_t[64 * e:64 * (e + 1)], axis=0, keepdims=True)
                    ps.append(p.astype(BF16))
                    d_ss.append((p * (d_p2[rows] - delta)).astype(BF16))
                    p_sink = jnp.exp(sk_ref[0:1, h:h + 1] - lse_h)
                    dsk = jnp.where(lane1 == h, -jnp.sum(p_sink * delta), dsk)
                d_s2, p2 = jnp.concatenate(d_ss, axis=0), jnp.concatenate(ps, axis=0)
                d_k2 = jnp.dot(d_s2, qch, preferred_element_type=F32)
                d_v2 = jnp.dot(p2, doch, preferred_element_type=F32)
                for e in range(2):
                    rows = slice(2 * CHUNK * e, 2 * CHUNK * (e + 1))
                    acc_k[2 * g + e] += d_k2[rows]
                    acc_v[2 * g + e] += d_v2[rows]
                dq_t = jnp.dot(kk2_t, d_s2, preferred_element_type=F32)
                dq_scr[:, csl] = (_rope(dq_t.T, c_ref[...], -s_ref[...]) * 0.125).astype(BF16)
            dsk_ref[...] += dsk

        @pl.when(n > 0)
        def _():
            for m2 in range(2):
                halves = []
                for g in (2 * m2, 2 * m2 + 1):
                    for acc in (acc_k, acc_v):
                        comb = jnp.where(lo, acc[2 * g, 0:CHUNK], acc[2 * g + 1, 0:CHUNK])
                        halves.append(comb + pltpu.roll(comb, 64, axis=1))
                d_kr = jnp.where(lo, halves[0], halves[2])
                d_v = jnp.where(lo, halves[1], halves[3])
                dqkv_ref[:, OFF_K + m2 * 128:OFF_K + (m2 + 1) * 128] = _rope(d_kr, cp_ref[...], -sp_ref[...]).astype(BF16)
                dqkv_ref[:, OFF_V + m2 * 128:OFF_V + (m2 + 1) * 128] = d_v.astype(BF16)

    qn = lambda n: jnp.minimum(n, nb - 1)
    pn = lambda n: jnp.maximum(jnp.minimum(n, nb) - 1, 0)
    cur = pl.BlockSpec((CHUNK, 1024), lambda n: (qn(n), 0))
    prev = pl.BlockSpec((CHUNK, 1024), lambda n: (pn(n), 0))
    cur128 = pl.BlockSpec((CHUNK, 128), lambda n: (qn(n), 0))
    prev128 = pl.BlockSpec((CHUNK, 128), lambda n: (pn(n), 0))
    cur_t = pl.BlockSpec((1024, CHUNK), lambda n: (0, qn(n)))
    prev_t = pl.BlockSpec((1024, CHUNK), lambda n: (0, pn(n)))
    one = pl.BlockSpec((1, 128), lambda n: (0, 0))
    return pl.pallas_call(
        body, name="attn_bwd", grid=(nb + 1,),
        in_specs=[cur, cur, prev, cur, prev, cur_t, prev_t, cur, cur, pl.BlockSpec((1, 16, CHUNK), lambda n: (qn(n), 0, 0)),
                  one, cur128, cur128, prev128, prev128, pl.BlockSpec(memory_space=pl.ANY)] + r_in_specs,
        out_specs=[pl.BlockSpec((CHUNK, 1536), lambda n: (pn(n), 0)), one] + r_out_specs,
        out_shape=[SDS((t, PROJ_W), BF16), SDS((1, 128), F32)] + r_outs,
        scratch_shapes=[pltpu.VMEM((8, 2 * CHUNK, 128), F32), pltpu.VMEM((8, 2 * CHUNK, 128), F32),
                        pltpu.VMEM((CHUNK, 1024), BF16)] + r_scratch,
        input_output_aliases={15: 0},
        compiler_params=_cp(dimension_semantics=("arbitrary",)),
    )(qr, kp, kp, vp, vp, kt, kt, d_o, o, lse, sinks, cos_t, sin_t, cos_t, sin_t, d_proj, *r_ops)


def _adamw(name, w, g, m, v, *, tr):
    rows, cols = w.shape
    tr = min(tr, rows)
    assert rows % tr == 0

    def body(w_ref, g_ref, m_ref, v_ref, d_ref, nm_ref, nv_ref):
        gv = g_ref[...]
        nm = ADAM_B1 * m_ref[...] + (1.0 - ADAM_B1) * gv
        nv = ADAM_B2 * v_ref[...] + (1.0 - ADAM_B2) * (gv * gv)
        m_hat = nm / (1.0 - ADAM_B1 ** ADAM_STEP)
        v_hat = nv / (1.0 - ADAM_B2 ** ADAM_STEP)
        d_ref[...] = -ADAM_LR * (m_hat / (jnp.sqrt(v_hat) + ADAM_EPS) + ADAM_WD * w_ref[...])
        nm_ref[...] = nm
        nv_ref[...] = nv

    tile = pl.BlockSpec((tr, cols), lambda i: (i, 0))
    return pl.pallas_call(
        body, name=name, grid=(rows // tr,), in_specs=[tile] * 4, out_specs=[tile] * 3,
        out_shape=[SDS((rows, cols), F32)] * 3, compiler_params=_cp(dimension_semantics=("arbitrary",)),
    )(w, g, m, v)


def _local_step(x, cos_t, sin_t, tgt, wb, ps, late=None, rides=None):
    t = x.shape[0]
    tm = min(512, t)
    tmw = min(1024, t)
    ij = lambda i, j, k: (i, j)
    i0 = lambda i, j, k: (i, 0)
    c0 = lambda i, j, k: (0, 0)
    cj = lambda i, j, k: (0, j)
    rides = rides or (lambda group, grads: None)
    rode = {}

    tkt = min(2048, t)
    proj, u, *arrived = _norm_mm("in_proj", x, ps['norm_mix_pre_w'], wb['cat'], tm=tmw, ride=late[0] if late else None)
    if late:
        more_wb, more_ps = late[1](arrived)
        wb, ps = {**wb, **more_wb}, {**ps, **more_ps}
    xc, xc_pre = _conv_silu_fwd(proj, ps['ssd_conv_w'], ps['ssd_conv_b'], tm=tm)
    bias_pad = jnp.pad(ps['ssd_dt_bias'], ((0, 0), (0, 96)))
    dtg, dtg_t = _dt_fwd(proj, bias_pad, tm=tmw)
    alog = jnp.pad(ps['ssd_a_log'].reshape(SSD_N_GROUPS, 1, 8), ((0, 0), (0, 0), (0, 120)))
    alog_t = ps['ssd_a_log'].reshape(SSD_N_GROUPS, 8, 1)
    d_exp = jnp.repeat(ps['ssd_d'], SSD_HEAD_DIM, axis=1)
    y, hs = _ssd_fwd(xc, dtg, dtg_t, alog, alog_t, d_exp)
    gn = _gated_norm_fwd(y, proj, ps['ssd_norm_w'], tm=tmw)
    qr, kp, vp, kt, vt = _rope_qkv(proj, cos_t, sin_t, tm=tm)
    sinks = jnp.pad(ps['attn_sinks'], ((0, 0), (0, 112)))
    ao, lse = _attn_fwd(qr, kp, vt, sinks)
    y_attn = _mm_plain("attn_out", ao, wb['ao'], tm=tmw, tn=1024, tk=1024)

    def merge_ep(acc, i, j, ins, outs):
        gs, ga, ya = ins
        outs[0][...] = (_sigmoid(gs[...]) * acc + _sigmoid(ga[...]) * ya[...]).astype(BF16)
        outs[1][...] = acc

    merged, y_ssd = _mm_call(
        "ssd_out_merge", gn, wb['so'], tm=tm, tn=D_MODEL, tk=2048, epilogue=merge_ep,
        extra_in=[(proj, (tm, D_MODEL), lambda i, j, k: (i, OFF_GS // D_MODEL)),
                  (proj, (tm, D_MODEL), lambda i, j, k: (i, OFF_GA // D_MODEL)), (y_attn, (tm, D_MODEL), i0)],
        outs=[((t, D_MODEL), BF16, (tm, D_MODEL), i0), ((t, D_MODEL), F32, (tm, D_MODEL), i0)])

    def mix_ep(acc, i, j, ins, outs):
        xv, wn = ins
        r = lax.rsqrt(jnp.mean(acc * acc, axis=-1, keepdims=True) + NORM_EPS)
        outs[0][...] = xv[...] + acc * r * wn[...]
        outs[1][...] = acc

    x1, mmix = _mm_call(
        "mix_out", merged, wb['mix'], tm=tm, tn=D_MODEL, tk=1024, epilogue=mix_ep,
        extra_in=[(x, (tm, D_MODEL), i0), (ps['norm_mix_post_w'], (1, D_MODEL), c0)],
        outs=[((t, D_MODEL), F32, (tm, D_MODEL), i0), ((t, D_MODEL), F32, (tm, D_MODEL), i0)])

    up_raw, h = _norm_mm("ffn_up", x1, ps['norm_ffn_pre_w'], wb['up'], tm=tmw)
    act, ffn_gate, ffn_val = _ffn_act_fwd(up_raw, ps['ffn_conv_w'], ps['ffn_conv_b'], tm=tm)

    def loss_ep(acc, i, j, ins, outs):
        x1v, tg, wn = ins
        d_ff_ref, dout_ref, loss_ref, dw_ref = outs
        wv = wn[...]
        r = lax.rsqrt(jnp.mean(acc * acc, axis=-1, keepdims=True) + NORM_EPS)
        err = x1v[...] + acc * r * wv - tg[...]
        dout = err * (1.0 / D_MODEL)
        dout_ref[...] = dout
        d_ff, dw = _rms_bwd(acc, wv, dout)
        d_ff_ref[...] = d_ff.astype(BF16)
        _accumulate(dw_ref, i == 0, dw)
        _accumulate(loss_ref, i == 0, jnp.sum(err * err, keepdims=True) * (0.5 / D_MODEL))

    d_ff, dout, loss, g_norm_ffn_post = _mm_call(
        "ffn_down_loss", act, wb['dn'], tm=tm, tn=D_MODEL, tk=FFN_D_FF, epilogue=loss_ep,
        extra_in=[(x1, (tm, D_MODEL), i0), (tgt, (tm, D_MODEL), i0), (ps['norm_ffn_post_w'], (1, D_MODEL), c0)],
        outs=[((t, D_MODEL), BF16, (tm, D_MODEL), i0), ((t, D_MODEL), F32, (tm, D_MODEL), i0),
              ((1, 1), F32, (1, 1), c0), ((1, D_MODEL), F32, (1, D_MODEL), c0)])

    d_act = _mm_plain("d_act", d_ff, wb['dn_t'], tm=tmw, tn=1408, tk=1024, out_dtype=BF16)
    g_w_down = _mm_plain("g_w_down", act, d_ff, tm=1408, tn=1024, tk=tkt, trans_a=True, out_dtype=BF16)
    d_gate, d_val, db_g, db_v = _ffn_act_bwd(ffn_gate, ffn_val, d_act, tm=tm)
    d_up_raw, gcw_g = _conv_bwd2("ffn_conv_bwd2_gate", d_gate, up_raw, 0, ps['ffn_conv_w'][:, :FFN_D_FF], tm=tm,
                                 tc=1408, out_cols=2 * FFN_D_FF, out_col0=0)
    d_up_raw, gcw_v = _conv_bwd2("ffn_conv_bwd2_val", d_val, up_raw, FFN_D_FF, ps['ffn_conv_w'][:, FFN_D_FF:], tm=tm,
                                 tc=1408, out_cols=2 * FFN_D_FF, out_col0=FFN_D_FF, fill=d_up_raw)
    g_ffn_conv_w = jnp.concatenate([gcw_g, gcw_v], axis=1)

    def dx1_ep(acc, i, j, ins, outs):
        x1v, wpre, dout_v, mmv, wpost = ins
        d_x1_ref, d_mm_ref, dwpre_ref, dwpost_ref = outs
        d_n, dw_pre = _rms_bwd(x1v[...], wpre[...], acc)
        d_x1 = dout_v[...] + d_n
        d_x1_ref[...] = d_x1
        d_mm, dw_post = _rms_bwd(mmv[...], wpost[...], d_x1)
        d_mm_ref[...] = d_mm.astype(BF16)
        _accumulate(dwpre_ref, i == 0, dw_pre)
        _accumulate(dwpost_ref, i == 0, dw_post)

    d_x1, d_mm, g_norm_ffn_pre, g_norm_mix_post = _mm_call(
        "d_h", d_up_raw, wb['up_t'], tm=tm, tn=D_MODEL, tk=2 * FFN_D_FF, epilogue=dx1_ep, vmem_mb=VMEM_BIG_MB,
        extra_in=[(x1, (tm, D_MODEL), i0), (ps['norm_ffn_pre_w'], (1, D_MODEL), c0), (dout, (tm, D_MODEL), i0),
                  (mmix, (tm, D_MODEL), i0), (ps['norm_mix_post_w'], (1, D_MODEL), c0)],
        outs=[((t, D_MODEL), F32, (tm, D_MODEL), i0), ((t, D_MODEL), BF16, (tm, D_MODEL), i0),
              ((1, D_MODEL), F32, (1, D_MODEL), c0), ((1, D_MODEL), F32, (1, D_MODEL), c0)])
    g_w_up_t = _mm_plain("g_w_up", d_up_raw, h, tm=1408, tn=1024, tk=tkt, trans_a=True, out_dtype=BF16)
    ride_ffn = rides('ffn', {'ffn_w_up': g_w_up_t, 'ffn_w_down': g_w_down})

    def dmerge_ep(acc, i, j, ins, outs):
        gs, ga, ys, ya = ins
        sg_s, sg_a = _sigmoid(gs[...]), _sigmoid(ga[...])
        outs[0][...] = (acc * sg_s).astype(BF16)
        outs[1][...] = (acc * sg_a).astype(BF16)
        outs[2][:, 0:D_MODEL] = (acc * ys[...] * sg_s * (1.0 - sg_s)).astype(BF16)
        outs[2][:, D_MODEL:2 * D_MODEL] = (acc * ya[...] * sg_a * (1.0 - sg_a)).astype(BF16)

    d_yssd, d_yattn, d_proj = _mm_call(
        "d_merged", d_mm, wb['mix_t'], tm=tm, tn=D_MODEL, tk=1024, epilogue=dmerge_ep,
        extra_in=[(proj, (tm, D_MODEL), lambda i, j, k: (i, OFF_GS // D_MODEL)),
                  (proj, (tm, D_MODEL), lambda i, j, k: (i, OFF_GA // D_MODEL)), (y_ssd, (tm, D_MODEL), i0), (y_attn, (tm, D_MODEL), i0)],
        outs=[((t, D_MODEL), BF16, (tm, D_MODEL), i0), ((t, D_MODEL), BF16, (tm, D_MODEL), i0),
              ((t, PROJ_W), BF16, (tm, 2 * D_MODEL), lambda i, j, k: (i, OFF_GS // (2 * D_MODEL)))])
    g_w_mix = _mm_plain("g_w_mix", merged, d_mm, tm=1024, tn=1024, tk=tkt, trans_a=True, out_dtype=BF16)

    def dgn_ep(acc, i, j, ins, outs):
        yv, zv, wn = ins
        d_y_ref, d_z_ref, dw_ref = outs
        zz = zv[...]
        sz = _sigmoid(zz)
        silu = zz * sz
        gv = yv[...] * silu
        r = lax.rsqrt(jnp.mean(gv * gv, axis=-1, keepdims=True) + NORM_EPS)
        gh = gv * r
        dgh = acc * wn[...]
        dg = r * (dgh - gh * jnp.mean(dgh * gh, axis=-1, keepdims=True))
        d_y_ref[...] = dg * silu
        d_z_ref[...] = (dg * yv[...] * (sz * (1.0 + zz * (1.0 - sz)))).astype(BF16)
        dw = jnp.sum(acc * gh, axis=0, keepdims=True)

        @pl.when(i == 0)
        def _():
            dw_ref[j] = dw

        @pl.when(i > 0)
        def _():
            dw_ref[j] += dw

    d_y, d_proj, g_ssd_norm = _mm_call(
        "d_gn", d_yssd, wb['so_t'], tm=tmw, tn=512, tk=1024, epilogue=dgn_ep, fill=(d_proj, 1),
        extra_in=[(y, (tmw, 512), ij), (proj, (tmw, 512), lambda i, j, k: (i, OFF_Z // 512 + j)), (ps['ssd_norm_w'], (1, 512), cj)],
        outs=[((t, SSD_D_INNER), F32, (tmw, 512), ij), ((t, PROJ_W), BF16, (tmw, 512), lambda i, j, k: (i, OFF_Z // 512 + j)),
              ((SSD_N_GROUPS, 1, 512), F32, (SSD_N_GROUPS, 1, 512), lambda i, j, k: (0, 0, 0))])
    g_ssd_norm = g_ssd_norm.reshape(1, SSD_D_INNER)
    g_w_so = _mm_plain("g_w_so", gn, d_yssd, tm=1024, tn=1024, tk=tkt, trans_a=True, out_dtype=BF16)
    d_xc, d_dtg, d_alog, d_dd, *rode['ffn'] = _ssd_bwd(xc, dtg, dtg_t, alog, alog_t, d_exp, d_y, hs, ride=ride_ffn)
    d_pre, g_ssd_conv_b = _conv_silu_bwd1(d_xc, xc_pre, tm=tm)
    d_proj, g_ssd_conv_w = _conv_bwd2("ssd_conv_bwd2", d_pre, proj, OFF_XBC, ps['ssd_conv_w'], tm=tm, tc=1536,
                                      out_cols=PROJ_W, out_col0=OFF_XBC, fill=d_proj)
    d_proj, g_dt_bias = _dt_bwd(d_dtg, proj, bias_pad, d_proj, tm=tmw)

    d_ao = _mm_plain("d_ao", d_yattn, wb['ao_t'], tm=tmw, tn=1024, tk=1024, out_dtype=BF16)
    g_w_ao = _mm_plain("g_w_ao", ao, d_yattn, tm=1024, tn=1024, tk=tkt, trans_a=True, out_dtype=BF16)
    ride_mix = rides('mix', {'ssd_w_out': g_w_so, 'attn_w_out': g_w_ao, 'w_mix_out': g_w_mix})
    d_proj, g_sinks, *rode['mix'] = _attn_bwd(qr, kp, vp, kt, d_ao, ao, lse, sinks, cos_t, sin_t, d_proj, ride=ride_mix)

    def dx_ep(acc, i, j, ins, outs):
        xv, wn, dx1v = ins
        d_n, dw = _rms_bwd(xv[...], wn[...], acc)
        outs[0][...] = dx1v[...] + d_n
        _accumulate(outs[1], i == 0, dw)

    g_cat_t = _mm_plain("g_w_in", d_proj, u, tm=1024, tn=1024, tk=tkt, trans_a=True, out_dtype=BF16)
    grad_x, g_norm_mix_pre, *rode['w_in'] = _mm_call(
        "d_u", d_proj, wb['cat_t'], tm=tm, tn=D_MODEL, tk=PROJ_W, epilogue=dx_ep, ride=rides('w_in', {'w_in': g_cat_t}),
        vmem_mb=VMEM_BIG_MB,
        extra_in=[(x, (tm, D_MODEL), i0), (ps['norm_mix_pre_w'], (1, D_MODEL), c0), (d_x1, (tm, D_MODEL), i0)],
        outs=[((t, D_MODEL), F32, (tm, D_MODEL), i0), ((1, D_MODEL), F32, (1, D_MODEL), c0)])

    grads = {
        'norm_mix_pre_w': g_norm_mix_pre, 'w_in': g_cat_t, 'ssd_conv_w': g_ssd_conv_w, 'ssd_conv_b': g_ssd_conv_b,
        'ssd_dt_bias': g_dt_bias[:, :SSD_N_HEADS], 'ssd_a_log': d_alog[:, 0, :8].reshape(1, SSD_N_HEADS),
        'ssd_d': d_dd[:, 0, :8].reshape(1, SSD_N_HEADS), 'ssd_norm_w': g_ssd_norm, 'ssd_w_out': g_w_so,
        'attn_sinks': g_sinks[:, :ATTN_N_HEADS], 'attn_w_out': g_w_ao, 'w_mix_out': g_w_mix,
        'norm_mix_post_w': g_norm_mix_post, 'norm_ffn_pre_w': g_norm_ffn_pre, 'ffn_w_up': g_w_up_t,
        'ffn_conv_w': g_ffn_conv_w, 'ffn_conv_b': jnp.concatenate([db_g, db_v], axis=1), 'ffn_w_down': g_w_down,
        'norm_ffn_post_w': g_norm_ffn_post,
    }
    return loss, grad_x, grads, rode


def _group_channels(a):
    parts = []
    for g in range(SSD_N_GROUPS):
        parts += [a[..., 512 * g:512 * (g + 1)], a[..., 2048 + 128 * g:2048 + 128 * (g + 1)],
                  a[..., 2560 + 128 * g:2560 + 128 * (g + 1)]]
    return jnp.concatenate(parts, axis=-1)


def _ungroup_channels(a):
    xs = [a[..., GROUP_W * g:GROUP_W * g + 512] for g in range(SSD_N_GROUPS)]
    bs = [a[..., GROUP_W * g + 512:GROUP_W * g + 640] for g in range(SSD_N_GROUPS)]
    cs = [a[..., GROUP_W * g + 640:GROUP_W * (g + 1)] for g in range(SSD_N_GROUPS)]
    return jnp.concatenate(xs + bs + cs, axis=-1)


def _proj_rows(a_t, lo, hi):
    out = []
    for start, length, dst in sorted(PROJ_SEGS):
        s, e = max(lo, start), min(hi, start + length)
        if s < e:
            out.append(a_t[dst + s - start:dst + e - start])
    return out


def _to_proj_layout(w_in_t):
    pieces, pos = [], 0
    for start, length, dst in sorted(PROJ_SEGS, key=lambda s: s[2]):
        if dst > pos:
            pieces.append(jnp.zeros((dst - pos, w_in_t.shape[1]), w_in_t.dtype))
        pieces.append(w_in_t[start:start + length])
        pos = dst + length
    if pos < PROJ_W:
        pieces.append(jnp.zeros((PROJ_W - pos, w_in_t.shape[1]), w_in_t.dtype))
    return jnp.concatenate(pieces, axis=0)


def _rope_tables(positions):
    half = 32
    inv_freq = ROPE_THETA ** (-jnp.arange(half, dtype=F32) * 2.0 / 64)
    ang = positions.astype(F32)[:, None] * inv_freq
    cos, sin = jnp.cos(ang), jnp.sin(ang)
    return jnp.concatenate([cos, cos, cos, cos], axis=1), jnp.concatenate([-sin, sin, -sin, sin], axis=1)


def _matmul_weights(w_in_t):
    cat_t = _to_proj_layout(w_in_t)
    return {'cat': _column_tiles(cat_t, 1024), 'cat_t': cat_t}


def _late_weights(so, ao, mix, up_t, dn):
    return {'so': so, 'so_t': so.T, 'ao': ao, 'ao_t': ao.T, 'mix': mix, 'mix_t': mix.T,
            'up': _column_tiles(up_t, 1408), 'up_t': up_t, 'dn': dn, 'dn_t': dn.T}


def _column_tiles(w_t, tn):
    n, dm = w_t.shape
    return w_t.reshape(n // tn, tn, dm).transpose(0, 2, 1)


ANY = pl.BlockSpec(memory_space=pl.ANY)
MESH = pl.DeviceIdType.MESH
ROW_ALIGN = 32


def _mesh_pos():
    return lax.axis_index("x"), lax.axis_index("y"), lax.axis_index("c")


def _other_chips(x, y):
    return [(1 - x, y), (x, 1 - y), (1 - x, 1 - y)]


def _remote(src, dst, send_sems, recv_sems, k, to):
    return pltpu.make_async_remote_copy(src_ref=src, dst_ref=dst, send_sem=send_sems.at[k], recv_sem=recv_sems.at[k],
                                        device_id=to, device_id_type=MESH)


def _half(c, rh):
    return pl.ds(pl.multiple_of(c * rh, 16), rh)


def _ag_ride(shard):
    r = shard.shape[0]
    rh = r // 2

    def first_copies(w_ref, out_ref, send_sems, recv_sems):
        x, y, c = _mesh_pos()
        p = 2 * x + y
        mine = _half(c, rh)
        cps = [_remote(w_ref, out_ref.at[p], send_sems, recv_sems, 6, (x, y, 1 - c))]
        return cps + [_remote(w_ref.at[mine], out_ref.at[p, mine], send_sems, recv_sems, j, (cx, cy, c))
                      for j, (cx, cy) in enumerate(_other_chips(x, y))]

    def start(ins, outs, send_sems, recv_sems):
        for cp in first_copies(ins[0], outs[0], send_sems, recv_sems):
            cp.start()

    def forwards(out_ref, send_sems, recv_sems, half):
        x, y, c = _mesh_pos()
        return [_remote(out_ref.at[2 * cx + cy, half], out_ref.at[2 * cx + cy, half], send_sems, recv_sems, 3 + j, (x, y, 1 - c))
                for j, (cx, cy) in enumerate(_other_chips(x, y))]

    def middle(ins, outs, send_sems, recv_sems):
        x, y, c = _mesh_pos()
        mine = _half(c, rh)
        for j, (fwd, (cx, cy)) in enumerate(zip(forwards(outs[0], send_sems, recv_sems, mine), _other_chips(x, y))):
            slab = outs[0].at[2 * cx + cy, mine]
            _remote(slab, slab, send_sems, recv_sems, j, (x, y, 1 - c)).wait_recv()
            fwd.start()

    def finish(ins, outs, send_sems, recv_sems):
        w_ref, out_ref = ins[0], outs[0]
        x, y, c = _mesh_pos()
        for cp in forwards(out_ref, send_sems, recv_sems, _half(1 - c, rh)):
            cp.wait_recv()
        _remote(w_ref, out_ref.at[2 * x + y], send_sems, recv_sems, 6, (x, y, 1 - c)).wait_recv()
        for cp in first_copies(w_ref, out_ref, send_sems, recv_sems) + forwards(out_ref, send_sems, recv_sems, _half(c, rh)):
            cp.wait_send()

    return _Ride((shard,), (SDS((N_CHIPS, r, COMM_LANES), shard.dtype),), 7, start, finish, middle)


def _rs_ride(gbuf):
    rh = gbuf.shape[1] // 2

    def copies(g_ref, r_ref, send_sems, recv_sems, landing):
        x, y, c = _mesh_pos()
        cps = []
        for k, (cx, cy) in enumerate(_other_chips(x, y)):
            for h in range(2):
                slot = 2 * k + c if landing else 2 * k + h
                cps.append(pltpu.make_async_remote_copy(
                    src_ref=g_ref.at[2 * cx + cy, pl.ds(h * rh, rh)], dst_ref=r_ref.at[slot],
                    send_sem=send_sems.at[2 * k + h], recv_sem=recv_sems.at[slot],
                    device_id=(cx, cy, h), device_id_type=MESH))
        cps.append(_remote(g_ref.at[2 * x + y, _half(1 - c, rh)], r_ref.at[6], send_sems, recv_sems, 6, (x, y, 1 - c)))
        return cps

    def start(ins, outs, send_sems, recv_sems):
        for cp in copies(ins[0], outs[0], send_sems, recv_sems, True):
            cp.start()

    def finish(ins, outs, send_sems, recv_sems):
        for cp in copies(ins[0], outs[0], send_sems, recv_sems, False):
            cp.wait()

    return _Ride((gbuf,), (SDS((7, rh, COMM_LANES), gbuf.dtype),), 7, start, finish)


def _rs_sum(name, gbuf, got, pc_idx):
    rh = got.shape[1]
    tr = max(d for d in range(16, 513, 16) if rh % d == 0)
    nb = rh // tr

    def body(pc_ref, own_ref, *refs):
        o_ref = refs[7]
        p, c = pc_ref[0], pc_ref[1]
        own = own_ref[0].astype(F32)
        slots = [r[0].astype(F32) for r in refs[:7]]

        def term(q, h):
            code = p ^ q
            far = jnp.where(code == 2, slots[h], jnp.where(code == 1, slots[2 + h], slots[4 + h]))
            return jnp.where(code == 0, jnp.where(c == h, own, slots[6]), far)

        acc = term(0, 0)
        for q, h in [(0, 1), (1, 0), (1, 1), (2, 0), (2, 1), (3, 0), (3, 1)]:
            acc = acc + term(q, h)
        o_ref[0] = acc

    slot = lambda s: pl.BlockSpec((1, tr, COMM_LANES), lambda i, pc: (s, i, 0))
    return pl.pallas_call(
        body, name=name,
        grid_spec=pltpu.PrefetchScalarGridSpec(
            num_scalar_prefetch=1, grid=(nb,),
            in_specs=[pl.BlockSpec((1, tr, COMM_LANES), lambda i, pc: (pc[0], pc[1] * nb + i, 0))] + [slot(s) for s in range(7)],
            out_specs=pl.BlockSpec((1, tr, COMM_LANES), lambda i, pc: (pc[1], i, 0))),
        out_shape=SDS((2, rh, COMM_LANES), F32), compiler_params=_cp(dimension_semantics=("arbitrary",)),
    )(pc_idx, gbuf, *([got] * 7))


def _pair_gather_all(bufs):
    n = len(bufs)

    def body(*refs):
        outs, send_sems, recv_sems = refs[n:2 * n], refs[2 * n], refs[2 * n + 1]
        x, y, c = _mesh_pos()
        cps = [_remote(o.at[c], o.at[c], send_sems, recv_sems, k, (x, y, 1 - c)) for k, o in enumerate(outs)]
        for cp in cps:
            cp.start()
        for k, o in enumerate(outs):
            _remote(o.at[1 - c], o.at[1 - c], send_sems, recv_sems, k, (x, y, 1 - c)).wait_recv()
        for cp in cps:
            cp.wait_send()

    return pl.pallas_call(
        body, name="grad_pair_gather", in_specs=[ANY] * n, out_specs=[ANY] * n,
        out_shape=[SDS(b.shape, b.dtype) for b in bufs],
        scratch_shapes=[pltpu.SemaphoreType.DMA((n,)), pltpu.SemaphoreType.DMA((n,))],
        input_output_aliases={k: k for k in range(n)},
    )(*bufs)


def _pack_rows(big, small=()):
    parts = list(big)
    if small:
        flat = jnp.concatenate([p.reshape(-1) for p in small])
        k = -(-flat.shape[0] // (16 * COMM_LANES)) * 16
        parts.append(jnp.pad(flat, (0, k * COMM_LANES - flat.shape[0])).reshape(k, COMM_LANES))
    pad = -sum(p.shape[0] for p in parts) % ROW_ALIGN
    if pad:
        parts.append(jnp.zeros((pad, COMM_LANES), parts[0].dtype))
    return jnp.concatenate(parts, axis=0) if len(parts) > 1 else parts[0]


def _take(flat, off, shape):
    n = 1
    for d in shape:
        n *= d
    return flat[off:off + n].reshape(shape), off + n


BIG_ROWS = {'w_in': 2184, 'ssd_w_out': 512, 'attn_w_out': 256, 'w_mix_out': 256, 'ffn_w_up': 1408, 'ffn_w_down': 704}
TRANSPOSED = ('w_in', 'ffn_w_up')
LATE = ('ssd_w_out', 'attn_w_out', 'w_mix_out', 'ffn_w_up', 'ffn_w_down')
CONV_TAPS = ('ssd_conv_w', 'ffn_conv_w')
RS_GROUPS = {'ffn': ('ffn_w_up', 'ffn_w_down'), 'mix': ('ssd_w_out', 'attn_w_out', 'w_mix_out'), 'w_in': ('w_in',)}


def _exchange(name, ride):
    n_in, n_out = len(ride.ins), len(ride.outs)

    def body(*refs):
        ins, outs, sems = refs[:n_in], refs[n_in:n_in + n_out], refs[n_in + n_out:]
        ride.start(ins, outs, *sems)
        if ride.middle is not None:
            ride.middle(ins, outs, *sems)
        ride.finish(ins, outs, *sems)

    return pl.pallas_call(
        body, name=name, in_specs=[ANY] * n_in, out_specs=[ANY] * n_out, out_shape=list(ride.outs),
        scratch_shapes=[pltpu.SemaphoreType.DMA((ride.n_sems,)), pltpu.SemaphoreType.DMA((ride.n_sems,))],
    )(*ride.ins)


def kernel(x, positions, norm_mix_pre_w, w_in, ssd_conv_w, ssd_conv_b, ssd_dt_bias, ssd_a_log, ssd_d, ssd_norm_w, ssd_w_out, attn_sinks, attn_w_out, w_mix_out, norm_mix_post_w, norm_ffn_pre_w, ffn_w_up, ffn_conv_w, ffn_conv_b, ffn_w_down, norm_ffn_post_w, loss_target, m_norm_mix_pre_w, m_w_in, m_ssd_conv_w, m_ssd_conv_b, m_ssd_dt_bias, m_ssd_a_log, m_ssd_d, m_ssd_norm_w, m_ssd_w_out, m_attn_sinks, m_attn_w_out, m_w_mix_out, m_norm_mix_post_w, m_norm_ffn_pre_w, m_ffn_w_up, m_ffn_conv_w, m_ffn_conv_b, m_ffn_w_down, m_norm_ffn_post_w, v_norm_mix_pre_w, v_w_in, v_ssd_conv_w, v_ssd_conv_b, v_ssd_dt_bias, v_ssd_a_log, v_ssd_d, v_ssd_norm_w, v_ssd_w_out, v_attn_sinks, v_attn_w_out, v_w_mix_out, v_norm_mix_post_w, v_norm_ffn_pre_w, v_ffn_w_up, v_ffn_conv_w, v_ffn_conv_b, v_ffn_w_down, v_norm_ffn_post_w):
    given = dict(locals())
    w = {n: given[n][0] for n in WEIGHTS}
    w = {n: (a if a.ndim == 2 else a[None]) for n, a in w.items()}
    mom_m = {n: given['m_' + n].reshape(w[n].shape) for n in WEIGHTS}
    mom_v = {n: given['v_' + n].reshape(w[n].shape) for n in WEIGHTS}
    cx, cy, cc = _mesh_pos()
    pc_idx = jnp.stack([2 * cx + cy, cc]).astype(jnp.int32)

    rows_of = lambda n: (w[n].T if n in TRANSPOSED else w[n]).astype(BF16)
    gathered = _exchange("w_in_all_gather", _ag_ride(_pack_rows([rows_of('w_in')])))[0]
    wb = _matmul_weights(jnp.concatenate([gathered[s, :BIG_ROWS['w_in']] for s in range(N_CHIPS)], axis=0))
    taps = [lax.bitcast_convert_type(w[n], BF16) for n in CONV_TAPS]

    def unpack_late(arrived):
        rows, conv = {n: [] for n in LATE}, {n: [] for n in CONV_TAPS}
        for s in range(N_CHIPS):
            r0 = 0
            for n in LATE:
                rows[n].append(arrived[0][s, r0:r0 + BIG_ROWS[n]])
                r0 += BIG_ROWS[n]
            flat, off = arrived[0][s, r0:r0 + 16].reshape(-1), 0
            for n in CONV_TAPS:
                a, off = _take(flat, off, w[n].shape + (2,))
                conv[n].append(lax.bitcast_convert_type(a, F32))
        full = {n: jnp.concatenate(rows[n], axis=0) for n in LATE}
        return (_late_weights(*[full[n] for n in LATE]),
                {'ssd_conv_w': _group_channels(jnp.concatenate(conv['ssd_conv_w'], axis=1)),
                 'ffn_conv_w': jnp.concatenate(conv['ffn_conv_w'], axis=1)})

    late = (_ag_ride(_pack_rows([rows_of(n) for n in LATE], taps)), unpack_late)

    sent = {}

    def rides(group, g):
        parts = []
        for s in range(N_CHIPS):
            slab = []
            for n in RS_GROUPS[group]:
                lo, hi = BIG_ROWS[n] * s, BIG_ROWS[n] * (s + 1)
                slab += _proj_rows(g[n], lo, hi) if n == 'w_in' else [g[n][lo:hi]]
            slab = [a.astype(BF16) for a in slab]
            pad = -sum(a.shape[0] for a in slab) % ROW_ALIGN
            parts += slab + ([jnp.zeros((pad, COMM_LANES), BF16)] if pad else [])
        sent[group] = jnp.concatenate(parts, axis=0).reshape(N_CHIPS, -1, COMM_LANES)
        return _rs_ride(sent[group])

    ps = {n: w[n] for n in REPLICATED}
    ps['ssd_conv_b'] = _group_channels(w['ssd_conv_b'])
    cos_t, sin_t = _rope_tables(positions[0])
    loss, grad_x, grads, rode = _local_step(x[0], cos_t, sin_t, loss_target[0], wb, ps, late, rides)
    grads['ssd_conv_w'] = _ungroup_channels(grads['ssd_conv_w'])
    grads['ssd_conv_b'] = _ungroup_channels(grads['ssd_conv_b'])

    shard_cols = {n: sh[1] for n, _, sh in SHARDED}
    parts = []
    for s in range(N_CHIPS):
        small = [grads[n][:, shard_cols[n] * s:shard_cols[n] * (s + 1)] for n in CONV_TAPS] + [grads[n] for n in REPLICATED]
        flat = _pack_rows([], small)
        high = flat.astype(BF16)
        parts += [high, (flat - high.astype(F32)).astype(BF16)]
    sent['small'] = jnp.concatenate(parts, axis=0).reshape(N_CHIPS, -1, COMM_LANES)
    rode['small'] = _exchange("grad_small_exchange", _rs_ride(sent['small']))

    groups = ('ffn', 'mix', 'w_in', 'small')
    red = _pair_gather_all([_rs_sum("grad_sum_" + g, sent[g], rode[g][0], pc_idx) for g in groups])
    red = {g: r.reshape(-1, COMM_LANES) for g, r in zip(groups, red)}
    g_red = {}
    for g in groups[:3]:
        r0 = 0
        for n in RS_GROUPS[g]:
            g_red[n] = red[g][r0:r0 + BIG_ROWS[n]].T if n in TRANSPOSED else red[g][r0:r0 + BIG_ROWS[n]]
            r0 += BIG_ROWS[n]
    half = red['small'].shape[0] // 2
    flat, off = (red['small'][:half] + red['small'][half:]).reshape(-1), 0
    for n in CONV_TAPS + REPLICATED:
        g_red[n], off = _take(flat, off, w[n].shape)

    small_names = [n for n in WEIGHTS if n not in MATMUL_WEIGHTS]
    delta, new_m, new_v = {}, {}, {}
    for n in MATMUL_WEIGHTS:
        delta[n], new_m[n], new_v[n] = _adamw("adamw_" + n, w[n], g_red[n], mom_m[n], mom_v[n],
                                                  tr=max(d for d in range(8, 353, 8) if w[n].shape[0] % d == 0))
    packed = [_pack_small([d[n] for n in small_names]) for d in (w, g_red, mom_m, mom_v)]
    outs = _adamw("adamw_small", *packed, tr=packed[0].shape[0])
    for res, o in zip((delta, new_m, new_v), outs):
        fl, off = o.reshape(-1), 0
        for n in small_names:
            res[n], off = _take(fl, off, w[n].shape)

    loss_all = lax.psum(loss[0, 0], ("x", "y", "c"))
    shaped = lambda d: [d[n].reshape(given[n].shape) for n in WEIGHTS]
    return (loss_all, grad_x[None], *shaped(g_red), *shaped(delta), *shaped(new_m), *shaped(new_v))


def _pack_small(pieces):
    flat = jnp.concatenate([p.reshape(-1) for p in pieces])
    rows = -(-flat.shape[0] // (128 * 8)) * 8
    return jnp.pad(flat, (0, rows * 128 - flat.shape[0])).reshape(rows, 128)
```
